```python
import jax, jax.numpy as jnp
from jax import lax
import numpy as np

D_MODEL = 2048
BATCH = 8
SEQ = 4096
DEPTH = 2

HEAD_DIM = 128
N_SB_HEADS = D_MODEL // (2 * HEAD_DIM)
N_FOX_HEADS = D_MODEL // (2 * HEAD_DIM)
SB_WIDTH = N_SB_HEADS * HEAD_DIM
FOX_WIDTH = N_FOX_HEADS * HEAD_DIM
IN_WIDTH = 3 * SB_WIDTH + 3 * FOX_WIDTH + N_FOX_HEADS

N_MLA_HEADS = 16
Q_LORA_RANK = 512
KV_LORA_RANK = 512
QK_NOPE_DIM = 128
QK_ROPE_DIM = 64
QK_HEAD_DIM = QK_NOPE_DIM + QK_ROPE_DIM
V_HEAD_DIM = 128
DOWN_WIDTH = Q_LORA_RANK + KV_LORA_RANK + QK_ROPE_DIM

D_FF = 4 * D_MODEL
BLOCK_Q = 128
ROPE_THETA = 10000.0
EPS = 1e-6
FORGET_BIAS_CENTER = 3.0

N_EVEN = (DEPTH + 1) // 2
N_ODD = DEPTH // 2

kernel_name = "hybrid_stickbreak_fox_mla_sqrelu"


def rmsnorm(x, g):
    x32 = x.astype(jnp.float32)
    y = x32 * lax.rsqrt(jnp.mean(x32 * x32, axis=-1, keepdims=True) + EPS)
    return (y * g.astype(jnp.float32)).astype(x.dtype)


def to_heads(t, n_heads):
    b, s, _ = t.shape
    return t.reshape(b, s, n_heads, -1).transpose(0, 2, 1, 3)


def query_blocks(q):
    b, h, s, d = q.shape
    return q.reshape(b, h, s // BLOCK_Q, BLOCK_Q, d).transpose(2, 0, 1, 3, 4)


def merge_blocks(o):
    nb, b, h, bq, d = o.shape
    return o.transpose(1, 0, 3, 2, 4).reshape(b, nb * bq, h * d)


def stick_breaking_attention(q, k, v):
    s = q.shape[2]
    scale = q.shape[-1] ** -0.5
    key_pos = jnp.arange(s)

    def one_block(args):
        qb, q0 = args
        z = jnp.einsum('bhqd,bhkd->bhqk', qb, k,
                       preferred_element_type=jnp.float32) * scale
        q_pos = q0 + jnp.arange(BLOCK_Q)
        strict = key_pos[None, :] < q_pos[:, None]
        log_beta = jax.nn.log_sigmoid(z)
        log_one_minus = jnp.where(strict, jax.nn.log_sigmoid(-z), 0.0)
        suffix = lax.cumsum(log_one_minus, axis=3, reverse=True) - log_one_minus
        w = jnp.where(strict, jnp.exp(log_beta + suffix), 0.0)
        return jnp.einsum('bhqk,bhkd->bhqd', w.astype(v.dtype), v,
                          preferred_element_type=jnp.float32).astype(v.dtype)

    starts = jnp.arange(s // BLOCK_Q, dtype=jnp.int32) * BLOCK_Q
    return merge_blocks(lax.map(one_block, (query_blocks(q), starts)))


def causal_softmax_attention(q, k, v, log_decay_cum=None):
    s = q.shape[2]
    scale = q.shape[-1] ** -0.5
    key_pos = jnp.arange(s)

    def one_block(args):
        qb, q0 = args
        logits = jnp.einsum('bhqd,bhkd->bhqk', qb, k,
                            preferred_element_type=jnp.float32) * scale
        if log_decay_cum is not None:
            f_q = lax.dynamic_slice_in_dim(log_decay_cum, q0, BLOCK_Q, axis=2)
            logits = logits + f_q[..., :, None] - log_decay_cum[..., None, :]
        q_pos = q0 + jnp.arange(BLOCK_Q)
        causal = key_pos[None, :] <= q_pos[:, None]
        p = jax.nn.softmax(jnp.where(causal, logits, -jnp.inf), axis=-1)
        return jnp.einsum('bhqk,bhkd->bhqd', p.astype(v.dtype), v,
                          preferred_element_type=jnp.float32).astype(v.dtype)

    starts = jnp.arange(s // BLOCK_Q, dtype=jnp.int32) * BLOCK_Q
    return merge_blocks(lax.map(one_block, (query_blocks(q), starts)))


def apply_rope(t, positions):
    half = t.shape[-1] // 2
    inv_freq = ROPE_THETA ** (-jnp.arange(half, dtype=jnp.float32) / half)
    ang = positions.astype(jnp.float32)[:, None, :, None] * inv_freq
    cos, sin = jnp.cos(ang), jnp.sin(ang)
    t32 = t.astype(jnp.float32)
    t1, t2 = t32[..., :half], t32[..., half:]
    return jnp.concatenate([t1 * cos - t2 * sin, t2 * cos + t1 * sin], axis=-1).astype(t.dtype)


def sb_fox_mixer(h, w_in, b_f, fox_q_g, fox_k_g, w_o):
    proj = h @ w_in
    cuts = np.cumsum([SB_WIDTH] * 3 + [FOX_WIDTH] * 3)
    q_sb, k_sb, v_sb, q_fx, k_fx, v_fx, f_logit = jnp.split(proj, cuts, axis=-1)
    o_sb = stick_breaking_attention(to_heads(q_sb, N_SB_HEADS),
                                    to_heads(k_sb, N_SB_HEADS),
                                    to_heads(v_sb, N_SB_HEADS))
    log_f = jax.nn.log_sigmoid(f_logit.astype(jnp.float32) + b_f.astype(jnp.float32))
    log_f_cum = jnp.cumsum(log_f, axis=1).transpose(0, 2, 1)
    q_f = rmsnorm(to_heads(q_fx, N_FOX_HEADS), fox_q_g)
    k_f = rmsnorm(to_heads(k_fx, N_FOX_HEADS), fox_k_g)
    o_fx = causal_softmax_attention(q_f, k_f, to_heads(v_fx, N_FOX_HEADS), log_f_cum)
    return jnp.concatenate([o_sb, o_fx], axis=-1) @ w_o


def mla_mixer(h, positions, w_down, q_a_g, kv_a_g, w_uq, w_ukv, q_g, k_g, w_o):
    b, s, _ = h.shape
    down = h @ w_down
    c_q, c_kv, k_pe = jnp.split(down, [Q_LORA_RANK, Q_LORA_RANK + KV_LORA_RANK], axis=-1)
    c_q = rmsnorm(c_q, q_a_g)
    c_kv = rmsnorm(c_kv, kv_a_g)
    q = to_heads(c_q @ w_uq, N_MLA_HEADS)
    kv = to_heads(c_kv @ w_ukv, N_MLA_HEADS)
    k_nope, v = kv[..., :QK_NOPE_DIM], kv[..., QK_NOPE_DIM:]
    k_pe = jnp.broadcast_to(k_pe[:, None], (b, N_MLA_HEADS, s, QK_ROPE_DIM))
    k = jnp.concatenate([k_nope, k_pe], axis=-1)
    q = rmsnorm(q, q_g)
    k = rmsnorm(k, k_g)
    q = jnp.concatenate([q[..., :QK_NOPE_DIM], apply_rope(q[..., QK_NOPE_DIM:], positions)], axis=-1)
    k = jnp.concatenate([k[..., :QK_NOPE_DIM], apply_rope(k[..., QK_NOPE_DIM:], positions)], axis=-1)
    o = causal_softmax_attention(q, k, v)
    return o @ w_o


def squared_relu_mlp(h, w_up, w_down):
    a = jnp.square(jax.nn.relu(h @ w_up))
    return a @ w_down


def _fwd_setup_inputs(seed: int = 0) -> dict:
    key = jax.random.key(seed)
    ks = jax.random.split(key, 24)

    def dense(k, shape, fan_in):
        return jax.random.normal(k, shape, jnp.float32) * (fan_in ** -0.5)

    def gain(k, shape):
        return 1.0 + 0.02 * jax.random.normal(k, shape, jnp.float32)

    x = jax.random.normal(ks[0], (BATCH, SEQ, D_MODEL), jnp.float32)
    offsets = jax.random.randint(ks[1], (BATCH, 1), 0, 2048, dtype=jnp.int32)
    positions = offsets + jnp.arange(SEQ, dtype=jnp.int32)[None, :]
    return {
        "x": x,
        "positions": positions,
        "ln_mix_g": gain(ks[2], (DEPTH, D_MODEL)),
        "ln_mlp_g": gain(ks[3], (DEPTH, D_MODEL)),
        "sf_w_in": dense(ks[4], (N_EVEN, D_MODEL, IN_WIDTH), D_MODEL),
        "sf_b_f": FORGET_BIAS_CENTER + 0.1 * jax.random.normal(ks[5], (N_EVEN, N_FOX_HEADS), jnp.float32),
        "fox_q_g": gain(ks[6], (N_EVEN, HEAD_DIM)),
        "fox_k_g": gain(ks[7], (N_EVEN, HEAD_DIM)),
        "sf_w_o": dense(ks[8], (N_EVEN, SB_WIDTH + FOX_WIDTH, D_MODEL), SB_WIDTH + FOX_WIDTH),
        "mla_w_down": dense(ks[9], (N_ODD, D_MODEL, DOWN_WIDTH), D_MODEL),
        "mla_q_a_g": gain(ks[10], (N_ODD, Q_LORA_RANK)),
        "mla_kv_a_g": gain(ks[11], (N_ODD, KV_LORA_RANK)),
        "mla_w_uq": dense(ks[12], (N_ODD, Q_LORA_RANK, N_MLA_HEADS * QK_HEAD_DIM), Q_LORA_RANK),
        "mla_w_ukv": dense(ks[13], (N_ODD, KV_LORA_RANK, N_MLA_HEADS * (QK_NOPE_DIM + V_HEAD_DIM)), KV_LORA_RANK),
        "mla_q_g": gain(ks[14], (N_ODD, QK_HEAD_DIM)),
        "mla_k_g": gain(ks[15], (N_ODD, QK_HEAD_DIM)),
        "mla_w_o": dense(ks[16], (N_ODD, N_MLA_HEADS * V_HEAD_DIM, D_MODEL), N_MLA_HEADS * V_HEAD_DIM),
        "mlp_w_up": dense(ks[17], (DEPTH, D_MODEL, D_FF), D_MODEL),
        "mlp_w_down": dense(ks[18], (DEPTH, D_FF, D_MODEL), D_FF),
    }


def _fwd_reference(x, positions, ln_mix_g, ln_mlp_g, sf_w_in, sf_b_f, fox_q_g, fox_k_g, sf_w_o,
              mla_w_down, mla_q_a_g, mla_kv_a_g, mla_w_uq, mla_w_ukv, mla_q_g, mla_k_g,
              mla_w_o, mlp_w_up, mlp_w_down):
    for layer in range(DEPTH):
        i = layer // 2
        h = rmsnorm(x, ln_mix_g[layer])
        if layer % 2 == 0:
            x = x + sb_fox_mixer(h, sf_w_in[i], sf_b_f[i], fox_q_g[i], fox_k_g[i], sf_w_o[i])
        else:
            x = x + mla_mixer(h, positions, mla_w_down[i], mla_q_a_g[i], mla_kv_a_g[i],
                              mla_w_uq[i], mla_w_ukv[i], mla_q_g[i], mla_k_g[i], mla_w_o[i])
        h = rmsnorm(x, ln_mlp_g[layer])
        x = x + squared_relu_mlp(h, mlp_w_up[layer], mlp_w_down[layer])
    return x


import jax as _jax
import jax.numpy as _jnp

TWIN_FORMAT = 'train_step'
FWD_PARAMS = ['x', 'positions', 'ln_mix_g', 'ln_mlp_g', 'sf_w_in', 'sf_b_f', 'fox_q_g', 'fox_k_g', 'sf_w_o', 'mla_w_down', 'mla_q_a_g', 'mla_kv_a_g', 'mla_w_uq', 'mla_w_ukv', 'mla_q_g', 'mla_k_g', 'mla_w_o', 'mlp_w_up', 'mlp_w_down']
TWIN_WEIGHTS = ['ln_mix_g', 'ln_mlp_g', 'sf_w_in', 'sf_b_f', 'fox_q_g', 'fox_k_g', 'sf_w_o', 'mla_w_down', 'mla_q_a_g', 'mla_kv_a_g', 'mla_w_uq', 'mla_w_ukv', 'mla_q_g', 'mla_k_g', 'mla_w_o', 'mlp_w_up', 'mlp_w_down']
TWIN_DIFF_INPUT = 'x'
TWIN_INPUTS = ['x', 'positions', 'ln_mix_g', 'ln_mlp_g', 'sf_w_in', 'sf_b_f', 'fox_q_g', 'fox_k_g', 'sf_w_o', 'mla_w_down', 'mla_q_a_g', 'mla_kv_a_g', 'mla_w_uq', 'mla_w_ukv', 'mla_q_g', 'mla_k_g', 'mla_w_o', 'mlp_w_up', 'mlp_w_down', 'loss_target', 'm_ln_mix_g', 'm_ln_mlp_g', 'm_sf_w_in', 'm_sf_b_f', 'm_fox_q_g', 'm_fox_k_g', 'm_sf_w_o', 'm_mla_w_down', 'm_mla_q_a_g', 'm_mla_kv_a_g', 'm_mla_w_uq', 'm_mla_w_ukv', 'm_mla_q_g', 'm_mla_k_g', 'm_mla_w_o', 'm_mlp_w_up', 'm_mlp_w_down', 'v_ln_mix_g', 'v_ln_mlp_g', 'v_sf_w_in', 'v_sf_b_f', 'v_fox_q_g', 'v_fox_k_g', 'v_sf_w_o', 'v_mla_w_down', 'v_mla_q_a_g', 'v_mla_kv_a_g', 'v_mla_w_uq', 'v_mla_w_ukv', 'v_mla_q_g', 'v_mla_k_g', 'v_mla_w_o', 'v_mlp_w_up', 'v_mlp_w_down']
TWIN_OUTPUTS = ['loss', 'grad_x', 'grad_ln_mix_g', 'grad_ln_mlp_g', 'grad_sf_w_in', 'grad_sf_b_f', 'grad_fox_q_g', 'grad_fox_k_g', 'grad_sf_w_o', 'grad_mla_w_down', 'grad_mla_q_a_g', 'grad_mla_kv_a_g', 'grad_mla_w_uq', 'grad_mla_w_ukv', 'grad_mla_q_g', 'grad_mla_k_g', 'grad_mla_w_o', 'grad_mlp_w_up', 'grad_mlp_w_down', 'delta_ln_mix_g', 'delta_ln_mlp_g', 'delta_sf_w_in', 'delta_sf_b_f', 'delta_fox_q_g', 'delta_fox_k_g', 'delta_sf_w_o', 'delta_mla_w_down', 'delta_mla_q_a_g', 'delta_mla_kv_a_g', 'delta_mla_w_uq', 'delta_mla_w_ukv', 'delta_mla_q_g', 'delta_mla_k_g', 'delta_mla_w_o', 'delta_mlp_w_up', 'delta_mlp_w_down', 'new_m_ln_mix_g', 'new_m_ln_mlp_g', 'new_m_sf_w_in', 'new_m_sf_b_f', 'new_m_fox_q_g', 'new_m_fox_k_g', 'new_m_sf_w_o', 'new_m_mla_w_down', 'new_m_mla_q_a_g', 'new_m_mla_kv_a_g', 'new_m_mla_w_uq', 'new_m_mla_w_ukv', 'new_m_mla_q_g', 'new_m_mla_k_g', 'new_m_mla_w_o', 'new_m_mlp_w_up', 'new_m_mlp_w_down', 'new_v_ln_mix_g', 'new_v_ln_mlp_g', 'new_v_sf_w_in', 'new_v_sf_b_f', 'new_v_fox_q_g', 'new_v_fox_k_g', 'new_v_sf_w_o', 'new_v_mla_w_down', 'new_v_mla_q_a_g', 'new_v_mla_kv_a_g', 'new_v_mla_w_uq', 'new_v_mla_w_ukv', 'new_v_mla_q_g', 'new_v_mla_k_g', 'new_v_mla_w_o', 'new_v_mlp_w_up', 'new_v_mlp_w_down']
TWIN_LEAF_KINDS = {'loss': 'loss', 'grad_x': 'grad_x', 'grad_ln_mix_g': 'grad_w', 'grad_ln_mlp_g': 'grad_w', 'grad_sf_w_in': 'grad_w', 'grad_sf_b_f': 'grad_w', 'grad_fox_q_g': 'grad_w', 'grad_fox_k_g': 'grad_w', 'grad_sf_w_o': 'grad_w', 'grad_mla_w_down': 'grad_w', 'grad_mla_q_a_g': 'grad_w', 'grad_mla_kv_a_g': 'grad_w', 'grad_mla_w_uq': 'grad_w', 'grad_mla_w_ukv': 'grad_w', 'grad_mla_q_g': 'grad_w', 'grad_mla_k_g': 'grad_w', 'grad_mla_w_o': 'grad_w', 'grad_mlp_w_up': 'grad_w', 'grad_mlp_w_down': 'grad_w', 'delta_ln_mix_g': 'delta_w', 'delta_ln_mlp_g': 'delta_w', 'delta_sf_w_in': 'delta_w', 'delta_sf_b_f': 'delta_w', 'delta_fox_q_g': 'delta_w', 'delta_fox_k_g': 'delta_w', 'delta_sf_w_o': 'delta_w', 'delta_mla_w_down': 'delta_w', 'delta_mla_q_a_g': 'delta_w', 'delta_mla_kv_a_g': 'delta_w', 'delta_mla_w_uq': 'delta_w', 'delta_mla_w_ukv': 'delta_w', 'delta_mla_q_g': 'delta_w', 'delta_mla_k_g': 'delta_w', 'delta_mla_w_o': 'delta_w', 'delta_mlp_w_up': 'delta_w', 'delta_mlp_w_down': 'delta_w', 'new_m_ln_mix_g': 'new_m', 'new_m_ln_mlp_g': 'new_m', 'new_m_sf_w_in': 'new_m', 'new_m_sf_b_f': 'new_m', 'new_m_fox_q_g': 'new_m', 'new_m_fox_k_g': 'new_m', 'new_m_sf_w_o': 'new_m', 'new_m_mla_w_down': 'new_m', 'new_m_mla_q_a_g': 'new_m', 'new_m_mla_kv_a_g': 'new_m', 'new_m_mla_w_uq': 'new_m', 'new_m_mla_w_ukv': 'new_m', 'new_m_mla_q_g': 'new_m', 'new_m_mla_k_g': 'new_m', 'new_m_mla_w_o': 'new_m', 'new_m_mlp_w_up': 'new_m', 'new_m_mlp_w_down': 'new_m', 'new_v_ln_mix_g': 'new_v', 'new_v_ln_mlp_g': 'new_v', 'new_v_sf_w_in': 'new_v', 'new_v_sf_b_f': 'new_v', 'new_v_fox_q_g': 'new_v', 'new_v_fox_k_g': 'new_v', 'new_v_sf_w_o': 'new_v', 'new_v_mla_w_down': 'new_v', 'new_v_mla_q_a_g': 'new_v', 'new_v_mla_kv_a_g': 'new_v', 'new_v_mla_w_uq': 'new_v', 'new_v_mla_w_ukv': 'new_v', 'new_v_mla_q_g': 'new_v', 'new_v_mla_k_g': 'new_v', 'new_v_mla_w_o': 'new_v', 'new_v_mlp_w_up': 'new_v', 'new_v_mlp_w_down': 'new_v'}


def _forward(args):
    return _fwd_reference(*[args[k] for k in FWD_PARAMS])


def _output_shape():
    def fwd():
        inp = _fwd_setup_inputs(0)
        return _fwd_reference(*[inp[k] for k in FWD_PARAMS])
    out = _jax.eval_shape(fwd)
    return out.shape, out.dtype

N_MICROBATCH = 1
ADAM_LR = 0.001
ADAM_B1 = 0.9
ADAM_B2 = 0.999
ADAM_EPS = 1e-08
ADAM_WD = 0.01
ADAM_STEP = 10
PER_EXAMPLE_BATCH_AXIS = {'x': 0, 'positions': 0, 'loss_target': 0}
SHARED_INPUTS = []
_WEIGHT_DTYPES = {'ln_mix_g': _jnp.float32, 'ln_mlp_g': _jnp.float32, 'sf_w_in': _jnp.float32, 'sf_b_f': _jnp.float32, 'fox_q_g': _jnp.float32, 'fox_k_g': _jnp.float32, 'sf_w_o': _jnp.float32, 'mla_w_down': _jnp.float32, 'mla_q_a_g': _jnp.float32, 'mla_kv_a_g': _jnp.float32, 'mla_w_uq': _jnp.float32, 'mla_w_ukv': _jnp.float32, 'mla_q_g': _jnp.float32, 'mla_k_g': _jnp.float32, 'mla_w_o': _jnp.float32, 'mlp_w_up': _jnp.float32, 'mlp_w_down': _jnp.float32}
MOMENT_SCALE = {'ln_mix_g': 5.501101e+00, 'ln_mlp_g': 4.917334e+01, 'sf_w_in': 2.469573e-01, 'sf_b_f': 5.180908e+01, 'fox_q_g': 5.135144e+00, 'fox_k_g': 5.163190e+00, 'sf_w_o': 3.968550e-01, 'mla_w_down': 7.887584e+00, 'mla_q_a_g': 3.626655e-01, 'mla_kv_a_g': 1.495708e+01, 'mla_w_uq': 1.285177e-01, 'mla_w_ukv': 3.923790e+00, 'mla_q_g': 9.415082e-01, 'mla_k_g': 9.437677e-01, 'mla_w_o': 5.325336e+00, 'mlp_w_up': 2.381036e+00, 'mlp_w_down': 9.636577e+00}


def _to_microbatches(a, axis):
    t = _jnp.moveaxis(a, axis, 0)
    t = t.reshape((N_MICROBATCH, t.shape[0] // N_MICROBATCH) + t.shape[1:])
    return _jnp.moveaxis(t, 1, axis + 1)


def setup_inputs(seed: int = 0) -> dict:
    inp = _fwd_setup_inputs(seed)
    key = _jax.random.fold_in(_jax.random.key(seed), 7919)
    shape, _ = _output_shape()
    out = dict(inp)
    out["loss_target"] = _jax.random.normal(_jax.random.fold_in(key, 0), shape, _jnp.float32)
    for i, name in enumerate(TWIN_WEIGHTS):
        w = inp[name].astype(_jnp.float32)
        if MOMENT_SCALE is None:
            s = _jnp.sqrt(_jnp.mean(_jnp.square(w)) + 1e-30)
        else:
            s = MOMENT_SCALE[name]
        km, kv = _jax.random.split(_jax.random.fold_in(key, i + 1))
        out[name] = w
        out["m_" + name] = s * _jax.random.normal(km, w.shape, _jnp.float32)
        out["v_" + name] = (s * s) * _jax.random.uniform(kv, w.shape, _jnp.float32, 0.5, 1.5)
    if N_MICROBATCH > 1:
        for name, axis in PER_EXAMPLE_BATCH_AXIS.items():
            out[name] = _to_microbatches(out[name], axis)
    return {'x': out['x'], 'positions': out['positions'], 'ln_mix_g': out['ln_mix_g'], 'ln_mlp_g': out['ln_mlp_g'], 'sf_w_in': out['sf_w_in'], 'sf_b_f': out['sf_b_f'], 'fox_q_g': out['fox_q_g'], 'fox_k_g': out['fox_k_g'], 'sf_w_o': out['sf_w_o'], 'mla_w_down': out['mla_w_down'], 'mla_q_a_g': out['mla_q_a_g'], 'mla_kv_a_g': out['mla_kv_a_g'], 'mla_w_uq': out['mla_w_uq'], 'mla_w_ukv': out['mla_w_ukv'], 'mla_q_g': out['mla_q_g'], 'mla_k_g': out['mla_k_g'], 'mla_w_o': out['mla_w_o'], 'mlp_w_up': out['mlp_w_up'], 'mlp_w_down': out['mlp_w_down'], 'loss_target': out['loss_target'], 'm_ln_mix_g': out['m_ln_mix_g'], 'm_ln_mlp_g': out['m_ln_mlp_g'], 'm_sf_w_in': out['m_sf_w_in'], 'm_sf_b_f': out['m_sf_b_f'], 'm_fox_q_g': out['m_fox_q_g'], 'm_fox_k_g': out['m_fox_k_g'], 'm_sf_w_o': out['m_sf_w_o'], 'm_mla_w_down': out['m_mla_w_down'], 'm_mla_q_a_g': out['m_mla_q_a_g'], 'm_mla_kv_a_g': out['m_mla_kv_a_g'], 'm_mla_w_uq': out['m_mla_w_uq'], 'm_mla_w_ukv': out['m_mla_w_ukv'], 'm_mla_q_g': out['m_mla_q_g'], 'm_mla_k_g': out['m_mla_k_g'], 'm_mla_w_o': out['m_mla_w_o'], 'm_mlp_w_up': out['m_mlp_w_up'], 'm_mlp_w_down': out['m_mlp_w_down'], 'v_ln_mix_g': out['v_ln_mix_g'], 'v_ln_mlp_g': out['v_ln_mlp_g'], 'v_sf_w_in': out['v_sf_w_in'], 'v_sf_b_f': out['v_sf_b_f'], 'v_fox_q_g': out['v_fox_q_g'], 'v_fox_k_g': out['v_fox_k_g'], 'v_sf_w_o': out['v_sf_w_o'], 'v_mla_w_down': out['v_mla_w_down'], 'v_mla_q_a_g': out['v_mla_q_a_g'], 'v_mla_kv_a_g': out['v_mla_kv_a_g'], 'v_mla_w_uq': out['v_mla_w_uq'], 'v_mla_w_ukv': out['v_mla_w_ukv'], 'v_mla_q_g': out['v_mla_q_g'], 'v_mla_k_g': out['v_mla_k_g'], 'v_mla_w_o': out['v_mla_w_o'], 'v_mlp_w_up': out['v_mlp_w_up'], 'v_mlp_w_down': out['v_mlp_w_down']}


def _loss(weights, diff, rest, loss_target):
    with _jax.named_scope("forward"):
        args = {**rest, TWIN_DIFF_INPUT: diff, **{k: w.astype(_WEIGHT_DTYPES[k]) for k, w in weights.items()}}
        y = _forward(args)
    with _jax.named_scope("loss_head"):
        err = _jnp.square(y.astype(_jnp.float32) - loss_target)
        return 0.5 * _jnp.sum(_jnp.mean(err, axis=-1)) if err.ndim else 0.5 * err


def _adamw(w, g, m, v):
    m = ADAM_B1 * m + (1.0 - ADAM_B1) * g
    v = ADAM_B2 * v + (1.0 - ADAM_B2) * _jnp.square(g)
    m_hat = m / (1.0 - ADAM_B1 ** ADAM_STEP)
    v_hat = v / (1.0 - ADAM_B2 ** ADAM_STEP)
    delta = -ADAM_LR * (m_hat / (_jnp.sqrt(v_hat) + ADAM_EPS) + ADAM_WD * w)
    return delta, m, v


def reference(x, positions, ln_mix_g, ln_mlp_g, sf_w_in, sf_b_f, fox_q_g, fox_k_g, sf_w_o, mla_w_down, mla_q_a_g, mla_kv_a_g, mla_w_uq, mla_w_ukv, mla_q_g, mla_k_g, mla_w_o, mlp_w_up, mlp_w_down, loss_target, m_ln_mix_g, m_ln_mlp_g, m_sf_w_in, m_sf_b_f, m_fox_q_g, m_fox_k_g, m_sf_w_o, m_mla_w_down, m_mla_q_a_g, m_mla_kv_a_g, m_mla_w_uq, m_mla_w_ukv, m_mla_q_g, m_mla_k_g, m_mla_w_o, m_mlp_w_up, m_mlp_w_down, v_ln_mix_g, v_ln_mlp_g, v_sf_w_in, v_sf_b_f, v_fox_q_g, v_fox_k_g, v_sf_w_o, v_mla_w_down, v_mla_q_a_g, v_mla_kv_a_g, v_mla_w_uq, v_mla_w_ukv, v_mla_q_g, v_mla_k_g, v_mla_w_o, v_mlp_w_up, v_mlp_w_down):
    given = dict(x=x, positions=positions, ln_mix_g=ln_mix_g, ln_mlp_g=ln_mlp_g, sf_w_in=sf_w_in, sf_b_f=sf_b_f, fox_q_g=fox_q_g, fox_k_g=fox_k_g, sf_w_o=sf_w_o, mla_w_down=mla_w_down, mla_q_a_g=mla_q_a_g, mla_kv_a_g=mla_kv_a_g, mla_w_uq=mla_w_uq, mla_w_ukv=mla_w_ukv, mla_q_g=mla_q_g, mla_k_g=mla_k_g, mla_w_o=mla_w_o, mlp_w_up=mlp_w_up, mlp_w_down=mlp_w_down, loss_target=loss_target, m_ln_mix_g=m_ln_mix_g, m_ln_mlp_g=m_ln_mlp_g, m_sf_w_in=m_sf_w_in, m_sf_b_f=m_sf_b_f, m_fox_q_g=m_fox_q_g, m_fox_k_g=m_fox_k_g, m_sf_w_o=m_sf_w_o, m_mla_w_down=m_mla_w_down, m_mla_q_a_g=m_mla_q_a_g, m_mla_kv_a_g=m_mla_kv_a_g, m_mla_w_uq=m_mla_w_uq, m_mla_w_ukv=m_mla_w_ukv, m_mla_q_g=m_mla_q_g, m_mla_k_g=m_mla_k_g, m_mla_w_o=m_mla_w_o, m_mlp_w_up=m_mlp_w_up, m_mlp_w_down=m_mlp_w_down, v_ln_mix_g=v_ln_mix_g, v_ln_mlp_g=v_ln_mlp_g, v_sf_w_in=v_sf_w_in, v_sf_b_f=v_sf_b_f, v_fox_q_g=v_fox_q_g, v_fox_k_g=v_fox_k_g, v_sf_w_o=v_sf_w_o, v_mla_w_down=v_mla_w_down, v_mla_q_a_g=v_mla_q_a_g, v_mla_kv_a_g=v_mla_kv_a_g, v_mla_w_uq=v_mla_w_uq, v_mla_w_ukv=v_mla_w_ukv, v_mla_q_g=v_mla_q_g, v_mla_k_g=v_mla_k_g, v_mla_w_o=v_mla_w_o, v_mlp_w_up=v_mlp_w_up, v_mlp_w_down=v_mlp_w_down)
    weights = {n: given[n] for n in TWIN_WEIGHTS}
    shared = {n: given[n] for n in SHARED_INPUTS}
    per_example = {n: given[n] for n in ['x', 'positions']}
    grad_fn = _jax.value_and_grad(_loss, argnums=(0, 1))

    def one_microbatch(ex, loss_target):
        ex = dict(ex)
        diff = ex.pop(TWIN_DIFF_INPUT)
        return grad_fn(weights, diff, {**shared, **ex}, loss_target)

    if N_MICROBATCH == 1:
        loss, (grad_w, grad_x) = one_microbatch(per_example, given["loss_target"])
    else:
        def body(carry, xs):
            loss_sum, grad_sum = carry
            l_k, (gw_k, gx_k) = one_microbatch(xs[0], xs[1])
            with _jax.named_scope("update"):
                return (loss_sum + l_k, _jax.tree.map(_jnp.add, grad_sum, gw_k)), gx_k

        init = (_jnp.zeros((), _jnp.float32), _jax.tree.map(_jnp.zeros_like, weights))
        (loss, grad_w), grad_x = _jax.lax.scan(body, init, (per_example, given["loss_target"]))
    with _jax.named_scope("update"):
        delta_w, new_m, new_v = {}, {}, {}
        for n in TWIN_WEIGHTS:
            delta_w[n], new_m[n], new_v[n] = _adamw(weights[n], grad_w[n], given["m_" + n], given["v_" + n])
    return (loss, grad_x, *[grad_w[n] for n in TWIN_WEIGHTS], *[delta_w[n] for n in TWIN_WEIGHTS],
            *[new_m[n] for n in TWIN_WEIGHTS], *[new_v[n] for n in TWIN_WEIGHTS])
```

```python
import functools

import numpy as np
import jax
import jax.numpy as jnp
from jax import lax
from jax.experimental import pallas as pl
from jax.experimental.pallas import tpu as pltpu

F32 = jnp.float32
BF16 = jnp.bfloat16
MESH = pl.DeviceIdType.MESH

EPS = 1e-6
HEAD = 128
N_SB = 8
N_FOX = 8
N_MLA = 16
Q_RANK = 512
KV_RANK = 512
NOPE = 128
ROPE = 64
QK_DIM = NOPE + ROPE
QK_PAD = 256
ROPE_THETA = 10000.0
N_CHIPS = 4

ADAM_LR = 0.001
ADAM_B1 = 0.9
ADAM_B2 = 0.999
ADAM_EPS = 1e-08
ADAM_WD = 0.01
ADAM_STEP = 10

VMEM_LIMIT = 56 * 1024 * 1024
LANES = 128
NEG = -1e30


def _cp(*sem):
    return pltpu.CompilerParams(dimension_semantics=sem, vmem_limit_bytes=VMEM_LIMIT)


def _pick(dim, target):
    if dim <= target:
        return dim
    t = (target // LANES) * LANES
    while t >= LANES:
        if dim % t == 0:
            return t
        t -= LANES
    raise ValueError(f"no tile for {dim}")


NT_DIMS = (((1,), (1,)), ((), ()))
TN_DIMS = (((0,), (0,)), ((), ()))


def _dot(a, b):
    return jnp.dot(a, b, preferred_element_type=F32)


def _dot_nt(a, b):
    return lax.dot_general(a, b, NT_DIMS, preferred_element_type=F32)


def _dot_tn(a, b):
    return lax.dot_general(a, b, TN_DIMS, preferred_element_type=F32)


def _matmul(a, b, *, name, form="nn", out_dtype=F32, n=None, b_n0=0, b_split=False,
            out_split=False, epilogue="plain", res=None, u=None, tm=1024, tn=1024, tk=512):
    if form == "tn":
        K, M = a.shape
    else:
        M, K = a.shape
    if b_split:
        if form == "nt":
            nb_full, kb_full = b.shape[1], b.shape[2] * N_CHIPS
        else:
            kb_full, nb_full = b.shape[1], b.shape[2] * N_CHIPS
    elif form == "nt":
        nb_full, kb_full = b.shape
    else:
        kb_full, nb_full = b.shape
    assert kb_full == K, (name, a.shape, b.shape)
    N = nb_full if n is None else n
    tm, tn, tk = _pick(M, tm), _pick(N, tn), _pick(K, tk)
    if b_split:
        per_chip = (b.shape[2])
        if form == "nt":
            tk = _pick(per_chip, tk)
        else:
            tn = _pick(per_chip, tn)
    if out_split:
        tn = _pick(N // N_CHIPS, tn)
    assert b_n0 % tn == 0
    nb0 = b_n0 // tn
    nk = K // tk
    grid = (M // tm, N // tn, nk)

    if form == "tn":
        a_spec = pl.BlockSpec((tk, tm), lambda i, j, k: (k, i))
    else:
        a_spec = pl.BlockSpec((tm, tk), lambda i, j, k: (i, k))
    if b_split:
        if form == "nt":
            kc = b.shape[2] // tk
            b_spec = pl.BlockSpec((None, tn, tk), lambda i, j, k: (k // kc, j, k % kc))
        else:
            nc = b.shape[2] // tn
            b_spec = pl.BlockSpec((None, tk, tn), lambda i, j, k: (j // nc, k, j % nc))
    elif form == "nt":
        b_spec = pl.BlockSpec((tn, tk), lambda i, j, k: (j + nb0, k))
    else:
        b_spec = pl.BlockSpec((tk, tn), lambda i, j, k: (k, j + nb0))
    mn_spec = pl.BlockSpec((tm, tn), lambda i, j, k: (i, j))
    if out_split:
        oc = (N // N_CHIPS) // tn
        out_spec = pl.BlockSpec((None, tm, tn), lambda i, j, k: (j // oc, i, j % oc))
        out_shape = jax.ShapeDtypeStruct((N_CHIPS, M, N // N_CHIPS), out_dtype)
    else:
        out_spec = mn_spec
        out_shape = jax.ShapeDtypeStruct((M, N), out_dtype)

    in_specs = [a_spec, b_spec]
    operands = [a, b]
    out_specs = out_spec
    if epilogue == "res":
        in_specs.append(mn_spec)
        operands.append(res)
    elif epilogue == "sqrelu_bwd":
        in_specs.append(mn_spec)
        operands.append(u)
    elif epilogue == "sqrelu":
        out_specs = (mn_spec, mn_spec)
        out_shape = (jax.ShapeDtypeStruct((M, N), F32), jax.ShapeDtypeStruct((M, N), BF16))

    def body(*refs):
        a_ref, b_ref = refs[0], refs[1]
        acc = refs[-1]
        k = pl.program_id(2)

        @pl.when(k == 0)
        def _():
            acc[...] = jnp.zeros_like(acc)

        at = a_ref[...].astype(BF16)
        bt = b_ref[...].astype(BF16)
        if form == "nn":
            acc[...] += _dot(at, bt)
        elif form == "nt":
            acc[...] += _dot_nt(at, bt)
        else:
            acc[...] += _dot_tn(at, bt)

        @pl.when(k == nk - 1)
        def _():
            r = acc[...]
            if epilogue == "plain":
                refs[2][...] = r.astype(out_dtype)
            elif epilogue == "res":
                refs[3][...] = (refs[2][...] + r).astype(out_dtype)
            elif epilogue == "sqrelu":
                refs[2][...] = r
                p = jnp.maximum(r, 0.0)
                refs[3][...] = (p * p).astype(BF16)
            else:
                refs[3][...] = (r * (2.0 * jnp.maximum(refs[2][...], 0.0))).astype(out_dtype)

    return pl.pallas_call(
        body, out_shape=out_shape, grid=grid, in_specs=in_specs, out_specs=out_specs,
        scratch_shapes=[pltpu.VMEM((tm, tn), F32)], name=name,
        compiler_params=_cp("parallel", "parallel", "arbitrary"))(*operands)


def _rms_fwd(x, g, *, name, c0=0, width=None, gw=None, tr=256):
    R, ctot = x.shape
    C = ctot if width is None else width
    gw = C if gw is None else gw
    assert c0 % C == 0 and C % gw == 0
    tr = _pick(R, tr)
    cb = c0 // C
    ng = C // gw

    def body(x_ref, g_ref, o_ref):
        gv = g_ref[...]
        for gi in range(ng):
            cols = slice(gi * gw, (gi + 1) * gw)
            xs = x_ref[:, cols]
            ms = jnp.sum(xs * xs, axis=-1, keepdims=True) * (1.0 / gw)
            o_ref[:, cols] = ((xs * lax.rsqrt(ms + EPS)) * gv).astype(o_ref.dtype)

    return pl.pallas_call(
        body, out_shape=jax.ShapeDtypeStruct((R, C), BF16), grid=(R // tr,),
        in_specs=[pl.BlockSpec((tr, C), lambda i: (i, cb)), pl.BlockSpec((1, gw), lambda i: (0, 0))],
        out_specs=pl.BlockSpec((tr, C), lambda i: (i, 0)), name=name,
        compiler_params=_cp("parallel"))(x, g.reshape(1, gw).astype(F32))


def _rms_bwd(x, g, dy, *, name, res=None, c0=0, width=None, gw=None, tr=256):
    R, ctot = x.shape
    C = ctot if width is None else width
    gw = C if gw is None else gw
    tr = _pick(R, tr)
    cb = c0 // C
    ng = C // gw
    nsteps = R // tr
    row_spec = pl.BlockSpec((tr, C), lambda i: (i, 0))
    in_specs = [pl.BlockSpec((tr, C), lambda i: (i, cb)), pl.BlockSpec((1, gw), lambda i: (0, 0)), row_spec]
    operands = [x, g.reshape(1, gw).astype(F32), dy]
    if res is not None:
        in_specs.append(row_spec)
        operands.append(res)

    def body(*refs):
        x_ref, g_ref, dy_ref = refs[:3]
        res_ref = refs[3] if res is not None else None
        dx_ref, dg_ref, acc = refs[-3:]
        i = pl.program_id(0)

        @pl.when(i == 0)
        def _():
            acc[...] = jnp.zeros_like(acc)

        gv = g_ref[...]
        for gi in range(ng):
            cols = slice(gi * gw, (gi + 1) * gw)
            xs = x_ref[:, cols]
            dys = dy_ref[:, cols].astype(F32)
            rstd = lax.rsqrt(jnp.sum(xs * xs, axis=-1, keepdims=True) * (1.0 / gw) + EPS)
            xh = xs * rstd
            gdy = dys * gv
            m = jnp.sum(gdy * xh, axis=-1, keepdims=True) * (1.0 / gw)
            dx = rstd * (gdy - xh * m)
            if res_ref is not None:
                dx = dx + res_ref[:, cols]
            dx_ref[:, cols] = dx
            acc[...] += jnp.sum((dys * xh).reshape(tr // 8, 8, gw), axis=0)

        @pl.when(i == nsteps - 1)
        def _():
            dg_ref[...] = jnp.sum(acc[...], axis=0, keepdims=True)

    dx, dg = pl.pallas_call(
        body, out_shape=(jax.ShapeDtypeStruct((R, C), F32), jax.ShapeDtypeStruct((1, gw), F32)),
        grid=(nsteps,), in_specs=in_specs,
        out_specs=(row_spec, pl.BlockSpec((1, gw), lambda i: (0, 0))),
        scratch_shapes=[pltpu.VMEM((8, gw), F32)], name=name,
        compiler_params=_cp("arbitrary"))(*operands)
    return dx, dg[0]


def _split3(x):
    hi = x.astype(BF16)
    r1 = x - hi.astype(F32)
    mid = r1.astype(BF16)
    lo = (r1 - mid.astype(F32)).astype(BF16)
    return hi, mid, lo


def _log_sigmoid(z):
    return jnp.minimum(z, 0.0) - jnp.log(1.0 + jnp.exp(-jnp.abs(z)))


def _forget_fwd(fl, b, *, name, tb=512):
    S = fl.shape[0]
    tb = _pick(S, tb)

    def body(fl_ref, b_ref, f_ref, carry):
        i = pl.program_id(0)

        @pl.when(i == 0)
        def _():
            carry[...] = jnp.zeros_like(carry)

        lf = _log_sigmoid(fl_ref[...] + b_ref[...])
        r = lax.broadcasted_iota(jnp.int32, (tb, tb), 0)
        c = lax.broadcasted_iota(jnp.int32, (tb, tb), 1)
        tri = (c <= r).astype(BF16)
        hi, mid, lo = _split3(lf)
        cs = _dot(tri, hi) + _dot(tri, mid) + _dot(tri, lo)
        f_ref[...] = cs + carry[...]
        carry[...] += jnp.sum(lf, axis=0, keepdims=True)

    return pl.pallas_call(
        body, out_shape=jax.ShapeDtypeStruct((S, LANES), F32), grid=(S // tb,),
        in_specs=[pl.BlockSpec((tb, LANES), lambda i: (i, 0)), pl.BlockSpec((1, LANES), lambda i: (0, 0))],
        out_specs=pl.BlockSpec((tb, LANES), lambda i: (i, 0)),
        scratch_shapes=[pltpu.VMEM((1, LANES), F32)], name=name,
        compiler_params=_cp("arbitrary"))(fl, b)


def _forget_bwd(fl, b, dF, *, name, tb=512):
    S = fl.shape[0]
    tb = _pick(S, tb)
    nb = S // tb

    def body(fl_ref, b_ref, df_ref, dfl_ref, db_ref, carry, acc):
        i = pl.program_id(0)

        @pl.when(i == 0)
        def _():
            carry[...] = jnp.zeros_like(carry)
            acc[...] = jnp.zeros_like(acc)

        d = df_ref[...]
        r = lax.broadcasted_iota(jnp.int32, (tb, tb), 0)
        c = lax.broadcasted_iota(jnp.int32, (tb, tb), 1)
        tri = (c >= r).astype(BF16)
        hi, mid, lo = _split3(d)
        rc = _dot(tri, hi) + _dot(tri, mid) + _dot(tri, lo) + carry[...]
        z = fl_ref[...] + b_ref[...]
        dfl = rc * jnp.exp(_log_sigmoid(-z))
        dfl_ref[...] = dfl
        carry[...] += jnp.sum(d, axis=0, keepdims=True)
        acc[...] += jnp.sum(dfl, axis=0, keepdims=True)

        @pl.when(i == nb - 1)
        def _():
            db_ref[...] = acc[...]

    rev = lambda i: (nb - 1 - i, 0)
    dfl, db = pl.pallas_call(
        body, out_shape=(jax.ShapeDtypeStruct((S, LANES), F32), jax.ShapeDtypeStruct((1, LANES), F32)),
        grid=(nb,),
        in_specs=[pl.BlockSpec((tb, LANES), rev), pl.BlockSpec((1, LANES), lambda i: (0, 0)),
                  pl.BlockSpec((tb, LANES), rev)],
        out_specs=(pl.BlockSpec((tb, LANES), rev), pl.BlockSpec((1, LANES), lambda i: (0, 0))),
        scratch_shapes=[pltpu.VMEM((1, LANES), F32), pltpu.VMEM((1, LANES), F32)], name=name,
        compiler_params=_cp("arbitrary"))(fl, b, dF)
    return dfl, db[0]


def _pos(qi, j, tq, tk):
    row = qi * tq + lax.broadcasted_iota(jnp.int32, (tq, tk), 0)
    col = j * tk + lax.broadcasted_iota(jnp.int32, (tq, tk), 1)
    return row, col


def _tri(tk, rel):
    r = lax.broadcasted_iota(jnp.int32, (tk, tk), 0)
    c = lax.broadcasted_iota(jnp.int32, (tk, tk), 1)
    m = {"gt": r > c, "le": r <= c, "lt": r < c}[rel]
    return m.astype(BF16)


def _split2(x):
    hi = x.astype(BF16)
    return hi, (x - hi.astype(F32)).astype(BF16)


def _sb_fwd(qkv, *, name, n_heads=N_SB, q_off=0, k_off=N_SB, v_off=2 * N_SB, tq=256):
    S = qkv.shape[0]
    tq = _pick(S, tq)
    tk = tq
    scale = HEAD ** -0.5
    nq = S // tq

    def body(q_ref, k_ref, v_ref, o_ref, t_ref, c_sc, acc_sc):
        qi = pl.program_id(1)
        c_sc[...] = jnp.zeros_like(c_sc)
        acc_sc[...] = jnp.zeros_like(acc_sc)
        q = q_ref[...]
        gt = _tri(tk, "gt")

        def step(it, carry):
            j = qi - it
            rows = pl.ds(pl.multiple_of(j * tk, tk), tk)
            z = _dot_nt(q, k_ref[rows, :]) * scale
            row, col = _pos(qi, j, tq, tk)
            strict = col < row
            sp = jnp.log(1.0 + jnp.exp(-jnp.abs(z)))
            la = jnp.minimum(z, 0.0) - sp
            lb = jnp.where(strict, -jnp.maximum(z, 0.0) - sp, 0.0)
            hi, lo = _split2(lb)
            suffix = _dot(hi, gt) + _dot(lo, gt)
            w = jnp.where(strict, jnp.exp(la + suffix + c_sc[...]), 0.0)
            acc_sc[...] += _dot(w.astype(BF16), v_ref[rows, :])
            c_sc[...] += jnp.sum(lb, axis=1, keepdims=True)
            return carry

        lax.fori_loop(0, qi + 1, step, 0)
        o_ref[...] = acc_sc[...]
        t_ref[...] = jnp.broadcast_to(c_sc[...], (tq, LANES))

    head_blk = lambda off: pl.BlockSpec((S, HEAD), lambda h, i: (0, h + off))
    return pl.pallas_call(
        body,
        out_shape=(jax.ShapeDtypeStruct((S, n_heads * HEAD), F32),
                   jax.ShapeDtypeStruct((n_heads, S, LANES), F32)),
        grid=(n_heads, nq),
        in_specs=[pl.BlockSpec((tq, HEAD), lambda h, i: (i, h + q_off)), head_blk(k_off), head_blk(v_off)],
        out_specs=(pl.BlockSpec((tq, HEAD), lambda h, i: (i, h)),
                   pl.BlockSpec((None, tq, LANES), lambda h, i: (h, i, 0))),
        scratch_shapes=[pltpu.VMEM((tq, 1), F32), pltpu.VMEM((tq, HEAD), F32)], name=name,
        compiler_params=_cp("parallel", "arbitrary"))(qkv, qkv, qkv)


def _sb_bwd(qkv, do, tstat, *, name, n_heads=N_SB, q_off=0, k_off=N_SB, v_off=2 * N_SB, do_off=0, tq=256):
    S = qkv.shape[0]
    tq = _pick(S, tq)
    tk = tq
    scale = HEAD ** -0.5
    nq = S // tq

    def body(q_ref, k_ref, v_ref, do_ref, t_ref, dq_ref, dk_ref, dv_ref, p_sc, r_sc, dq_sc):
        qi = pl.program_id(1)

        @pl.when(qi == 0)
        def _():
            dk_ref[...] = jnp.zeros_like(dk_ref)
            dv_ref[...] = jnp.zeros_like(dv_ref)

        p_sc[...] = jnp.zeros_like(p_sc)
        r_sc[...] = jnp.zeros_like(r_sc)
        dq_sc[...] = jnp.zeros_like(dq_sc)
        q = q_ref[...]
        do_b = do_ref[...].astype(BF16)
        total = t_ref[:, 0:1]
        le = _tri(tk, "le")
        lt = _tri(tk, "lt")

        def step(j, carry):
            rows = pl.ds(pl.multiple_of(j * tk, tk), tk)
            kb = k_ref[rows, :]
            vb = v_ref[rows, :]
            z = _dot_nt(q, kb) * scale
            row, col = _pos(qi, j, tq, tk)
            strict = col < row
            sp = jnp.log(1.0 + jnp.exp(-jnp.abs(z)))
            la = jnp.minimum(z, 0.0) - sp
            lb = jnp.where(strict, -jnp.maximum(z, 0.0) - sp, 0.0)
            hi, lo = _split2(lb)
            prefix = _dot(hi, le) + _dot(lo, le) + p_sc[...]
            w = jnp.where(strict, jnp.exp(la + total - prefix), 0.0)
            g = _dot_nt(do_b, vb)
            r = w * g
            hi, lo = _split2(r)
            rex = _dot(hi, lt) + _dot(lo, lt) + r_sc[...]
            beta = jnp.exp(la)
            dz = jnp.where(strict, r * (1.0 - beta) - rex * beta, 0.0) * scale
            dzb = dz.astype(BF16)
            dq_sc[...] += _dot(dzb, kb)
            dk_ref[rows, :] += _dot_tn(dzb, q)
            dv_ref[rows, :] += _dot_tn(w.astype(BF16), do_b)
            p_sc[...] += jnp.sum(lb, axis=1, keepdims=True)
            r_sc[...] += jnp.sum(r, axis=1, keepdims=True)
            return carry

        lax.fori_loop(0, qi + 1, step, 0)
        dq_ref[...] = dq_sc[...]

    head_blk = lambda off: pl.BlockSpec((S, HEAD), lambda h, i: (0, h + off))
    out_head = pl.BlockSpec((S, HEAD), lambda h, i: (0, h))
    out_sd = jax.ShapeDtypeStruct((S, n_heads * HEAD), F32)
    return pl.pallas_call(
        body, out_shape=(out_sd, out_sd, out_sd), grid=(n_heads, nq),
        in_specs=[pl.BlockSpec((tq, HEAD), lambda h, i: (i, h + q_off)), head_blk(k_off), head_blk(v_off),
                  pl.BlockSpec((tq, HEAD), lambda h, i: (i, h + do_off)),
                  pl.BlockSpec((None, tq, LANES), lambda h, i: (h, i, 0))],
        out_specs=(pl.BlockSpec((tq, HEAD), lambda h, i: (i, h)), out_head, out_head),
        scratch_shapes=[pltpu.VMEM((tq, 1), F32), pltpu.VMEM((tq, 1), F32), pltpu.VMEM((tq, HEAD), F32)],
        name=name, compiler_params=_cp("parallel", "arbitrary"))(qkv, qkv, qkv, do, tstat)


def _attn_fwd(q, k, v, bias, *, name, n_heads, dqk, scale, v_off=0, tq=512, exact_p=False):
    S = q.shape[0]
    tq = _pick(S, tq)
    tk = tq
    nq = S // tq
    has_bias = bias is not None

    def body(*refs):
        q_ref, k_ref, v_ref = refs[:3]
        b_ref = refs[3] if has_bias else None
        o_ref, lse_ref, m_sc, l_sc, acc_sc = refs[-5:]
        qi = pl.program_id(1)
        m_sc[...] = jnp.full_like(m_sc, NEG)
        l_sc[...] = jnp.zeros_like(l_sc)
        acc_sc[...] = jnp.zeros_like(acc_sc)
        qb = q_ref[...]

        def step(j, carry):
            rows = pl.ds(pl.multiple_of(j * tk, tk), tk)
            s = _dot_nt(qb, k_ref[rows, :]) * scale
            if has_bias:
                s = s + b_ref[:, rows]
            row, col = _pos(qi, j, tq, tk)
            s = jnp.where(col <= row, s, NEG)
            m_old = m_sc[...]
            m_new = jnp.maximum(m_old, jnp.max(s, axis=1, keepdims=True))
            alpha = jnp.exp(m_old - m_new)
            p = jnp.exp(s - m_new)
            l_sc[...] = alpha * l_sc[...] + jnp.sum(p, axis=1, keepdims=True)
            vb = v_ref[rows, :]
            if exact_p:
                hi, lo = _split2(p)
                pv = _dot(hi, vb) + _dot(lo, vb)
            else:
                pv = _dot(p.astype(BF16), vb)
            acc_sc[...] = alpha * acc_sc[...] + pv
            m_sc[...] = m_new
            return carry

        lax.fori_loop(0, qi + 1, step, 0)
        l = l_sc[...]
        o_ref[...] = acc_sc[...] / l
        lse_ref[...] = jnp.broadcast_to(m_sc[...] + jnp.log(l), (tq, LANES))

    in_specs = [pl.BlockSpec((tq, dqk), lambda h, i: (i, h)),
                pl.BlockSpec((S, dqk), lambda h, i: (0, h)),
                pl.BlockSpec((S, HEAD), lambda h, i: (0, h + v_off))]
    operands = [q, k, v]
    if has_bias:
        in_specs.append(pl.BlockSpec((None, 1, S), lambda h, i: (h, 0, 0)))
        operands.append(bias)
    return pl.pallas_call(
        body,
        out_shape=(jax.ShapeDtypeStruct((S, n_heads * HEAD), F32),
                   jax.ShapeDtypeStruct((n_heads, S, LANES), F32)),
        grid=(n_heads, nq), in_specs=in_specs,
        out_specs=(pl.BlockSpec((tq, HEAD), lambda h, i: (i, h)),
                   pl.BlockSpec((None, tq, LANES), lambda h, i: (h, i, 0))),
        scratch_shapes=[pltpu.VMEM((tq, 1), F32), pltpu.VMEM((tq, 1), F32), pltpu.VMEM((tq, HEAD), F32)],
        name=name, compiler_params=_cp("parallel", "arbitrary"))(*operands)


def _attn_bwd(q, k, v, bias, o, lse, do, *, name, n_heads, dqk, scale, v_off=0, do_off=0, tq=512):
    S = q.shape[0]
    tq = _pick(S, tq)
    tk = tq
    nq = S // tq
    has_bias = bias is not None

    def body(*refs):
        q_ref, k_ref, v_ref, o_ref, lse_ref, do_ref = refs[:6]
        b_ref = refs[6] if has_bias else None
        n_out = 5 if has_bias else 3
        outs = refs[-(n_out + 2):-2]
        dq_ref, dk_ref, dv_ref = outs[:3]
        db_ref, dr_ref = (outs[3], outs[4]) if has_bias else (None, None)
        dq_sc, rs_sc = refs[-2:]
        qi = pl.program_id(1)

        @pl.when(qi == 0)
        def _():
            dk_ref[...] = jnp.zeros_like(dk_ref)
            dv_ref[...] = jnp.zeros_like(dv_ref)
            if has_bias:
                db_ref[...] = jnp.zeros_like(db_ref)

        dq_sc[...] = jnp.zeros_like(dq_sc)
        rs_sc[...] = jnp.zeros_like(rs_sc)
        qb = q_ref[...]
        do_b = do_ref[...].astype(BF16)
        delta = jnp.sum(do_b.astype(F32) * o_ref[...], axis=1, keepdims=True)
        lse_c = lse_ref[:, 0:1]

        def step(j, carry):
            rows = pl.ds(pl.multiple_of(j * tk, tk), tk)
            kb = k_ref[rows, :]
            vb = v_ref[rows, :]
            s = _dot_nt(qb, kb) * scale
            if has_bias:
                s = s + b_ref[:, rows]
            row, col = _pos(qi, j, tq, tk)
            p = jnp.where(col <= row, jnp.exp(s - lse_c), 0.0)
            dp = _dot_nt(do_b, vb)
            ds = p * (dp - delta)
            dsb = (ds * scale).astype(BF16)
            dq_sc[...] += _dot(dsb, kb)
            dk_ref[rows, :] += _dot_tn(dsb, qb)
            dv_ref[rows, :] += _dot_tn(p.astype(BF16), do_b)
            if has_bias:
                db_ref[:, rows] += jnp.sum(ds, axis=0, keepdims=True)
                rs_sc[...] += jnp.sum(ds, axis=1, keepdims=True)
            return carry

        lax.fori_loop(0, qi + 1, step, 0)
        dq_ref[...] = dq_sc[...]
        if has_bias:
            dr_ref[...] = jnp.broadcast_to(rs_sc[...], (tq, LANES))

    in_specs = [pl.BlockSpec((tq, dqk), lambda h, i: (i, h)),
                pl.BlockSpec((S, dqk), lambda h, i: (0, h)),
                pl.BlockSpec((S, HEAD), lambda h, i: (0, h + v_off)),
                pl.BlockSpec((tq, HEAD), lambda h, i: (i, h)),
                pl.BlockSpec((None, tq, LANES), lambda h, i: (h, i, 0)),
                pl.BlockSpec((tq, HEAD), lambda h, i: (i, h + do_off))]
    operands = [q, k, v, o, lse, do]
    out_shape = [jax.ShapeDtypeStruct((S, n_heads * dqk), F32), jax.ShapeDtypeStruct((S, n_heads * dqk), F32),
                 jax.ShapeDtypeStruct((S, n_heads * HEAD), F32)]
    out_specs = [pl.BlockSpec((tq, dqk), lambda h, i: (i, h)), pl.BlockSpec((S, dqk), lambda h, i: (0, h)),
                 pl.BlockSpec((S, HEAD), lambda h, i: (0, h))]
    if has_bias:
        in_specs.append(pl.BlockSpec((None, 1, S), lambda h, i: (h, 0, 0)))
        operands.append(bias)
        out_shape.append(jax.ShapeDtypeStruct((n_heads, 1, S), F32))
        out_specs.append(pl.BlockSpec((None, 1, S), lambda h, i: (h, 0, 0)))
        out_shape.append(jax.ShapeDtypeStruct((n_heads, S, LANES), F32))
        out_specs.append(pl.BlockSpec((None, tq, LANES), lambda h, i: (h, i, 0)))
    return pl.pallas_call(
        body, out_shape=tuple(out_shape), grid=(n_heads, nq), in_specs=in_specs, out_specs=tuple(out_specs),
        scratch_shapes=[pltpu.VMEM((tq, dqk), F32), pltpu.VMEM((tq, 1), F32)], name=name,
        compiler_params=_cp("parallel", "arbitrary"))(*operands)


def _rot_half(y):
    lane = lax.broadcasted_iota(jnp.int32, y.shape, 1)
    up = pltpu.roll(y, 96, 1)
    down = pltpu.roll(y, 32, 1)
    return jnp.where(lane < 32, -up, jnp.where(lane < 64, down, 0.0))


def _mla_prep_fwd(q_raw, kv_raw, down, cos, sin, q_g, k_g, *, name, ts=128):
    S = q_raw.shape[0]
    ts = _pick(S, ts)
    pe_blk = Q_RANK // LANES + KV_RANK // LANES

    def norm_rope(x0, x1, g0, g1, c, s):
        ms = (jnp.sum(x0 * x0, axis=-1, keepdims=True) + jnp.sum(x1 * x1, axis=-1, keepdims=True)) * (1.0 / QK_DIM)
        rstd = lax.rsqrt(ms + EPS)
        y0 = (x0 * rstd) * g0
        y1 = (x1 * rstd) * g1
        return y0, y1 * c + _rot_half(y1) * s

    def body(q_ref, kv_ref, pe_ref, cos_ref, sin_ref, qg_ref, kg_ref, qo_ref, ko_ref, vo_ref):
        c, s = cos_ref[...], sin_ref[...]
        pe = pe_ref[...]
        qg0, qg1 = qg_ref[:, :NOPE], qg_ref[:, NOPE:]
        kg0, kg1 = kg_ref[:, :NOPE], kg_ref[:, NOPE:]
        for h in range(N_MLA):
            b = h * QK_PAD
            y0, y1 = norm_rope(q_ref[:, b:b + NOPE], q_ref[:, b + NOPE:b + QK_PAD], qg0, qg1, c, s)
            qo_ref[:, b:b + NOPE] = y0.astype(BF16)
            qo_ref[:, b + NOPE:b + QK_PAD] = y1.astype(BF16)
            y0, y1 = norm_rope(kv_ref[:, b:b + NOPE], pe, kg0, kg1, c, s)
            ko_ref[:, b:b + NOPE] = y0.astype(BF16)
            ko_ref[:, b + NOPE:b + QK_PAD] = y1.astype(BF16)
            vo_ref[:, h * HEAD:(h + 1) * HEAD] = kv_ref[:, b + NOPE:b + QK_PAD].astype(BF16)

    wide = pl.BlockSpec((ts, N_MLA * QK_PAD), lambda i: (i, 0))
    lane_blk = pl.BlockSpec((ts, LANES), lambda i: (i, 0))
    gain = pl.BlockSpec((1, QK_PAD), lambda i: (0, 0))
    return pl.pallas_call(
        body,
        out_shape=(jax.ShapeDtypeStruct((S, N_MLA * QK_PAD), BF16), jax.ShapeDtypeStruct((S, N_MLA * QK_PAD), BF16),
                   jax.ShapeDtypeStruct((S, N_MLA * HEAD), BF16)),
        grid=(S // ts,),
        in_specs=[wide, wide, pl.BlockSpec((ts, LANES), lambda i: (i, pe_blk)), lane_blk, lane_blk, gain, gain],
        out_specs=(wide, wide, pl.BlockSpec((ts, N_MLA * HEAD), lambda i: (i, 0))), name=name,
        compiler_params=_cp("parallel"))(q_raw, kv_raw, down, cos, sin, q_g, k_g)


def _mla_prep_bwd(dq, dk, dv, q_raw, kv_raw, down, cos, sin, q_g, k_g, *, name, ts=128):
    S = q_raw.shape[0]
    ts = _pick(S, ts)
    nsteps = S // ts
    pe_blk = Q_RANK // LANES + KV_RANK // LANES

    def back(x0, x1, g0, g1, c, s, d0, d1r):
        d1 = d1r * c - _rot_half(d1r * s)
        ms = (jnp.sum(x0 * x0, axis=-1, keepdims=True) + jnp.sum(x1 * x1, axis=-1, keepdims=True)) * (1.0 / QK_DIM)
        rstd = lax.rsqrt(ms + EPS)
        h0, h1 = x0 * rstd, x1 * rstd
        e0, e1 = d0 * g0, d1 * g1
        m = (jnp.sum(e0 * h0, axis=-1, keepdims=True) + jnp.sum(e1 * h1, axis=-1, keepdims=True)) * (1.0 / QK_DIM)
        return rstd * (e0 - h0 * m), rstd * (e1 - h1 * m), d0 * h0, d1 * h1

    def fold(a):
        return jnp.sum(a.reshape(ts // 8, 8, a.shape[-1]), axis=0)

    def body(dq_ref, dk_ref, dv_ref, q_ref, kv_ref, pe_ref, cos_ref, sin_ref, qg_ref, kg_ref,
             dqr_ref, dkv_ref, dpe_ref, dqg_ref, dkg_ref, gq_sc, gk_sc):
        i = pl.program_id(0)

        @pl.when(i == 0)
        def _():
            gq_sc[...] = jnp.zeros_like(gq_sc)
            gk_sc[...] = jnp.zeros_like(gk_sc)

        c, s = cos_ref[...], sin_ref[...]
        pe = pe_ref[...]
        qg0, qg1 = qg_ref[:, :NOPE], qg_ref[:, NOPE:]
        kg0, kg1 = kg_ref[:, :NOPE], kg_ref[:, NOPE:]
        dpe = jnp.zeros((ts, LANES), F32)
        for h in range(N_MLA):
            b = h * QK_PAD
            dx0, dx1, a0, a1 = back(q_ref[:, b:b + NOPE], q_ref[:, b + NOPE:b + QK_PAD], qg0, qg1, c, s,
                                    dq_ref[:, b:b + NOPE], dq_ref[:, b + NOPE:b + QK_PAD])
            dqr_ref[:, b:b + NOPE] = dx0.astype(BF16)
            dqr_ref[:, b + NOPE:b + QK_PAD] = dx1.astype(BF16)
            gq_sc[:, :NOPE] += fold(a0)
            gq_sc[:, NOPE:] += fold(a1)
            dx0, dx1, a0, a1 = back(kv_ref[:, b:b + NOPE], pe, kg0, kg1, c, s,
                                    dk_ref[:, b:b + NOPE], dk_ref[:, b + NOPE:b + QK_PAD])
            dkv_ref[:, b:b + NOPE] = dx0.astype(BF16)
            dkv_ref[:, b + NOPE:b + QK_PAD] = dv_ref[:, h * HEAD:(h + 1) * HEAD].astype(BF16)
            dpe = dpe + dx1
            gk_sc[:, :NOPE] += fold(a0)
            gk_sc[:, NOPE:] += fold(a1)
        dpe_ref[...] = dpe

        @pl.when(i == nsteps - 1)
        def _():
            dqg_ref[...] = jnp.sum(gq_sc[...], axis=0, keepdims=True)
            dkg_ref[...] = jnp.sum(gk_sc[...], axis=0, keepdims=True)

    wide = pl.BlockSpec((ts, N_MLA * QK_PAD), lambda i: (i, 0))
    lane_blk = pl.BlockSpec((ts, LANES), lambda i: (i, 0))
    gain = pl.BlockSpec((1, QK_PAD), lambda i: (0, 0))
    outs = pl.pallas_call(
        body,
        out_shape=(jax.ShapeDtypeStruct((S, N_MLA * QK_PAD), BF16), jax.ShapeDtypeStruct((S, N_MLA * QK_PAD), BF16),
                   jax.ShapeDtypeStruct((S, LANES), F32), jax.ShapeDtypeStruct((1, QK_PAD), F32),
                   jax.ShapeDtypeStruct((1, QK_PAD), F32)),
        grid=(nsteps,),
        in_specs=[wide, wide, pl.BlockSpec((ts, N_MLA * HEAD), lambda i: (i, 0)), wide, wide,
                  pl.BlockSpec((ts, LANES), lambda i: (i, pe_blk)), lane_blk, lane_blk, gain, gain],
        out_specs=(wide, wide, lane_blk, gain, gain),
        scratch_shapes=[pltpu.VMEM((8, QK_PAD), F32), pltpu.VMEM((8, QK_PAD), F32)], name=name,
        compiler_params=_cp("arbitrary"))(dq, dk, dv, q_raw, kv_raw, down, cos, sin, q_g, k_g)
    return outs[0], outs[1], outs[2], outs[3][0], outs[4][0]


def _loss_head(y, target, *, name, tr=256):
    R, C = y.shape
    tr = _pick(R, tr)
    nsteps = R // tr

    def body(y_ref, t_ref, dy_ref, loss_ref, acc):
        i = pl.program_id(0)

        @pl.when(i == 0)
        def _():
            acc[...] = jnp.zeros_like(acc)

        err = y_ref[...] - t_ref[...]
        dy_ref[...] = err * (1.0 / C)
        acc[...] += jnp.sum((err * err).reshape(tr // 8, 8, C), axis=0)

        @pl.when(i == nsteps - 1)
        def _():
            tot = jnp.sum(jnp.sum(acc[...], axis=0, keepdims=True), axis=1, keepdims=True)
            loss_ref[...] = jnp.broadcast_to(tot * (0.5 / C), (8, LANES))

    blk = pl.BlockSpec((tr, C), lambda i: (i, 0))
    dy, loss = pl.pallas_call(
        body, out_shape=(jax.ShapeDtypeStruct((R, C), F32), jax.ShapeDtypeStruct((8, LANES), F32)),
        grid=(nsteps,), in_specs=[blk, blk], out_specs=(blk, pl.BlockSpec((8, LANES), lambda i: (0, 0))),
        scratch_shapes=[pltpu.VMEM((8, C), F32)], name=name, compiler_params=_cp("arbitrary"))(y, target)
    return dy, loss[0, 0]


def _adamw(w, g, m, v, *, name, block_bytes=1 << 20):
    R, C = w.shape
    tr = max(8, min(R, (block_bytes // (4 * C)) // 8 * 8))
    while R % tr:
        tr -= 8
    if tr <= 0:
        tr = R
    c1 = 1.0 / (1.0 - ADAM_B1 ** ADAM_STEP)
    c2 = 1.0 / (1.0 - ADAM_B2 ** ADAM_STEP)

    def body(w_ref, g_ref, m_ref, v_ref, d_ref, mo_ref, vo_ref):
        gv = g_ref[...]
        mn = ADAM_B1 * m_ref[...] + (1.0 - ADAM_B1) * gv
        vn = ADAM_B2 * v_ref[...] + (1.0 - ADAM_B2) * (gv * gv)
        d_ref[...] = -ADAM_LR * ((mn * c1) / (jnp.sqrt(vn * c2) + ADAM_EPS) + ADAM_WD * w_ref[...])
        mo_ref[...] = mn
        vo_ref[...] = vn

    blk = pl.BlockSpec((tr, C), lambda i: (i, 0))
    sd = jax.ShapeDtypeStruct((R, C), F32)
    return pl.pallas_call(
        body, out_shape=(sd, sd, sd), grid=(R // tr,), in_specs=[blk] * 4, out_specs=(blk,) * 3, name=name,
        compiler_params=_cp("parallel"))(w, g, m, v)


def _row_tile(r, c, itemsize=4, block_bytes=1 << 20):
    tr = max(16, min(r, (block_bytes // (itemsize * c)) // 16 * 16))
    while r % tr:
        tr -= 16
    return tr if tr > 0 else r


def _add_sibling(g, recv, core, *, name):
    nch, _, r, c = g.shape
    tr = _row_tile(r, c)

    def body(core_ref, g_ref, r_ref, o_ref):
        o_ref[...] = (g_ref[...] + r_ref[...]).astype(BF16)

    grid_spec = pltpu.PrefetchScalarGridSpec(
        num_scalar_prefetch=1, grid=(nch, r // tr),
        in_specs=[pl.BlockSpec((None, None, tr, c), lambda j, i, cr: (j, cr[0], i, 0)),
                  pl.BlockSpec((None, tr, c), lambda j, i, cr: (j, i, 0))],
        out_specs=pl.BlockSpec((None, tr, c), lambda j, i, cr: (j, i, 0)))
    return pl.pallas_call(
        body, out_shape=jax.ShapeDtypeStruct((nch, r, c), BF16), grid_spec=grid_spec, name=name,
        compiler_params=_cp("parallel", "parallel"))(core, g, recv)


def _add_chips(slots, *, name):
    nch, r, c = slots.shape
    tr = _row_tile(r, c)

    def body(s_ref, o_ref):
        acc = s_ref[0].astype(F32)
        for j in range(1, nch):
            acc = acc + s_ref[j].astype(F32)
        o_ref[...] = acc

    return pl.pallas_call(
        body, out_shape=jax.ShapeDtypeStruct((r, c), F32), grid=(r // tr,),
        in_specs=[pl.BlockSpec((nch, tr, c), lambda i: (0, i, 0))],
        out_specs=pl.BlockSpec((tr, c), lambda i: (i, 0)), name=name, compiler_params=_cp("parallel"))(slots)


def _place():
    x, y, c = lax.axis_index("x"), lax.axis_index("y"), lax.axis_index("c")
    others = [(1 - x, y), (x, 1 - y), (1 - x, 1 - y)]
    return x, y, c, 2 * x + y, others


ANY = pl.BlockSpec(memory_space=pl.ANY)


def _all_gather(shards, *, name):
    n = len(shards)

    def body(*refs):
        ins, outs = refs[:n], refs[n:2 * n]
        send, recv, local = refs[2 * n:]
        x, y, c, me, others = _place()
        sib = (x, y, 1 - c)

        def half(a, chip, core):
            hr = shards[a].shape[0] // 2
            return outs[a].at[chip, pl.ds(core * hr, hr), :]

        def remote(a, k, src, dst, to):
            return pltpu.make_async_remote_copy(src_ref=src, dst_ref=dst, send_sem=send.at[6 * a + k],
                                                recv_sem=recv.at[6 * a + k], device_id=to, device_id_type=MESH)

        locals_, sends = [], []
        for a in range(n):
            hr = shards[a].shape[0] // 2
            cp = pltpu.make_async_copy(ins[a], outs[a].at[me], local.at[a])
            cp.start()
            locals_.append(cp)
            mine = ins[a].at[pl.ds(c * hr, hr), :]
            for k, (ox, oy) in enumerate(others):
                cp = remote(a, k, mine, half(a, me, c), (ox, oy, c))
                cp.start()
                sends.append(cp)
        for a in range(n):
            for k, (ox, oy) in enumerate(others):
                landed = half(a, 2 * ox + oy, c)
                remote(a, k, landed, landed, (ox, oy, c)).wait_recv()
                cp = remote(a, 3 + k, landed, landed, sib)
                cp.start()
                sends.append(cp)
        for a in range(n):
            for k, (ox, oy) in enumerate(others):
                got = half(a, 2 * ox + oy, 1 - c)
                remote(a, 3 + k, got, got, sib).wait_recv()
        for cp in sends:
            cp.wait_send()
        for cp in locals_:
            cp.wait()

    return pl.pallas_call(
        body, out_shape=tuple(jax.ShapeDtypeStruct((N_CHIPS,) + s.shape, s.dtype) for s in shards),
        in_specs=[ANY] * n, out_specs=tuple([ANY] * n),
        scratch_shapes=[pltpu.SemaphoreType.DMA((6 * n,)), pltpu.SemaphoreType.DMA((6 * n,)),
                        pltpu.SemaphoreType.DMA((n,))],
        name=name, compiler_params=pltpu.CompilerParams(has_side_effects=True))(*shards)


def _swap_halves(grads, *, name):
    n = len(grads)

    def body(*refs):
        ins, outs = refs[:n], refs[n:2 * n]
        send, recv = refs[2 * n:]
        x, y, c, me, others = _place()
        cps = []
        for a in range(n):
            cp = pltpu.make_async_remote_copy(src_ref=ins[a].at[:, 1 - c], dst_ref=outs[a], send_sem=send.at[a],
                                              recv_sem=recv.at[a], device_id=(x, y, 1 - c), device_id_type=MESH)
            cp.start()
            cps.append(cp)
        for cp in cps:
            cp.wait()

    return pl.pallas_call(
        body, out_shape=tuple(jax.ShapeDtypeStruct((g.shape[0],) + g.shape[2:], g.dtype) for g in grads),
        in_specs=[ANY] * n, out_specs=tuple([ANY] * n),
        scratch_shapes=[pltpu.SemaphoreType.DMA((n,)), pltpu.SemaphoreType.DMA((n,))],
        name=name, compiler_params=pltpu.CompilerParams(has_side_effects=True))(*grads)


def _scatter_chips(parts, *, name):
    n = len(parts)

    def body(*refs):
        ins, outs = refs[:n], refs[n:2 * n]
        send, recv, local = refs[2 * n:]
        x, y, c, me, others = _place()
        locals_, sends = [], []
        for a in range(n):
            cp = pltpu.make_async_copy(ins[a].at[me], outs[a].at[me], local.at[a])
            cp.start()
            locals_.append(cp)
            for k, (ox, oy) in enumerate(others):
                cp = pltpu.make_async_remote_copy(
                    src_ref=ins[a].at[2 * ox + oy], dst_ref=outs[a].at[me], send_sem=send.at[3 * a + k],
                    recv_sem=recv.at[3 * a + k], device_id=(ox, oy, c), device_id_type=MESH)
                cp.start()
                sends.append(cp)
        for a in range(n):
            for k, (ox, oy) in enumerate(others):
                slot = outs[a].at[2 * ox + oy]
                pltpu.make_async_remote_copy(src_ref=slot, dst_ref=slot, send_sem=send.at[3 * a + k],
                                             recv_sem=recv.at[3 * a + k], device_id=(ox, oy, c),
                                             device_id_type=MESH).wait_recv()
        for cp in sends:
            cp.wait_send()
        for cp in locals_:
            cp.wait()

    return pl.pallas_call(
        body, out_shape=tuple(jax.ShapeDtypeStruct(p.shape, p.dtype) for p in parts),
        in_specs=[ANY] * n, out_specs=tuple([ANY] * n),
        scratch_shapes=[pltpu.SemaphoreType.DMA((3 * n,)), pltpu.SemaphoreType.DMA((3 * n,)),
                        pltpu.SemaphoreType.DMA((n,))],
        name=name, compiler_params=pltpu.CompilerParams(has_side_effects=True))(*parts)


def _join_halves(halves, *, name):
    n = len(halves)

    def body(*refs):
        ins, outs = refs[:n], refs[n:2 * n]
        send, recv, local = refs[2 * n:]
        x, y, c, me, others = _place()
        locals_, sends = [], []
        for a in range(n):
            cp = pltpu.make_async_copy(ins[a], outs[a].at[c], local.at[a])
            cp.start()
            locals_.append(cp)
            cp = pltpu.make_async_remote_copy(src_ref=ins[a], dst_ref=outs[a].at[c], send_sem=send.at[a],
                                              recv_sem=recv.at[a], device_id=(x, y, 1 - c), device_id_type=MESH)
            cp.start()
            sends.append(cp)
        for a in range(n):
            got = outs[a].at[1 - c]
            pltpu.make_async_remote_copy(src_ref=got, dst_ref=got, send_sem=send.at[a], recv_sem=recv.at[a],
                                         device_id=(x, y, 1 - c), device_id_type=MESH).wait_recv()
        for cp in sends:
            cp.wait_send()
        for cp in locals_:
            cp.wait()

    return pl.pallas_call(
        body, out_shape=tuple(jax.ShapeDtypeStruct((2,) + h.shape, h.dtype) for h in halves),
        in_specs=[ANY] * n, out_specs=tuple([ANY] * n),
        scratch_shapes=[pltpu.SemaphoreType.DMA((n,)), pltpu.SemaphoreType.DMA((n,)),
                        pltpu.SemaphoreType.DMA((n,))],
        name=name, compiler_params=pltpu.CompilerParams(has_side_effects=True))(*halves)


def _all_reduce_small(v, *, name):
    R = v.shape[0]

    flips = [(dx, dy, dc) for dx in range(2) for dy in range(2) for dc in range(2) if dx or dy or dc]

    def body(v_ref, o_ref, slots, send, recv):
        x, y, c, me, others = _place()
        mine = 2 * me + c
        slots[mine] = v_ref[...]

        def copy(k, slot):
            dx, dy, dc = flips[k]
            peer = (x + dx - 2 * x * dx, y + dy - 2 * y * dy, c + dc - 2 * c * dc)
            peer_slot = 4 * peer[0] + 2 * peer[1] + peer[2]
            return pltpu.make_async_remote_copy(
                src_ref=v_ref, dst_ref=slots.at[mine if slot == "mine" else peer_slot], send_sem=send.at[k],
                recv_sem=recv.at[k], device_id=peer, device_id_type=MESH)

        for k in range(7):
            copy(k, "mine").start()
        for k in range(7):
            copy(k, "peer").wait_recv()
        for k in range(7):
            copy(k, "mine").wait_send()
        acc = slots[0]
        for j in range(1, 8):
            acc = acc + slots[j]
        o_ref[...] = acc

    vm = pl.BlockSpec(memory_space=pltpu.VMEM)
    return pl.pallas_call(
        body, out_shape=jax.ShapeDtypeStruct(v.shape, F32), in_specs=[vm], out_specs=vm,
        scratch_shapes=[pltpu.VMEM((8, R, LANES), F32), pltpu.SemaphoreType.DMA((7,)),
                        pltpu.SemaphoreType.DMA((7,))],
        name=name, compiler_params=pltpu.CompilerParams(has_side_effects=True))(v)


def _rows(v, n_rows):
    v = v.reshape(-1).astype(F32)
    return jnp.pad(v, (0, n_rows * LANES - v.shape[0])).reshape(n_rows, LANES)


def _mlp_fwd(x_in, g, w_up, w_down, tag):
    h = _rms_fwd(x_in, g, name=f"{tag}_norm")
    u, a = _matmul(h, w_up, b_split=True, epilogue="sqrelu", name=f"{tag}_up")
    x_out = _matmul(a, w_down, epilogue="res", res=x_in, name=f"{tag}_down")
    return x_out, (h, u, a)


def _mlp_bwd(dy, x_in, g, w_up, w_down, saved, tag):
    h, u, a = saved
    dw_down = _matmul(a, dy, form="tn", name=f"{tag}_dwdown")
    du = _matmul(dy, w_down, form="nt", epilogue="sqrelu_bwd", u=u, out_dtype=BF16, name=f"{tag}_du")
    dw_up = _matmul(h, du, form="tn", out_split=True, name=f"{tag}_dwup")
    dh = _matmul(du, w_up, form="nt", b_split=True, name=f"{tag}_dh")
    dx, dg = _rms_bwd(x_in, g, dh, res=dy, name=f"{tag}_dnorm")
    return dx, dg, dw_up, dw_down


def kernel(x, positions, ln_mix_g, ln_mlp_g, sf_w_in, sf_b_f, fox_q_g, fox_k_g, sf_w_o, mla_w_down, mla_q_a_g, mla_kv_a_g, mla_w_uq, mla_w_ukv, mla_q_g, mla_k_g, mla_w_o, mlp_w_up, mlp_w_down, loss_target, m_ln_mix_g, m_ln_mlp_g, m_sf_w_in, m_sf_b_f, m_fox_q_g, m_fox_k_g, m_sf_w_o, m_mla_w_down, m_mla_q_a_g, m_mla_kv_a_g, m_mla_w_uq, m_mla_w_ukv, m_mla_q_g, m_mla_k_g, m_mla_w_o, m_mlp_w_up, m_mlp_w_down, v_ln_mix_g, v_ln_mlp_g, v_sf_w_in, v_sf_b_f, v_fox_q_g, v_fox_k_g, v_sf_w_o, v_mla_w_down, v_mla_q_a_g, v_mla_kv_a_g, v_mla_w_uq, v_mla_w_ukv, v_mla_q_g, v_mla_k_g, v_mla_w_o, v_mlp_w_up, v_mlp_w_down):
    S, D = x.shape[1], x.shape[2]
    xs, tgt, pos = x[0], loss_target[0], positions[0]
    xi, yi, ci = lax.axis_index("x"), lax.axis_index("y"), lax.axis_index("c")
    chip = 2 * xi + yi
    core = ci.astype(jnp.int32).reshape(1)
    d_ff = mlp_w_up.shape[2] * N_CHIPS
    in_w = sf_w_in.shape[2] * N_CHIPS
    qkv_w = 3 * N_SB * HEAD + 3 * N_FOX * HEAD
    dn_w = mla_w_down.shape[2]
    dn_pad = Q_RANK + KV_RANK + LANES

    bf = lambda w: w.astype(BF16)
    ag_in, ag_o0 = _all_gather([bf(sf_w_in[0]), bf(sf_w_o[0])], name="gather_mix0")
    ag_up0, ag_dw0 = _all_gather([bf(mlp_w_up[0]), bf(mlp_w_down[0])], name="gather_mlp0")
    ag_dn, ag_uq, ag_ukv, ag_o1 = _all_gather(
        [bf(mla_w_down[0]), bf(mla_w_uq[0]), bf(mla_w_ukv[0]), bf(mla_w_o[0])], name="gather_mix1")
    ag_up1, ag_dw1 = _all_gather([bf(mlp_w_up[1]), bf(mlp_w_down[1])], name="gather_mlp1")

    cols = lambda ag: ag.transpose(1, 0, 2).reshape(ag.shape[1], -1)
    rows = lambda ag: ag.reshape(-1, ag.shape[2])
    w_in_full = cols(ag_in)
    w_qkv = w_in_full[:, :qkv_w]
    w_f = jnp.pad(w_in_full[:, qkv_w:], ((0, 0), (0, LANES - (in_w - qkv_w))))
    w_o0 = rows(ag_o0)
    w_dn = jnp.pad(rows(ag_dn), ((0, 0), (0, dn_pad - dn_w)))
    w_uq = jnp.pad(cols(ag_uq).reshape(Q_RANK, N_MLA, QK_DIM), ((0, 0), (0, 0), (0, QK_PAD - QK_DIM)))
    w_uq = w_uq.reshape(Q_RANK, N_MLA * QK_PAD)
    w_ukv = cols(ag_ukv)
    w_o1 = rows(ag_o1)
    w_up = [ag_up0, ag_up1]
    w_dw = [rows(ag_dw0), rows(ag_dw1)]

    gain_blk = jnp.concatenate([mla_q_a_g, mla_kv_a_g], axis=0) * (ci == 0).astype(F32)
    placed = jnp.zeros((2, N_CHIPS, LANES), F32)
    placed = lax.dynamic_update_slice(placed, gain_blk[:, None, :], (0, chip, 0))
    gains = _all_reduce_small(placed.reshape(2 * N_CHIPS, LANES), name="gather_gains")
    q_a_full = gains[:N_CHIPS].reshape(Q_RANK)
    kv_a_full = gains[N_CHIPS:].reshape(KV_RANK)

    pad_gain = lambda g: jnp.pad(g.reshape(1, QK_DIM), ((0, 0), (0, QK_PAD - QK_DIM)))
    q_g_pad, k_g_pad = pad_gain(mla_q_g), pad_gain(mla_k_g)
    b_pad = _rows(sf_b_f, 1)

    h0 = _rms_fwd(xs, ln_mix_g[0], name="mix0_norm")
    qkv_sb = _matmul(h0, w_qkv, n=3 * N_SB * HEAD, b_n0=0, out_dtype=BF16, name="mix0_qkv_sb")
    qk_fx = _matmul(h0, w_qkv, n=2 * N_FOX * HEAD, b_n0=3 * N_SB * HEAD, name="mix0_qk_fox")
    v_fx = _matmul(h0, w_qkv, n=N_FOX * HEAD, b_n0=(3 * N_SB + 2 * N_FOX) * HEAD, out_dtype=BF16,
                   name="mix0_v_fox")
    fl = _matmul(h0, w_f, name="mix0_forget_logit")
    f_cum = _forget_fwd(fl, b_pad, name="forget_fwd")
    neg_f = (-f_cum[:, :N_FOX]).T.reshape(N_FOX, 1, S)
    q_f = _rms_fwd(qk_fx, fox_q_g[0], c0=0, width=N_FOX * HEAD, gw=HEAD, name="fox_q_norm")
    k_f = _rms_fwd(qk_fx, fox_k_g[0], c0=N_FOX * HEAD, width=N_FOX * HEAD, gw=HEAD, name="fox_k_norm")
    o_sb, t_sb = _sb_fwd(qkv_sb, name="sb_fwd")
    o_fx, lse0 = _attn_fwd(q_f, k_f, v_fx, neg_f, n_heads=N_FOX, dqk=HEAD, scale=HEAD ** -0.5, exact_p=True,
                           name="fox_fwd")
    o0 = jnp.concatenate([o_sb, o_fx], axis=1)
    x1 = _matmul(o0, w_o0, epilogue="res", res=xs, name="mix0_out")
    x2, mlp0 = _mlp_fwd(x1, ln_mlp_g[0], w_up[0], w_dw[0], "mlp0")

    h2 = _rms_fwd(x2, ln_mix_g[1], name="mix1_norm")
    down = _matmul(h2, w_dn, name="mix1_down")
    c_q = _rms_fwd(down, q_a_full, c0=0, width=Q_RANK, name="mix1_q_a_norm")
    c_kv = _rms_fwd(down, kv_a_full, c0=Q_RANK, width=KV_RANK, name="mix1_kv_a_norm")
    q_raw = _matmul(c_q, w_uq, name="mix1_uq")
    kv_raw = _matmul(c_kv, w_ukv, name="mix1_ukv")
    half = ROPE // 2
    inv_freq = ROPE_THETA ** (-jnp.arange(half, dtype=F32) / half)
    ang = pos.astype(F32)[:, None] * inv_freq
    table = lambda t: jnp.pad(jnp.concatenate([t, t], axis=1), ((0, 0), (0, LANES - ROPE)))
    cos_t, sin_t = table(jnp.cos(ang)), table(jnp.sin(ang))
    q_pad, k_pad, v1 = _mla_prep_fwd(q_raw, kv_raw, down, cos_t, sin_t, q_g_pad, k_g_pad, name="mla_prep_fwd")
    o1, lse1 = _attn_fwd(q_pad, k_pad, v1, None, n_heads=N_MLA, dqk=QK_PAD, scale=QK_DIM ** -0.5, name="mla_fwd")
    x3 = _matmul(o1, w_o1, epilogue="res", res=x2, name="mix1_out")
    x4, mlp1 = _mlp_fwd(x3, ln_mlp_g[1], w_up[1], w_dw[1], "mlp1")

    dx4, loss_local = _loss_head(x4, tgt, name="loss_head")
    loss = lax.psum(loss_local, ("x", "y", "c"))

    dx3, dg_mlp1, dw_up1, dw_dw1 = _mlp_bwd(dx4, x3, ln_mlp_g[1], w_up[1], w_dw[1], mlp1, "mlp1")

    dw_o1 = _matmul(o1, dx3, form="tn", name="mix1_dwo")
    do1 = _matmul(dx3, w_o1, form="nt", name="mix1_do")
    dq_pad, dk_pad, dv1 = _attn_bwd(q_pad, k_pad, v1, None, o1, lse1, do1, n_heads=N_MLA, dqk=QK_PAD,
                                    scale=QK_DIM ** -0.5, name="mla_bwd")
    dq_raw, dkv_raw, dpe, dg_q, dg_k = _mla_prep_bwd(dq_pad, dk_pad, dv1, q_raw, kv_raw, down, cos_t, sin_t,
                                                     q_g_pad, k_g_pad, name="mla_prep_bwd")
    dw_uq = _matmul(c_q, dq_raw, form="tn", name="mix1_dwuq")
    dc_q = _matmul(dq_raw, w_uq, form="nt", name="mix1_dcq")
    dw_ukv = _matmul(c_kv, dkv_raw, form="tn", name="mix1_dwukv")
    dc_kv = _matmul(dkv_raw, w_ukv, form="nt", name="mix1_dckv")
    d_cq, dg_qa = _rms_bwd(down, q_a_full, dc_q, c0=0, width=Q_RANK, name="mix1_q_a_dnorm")
    d_ckv, dg_kva = _rms_bwd(down, kv_a_full, dc_kv, c0=Q_RANK, width=KV_RANK, name="mix1_kv_a_dnorm")
    d_down = jnp.concatenate([d_cq, d_ckv, dpe], axis=1)
    dw_dn = _matmul(h2, d_down, form="tn", name="mix1_dwdown")
    dh2 = _matmul(d_down, w_dn, form="nt", name="mix1_dh")
    dx2, dg_mix1 = _rms_bwd(x2, ln_mix_g[1], dh2, res=dx3, name="mix1_dnorm")

    dx1, dg_mlp0, dw_up0, dw_dw0 = _mlp_bwd(dx2, x1, ln_mlp_g[0], w_up[0], w_dw[0], mlp0, "mlp0")

    dw_o0 = _matmul(o0, dx1, form="tn", name="mix0_dwo")
    do0 = _matmul(dx1, w_o0, form="nt", name="mix0_do")
    dq_sb, dk_sb, dv_sb = _sb_bwd(qkv_sb, do0, t_sb, do_off=0, name="sb_bwd")
    dq_f, dk_f, dv_fx, dbias, drow = _attn_bwd(q_f, k_f, v_fx, neg_f, o_fx, lse0, do0, n_heads=N_FOX, dqk=HEAD,
                                               scale=HEAD ** -0.5, do_off=N_SB, name="fox_bwd")
    dq_fx, dg_fq = _rms_bwd(qk_fx, fox_q_g[0], dq_f, c0=0, width=N_FOX * HEAD, gw=HEAD, name="fox_q_dnorm")
    dk_fx, dg_fk = _rms_bwd(qk_fx, fox_k_g[0], dk_f, c0=N_FOX * HEAD, width=N_FOX * HEAD, gw=HEAD,
                            name="fox_k_dnorm")
    d_fcum = jnp.pad((drow[:, :, 0] - dbias.reshape(N_FOX, S)).T, ((0, 0), (0, LANES - N_FOX)))
    dfl, db_f = _forget_bwd(fl, b_pad, d_fcum, name="forget_bwd")
    dproj = jnp.concatenate([dq_sb, dk_sb, dv_sb, dq_fx, dk_fx, dv_fx], axis=1).astype(BF16)
    dw_qkv = _matmul(h0, dproj, form="tn", name="mix0_dwqkv")
    dw_f = _matmul(h0, dfl, form="tn", name="mix0_dwf")
    dh0 = _matmul(dfl, w_f, form="nt", name="mix0_dh_f")
    dh0 = _matmul(dproj, w_qkv, form="nt", epilogue="res", res=dh0, name="mix0_dh")
    grad_x, dg_mix0 = _rms_bwd(xs, ln_mix_g[0], dh0, res=dx1, name="mix0_dnorm")

    by_cols = lambda g: g.reshape(g.shape[0], N_CHIPS, -1).transpose(1, 0, 2)
    by_rows = lambda g: g.reshape(N_CHIPS, g.shape[0] // N_CHIPS, g.shape[1])
    halves = lambda g: g.reshape(N_CHIPS, 2, g.shape[1] // 2, g.shape[2])
    g_in = jnp.concatenate([dw_qkv, dw_f[:, :in_w - qkv_w]], axis=1)
    g_uq = dw_uq.reshape(Q_RANK, N_MLA, QK_PAD)[:, :, :QK_DIM].reshape(Q_RANK, N_MLA * QK_DIM)
    grads = [halves(by_cols(g_in)), halves(by_rows(dw_o0)), halves(by_rows(dw_dn[:, :dn_w])),
             halves(by_cols(g_uq)), halves(by_cols(dw_ukv)), halves(by_rows(dw_o1)), halves(dw_up0),
             halves(dw_up1), halves(by_rows(dw_dw0)), halves(by_rows(dw_dw1))]
    tags = ["w_in", "w_o0", "w_dn", "w_uq", "w_ukv", "w_o1", "w_up0", "w_up1", "w_dw0", "w_dw1"]
    from_sibling = _swap_halves(grads, name="reduce_sibling")
    parts = [_add_sibling(g, r, core, name=f"add_sibling_{t}") for g, r, t in zip(grads, from_sibling, tags)]
    slots = _scatter_chips(parts, name="reduce_chips")
    mine = [_add_chips(s, name=f"add_chips_{t}") for s, t in zip(slots, tags)]
    joined = _join_halves(mine, name="share_sibling")
    shard = lambda j: j.reshape(2 * j.shape[1], j.shape[2])
    (gs_in, gs_o0, gs_dn, gs_uq, gs_ukv, gs_o1, gs_up0, gs_up1, gs_dw0, gs_dw1) = [shard(j) for j in joined]
    gs_up = jnp.concatenate([gs_up0, gs_up1], axis=0)
    gs_dw = jnp.concatenate([gs_dw0, gs_dw1], axis=0)

    ln_rows = D // LANES
    small = jnp.concatenate([
        _rows(dg_mix0, ln_rows), _rows(dg_mix1, ln_rows), _rows(dg_mlp0, ln_rows), _rows(dg_mlp1, ln_rows),
        _rows(db_f, 8), _rows(dg_fq, 8), _rows(dg_fk, 8), _rows(dg_qa, 8), _rows(dg_kva, 8), _rows(dg_q, 8),
        _rows(dg_k, 8)], axis=0)
    small = _all_reduce_small(small, name="reduce_small")
    flat = lambda r0, nr, n: small[r0:r0 + nr].reshape(-1)[:n]
    r0 = 4 * ln_rows
    g_ln_mix = jnp.stack([flat(0, ln_rows, D), flat(ln_rows, ln_rows, D)])
    g_ln_mlp = jnp.stack([flat(2 * ln_rows, ln_rows, D), flat(3 * ln_rows, ln_rows, D)])
    g_b_f = flat(r0, 8, N_FOX)[None]
    g_fq, g_fk = flat(r0 + 8, 8, HEAD)[None], flat(r0 + 16, 8, HEAD)[None]
    g_qa = lax.dynamic_slice(flat(r0 + 24, 8, Q_RANK), (chip * LANES,), (LANES,))[None]
    g_kva = lax.dynamic_slice(flat(r0 + 32, 8, KV_RANK), (chip * LANES,), (LANES,))[None]
    g_q, g_k = flat(r0 + 40, 8, QK_DIM)[None], flat(r0 + 48, 8, QK_DIM)[None]

    def pack_small(ln_mix, ln_mlp, *rest):
        return jnp.concatenate([_rows(ln_mix, 2 * ln_rows), _rows(ln_mlp, 2 * ln_rows)] + [_rows(t, 8) for t in rest],
                               axis=0)

    def unpack_small(p):
        f = lambda r, nr, shape: p[r:r + nr].reshape(-1)[:int(np.prod(shape))].reshape(shape)
        shapes = [(1, N_FOX), (1, HEAD), (1, HEAD), (1, LANES), (1, LANES), (1, QK_DIM), (1, QK_DIM)]
        return (f(0, 2 * ln_rows, (2, D)), f(2 * ln_rows, 2 * ln_rows, (2, D)),
                *[f(r0 + 8 * i, 8, shp) for i, shp in enumerate(shapes)])

    small_out = _adamw(
        pack_small(ln_mix_g, ln_mlp_g, sf_b_f, fox_q_g, fox_k_g, mla_q_a_g, mla_kv_a_g, mla_q_g, mla_k_g),
        pack_small(g_ln_mix, g_ln_mlp, g_b_f, g_fq, g_fk, g_qa, g_kva, g_q, g_k),
        pack_small(m_ln_mix_g, m_ln_mlp_g, m_sf_b_f, m_fox_q_g, m_fox_k_g, m_mla_q_a_g, m_mla_kv_a_g, m_mla_q_g,
                   m_mla_k_g),
        pack_small(v_ln_mix_g, v_ln_mlp_g, v_sf_b_f, v_fox_q_g, v_fox_k_g, v_mla_q_a_g, v_mla_kv_a_g, v_mla_q_g,
                   v_mla_k_g), name="adamw_small")
    d_small, m_small, v_small = [unpack_small(p) for p in small_out]

    def big(w, g, m, v, tag):
        shp = w.shape
        two_d = lambda t: t.reshape(-1, shp[-1])
        d, mn, vn = _adamw(two_d(w), g, two_d(m), two_d(v), name=f"adamw_{tag}")
        return g.reshape(shp), d.reshape(shp), mn.reshape(shp), vn.reshape(shp)

    r_in = big(sf_w_in, gs_in, m_sf_w_in, v_sf_w_in, "w_in")
    r_o0 = big(sf_w_o, gs_o0, m_sf_w_o, v_sf_w_o, "w_o0")
    r_dn = big(mla_w_down, gs_dn, m_mla_w_down, v_mla_w_down, "w_dn")
    r_uq = big(mla_w_uq, gs_uq, m_mla_w_uq, v_mla_w_uq, "w_uq")
    r_ukv = big(mla_w_ukv, gs_ukv, m_mla_w_ukv, v_mla_w_ukv, "w_ukv")
    r_o1 = big(mla_w_o, gs_o1, m_mla_w_o, v_mla_w_o, "w_o1")
    r_up = big(mlp_w_up, gs_up, m_mlp_w_up, v_mlp_w_up, "w_up")
    r_dw = big(mlp_w_down, gs_dw, m_mlp_w_down, v_mlp_w_down, "w_dw")

    g_small = (g_ln_mix, g_ln_mlp, g_b_f, g_fq, g_fk, g_qa, g_kva, g_q, g_k)

    def ordered(k, sm):
        return (sm[0], sm[1], r_in[k], sm[2], sm[3], sm[4], r_o0[k], r_dn[k], sm[5], sm[6], r_uq[k], r_ukv[k],
                sm[7], sm[8], r_o1[k], r_up[k], r_dw[k])

    return (loss, grad_x[None], *ordered(0, g_small), *ordered(1, d_small), *ordered(2, m_small),
            *ordered(3, v_small))
```

```python
import functools

import numpy as np
import jax
import jax.numpy as jnp
from jax import lax
from jax.experimental import pallas as pl
from jax.experimental.pallas import tpu as pltpu

F32 = jnp.float32
BF16 = jnp.bfloat16
MESH = pl.DeviceIdType.MESH

EPS = 1e-6
HEAD = 128
N_SB = 8
N_FOX = 8
N_MLA = 16
Q_RANK = 512
KV_RANK = 512
NOPE = 128
ROPE = 64
QK_DIM = NOPE + ROPE
QK_PAD = 256
ROPE_THETA = 10000.0
N_CHIPS = 4

ADAM_LR = 0.001
ADAM_B1 = 0.9
ADAM_B2 = 0.999
ADAM_EPS = 1e-08
ADAM_WD = 0.01
ADAM_STEP = 10

VMEM_LIMIT = 56 * 1024 * 1024
LANES = 128
NEG = -1e30


def _cp(*sem):
    return pltpu.CompilerParams(dimension_semantics=sem, vmem_limit_bytes=VMEM_LIMIT)


def _pick(dim, target):
    if dim <= target:
        return dim
    t = (target // LANES) * LANES
    while t >= LANES:
        if dim % t == 0:
            return t
        t -= LANES
    raise ValueError(f"no tile for {dim}")


NT_DIMS = (((1,), (1,)), ((), ()))
TN_DIMS = (((0,), (0,)), ((), ()))


def _dot(a, b):
    return jnp.dot(a, b, preferred_element_type=F32)


def _dot_nt(a, b):
    return lax.dot_general(a, b, NT_DIMS, preferred_element_type=F32)


def _dot_tn(a, b):
    return lax.dot_general(a, b, TN_DIMS, preferred_element_type=F32)


def _matmul(a, b, *, name, form="nn", out_dtype=F32, n=None, b_n0=0, b_split=False,
            out_split=False, epilogue="plain", res=None, u=None, tm=1024, tn=1024, tk=2048):
    if form == "tn":
        K, M = a.shape
    else:
        M, K = a.shape
    if b_split:
        if form == "nt":
            nb_full, kb_full = b.shape[1], b.shape[2] * N_CHIPS
        else:
            kb_full, nb_full = b.shape[1], b.shape[2] * N_CHIPS
    elif form == "nt":
        nb_full, kb_full = b.shape
    else:
        kb_full, nb_full = b.shape
    assert kb_full == K, (name, a.shape, b.shape)
    N = nb_full if n is None else n
    if a.dtype != BF16 or b.dtype != BF16:
        tk = max(tk // 2, LANES)
    tm, tn, tk = _pick(M, tm), _pick(N, tn), _pick(K, tk)
    if b_split:
        per_chip = (b.shape[2])
        if form == "nt":
            tk = _pick(per_chip, tk)
        else:
            tn = _pick(per_chip, tn)
    if out_split:
        tn = _pick(N // N_CHIPS, tn)
    assert b_n0 % tn == 0
    nb0 = b_n0 // tn
    nk = K // tk
    grid = (M // tm, N // tn, nk)

    if form == "tn":
        a_spec = pl.BlockSpec((tk, tm), lambda i, j, k: (k, i))
    else:
        a_spec = pl.BlockSpec((tm, tk), lambda i, j, k: (i, k))
    if b_split:
        if form == "nt":
            kc = b.shape[2] // tk
            b_spec = pl.BlockSpec((None, tn, tk), lambda i, j, k: (k // kc, j, k % kc))
        else:
            nc = b.shape[2] // tn
            b_spec = pl.BlockSpec((None, tk, tn), lambda i, j, k: (j // nc, k, j % nc))
    elif form == "nt":
        b_spec = pl.BlockSpec((tn, tk), lambda i, j, k: (j + nb0, k))
    else:
        b_spec = pl.BlockSpec((tk, tn), lambda i, j, k: (k, j + nb0))
    mn_spec = pl.BlockSpec((tm, tn), lambda i, j, k: (i, j))
    if out_split:
        oc = (N // N_CHIPS) // tn
        out_spec = pl.BlockSpec((None, tm, tn), lambda i, j, k: (j // oc, i, j % oc))
        out_shape = jax.ShapeDtypeStruct((N_CHIPS, M, N // N_CHIPS), out_dtype)
    else:
        out_spec = mn_spec
        out_shape = jax.ShapeDtypeStruct((M, N), out_dtype)

    in_specs = [a_spec, b_spec]
    operands = [a, b]
    out_specs = out_spec
    if epilogue == "res":
        in_specs.append(mn_spec)
        operands.append(res)
    elif epilogue == "sqrelu_bwd":
        in_specs.append(mn_spec)
        operands.append(u)
    elif epilogue == "sqrelu":
        out_specs = (mn_spec, mn_spec)
        out_shape = (jax.ShapeDtypeStruct((M, N), F32), jax.ShapeDtypeStruct((M, N), BF16))

    def finish(refs, r):
        if epilogue == "plain":
            refs[2][...] = r.astype(out_dtype)
        elif epilogue == "res":
            refs[3][...] = (refs[2][...] + r).astype(out_dtype)
        elif epilogue == "sqrelu":
            refs[2][...] = r
            p = jnp.maximum(r, 0.0)
            refs[3][...] = (p * p).astype(BF16)
        else:
            refs[3][...] = (r * (2.0 * jnp.maximum(refs[2][...], 0.0))).astype(out_dtype)

    def body(*refs):
        at = refs[0][...].astype(BF16)
        bt = refs[1][...].astype(BF16)
        if form == "nn":
            part = _dot(at, bt)
        elif form == "nt":
            part = _dot_nt(at, bt)
        else:
            part = _dot_tn(at, bt)
        if nk == 1:
            finish(refs, part)
            return
        acc = refs[-1]
        k = pl.program_id(2)

        @pl.when(k == 0)
        def _():
            acc[...] = part

        @pl.when(jnp.logical_and(k > 0, k < nk - 1))
        def _():
            acc[...] += part

        @pl.when(k == nk - 1)
        def _():
            finish(refs, acc[...] + part)

    return pl.pallas_call(
        body, out_shape=out_shape, grid=grid, in_specs=in_specs, out_specs=out_specs,
        scratch_shapes=[] if nk == 1 else [pltpu.VMEM((tm, tn), F32)], name=name,
        compiler_params=_cp("parallel", "parallel", "arbitrary"))(*operands)


def _rms_fwd(x, g, *, name, c0=0, width=None, gw=None, tr=256):
    R, ctot = x.shape
    C = ctot if width is None else width
    gw = C if gw is None else gw
    assert c0 % C == 0 and C % gw == 0
    tr = _pick(R, tr)
    cb = c0 // C
    ng = C // gw

    def body(x_ref, g_ref, o_ref):
        gv = g_ref[...]
        for gi in range(ng):
            cols = slice(gi * gw, (gi + 1) * gw)
            xs = x_ref[:, cols]
            ms = jnp.sum(xs * xs, axis=-1, keepdims=True) * (1.0 / gw)
            o_ref[:, cols] = ((xs * lax.rsqrt(ms + EPS)) * gv).astype(o_ref.dtype)

    return pl.pallas_call(
        body, out_shape=jax.ShapeDtypeStruct((R, C), BF16), grid=(R // tr,),
        in_specs=[pl.BlockSpec((tr, C), lambda i: (i, cb)), pl.BlockSpec((1, gw), lambda i: (0, 0))],
        out_specs=pl.BlockSpec((tr, C), lambda i: (i, 0)), name=name,
        compiler_params=_cp("parallel"))(x, g.reshape(1, gw).astype(F32))


def _rms_bwd(x, g, dy, *, name, res=None, c0=0, width=None, gw=None, tr=256):
    R, ctot = x.shape
    C = ctot if width is None else width
    gw = C if gw is None else gw
    tr = _pick(R, tr)
    cb = c0 // C
    ng = C // gw
    nsteps = R // tr
    row_spec = pl.BlockSpec((tr, C), lambda i: (i, 0))
    in_specs = [pl.BlockSpec((tr, C), lambda i: (i, cb)), pl.BlockSpec((1, gw), lambda i: (0, 0)), row_spec]
    operands = [x, g.reshape(1, gw).astype(F32), dy]
    if res is not None:
        in_specs.append(row_spec)
        operands.append(res)

    def body(*refs):
        x_ref, g_ref, dy_ref = refs[:3]
        res_ref = refs[3] if res is not None else None
        dx_ref, dg_ref, acc = refs[-3:]
        i = pl.program_id(0)

        @pl.when(i == 0)
        def _():
            acc[...] = jnp.zeros_like(acc)

        gv = g_ref[...]
        for gi in range(ng):
            cols = slice(gi * gw, (gi + 1) * gw)
            xs = x_ref[:, cols]
            dys = dy_ref[:, cols].astype(F32)
            rstd = lax.rsqrt(jnp.sum(xs * xs, axis=-1, keepdims=True) * (1.0 / gw) + EPS)
            xh = xs * rstd
            gdy = dys * gv
            m = jnp.sum(gdy * xh, axis=-1, keepdims=True) * (1.0 / gw)
            dx = rstd * (gdy - xh * m)
            if res_ref is not None:
                dx = dx + res_ref[:, cols]
            dx_ref[:, cols] = dx
            acc[...] += jnp.sum((dys * xh).reshape(tr // 8, 8, gw), axis=0)

        @pl.when(i == nsteps - 1)
        def _():
            dg_ref[...] = jnp.sum(acc[...], axis=0, keepdims=True)

    dx, dg = pl.pallas_call(
        body, out_shape=(jax.ShapeDtypeStruct((R, C), F32), jax.ShapeDtypeStruct((1, gw), F32)),
        grid=(nsteps,), in_specs=in_specs,
        out_specs=(row_spec, pl.BlockSpec((1, gw), lambda i: (0, 0))),
        scratch_shapes=[pltpu.VMEM((8, gw), F32)], name=name,
        compiler_params=_cp("arbitrary"))(*operands)
    return dx, dg[0]


def _split3(x):
    hi = x.astype(BF16)
    r1 = x - hi.astype(F32)
    mid = r1.astype(BF16)
    lo = (r1 - mid.astype(F32)).astype(BF16)
    return hi, mid, lo


def _log_sigmoid(z):
    return jnp.minimum(z, 0.0) - jnp.log(1.0 + jnp.exp(-jnp.abs(z)))


def _forget_fwd(fl, b, *, name, tb=512):
    S = fl.shape[0]
    tb = _pick(S, tb)

    def body(fl_ref, b_ref, f_ref, carry):
        i = pl.program_id(0)

        @pl.when(i == 0)
        def _():
            carry[...] = jnp.zeros_like(carry)

        lf = _log_sigmoid(fl_ref[...] + b_ref[...])
        r = lax.broadcasted_iota(jnp.int32, (tb, tb), 0)
        c = lax.broadcasted_iota(jnp.int32, (tb, tb), 1)
        tri = (c <= r).astype(BF16)
        hi, mid, lo = _split3(lf)
        cs = _dot(tri, hi) + _dot(tri, mid) + _dot(tri, lo)
        f_ref[...] = cs + carry[...]
        carry[...] += jnp.sum(lf, axis=0, keepdims=True)

    return pl.pallas_call(
        body, out_shape=jax.ShapeDtypeStruct((S, LANES), F32), grid=(S // tb,),
        in_specs=[pl.BlockSpec((tb, LANES), lambda i: (i, 0)), pl.BlockSpec((1, LANES), lambda i: (0, 0))],
        out_specs=pl.BlockSpec((tb, LANES), lambda i: (i, 0)),
        scratch_shapes=[pltpu.VMEM((1, LANES), F32)], name=name,
        compiler_params=_cp("arbitrary"))(fl, b)


def _forget_bwd(fl, b, dF, *, name, tb=512):
    S = fl.shape[0]
    tb = _pick(S, tb)
    nb = S // tb

    def body(fl_ref, b_ref, df_ref, dfl_ref, db_ref, carry, acc):
        i = pl.program_id(0)

        @pl.when(i == 0)
        def _():
            carry[...] = jnp.zeros_like(carry)
            acc[...] = jnp.zeros_like(acc)

        d = df_ref[...]
        r = lax.broadcasted_iota(jnp.int32, (tb, tb), 0)
        c = lax.broadcasted_iota(jnp.int32, (tb, tb), 1)
        tri = (c >= r).astype(BF16)
        hi, mid, lo = _split3(d)
        rc = _dot(tri, hi) + _dot(tri, mid) + _dot(tri, lo) + carry[...]
        z = fl_ref[...] + b_ref[...]
        dfl = rc * jnp.exp(_log_sigmoid(-z))
        dfl_ref[...] = dfl
        carry[...] += jnp.sum(d, axis=0, keepdims=True)
        acc[...] += jnp.sum(dfl, axis=0, keepdims=True)

        @pl.when(i == nb - 1)
        def _():
            db_ref[...] = acc[...]

    rev = lambda i: (nb - 1 - i, 0)
    dfl, db = pl.pallas_call(
        body, out_shape=(jax.ShapeDtypeStruct((S, LANES), F32), jax.ShapeDtypeStruct((1, LANES), F32)),
        grid=(nb,),
        in_specs=[pl.BlockSpec((tb, LANES), rev), pl.BlockSpec((1, LANES), lambda i: (0, 0)),
                  pl.BlockSpec((tb, LANES), rev)],
        out_specs=(pl.BlockSpec((tb, LANES), rev), pl.BlockSpec((1, LANES), lambda i: (0, 0))),
        scratch_shapes=[pltpu.VMEM((1, LANES), F32), pltpu.VMEM((1, LANES), F32)], name=name,
        compiler_params=_cp("arbitrary"))(fl, b, dF)
    return dfl, db[0]


def _tri(tk, rel):
    r = lax.broadcasted_iota(jnp.int32, (tk, tk), 0)
    c = lax.broadcasted_iota(jnp.int32, (tk, tk), 1)
    m = {"gt": r > c, "le": r <= c, "lt": r < c}[rel]
    return m.astype(BF16)


def _split2(x):
    hi = x.astype(BF16)
    return hi, (x - hi.astype(F32)).astype(BF16)


HEADS_PER_STEP = 2


def _diag_mask(tq, strict):
    r = lax.broadcasted_iota(jnp.int32, (tq, tq), 0)
    c = lax.broadcasted_iota(jnp.int32, (tq, tq), 1)
    return c < r if strict else c <= r


def _sb_fwd(qkv, *, name, n_heads=N_SB, q_off=0, k_off=N_SB, v_off=2 * N_SB, tq=256, hp=HEADS_PER_STEP):
    S = qkv.shape[0]
    tq = _pick(S, tq)
    tk = tq
    scale = HEAD ** -0.5
    nq = S // tq
    assert n_heads % hp == 0 and q_off % hp == 0 and k_off % hp == 0 and v_off % hp == 0

    def body(q_ref, k_ref, v_ref, o_ref, t_ref, c_sc, acc_sc):
        qi = pl.program_id(1)
        c_sc[...] = jnp.zeros_like(c_sc)
        acc_sc[...] = jnp.zeros_like(acc_sc)
        gt = _tri(tk, "gt")

        def tile(hh, j, diag):
            cs = slice(hh * HEAD, (hh + 1) * HEAD)
            rows = pl.ds(pl.multiple_of(j * tk, tk), tk)
            z = _dot_nt(q_ref[:, cs], k_ref[rows, cs]) * scale
            sp = jnp.log(1.0 + jnp.exp(-jnp.abs(z)))
            la = jnp.minimum(z, 0.0) - sp
            lb = -jnp.maximum(z, 0.0) - sp
            if diag:
                strict = _diag_mask(tq, True)
                lb = jnp.where(strict, lb, 0.0)
            hi, lo = _split2(lb)
            suffix = _dot(hi, gt) + _dot(lo, gt)
            w = jnp.exp(la + suffix + c_sc[hh])
            if diag:
                w = jnp.where(strict, w, 0.0)
            acc_sc[hh] += _dot(w.astype(BF16), v_ref[rows, cs])
            c_sc[hh] += jnp.sum(lb, axis=1, keepdims=True)

        for hh in range(hp):
            tile(hh, qi, True)

        def step(it, carry):
            for hh in range(hp):
                tile(hh, qi - 1 - it, False)
            return carry

        lax.fori_loop(0, qi, step, 0)
        for hh in range(hp):
            o_ref[:, hh * HEAD:(hh + 1) * HEAD] = acc_sc[hh]
            t_ref[hh] = jnp.broadcast_to(c_sc[hh], (tq, LANES))

    w = hp * HEAD
    head_blk = lambda off: pl.BlockSpec((S, w), lambda h, i: (0, h + off // hp))
    return pl.pallas_call(
        body,
        out_shape=(jax.ShapeDtypeStruct((S, n_heads * HEAD), F32),
                   jax.ShapeDtypeStruct((n_heads, S, LANES), F32)),
        grid=(n_heads // hp, nq),
        in_specs=[pl.BlockSpec((tq, w), lambda h, i: (i, h + q_off // hp)), head_blk(k_off), head_blk(v_off)],
        out_specs=(pl.BlockSpec((tq, w), lambda h, i: (i, h)),
                   pl.BlockSpec((hp, tq, LANES), lambda h, i: (h, i, 0))),
        scratch_shapes=[pltpu.VMEM((hp, tq, 1), F32), pltpu.VMEM((hp, tq, HEAD), F32)], name=name,
        compiler_params=_cp("parallel", "arbitrary"))(qkv, qkv, qkv)


def _sb_bwd(qkv, do, tstat, *, name, n_heads=N_SB, q_off=0, k_off=N_SB, v_off=2 * N_SB, do_off=0, tq=256,
            hp=HEADS_PER_STEP):
    S = qkv.shape[0]
    tq = _pick(S, tq)
    tk = tq
    scale = HEAD ** -0.5
    nq = S // tq
    assert n_heads % hp == 0 and q_off % hp == 0 and k_off % hp == 0 and v_off % hp == 0 and do_off % hp == 0

    def body(q_ref, k_ref, v_ref, do_ref, t_ref, dq_ref, dk_ref, dv_ref, p_sc, r_sc, dq_sc):
        qi = pl.program_id(1)

        @pl.when(qi == 0)
        def _():
            dk_ref[...] = jnp.zeros_like(dk_ref)
            dv_ref[...] = jnp.zeros_like(dv_ref)

        p_sc[...] = jnp.zeros_like(p_sc)
        r_sc[...] = jnp.zeros_like(r_sc)
        dq_sc[...] = jnp.zeros_like(dq_sc)
        le = _tri(tk, "le")
        lt = _tri(tk, "lt")

        def tile(hh, j, diag):
            cs = slice(hh * HEAD, (hh + 1) * HEAD)
            rows = pl.ds(pl.multiple_of(j * tk, tk), tk)
            q = q_ref[:, cs]
            do_b = do_ref[:, cs].astype(BF16)
            kb = k_ref[rows, cs]
            z = _dot_nt(q, kb) * scale
            sp = jnp.log(1.0 + jnp.exp(-jnp.abs(z)))
            la = jnp.minimum(z, 0.0) - sp
            lb = -jnp.maximum(z, 0.0) - sp
            if diag:
                strict = _diag_mask(tq, True)
                lb = jnp.where(strict, lb, 0.0)
            hi, lo = _split2(lb)
            prefix = _dot(hi, le) + _dot(lo, le) + p_sc[hh]
            w = jnp.exp(la + t_ref[hh, :, 0:1] - prefix)
            if diag:
                w = jnp.where(strict, w, 0.0)
            r = w * _dot_nt(do_b, v_ref[rows, cs])
            hi, lo = _split2(r)
            rex = _dot(hi, lt) + _dot(lo, lt) + r_sc[hh]
            beta = jnp.exp(la)
            dz = (r * (1.0 - beta) - rex * beta) * scale
            if diag:
                dz = jnp.where(strict, dz, 0.0)
            dzb = dz.astype(BF16)
            dq_sc[hh] += _dot(dzb, kb)
            dk_ref[rows, cs] += _dot_tn(dzb, q)
            dv_ref[rows, cs] += _dot_tn(w.astype(BF16), do_b)
            p_sc[hh] += jnp.sum(lb, axis=1, keepdims=True)
            r_sc[hh] += jnp.sum(r, axis=1, keepdims=True)

        def step(j, carry):
            for hh in range(hp):
                tile(hh, j, False)
            return carry

        lax.fori_loop(0, qi, step, 0)
        for hh in range(hp):
            tile(hh, qi, True)
            dq_ref[:, hh * HEAD:(hh + 1) * HEAD] = dq_sc[hh]

    w = hp * HEAD
    head_blk = lambda off: pl.BlockSpec((S, w), lambda h, i: (0, h + off // hp))
    out_head = pl.BlockSpec((S, w), lambda h, i: (0, h))
    out_sd = jax.ShapeDtypeStruct((S, n_heads * HEAD), F32)
    return pl.pallas_call(
        body, out_shape=(out_sd, out_sd, out_sd), grid=(n_heads // hp, nq),
        in_specs=[pl.BlockSpec((tq, w), lambda h, i: (i, h + q_off // hp)), head_blk(k_off), head_blk(v_off),
                  pl.BlockSpec((tq, w), lambda h, i: (i, h + do_off // hp)),
                  pl.BlockSpec((hp, tq, LANES), lambda h, i: (h, i, 0))],
        out_specs=(pl.BlockSpec((tq, w), lambda h, i: (i, h)), out_head, out_head),
        scratch_shapes=[pltpu.VMEM((hp, tq, 1), F32), pltpu.VMEM((hp, tq, 1), F32), pltpu.VMEM((hp, tq, HEAD), F32)],
        name=name, compiler_params=_cp("parallel", "arbitrary"))(qkv, qkv, qkv, do, tstat)


def _attn_fwd(q, k, v, bias, *, name, n_heads, dqk, scale, v_off=0, tq=512, exact_p=False, hp=HEADS_PER_STEP):
    S = q.shape[0]
    tq = _pick(S, tq)
    tk = tq
    nq = S // tq
    has_bias = bias is not None

    assert n_heads % hp == 0 and v_off % hp == 0

    def body(*refs):
        q_ref, k_ref, v_ref = refs[:3]
        b_ref = refs[3] if has_bias else None
        o_ref, lse_ref, m_sc, l_sc, acc_sc = refs[-5:]
        qi = pl.program_id(1)
        m_sc[...] = jnp.full_like(m_sc, NEG)
        l_sc[...] = jnp.zeros_like(l_sc)
        acc_sc[...] = jnp.zeros_like(acc_sc)

        def tile(hh, j, diag):
            rows = pl.ds(pl.multiple_of(j * tk, tk), tk)
            s = _dot_nt(q_ref[:, hh * dqk:(hh + 1) * dqk], k_ref[rows, hh * dqk:(hh + 1) * dqk]) * scale
            if has_bias:
                s = s + b_ref[hh, :, rows]
            if diag:
                s = jnp.where(_diag_mask(tq, False), s, NEG)
            m_old = m_sc[hh]
            m_new = jnp.maximum(m_old, jnp.max(s, axis=1, keepdims=True))
            alpha = jnp.exp(m_old - m_new)
            p = jnp.exp(s - m_new)
            l_sc[hh] = alpha * l_sc[hh] + jnp.sum(p, axis=1, keepdims=True)
            vb = v_ref[rows, hh * HEAD:(hh + 1) * HEAD]
            if exact_p:
                hi, lo = _split2(p)
                pv = _dot(hi, vb) + _dot(lo, vb)
            else:
                pv = _dot(p.astype(BF16), vb)
            acc_sc[hh] = alpha * acc_sc[hh] + pv
            m_sc[hh] = m_new

        def step(j, carry):
            for hh in range(hp):
                tile(hh, j, False)
            return carry

        lax.fori_loop(0, qi, step, 0)
        for hh in range(hp):
            tile(hh, qi, True)
            l = l_sc[hh]
            o_ref[:, hh * HEAD:(hh + 1) * HEAD] = acc_sc[hh] / l
            lse_ref[hh] = jnp.broadcast_to(m_sc[hh] + jnp.log(l), (tq, LANES))

    in_specs = [pl.BlockSpec((tq, hp * dqk), lambda h, i: (i, h)),
                pl.BlockSpec((S, hp * dqk), lambda h, i: (0, h)),
                pl.BlockSpec((S, hp * HEAD), lambda h, i: (0, h + v_off // hp))]
    operands = [q, k, v]
    if has_bias:
        in_specs.append(pl.BlockSpec((hp, 1, S), lambda h, i: (h, 0, 0)))
        operands.append(bias)
    return pl.pallas_call(
        body,
        out_shape=(jax.ShapeDtypeStruct((S, n_heads * HEAD), F32),
                   jax.ShapeDtypeStruct((n_heads, S, LANES), F32)),
        grid=(n_heads // hp, nq), in_specs=in_specs,
        out_specs=(pl.BlockSpec((tq, hp * HEAD), lambda h, i: (i, h)),
                   pl.BlockSpec((hp, tq, LANES), lambda h, i: (h, i, 0))),
        scratch_shapes=[pltpu.VMEM((hp, tq, 1), F32), pltpu.VMEM((hp, tq, 1), F32), pltpu.VMEM((hp, tq, HEAD), F32)],
        name=name, compiler_params=_cp("parallel", "arbitrary"))(*operands)


def _attn_bwd(q, k, v, bias, o, lse, do, *, name, n_heads, dqk, scale, v_off=0, do_off=0, tq=512,
              hp=HEADS_PER_STEP):
    S = q.shape[0]
    tq = _pick(S, tq)
    tk = tq
    nq = S // tq
    has_bias = bias is not None
    assert n_heads % hp == 0 and v_off % hp == 0 and do_off % hp == 0

    def body(*refs):
        q_ref, k_ref, v_ref, o_ref, lse_ref, do_ref = refs[:6]
        b_ref = refs[6] if has_bias else None
        n_out = 5 if has_bias else 3
        outs = refs[-(n_out + 3):-3]
        dq_ref, dk_ref, dv_ref = outs[:3]
        db_ref, dr_ref = (outs[3], outs[4]) if has_bias else (None, None)
        dq_sc, rs_sc, delta_sc = refs[-3:]
        qi = pl.program_id(1)

        @pl.when(qi == 0)
        def _():
            dk_ref[...] = jnp.zeros_like(dk_ref)
            dv_ref[...] = jnp.zeros_like(dv_ref)
            if has_bias:
                db_ref[...] = jnp.zeros_like(db_ref)

        dq_sc[...] = jnp.zeros_like(dq_sc)
        rs_sc[...] = jnp.zeros_like(rs_sc)
        for hh in range(hp):
            vs = slice(hh * HEAD, (hh + 1) * HEAD)
            do_r = do_ref[:, vs].astype(BF16).astype(F32)
            delta_sc[hh] = jnp.sum(do_r * o_ref[:, vs], axis=1, keepdims=True)

        def tile(hh, j, diag):
            qs = slice(hh * dqk, (hh + 1) * dqk)
            vs = slice(hh * HEAD, (hh + 1) * HEAD)
            rows = pl.ds(pl.multiple_of(j * tk, tk), tk)
            qb = q_ref[:, qs]
            do_b = do_ref[:, vs].astype(BF16)
            delta = delta_sc[hh]
            kb = k_ref[rows, qs]
            s = _dot_nt(qb, kb) * scale
            if has_bias:
                s = s + b_ref[hh, :, rows]
            p = jnp.exp(s - lse_ref[hh, :, 0:1])
            if diag:
                p = jnp.where(_diag_mask(tq, False), p, 0.0)
            ds = p * (_dot_nt(do_b, v_ref[rows, vs]) - delta)
            dsb = (ds * scale).astype(BF16)
            dq_sc[hh] += _dot(dsb, kb)
            dk_ref[rows, qs] += _dot_tn(dsb, qb)
            dv_ref[rows, vs] += _dot_tn(p.astype(BF16), do_b)
            if has_bias:
                db_ref[hh, :, rows] += jnp.sum(ds, axis=0, keepdims=True)
                rs_sc[hh] += jnp.sum(ds, axis=1, keepdims=True)

        def step(j, carry):
            for hh in range(hp):
                tile(hh, j, False)
            return carry

        lax.fori_loop(0, qi, step, 0)
        for hh in range(hp):
            tile(hh, qi, True)
            dq_ref[:, hh * dqk:(hh + 1) * dqk] = dq_sc[hh]
            if has_bias:
                dr_ref[hh] = jnp.broadcast_to(rs_sc[hh], (tq, LANES))

    stat = pl.BlockSpec((hp, tq, LANES), lambda h, i: (h, i, 0))
    in_specs = [pl.BlockSpec((tq, hp * dqk), lambda h, i: (i, h)),
                pl.BlockSpec((S, hp * dqk), lambda h, i: (0, h)),
                pl.BlockSpec((S, hp * HEAD), lambda h, i: (0, h + v_off // hp)),
                pl.BlockSpec((tq, hp * HEAD), lambda h, i: (i, h)),
                stat,
                pl.BlockSpec((tq, hp * HEAD), lambda h, i: (i, h + do_off // hp))]
    operands = [q, k, v, o, lse, do]
    out_shape = [jax.ShapeDtypeStruct((S, n_heads * dqk), F32), jax.ShapeDtypeStruct((S, n_heads * dqk), F32),
                 jax.ShapeDtypeStruct((S, n_heads * HEAD), F32)]
    out_specs = [pl.BlockSpec((tq, hp * dqk), lambda h, i: (i, h)), pl.BlockSpec((S, hp * dqk), lambda h, i: (0, h)),
                 pl.BlockSpec((S, hp * HEAD), lambda h, i: (0, h))]
    if has_bias:
        in_specs.append(pl.BlockSpec((hp, 1, S), lambda h, i: (h, 0, 0)))
        operands.append(bias)
        out_shape.append(jax.ShapeDtypeStruct((n_heads, 1, S), F32))
        out_specs.append(pl.BlockSpec((hp, 1, S), lambda h, i: (h, 0, 0)))
        out_shape.append(jax.ShapeDtypeStruct((n_heads, S, LANES), F32))
        out_specs.append(stat)
    return pl.pallas_call(
        body, out_shape=tuple(out_shape), grid=(n_heads // hp, nq), in_specs=in_specs, out_specs=tuple(out_specs),
        scratch_shapes=[pltpu.VMEM((hp, tq, dqk), F32), pltpu.VMEM((hp, tq, 1), F32),
                        pltpu.VMEM((hp, tq, 1), F32)], name=name,
        compiler_params=_cp("parallel", "arbitrary"))(*operands)


def _rot_half(y):
    lane = lax.broadcasted_iota(jnp.int32, y.shape, 1)
    up = pltpu.roll(y, 96, 1)
    down = pltpu.roll(y, 32, 1)
    return jnp.where(lane < 32, -up, jnp.where(lane < 64, down, 0.0))


def _mla_prep_fwd(q_raw, kv_raw, down, cos, sin, q_g, k_g, *, name, ts=128):
    S = q_raw.shape[0]
    ts = _pick(S, ts)
    pe_blk = Q_RANK // LANES + KV_RANK // LANES

    def norm_rope(x0, x1, g0, g1, c, s):
        ms = (jnp.sum(x0 * x0, axis=-1, keepdims=True) + jnp.sum(x1 * x1, axis=-1, keepdims=True)) * (1.0 / QK_DIM)
        rstd = lax.rsqrt(ms + EPS)
        y0 = (x0 * rstd) * g0
        y1 = (x1 * rstd) * g1
        return y0, y1 * c + _rot_half(y1) * s

    def body(q_ref, kv_ref, pe_ref, cos_ref, sin_ref, qg_ref, kg_ref, qo_ref, ko_ref, vo_ref):
        c, s = cos_ref[...], sin_ref[...]
        pe = pe_ref[...]
        qg0, qg1 = qg_ref[:, :NOPE], qg_ref[:, NOPE:]
        kg0, kg1 = kg_ref[:, :NOPE], kg_ref[:, NOPE:]
        for h in range(N_MLA):
            b = h * QK_PAD
            y0, y1 = norm_rope(q_ref[:, b:b + NOPE], q_ref[:, b + NOPE:b + QK_PAD], qg0, qg1, c, s)
            qo_ref[:, b:b + NOPE] = y0.astype(BF16)
            qo_ref[:, b + NOPE:b + QK_PAD] = y1.astype(BF16)
            y0, y1 = norm_rope(kv_ref[:, b:b + NOPE], pe, kg0, kg1, c, s)
            ko_ref[:, b:b + NOPE] = y0.astype(BF16)
            ko_ref[:, b + NOPE:b + QK_PAD] = y1.astype(BF16)
            vo_ref[:, h * HEAD:(h + 1) * HEAD] = kv_ref[:, b + NOPE:b + QK_PAD].astype(BF16)

    wide = pl.BlockSpec((ts, N_MLA * QK_PAD), lambda i: (i, 0))
    lane_blk = pl.BlockSpec((ts, LANES), lambda i: (i, 0))
    gain = pl.BlockSpec((1, QK_PAD), lambda i: (0, 0))
    return pl.pallas_call(
        body,
        out_shape=(jax.ShapeDtypeStruct((S, N_MLA * QK_PAD), BF16), jax.ShapeDtypeStruct((S, N_MLA * QK_PAD), BF16),
                   jax.ShapeDtypeStruct((S, N_MLA * HEAD), BF16)),
        grid=(S // ts,),
        in_specs=[wide, wide, pl.BlockSpec((ts, LANES), lambda i: (i, pe_blk)), lane_blk, lane_blk, gain, gain],
        out_specs=(wide, wide, pl.BlockSpec((ts, N_MLA * HEAD), lambda i: (i, 0))), name=name,
        compiler_params=_cp("parallel"))(q_raw, kv_raw, down, cos, sin, q_g, k_g)


def _mla_prep_bwd(dq, dk, dv, q_raw, kv_raw, down, cos, sin, q_g, k_g, *, name, ts=128):
    S = q_raw.shape[0]
    ts = _pick(S, ts)
    nsteps = S // ts
    pe_blk = Q_RANK // LANES + KV_RANK // LANES

    def back(x0, x1, g0, g1, c, s, d0, d1r):
        d1 = d1r * c - _rot_half(d1r * s)
        ms = (jnp.sum(x0 * x0, axis=-1, keepdims=True) + jnp.sum(x1 * x1, axis=-1, keepdims=True)) * (1.0 / QK_DIM)
        rstd = lax.rsqrt(ms + EPS)
        h0, h1 = x0 * rstd, x1 * rstd
        e0, e1 = d0 * g0, d1 * g1
        m = (jnp.sum(e0 * h0, axis=-1, keepdims=True) + jnp.sum(e1 * h1, axis=-1, keepdims=True)) * (1.0 / QK_DIM)
        return rstd * (e0 - h0 * m), rstd * (e1 - h1 * m), d0 * h0, d1 * h1

    def fold(a):
        return jnp.sum(a.reshape(ts // 8, 8, a.shape[-1]), axis=0)

    def body(dq_ref, dk_ref, dv_ref, q_ref, kv_ref, pe_ref, cos_ref, sin_ref, qg_ref, kg_ref,
             dqr_ref, dkv_ref, dpe_ref, dqg_ref, dkg_ref, gq_sc, gk_sc):
        i = pl.program_id(0)

        @pl.when(i == 0)
        def _():
            gq_sc[...] = jnp.zeros_like(gq_sc)
            gk_sc[...] = jnp.zeros_like(gk_sc)

        c, s = cos_ref[...], sin_ref[...]
        pe = pe_ref[...]
        qg0, qg1 = qg_ref[:, :NOPE], qg_ref[:, NOPE:]
        kg0, kg1 = kg_ref[:, :NOPE], kg_ref[:, NOPE:]
        dpe = jnp.zeros((ts, LANES), F32)
        for h in range(N_MLA):
            b = h * QK_PAD
            dx0, dx1, a0, a1 = back(q_ref[:, b:b + NOPE], q_ref[:, b + NOPE:b + QK_PAD], qg0, qg1, c, s,
                                    dq_ref[:, b:b + NOPE], dq_ref[:, b + NOPE:b + QK_PAD])
            dqr_ref[:, b:b + NOPE] = dx0.astype(BF16)
            dqr_ref[:, b + NOPE:b + QK_PAD] = dx1.astype(BF16)
            gq_sc[:, :NOPE] += fold(a0)
            gq_sc[:, NOPE:] += fold(a1)
            dx0, dx1, a0, a1 = back(kv_ref[:, b:b + NOPE], pe, kg0, kg1, c, s,
                                    dk_ref[:, b:b + NOPE], dk_ref[:, b + NOPE:b + QK_PAD])
            dkv_ref[:, b:b + NOPE] = dx0.astype(BF16)
            dkv_ref[:, b + NOPE:b + QK_PAD] = dv_ref[:, h * HEAD:(h + 1) * HEAD].astype(BF16)
            dpe = dpe + dx1
            gk_sc[:, :NOPE] += fold(a0)
            gk_sc[:, NOPE:] += fold(a1)
        dpe_ref[...] = dpe

        @pl.when(i == nsteps - 1)
        def _():
            dqg_ref[...] = jnp.sum(gq_sc[...], axis=0, keepdims=True)
            dkg_ref[...] = jnp.sum(gk_sc[...], axis=0, keepdims=True)

    wide = pl.BlockSpec((ts, N_MLA * QK_PAD), lambda i: (i, 0))
    lane_blk = pl.BlockSpec((ts, LANES), lambda i: (i, 0))
    gain = pl.BlockSpec((1, QK_PAD), lambda i: (0, 0))
    outs = pl.pallas_call(
        body,
        out_shape=(jax.ShapeDtypeStruct((S, N_MLA * QK_PAD), BF16), jax.ShapeDtypeStruct((S, N_MLA * QK_PAD), BF16),
                   jax.ShapeDtypeStruct((S, LANES), F32), jax.ShapeDtypeStruct((1, QK_PAD), F32),
                   jax.ShapeDtypeStruct((1, QK_PAD), F32)),
        grid=(nsteps,),
        in_specs=[wide, wide, pl.BlockSpec((ts, N_MLA * HEAD), lambda i: (i, 0)), wide, wide,
                  pl.BlockSpec((ts, LANES), lambda i: (i, pe_blk)), lane_blk, lane_blk, gain, gain],
        out_specs=(wide, wide, lane_blk, gain, gain),
        scratch_shapes=[pltpu.VMEM((8, QK_PAD), F32), pltpu.VMEM((8, QK_PAD), F32)], name=name,
        compiler_params=_cp("arbitrary"))(dq, dk, dv, q_raw, kv_raw, down, cos, sin, q_g, k_g)
    return outs[0], outs[1], outs[2], outs[3][0], outs[4][0]


def _loss_head(y, target, *, name, tr=256):
    R, C = y.shape
    tr = _pick(R, tr)
    nsteps = R // tr

    def body(y_ref, t_ref, dy_ref, loss_ref, acc):
        i = pl.program_id(0)

        @pl.when(i == 0)
        def _():
            acc[...] = jnp.zeros_like(acc)

        err = y_ref[...] - t_ref[...]
        dy_ref[...] = err * (1.0 / C)
        acc[...] += jnp.sum((err * err).reshape(tr // 8, 8, C), axis=0)

        @pl.when(i == nsteps - 1)
        def _():
            tot = jnp.sum(jnp.sum(acc[...], axis=0, keepdims=True), axis=1, keepdims=True)
            loss_ref[...] = jnp.broadcast_to(tot * (0.5 / C), (8, LANES))

    blk = pl.BlockSpec((tr, C), lambda i: (i, 0))
    dy, loss = pl.pallas_call(
        body, out_shape=(jax.ShapeDtypeStruct((R, C), F32), jax.ShapeDtypeStruct((8, LANES), F32)),
        grid=(nsteps,), in_specs=[blk, blk], out_specs=(blk, pl.BlockSpec((8, LANES), lambda i: (0, 0))),
        scratch_shapes=[pltpu.VMEM((8, C), F32)], name=name, compiler_params=_cp("arbitrary"))(y, target)
    return dy, loss[0, 0]


def _adamw(w, g, m, v, *, name, block_bytes=1 << 20):
    R, C = w.shape
    tr = max(8, min(R, (block_bytes // (4 * C)) // 8 * 8))
    while R % tr:
        tr -= 8
    if tr <= 0:
        tr = R
    c1 = 1.0 / (1.0 - ADAM_B1 ** ADAM_STEP)
    c2 = 1.0 / (1.0 - ADAM_B2 ** ADAM_STEP)

    def body(w_ref, g_ref, m_ref, v_ref, d_ref, mo_ref, vo_ref):
        gv = g_ref[...]
        mn = ADAM_B1 * m_ref[...] + (1.0 - ADAM_B1) * gv
        vn = ADAM_B2 * v_ref[...] + (1.0 - ADAM_B2) * (gv * gv)
        d_ref[...] = -ADAM_LR * ((mn * c1) / (jnp.sqrt(vn * c2) + ADAM_EPS) + ADAM_WD * w_ref[...])
        mo_ref[...] = mn
        vo_ref[...] = vn

    blk = pl.BlockSpec((tr, C), lambda i: (i, 0))
    sd = jax.ShapeDtypeStruct((R, C), F32)
    return pl.pallas_call(
        body, out_shape=(sd, sd, sd), grid=(R // tr,), in_specs=[blk] * 4, out_specs=(blk,) * 3, name=name,
        compiler_params=_cp("parallel"))(w, g, m, v)


def _row_tile(r, c, itemsize=4, block_bytes=1 << 20):
    tr = max(16, min(r, (block_bytes // (itemsize * c)) // 16 * 16))
    while r % tr:
        tr -= 16
    return tr if tr > 0 else r


def _add_sibling(g, recv, core, *, name):
    nch, _, r, c = g.shape
    tr = _row_tile(r, c)

    def body(core_ref, g_ref, r_ref, o_ref):
        o_ref[...] = (g_ref[...] + r_ref[...]).astype(BF16)

    grid_spec = pltpu.PrefetchScalarGridSpec(
        num_scalar_prefetch=1, grid=(nch, r // tr),
        in_specs=[pl.BlockSpec((None, None, tr, c), lambda j, i, cr: (j, cr[0], i, 0)),
                  pl.BlockSpec((None, tr, c), lambda j, i, cr: (j, i, 0))],
        out_specs=pl.BlockSpec((None, tr, c), lambda j, i, cr: (j, i, 0)))
    return pl.pallas_call(
        body, out_shape=jax.ShapeDtypeStruct((nch, r, c), BF16), grid_spec=grid_spec, name=name,
        compiler_params=_cp("parallel", "parallel"))(core, g, recv)


def _add_chips(slots, *, name):
    nch, r, c = slots.shape
    tr = _row_tile(r, c)

    def body(s_ref, o_ref):
        acc = s_ref[0].astype(F32)
        for j in range(1, nch):
            acc = acc + s_ref[j].astype(F32)
        o_ref[...] = acc

    return pl.pallas_call(
        body, out_shape=jax.ShapeDtypeStruct((r, c), F32), grid=(r // tr,),
        in_specs=[pl.BlockSpec((nch, tr, c), lambda i: (0, i, 0))],
        out_specs=pl.BlockSpec((tr, c), lambda i: (i, 0)), name=name, compiler_params=_cp("parallel"))(slots)


def _place():
    x, y, c = lax.axis_index("x"), lax.axis_index("y"), lax.axis_index("c")
    others = [(1 - x, y), (x, 1 - y), (1 - x, 1 - y)]
    return x, y, c, 2 * x + y, others


ANY = pl.BlockSpec(memory_space=pl.ANY)


def _all_gather(shards, *, name):
    n = len(shards)

    def body(*refs):
        ins, outs = refs[:n], refs[n:2 * n]
        send, recv = refs[2 * n:]
        x, y, c, me, others = _place()
        sib = (x, y, 1 - c)

        def half(a, chip, core):
            hr = shards[a].shape[0] // 2
            return outs[a].at[chip, pl.ds(core * hr, hr), :]

        def remote(a, k, src, dst, to):
            return pltpu.make_async_remote_copy(src_ref=src, dst_ref=dst, send_sem=send.at[6 * a + k],
                                                recv_sem=recv.at[6 * a + k], device_id=to, device_id_type=MESH)

        sends = []
        for a in range(n):
            hr = shards[a].shape[0] // 2
            mine = ins[a].at[pl.ds(c * hr, hr), :]
            for k, (ox, oy) in enumerate(others):
                cp = remote(a, k, mine, half(a, me, c), (ox, oy, c))
                cp.start()
                sends.append(cp)
        for a in range(n):
            for k, (ox, oy) in enumerate(others):
                landed = half(a, 2 * ox + oy, c)
                remote(a, k, landed, landed, (ox, oy, c)).wait_recv()
                cp = remote(a, 3 + k, landed, landed, sib)
                cp.start()
                sends.append(cp)
        for a in range(n):
            for k, (ox, oy) in enumerate(others):
                got = half(a, 2 * ox + oy, 1 - c)
                remote(a, 3 + k, got, got, sib).wait_recv()
        for cp in sends:
            cp.wait_send()

    return pl.pallas_call(
        body, out_shape=tuple(jax.ShapeDtypeStruct((N_CHIPS,) + s.shape, s.dtype) for s in shards),
        in_specs=[ANY] * n, out_specs=tuple([ANY] * n),
        scratch_shapes=[pltpu.SemaphoreType.DMA((6 * n,)), pltpu.SemaphoreType.DMA((6 * n,))],
        name=name, compiler_params=pltpu.CompilerParams(has_side_effects=True))(*shards)


def _own_slot(buf, piece, idx):
    return lax.dynamic_update_slice(buf, piece[None], (idx,) + (0,) * piece.ndim)


def _swap_halves(grads, *, name):
    n = len(grads)

    def body(*refs):
        ins, outs = refs[:n], refs[n:2 * n]
        send, recv = refs[2 * n:]
        x, y, c, me, others = _place()
        cps = []
        for a in range(n):
            cp = pltpu.make_async_remote_copy(src_ref=ins[a].at[:, 1 - c], dst_ref=outs[a], send_sem=send.at[a],
                                              recv_sem=recv.at[a], device_id=(x, y, 1 - c), device_id_type=MESH)
            cp.start()
            cps.append(cp)
        for cp in cps:
            cp.wait()

    return pl.pallas_call(
        body, out_shape=tuple(jax.ShapeDtypeStruct((g.shape[0],) + g.shape[2:], g.dtype) for g in grads),
        in_specs=[ANY] * n, out_specs=tuple([ANY] * n),
        scratch_shapes=[pltpu.SemaphoreType.DMA((n,)), pltpu.SemaphoreType.DMA((n,))],
        name=name, compiler_params=pltpu.CompilerParams(has_side_effects=True))(*grads)


def _scatter_chips(parts, *, name):
    n = len(parts)

    def body(*refs):
        ins, outs = refs[:n], refs[n:2 * n]
        send, recv = refs[2 * n:]
        x, y, c, me, others = _place()
        sends = []
        for a in range(n):
            for k, (ox, oy) in enumerate(others):
                cp = pltpu.make_async_remote_copy(
                    src_ref=ins[a].at[2 * ox + oy], dst_ref=outs[a].at[me], send_sem=send.at[3 * a + k],
                    recv_sem=recv.at[3 * a + k], device_id=(ox, oy, c), device_id_type=MESH)
                cp.start()
                sends.append(cp)
        for a in range(n):
            for k, (ox, oy) in enumerate(others):
                slot = outs[a].at[2 * ox + oy]
                pltpu.make_async_remote_copy(src_ref=slot, dst_ref=slot, send_sem=send.at[3 * a + k],
                                             recv_sem=recv.at[3 * a + k], device_id=(ox, oy, c),
                                             device_id_type=MESH).wait_recv()
        for cp in sends:
            cp.wait_send()

    return pl.pallas_call(
        body, out_shape=tuple(jax.ShapeDtypeStruct(p.shape, p.dtype) for p in parts),
        in_specs=[ANY] * n, out_specs=tuple([ANY] * n),
        scratch_shapes=[pltpu.SemaphoreType.DMA((3 * n,)), pltpu.SemaphoreType.DMA((3 * n,))],
        name=name, compiler_params=pltpu.CompilerParams(has_side_effects=True))(*parts)


def _join_halves(halves, *, name):
    n = len(halves)

    def body(*refs):
        ins, outs = refs[:n], refs[n:2 * n]
        send, recv = refs[2 * n:]
        x, y, c, me, others = _place()
        sends = []
        for a in range(n):
            cp = pltpu.make_async_remote_copy(src_ref=ins[a], dst_ref=outs[a].at[c], send_sem=send.at[a],
                                              recv_sem=recv.at[a], device_id=(x, y, 1 - c), device_id_type=MESH)
            cp.start()
            sends.append(cp)
        for a in range(n):
            got = outs[a].at[1 - c]
            pltpu.make_async_remote_copy(src_ref=got, dst_ref=got, send_sem=send.at[a], recv_sem=recv.at[a],
                                         device_id=(x, y, 1 - c), device_id_type=MESH).wait_recv()
        for cp in sends:
            cp.wait_send()

    return pl.pallas_call(
        body, out_shape=tuple(jax.ShapeDtypeStruct((2,) + h.shape, h.dtype) for h in halves),
        in_specs=[ANY] * n, out_specs=tuple([ANY] * n),
        scratch_shapes=[pltpu.SemaphoreType.DMA((n,)), pltpu.SemaphoreType.DMA((n,))],
        name=name, compiler_params=pltpu.CompilerParams(has_side_effects=True))(*halves)


def _all_reduce_small(v, *, name):
    R = v.shape[0]

    flips = [(dx, dy, dc) for dx in range(2) for dy in range(2) for dc in range(2) if dx or dy or dc]

    def body(v_ref, o_ref, slots, send, recv):
        x, y, c, me, others = _place()
        mine = 2 * me + c
        slots[mine] = v_ref[...]

        def copy(k, slot):
            dx, dy, dc = flips[k]
            peer = (x + dx - 2 * x * dx, y + dy - 2 * y * dy, c + dc - 2 * c * dc)
            peer_slot = 4 * peer[0] + 2 * peer[1] + peer[2]
            return pltpu.make_async_remote_copy(
                src_ref=v_ref, dst_ref=slots.at[mine if slot == "mine" else peer_slot], send_sem=send.at[k],
                recv_sem=recv.at[k], device_id=peer, device_id_type=MESH)

        for k in range(7):
            copy(k, "mine").start()
        for k in range(7):
            copy(k, "peer").wait_recv()
        for k in range(7):
            copy(k, "mine").wait_send()
        acc = slots[0]
        for j in range(1, 8):
            acc = acc + slots[j]
        o_ref[...] = acc

    vm = pl.BlockSpec(memory_space=pltpu.VMEM)
    return pl.pallas_call(
        body, out_shape=jax.ShapeDtypeStruct(v.shape, F32), in_specs=[vm], out_specs=vm,
        scratch_shapes=[pltpu.VMEM((8, R, LANES), F32), pltpu.SemaphoreType.DMA((7,)),
                        pltpu.SemaphoreType.DMA((7,))],
        name=name, compiler_params=pltpu.CompilerParams(has_side_effects=True))(v)


def _rows(v, n_rows):
    v = v.reshape(-1).astype(F32)
    return jnp.pad(v, (0, n_rows * LANES - v.shape[0])).reshape(n_rows, LANES)


def _mlp_fwd(x_in, g, w_up, w_down, tag):
    h = _rms_fwd(x_in, g, name=f"{tag}_norm")
    u, a = _matmul(h, w_up, b_split=True, epilogue="sqrelu", name=f"{tag}_up")
    x_out = _matmul(a, w_down, epilogue="res", res=x_in, name=f"{tag}_down")
    return x_out, (h, u, a)


def _mlp_bwd(dy, x_in, g, w_up, w_down, saved, tag):
    h, u, a = saved
    dw_down = _matmul(a, dy, form="tn", name=f"{tag}_dwdown")
    du = _matmul(dy, w_down, form="nt", epilogue="sqrelu_bwd", u=u, out_dtype=BF16, name=f"{tag}_du")
    dw_up = _matmul(h, du, form="tn", out_split=True, name=f"{tag}_dwup")
    dh = _matmul(du, w_up, form="nt", b_split=True, name=f"{tag}_dh")
    dx, dg = _rms_bwd(x_in, g, dh, res=dy, name=f"{tag}_dnorm")
    return dx, dg, dw_up, dw_down


def kernel(x, positions, ln_mix_g, ln_mlp_g, sf_w_in, sf_b_f, fox_q_g, fox_k_g, sf_w_o, mla_w_down, mla_q_a_g, mla_kv_a_g, mla_w_uq, mla_w_ukv, mla_q_g, mla_k_g, mla_w_o, mlp_w_up, mlp_w_down, loss_target, m_ln_mix_g, m_ln_mlp_g, m_sf_w_in, m_sf_b_f, m_fox_q_g, m_fox_k_g, m_sf_w_o, m_mla_w_down, m_mla_q_a_g, m_mla_kv_a_g, m_mla_w_uq, m_mla_w_ukv, m_mla_q_g, m_mla_k_g, m_mla_w_o, m_mlp_w_up, m_mlp_w_down, v_ln_mix_g, v_ln_mlp_g, v_sf_w_in, v_sf_b_f, v_fox_q_g, v_fox_k_g, v_sf_w_o, v_mla_w_down, v_mla_q_a_g, v_mla_kv_a_g, v_mla_w_uq, v_mla_w_ukv, v_mla_q_g, v_mla_k_g, v_mla_w_o, v_mlp_w_up, v_mlp_w_down):
    S, D = x.shape[1], x.shape[2]
    xs, tgt, pos = x[0], loss_target[0], positions[0]
    xi, yi, ci = lax.axis_index("x"), lax.axis_index("y"), lax.axis_index("c")
    chip = 2 * xi + yi
    core = ci.astype(jnp.int32).reshape(1)
    d_ff = mlp_w_up.shape[2] * N_CHIPS
    in_w = sf_w_in.shape[2] * N_CHIPS
    qkv_w = 3 * N_SB * HEAD + 3 * N_FOX * HEAD
    dn_w = mla_w_down.shape[2]
    dn_pad = Q_RANK + KV_RANK + LANES

    def gather(ws, name):
        shards = [w.astype(BF16) for w in ws]
        return [_own_slot(ag, s, chip) for ag, s in zip(_all_gather(shards, name=name), shards)]

    ag_in, ag_o0 = gather([sf_w_in[0], sf_w_o[0]], "gather_mix0")
    ag_up0, ag_dw0 = gather([mlp_w_up[0], mlp_w_down[0]], "gather_mlp0")
    ag_dn, ag_uq, ag_ukv, ag_o1 = gather([mla_w_down[0], mla_w_uq[0], mla_w_ukv[0], mla_w_o[0]], "gather_mix1")
    ag_up1, ag_dw1 = gather([mlp_w_up[1], mlp_w_down[1]], "gather_mlp1")

    cols = lambda ag: ag.transpose(1, 0, 2).reshape(ag.shape[1], -1)
    rows = lambda ag: ag.reshape(-1, ag.shape[2])
    w_in_full = cols(ag_in)
    w_qkv = w_in_full[:, :qkv_w]
    w_f = jnp.pad(w_in_full[:, qkv_w:], ((0, 0), (0, LANES - (in_w - qkv_w))))
    w_o0 = rows(ag_o0)
    w_dn = jnp.pad(rows(ag_dn), ((0, 0), (0, dn_pad - dn_w)))
    w_uq = jnp.pad(cols(ag_uq).reshape(Q_RANK, N_MLA, QK_DIM), ((0, 0), (0, 0), (0, QK_PAD - QK_DIM)))
    w_uq = w_uq.reshape(Q_RANK, N_MLA * QK_PAD)
    w_ukv = cols(ag_ukv)
    w_o1 = rows(ag_o1)
    w_up = [ag_up0, ag_up1]
    w_dw = [rows(ag_dw0), rows(ag_dw1)]

    gain_blk = jnp.concatenate([mla_q_a_g, mla_kv_a_g], axis=0) * (ci == 0).astype(F32)
    placed = jnp.zeros((2, N_CHIPS, LANES), F32)
    placed = lax.dynamic_update_slice(placed, gain_blk[:, None, :], (0, chip, 0))
    gains = _all_reduce_small(placed.reshape(2 * N_CHIPS, LANES), name="gather_gains")
    q_a_full = gains[:N_CHIPS].reshape(Q_RANK)
    kv_a_full = gains[N_CHIPS:].reshape(KV_RANK)

    pad_gain = lambda g: jnp.pad(g.reshape(1, QK_DIM), ((0, 0), (0, QK_PAD - QK_DIM)))
    q_g_pad, k_g_pad = pad_gain(mla_q_g), pad_gain(mla_k_g)
    b_pad = _rows(sf_b_f, 1)

    h0 = _rms_fwd(xs, ln_mix_g[0], name="mix0_norm")
    qkv_sb = _matmul(h0, w_qkv, n=3 * N_SB * HEAD, b_n0=0, out_dtype=BF16, name="mix0_qkv_sb")
    qk_fx = _matmul(h0, w_qkv, n=2 * N_FOX * HEAD, b_n0=3 * N_SB * HEAD, name="mix0_qk_fox")
    v_fx = _matmul(h0, w_qkv, n=N_FOX * HEAD, b_n0=(3 * N_SB + 2 * N_FOX) * HEAD, out_dtype=BF16,
                   name="mix0_v_fox")
    fl = _matmul(h0, w_f, name="mix0_forget_logit")
    f_cum = _forget_fwd(fl, b_pad, name="forget_fwd")
    neg_f = (-f_cum[:, :N_FOX]).T.reshape(N_FOX, 1, S)
    q_f = _rms_fwd(qk_fx, fox_q_g[0], c0=0, width=N_FOX * HEAD, gw=HEAD, name="fox_q_norm")
    k_f = _rms_fwd(qk_fx, fox_k_g[0], c0=N_FOX * HEAD, width=N_FOX * HEAD, gw=HEAD, name="fox_k_norm")
    o_sb, t_sb = _sb_fwd(qkv_sb, name="sb_fwd")
    o_fx, lse0 = _attn_fwd(q_f, k_f, v_fx, neg_f, n_heads=N_FOX, dqk=HEAD, scale=HEAD ** -0.5, exact_p=True,
                           name="fox_fwd")
    o0 = jnp.concatenate([o_sb, o_fx], axis=1)
    x1 = _matmul(o0, w_o0, epilogue="res", res=xs, name="mix0_out")
    x2, mlp0 = _mlp_fwd(x1, ln_mlp_g[0], w_up[0], w_dw[0], "mlp0")

    h2 = _rms_fwd(x2, ln_mix_g[1], name="mix1_norm")
    down = _matmul(h2, w_dn, name="mix1_down")
    c_q = _rms_fwd(down, q_a_full, c0=0, width=Q_RANK, name="mix1_q_a_norm")
    c_kv = _rms_fwd(down, kv_a_full, c0=Q_RANK, width=KV_RANK, name="mix1_kv_a_norm")
    q_raw = _matmul(c_q, w_uq, name="mix1_uq")
    kv_raw = _matmul(c_kv, w_ukv, name="mix1_ukv")
    half = ROPE // 2
    inv_freq = ROPE_THETA ** (-jnp.arange(half, dtype=F32) / half)
    ang = pos.astype(F32)[:, None] * inv_freq
    table = lambda t: jnp.pad(jnp.concatenate([t, t], axis=1), ((0, 0), (0, LANES - ROPE)))
    cos_t, sin_t = table(jnp.cos(ang)), table(jnp.sin(ang))
    q_pad, k_pad, v1 = _mla_prep_fwd(q_raw, kv_raw, down, cos_t, sin_t, q_g_pad, k_g_pad, name="mla_prep_fwd")
    o1, lse1 = _attn_fwd(q_pad, k_pad, v1, None, n_heads=N_MLA, dqk=QK_PAD, scale=QK_DIM ** -0.5, name="mla_fwd")
    x3 = _matmul(o1, w_o1, epilogue="res", res=x2, name="mix1_out")
    x4, mlp1 = _mlp_fwd(x3, ln_mlp_g[1], w_up[1], w_dw[1], "mlp1")

    dx4, loss_local = _loss_head(x4, tgt, name="loss_head")
    loss = lax.psum(loss_local, ("x", "y", "c"))

    dx3, dg_mlp1, dw_up1, dw_dw1 = _mlp_bwd(dx4, x3, ln_mlp_g[1], w_up[1], w_dw[1], mlp1, "mlp1")

    dw_o1 = _matmul(o1, dx3, form="tn", name="mix1_dwo")
    do1 = _matmul(dx3, w_o1, form="nt", name="mix1_do")
    dq_pad, dk_pad, dv1 = _attn_bwd(q_pad, k_pad, v1, None, o1, lse1, do1, n_heads=N_MLA, dqk=QK_PAD,
                                    scale=QK_DIM ** -0.5, name="mla_bwd")
    dq_raw, dkv_raw, dpe, dg_q, dg_k = _mla_prep_bwd(dq_pad, dk_pad, dv1, q_raw, kv_raw, down, cos_t, sin_t,
                                                     q_g_pad, k_g_pad, name="mla_prep_bwd")
    dw_uq = _matmul(c_q, dq_raw, form="tn", name="mix1_dwuq")
    dc_q = _matmul(dq_raw, w_uq, form="nt", name="mix1_dcq")
    dw_ukv = _matmul(c_kv, dkv_raw, form="tn", name="mix1_dwukv")
    dc_kv = _matmul(dkv_raw, w_ukv, form="nt", name="mix1_dckv")
    d_cq, dg_qa = _rms_bwd(down, q_a_full, dc_q, c0=0, width=Q_RANK, name="mix1_q_a_dnorm")
    d_ckv, dg_kva = _rms_bwd(down, kv_a_full, dc_kv, c0=Q_RANK, width=KV_RANK, name="mix1_kv_a_dnorm")
    d_down = jnp.concatenate([d_cq, d_ckv, dpe], axis=1)
    dw_dn = _matmul(h2, d_down, form="tn", name="mix1_dwdown")
    dh2 = _matmul(d_down, w_dn, form="nt", name="mix1_dh")
    dx2, dg_mix1 = _rms_bwd(x2, ln_mix_g[1], dh2, res=dx3, name="mix1_dnorm")

    dx1, dg_mlp0, dw_up0, dw_dw0 = _mlp_bwd(dx2, x1, ln_mlp_g[0], w_up[0], w_dw[0], mlp0, "mlp0")

    dw_o0 = _matmul(o0, dx1, form="tn", name="mix0_dwo")
    do0 = _matmul(dx1, w_o0, form="nt", name="mix0_do")
    dq_sb, dk_sb, dv_sb = _sb_bwd(qkv_sb, do0, t_sb, do_off=0, name="sb_bwd")
    dq_f, dk_f, dv_fx, dbias, drow = _attn_bwd(q_f, k_f, v_fx, neg_f, o_fx, lse0, do0, n_heads=N_FOX, dqk=HEAD,
                                               scale=HEAD ** -0.5, do_off=N_SB, name="fox_bwd")
    dq_fx, dg_fq = _rms_bwd(qk_fx, fox_q_g[0], dq_f, c0=0, width=N_FOX * HEAD, gw=HEAD, name="fox_q_dnorm")
    dk_fx, dg_fk = _rms_bwd(qk_fx, fox_k_g[0], dk_f, c0=N_FOX * HEAD, width=N_FOX * HEAD, gw=HEAD,
                            name="fox_k_dnorm")
    d_fcum = jnp.pad((drow[:, :, 0] - dbias.reshape(N_FOX, S)).T, ((0, 0), (0, LANES - N_FOX)))
    dfl, db_f = _forget_bwd(fl, b_pad, d_fcum, name="forget_bwd")
    dproj = jnp.concatenate([dq_sb, dk_sb, dv_sb, dq_fx, dk_fx, dv_fx], axis=1).astype(BF16)
    dw_qkv = _matmul(h0, dproj, form="tn", name="mix0_dwqkv")
    dw_f = _matmul(h0, dfl, form="tn", name="mix0_dwf")
    dh0 = _matmul(dfl, w_f, form="nt", name="mix0_dh_f")
    dh0 = _matmul(dproj, w_qkv, form="nt", epilogue="res", res=dh0, name="mix0_dh")
    grad_x, dg_mix0 = _rms_bwd(xs, ln_mix_g[0], dh0, res=dx1, name="mix0_dnorm")

    by_cols = lambda g: g.reshape(g.shape[0], N_CHIPS, -1).transpose(1, 0, 2)
    by_rows = lambda g: g.reshape(N_CHIPS, g.shape[0] // N_CHIPS, g.shape[1])
    halves = lambda g: g.reshape(N_CHIPS, 2, g.shape[1] // 2, g.shape[2])
    g_in = jnp.concatenate([dw_qkv, dw_f[:, :in_w - qkv_w]], axis=1)
    g_uq = dw_uq.reshape(Q_RANK, N_MLA, QK_PAD)[:, :, :QK_DIM].reshape(Q_RANK, N_MLA * QK_DIM)
    grads = [halves(by_cols(g_in)), halves(by_rows(dw_o0)), halves(by_rows(dw_dn[:, :dn_w])),
             halves(by_cols(g_uq)), halves(by_cols(dw_ukv)), halves(by_rows(dw_o1)), halves(dw_up0),
             halves(dw_up1), halves(by_rows(dw_dw0)), halves(by_rows(dw_dw1))]
    tags = ["w_in", "w_o0", "w_dn", "w_uq", "w_ukv", "w_o1", "w_up0", "w_up1", "w_dw0", "w_dw1"]
    from_sibling = _swap_halves(grads, name="reduce_sibling")
    parts = [_add_sibling(g, r, core, name=f"add_sibling_{t}") for g, r, t in zip(grads, from_sibling, tags)]
    slots = [_own_slot(s, lax.dynamic_index_in_dim(p, chip, 0, keepdims=False), chip)
             for s, p in zip(_scatter_chips(parts, name="reduce_chips"), parts)]
    mine = [_add_chips(s, name=f"add_chips_{t}") for s, t in zip(slots, tags)]
    joined = [_own_slot(j, m, ci) for j, m in zip(_join_halves(mine, name="share_sibling"), mine)]
    shard = lambda j: j.reshape(2 * j.shape[1], j.shape[2])
    (gs_in, gs_o0, gs_dn, gs_uq, gs_ukv, gs_o1, gs_up0, gs_up1, gs_dw0, gs_dw1) = [shard(j) for j in joined]
    gs_up = jnp.concatenate([gs_up0, gs_up1], axis=0)
    gs_dw = jnp.concatenate([gs_dw0, gs_dw1], axis=0)

    ln_rows = D // LANES
    small = jnp.concatenate([
        _rows(dg_mix0, ln_rows), _rows(dg_mix1, ln_rows), _rows(dg_mlp0, ln_rows), _rows(dg_mlp1, ln_rows),
        _rows(db_f, 8), _rows(dg_fq, 8), _rows(dg_fk, 8), _rows(dg_qa, 8), _rows(dg_kva, 8), _rows(dg_q, 8),
        _rows(dg_k, 8)], axis=0)
    small = _all_reduce_small(small, name="reduce_small")
    flat = lambda r0, nr, n: small[r0:r0 + nr].reshape(-1)[:n]
    r0 = 4 * ln_rows
    g_ln_mix = jnp.stack([flat(0, ln_rows, D), flat(ln_rows, ln_rows, D)])
    g_ln_mlp = jnp.stack([flat(2 * ln_rows, ln_rows, D), flat(3 * ln_rows, ln_rows, D)])
    g_b_f = flat(r0, 8, N_FOX)[None]
    g_fq, g_fk = flat(r0 + 8, 8, HEAD)[None], flat(r0 + 16, 8, HEAD)[None]
    g_qa = lax.dynamic_slice(flat(r0 + 24, 8, Q_RANK), (chip * LANES,), (LANES,))[None]
    g_kva = lax.dynamic_slice(flat(r0 + 32, 8, KV_RANK), (chip * LANES,), (LANES,))[None]
    g_q, g_k = flat(r0 + 40, 8, QK_DIM)[None], flat(r0 + 48, 8, QK_DIM)[None]

    def pack_small(ln_mix, ln_mlp, *rest):
        return jnp.concatenate([_rows(ln_mix, 2 * ln_rows), _rows(ln_mlp, 2 * ln_rows)] + [_rows(t, 8) for t in rest],
                               axis=0)

    def unpack_small(p):
        f = lambda r, nr, shape: p[r:r + nr].reshape(-1)[:int(np.prod(shape))].reshape(shape)
        shapes = [(1, N_FOX), (1, HEAD), (1, HEAD), (1, LANES), (1, LANES), (1, QK_DIM), (1, QK_DIM)]
        return (f(0, 2 * ln_rows, (2, D)), f(2 * ln_rows, 2 * ln_rows, (2, D)),
                *[f(r0 + 8 * i, 8, shp) for i, shp in enumerate(shapes)])

    small_out = _adamw(
        pack_small(ln_mix_g, ln_mlp_g, sf_b_f, fox_q_g, fox_k_g, mla_q_a_g, mla_kv_a_g, mla_q_g, mla_k_g),
        pack_small(g_ln_mix, g_ln_mlp, g_b_f, g_fq, g_fk, g_qa, g_kva, g_q, g_k),
        pack_small(m_ln_mix_g, m_ln_mlp_g, m_sf_b_f, m_fox_q_g, m_fox_k_g, m_mla_q_a_g, m_mla_kv_a_g, m_mla_q_g,
                   m_mla_k_g),
        pack_small(v_ln_mix_g, v_ln_mlp_g, v_sf_b_f, v_fox_q_g, v_fox_k_g, v_mla_q_a_g, v_mla_kv_a_g, v_mla_q_g,
                   v_mla_k_g), name="adamw_small")
    d_small, m_small, v_small = [unpack_small(p) for p in small_out]

    def big(w, g, m, v, tag):
        shp = w.shape
        two_d = lambda t: t.reshape(-1, shp[-1])
        d, mn, vn = _adamw(two_d(w), g, two_d(m), two_d(v), name=f"adamw_{tag}")
        return g.reshape(shp), d.reshape(shp), mn.reshape(shp), vn.reshape(shp)

    r_in = big(sf_w_in, gs_in, m_sf_w_in, v_sf_w_in, "w_in")
    r_o0 = big(sf_w_o, gs_o0, m_sf_w_o, v_sf_w_o, "w_o0")
    r_dn = big(mla_w_down, gs_dn, m_mla_w_down, v_mla_w_down, "w_dn")
    r_uq = big(mla_w_uq, gs_uq, m_mla_w_uq, v_mla_w_uq, "w_uq")
    r_ukv = big(mla_w_ukv, gs_ukv, m_mla_w_ukv, v_mla_w_ukv, "w_ukv")
    r_o1 = big(mla_w_o, gs_o1, m_mla_w_o, v_mla_w_o, "w_o1")
    r_up = big(mlp_w_up, gs_up, m_mlp_w_up, v_mlp_w_up, "w_up")
    r_dw = big(mlp_w_down, gs_dw, m_mlp_w_down, v_mlp_w_down, "w_dw")

    g_small = (g_ln_mix, g_ln_mlp, g_b_f, g_fq, g_fk, g_qa, g_kva, g_q, g_k)

    def ordered(k, sm):
        return (sm[0], sm[1], r_in[k], sm[2], sm[3], sm[4], r_o0[k], r_dn[k], sm[5], sm[6], r_uq[k], r_ukv[k],
                sm[7], sm[8], r_o1[k], r_up[k], r_dw[k])

    return (loss, grad_x[None], *ordered(0, g_small), *ordered(1, d_small), *ordered(2, m_small),
            *ordered(3, v_small))
```

```python
import functools

import numpy as np
import jax
import jax.numpy as jnp
from jax import lax
from jax.experimental import pallas as pl
from jax.experimental.pallas import tpu as pltpu

F32 = jnp.float32
BF16 = jnp.bfloat16
MESH = pl.DeviceIdType.MESH

EPS = 1e-6
HEAD = 128
N_SB = 8
N_FOX = 8
N_MLA = 16
Q_RANK = 512
KV_RANK = 512
NOPE = 128
ROPE = 64
QK_DIM = NOPE + ROPE
QK_PAD = 256
ROPE_THETA = 10000.0
N_CHIPS = 4

ADAM_LR = 0.001
ADAM_B1 = 0.9
ADAM_B2 = 0.999
ADAM_EPS = 1e-08
ADAM_WD = 0.01
ADAM_STEP = 10

VMEM_LIMIT = 56 * 1024 * 1024
LANES = 128
NEG = -1e30


def _cp(*sem):
    return pltpu.CompilerParams(dimension_semantics=sem, vmem_limit_bytes=VMEM_LIMIT)


def _pick(dim, target):
    if dim <= target:
        return dim
    t = (target // LANES) * LANES
    while t >= LANES:
        if dim % t == 0:
            return t
        t -= LANES
    raise ValueError(f"no tile for {dim}")


NT_DIMS = (((1,), (1,)), ((), ()))
TN_DIMS = (((0,), (0,)), ((), ()))


def _dot(a, b):
    return jnp.dot(a, b, preferred_element_type=F32)


def _dot_nt(a, b):
    return lax.dot_general(a, b, NT_DIMS, preferred_element_type=F32)


def _dot_tn(a, b):
    return lax.dot_general(a, b, TN_DIMS, preferred_element_type=F32)


def _matmul(a, b, *, name, form="nn", out_dtype=F32, n=None, b_n0=0, b_split=False,
            out_split=False, epilogue="plain", res=None, u=None, tm=1024, tn=1024, tk=2048, comm=None):
    if form == "tn":
        K, M = a.shape
    else:
        M, K = a.shape
    if b_split:
        if form == "nt":
            nb_full, kb_full = b.shape[1], b.shape[2] * N_CHIPS
        else:
            kb_full, nb_full = b.shape[1], b.shape[2] * N_CHIPS
    elif form == "nt":
        nb_full, kb_full = b.shape
    else:
        kb_full, nb_full = b.shape
    assert kb_full == K, (name, a.shape, b.shape)
    N = nb_full if n is None else n
    if a.dtype != BF16 or b.dtype != BF16:
        tk = max(tk // 2, LANES)
    tm, tn, tk = _pick(M, tm), _pick(N, tn), _pick(K, tk)
    if b_split:
        per_chip = (b.shape[2])
        if form == "nt":
            tk = _pick(per_chip, tk)
        else:
            tn = _pick(per_chip, tn)
    if out_split:
        tn = _pick(N // N_CHIPS, tn)
    assert b_n0 % tn == 0
    nb0 = b_n0 // tn
    nk = K // tk
    grid = (M // tm, N // tn, nk)

    if form == "tn":
        a_spec = pl.BlockSpec((tk, tm), lambda i, j, k: (k, i))
    else:
        a_spec = pl.BlockSpec((tm, tk), lambda i, j, k: (i, k))
    if b_split:
        if form == "nt":
            kc = b.shape[2] // tk
            b_spec = pl.BlockSpec((None, tn, tk), lambda i, j, k: (k // kc, j, k % kc))
        else:
            nc = b.shape[2] // tn
            b_spec = pl.BlockSpec((None, tk, tn), lambda i, j, k: (j // nc, k, j % nc))
    elif form == "nt":
        b_spec = pl.BlockSpec((tn, tk), lambda i, j, k: (j + nb0, k))
    else:
        b_spec = pl.BlockSpec((tk, tn), lambda i, j, k: (k, j + nb0))
    mn_spec = pl.BlockSpec((tm, tn), lambda i, j, k: (i, j))
    if out_split:
        oc = (N // N_CHIPS) // tn
        out_spec = pl.BlockSpec((None, tm, tn), lambda i, j, k: (j // oc, i, j % oc))
        out_shape = jax.ShapeDtypeStruct((N_CHIPS, M, N // N_CHIPS), out_dtype)
    else:
        out_spec = mn_spec
        out_shape = jax.ShapeDtypeStruct((M, N), out_dtype)

    in_specs = [a_spec, b_spec]
    operands = [a, b]
    out_specs = (out_spec,)
    out_shape = (out_shape,)
    if epilogue == "res":
        in_specs.append(mn_spec)
        operands.append(res)
    elif epilogue == "sqrelu_bwd":
        in_specs.append(mn_spec)
        operands.append(u)
    elif epilogue == "sqrelu":
        out_specs = (mn_spec, mn_spec)
        out_shape = (jax.ShapeDtypeStruct((M, N), F32), jax.ShapeDtypeStruct((M, N), BF16))

    def finish(refs, r):
        if epilogue == "plain":
            refs[2][...] = r.astype(out_dtype)
        elif epilogue == "res":
            refs[3][...] = (refs[2][...] + r).astype(out_dtype)
        elif epilogue == "sqrelu":
            refs[2][...] = r
            p = jnp.maximum(r, 0.0)
            refs[3][...] = (p * p).astype(BF16)
        else:
            refs[3][...] = (r * (2.0 * jnp.maximum(refs[2][...], 0.0))).astype(out_dtype)

    def body(*refs):
        at = refs[0][...].astype(BF16)
        bt = refs[1][...].astype(BF16)
        if form == "nn":
            part = _dot(at, bt)
        elif form == "nt":
            part = _dot_nt(at, bt)
        else:
            part = _dot_tn(at, bt)
        if nk == 1:
            finish(refs, part)
            return
        acc = refs[-1]
        k = pl.program_id(2)

        @pl.when(k == 0)
        def _():
            acc[...] = part

        @pl.when(jnp.logical_and(k > 0, k < nk - 1))
        def _():
            acc[...] += part

        @pl.when(k == nk - 1)
        def _():
            finish(refs, acc[...] + part)

    outs, comm_outs = _hosted_call(
        body, grid=grid, in_specs=in_specs, out_specs=out_specs, out_shape=out_shape,
        scratch_shapes=[] if nk == 1 else [pltpu.VMEM((tm, tn), F32)], operands=operands, name=name,
        sem=("parallel", "parallel", "arbitrary"), comm=comm)
    result = outs if epilogue == "sqrelu" else outs[0]
    return result if comm is None else (result, comm_outs)


def _rms_fwd(x, g, *, name, c0=0, width=None, gw=None, tr=256):
    R, ctot = x.shape
    C = ctot if width is None else width
    gw = C if gw is None else gw
    assert c0 % C == 0 and C % gw == 0
    tr = _pick(R, tr)
    cb = c0 // C
    ng = C // gw

    def body(x_ref, g_ref, o_ref):
        gv = g_ref[...]
        for gi in range(ng):
            cols = slice(gi * gw, (gi + 1) * gw)
            xs = x_ref[:, cols]
            ms = jnp.sum(xs * xs, axis=-1, keepdims=True) * (1.0 / gw)
            o_ref[:, cols] = ((xs * lax.rsqrt(ms + EPS)) * gv).astype(o_ref.dtype)

    return pl.pallas_call(
        body, out_shape=jax.ShapeDtypeStruct((R, C), BF16), grid=(R // tr,),
        in_specs=[pl.BlockSpec((tr, C), lambda i: (i, cb)), pl.BlockSpec((1, gw), lambda i: (0, 0))],
        out_specs=pl.BlockSpec((tr, C), lambda i: (i, 0)), name=name,
        compiler_params=_cp("parallel"))(x, g.reshape(1, gw).astype(F32))


def _rms_bwd(x, g, dy, *, name, res=None, c0=0, width=None, gw=None, tr=256):
    R, ctot = x.shape
    C = ctot if width is None else width
    gw = C if gw is None else gw
    tr = _pick(R, tr)
    cb = c0 // C
    ng = C // gw
    nsteps = R // tr
    row_spec = pl.BlockSpec((tr, C), lambda i: (i, 0))
    in_specs = [pl.BlockSpec((tr, C), lambda i: (i, cb)), pl.BlockSpec((1, gw), lambda i: (0, 0)), row_spec]
    operands = [x, g.reshape(1, gw).astype(F32), dy]
    if res is not None:
        in_specs.append(row_spec)
        operands.append(res)

    def body(*refs):
        x_ref, g_ref, dy_ref = refs[:3]
        res_ref = refs[3] if res is not None else None
        dx_ref, dg_ref, acc = refs[-3:]
        i = pl.program_id(0)

        @pl.when(i == 0)
        def _():
            acc[...] = jnp.zeros_like(acc)

        gv = g_ref[...]
        for gi in range(ng):
            cols = slice(gi * gw, (gi + 1) * gw)
            xs = x_ref[:, cols]
            dys = dy_ref[:, cols].astype(F32)
            rstd = lax.rsqrt(jnp.sum(xs * xs, axis=-1, keepdims=True) * (1.0 / gw) + EPS)
            xh = xs * rstd
            gdy = dys * gv
            m = jnp.sum(gdy * xh, axis=-1, keepdims=True) * (1.0 / gw)
            dx = rstd * (gdy - xh * m)
            if res_ref is not None:
                dx = dx + res_ref[:, cols]
            dx_ref[:, cols] = dx
            acc[...] += jnp.sum((dys * xh).reshape(tr // 8, 8, gw), axis=0)

        @pl.when(i == nsteps - 1)
        def _():
            dg_ref[...] = jnp.sum(acc[...], axis=0, keepdims=True)

    dx, dg = pl.pallas_call(
        body, out_shape=(jax.ShapeDtypeStruct((R, C), F32), jax.ShapeDtypeStruct((1, gw), F32)),
        grid=(nsteps,), in_specs=in_specs,
        out_specs=(row_spec, pl.BlockSpec((1, gw), lambda i: (0, 0))),
        scratch_shapes=[pltpu.VMEM((8, gw), F32)], name=name,
        compiler_params=_cp("arbitrary"))(*operands)
    return dx, dg[0]


def _split3(x):
    hi = x.astype(BF16)
    r1 = x - hi.astype(F32)
    mid = r1.astype(BF16)
    lo = (r1 - mid.astype(F32)).astype(BF16)
    return hi, mid, lo


def _log_sigmoid(z):
    return jnp.minimum(z, 0.0) - jnp.log(1.0 + jnp.exp(-jnp.abs(z)))


def _forget_fwd(fl, b, *, name, tb=512):
    S = fl.shape[0]
    tb = _pick(S, tb)

    def body(fl_ref, b_ref, f_ref, carry):
        i = pl.program_id(0)

        @pl.when(i == 0)
        def _():
            carry[...] = jnp.zeros_like(carry)

        lf = _log_sigmoid(fl_ref[...] + b_ref[...])
        r = lax.broadcasted_iota(jnp.int32, (tb, tb), 0)
        c = lax.broadcasted_iota(jnp.int32, (tb, tb), 1)
        tri = (c <= r).astype(BF16)
        hi, mid, lo = _split3(lf)
        cs = _dot(tri, hi) + _dot(tri, mid) + _dot(tri, lo)
        f_ref[...] = cs + carry[...]
        carry[...] += jnp.sum(lf, axis=0, keepdims=True)

    return pl.pallas_call(
        body, out_shape=jax.ShapeDtypeStruct((S, LANES), F32), grid=(S // tb,),
        in_specs=[pl.BlockSpec((tb, LANES), lambda i: (i, 0)), pl.BlockSpec((1, LANES), lambda i: (0, 0))],
        out_specs=pl.BlockSpec((tb, LANES), lambda i: (i, 0)),
        scratch_shapes=[pltpu.VMEM((1, LANES), F32)], name=name,
        compiler_params=_cp("arbitrary"))(fl, b)


def _forget_bwd(fl, b, dF, *, name, tb=512):
    S = fl.shape[0]
    tb = _pick(S, tb)
    nb = S // tb

    def body(fl_ref, b_ref, df_ref, dfl_ref, db_ref, carry, acc):
        i = pl.program_id(0)

        @pl.when(i == 0)
        def _():
            carry[...] = jnp.zeros_like(carry)
            acc[...] = jnp.zeros_like(acc)

        d = df_ref[...]
        r = lax.broadcasted_iota(jnp.int32, (tb, tb), 0)
        c = lax.broadcasted_iota(jnp.int32, (tb, tb), 1)
        tri = (c >= r).astype(BF16)
        hi, mid, lo = _split3(d)
        rc = _dot(tri, hi) + _dot(tri, mid) + _dot(tri, lo) + carry[...]
        z = fl_ref[...] + b_ref[...]
        dfl = rc * jnp.exp(_log_sigmoid(-z))
        dfl_ref[...] = dfl
        carry[...] += jnp.sum(d, axis=0, keepdims=True)
        acc[...] += jnp.sum(dfl, axis=0, keepdims=True)

        @pl.when(i == nb - 1)
        def _():
            db_ref[...] = acc[...]

    rev = lambda i: (nb - 1 - i, 0)
    dfl, db = pl.pallas_call(
        body, out_shape=(jax.ShapeDtypeStruct((S, LANES), F32), jax.ShapeDtypeStruct((1, LANES), F32)),
        grid=(nb,),
        in_specs=[pl.BlockSpec((tb, LANES), rev), pl.BlockSpec((1, LANES), lambda i: (0, 0)),
                  pl.BlockSpec((tb, LANES), rev)],
        out_specs=(pl.BlockSpec((tb, LANES), rev), pl.BlockSpec((1, LANES), lambda i: (0, 0))),
        scratch_shapes=[pltpu.VMEM((1, LANES), F32), pltpu.VMEM((1, LANES), F32)], name=name,
        compiler_params=_cp("arbitrary"))(fl, b, dF)
    return dfl, db[0]


def _tri(tk, rel):
    r = lax.broadcasted_iota(jnp.int32, (tk, tk), 0)
    c = lax.broadcasted_iota(jnp.int32, (tk, tk), 1)
    m = {"gt": r > c, "le": r <= c, "lt": r < c}[rel]
    return m.astype(BF16)


def _split2(x):
    hi = x.astype(BF16)
    return hi, (x - hi.astype(F32)).astype(BF16)


HEADS_PER_STEP = 2


def _diag_mask(tq, strict):
    r = lax.broadcasted_iota(jnp.int32, (tq, tq), 0)
    c = lax.broadcasted_iota(jnp.int32, (tq, tq), 1)
    return c < r if strict else c <= r


def _sb_fwd(qkv, *, name, n_heads=N_SB, q_off=0, k_off=N_SB, v_off=2 * N_SB, tq=256, hp=HEADS_PER_STEP,
            comm=None):
    S = qkv.shape[0]
    tq = _pick(S, tq)
    tk = tq
    scale = HEAD ** -0.5
    nq = S // tq
    assert n_heads % hp == 0 and q_off % hp == 0 and k_off % hp == 0 and v_off % hp == 0

    def body(q_ref, k_ref, v_ref, o_ref, t_ref, c_sc, acc_sc):
        qi = pl.program_id(1)
        c_sc[...] = jnp.zeros_like(c_sc)
        acc_sc[...] = jnp.zeros_like(acc_sc)
        gt = _tri(tk, "gt")

        def tile(hh, j, diag):
            cs = slice(hh * HEAD, (hh + 1) * HEAD)
            rows = pl.ds(pl.multiple_of(j * tk, tk), tk)
            z = _dot_nt(q_ref[:, cs], k_ref[rows, cs]) * scale
            sp = jnp.log(1.0 + jnp.exp(-jnp.abs(z)))
            la = jnp.minimum(z, 0.0) - sp
            lb = -jnp.maximum(z, 0.0) - sp
            if diag:
                strict = _diag_mask(tq, True)
                lb = jnp.where(strict, lb, 0.0)
            hi, lo = _split2(lb)
            suffix = _dot(hi, gt) + _dot(lo, gt)
            w = jnp.exp(la + suffix + c_sc[hh])
            if diag:
                w = jnp.where(strict, w, 0.0)
            acc_sc[hh] += _dot(w.astype(BF16), v_ref[rows, cs])
            c_sc[hh] += jnp.sum(lb, axis=1, keepdims=True)

        for hh in range(hp):
            tile(hh, qi, True)

        def step(it, carry):
            for hh in range(hp):
                tile(hh, qi - 1 - it, False)
            return carry

        lax.fori_loop(0, qi, step, 0)
        for hh in range(hp):
            o_ref[:, hh * HEAD:(hh + 1) * HEAD] = acc_sc[hh]
            t_ref[hh] = jnp.broadcast_to(c_sc[hh], (tq, LANES))

    w = hp * HEAD
    head_blk = lambda off: pl.BlockSpec((S, w), lambda h, i: (0, h + off // hp))
    outs, comm_outs = _hosted_call(
        body,
        out_shape=(jax.ShapeDtypeStruct((S, n_heads * HEAD), F32),
                   jax.ShapeDtypeStruct((n_heads, S, LANES), F32)),
        grid=(n_heads // hp, nq),
        in_specs=[pl.BlockSpec((tq, w), lambda h, i: (i, h + q_off // hp)), head_blk(k_off), head_blk(v_off)],
        out_specs=(pl.BlockSpec((tq, w), lambda h, i: (i, h)),
                   pl.BlockSpec((hp, tq, LANES), lambda h, i: (h, i, 0))),
        scratch_shapes=[pltpu.VMEM((hp, tq, 1), F32), pltpu.VMEM((hp, tq, HEAD), F32)], name=name,
        sem=("parallel", "arbitrary"), operands=(qkv, qkv, qkv), comm=comm)
    return outs if comm is None else (outs, comm_outs)


def _sb_bwd(qkv, do, tstat, *, name, n_heads=N_SB, q_off=0, k_off=N_SB, v_off=2 * N_SB, do_off=0, tq=256,
            hp=HEADS_PER_STEP, comm=None):
    S = qkv.shape[0]
    tq = _pick(S, tq)
    tk = tq
    scale = HEAD ** -0.5
    nq = S // tq
    assert n_heads % hp == 0 and q_off % hp == 0 and k_off % hp == 0 and v_off % hp == 0 and do_off % hp == 0

    def body(q_ref, k_ref, v_ref, do_ref, t_ref, dq_ref, dk_ref, dv_ref, p_sc, r_sc, dq_sc):
        qi = pl.program_id(1)

        @pl.when(qi == 0)
        def _():
            dk_ref[...] = jnp.zeros_like(dk_ref)
            dv_ref[...] = jnp.zeros_like(dv_ref)

        p_sc[...] = jnp.zeros_like(p_sc)
        r_sc[...] = jnp.zeros_like(r_sc)
        dq_sc[...] = jnp.zeros_like(dq_sc)
        le = _tri(tk, "le")
        lt = _tri(tk, "lt")

        def tile(hh, j, diag):
            cs = slice(hh * HEAD, (hh + 1) * HEAD)
            rows = pl.ds(pl.multiple_of(j * tk, tk), tk)
            q = q_ref[:, cs]
            do_b = do_ref[:, cs].astype(BF16)
            kb = k_ref[rows, cs]
            z = _dot_nt(q, kb) * scale
            sp = jnp.log(1.0 + jnp.exp(-jnp.abs(z)))
            la = jnp.minimum(z, 0.0) - sp
            lb = -jnp.maximum(z, 0.0) - sp
            if diag:
                strict = _diag_mask(tq, True)
                lb = jnp.where(strict, lb, 0.0)
            hi, lo = _split2(lb)
            prefix = _dot(hi, le) + _dot(lo, le) + p_sc[hh]
            w = jnp.exp(la + t_ref[hh, :, 0:1] - prefix)
            if diag:
                w = jnp.where(strict, w, 0.0)
            r = w * _dot_nt(do_b, v_ref[rows, cs])
            hi, lo = _split2(r)
            rex = _dot(hi, lt) + _dot(lo, lt) + r_sc[hh]
            beta = jnp.exp(la)
            dz = (r * (1.0 - beta) - rex * beta) * scale
            if diag:
                dz = jnp.where(strict, dz, 0.0)
            dzb = dz.astype(BF16)
            dq_sc[hh] += _dot(dzb, kb)
            dk_ref[rows, cs] += _dot_tn(dzb, q)
            dv_ref[rows, cs] += _dot_tn(w.astype(BF16), do_b)
            p_sc[hh] += jnp.sum(lb, axis=1, keepdims=True)
            r_sc[hh] += jnp.sum(r, axis=1, keepdims=True)

        def step(j, carry):
            for hh in range(hp):
                tile(hh, j, False)
            return carry

        lax.fori_loop(0, qi, step, 0)
        for hh in range(hp):
            tile(hh, qi, True)
            dq_ref[:, hh * HEAD:(hh + 1) * HEAD] = dq_sc[hh]

    w = hp * HEAD
    head_blk = lambda off: pl.BlockSpec((S, w), lambda h, i: (0, h + off // hp))
    out_head = pl.BlockSpec((S, w), lambda h, i: (0, h))
    out_sd = jax.ShapeDtypeStruct((S, n_heads * HEAD), F32)
    outs, comm_outs = _hosted_call(
        body, out_shape=(out_sd, out_sd, out_sd), grid=(n_heads // hp, nq),
        in_specs=[pl.BlockSpec((tq, w), lambda h, i: (i, h + q_off // hp)), head_blk(k_off), head_blk(v_off),
                  pl.BlockSpec((tq, w), lambda h, i: (i, h + do_off // hp)),
                  pl.BlockSpec((hp, tq, LANES), lambda h, i: (h, i, 0))],
        out_specs=(pl.BlockSpec((tq, w), lambda h, i: (i, h)), out_head, out_head),
        scratch_shapes=[pltpu.VMEM((hp, tq, 1), F32), pltpu.VMEM((hp, tq, 1), F32), pltpu.VMEM((hp, tq, HEAD), F32)],
        name=name, sem=("parallel", "arbitrary"), operands=(qkv, qkv, qkv, do, tstat), comm=comm)
    return outs if comm is None else (outs, comm_outs)


def _attn_fwd(q, k, v, bias, *, name, n_heads, dqk, scale, v_off=0, tq=512, exact_p=False, hp=HEADS_PER_STEP,
              comm=None):
    S = q.shape[0]
    tq = _pick(S, tq)
    tk = tq
    nq = S // tq
    has_bias = bias is not None

    assert n_heads % hp == 0 and v_off % hp == 0

    def body(*refs):
        q_ref, k_ref, v_ref = refs[:3]
        b_ref = refs[3] if has_bias else None
        o_ref, lse_ref, m_sc, l_sc, acc_sc = refs[-5:]
        qi = pl.program_id(1)
        m_sc[...] = jnp.full_like(m_sc, NEG)
        l_sc[...] = jnp.zeros_like(l_sc)
        acc_sc[...] = jnp.zeros_like(acc_sc)

        def tile(hh, j, diag):
            rows = pl.ds(pl.multiple_of(j * tk, tk), tk)
            s = _dot_nt(q_ref[:, hh * dqk:(hh + 1) * dqk], k_ref[rows, hh * dqk:(hh + 1) * dqk]) * scale
            if has_bias:
                s = s + b_ref[hh, :, rows]
            if diag:
                s = jnp.where(_diag_mask(tq, False), s, NEG)
            m_old = m_sc[hh]
            m_new = jnp.maximum(m_old, jnp.max(s, axis=1, keepdims=True))
            alpha = jnp.exp(m_old - m_new)
            p = jnp.exp(s - m_new)
            l_sc[hh] = alpha * l_sc[hh] + jnp.sum(p, axis=1, keepdims=True)
            vb = v_ref[rows, hh * HEAD:(hh + 1) * HEAD]
            if exact_p:
                hi, lo = _split2(p)
                pv = _dot(hi, vb) + _dot(lo, vb)
            else:
                pv = _dot(p.astype(BF16), vb)
            acc_sc[hh] = alpha * acc_sc[hh] + pv
            m_sc[hh] = m_new

        def step(j, carry):
            for hh in range(hp):
                tile(hh, j, False)
            return carry

        lax.fori_loop(0, qi, step, 0)
        for hh in range(hp):
            tile(hh, qi, True)
            l = l_sc[hh]
            o_ref[:, hh * HEAD:(hh + 1) * HEAD] = acc_sc[hh] / l
            lse_ref[hh] = jnp.broadcast_to(m_sc[hh] + jnp.log(l), (tq, LANES))

    in_specs = [pl.BlockSpec((tq, hp * dqk), lambda h, i: (i, h)),
                pl.BlockSpec((S, hp * dqk), lambda h, i: (0, h)),
                pl.BlockSpec((S, hp * HEAD), lambda h, i: (0, h + v_off // hp))]
    operands = [q, k, v]
    if has_bias:
        in_specs.append(pl.BlockSpec((hp, 1, S), lambda h, i: (h, 0, 0)))
        operands.append(bias)
    outs, comm_outs = _hosted_call(
        body,
        out_shape=(jax.ShapeDtypeStruct((S, n_heads * HEAD), F32),
                   jax.ShapeDtypeStruct((n_heads, S, LANES), F32)),
        grid=(n_heads // hp, nq), in_specs=in_specs,
        out_specs=(pl.BlockSpec((tq, hp * HEAD), lambda h, i: (i, h)),
                   pl.BlockSpec((hp, tq, LANES), lambda h, i: (h, i, 0))),
        scratch_shapes=[pltpu.VMEM((hp, tq, 1), F32), pltpu.VMEM((hp, tq, 1), F32), pltpu.VMEM((hp, tq, HEAD), F32)],
        name=name, sem=("parallel", "arbitrary"), operands=operands, comm=comm)
    return outs if comm is None else (outs, comm_outs)


def _attn_bwd(q, k, v, bias, o, lse, do, *, name, n_heads, dqk, scale, v_off=0, do_off=0, tq=512,
              hp=HEADS_PER_STEP, comm=None):
    S = q.shape[0]
    tq = _pick(S, tq)
    tk = tq
    nq = S // tq
    has_bias = bias is not None
    assert n_heads % hp == 0 and v_off % hp == 0 and do_off % hp == 0

    def body(*refs):
        q_ref, k_ref, v_ref, o_ref, lse_ref, do_ref = refs[:6]
        b_ref = refs[6] if has_bias else None
        n_out = 5 if has_bias else 3
        outs = refs[-(n_out + 3):-3]
        dq_ref, dk_ref, dv_ref = outs[:3]
        db_ref, dr_ref = (outs[3], outs[4]) if has_bias else (None, None)
        dq_sc, rs_sc, delta_sc = refs[-3:]
        qi = pl.program_id(1)

        @pl.when(qi == 0)
        def _():
            dk_ref[...] = jnp.zeros_like(dk_ref)
            dv_ref[...] = jnp.zeros_like(dv_ref)
            if has_bias:
                db_ref[...] = jnp.zeros_like(db_ref)

        dq_sc[...] = jnp.zeros_like(dq_sc)
        rs_sc[...] = jnp.zeros_like(rs_sc)
        for hh in range(hp):
            vs = slice(hh * HEAD, (hh + 1) * HEAD)
            do_r = do_ref[:, vs].astype(BF16).astype(F32)
            delta_sc[hh] = jnp.sum(do_r * o_ref[:, vs], axis=1, keepdims=True)

        def tile(hh, j, diag):
            qs = slice(hh * dqk, (hh + 1) * dqk)
            vs = slice(hh * HEAD, (hh + 1) * HEAD)
            rows = pl.ds(pl.multiple_of(j * tk, tk), tk)
            qb = q_ref[:, qs]
            do_b = do_ref[:, vs].astype(BF16)
            delta = delta_sc[hh]
            kb = k_ref[rows, qs]
            s = _dot_nt(qb, kb) * scale
            if has_bias:
                s = s + b_ref[hh, :, rows]
            p = jnp.exp(s - lse_ref[hh, :, 0:1])
            if diag:
                p = jnp.where(_diag_mask(tq, False), p, 0.0)
            ds = p * (_dot_nt(do_b, v_ref[rows, vs]) - delta)
            dsb = (ds * scale).astype(BF16)
            dq_sc[hh] += _dot(dsb, kb)
            dk_ref[rows, qs] += _dot_tn(dsb, qb)
            dv_ref[rows, vs] += _dot_tn(p.astype(BF16), do_b)
            if has_bias:
                db_ref[hh, :, rows] += jnp.sum(ds, axis=0, keepdims=True)
                rs_sc[hh] += jnp.sum(ds, axis=1, keepdims=True)

        def step(j, carry):
            for hh in range(hp):
                tile(hh, j, False)
            return carry

        lax.fori_loop(0, qi, step, 0)
        for hh in range(hp):
            tile(hh, qi, True)
            dq_ref[:, hh * dqk:(hh + 1) * dqk] = dq_sc[hh]
            if has_bias:
                dr_ref[hh] = jnp.broadcast_to(rs_sc[hh], (tq, LANES))

    stat = pl.BlockSpec((hp, tq, LANES), lambda h, i: (h, i, 0))
    in_specs = [pl.BlockSpec((tq, hp * dqk), lambda h, i: (i, h)),
                pl.BlockSpec((S, hp * dqk), lambda h, i: (0, h)),
                pl.BlockSpec((S, hp * HEAD), lambda h, i: (0, h + v_off // hp)),
                pl.BlockSpec((tq, hp * HEAD), lambda h, i: (i, h)),
                stat,
                pl.BlockSpec((tq, hp * HEAD), lambda h, i: (i, h + do_off // hp))]
    operands = [q, k, v, o, lse, do]
    out_shape = [jax.ShapeDtypeStruct((S, n_heads * dqk), F32), jax.ShapeDtypeStruct((S, n_heads * dqk), F32),
                 jax.ShapeDtypeStruct((S, n_heads * HEAD), F32)]
    out_specs = [pl.BlockSpec((tq, hp * dqk), lambda h, i: (i, h)), pl.BlockSpec((S, hp * dqk), lambda h, i: (0, h)),
                 pl.BlockSpec((S, hp * HEAD), lambda h, i: (0, h))]
    if has_bias:
        in_specs.append(pl.BlockSpec((hp, 1, S), lambda h, i: (h, 0, 0)))
        operands.append(bias)
        out_shape.append(jax.ShapeDtypeStruct((n_heads, 1, S), F32))
        out_specs.append(pl.BlockSpec((hp, 1, S), lambda h, i: (h, 0, 0)))
        out_shape.append(jax.ShapeDtypeStruct((n_heads, S, LANES), F32))
        out_specs.append(stat)
    outs, comm_outs = _hosted_call(
        body, out_shape=tuple(out_shape), grid=(n_heads // hp, nq), in_specs=in_specs, out_specs=tuple(out_specs),
        scratch_shapes=[pltpu.VMEM((hp, tq, dqk), F32), pltpu.VMEM((hp, tq, 1), F32),
                        pltpu.VMEM((hp, tq, 1), F32)], name=name,
        sem=("parallel", "arbitrary"), operands=operands, comm=comm)
    return outs if comm is None else (outs, comm_outs)


def _rot_half(y):
    lane = lax.broadcasted_iota(jnp.int32, y.shape, 1)
    up = pltpu.roll(y, 96, 1)
    down = pltpu.roll(y, 32, 1)
    return jnp.where(lane < 32, -up, jnp.where(lane < 64, down, 0.0))


def _mla_prep_fwd(q_raw, kv_raw, down, cos, sin, q_g, k_g, *, name, ts=128):
    S = q_raw.shape[0]
    ts = _pick(S, ts)
    pe_blk = Q_RANK // LANES + KV_RANK // LANES

    def norm_rope(x0, x1, g0, g1, c, s):
        ms = (jnp.sum(x0 * x0, axis=-1, keepdims=True) + jnp.sum(x1 * x1, axis=-1, keepdims=True)) * (1.0 / QK_DIM)
        rstd = lax.rsqrt(ms + EPS)
        y0 = (x0 * rstd) * g0
        y1 = (x1 * rstd) * g1
        return y0, y1 * c + _rot_half(y1) * s

    def body(q_ref, kv_ref, pe_ref, cos_ref, sin_ref, qg_ref, kg_ref, qo_ref, ko_ref, vo_ref):
        c, s = cos_ref[...], sin_ref[...]
        pe = pe_ref[...]
        qg0, qg1 = qg_ref[:, :NOPE], qg_ref[:, NOPE:]
        kg0, kg1 = kg_ref[:, :NOPE], kg_ref[:, NOPE:]
        for h in range(N_MLA):
            b = h * QK_PAD
            y0, y1 = norm_rope(q_ref[:, b:b + NOPE], q_ref[:, b + NOPE:b + QK_PAD], qg0, qg1, c, s)
            qo_ref[:, b:b + NOPE] = y0.astype(BF16)
            qo_ref[:, b + NOPE:b + QK_PAD] = y1.astype(BF16)
            y0, y1 = norm_rope(kv_ref[:, b:b + NOPE], pe, kg0, kg1, c, s)
            ko_ref[:, b:b + NOPE] = y0.astype(BF16)
            ko_ref[:, b + NOPE:b + QK_PAD] = y1.astype(BF16)
            vo_ref[:, h * HEAD:(h + 1) * HEAD] = kv_ref[:, b + NOPE:b + QK_PAD].astype(BF16)

    wide = pl.BlockSpec((ts, N_MLA * QK_PAD), lambda i: (i, 0))
    lane_blk = pl.BlockSpec((ts, LANES), lambda i: (i, 0))
    gain = pl.BlockSpec((1, QK_PAD), lambda i: (0, 0))
    return pl.pallas_call(
        body,
        out_shape=(jax.ShapeDtypeStruct((S, N_MLA * QK_PAD), BF16), jax.ShapeDtypeStruct((S, N_MLA * QK_PAD), BF16),
                   jax.ShapeDtypeStruct((S, N_MLA * HEAD), BF16)),
        grid=(S // ts,),
        in_specs=[wide, wide, pl.BlockSpec((ts, LANES), lambda i: (i, pe_blk)), lane_blk, lane_blk, gain, gain],
        out_specs=(wide, wide, pl.BlockSpec((ts, N_MLA * HEAD), lambda i: (i, 0))), name=name,
        compiler_params=_cp("parallel"))(q_raw, kv_raw, down, cos, sin, q_g, k_g)


def _mla_prep_bwd(dq, dk, dv, q_raw, kv_raw, down, cos, sin, q_g, k_g, *, name, ts=128):
    S = q_raw.shape[0]
    ts = _pick(S, ts)
    nsteps = S // ts
    pe_blk = Q_RANK // LANES + KV_RANK // LANES

    def back(x0, x1, g0, g1, c, s, d0, d1r):
        d1 = d1r * c - _rot_half(d1r * s)
        ms = (jnp.sum(x0 * x0, axis=-1, keepdims=True) + jnp.sum(x1 * x1, axis=-1, keepdims=True)) * (1.0 / QK_DIM)
        rstd = lax.rsqrt(ms + EPS)
        h0, h1 = x0 * rstd, x1 * rstd
        e0, e1 = d0 * g0, d1 * g1
        m = (jnp.sum(e0 * h0, axis=-1, keepdims=True) + jnp.sum(e1 * h1, axis=-1, keepdims=True)) * (1.0 / QK_DIM)
        return rstd * (e0 - h0 * m), rstd * (e1 - h1 * m), d0 * h0, d1 * h1

    def fold(a):
        return jnp.sum(a.reshape(ts // 8, 8, a.shape[-1]), axis=0)

    def body(dq_ref, dk_ref, dv_ref, q_ref, kv_ref, pe_ref, cos_ref, sin_ref, qg_ref, kg_ref,
             dqr_ref, dkv_ref, dpe_ref, dqg_ref, dkg_ref, gq_sc, gk_sc):
        i = pl.program_id(0)

        @pl.when(i == 0)
        def _():
            gq_sc[...] = jnp.zeros_like(gq_sc)
            gk_sc[...] = jnp.zeros_like(gk_sc)

        c, s = cos_ref[...], sin_ref[...]
        pe = pe_ref[...]
        qg0, qg1 = qg_ref[:, :NOPE], qg_ref[:, NOPE:]
        kg0, kg1 = kg_ref[:, :NOPE], kg_ref[:, NOPE:]
        dpe = jnp.zeros((ts, LANES), F32)
        for h in range(N_MLA):
            b = h * QK_PAD
            dx0, dx1, a0, a1 = back(q_ref[:, b:b + NOPE], q_ref[:, b + NOPE:b + QK_PAD], qg0, qg1, c, s,
                                    dq_ref[:, b:b + NOPE], dq_ref[:, b + NOPE:b + QK_PAD])
            dqr_ref[:, b:b + NOPE] = dx0.astype(BF16)
            dqr_ref[:, b + NOPE:b + QK_PAD] = dx1.astype(BF16)
            gq_sc[:, :NOPE] += fold(a0)
            gq_sc[:, NOPE:] += fold(a1)
            dx0, dx1, a0, a1 = back(kv_ref[:, b:b + NOPE], pe, kg0, kg1, c, s,
                                    dk_ref[:, b:b + NOPE], dk_ref[:, b + NOPE:b + QK_PAD])
            dkv_ref[:, b:b + NOPE] = dx0.astype(BF16)
            dkv_ref[:, b + NOPE:b + QK_PAD] = dv_ref[:, h * HEAD:(h + 1) * HEAD].astype(BF16)
            dpe = dpe + dx1
            gk_sc[:, :NOPE] += fold(a0)
            gk_sc[:, NOPE:] += fold(a1)
        dpe_ref[...] = dpe

        @pl.when(i == nsteps - 1)
        def _():
            dqg_ref[...] = jnp.sum(gq_sc[...], axis=0, keepdims=True)
            dkg_ref[...] = jnp.sum(gk_sc[...], axis=0, keepdims=True)

    wide = pl.BlockSpec((ts, N_MLA * QK_PAD), lambda i: (i, 0))
    lane_blk = pl.BlockSpec((ts, LANES), lambda i: (i, 0))
    gain = pl.BlockSpec((1, QK_PAD), lambda i: (0, 0))
    outs = pl.pallas_call(
        body,
        out_shape=(jax.ShapeDtypeStruct((S, N_MLA * QK_PAD), BF16), jax.ShapeDtypeStruct((S, N_MLA * QK_PAD), BF16),
                   jax.ShapeDtypeStruct((S, LANES), F32), jax.ShapeDtypeStruct((1, QK_PAD), F32),
                   jax.ShapeDtypeStruct((1, QK_PAD), F32)),
        grid=(nsteps,),
        in_specs=[wide, wide, pl.BlockSpec((ts, N_MLA * HEAD), lambda i: (i, 0)), wide, wide,
                  pl.BlockSpec((ts, LANES), lambda i: (i, pe_blk)), lane_blk, lane_blk, gain, gain],
        out_specs=(wide, wide, lane_blk, gain, gain),
        scratch_shapes=[pltpu.VMEM((8, QK_PAD), F32), pltpu.VMEM((8, QK_PAD), F32)], name=name,
        compiler_params=_cp("arbitrary"))(dq, dk, dv, q_raw, kv_raw, down, cos, sin, q_g, k_g)
    return outs[0], outs[1], outs[2], outs[3][0], outs[4][0]


def _loss_head(y, target, *, name, tr=256):
    R, C = y.shape
    tr = _pick(R, tr)
    nsteps = R // tr

    def body(y_ref, t_ref, dy_ref, loss_ref, acc):
        i = pl.program_id(0)

        @pl.when(i == 0)
        def _():
            acc[...] = jnp.zeros_like(acc)

        err = y_ref[...] - t_ref[...]
        dy_ref[...] = err * (1.0 / C)
        acc[...] += jnp.sum((err * err).reshape(tr // 8, 8, C), axis=0)

        @pl.when(i == nsteps - 1)
        def _():
            tot = jnp.sum(jnp.sum(acc[...], axis=0, keepdims=True), axis=1, keepdims=True)
            loss_ref[...] = jnp.broadcast_to(tot * (0.5 / C), (8, LANES))

    blk = pl.BlockSpec((tr, C), lambda i: (i, 0))
    dy, loss = pl.pallas_call(
        body, out_shape=(jax.ShapeDtypeStruct((R, C), F32), jax.ShapeDtypeStruct((8, LANES), F32)),
        grid=(nsteps,), in_specs=[blk, blk], out_specs=(blk, pl.BlockSpec((8, LANES), lambda i: (0, 0))),
        scratch_shapes=[pltpu.VMEM((8, C), F32)], name=name, compiler_params=_cp("arbitrary"))(y, target)
    return dy, loss[0, 0]


def _adamw(w, g, m, v, *, name, block_bytes=1 << 20):
    R, C = w.shape
    tr = max(8, min(R, (block_bytes // (4 * C)) // 8 * 8))
    while R % tr:
        tr -= 8
    if tr <= 0:
        tr = R
    c1 = 1.0 / (1.0 - ADAM_B1 ** ADAM_STEP)
    c2 = 1.0 / (1.0 - ADAM_B2 ** ADAM_STEP)

    def body(w_ref, g_ref, m_ref, v_ref, d_ref, mo_ref, vo_ref):
        gv = g_ref[...]
        mn = ADAM_B1 * m_ref[...] + (1.0 - ADAM_B1) * gv
        vn = ADAM_B2 * v_ref[...] + (1.0 - ADAM_B2) * (gv * gv)
        d_ref[...] = -ADAM_LR * ((mn * c1) / (jnp.sqrt(vn * c2) + ADAM_EPS) + ADAM_WD * w_ref[...])
        mo_ref[...] = mn
        vo_ref[...] = vn

    blk = pl.BlockSpec((tr, C), lambda i: (i, 0))
    sd = jax.ShapeDtypeStruct((R, C), F32)
    return pl.pallas_call(
        body, out_shape=(sd, sd, sd), grid=(R // tr,), in_specs=[blk] * 4, out_specs=(blk,) * 3, name=name,
        compiler_params=_cp("parallel"))(w, g, m, v)


def _row_tile(r, c, itemsize=4, block_bytes=1 << 20):
    tr = max(16, min(r, (block_bytes // (itemsize * c)) // 16 * 16))
    while r % tr:
        tr -= 16
    return tr if tr > 0 else r


def _add_sibling(g, recv, core, *, name):
    nch, _, r, c = g.shape
    tr = _row_tile(r, c)

    def body(core_ref, g_ref, r_ref, o_ref):
        o_ref[...] = (g_ref[...] + r_ref[...]).astype(BF16)

    grid_spec = pltpu.PrefetchScalarGridSpec(
        num_scalar_prefetch=1, grid=(nch, r // tr),
        in_specs=[pl.BlockSpec((None, None, tr, c), lambda j, i, cr: (j, cr[0], i, 0)),
                  pl.BlockSpec((None, tr, c), lambda j, i, cr: (j, i, 0))],
        out_specs=pl.BlockSpec((None, tr, c), lambda j, i, cr: (j, i, 0)))
    return pl.pallas_call(
        body, out_shape=jax.ShapeDtypeStruct((nch, r, c), BF16), grid_spec=grid_spec, name=name,
        compiler_params=_cp("parallel", "parallel"))(core, g, recv)


def _add_chips(slots, *, name):
    nch, r, c = slots.shape
    tr = _row_tile(r, c)

    def body(s_ref, o_ref):
        acc = s_ref[0].astype(F32)
        for j in range(1, nch):
            acc = acc + s_ref[j].astype(F32)
        o_ref[...] = acc

    return pl.pallas_call(
        body, out_shape=jax.ShapeDtypeStruct((r, c), F32), grid=(r // tr,),
        in_specs=[pl.BlockSpec((nch, tr, c), lambda i: (0, i, 0))],
        out_specs=pl.BlockSpec((tr, c), lambda i: (i, 0)), name=name, compiler_params=_cp("parallel"))(slots)


def _place():
    x, y, c = lax.axis_index("x"), lax.axis_index("y"), lax.axis_index("c")
    others = [(1 - x, y), (x, 1 - y), (1 - x, 1 - y)]
    return x, y, c, 2 * x + y, others


ANY = pl.BlockSpec(memory_space=pl.ANY)


class _ChipExchange:
    def __init__(self, kind, arrays):
        self.kind, self.ins = kind, list(arrays)
        n = len(self.ins)
        shp = lambda a: ((N_CHIPS,) + a.shape) if kind == "gather" else a.shape
        self.out_shapes = [jax.ShapeDtypeStruct(shp(a), a.dtype) for a in self.ins]
        self.sems = [pltpu.SemaphoreType.DMA((3 * n,)), pltpu.SemaphoreType.DMA((3 * n,))]

    def _copies(self, ins, outs, sems):
        send, recv = sems
        x, y, c, me, others = _place()
        for a in range(len(self.ins)):
            for k, (ox, oy) in enumerate(others):
                peer = 2 * ox + oy
                if self.kind == "gather":
                    hr = self.ins[a].shape[0] // 2
                    rows = pl.ds(c * hr, hr)
                    src, dst, land = ins[a].at[rows, :], outs[a].at[me, rows, :], outs[a].at[peer, rows, :]
                else:
                    src, dst, land = ins[a].at[peer], outs[a].at[me], outs[a].at[peer]
                mk = lambda s, d: pltpu.make_async_remote_copy(
                    src_ref=s, dst_ref=d, send_sem=send.at[3 * a + k], recv_sem=recv.at[3 * a + k],
                    device_id=(ox, oy, c), device_id_type=MESH)
                yield mk(src, dst), mk(land, land)

    def start(self, ins, outs, sems):
        for cp, _ in self._copies(ins, outs, sems):
            cp.start()

    def finish(self, ins, outs, sems):
        pairs = list(self._copies(ins, outs, sems))
        for _, landing in pairs:
            landing.wait_recv()
        for cp, _ in pairs:
            cp.wait_send()


def _hosted_call(body, *, grid, in_specs, out_specs, out_shape, scratch_shapes, operands, name, sem, comm=None):
    out_specs, out_shape = tuple(out_specs), tuple(out_shape)
    if comm is None:
        res = pl.pallas_call(body, out_shape=out_shape, grid=grid, in_specs=list(in_specs), out_specs=out_specs,
                             scratch_shapes=list(scratch_shapes), name=name, compiler_params=_cp(*sem))(*operands)
        return tuple(res), ()
    n_in, n_out, n_sc = len(in_specs), len(out_specs), len(scratch_shapes)
    ci, co = len(comm.ins), len(comm.out_shapes)

    def wrapped(*refs):
        ins, c_ins = refs[:n_in], refs[n_in:n_in + ci]
        outs = refs[n_in + ci:n_in + ci + n_out]
        c_outs = refs[n_in + ci + n_out:n_in + ci + n_out + co]
        scratch = refs[n_in + ci + n_out + co:n_in + ci + n_out + co + n_sc]
        sems = refs[n_in + ci + n_out + co + n_sc:]
        ids = [pl.program_id(d) for d in range(len(grid))]
        first = functools.reduce(jnp.logical_and, [i == 0 for i in ids])
        last = functools.reduce(jnp.logical_and, [i == g - 1 for i, g in zip(ids, grid)])

        @pl.when(first)
        def _():
            comm.start(c_ins, c_outs, sems)

        body(*ins, *outs, *scratch)

        @pl.when(last)
        def _():
            comm.finish(c_ins, c_outs, sems)

    res = pl.pallas_call(
        wrapped, out_shape=out_shape + tuple(comm.out_shapes), grid=grid, in_specs=list(in_specs) + [ANY] * ci,
        out_specs=out_specs + tuple([ANY] * co), scratch_shapes=list(scratch_shapes) + comm.sems, name=name,
        compiler_params=pltpu.CompilerParams(dimension_semantics=("arbitrary",) * len(grid),
                                             vmem_limit_bytes=VMEM_LIMIT, has_side_effects=True),
    )(*operands, *comm.ins)
    return tuple(res[:n_out]), tuple(res[n_out:])


def _run_exchange(comm, *, name):
    ci = len(comm.ins)

    def body(*refs):
        ins, outs, sems = refs[:ci], refs[ci:2 * ci], refs[2 * ci:]
        comm.start(ins, outs, sems)
        comm.finish(ins, outs, sems)

    return pl.pallas_call(
        body, out_shape=tuple(comm.out_shapes), in_specs=[ANY] * ci, out_specs=tuple([ANY] * ci),
        scratch_shapes=comm.sems, name=name, compiler_params=pltpu.CompilerParams(has_side_effects=True))(*comm.ins)


def _forward_sibling(gathered, *, name):
    n = len(gathered)

    def body(*refs):
        ins, outs = refs[:n], refs[n:2 * n]
        send, recv = refs[2 * n:]
        x, y, c, me, others = _place()
        sends = []
        for a in range(n):
            hr = gathered[a].shape[1] // 2
            for k, (ox, oy) in enumerate(others):
                landed = ins[a].at[2 * ox + oy, pl.ds(c * hr, hr), :]
                cp = pltpu.make_async_remote_copy(
                    src_ref=landed, dst_ref=outs[a].at[2 * ox + oy, pl.ds(c * hr, hr), :], send_sem=send.at[3 * a + k],
                    recv_sem=recv.at[3 * a + k], device_id=(x, y, 1 - c), device_id_type=MESH)
                cp.start()
                sends.append(cp)
        for a in range(n):
            hr = gathered[a].shape[1] // 2
            for k, (ox, oy) in enumerate(others):
                got = outs[a].at[2 * ox + oy, pl.ds((1 - c) * hr, hr), :]
                pltpu.make_async_remote_copy(src_ref=got, dst_ref=got, send_sem=send.at[3 * a + k],
                                             recv_sem=recv.at[3 * a + k], device_id=(x, y, 1 - c),
                                             device_id_type=MESH).wait_recv()
        for cp in sends:
            cp.wait_send()

    return pl.pallas_call(
        body, out_shape=tuple(jax.ShapeDtypeStruct(g.shape, g.dtype) for g in gathered),
        in_specs=[ANY] * n, out_specs=tuple([ANY] * n), input_output_aliases={a: a for a in range(n)},
        scratch_shapes=[pltpu.SemaphoreType.DMA((3 * n,)), pltpu.SemaphoreType.DMA((3 * n,))],
        name=name, compiler_params=pltpu.CompilerParams(has_side_effects=True))(*gathered)


def _own_slot(buf, piece, idx):
    return lax.dynamic_update_slice(buf, piece[None], (idx,) + (0,) * piece.ndim)


def _swap_halves(grads, *, name):
    n = len(grads)

    def body(*refs):
        ins, outs = refs[:n], refs[n:2 * n]
        send, recv = refs[2 * n:]
        x, y, c, me, others = _place()
        cps = []
        for a in range(n):
            cp = pltpu.make_async_remote_copy(src_ref=ins[a].at[:, 1 - c], dst_ref=outs[a], send_sem=send.at[a],
                                              recv_sem=recv.at[a], device_id=(x, y, 1 - c), device_id_type=MESH)
            cp.start()
            cps.append(cp)
        for cp in cps:
            cp.wait()

    return pl.pallas_call(
        body, out_shape=tuple(jax.ShapeDtypeStruct((g.shape[0],) + g.shape[2:], g.dtype) for g in grads),
        in_specs=[ANY] * n, out_specs=tuple([ANY] * n),
        scratch_shapes=[pltpu.SemaphoreType.DMA((n,)), pltpu.SemaphoreType.DMA((n,))],
        name=name, compiler_params=pltpu.CompilerParams(has_side_effects=True))(*grads)


def _join_halves(halves, *, name):
    n = len(halves)

    def body(*refs):
        ins, outs = refs[:n], refs[n:2 * n]
        send, recv = refs[2 * n:]
        x, y, c, me, others = _place()
        sends = []
        for a in range(n):
            cp = pltpu.make_async_remote_copy(src_ref=ins[a], dst_ref=outs[a].at[c], send_sem=send.at[a],
                                              recv_sem=recv.at[a], device_id=(x, y, 1 - c), device_id_type=MESH)
            cp.start()
            sends.append(cp)
        for a in range(n):
            got = outs[a].at[1 - c]
            pltpu.make_async_remote_copy(src_ref=got, dst_ref=got, send_sem=send.at[a], recv_sem=recv.at[a],
                                         device_id=(x, y, 1 - c), device_id_type=MESH).wait_recv()
        for cp in sends:
            cp.wait_send()

    return pl.pallas_call(
        body, out_shape=tuple(jax.ShapeDtypeStruct((2,) + h.shape, h.dtype) for h in halves),
        in_specs=[ANY] * n, out_specs=tuple([ANY] * n),
        scratch_shapes=[pltpu.SemaphoreType.DMA((n,)), pltpu.SemaphoreType.DMA((n,))],
        name=name, compiler_params=pltpu.CompilerParams(has_side_effects=True))(*halves)


def _all_reduce_small(v, *, name):
    R = v.shape[0]

    flips = [(dx, dy, dc) for dx in range(2) for dy in range(2) for dc in range(2) if dx or dy or dc]

    def body(v_ref, o_ref, slots, send, recv):
        x, y, c, me, others = _place()
        mine = 2 * me + c
        slots[mine] = v_ref[...]

        def copy(k, slot):
            dx, dy, dc = flips[k]
            peer = (x + dx - 2 * x * dx, y + dy - 2 * y * dy, c + dc - 2 * c * dc)
            peer_slot = 4 * peer[0] + 2 * peer[1] + peer[2]
            return pltpu.make_async_remote_copy(
                src_ref=v_ref, dst_ref=slots.at[mine if slot == "mine" else peer_slot], send_sem=send.at[k],
                recv_sem=recv.at[k], device_id=peer, device_id_type=MESH)

        for k in range(7):
            copy(k, "mine").start()
        for k in range(7):
            copy(k, "peer").wait_recv()
        for k in range(7):
            copy(k, "mine").wait_send()
        acc = slots[0]
        for j in range(1, 8):
            acc = acc + slots[j]
        o_ref[...] = acc

    vm = pl.BlockSpec(memory_space=pltpu.VMEM)
    return pl.pallas_call(
        body, out_shape=jax.ShapeDtypeStruct(v.shape, F32), in_specs=[vm], out_specs=vm,
        scratch_shapes=[pltpu.VMEM((8, R, LANES), F32), pltpu.SemaphoreType.DMA((7,)),
                        pltpu.SemaphoreType.DMA((7,))],
        name=name, compiler_params=pltpu.CompilerParams(has_side_effects=True))(v)


def _rows(v, n_rows):
    v = v.reshape(-1).astype(F32)
    return jnp.pad(v, (0, n_rows * LANES - v.shape[0])).reshape(n_rows, LANES)


def _mlp_fwd(x_in, g, w_up, w_down, tag):
    h = _rms_fwd(x_in, g, name=f"{tag}_norm")
    u, a = _matmul(h, w_up, b_split=True, epilogue="sqrelu", name=f"{tag}_up")
    x_out = _matmul(a, w_down, epilogue="res", res=x_in, name=f"{tag}_down")
    return x_out, (h, u, a)


def _mlp_bwd(dy, x_in, g, w_up, w_down, saved, tag, comm=None):
    h, u, a = saved
    dw_down = _matmul(a, dy, form="tn", name=f"{tag}_dwdown", comm=comm)
    if comm is not None:
        dw_down, comm_outs = dw_down
    du = _matmul(dy, w_down, form="nt", epilogue="sqrelu_bwd", u=u, out_dtype=BF16, name=f"{tag}_du")
    dw_up = _matmul(h, du, form="tn", out_split=True, name=f"{tag}_dwup")
    dh = _matmul(du, w_up, form="nt", b_split=True, name=f"{tag}_dh")
    dx, dg = _rms_bwd(x_in, g, dh, res=dy, name=f"{tag}_dnorm")
    if comm is not None:
        return dx, dg, dw_up, dw_down, comm_outs
    return dx, dg, dw_up, dw_down


def kernel(x, positions, ln_mix_g, ln_mlp_g, sf_w_in, sf_b_f, fox_q_g, fox_k_g, sf_w_o, mla_w_down, mla_q_a_g, mla_kv_a_g, mla_w_uq, mla_w_ukv, mla_q_g, mla_k_g, mla_w_o, mlp_w_up, mlp_w_down, loss_target, m_ln_mix_g, m_ln_mlp_g, m_sf_w_in, m_sf_b_f, m_fox_q_g, m_fox_k_g, m_sf_w_o, m_mla_w_down, m_mla_q_a_g, m_mla_kv_a_g, m_mla_w_uq, m_mla_w_ukv, m_mla_q_g, m_mla_k_g, m_mla_w_o, m_mlp_w_up, m_mlp_w_down, v_ln_mix_g, v_ln_mlp_g, v_sf_w_in, v_sf_b_f, v_fox_q_g, v_fox_k_g, v_sf_w_o, v_mla_w_down, v_mla_q_a_g, v_mla_kv_a_g, v_mla_w_uq, v_mla_w_ukv, v_mla_q_g, v_mla_k_g, v_mla_w_o, v_mlp_w_up, v_mlp_w_down):
    S, D = x.shape[1], x.shape[2]
    xs, tgt, pos = x[0], loss_target[0], positions[0]
    xi, yi, ci = lax.axis_index("x"), lax.axis_index("y"), lax.axis_index("c")
    chip = 2 * xi + yi
    core = ci.astype(jnp.int32).reshape(1)
    d_ff = mlp_w_up.shape[2] * N_CHIPS
    in_w = sf_w_in.shape[2] * N_CHIPS
    qkv_w = 3 * N_SB * HEAD + 3 * N_FOX * HEAD
    dn_w = mla_w_down.shape[2]
    dn_pad = Q_RANK + KV_RANK + LANES

    def gather_begin(ws):
        shards = [w.astype(BF16) for w in ws]
        return shards, _ChipExchange("gather", shards)

    def gather_end(gathered, shards, name):
        both = _forward_sibling(list(gathered), name=name)
        return [_own_slot(ag, s, chip) for ag, s in zip(both, shards)]

    cols = lambda ag: ag.transpose(1, 0, 2).reshape(ag.shape[1], -1)
    rows = lambda ag: ag.reshape(-1, ag.shape[2])
    s_mix0, ex_mix0 = gather_begin([sf_w_in[0], sf_w_o[0]])
    ag_in, ag_o0 = gather_end(_run_exchange(ex_mix0, name="gather_mix0"), s_mix0, "gather_mix0_sibling")
    w_in_full = cols(ag_in)
    w_qkv = w_in_full[:, :qkv_w]
    w_f = jnp.pad(w_in_full[:, qkv_w:], ((0, 0), (0, LANES - (in_w - qkv_w))))
    w_o0 = rows(ag_o0)
    s_mlp0, ex_mlp0 = gather_begin([mlp_w_up[0], mlp_w_down[0]])
    s_mix1, ex_mix1 = gather_begin([mla_w_down[0], mla_w_uq[0], mla_w_ukv[0], mla_w_o[0]])
    s_mlp1, ex_mlp1 = gather_begin([mlp_w_up[1], mlp_w_down[1]])

    gain_blk = jnp.concatenate([mla_q_a_g, mla_kv_a_g], axis=0) * (ci == 0).astype(F32)
    placed = jnp.zeros((2, N_CHIPS, LANES), F32)
    placed = lax.dynamic_update_slice(placed, gain_blk[:, None, :], (0, chip, 0))
    gains = _all_reduce_small(placed.reshape(2 * N_CHIPS, LANES), name="gather_gains")
    q_a_full = gains[:N_CHIPS].reshape(Q_RANK)
    kv_a_full = gains[N_CHIPS:].reshape(KV_RANK)

    pad_gain = lambda g: jnp.pad(g.reshape(1, QK_DIM), ((0, 0), (0, QK_PAD - QK_DIM)))
    q_g_pad, k_g_pad = pad_gain(mla_q_g), pad_gain(mla_k_g)
    b_pad = _rows(sf_b_f, 1)

    h0 = _rms_fwd(xs, ln_mix_g[0], name="mix0_norm")
    qkv_sb = _matmul(h0, w_qkv, n=3 * N_SB * HEAD, b_n0=0, out_dtype=BF16, name="mix0_qkv_sb")
    qk_fx = _matmul(h0, w_qkv, n=2 * N_FOX * HEAD, b_n0=3 * N_SB * HEAD, name="mix0_qk_fox")
    v_fx = _matmul(h0, w_qkv, n=N_FOX * HEAD, b_n0=(3 * N_SB + 2 * N_FOX) * HEAD, out_dtype=BF16,
                   name="mix0_v_fox")
    fl = _matmul(h0, w_f, name="mix0_forget_logit")
    f_cum = _forget_fwd(fl, b_pad, name="forget_fwd")
    neg_f = (-f_cum[:, :N_FOX]).T.reshape(N_FOX, 1, S)
    q_f = _rms_fwd(qk_fx, fox_q_g[0], c0=0, width=N_FOX * HEAD, gw=HEAD, name="fox_q_norm")
    k_f = _rms_fwd(qk_fx, fox_k_g[0], c0=N_FOX * HEAD, width=N_FOX * HEAD, gw=HEAD, name="fox_k_norm")
    (o_sb, t_sb), landed = _sb_fwd(qkv_sb, name="sb_fwd", comm=ex_mlp0)
    ag_up0, ag_dw0 = gather_end(landed, s_mlp0, "gather_mlp0_sibling")
    (o_fx, lse0), landed = _attn_fwd(q_f, k_f, v_fx, neg_f, n_heads=N_FOX, dqk=HEAD, scale=HEAD ** -0.5,
                                     exact_p=True, name="fox_fwd", comm=ex_mix1)
    ag_dn, ag_uq, ag_ukv, ag_o1 = gather_end(landed, s_mix1, "gather_mix1_sibling")
    w_dn = jnp.pad(rows(ag_dn), ((0, 0), (0, dn_pad - dn_w)))
    w_uq = jnp.pad(cols(ag_uq).reshape(Q_RANK, N_MLA, QK_DIM), ((0, 0), (0, 0), (0, QK_PAD - QK_DIM)))
    w_uq = w_uq.reshape(Q_RANK, N_MLA * QK_PAD)
    w_ukv = cols(ag_ukv)
    w_o1 = rows(ag_o1)
    o0 = jnp.concatenate([o_sb, o_fx], axis=1)
    x1 = _matmul(o0, w_o0, epilogue="res", res=xs, name="mix0_out")
    x2, mlp0 = _mlp_fwd(x1, ln_mlp_g[0], ag_up0, rows(ag_dw0), "mlp0")

    h2 = _rms_fwd(x2, ln_mix_g[1], name="mix1_norm")
    down = _matmul(h2, w_dn, name="mix1_down")
    c_q = _rms_fwd(down, q_a_full, c0=0, width=Q_RANK, name="mix1_q_a_norm")
    c_kv = _rms_fwd(down, kv_a_full, c0=Q_RANK, width=KV_RANK, name="mix1_kv_a_norm")
    q_raw = _matmul(c_q, w_uq, name="mix1_uq")
    kv_raw = _matmul(c_kv, w_ukv, name="mix1_ukv")
    half = ROPE // 2
    inv_freq = ROPE_THETA ** (-jnp.arange(half, dtype=F32) / half)
    ang = pos.astype(F32)[:, None] * inv_freq
    table = lambda t: jnp.pad(jnp.concatenate([t, t], axis=1), ((0, 0), (0, LANES - ROPE)))
    cos_t, sin_t = table(jnp.cos(ang)), table(jnp.sin(ang))
    q_pad, k_pad, v1 = _mla_prep_fwd(q_raw, kv_raw, down, cos_t, sin_t, q_g_pad, k_g_pad, name="mla_prep_fwd")
    (o1, lse1), landed = _attn_fwd(q_pad, k_pad, v1, None, n_heads=N_MLA, dqk=QK_PAD, scale=QK_DIM ** -0.5,
                                   name="mla_fwd", comm=ex_mlp1)
    ag_up1, ag_dw1 = gather_end(landed, s_mlp1, "gather_mlp1_sibling")
    w_up = [ag_up0, ag_up1]
    w_dw = [rows(ag_dw0), rows(ag_dw1)]
    x3 = _matmul(o1, w_o1, epilogue="res", res=x2, name="mix1_out")
    x4, mlp1 = _mlp_fwd(x3, ln_mlp_g[1], w_up[1], w_dw[1], "mlp1")

    dx4, loss_local = _loss_head(x4, tgt, name="loss_head")
    loss = lax.psum(loss_local, ("x", "y", "c"))

    by_cols = lambda g: g.reshape(g.shape[0], N_CHIPS, -1).transpose(1, 0, 2)
    by_rows = lambda g: g.reshape(N_CHIPS, g.shape[0] // N_CHIPS, g.shape[1])
    halves = lambda g: g.reshape(N_CHIPS, 2, g.shape[1] // 2, g.shape[2])

    def reduce_begin(grads, tags, name):
        from_sibling = _swap_halves(grads, name=f"{name}_swap")
        parts = [_add_sibling(g, r, core, name=f"add_sibling_{t}") for g, r, t in zip(grads, from_sibling, tags)]
        return parts, _ChipExchange("scatter", parts)

    def reduce_end(slots, parts, tags, name):
        slots = [_own_slot(s, lax.dynamic_index_in_dim(p, chip, 0, keepdims=False), chip)
                 for s, p in zip(slots, parts)]
        mine = [_add_chips(s, name=f"add_chips_{t}") for s, t in zip(slots, tags)]
        joined = [_own_slot(j, m, ci) for j, m in zip(_join_halves(mine, name=f"{name}_join"), mine)]
        return [j.reshape(2 * j.shape[1], j.shape[2]) for j in joined]

    dx3, dg_mlp1, dw_up1, dw_dw1 = _mlp_bwd(dx4, x3, ln_mlp_g[1], w_up[1], w_dw[1], mlp1, "mlp1")
    tags_mlp1 = ["w_up1", "w_dw1"]
    p_mlp1, ex = reduce_begin([halves(dw_up1), halves(by_rows(dw_dw1))], tags_mlp1, "reduce_mlp1")

    dw_o1 = _matmul(o1, dx3, form="tn", name="mix1_dwo")
    do1 = _matmul(dx3, w_o1, form="nt", name="mix1_do")
    (dq_pad, dk_pad, dv1), slots = _attn_bwd(q_pad, k_pad, v1, None, o1, lse1, do1, n_heads=N_MLA, dqk=QK_PAD,
                                             scale=QK_DIM ** -0.5, name="mla_bwd", comm=ex)
    gs_up1, gs_dw1 = reduce_end(slots, p_mlp1, tags_mlp1, "reduce_mlp1")
    dq_raw, dkv_raw, dpe, dg_q, dg_k = _mla_prep_bwd(dq_pad, dk_pad, dv1, q_raw, kv_raw, down, cos_t, sin_t,
                                                     q_g_pad, k_g_pad, name="mla_prep_bwd")
    dw_uq = _matmul(c_q, dq_raw, form="tn", name="mix1_dwuq")
    dc_q = _matmul(dq_raw, w_uq, form="nt", name="mix1_dcq")
    dw_ukv = _matmul(c_kv, dkv_raw, form="tn", name="mix1_dwukv")
    dc_kv = _matmul(dkv_raw, w_ukv, form="nt", name="mix1_dckv")
    d_cq, dg_qa = _rms_bwd(down, q_a_full, dc_q, c0=0, width=Q_RANK, name="mix1_q_a_dnorm")
    d_ckv, dg_kva = _rms_bwd(down, kv_a_full, dc_kv, c0=Q_RANK, width=KV_RANK, name="mix1_kv_a_dnorm")
    d_down = jnp.concatenate([d_cq, d_ckv, dpe], axis=1)
    dw_dn = _matmul(h2, d_down, form="tn", name="mix1_dwdown")
    dh2 = _matmul(d_down, w_dn, form="nt", name="mix1_dh")
    dx2, dg_mix1 = _rms_bwd(x2, ln_mix_g[1], dh2, res=dx3, name="mix1_dnorm")
    g_uq = dw_uq.reshape(Q_RANK, N_MLA, QK_PAD)[:, :, :QK_DIM].reshape(Q_RANK, N_MLA * QK_DIM)
    tags_mix1 = ["w_dn", "w_uq", "w_ukv", "w_o1"]
    p_mix1, ex = reduce_begin([halves(by_rows(dw_dn[:, :dn_w])), halves(by_cols(g_uq)), halves(by_cols(dw_ukv)),
                               halves(by_rows(dw_o1))], tags_mix1, "reduce_mix1")

    dx1, dg_mlp0, dw_up0, dw_dw0, slots = _mlp_bwd(dx2, x1, ln_mlp_g[0], w_up[0], w_dw[0], mlp0, "mlp0", comm=ex)
    gs_dn, gs_uq, gs_ukv, gs_o1 = reduce_end(slots, p_mix1, tags_mix1, "reduce_mix1")
    tags_mlp0 = ["w_up0", "w_dw0"]
    p_mlp0, ex = reduce_begin([halves(dw_up0), halves(by_rows(dw_dw0))], tags_mlp0, "reduce_mlp0")

    dw_o0 = _matmul(o0, dx1, form="tn", name="mix0_dwo")
    do0 = _matmul(dx1, w_o0, form="nt", name="mix0_do")
    (dq_sb, dk_sb, dv_sb), slots = _sb_bwd(qkv_sb, do0, t_sb, do_off=0, name="sb_bwd", comm=ex)
    gs_up0, gs_dw0 = reduce_end(slots, p_mlp0, tags_mlp0, "reduce_mlp0")
    dq_f, dk_f, dv_fx, dbias, drow = _attn_bwd(q_f, k_f, v_fx, neg_f, o_fx, lse0, do0, n_heads=N_FOX, dqk=HEAD,
                                               scale=HEAD ** -0.5, do_off=N_SB, name="fox_bwd")
    dq_fx, dg_fq = _rms_bwd(qk_fx, fox_q_g[0], dq_f, c0=0, width=N_FOX * HEAD, gw=HEAD, name="fox_q_dnorm")
    dk_fx, dg_fk = _rms_bwd(qk_fx, fox_k_g[0], dk_f, c0=N_FOX * HEAD, width=N_FOX * HEAD, gw=HEAD,
                            name="fox_k_dnorm")
    d_fcum = jnp.pad((drow[:, :, 0] - dbias.reshape(N_FOX, S)).T, ((0, 0), (0, LANES - N_FOX)))
    dfl, db_f = _forget_bwd(fl, b_pad, d_fcum, name="forget_bwd")
    dproj = jnp.concatenate([dq_sb, dk_sb, dv_sb, dq_fx, dk_fx, dv_fx], axis=1).astype(BF16)
    dw_qkv = _matmul(h0, dproj, form="tn", name="mix0_dwqkv")
    dw_f = _matmul(h0, dfl, form="tn", name="mix0_dwf")
    dh0 = _matmul(dfl, w_f, form="nt", name="mix0_dh_f")
    dh0 = _matmul(dproj, w_qkv, form="nt", epilogue="res", res=dh0, name="mix0_dh")
    grad_x, dg_mix0 = _rms_bwd(xs, ln_mix_g[0], dh0, res=dx1, name="mix0_dnorm")

    g_in = jnp.concatenate([dw_qkv, dw_f[:, :in_w - qkv_w]], axis=1)
    tags_mix0 = ["w_in", "w_o0"]
    p_mix0, ex = reduce_begin([halves(by_cols(g_in)), halves(by_rows(dw_o0))], tags_mix0, "reduce_mix0")
    gs_in, gs_o0 = reduce_end(_run_exchange(ex, name="reduce_mix0_chips"), p_mix0, tags_mix0, "reduce_mix0")
    gs_up = jnp.concatenate([gs_up0, gs_up1], axis=0)
    gs_dw = jnp.concatenate([gs_dw0, gs_dw1], axis=0)

    ln_rows = D // LANES
    small = jnp.concatenate([
        _rows(dg_mix0, ln_rows), _rows(dg_mix1, ln_rows), _rows(dg_mlp0, ln_rows), _rows(dg_mlp1, ln_rows),
        _rows(db_f, 8), _rows(dg_fq, 8), _rows(dg_fk, 8), _rows(dg_qa, 8), _rows(dg_kva, 8), _rows(dg_q, 8),
        _rows(dg_k, 8)], axis=0)
    small = _all_reduce_small(small, name="reduce_small")
    flat = lambda r0, nr, n: small[r0:r0 + nr].reshape(-1)[:n]
    r0 = 4 * ln_rows
    g_ln_mix = jnp.stack([flat(0, ln_rows, D), flat(ln_rows, ln_rows, D)])
    g_ln_mlp = jnp.stack([flat(2 * ln_rows, ln_rows, D), flat(3 * ln_rows, ln_rows, D)])
    g_b_f = flat(r0, 8, N_FOX)[None]
    g_fq, g_fk = flat(r0 + 8, 8, HEAD)[None], flat(r0 + 16, 8, HEAD)[None]
    g_qa = lax.dynamic_slice(flat(r0 + 24, 8, Q_RANK), (chip * LANES,), (LANES,))[None]
    g_kva = lax.dynamic_slice(flat(r0 + 32, 8, KV_RANK), (chip * LANES,), (LANES,))[None]
    g_q, g_k = flat(r0 + 40, 8, QK_DIM)[None], flat(r0 + 48, 8, QK_DIM)[None]

    def pack_small(ln_mix, ln_mlp, *rest):
        return jnp.concatenate([_rows(ln_mix, 2 * ln_rows), _rows(ln_mlp, 2 * ln_rows)] + [_rows(t, 8) for t in rest],
                               axis=0)

    def unpack_small(p):
        f = lambda r, nr, shape: p[r:r + nr].reshape(-1)[:int(np.prod(shape))].reshape(shape)
        shapes = [(1, N_FOX), (1, HEAD), (1, HEAD), (1, LANES), (1, LANES), (1, QK_DIM), (1, QK_DIM)]
        return (f(0, 2 * ln_rows, (2, D)), f(2 * ln_rows, 2 * ln_rows, (2, D)),
                *[f(r0 + 8 * i, 8, shp) for i, shp in enumerate(shapes)])

    small_out = _adamw(
        pack_small(ln_mix_g, ln_mlp_g, sf_b_f, fox_q_g, fox_k_g, mla_q_a_g, mla_kv_a_g, mla_q_g, mla_k_g),
        pack_small(g_ln_mix, g_ln_mlp, g_b_f, g_fq, g_fk, g_qa, g_kva, g_q, g_k),
        pack_small(m_ln_mix_g, m_ln_mlp_g, m_sf_b_f, m_fox_q_g, m_fox_k_g, m_mla_q_a_g, m_mla_kv_a_g, m_mla_q_g,
                   m_mla_k_g),
        pack_small(v_ln_mix_g, v_ln_mlp_g, v_sf_b_f, v_fox_q_g, v_fox_k_g, v_mla_q_a_g, v_mla_kv_a_g, v_mla_q_g,
                   v_mla_k_g), name="adamw_small")
    d_small, m_small, v_small = [unpack_small(p) for p in small_out]

    def big(w, g, m, v, tag):
        shp = w.shape
        two_d = lambda t: t.reshape(-1, shp[-1])
        d, mn, vn = _adamw(two_d(w), g, two_d(m), two_d(v), name=f"adamw_{tag}")
        return g.reshape(shp), d.reshape(shp), mn.reshape(shp), vn.reshape(shp)

    r_in = big(sf_w_in, gs_in, m_sf_w_in, v_sf_w_in, "w_in")
    r_o0 = big(sf_w_o, gs_o0, m_sf_w_o, v_sf_w_o, "w_o0")
    r_dn = big(mla_w_down, gs_dn, m_mla_w_down, v_mla_w_down, "w_dn")
    r_uq = big(mla_w_uq, gs_uq, m_mla_w_uq, v_mla_w_uq, "w_uq")
    r_ukv = big(mla_w_ukv, gs_ukv, m_mla_w_ukv, v_mla_w_ukv, "w_ukv")
    r_o1 = big(mla_w_o, gs_o1, m_mla_w_o, v_mla_w_o, "w_o1")
    r_up = big(mlp_w_up, gs_up, m_mlp_w_up, v_mlp_w_up, "w_up")
    r_dw = big(mlp_w_down, gs_dw, m_mlp_w_down, v_mlp_w_down, "w_dw")

    g_small = (g_ln_mix, g_ln_mlp, g_b_f, g_fq, g_fk, g_qa, g_kva, g_q, g_k)

    def ordered(k, sm):
        return (sm[0], sm[1], r_in[k], sm[2], sm[3], sm[4], r_o0[k], r_dn[k], sm[5], sm[6], r_uq[k], r_ukv[k],
                sm[7], sm[8], r_o1[k], r_up[k], r_dw[k])

    return (loss, grad_x[None], *ordered(0, g_small), *ordered(1, d_small), *ordered(2, m_small),
            *ordered(3, v_small))
```

```python
import functools

import numpy as np
import jax
import jax.numpy as jnp
from jax import lax
from jax.experimental import pallas as pl
from jax.experimental.pallas import tpu as pltpu

F32 = jnp.float32
BF16 = jnp.bfloat16
MESH = pl.DeviceIdType.MESH

EPS = 1e-6
HEAD = 128
N_SB = 8
N_FOX = 8
N_MLA = 16
Q_RANK = 512
KV_RANK = 512
NOPE = 128
ROPE = 64
QK_DIM = NOPE + ROPE
QK_PAD = 256
ROPE_THETA = 10000.0
N_CHIPS = 4

ADAM_LR = 0.001
ADAM_B1 = 0.9
ADAM_B2 = 0.999
ADAM_EPS = 1e-08
ADAM_WD = 0.01
ADAM_STEP = 10

VMEM_LIMIT = 56 * 1024 * 1024
LANES = 128
NEG = -1e30


def _cp(*sem):
    return pltpu.CompilerParams(dimension_semantics=sem, vmem_limit_bytes=VMEM_LIMIT)


def _pick(dim, target):
    if dim <= target:
        return dim
    t = (target // LANES) * LANES
    while t >= LANES:
        if dim % t == 0:
            return t
        t -= LANES
    raise ValueError(f"no tile for {dim}")


NT_DIMS = (((1,), (1,)), ((), ()))
TN_DIMS = (((0,), (0,)), ((), ()))


def _dot(a, b):
    return jnp.dot(a, b, preferred_element_type=F32)


def _dot_nt(a, b):
    return lax.dot_general(a, b, NT_DIMS, preferred_element_type=F32)


def _dot_tn(a, b):
    return lax.dot_general(a, b, TN_DIMS, preferred_element_type=F32)


def _matmul(a, b, *, name, form="nn", out_dtype=F32, n=None, b_n0=0, b_split=False,
            out_split=False, epilogue="plain", res=None, u=None, tm=1024, tn=1024, tk=2048, comm=None):
    if form == "tn":
        K, M = a.shape
    else:
        M, K = a.shape
    if b_split:
        if form == "nt":
            nb_full, kb_full = b.shape[1], b.shape[2] * N_CHIPS
        else:
            kb_full, nb_full = b.shape[1], b.shape[2] * N_CHIPS
    elif form == "nt":
        nb_full, kb_full = b.shape
    else:
        kb_full, nb_full = b.shape
    assert kb_full == K, (name, a.shape, b.shape)
    N = nb_full if n is None else n
    if a.dtype != BF16 or b.dtype != BF16:
        tk = max(tk // 2, LANES)
    tm, tn, tk = _pick(M, tm), _pick(N, tn), _pick(K, tk)
    if b_split:
        per_chip = (b.shape[2])
        if form == "nt":
            tk = _pick(per_chip, tk)
        else:
            tn = _pick(per_chip, tn)
    if out_split:
        tn = _pick(N // N_CHIPS, tn)
    assert b_n0 % tn == 0
    nb0 = b_n0 // tn
    nk = K // tk
    grid = (M // tm, N // tn, nk)

    if form == "tn":
        a_spec = pl.BlockSpec((tk, tm), lambda i, j, k: (k, i))
    else:
        a_spec = pl.BlockSpec((tm, tk), lambda i, j, k: (i, k))
    if b_split:
        if form == "nt":
            kc = b.shape[2] // tk
            b_spec = pl.BlockSpec((None, tn, tk), lambda i, j, k: (k // kc, j, k % kc))
        else:
            nc = b.shape[2] // tn
            b_spec = pl.BlockSpec((None, tk, tn), lambda i, j, k: (j // nc, k, j % nc))
    elif form == "nt":
        b_spec = pl.BlockSpec((tn, tk), lambda i, j, k: (j + nb0, k))
    else:
        b_spec = pl.BlockSpec((tk, tn), lambda i, j, k: (k, j + nb0))
    mn_spec = pl.BlockSpec((tm, tn), lambda i, j, k: (i, j))
    if out_split:
        oc = (N // N_CHIPS) // tn
        out_spec = pl.BlockSpec((None, tm, tn), lambda i, j, k: (j // oc, i, j % oc))
        out_shape = jax.ShapeDtypeStruct((N_CHIPS, M, N // N_CHIPS), out_dtype)
    else:
        out_spec = mn_spec
        out_shape = jax.ShapeDtypeStruct((M, N), out_dtype)

    in_specs = [a_spec, b_spec]
    operands = [a, b]
    out_specs = (out_spec,)
    out_shape = (out_shape,)
    if epilogue == "res":
        in_specs.append(mn_spec)
        operands.append(res)
    elif epilogue == "sqrelu_bwd":
        in_specs.append(mn_spec)
        operands.append(u)
    elif epilogue == "sqrelu":
        out_specs = (mn_spec, mn_spec)
        out_shape = (jax.ShapeDtypeStruct((M, N), F32), jax.ShapeDtypeStruct((M, N), BF16))

    def finish(refs, r):
        if epilogue == "plain":
            refs[2][...] = r.astype(out_dtype)
        elif epilogue == "res":
            refs[3][...] = (refs[2][...] + r).astype(out_dtype)
        elif epilogue == "sqrelu":
            refs[2][...] = r
            p = jnp.maximum(r, 0.0)
            refs[3][...] = (p * p).astype(BF16)
        else:
            refs[3][...] = (r * (2.0 * jnp.maximum(refs[2][...], 0.0))).astype(out_dtype)

    def body(*refs):
        at = refs[0][...].astype(BF16)
        bt = refs[1][...].astype(BF16)
        if form == "nn":
            part = _dot(at, bt)
        elif form == "nt":
            part = _dot_nt(at, bt)
        else:
            part = _dot_tn(at, bt)
        if nk == 1:
            finish(refs, part)
            return
        acc = refs[-1]
        k = pl.program_id(2)

        @pl.when(k == 0)
        def _():
            acc[...] = part

        @pl.when(jnp.logical_and(k > 0, k < nk - 1))
        def _():
            acc[...] += part

        @pl.when(k == nk - 1)
        def _():
            finish(refs, acc[...] + part)

    outs, comm_outs = _hosted_call(
        body, grid=grid, in_specs=in_specs, out_specs=out_specs, out_shape=out_shape,
        scratch_shapes=[] if nk == 1 else [pltpu.VMEM((tm, tn), F32)], operands=operands, name=name,
        sem=("parallel", "parallel", "arbitrary"), comm=comm)
    result = outs if epilogue == "sqrelu" else outs[0]
    return result if comm is None else (result, comm_outs)


def _rms_fwd(x, g, *, name, c0=0, width=None, gw=None, tr=256):
    R, ctot = x.shape
    C = ctot if width is None else width
    gw = C if gw is None else gw
    assert c0 % C == 0 and C % gw == 0
    tr = _pick(R, tr)
    cb = c0 // C
    ng = C // gw

    def body(x_ref, g_ref, o_ref):
        gv = g_ref[...]
        for gi in range(ng):
            cols = slice(gi * gw, (gi + 1) * gw)
            xs = x_ref[:, cols]
            ms = jnp.sum(xs * xs, axis=-1, keepdims=True) * (1.0 / gw)
            o_ref[:, cols] = ((xs * lax.rsqrt(ms + EPS)) * gv).astype(o_ref.dtype)

    return pl.pallas_call(
        body, out_shape=jax.ShapeDtypeStruct((R, C), BF16), grid=(R // tr,),
        in_specs=[pl.BlockSpec((tr, C), lambda i: (i, cb)), pl.BlockSpec((1, gw), lambda i: (0, 0))],
        out_specs=pl.BlockSpec((tr, C), lambda i: (i, 0)), name=name,
        compiler_params=_cp("parallel"))(x, g.reshape(1, gw).astype(F32))


def _rms_bwd(x, g, dy, *, name, res=None, c0=0, width=None, gw=None, tr=256, bf16_copy=False):
    bf16_copy = int(bf16_copy)
    R, ctot = x.shape
    C = ctot if width is None else width
    gw = C if gw is None else gw
    tr = _pick(R, tr)
    cb = c0 // C
    ng = C // gw
    nsteps = R // tr
    row_spec = pl.BlockSpec((tr, C), lambda i: (i, 0))
    in_specs = [pl.BlockSpec((tr, C), lambda i: (i, cb)), pl.BlockSpec((1, gw), lambda i: (0, 0)), row_spec]
    operands = [x, g.reshape(1, gw).astype(F32), dy]
    if res is not None:
        in_specs.append(row_spec)
        operands.append(res)

    def body(*refs):
        x_ref, g_ref, dy_ref = refs[:3]
        res_ref = refs[3] if res is not None else None
        dx_ref, dg_ref = refs[-3 - bf16_copy], refs[-2]
        acc = refs[-1]
        i = pl.program_id(0)

        @pl.when(i == 0)
        def _():
            acc[...] = jnp.zeros_like(acc)

        gv = g_ref[...]
        for gi in range(ng):
            cols = slice(gi * gw, (gi + 1) * gw)
            xs = x_ref[:, cols]
            dys = dy_ref[:, cols].astype(F32)
            rstd = lax.rsqrt(jnp.sum(xs * xs, axis=-1, keepdims=True) * (1.0 / gw) + EPS)
            xh = xs * rstd
            gdy = dys * gv
            m = jnp.sum(gdy * xh, axis=-1, keepdims=True) * (1.0 / gw)
            dx = rstd * (gdy - xh * m)
            if res_ref is not None:
                dx = dx + res_ref[:, cols]
            dx_ref[:, cols] = dx
            if bf16_copy:
                refs[-3][:, cols] = dx.astype(BF16)
            acc[...] += jnp.sum((dys * xh).reshape(tr // 8, 8, gw), axis=0)

        @pl.when(i == nsteps - 1)
        def _():
            dg_ref[...] = jnp.sum(acc[...], axis=0, keepdims=True)

    out_shape = [jax.ShapeDtypeStruct((R, C), F32)] + [jax.ShapeDtypeStruct((R, C), BF16)] * bf16_copy
    outs = pl.pallas_call(
        body, out_shape=tuple(out_shape + [jax.ShapeDtypeStruct((1, gw), F32)]),
        grid=(nsteps,), in_specs=in_specs,
        out_specs=tuple([row_spec] * len(out_shape) + [pl.BlockSpec((1, gw), lambda i: (0, 0))]),
        scratch_shapes=[pltpu.VMEM((8, gw), F32)], name=name,
        compiler_params=_cp("arbitrary"))(*operands)
    return (*outs[:-1], outs[-1][0])


def _split3(x):
    hi = x.astype(BF16)
    r1 = x - hi.astype(F32)
    mid = r1.astype(BF16)
    lo = (r1 - mid.astype(F32)).astype(BF16)
    return hi, mid, lo


def _log_sigmoid(z):
    return jnp.minimum(z, 0.0) - jnp.log(1.0 + jnp.exp(-jnp.abs(z)))


def _forget_fwd(fl, b, *, name, tb=512):
    S = fl.shape[0]
    tb = _pick(S, tb)

    def body(fl_ref, b_ref, f_ref, carry):
        i = pl.program_id(0)

        @pl.when(i == 0)
        def _():
            carry[...] = jnp.zeros_like(carry)

        lf = _log_sigmoid(fl_ref[...] + b_ref[...])
        r = lax.broadcasted_iota(jnp.int32, (tb, tb), 0)
        c = lax.broadcasted_iota(jnp.int32, (tb, tb), 1)
        tri = (c <= r).astype(BF16)
        hi, mid, lo = _split3(lf)
        cs = _dot(tri, hi) + _dot(tri, mid) + _dot(tri, lo)
        f_ref[...] = cs + carry[...]
        carry[...] += jnp.sum(lf, axis=0, keepdims=True)

    return pl.pallas_call(
        body, out_shape=jax.ShapeDtypeStruct((S, LANES), F32), grid=(S // tb,),
        in_specs=[pl.BlockSpec((tb, LANES), lambda i: (i, 0)), pl.BlockSpec((1, LANES), lambda i: (0, 0))],
        out_specs=pl.BlockSpec((tb, LANES), lambda i: (i, 0)),
        scratch_shapes=[pltpu.VMEM((1, LANES), F32)], name=name,
        compiler_params=_cp("arbitrary"))(fl, b)


def _forget_bwd(fl, b, dF, *, name, tb=512):
    S = fl.shape[0]
    tb = _pick(S, tb)
    nb = S // tb

    def body(fl_ref, b_ref, df_ref, dfl_ref, db_ref, carry, acc):
        i = pl.program_id(0)

        @pl.when(i == 0)
        def _():
            carry[...] = jnp.zeros_like(carry)
            acc[...] = jnp.zeros_like(acc)

        d = df_ref[...]
        r = lax.broadcasted_iota(jnp.int32, (tb, tb), 0)
        c = lax.broadcasted_iota(jnp.int32, (tb, tb), 1)
        tri = (c >= r).astype(BF16)
        hi, mid, lo = _split3(d)
        rc = _dot(tri, hi) + _dot(tri, mid) + _dot(tri, lo) + carry[...]
        z = fl_ref[...] + b_ref[...]
        dfl = rc * jnp.exp(_log_sigmoid(-z))
        dfl_ref[...] = dfl
        carry[...] += jnp.sum(d, axis=0, keepdims=True)
        acc[...] += jnp.sum(dfl, axis=0, keepdims=True)

        @pl.when(i == nb - 1)
        def _():
            db_ref[...] = acc[...]

    rev = lambda i: (nb - 1 - i, 0)
    dfl, db = pl.pallas_call(
        body, out_shape=(jax.ShapeDtypeStruct((S, LANES), F32), jax.ShapeDtypeStruct((1, LANES), F32)),
        grid=(nb,),
        in_specs=[pl.BlockSpec((tb, LANES), rev), pl.BlockSpec((1, LANES), lambda i: (0, 0)),
                  pl.BlockSpec((tb, LANES), rev)],
        out_specs=(pl.BlockSpec((tb, LANES), rev), pl.BlockSpec((1, LANES), lambda i: (0, 0))),
        scratch_shapes=[pltpu.VMEM((1, LANES), F32), pltpu.VMEM((1, LANES), F32)], name=name,
        compiler_params=_cp("arbitrary"))(fl, b, dF)
    return dfl, db[0]


def _tri(tk, rel):
    r = lax.broadcasted_iota(jnp.int32, (tk, tk), 0)
    c = lax.broadcasted_iota(jnp.int32, (tk, tk), 1)
    m = {"gt": r > c, "le": r <= c, "lt": r < c}[rel]
    return m.astype(BF16)


def _split2(x):
    hi = x.astype(BF16)
    return hi, (x - hi.astype(F32)).astype(BF16)


HEADS_PER_STEP = 2


def _diag_mask(tq, strict):
    r = lax.broadcasted_iota(jnp.int32, (tq, tq), 0)
    c = lax.broadcasted_iota(jnp.int32, (tq, tq), 1)
    return c < r if strict else c <= r


def _sb_fwd(qkv, *, name, n_heads=N_SB, q_off=0, k_off=N_SB, v_off=2 * N_SB, tq=256, hp=HEADS_PER_STEP,
            comm=None):
    S = qkv.shape[0]
    tq = _pick(S, tq)
    tk = tq
    scale = HEAD ** -0.5
    nq = S // tq
    assert n_heads % hp == 0 and q_off % hp == 0 and k_off % hp == 0 and v_off % hp == 0

    def body(q_ref, k_ref, v_ref, o_ref, t_ref, c_sc, acc_sc):
        qi = pl.program_id(1)
        c_sc[...] = jnp.zeros_like(c_sc)
        acc_sc[...] = jnp.zeros_like(acc_sc)
        gt = _tri(tk, "gt")

        def tile(hh, j, diag):
            cs = slice(hh * HEAD, (hh + 1) * HEAD)
            rows = pl.ds(pl.multiple_of(j * tk, tk), tk)
            z = _dot_nt(q_ref[:, cs], k_ref[rows, cs]) * scale
            sp = jnp.log(1.0 + jnp.exp(-jnp.abs(z)))
            la = jnp.minimum(z, 0.0) - sp
            lb = -jnp.maximum(z, 0.0) - sp
            if diag:
                strict = _diag_mask(tq, True)
                lb = jnp.where(strict, lb, 0.0)
            hi, lo = _split2(lb)
            suffix = _dot(hi, gt) + _dot(lo, gt)
            w = jnp.exp(la + suffix + c_sc[hh])
            if diag:
                w = jnp.where(strict, w, 0.0)
            acc_sc[hh] += _dot(w.astype(BF16), v_ref[rows, cs])
            c_sc[hh] += jnp.sum(lb, axis=1, keepdims=True)

        for hh in range(hp):
            tile(hh, qi, True)

        def step(it, carry):
            for hh in range(hp):
                tile(hh, qi - 1 - it, False)
            return carry

        lax.fori_loop(0, qi, step, 0)
        for hh in range(hp):
            o_ref[:, hh * HEAD:(hh + 1) * HEAD] = acc_sc[hh]
            t_ref[hh] = jnp.broadcast_to(c_sc[hh], (tq, LANES))

    w = hp * HEAD
    head_blk = lambda off: pl.BlockSpec((S, w), lambda h, i: (0, h + off // hp))
    outs, comm_outs = _hosted_call(
        body,
        out_shape=(jax.ShapeDtypeStruct((S, n_heads * HEAD), F32),
                   jax.ShapeDtypeStruct((n_heads, S, LANES), F32)),
        grid=(n_heads // hp, nq),
        in_specs=[pl.BlockSpec((tq, w), lambda h, i: (i, h + q_off // hp)), head_blk(k_off), head_blk(v_off)],
        out_specs=(pl.BlockSpec((tq, w), lambda h, i: (i, h)),
                   pl.BlockSpec((hp, tq, LANES), lambda h, i: (h, i, 0))),
        scratch_shapes=[pltpu.VMEM((hp, tq, 1), F32), pltpu.VMEM((hp, tq, HEAD), F32)], name=name,
        sem=("parallel", "arbitrary"), operands=(qkv, qkv, qkv), comm=comm)
    return outs if comm is None else (outs, comm_outs)


def _sb_bwd(qkv, do, tstat, *, name, n_heads=N_SB, q_off=0, k_off=N_SB, v_off=2 * N_SB, do_off=0, tq=256,
            hp=HEADS_PER_STEP, comm=None):
    S = qkv.shape[0]
    tq = _pick(S, tq)
    tk = tq
    scale = HEAD ** -0.5
    nq = S // tq
    assert n_heads % hp == 0 and q_off % hp == 0 and k_off % hp == 0 and v_off % hp == 0 and do_off % hp == 0

    def body(q_ref, k_ref, v_ref, do_ref, t_ref, dq_ref, dk_ref, dv_ref, p_sc, r_sc, dq_sc):
        qi = pl.program_id(1)

        @pl.when(qi == 0)
        def _():
            dk_ref[...] = jnp.zeros_like(dk_ref)
            dv_ref[...] = jnp.zeros_like(dv_ref)

        p_sc[...] = jnp.zeros_like(p_sc)
        r_sc[...] = jnp.zeros_like(r_sc)
        dq_sc[...] = jnp.zeros_like(dq_sc)
        le = _tri(tk, "le")
        lt = _tri(tk, "lt")

        def tile(hh, j, diag):
            cs = slice(hh * HEAD, (hh + 1) * HEAD)
            rows = pl.ds(pl.multiple_of(j * tk, tk), tk)
            q = q_ref[:, cs]
            do_b = do_ref[:, cs].astype(BF16)
            kb = k_ref[rows, cs]
            z = _dot_nt(q, kb) * scale
            sp = jnp.log(1.0 + jnp.exp(-jnp.abs(z)))
            la = jnp.minimum(z, 0.0) - sp
            lb = -jnp.maximum(z, 0.0) - sp
            if diag:
                strict = _diag_mask(tq, True)
                lb = jnp.where(strict, lb, 0.0)
            hi, lo = _split2(lb)
            prefix = _dot(hi, le) + _dot(lo, le) + p_sc[hh]
            w = jnp.exp(la + t_ref[hh, :, 0:1] - prefix)
            if diag:
                w = jnp.where(strict, w, 0.0)
            r = w * _dot_nt(do_b, v_ref[rows, cs])
            hi, lo = _split2(r)
            rex = _dot(hi, lt) + _dot(lo, lt) + r_sc[hh]
            beta = jnp.exp(la)
            dz = (r * (1.0 - beta) - rex * beta) * scale
            if diag:
                dz = jnp.where(strict, dz, 0.0)
            dzb = dz.astype(BF16)
            dq_sc[hh] += _dot(dzb, kb)
            dk_ref[rows, cs] += _dot_tn(dzb, q)
            dv_ref[rows, cs] += _dot_tn(w.astype(BF16), do_b)
            p_sc[hh] += jnp.sum(lb, axis=1, keepdims=True)
            r_sc[hh] += jnp.sum(r, axis=1, keepdims=True)

        def step(j, carry):
            for hh in range(hp):
                tile(hh, j, False)
            return carry

        lax.fori_loop(0, qi, step, 0)
        for hh in range(hp):
            tile(hh, qi, True)
            dq_ref[:, hh * HEAD:(hh + 1) * HEAD] = dq_sc[hh]

    w = hp * HEAD
    head_blk = lambda off: pl.BlockSpec((S, w), lambda h, i: (0, h + off // hp))
    out_head = pl.BlockSpec((S, w), lambda h, i: (0, h))
    out_sd = jax.ShapeDtypeStruct((S, n_heads * HEAD), F32)
    outs, comm_outs = _hosted_call(
        body, out_shape=(out_sd, out_sd, out_sd), grid=(n_heads // hp, nq),
        in_specs=[pl.BlockSpec((tq, w), lambda h, i: (i, h + q_off // hp)), head_blk(k_off), head_blk(v_off),
                  pl.BlockSpec((tq, w), lambda h, i: (i, h + do_off // hp)),
                  pl.BlockSpec((hp, tq, LANES), lambda h, i: (h, i, 0))],
        out_specs=(pl.BlockSpec((tq, w), lambda h, i: (i, h)), out_head, out_head),
        scratch_shapes=[pltpu.VMEM((hp, tq, 1), F32), pltpu.VMEM((hp, tq, 1), F32), pltpu.VMEM((hp, tq, HEAD), F32)],
        name=name, sem=("parallel", "arbitrary"), operands=(qkv, qkv, qkv, do, tstat), comm=comm)
    return outs if comm is None else (outs, comm_outs)


def _attn_fwd(q, k, v, bias, *, name, n_heads, dqk, scale, v_off=0, tq=512, exact_p=False, hp=HEADS_PER_STEP,
              comm=None):
    S = q.shape[0]
    tq = _pick(S, tq)
    tk = tq
    nq = S // tq
    has_bias = bias is not None

    assert n_heads % hp == 0 and v_off % hp == 0

    def body(*refs):
        q_ref, k_ref, v_ref = refs[:3]
        b_ref = refs[3] if has_bias else None
        o_ref, lse_ref, m_sc, l_sc, acc_sc = refs[-5:]
        qi = pl.program_id(1)
        m_sc[...] = jnp.full_like(m_sc, NEG)
        l_sc[...] = jnp.zeros_like(l_sc)
        acc_sc[...] = jnp.zeros_like(acc_sc)

        def tile(hh, j, diag):
            rows = pl.ds(pl.multiple_of(j * tk, tk), tk)
            s = _dot_nt(q_ref[:, hh * dqk:(hh + 1) * dqk], k_ref[rows, hh * dqk:(hh + 1) * dqk]) * scale
            if has_bias:
                s = s + b_ref[hh, :, rows]
            if diag:
                s = jnp.where(_diag_mask(tq, False), s, NEG)
            m_old = m_sc[hh]
            m_new = jnp.maximum(m_old, jnp.max(s, axis=1, keepdims=True))
            alpha = jnp.exp(m_old - m_new)
            p = jnp.exp(s - m_new)
            l_sc[hh] = alpha * l_sc[hh] + jnp.sum(p, axis=1, keepdims=True)
            vb = v_ref[rows, hh * HEAD:(hh + 1) * HEAD]
            if exact_p:
                hi, lo = _split2(p)
                pv = _dot(hi, vb) + _dot(lo, vb)
            else:
                pv = _dot(p.astype(BF16), vb)
            acc_sc[hh] = alpha * acc_sc[hh] + pv
            m_sc[hh] = m_new

        def step(j, carry):
            for hh in range(hp):
                tile(hh, j, False)
            return carry

        lax.fori_loop(0, qi, step, 0)
        for hh in range(hp):
            tile(hh, qi, True)
            l = l_sc[hh]
            o_ref[:, hh * HEAD:(hh + 1) * HEAD] = acc_sc[hh] / l
            lse_ref[hh] = jnp.broadcast_to(m_sc[hh] + jnp.log(l), (tq, LANES))

    in_specs = [pl.BlockSpec((tq, hp * dqk), lambda h, i: (i, h)),
                pl.BlockSpec((S, hp * dqk), lambda h, i: (0, h)),
                pl.BlockSpec((S, hp * HEAD), lambda h, i: (0, h + v_off // hp))]
    operands = [q, k, v]
    if has_bias:
        in_specs.append(pl.BlockSpec((hp, 1, S), lambda h, i: (h, 0, 0)))
        operands.append(bias)
    outs, comm_outs = _hosted_call(
        body,
        out_shape=(jax.ShapeDtypeStruct((S, n_heads * HEAD), F32),
                   jax.ShapeDtypeStruct((n_heads, S, LANES), F32)),
        grid=(n_heads // hp, nq), in_specs=in_specs,
        out_specs=(pl.BlockSpec((tq, hp * HEAD), lambda h, i: (i, h)),
                   pl.BlockSpec((hp, tq, LANES), lambda h, i: (h, i, 0))),
        scratch_shapes=[pltpu.VMEM((hp, tq, 1), F32), pltpu.VMEM((hp, tq, 1), F32), pltpu.VMEM((hp, tq, HEAD), F32)],
        name=name, sem=("parallel", "arbitrary"), operands=operands, comm=comm)
    return outs if comm is None else (outs, comm_outs)


def _attn_bwd(q, k, v, bias, o, lse, do, *, name, n_heads, dqk, scale, v_off=0, do_off=0, tq=512,
              hp=HEADS_PER_STEP, comm=None):
    S = q.shape[0]
    tq = _pick(S, tq)
    tk = tq
    nq = S // tq
    has_bias = bias is not None
    assert n_heads % hp == 0 and v_off % hp == 0 and do_off % hp == 0

    def body(*refs):
        q_ref, k_ref, v_ref, o_ref, lse_ref, do_ref = refs[:6]
        b_ref = refs[6] if has_bias else None
        n_out = 5 if has_bias else 3
        outs = refs[-(n_out + 3):-3]
        dq_ref, dk_ref, dv_ref = outs[:3]
        db_ref, dr_ref = (outs[3], outs[4]) if has_bias else (None, None)
        dq_sc, rs_sc, delta_sc = refs[-3:]
        qi = pl.program_id(1)

        @pl.when(qi == 0)
        def _():
            dk_ref[...] = jnp.zeros_like(dk_ref)
            dv_ref[...] = jnp.zeros_like(dv_ref)
            if has_bias:
                db_ref[...] = jnp.zeros_like(db_ref)

        dq_sc[...] = jnp.zeros_like(dq_sc)
        rs_sc[...] = jnp.zeros_like(rs_sc)
        for hh in range(hp):
            vs = slice(hh * HEAD, (hh + 1) * HEAD)
            do_r = do_ref[:, vs].astype(BF16).astype(F32)
            delta_sc[hh] = jnp.sum(do_r * o_ref[:, vs], axis=1, keepdims=True)

        def tile(hh, j, diag):
            qs = slice(hh * dqk, (hh + 1) * dqk)
            vs = slice(hh * HEAD, (hh + 1) * HEAD)
            rows = pl.ds(pl.multiple_of(j * tk, tk), tk)
            qb = q_ref[:, qs]
            do_b = do_ref[:, vs].astype(BF16)
            delta = delta_sc[hh]
            kb = k_ref[rows, qs]
            s = _dot_nt(qb, kb) * scale
            if has_bias:
                s = s + b_ref[hh, :, rows]
            p = jnp.exp(s - lse_ref[hh, :, 0:1])
            if diag:
                p = jnp.where(_diag_mask(tq, False), p, 0.0)
            ds = p * (_dot_nt(do_b, v_ref[rows, vs]) - delta)
            dsb = (ds * scale).astype(BF16)
            dq_sc[hh] += _dot(dsb, kb)
            dk_ref[rows, qs] += _dot_tn(dsb, qb)
            dv_ref[rows, vs] += _dot_tn(p.astype(BF16), do_b)
            if has_bias:
                db_ref[hh, :, rows] += jnp.sum(ds, axis=0, keepdims=True)
                rs_sc[hh] += jnp.sum(ds, axis=1, keepdims=True)

        def step(j, carry):
            for hh in range(hp):
                tile(hh, j, False)
            return carry

        lax.fori_loop(0, qi, step, 0)
        for hh in range(hp):
            tile(hh, qi, True)
            dq_ref[:, hh * dqk:(hh + 1) * dqk] = dq_sc[hh]
            if has_bias:
                dr_ref[hh] = jnp.broadcast_to(rs_sc[hh], (tq, LANES))

    stat = pl.BlockSpec((hp, tq, LANES), lambda h, i: (h, i, 0))
    in_specs = [pl.BlockSpec((tq, hp * dqk), lambda h, i: (i, h)),
                pl.BlockSpec((S, hp * dqk), lambda h, i: (0, h)),
                pl.BlockSpec((S, hp * HEAD), lambda h, i: (0, h + v_off // hp)),
                pl.BlockSpec((tq, hp * HEAD), lambda h, i: (i, h)),
                stat,
                pl.BlockSpec((tq, hp * HEAD), lambda h, i: (i, h + do_off // hp))]
    operands = [q, k, v, o, lse, do]
    out_shape = [jax.ShapeDtypeStruct((S, n_heads * dqk), F32), jax.ShapeDtypeStruct((S, n_heads * dqk), F32),
                 jax.ShapeDtypeStruct((S, n_heads * HEAD), F32)]
    out_specs = [pl.BlockSpec((tq, hp * dqk), lambda h, i: (i, h)), pl.BlockSpec((S, hp * dqk), lambda h, i: (0, h)),
                 pl.BlockSpec((S, hp * HEAD), lambda h, i: (0, h))]
    if has_bias:
        in_specs.append(pl.BlockSpec((hp, 1, S), lambda h, i: (h, 0, 0)))
        operands.append(bias)
        out_shape.append(jax.ShapeDtypeStruct((n_heads, 1, S), F32))
        out_specs.append(pl.BlockSpec((hp, 1, S), lambda h, i: (h, 0, 0)))
        out_shape.append(jax.ShapeDtypeStruct((n_heads, S, LANES), F32))
        out_specs.append(stat)
    outs, comm_outs = _hosted_call(
        body, out_shape=tuple(out_shape), grid=(n_heads // hp, nq), in_specs=in_specs, out_specs=tuple(out_specs),
        scratch_shapes=[pltpu.VMEM((hp, tq, dqk), F32), pltpu.VMEM((hp, tq, 1), F32),
                        pltpu.VMEM((hp, tq, 1), F32)], name=name,
        sem=("parallel", "arbitrary"), operands=operands, comm=comm)
    return outs if comm is None else (outs, comm_outs)


def _rot_half(y):
    lane = lax.broadcasted_iota(jnp.int32, y.shape, 1)
    up = pltpu.roll(y, 96, 1)
    down = pltpu.roll(y, 32, 1)
    return jnp.where(lane < 32, -up, jnp.where(lane < 64, down, 0.0))


def _mla_prep_fwd(q_raw, kv_raw, down, cos, sin, q_g, k_g, *, name, ts=128):
    S = q_raw.shape[0]
    ts = _pick(S, ts)
    pe_blk = Q_RANK // LANES + KV_RANK // LANES

    def norm_rope(x0, x1, g0, g1, c, s):
        ms = (jnp.sum(x0 * x0, axis=-1, keepdims=True) + jnp.sum(x1 * x1, axis=-1, keepdims=True)) * (1.0 / QK_DIM)
        rstd = lax.rsqrt(ms + EPS)
        y0 = (x0 * rstd) * g0
        y1 = (x1 * rstd) * g1
        return y0, y1 * c + _rot_half(y1) * s

    def body(q_ref, kv_ref, pe_ref, cos_ref, sin_ref, qg_ref, kg_ref, qo_ref, ko_ref, vo_ref):
        c, s = cos_ref[...], sin_ref[...]
        pe = pe_ref[...]
        qg0, qg1 = qg_ref[:, :NOPE], qg_ref[:, NOPE:]
        kg0, kg1 = kg_ref[:, :NOPE], kg_ref[:, NOPE:]
        for h in range(N_MLA):
            b = h * QK_PAD
            y0, y1 = norm_rope(q_ref[:, b:b + NOPE], q_ref[:, b + NOPE:b + QK_PAD], qg0, qg1, c, s)
            qo_ref[:, b:b + NOPE] = y0.astype(BF16)
            qo_ref[:, b + NOPE:b + QK_PAD] = y1.astype(BF16)
            y0, y1 = norm_rope(kv_ref[:, b:b + NOPE], pe, kg0, kg1, c, s)
            ko_ref[:, b:b + NOPE] = y0.astype(BF16)
            ko_ref[:, b + NOPE:b + QK_PAD] = y1.astype(BF16)
            vo_ref[:, h * HEAD:(h + 1) * HEAD] = kv_ref[:, b + NOPE:b + QK_PAD].astype(BF16)

    wide = pl.BlockSpec((ts, N_MLA * QK_PAD), lambda i: (i, 0))
    lane_blk = pl.BlockSpec((ts, LANES), lambda i: (i, 0))
    gain = pl.BlockSpec((1, QK_PAD), lambda i: (0, 0))
    return pl.pallas_call(
        body,
        out_shape=(jax.ShapeDtypeStruct((S, N_MLA * QK_PAD), BF16), jax.ShapeDtypeStruct((S, N_MLA * QK_PAD), BF16),
                   jax.ShapeDtypeStruct((S, N_MLA * HEAD), BF16)),
        grid=(S // ts,),
        in_specs=[wide, wide, pl.BlockSpec((ts, LANES), lambda i: (i, pe_blk)), lane_blk, lane_blk, gain, gain],
        out_specs=(wide, wide, pl.BlockSpec((ts, N_MLA * HEAD), lambda i: (i, 0))), name=name,
        compiler_params=_cp("parallel"))(q_raw, kv_raw, down, cos, sin, q_g, k_g)


def _mla_prep_bwd(dq, dk, dv, q_raw, kv_raw, down, cos, sin, q_g, k_g, *, name, ts=128):
    S = q_raw.shape[0]
    ts = _pick(S, ts)
    nsteps = S // ts
    pe_blk = Q_RANK // LANES + KV_RANK // LANES

    def back(x0, x1, g0, g1, c, s, d0, d1r):
        d1 = d1r * c - _rot_half(d1r * s)
        ms = (jnp.sum(x0 * x0, axis=-1, keepdims=True) + jnp.sum(x1 * x1, axis=-1, keepdims=True)) * (1.0 / QK_DIM)
        rstd = lax.rsqrt(ms + EPS)
        h0, h1 = x0 * rstd, x1 * rstd
        e0, e1 = d0 * g0, d1 * g1
        m = (jnp.sum(e0 * h0, axis=-1, keepdims=True) + jnp.sum(e1 * h1, axis=-1, keepdims=True)) * (1.0 / QK_DIM)
        return rstd * (e0 - h0 * m), rstd * (e1 - h1 * m), d0 * h0, d1 * h1

    def fold(a):
        return jnp.sum(a.reshape(ts // 8, 8, a.shape[-1]), axis=0)

    def body(dq_ref, dk_ref, dv_ref, q_ref, kv_ref, pe_ref, cos_ref, sin_ref, qg_ref, kg_ref,
             dqr_ref, dkv_ref, dpe_ref, dqg_ref, dkg_ref, gq_sc, gk_sc):
        i = pl.program_id(0)

        @pl.when(i == 0)
        def _():
            gq_sc[...] = jnp.zeros_like(gq_sc)
            gk_sc[...] = jnp.zeros_like(gk_sc)

        c, s = cos_ref[...], sin_ref[...]
        pe = pe_ref[...]
        qg0, qg1 = qg_ref[:, :NOPE], qg_ref[:, NOPE:]
        kg0, kg1 = kg_ref[:, :NOPE], kg_ref[:, NOPE:]
        dpe = jnp.zeros((ts, LANES), F32)
        for h in range(N_MLA):
            b = h * QK_PAD
            dx0, dx1, a0, a1 = back(q_ref[:, b:b + NOPE], q_ref[:, b + NOPE:b + QK_PAD], qg0, qg1, c, s,
                                    dq_ref[:, b:b + NOPE], dq_ref[:, b + NOPE:b + QK_PAD])
            dqr_ref[:, b:b + NOPE] = dx0.astype(BF16)
            dqr_ref[:, b + NOPE:b + QK_PAD] = dx1.astype(BF16)
            gq_sc[:, :NOPE] += fold(a0)
            gq_sc[:, NOPE:] += fold(a1)
            dx0, dx1, a0, a1 = back(kv_ref[:, b:b + NOPE], pe, kg0, kg1, c, s,
                                    dk_ref[:, b:b + NOPE], dk_ref[:, b + NOPE:b + QK_PAD])
            dkv_ref[:, b:b + NOPE] = dx0.astype(BF16)
            dkv_ref[:, b + NOPE:b + QK_PAD] = dv_ref[:, h * HEAD:(h + 1) * HEAD].astype(BF16)
            dpe = dpe + dx1
            gk_sc[:, :NOPE] += fold(a0)
            gk_sc[:, NOPE:] += fold(a1)
        dpe_ref[...] = dpe

        @pl.when(i == nsteps - 1)
        def _():
            dqg_ref[...] = jnp.sum(gq_sc[...], axis=0, keepdims=True)
            dkg_ref[...] = jnp.sum(gk_sc[...], axis=0, keepdims=True)

    wide = pl.BlockSpec((ts, N_MLA * QK_PAD), lambda i: (i, 0))
    lane_blk = pl.BlockSpec((ts, LANES), lambda i: (i, 0))
    gain = pl.BlockSpec((1, QK_PAD), lambda i: (0, 0))
    outs = pl.pallas_call(
        body,
        out_shape=(jax.ShapeDtypeStruct((S, N_MLA * QK_PAD), BF16), jax.ShapeDtypeStruct((S, N_MLA * QK_PAD), BF16),
                   jax.ShapeDtypeStruct((S, LANES), F32), jax.ShapeDtypeStruct((1, QK_PAD), F32),
                   jax.ShapeDtypeStruct((1, QK_PAD), F32)),
        grid=(nsteps,),
        in_specs=[wide, wide, pl.BlockSpec((ts, N_MLA * HEAD), lambda i: (i, 0)), wide, wide,
                  pl.BlockSpec((ts, LANES), lambda i: (i, pe_blk)), lane_blk, lane_blk, gain, gain],
        out_specs=(wide, wide, lane_blk, gain, gain),
        scratch_shapes=[pltpu.VMEM((8, QK_PAD), F32), pltpu.VMEM((8, QK_PAD), F32)], name=name,
        compiler_params=_cp("arbitrary"))(dq, dk, dv, q_raw, kv_raw, down, cos, sin, q_g, k_g)
    return outs[0], outs[1], outs[2], outs[3][0], outs[4][0]


def _loss_head(y, target, *, name, tr=256):
    R, C = y.shape
    tr = _pick(R, tr)
    nsteps = R // tr

    def body(y_ref, t_ref, dy_ref, dyb_ref, loss_ref, acc):
        i = pl.program_id(0)

        @pl.when(i == 0)
        def _():
            acc[...] = jnp.zeros_like(acc)

        err = y_ref[...] - t_ref[...]
        dy = err * (1.0 / C)
        dy_ref[...] = dy
        dyb_ref[...] = dy.astype(BF16)
        acc[...] += jnp.sum((err * err).reshape(tr // 8, 8, C), axis=0)

        @pl.when(i == nsteps - 1)
        def _():
            tot = jnp.sum(jnp.sum(acc[...], axis=0, keepdims=True), axis=1, keepdims=True)
            loss_ref[...] = jnp.broadcast_to(tot * (0.5 / C), (8, LANES))

    blk = pl.BlockSpec((tr, C), lambda i: (i, 0))
    dy, dy_b, loss = pl.pallas_call(
        body, out_shape=(jax.ShapeDtypeStruct((R, C), F32), jax.ShapeDtypeStruct((R, C), BF16),
                         jax.ShapeDtypeStruct((8, LANES), F32)),
        grid=(nsteps,), in_specs=[blk, blk], out_specs=(blk, blk, pl.BlockSpec((8, LANES), lambda i: (0, 0))),
        scratch_shapes=[pltpu.VMEM((8, C), F32)], name=name, compiler_params=_cp("arbitrary"))(y, target)
    return dy, dy_b, loss[0, 0]


def _adamw(w, g, m, v, *, name, block_bytes=1 << 20, comm=None):
    R, C = w.shape
    tr = max(8, min(R, (block_bytes // (4 * C)) // 8 * 8))
    while R % tr:
        tr -= 8
    if tr <= 0:
        tr = R
    c1 = 1.0 / (1.0 - ADAM_B1 ** ADAM_STEP)
    c2 = 1.0 / (1.0 - ADAM_B2 ** ADAM_STEP)

    def body(w_ref, g_ref, m_ref, v_ref, d_ref, mo_ref, vo_ref):
        gv = g_ref[...]
        mn = ADAM_B1 * m_ref[...] + (1.0 - ADAM_B1) * gv
        vn = ADAM_B2 * v_ref[...] + (1.0 - ADAM_B2) * (gv * gv)
        d_ref[...] = -ADAM_LR * ((mn * c1) / (jnp.sqrt(vn * c2) + ADAM_EPS) + ADAM_WD * w_ref[...])
        mo_ref[...] = mn
        vo_ref[...] = vn

    blk = pl.BlockSpec((tr, C), lambda i: (i, 0))
    sd = jax.ShapeDtypeStruct((R, C), F32)
    outs, comm_outs = _hosted_call(
        body, out_shape=(sd, sd, sd), grid=(R // tr,), in_specs=[blk] * 4, out_specs=(blk,) * 3, scratch_shapes=[],
        name=name, sem=("parallel",), operands=(w, g, m, v), comm=comm)
    return outs if comm is None else (outs, comm_outs)


def _row_tile(r, c, itemsize=4, block_bytes=1 << 20):
    tr = max(16, min(r, (block_bytes // (itemsize * c)) // 16 * 16))
    while r % tr:
        tr -= 16
    return tr if tr > 0 else r


def _add_sibling(g, recv, core, *, name):
    nch, _, r, c = g.shape
    tr = _row_tile(r, c)

    def body(core_ref, g_ref, r_ref, o_ref):
        o_ref[...] = (g_ref[...] + r_ref[...]).astype(BF16)

    grid_spec = pltpu.PrefetchScalarGridSpec(
        num_scalar_prefetch=1, grid=(nch, r // tr),
        in_specs=[pl.BlockSpec((None, None, tr, c), lambda j, i, cr: (j, cr[0], i, 0)),
                  pl.BlockSpec((None, tr, c), lambda j, i, cr: (j, i, 0))],
        out_specs=pl.BlockSpec((None, tr, c), lambda j, i, cr: (j, i, 0)))
    return pl.pallas_call(
        body, out_shape=jax.ShapeDtypeStruct((nch, r, c), BF16), grid_spec=grid_spec, name=name,
        compiler_params=_cp("parallel", "parallel"))(core, g, recv)


def _add_chips(slots, *, name):
    nch, r, c = slots.shape
    tr = _row_tile(r, c)

    def body(s_ref, o_ref):
        acc = s_ref[0].astype(F32)
        for j in range(1, nch):
            acc = acc + s_ref[j].astype(F32)
        o_ref[...] = acc

    return pl.pallas_call(
        body, out_shape=jax.ShapeDtypeStruct((r, c), F32), grid=(r // tr,),
        in_specs=[pl.BlockSpec((nch, tr, c), lambda i: (0, i, 0))],
        out_specs=pl.BlockSpec((tr, c), lambda i: (i, 0)), name=name, compiler_params=_cp("parallel"))(slots)


def _place():
    x, y, c = lax.axis_index("x"), lax.axis_index("y"), lax.axis_index("c")
    others = [(1 - x, y), (x, 1 - y), (1 - x, 1 - y)]
    return x, y, c, 2 * x + y, others


ANY = pl.BlockSpec(memory_space=pl.ANY)


class _Exchange:
    def __init__(self, kind, arrays):
        self.kind, self.ins = kind, list(arrays)
        self.n_peers = 3 if kind in ("gather", "scatter") else 1
        n = len(self.ins) * self.n_peers
        shp = {"gather": lambda a: (N_CHIPS,) + a.shape, "scatter": lambda a: a.shape,
               "swap": lambda a: (a.shape[0],) + a.shape[2:], "join": lambda a: (2,) + a.shape}[kind]
        self.out_shapes = [jax.ShapeDtypeStruct(shp(a), a.dtype) for a in self.ins]
        self.sems = [pltpu.SemaphoreType.DMA((n,)), pltpu.SemaphoreType.DMA((n,))]

    def _copies(self, ins, outs, sems):
        send, recv = sems
        x, y, c, me, others = _place()
        peers = [(ox, oy, c) for ox, oy in others] if self.n_peers == 3 else [(x, y, 1 - c)]
        for a in range(len(self.ins)):
            for k, to in enumerate(peers):
                peer = 2 * to[0] + to[1]
                if self.kind == "gather":
                    hr = self.ins[a].shape[0] // 2
                    rows = pl.ds(c * hr, hr)
                    src, dst, land = ins[a].at[rows, :], outs[a].at[me, rows, :], outs[a].at[peer, rows, :]
                elif self.kind == "scatter":
                    src, dst, land = ins[a].at[peer], outs[a].at[me], outs[a].at[peer]
                elif self.kind == "swap":
                    src, dst, land = ins[a].at[:, 1 - c], outs[a], outs[a]
                else:
                    src, dst, land = ins[a], outs[a].at[c], outs[a].at[1 - c]
                i = self.n_peers * a + k
                mk = lambda s, d: pltpu.make_async_remote_copy(
                    src_ref=s, dst_ref=d, send_sem=send.at[i], recv_sem=recv.at[i], device_id=to,
                    device_id_type=MESH)
                yield mk(src, dst), mk(land, land)

    def start(self, ins, outs, sems):
        for cp, _ in self._copies(ins, outs, sems):
            cp.start()

    def finish(self, ins, outs, sems):
        pairs = list(self._copies(ins, outs, sems))
        for _, landing in pairs:
            landing.wait_recv()
        for cp, _ in pairs:
            cp.wait_send()


class _Several:
    def __init__(self, parts):
        self.parts = list(parts)
        self.ins = [a for p in self.parts for a in p.ins]
        self.out_shapes = [s for p in self.parts for s in p.out_shapes]
        self.sems = [s for p in self.parts for s in p.sems]

    def split(self, ins, outs, sems=None):
        i = 0
        for k, p in enumerate(self.parts):
            n = len(p.ins)
            yield p, ins[i:i + n], outs[i:i + n], None if sems is None else sems[2 * k:2 * k + 2]
            i += n

    def start(self, ins, outs, sems):
        for p, a, b, s in self.split(ins, outs, sems):
            p.start(a, b, s)

    def finish(self, ins, outs, sems):
        for p, a, b, s in self.split(ins, outs, sems):
            p.finish(a, b, s)


def _hosted_call(body, *, grid, in_specs, out_specs, out_shape, scratch_shapes, operands, name, sem, comm=None):
    out_specs, out_shape = tuple(out_specs), tuple(out_shape)
    if isinstance(comm, (list, tuple)):
        several = _Several(comm)
        outs, comm_outs = _hosted_call(body, grid=grid, in_specs=in_specs, out_specs=out_specs, out_shape=out_shape,
                                       scratch_shapes=scratch_shapes, operands=operands, name=name, sem=sem,
                                       comm=several)
        return outs, [tuple(o) for _, _, o, _ in several.split(several.ins, comm_outs)]
    if comm is None:
        res = pl.pallas_call(body, out_shape=out_shape, grid=grid, in_specs=list(in_specs), out_specs=out_specs,
                             scratch_shapes=list(scratch_shapes), name=name, compiler_params=_cp(*sem))(*operands)
        return tuple(res), ()
    n_in, n_out, n_sc = len(in_specs), len(out_specs), len(scratch_shapes)
    ci, co = len(comm.ins), len(comm.out_shapes)

    def wrapped(*refs):
        ins, c_ins = refs[:n_in], refs[n_in:n_in + ci]
        outs = refs[n_in + ci:n_in + ci + n_out]
        c_outs = refs[n_in + ci + n_out:n_in + ci + n_out + co]
        scratch = refs[n_in + ci + n_out + co:n_in + ci + n_out + co + n_sc]
        sems = refs[n_in + ci + n_out + co + n_sc:]
        ids = [pl.program_id(d) for d in range(len(grid))]
        first = functools.reduce(jnp.logical_and, [i == 0 for i in ids])
        last = functools.reduce(jnp.logical_and, [i == g - 1 for i, g in zip(ids, grid)])

        @pl.when(first)
        def _():
            comm.start(c_ins, c_outs, sems)

        body(*ins, *outs, *scratch)

        @pl.when(last)
        def _():
            comm.finish(c_ins, c_outs, sems)

    res = pl.pallas_call(
        wrapped, out_shape=out_shape + tuple(comm.out_shapes), grid=grid, in_specs=list(in_specs) + [ANY] * ci,
        out_specs=out_specs + tuple([ANY] * co), scratch_shapes=list(scratch_shapes) + comm.sems, name=name,
        compiler_params=pltpu.CompilerParams(dimension_semantics=("arbitrary",) * len(grid),
                                             vmem_limit_bytes=VMEM_LIMIT, has_side_effects=True),
    )(*operands, *comm.ins)
    return tuple(res[:n_out]), tuple(res[n_out:])


def _run_exchange(comm, *, name):
    ci = len(comm.ins)

    def body(*refs):
        ins, outs, sems = refs[:ci], refs[ci:2 * ci], refs[2 * ci:]
        comm.start(ins, outs, sems)
        comm.finish(ins, outs, sems)

    return pl.pallas_call(
        body, out_shape=tuple(comm.out_shapes), in_specs=[ANY] * ci, out_specs=tuple([ANY] * ci),
        scratch_shapes=comm.sems, name=name, compiler_params=pltpu.CompilerParams(has_side_effects=True))(*comm.ins)


def _forward_sibling(gathered, *, name):
    n = len(gathered)

    def body(*refs):
        ins, outs = refs[:n], refs[n:2 * n]
        send, recv = refs[2 * n:]
        x, y, c, me, others = _place()
        sends = []
        for a in range(n):
            hr = gathered[a].shape[1] // 2
            for k, (ox, oy) in enumerate(others):
                landed = ins[a].at[2 * ox + oy, pl.ds(c * hr, hr), :]
                cp = pltpu.make_async_remote_copy(
                    src_ref=landed, dst_ref=outs[a].at[2 * ox + oy, pl.ds(c * hr, hr), :], send_sem=send.at[3 * a + k],
                    recv_sem=recv.at[3 * a + k], device_id=(x, y, 1 - c), device_id_type=MESH)
                cp.start()
                sends.append(cp)
        for a in range(n):
            hr = gathered[a].shape[1] // 2
            for k, (ox, oy) in enumerate(others):
                got = outs[a].at[2 * ox + oy, pl.ds((1 - c) * hr, hr), :]
                pltpu.make_async_remote_copy(src_ref=got, dst_ref=got, send_sem=send.at[3 * a + k],
                                             recv_sem=recv.at[3 * a + k], device_id=(x, y, 1 - c),
                                             device_id_type=MESH).wait_recv()
        for cp in sends:
            cp.wait_send()

    return pl.pallas_call(
        body, out_shape=tuple(jax.ShapeDtypeStruct(g.shape, g.dtype) for g in gathered),
        in_specs=[ANY] * n, out_specs=tuple([ANY] * n), input_output_aliases={a: a for a in range(n)},
        scratch_shapes=[pltpu.SemaphoreType.DMA((3 * n,)), pltpu.SemaphoreType.DMA((3 * n,))],
        name=name, compiler_params=pltpu.CompilerParams(has_side_effects=True))(*gathered)


def _own_slot(buf, piece, idx):
    return lax.dynamic_update_slice(buf, piece[None], (idx,) + (0,) * piece.ndim)


def _all_reduce_small(v, *, name):
    R = v.shape[0]

    flips = [(dx, dy, dc) for dx in range(2) for dy in range(2) for dc in range(2) if dx or dy or dc]

    def body(v_ref, o_ref, slots, send, recv):
        x, y, c, me, others = _place()
        mine = 2 * me + c
        slots[mine] = v_ref[...]

        def copy(k, slot):
            dx, dy, dc = flips[k]
            peer = (x + dx - 2 * x * dx, y + dy - 2 * y * dy, c + dc - 2 * c * dc)
            peer_slot = 4 * peer[0] + 2 * peer[1] + peer[2]
            return pltpu.make_async_remote_copy(
                src_ref=v_ref, dst_ref=slots.at[mine if slot == "mine" else peer_slot], send_sem=send.at[k],
                recv_sem=recv.at[k], device_id=peer, device_id_type=MESH)

        for k in range(7):
            copy(k, "mine").start()
        for k in range(7):
            copy(k, "peer").wait_recv()
        for k in range(7):
            copy(k, "mine").wait_send()
        acc = slots[0]
        for j in range(1, 8):
            acc = acc + slots[j]
        o_ref[...] = acc

    vm = pl.BlockSpec(memory_space=pltpu.VMEM)
    return pl.pallas_call(
        body, out_shape=jax.ShapeDtypeStruct(v.shape, F32), in_specs=[vm], out_specs=vm,
        scratch_shapes=[pltpu.VMEM((8, R, LANES), F32), pltpu.SemaphoreType.DMA((7,)),
                        pltpu.SemaphoreType.DMA((7,))],
        name=name, compiler_params=pltpu.CompilerParams(has_side_effects=True))(v)


def _rows(v, n_rows):
    v = v.reshape(-1).astype(F32)
    return jnp.pad(v, (0, n_rows * LANES - v.shape[0])).reshape(n_rows, LANES)


def _mlp_fwd(x_in, g, w_up, w_down, tag):
    h = _rms_fwd(x_in, g, name=f"{tag}_norm")
    u, a = _matmul(h, w_up, b_split=True, epilogue="sqrelu", name=f"{tag}_up")
    x_out = _matmul(a, w_down, epilogue="res", res=x_in, name=f"{tag}_down")
    return x_out, (h, u, a)


def _mlp_bwd(dy, dy_b, x_in, g, w_up, w_down, saved, tag, comms=None):
    h, u, a = saved
    comms = comms or {}
    landed = {}

    def mm(key, *args, **kw):
        comm = comms.get(key)
        if callable(comm):
            comm = comm(landed)
        out = _matmul(*args, name=f"{tag}_{key}", comm=comm, **kw)
        if comm is not None:
            out, landed[key] = out
        return out

    dw_down = mm("dwdown", a, dy_b, form="tn")
    du = mm("du", dy_b, w_down, form="nt", epilogue="sqrelu_bwd", u=u, out_dtype=BF16)
    dw_up = mm("dwup", h, du, form="tn", out_split=True)
    dh = mm("dh", du, w_up, form="nt", b_split=True)
    dx, dx_b, dg = _rms_bwd(x_in, g, dh, res=dy, bf16_copy=True, name=f"{tag}_dnorm")
    return dx, dx_b, dg, dw_up, dw_down, landed


def kernel(x, positions, ln_mix_g, ln_mlp_g, sf_w_in, sf_b_f, fox_q_g, fox_k_g, sf_w_o, mla_w_down, mla_q_a_g, mla_kv_a_g, mla_w_uq, mla_w_ukv, mla_q_g, mla_k_g, mla_w_o, mlp_w_up, mlp_w_down, loss_target, m_ln_mix_g, m_ln_mlp_g, m_sf_w_in, m_sf_b_f, m_fox_q_g, m_fox_k_g, m_sf_w_o, m_mla_w_down, m_mla_q_a_g, m_mla_kv_a_g, m_mla_w_uq, m_mla_w_ukv, m_mla_q_g, m_mla_k_g, m_mla_w_o, m_mlp_w_up, m_mlp_w_down, v_ln_mix_g, v_ln_mlp_g, v_sf_w_in, v_sf_b_f, v_fox_q_g, v_fox_k_g, v_sf_w_o, v_mla_w_down, v_mla_q_a_g, v_mla_kv_a_g, v_mla_w_uq, v_mla_w_ukv, v_mla_q_g, v_mla_k_g, v_mla_w_o, v_mlp_w_up, v_mlp_w_down):
    S, D = x.shape[1], x.shape[2]
    xs, tgt, pos = x[0], loss_target[0], positions[0]
    xi, yi, ci = lax.axis_index("x"), lax.axis_index("y"), lax.axis_index("c")
    chip = 2 * xi + yi
    core = ci.astype(jnp.int32).reshape(1)
    d_ff = mlp_w_up.shape[2] * N_CHIPS
    in_w = sf_w_in.shape[2] * N_CHIPS
    qkv_w = 3 * N_SB * HEAD + 3 * N_FOX * HEAD
    dn_w = mla_w_down.shape[2]
    dn_pad = Q_RANK + KV_RANK + LANES

    def gather_begin(ws):
        shards = [w.astype(BF16) for w in ws]
        return shards, _Exchange("gather", shards)

    def gather_end(gathered, shards, name):
        both = _forward_sibling(list(gathered), name=name)
        return [_own_slot(ag, s, chip) for ag, s in zip(both, shards)]

    cols = lambda ag: ag.transpose(1, 0, 2).reshape(ag.shape[1], -1)
    rows = lambda ag: ag.reshape(-1, ag.shape[2])
    s_mix0, ex_mix0 = gather_begin([sf_w_in[0], sf_w_o[0]])
    ag_in, ag_o0 = gather_end(_run_exchange(ex_mix0, name="gather_mix0"), s_mix0, "gather_mix0_sibling")
    w_in_full = cols(ag_in)
    w_qkv = w_in_full[:, :qkv_w]
    w_f = jnp.pad(w_in_full[:, qkv_w:], ((0, 0), (0, LANES - (in_w - qkv_w))))
    w_o0 = rows(ag_o0)
    s_mlp0, ex_mlp0 = gather_begin([mlp_w_up[0], mlp_w_down[0]])
    s_mix1, ex_mix1 = gather_begin([mla_w_down[0], mla_w_uq[0], mla_w_ukv[0], mla_w_o[0]])
    s_mlp1, ex_mlp1 = gather_begin([mlp_w_up[1], mlp_w_down[1]])

    gain_blk = jnp.concatenate([mla_q_a_g, mla_kv_a_g], axis=0) * (ci == 0).astype(F32)
    placed = jnp.zeros((2, N_CHIPS, LANES), F32)
    placed = lax.dynamic_update_slice(placed, gain_blk[:, None, :], (0, chip, 0))
    gains = _all_reduce_small(placed.reshape(2 * N_CHIPS, LANES), name="gather_gains")
    q_a_full = gains[:N_CHIPS].reshape(Q_RANK)
    kv_a_full = gains[N_CHIPS:].reshape(KV_RANK)

    pad_gain = lambda g: jnp.pad(g.reshape(1, QK_DIM), ((0, 0), (0, QK_PAD - QK_DIM)))
    q_g_pad, k_g_pad = pad_gain(mla_q_g), pad_gain(mla_k_g)
    b_pad = _rows(sf_b_f, 1)

    h0 = _rms_fwd(xs, ln_mix_g[0], name="mix0_norm")
    qkv_sb = _matmul(h0, w_qkv, n=3 * N_SB * HEAD, b_n0=0, out_dtype=BF16, name="mix0_qkv_sb")
    qk_fx = _matmul(h0, w_qkv, n=2 * N_FOX * HEAD, b_n0=3 * N_SB * HEAD, name="mix0_qk_fox")
    v_fx = _matmul(h0, w_qkv, n=N_FOX * HEAD, b_n0=(3 * N_SB + 2 * N_FOX) * HEAD, out_dtype=BF16,
                   name="mix0_v_fox")
    fl = _matmul(h0, w_f, name="mix0_forget_logit")
    f_cum = _forget_fwd(fl, b_pad, name="forget_fwd")
    neg_f = (-f_cum[:, :N_FOX]).T.reshape(N_FOX, 1, S)
    q_f = _rms_fwd(qk_fx, fox_q_g[0], c0=0, width=N_FOX * HEAD, gw=HEAD, name="fox_q_norm")
    k_f = _rms_fwd(qk_fx, fox_k_g[0], c0=N_FOX * HEAD, width=N_FOX * HEAD, gw=HEAD, name="fox_k_norm")
    (o_sb, t_sb), landed = _sb_fwd(qkv_sb, name="sb_fwd", comm=ex_mlp0)
    ag_up0, ag_dw0 = gather_end(landed, s_mlp0, "gather_mlp0_sibling")
    (o_fx, lse0), landed = _attn_fwd(q_f, k_f, v_fx, neg_f, n_heads=N_FOX, dqk=HEAD, scale=HEAD ** -0.5,
                                     exact_p=True, name="fox_fwd", comm=ex_mix1)
    ag_dn, ag_uq, ag_ukv, ag_o1 = gather_end(landed, s_mix1, "gather_mix1_sibling")
    w_dn = jnp.pad(rows(ag_dn), ((0, 0), (0, dn_pad - dn_w)))
    w_uq = jnp.pad(cols(ag_uq).reshape(Q_RANK, N_MLA, QK_DIM), ((0, 0), (0, 0), (0, QK_PAD - QK_DIM)))
    w_uq = w_uq.reshape(Q_RANK, N_MLA * QK_PAD)
    w_ukv = cols(ag_ukv)
    w_o1 = rows(ag_o1)
    o0 = jnp.concatenate([o_sb, o_fx], axis=1)
    x1 = _matmul(o0, w_o0, epilogue="res", res=xs, name="mix0_out")
    x2, mlp0 = _mlp_fwd(x1, ln_mlp_g[0], ag_up0, rows(ag_dw0), "mlp0")

    h2 = _rms_fwd(x2, ln_mix_g[1], name="mix1_norm")
    down = _matmul(h2, w_dn, name="mix1_down")
    c_q = _rms_fwd(down, q_a_full, c0=0, width=Q_RANK, name="mix1_q_a_norm")
    c_kv = _rms_fwd(down, kv_a_full, c0=Q_RANK, width=KV_RANK, name="mix1_kv_a_norm")
    q_raw = _matmul(c_q, w_uq, name="mix1_uq")
    kv_raw = _matmul(c_kv, w_ukv, name="mix1_ukv")
    half = ROPE // 2
    inv_freq = ROPE_THETA ** (-jnp.arange(half, dtype=F32) / half)
    ang = pos.astype(F32)[:, None] * inv_freq
    table = lambda t: jnp.pad(jnp.concatenate([t, t], axis=1), ((0, 0), (0, LANES - ROPE)))
    cos_t, sin_t = table(jnp.cos(ang)), table(jnp.sin(ang))
    q_pad, k_pad, v1 = _mla_prep_fwd(q_raw, kv_raw, down, cos_t, sin_t, q_g_pad, k_g_pad, name="mla_prep_fwd")
    (o1, lse1), landed = _attn_fwd(q_pad, k_pad, v1, None, n_heads=N_MLA, dqk=QK_PAD, scale=QK_DIM ** -0.5,
                                   name="mla_fwd", comm=ex_mlp1)
    ag_up1, ag_dw1 = gather_end(landed, s_mlp1, "gather_mlp1_sibling")
    w_up = [ag_up0, ag_up1]
    w_dw = [rows(ag_dw0), rows(ag_dw1)]
    x3 = _matmul(o1, w_o1, epilogue="res", res=x2, name="mix1_out")
    x4, mlp1 = _mlp_fwd(x3, ln_mlp_g[1], w_up[1], w_dw[1], "mlp1")

    dx4, dx4_b, loss_local = _loss_head(x4, tgt, name="loss_head")
    loss = lax.psum(loss_local, ("x", "y", "c"))

    by_cols = lambda g: g.reshape(g.shape[0], N_CHIPS, -1).transpose(1, 0, 2)
    by_rows = lambda g: g.reshape(N_CHIPS, g.shape[0] // N_CHIPS, g.shape[1])
    halves = lambda g: g.reshape(N_CHIPS, 2, g.shape[1] // 2, g.shape[2])

    def scatter_of(grads, from_sibling, tags):
        parts = [_add_sibling(g, r, core, name=f"add_sibling_{t}") for g, r, t in zip(grads, from_sibling, tags)]
        return parts, _Exchange("scatter", parts)

    def reduce_end(slots, parts, tags, name):
        slots = [_own_slot(s, lax.dynamic_index_in_dim(p, chip, 0, keepdims=False), chip)
                 for s, p in zip(slots, parts)]
        mine = [_add_chips(s, name=f"add_chips_{t}") for s, t in zip(slots, tags)]
        joined = _run_exchange(_Exchange("join", mine), name=f"{name}_join")
        return [_own_slot(j, m, ci).reshape(2 * m.shape[0], m.shape[1]) for j, m in zip(joined, mine)]

    dx3, dx3_b, dg_mlp1, dw_up1, dw_dw1, _ = _mlp_bwd(dx4, dx4_b, x3, ln_mlp_g[1], w_up[1], w_dw[1], mlp1, "mlp1")
    tags_mlp1 = ["w_up1", "w_dw1"]
    g_mlp1 = [halves(dw_up1), halves(by_rows(dw_dw1))]

    dw_o1 = _matmul(o1, dx3_b, form="tn", name="mix1_dwo")
    do1 = _matmul(dx3_b, w_o1, form="nt", name="mix1_do")
    (dq_pad, dk_pad, dv1), from_sibling = _attn_bwd(
        q_pad, k_pad, v1, None, o1, lse1, do1, n_heads=N_MLA, dqk=QK_PAD, scale=QK_DIM ** -0.5, name="mla_bwd",
        comm=_Exchange("swap", g_mlp1))
    p_mlp1, _ = scatter_of(g_mlp1, from_sibling, tags_mlp1)
    dq_raw, dkv_raw, dpe, dg_q, dg_k = _mla_prep_bwd(dq_pad, dk_pad, dv1, q_raw, kv_raw, down, cos_t, sin_t,
                                                     q_g_pad, k_g_pad, name="mla_prep_bwd")
    dw_uq = _matmul(c_q, dq_raw, form="tn", name="mix1_dwuq")
    dc_q = _matmul(dq_raw, w_uq, form="nt", name="mix1_dcq")
    dw_ukv = _matmul(c_kv, dkv_raw, form="tn", name="mix1_dwukv")
    dc_kv = _matmul(dkv_raw, w_ukv, form="nt", name="mix1_dckv")
    d_cq, dg_qa = _rms_bwd(down, q_a_full, dc_q, c0=0, width=Q_RANK, name="mix1_q_a_dnorm")
    d_ckv, dg_kva = _rms_bwd(down, kv_a_full, dc_kv, c0=Q_RANK, width=KV_RANK, name="mix1_kv_a_dnorm")
    d_down = jnp.concatenate([d_cq, d_ckv, dpe], axis=1)
    dw_dn = _matmul(h2, d_down, form="tn", name="mix1_dwdown")
    dh2 = _matmul(d_down, w_dn, form="nt", name="mix1_dh")
    dx2, dx2_b, dg_mix1 = _rms_bwd(x2, ln_mix_g[1], dh2, res=dx3, bf16_copy=True, name="mix1_dnorm")
    g_uq = dw_uq.reshape(Q_RANK, N_MLA, QK_PAD)[:, :, :QK_DIM].reshape(Q_RANK, N_MLA * QK_DIM)
    tags_mix1 = ["w_dn", "w_uq", "w_ukv", "w_o1"]
    g_mix1 = [halves(by_rows(dw_dn[:, :dn_w])), halves(by_cols(g_uq)), halves(by_cols(dw_ukv)),
              halves(by_rows(dw_o1))]

    p_mix1 = []

    def scatter_mix1(landed):
        parts, ex = scatter_of(g_mix1, landed["dwdown"][1], tags_mix1)
        p_mix1.extend(parts)
        return ex

    dx1, dx1_b, dg_mlp0, dw_up0, dw_dw0, landed = _mlp_bwd(
        dx2, dx2_b, x1, ln_mlp_g[0], w_up[0], w_dw[0], mlp0, "mlp0",
        comms={"dwdown": [_Exchange("scatter", p_mlp1[:1]), _Exchange("swap", g_mix1)],
               "du": _Exchange("scatter", p_mlp1[1:]), "dwup": scatter_mix1})
    gs_up1, gs_dw1 = reduce_end(landed["dwdown"][0] + landed["du"], p_mlp1, tags_mlp1, "reduce_mlp1")
    gs_dn, gs_uq, gs_ukv, gs_o1 = reduce_end(landed["dwup"], p_mix1, tags_mix1, "reduce_mix1")
    tags_mlp0 = ["w_up0", "w_dw0"]
    g_mlp0 = [halves(dw_up0), halves(by_rows(dw_dw0))]

    dw_o0 = _matmul(o0, dx1_b, form="tn", name="mix0_dwo")
    do0 = _matmul(dx1_b, w_o0, form="nt", name="mix0_do")
    (dq_f, dk_f, dv_fx, dbias, drow), from_sibling = _attn_bwd(
        q_f, k_f, v_fx, neg_f, o_fx, lse0, do0, n_heads=N_FOX, dqk=HEAD, scale=HEAD ** -0.5, do_off=N_SB,
        name="fox_bwd", comm=_Exchange("swap", g_mlp0))
    p_mlp0, ex = scatter_of(g_mlp0, from_sibling, tags_mlp0)
    (dq_sb, dk_sb, dv_sb), slots = _sb_bwd(qkv_sb, do0, t_sb, do_off=0, name="sb_bwd", comm=ex)
    gs_up0, gs_dw0 = reduce_end(slots, p_mlp0, tags_mlp0, "reduce_mlp0")
    dq_fx, dg_fq = _rms_bwd(qk_fx, fox_q_g[0], dq_f, c0=0, width=N_FOX * HEAD, gw=HEAD, name="fox_q_dnorm")
    dk_fx, dg_fk = _rms_bwd(qk_fx, fox_k_g[0], dk_f, c0=N_FOX * HEAD, width=N_FOX * HEAD, gw=HEAD,
                            name="fox_k_dnorm")
    d_fcum = jnp.pad((drow[:, :, 0] - dbias.reshape(N_FOX, S)).T, ((0, 0), (0, LANES - N_FOX)))
    dfl, db_f = _forget_bwd(fl, b_pad, d_fcum, name="forget_bwd")
    dproj = jnp.concatenate([dq_sb, dk_sb, dv_sb, dq_fx, dk_fx, dv_fx], axis=1).astype(BF16)
    dw_qkv = _matmul(h0, dproj, form="tn", name="mix0_dwqkv")
    dw_f = _matmul(h0, dfl, form="tn", name="mix0_dwf")
    g_in = jnp.concatenate([dw_qkv, dw_f[:, :in_w - qkv_w]], axis=1)
    tags_mix0 = ["w_in", "w_o0"]
    g_mix0 = [halves(by_cols(g_in)), halves(by_rows(dw_o0))]
    dh0 = _matmul(dfl, w_f, form="nt", name="mix0_dh_f")
    dh0, from_sibling = _matmul(dproj, w_qkv, form="nt", epilogue="res", res=dh0, name="mix0_dh",
                                comm=_Exchange("swap", g_mix0))
    p_mix0, ex_mix0_grads = scatter_of(g_mix0, from_sibling, tags_mix0)
    grad_x, dg_mix0 = _rms_bwd(xs, ln_mix_g[0], dh0, res=dx1, name="mix0_dnorm")
    gs_up = jnp.concatenate([gs_up0, gs_up1], axis=0)
    gs_dw = jnp.concatenate([gs_dw0, gs_dw1], axis=0)

    ln_rows = D // LANES
    small = jnp.concatenate([
        _rows(dg_mix0, ln_rows), _rows(dg_mix1, ln_rows), _rows(dg_mlp0, ln_rows), _rows(dg_mlp1, ln_rows),
        _rows(db_f, 8), _rows(dg_fq, 8), _rows(dg_fk, 8), _rows(dg_qa, 8), _rows(dg_kva, 8), _rows(dg_q, 8),
        _rows(dg_k, 8)], axis=0)
    small = _all_reduce_small(small, name="reduce_small")
    flat = lambda r0, nr, n: small[r0:r0 + nr].reshape(-1)[:n]
    r0 = 4 * ln_rows
    g_ln_mix = jnp.stack([flat(0, ln_rows, D), flat(ln_rows, ln_rows, D)])
    g_ln_mlp = jnp.stack([flat(2 * ln_rows, ln_rows, D), flat(3 * ln_rows, ln_rows, D)])
    g_b_f = flat(r0, 8, N_FOX)[None]
    g_fq, g_fk = flat(r0 + 8, 8, HEAD)[None], flat(r0 + 16, 8, HEAD)[None]
    g_qa = lax.dynamic_slice(flat(r0 + 24, 8, Q_RANK), (chip * LANES,), (LANES,))[None]
    g_kva = lax.dynamic_slice(flat(r0 + 32, 8, KV_RANK), (chip * LANES,), (LANES,))[None]
    g_q, g_k = flat(r0 + 40, 8, QK_DIM)[None], flat(r0 + 48, 8, QK_DIM)[None]

    def pack_small(ln_mix, ln_mlp, *rest):
        return jnp.concatenate([_rows(ln_mix, 2 * ln_rows), _rows(ln_mlp, 2 * ln_rows)] + [_rows(t, 8) for t in rest],
                               axis=0)

    def unpack_small(p):
        f = lambda r, nr, shape: p[r:r + nr].reshape(-1)[:int(np.prod(shape))].reshape(shape)
        shapes = [(1, N_FOX), (1, HEAD), (1, HEAD), (1, LANES), (1, LANES), (1, QK_DIM), (1, QK_DIM)]
        return (f(0, 2 * ln_rows, (2, D)), f(2 * ln_rows, 2 * ln_rows, (2, D)),
                *[f(r0 + 8 * i, 8, shp) for i, shp in enumerate(shapes)])

    small_out = _adamw(
        pack_small(ln_mix_g, ln_mlp_g, sf_b_f, fox_q_g, fox_k_g, mla_q_a_g, mla_kv_a_g, mla_q_g, mla_k_g),
        pack_small(g_ln_mix, g_ln_mlp, g_b_f, g_fq, g_fk, g_qa, g_kva, g_q, g_k),
        pack_small(m_ln_mix_g, m_ln_mlp_g, m_sf_b_f, m_fox_q_g, m_fox_k_g, m_mla_q_a_g, m_mla_kv_a_g, m_mla_q_g,
                   m_mla_k_g),
        pack_small(v_ln_mix_g, v_ln_mlp_g, v_sf_b_f, v_fox_q_g, v_fox_k_g, v_mla_q_a_g, v_mla_kv_a_g, v_mla_q_g,
                   v_mla_k_g), name="adamw_small")
    d_small, m_small, v_small = [unpack_small(p) for p in small_out]

    def big(w, g, m, v, tag, comm=None):
        shp = w.shape
        two_d = lambda t: t.reshape(-1, shp[-1])
        out = _adamw(two_d(w), g, two_d(m), two_d(v), name=f"adamw_{tag}", comm=comm)
        (d, mn, vn), landed = out if comm is not None else (out, None)
        res = (g.reshape(shp), d.reshape(shp), mn.reshape(shp), vn.reshape(shp))
        return res if comm is None else (res, landed)

    r_up, slots = big(mlp_w_up, gs_up, m_mlp_w_up, v_mlp_w_up, "w_up", comm=ex_mix0_grads)
    gs_in, gs_o0 = reduce_end(slots, p_mix0, tags_mix0, "reduce_mix0")
    r_dw = big(mlp_w_down, gs_dw, m_mlp_w_down, v_mlp_w_down, "w_dw")
    r_in = big(sf_w_in, gs_in, m_sf_w_in, v_sf_w_in, "w_in")
    r_o0 = big(sf_w_o, gs_o0, m_sf_w_o, v_sf_w_o, "w_o0")
    r_dn = big(mla_w_down, gs_dn, m_mla_w_down, v_mla_w_down, "w_dn")
    r_uq = big(mla_w_uq, gs_uq, m_mla_w_uq, v_mla_w_uq, "w_uq")
    r_ukv = big(mla_w_ukv, gs_ukv, m_mla_w_ukv, v_mla_w_ukv, "w_ukv")
    r_o1 = big(mla_w_o, gs_o1, m_mla_w_o, v_mla_w_o, "w_o1")

    g_small = (g_ln_mix, g_ln_mlp, g_b_f, g_fq, g_fk, g_qa, g_kva, g_q, g_k)

    def ordered(k, sm):
        return (sm[0], sm[1], r_in[k], sm[2], sm[3], sm[4], r_o0[k], r_dn[k], sm[5], sm[6], r_uq[k], r_ukv[k],
                sm[7], sm[8], r_o1[k], r_up[k], r_dw[k])

    return (loss, grad_x[None], *ordered(0, g_small), *ordered(1, d_small), *ordered(2, m_small),
            *ordered(3, v_small))
```

```python
import functools

import numpy as np
import jax
import jax.numpy as jnp
from jax import lax
from jax.experimental import pallas as pl
from jax.experimental.pallas import tpu as pltpu

F32 = jnp.float32
BF16 = jnp.bfloat16
MESH = pl.DeviceIdType.MESH

EPS = 1e-6
HEAD = 128
N_SB = 8
N_FOX = 8
N_MLA = 16
Q_RANK = 512
KV_RANK = 512
NOPE = 128
ROPE = 64
QK_DIM = NOPE + ROPE
QK_PAD = 256
ROPE_THETA = 10000.0
N_CHIPS = 4

ADAM_LR = 0.001
ADAM_B1 = 0.9
ADAM_B2 = 0.999
ADAM_EPS = 1e-08
ADAM_WD = 0.01
ADAM_STEP = 10

VMEM_LIMIT = 56 * 1024 * 1024
LANES = 128
NEG = -1e30


def _cp(*sem):
    return pltpu.CompilerParams(dimension_semantics=sem, vmem_limit_bytes=VMEM_LIMIT)


def _pick(dim, target):
    if dim <= target:
        return dim
    t = (target // LANES) * LANES
    while t >= LANES:
        if dim % t == 0:
            return t
        t -= LANES
    raise ValueError(f"no tile for {dim}")


NT_DIMS = (((1,), (1,)), ((), ()))
TN_DIMS = (((0,), (0,)), ((), ()))


def _dot(a, b):
    return jnp.dot(a, b, preferred_element_type=F32)


def _dot_nt(a, b):
    return lax.dot_general(a, b, NT_DIMS, preferred_element_type=F32)


def _dot_tn(a, b):
    return lax.dot_general(a, b, TN_DIMS, preferred_element_type=F32)


def _matmul(a, b, *, name, form="nn", out_dtype=F32, n=None, b_n0=0, b_split=False,
            out_split=False, epilogue="plain", res=None, u=None, tm=1024, tn=1024, tk=2048, comm=None):
    if form == "tn":
        K, M = a.shape
    else:
        M, K = a.shape
    if b_split:
        if form == "nt":
            nb_full, kb_full = b.shape[1], b.shape[2] * N_CHIPS
        else:
            kb_full, nb_full = b.shape[1], b.shape[2] * N_CHIPS
    elif form == "nt":
        nb_full, kb_full = b.shape
    else:
        kb_full, nb_full = b.shape
    assert kb_full == K, (name, a.shape, b.shape)
    N = nb_full if n is None else n
    if a.dtype != BF16 or b.dtype != BF16:
        tk = max(tk // 2, LANES)
    tm, tn, tk = _pick(M, tm), _pick(N, tn), _pick(K, tk)
    if b_split:
        per_chip = (b.shape[2])
        if form == "nt":
            tk = _pick(per_chip, tk)
        else:
            tn = _pick(per_chip, tn)
    if out_split:
        tn = _pick(N // N_CHIPS, tn)
    assert b_n0 % tn == 0
    nb0 = b_n0 // tn
    nk = K // tk
    grid = (M // tm, N // tn, nk)

    if form == "tn":
        a_spec = pl.BlockSpec((tk, tm), lambda i, j, k: (k, i))
    else:
        a_spec = pl.BlockSpec((tm, tk), lambda i, j, k: (i, k))
    if b_split:
        if form == "nt":
            kc = b.shape[2] // tk
            b_spec = pl.BlockSpec((None, tn, tk), lambda i, j, k: (k // kc, j, k % kc))
        else:
            nc = b.shape[2] // tn
            b_spec = pl.BlockSpec((None, tk, tn), lambda i, j, k: (j // nc, k, j % nc))
    elif form == "nt":
        b_spec = pl.BlockSpec((tn, tk), lambda i, j, k: (j + nb0, k))
    else:
        b_spec = pl.BlockSpec((tk, tn), lambda i, j, k: (k, j + nb0))
    mn_spec = pl.BlockSpec((tm, tn), lambda i, j, k: (i, j))
    if out_split:
        oc = (N // N_CHIPS) // tn
        out_spec = pl.BlockSpec((None, tm, tn), lambda i, j, k: (j // oc, i, j % oc))
        out_shape = jax.ShapeDtypeStruct((N_CHIPS, M, N // N_CHIPS), out_dtype)
    else:
        out_spec = mn_spec
        out_shape = jax.ShapeDtypeStruct((M, N), out_dtype)

    in_specs = [a_spec, b_spec]
    operands = [a, b]
    out_specs = (out_spec,)
    out_shape = (out_shape,)
    if epilogue == "res":
        in_specs.append(mn_spec)
        operands.append(res)
    elif epilogue == "sqrelu_bwd":
        in_specs.append(mn_spec)
        operands.append(u)
    elif epilogue == "sqrelu":
        out_specs = (mn_spec, mn_spec)
        out_shape = (jax.ShapeDtypeStruct((M, N), F32), jax.ShapeDtypeStruct((M, N), BF16))

    def finish(refs, r):
        if epilogue == "plain":
            refs[2][...] = r.astype(out_dtype)
        elif epilogue == "res":
            refs[3][...] = (refs[2][...] + r).astype(out_dtype)
        elif epilogue == "sqrelu":
            refs[2][...] = r
            p = jnp.maximum(r, 0.0)
            refs[3][...] = (p * p).astype(BF16)
        else:
            refs[3][...] = (r * (2.0 * jnp.maximum(refs[2][...], 0.0))).astype(out_dtype)

    def body(*refs):
        at = refs[0][...].astype(BF16)
        bt = refs[1][...].astype(BF16)
        if form == "nn":
            part = _dot(at, bt)
        elif form == "nt":
            part = _dot_nt(at, bt)
        else:
            part = _dot_tn(at, bt)
        if nk == 1:
            finish(refs, part)
            return
        acc = refs[-1]
        k = pl.program_id(2)

        @pl.when(k == 0)
        def _():
            acc[...] = part

        @pl.when(jnp.logical_and(k > 0, k < nk - 1))
        def _():
            acc[...] += part

        @pl.when(k == nk - 1)
        def _():
            finish(refs, acc[...] + part)

    outs, comm_outs = _hosted_call(
        body, grid=grid, in_specs=in_specs, out_specs=out_specs, out_shape=out_shape,
        scratch_shapes=[] if nk == 1 else [pltpu.VMEM((tm, tn), F32)], operands=operands, name=name,
        sem=("parallel", "parallel", "arbitrary"), comm=comm)
    result = outs if epilogue == "sqrelu" else outs[0]
    return result if comm is None else (result, comm_outs)


def _rms_fwd(x, g, *, name, c0=0, width=None, gw=None, tr=256):
    R, ctot = x.shape
    C = ctot if width is None else width
    gw = C if gw is None else gw
    assert c0 % C == 0 and C % gw == 0
    tr = _pick(R, tr)
    cb = c0 // C
    ng = C // gw

    def body(x_ref, g_ref, o_ref):
        gv = g_ref[...]
        for gi in range(ng):
            cols = slice(gi * gw, (gi + 1) * gw)
            xs = x_ref[:, cols]
            ms = jnp.sum(xs * xs, axis=-1, keepdims=True) * (1.0 / gw)
            o_ref[:, cols] = ((xs * lax.rsqrt(ms + EPS)) * gv).astype(o_ref.dtype)

    return pl.pallas_call(
        body, out_shape=jax.ShapeDtypeStruct((R, C), BF16), grid=(R // tr,),
        in_specs=[pl.BlockSpec((tr, C), lambda i: (i, cb)), pl.BlockSpec((1, gw), lambda i: (0, 0))],
        out_specs=pl.BlockSpec((tr, C), lambda i: (i, 0)), name=name,
        compiler_params=_cp("parallel"))(x, g.reshape(1, gw).astype(F32))


def _rms_bwd(x, g, dy, *, name, res=None, c0=0, width=None, gw=None, tr=256, bf16_copy=False):
    bf16_copy = int(bf16_copy)
    R, ctot = x.shape
    C = ctot if width is None else width
    gw = C if gw is None else gw
    tr = _pick(R, tr)
    cb = c0 // C
    ng = C // gw
    nsteps = R // tr
    row_spec = pl.BlockSpec((tr, C), lambda i: (i, 0))
    in_specs = [pl.BlockSpec((tr, C), lambda i: (i, cb)), pl.BlockSpec((1, gw), lambda i: (0, 0)), row_spec]
    operands = [x, g.reshape(1, gw).astype(F32), dy]
    if res is not None:
        in_specs.append(row_spec)
        operands.append(res)

    def body(*refs):
        x_ref, g_ref, dy_ref = refs[:3]
        res_ref = refs[3] if res is not None else None
        dx_ref, dg_ref = refs[-3 - bf16_copy], refs[-2]
        acc = refs[-1]
        i = pl.program_id(0)

        @pl.when(i == 0)
        def _():
            acc[...] = jnp.zeros_like(acc)

        gv = g_ref[...]
        for gi in range(ng):
            cols = slice(gi * gw, (gi + 1) * gw)
            xs = x_ref[:, cols]
            dys = dy_ref[:, cols].astype(F32)
            rstd = lax.rsqrt(jnp.sum(xs * xs, axis=-1, keepdims=True) * (1.0 / gw) + EPS)
            xh = xs * rstd
            gdy = dys * gv
            m = jnp.sum(gdy * xh, axis=-1, keepdims=True) * (1.0 / gw)
            dx = rstd * (gdy - xh * m)
            if res_ref is not None:
                dx = dx + res_ref[:, cols]
            dx_ref[:, cols] = dx
            if bf16_copy:
                refs[-3][:, cols] = dx.astype(BF16)
            acc[...] += jnp.sum((dys * xh).reshape(tr // 8, 8, gw), axis=0)

        @pl.when(i == nsteps - 1)
        def _():
            dg_ref[...] = jnp.sum(acc[...], axis=0, keepdims=True)

    out_shape = [jax.ShapeDtypeStruct((R, C), F32)] + [jax.ShapeDtypeStruct((R, C), BF16)] * bf16_copy
    outs = pl.pallas_call(
        body, out_shape=tuple(out_shape + [jax.ShapeDtypeStruct((1, gw), F32)]),
        grid=(nsteps,), in_specs=in_specs,
        out_specs=tuple([row_spec] * len(out_shape) + [pl.BlockSpec((1, gw), lambda i: (0, 0))]),
        scratch_shapes=[pltpu.VMEM((8, gw), F32)], name=name,
        compiler_params=_cp("arbitrary"))(*operands)
    return (*outs[:-1], outs[-1][0])


def _split3(x):
    hi = x.astype(BF16)
    r1 = x - hi.astype(F32)
    mid = r1.astype(BF16)
    lo = (r1 - mid.astype(F32)).astype(BF16)
    return hi, mid, lo


def _log_sigmoid(z):
    return jnp.minimum(z, 0.0) - jnp.log(1.0 + jnp.exp(-jnp.abs(z)))


def _forget_fwd(fl, b, *, name, tb=512):
    S = fl.shape[0]
    tb = _pick(S, tb)

    def body(fl_ref, b_ref, f_ref, carry):
        i = pl.program_id(0)

        @pl.when(i == 0)
        def _():
            carry[...] = jnp.zeros_like(carry)

        lf = _log_sigmoid(fl_ref[...] + b_ref[...])
        r = lax.broadcasted_iota(jnp.int32, (tb, tb), 0)
        c = lax.broadcasted_iota(jnp.int32, (tb, tb), 1)
        tri = (c <= r).astype(BF16)
        hi, mid, lo = _split3(lf)
        cs = _dot(tri, hi) + _dot(tri, mid) + _dot(tri, lo)
        f_ref[...] = cs + carry[...]
        carry[...] += jnp.sum(lf, axis=0, keepdims=True)

    return pl.pallas_call(
        body, out_shape=jax.ShapeDtypeStruct((S, LANES), F32), grid=(S // tb,),
        in_specs=[pl.BlockSpec((tb, LANES), lambda i: (i, 0)), pl.BlockSpec((1, LANES), lambda i: (0, 0))],
        out_specs=pl.BlockSpec((tb, LANES), lambda i: (i, 0)),
        scratch_shapes=[pltpu.VMEM((1, LANES), F32)], name=name,
        compiler_params=_cp("arbitrary"))(fl, b)


def _forget_bwd(fl, b, dF, *, name, tb=512):
    S = fl.shape[0]
    tb = _pick(S, tb)
    nb = S // tb

    def body(fl_ref, b_ref, df_ref, dfl_ref, db_ref, carry, acc):
        i = pl.program_id(0)

        @pl.when(i == 0)
        def _():
            carry[...] = jnp.zeros_like(carry)
            acc[...] = jnp.zeros_like(acc)

        d = df_ref[...]
        r = lax.broadcasted_iota(jnp.int32, (tb, tb), 0)
        c = lax.broadcasted_iota(jnp.int32, (tb, tb), 1)
        tri = (c >= r).astype(BF16)
        hi, mid, lo = _split3(d)
        rc = _dot(tri, hi) + _dot(tri, mid) + _dot(tri, lo) + carry[...]
        z = fl_ref[...] + b_ref[...]
        dfl = rc * jnp.exp(_log_sigmoid(-z))
        dfl_ref[...] = dfl
        carry[...] += jnp.sum(d, axis=0, keepdims=True)
        acc[...] += jnp.sum(dfl, axis=0, keepdims=True)

        @pl.when(i == nb - 1)
        def _():
            db_ref[...] = acc[...]

    rev = lambda i: (nb - 1 - i, 0)
    dfl, db = pl.pallas_call(
        body, out_shape=(jax.ShapeDtypeStruct((S, LANES), F32), jax.ShapeDtypeStruct((1, LANES), F32)),
        grid=(nb,),
        in_specs=[pl.BlockSpec((tb, LANES), rev), pl.BlockSpec((1, LANES), lambda i: (0, 0)),
                  pl.BlockSpec((tb, LANES), rev)],
        out_specs=(pl.BlockSpec((tb, LANES), rev), pl.BlockSpec((1, LANES), lambda i: (0, 0))),
        scratch_shapes=[pltpu.VMEM((1, LANES), F32), pltpu.VMEM((1, LANES), F32)], name=name,
        compiler_params=_cp("arbitrary"))(fl, b, dF)
    return dfl, db[0]


def _tri(tk, rel):
    r = lax.broadcasted_iota(jnp.int32, (tk, tk), 0)
    c = lax.broadcasted_iota(jnp.int32, (tk, tk), 1)
    m = {"gt": r > c, "le": r <= c, "lt": r < c}[rel]
    return m.astype(BF16)


def _split2(x):
    hi = x.astype(BF16)
    return hi, (x - hi.astype(F32)).astype(BF16)


HEADS_PER_STEP = 2
CUM_CHUNK = 256


def _cum_cols(x, tri, suffix):
    ck = tri.shape[0]
    n = x.shape[1] // ck
    hi, lo = _split2(x)
    parts, sums = [], []
    for c in range(n):
        cs = slice(c * ck, (c + 1) * ck)
        parts.append(_dot(hi[:, cs], tri) + _dot(lo[:, cs], tri))
        sums.append(jnp.sum(x[:, cs], axis=1, keepdims=True))
    carry = None
    for c in (reversed(range(n)) if suffix else range(n)):
        if carry is not None:
            parts[c] = parts[c] + carry
        carry = sums[c] if carry is None else carry + sums[c]
    return (parts[0] if n == 1 else jnp.concatenate(parts, axis=1)), carry


def _diag_mask(tq, strict):
    r = lax.broadcasted_iota(jnp.int32, (tq, tq), 0)
    c = lax.broadcasted_iota(jnp.int32, (tq, tq), 1)
    return c < r if strict else c <= r


def _sb_fwd(qkv, *, name, n_heads=N_SB, q_off=0, k_off=N_SB, v_off=2 * N_SB, tq=512, hp=HEADS_PER_STEP,
            comm=None):
    S = qkv.shape[0]
    tq = _pick(S, tq)
    tk = tq
    scale = HEAD ** -0.5
    nq = S // tq
    assert n_heads % hp == 0 and q_off % hp == 0 and k_off % hp == 0 and v_off % hp == 0

    def body(q_ref, k_ref, v_ref, o_ref, t_ref, c_sc, acc_sc):
        qi = pl.program_id(1)
        c_sc[...] = jnp.zeros_like(c_sc)
        acc_sc[...] = jnp.zeros_like(acc_sc)
        gt = _tri(min(CUM_CHUNK, tk), "gt")

        def tile(hh, j, diag):
            cs = slice(hh * HEAD, (hh + 1) * HEAD)
            rows = pl.ds(pl.multiple_of(j * tk, tk), tk)
            z = _dot_nt(q_ref[:, cs], k_ref[rows, cs]) * scale
            sp = jnp.log(1.0 + jnp.exp(-jnp.abs(z)))
            la = jnp.minimum(z, 0.0) - sp
            lb = -jnp.maximum(z, 0.0) - sp
            if diag:
                strict = _diag_mask(tq, True)
                lb = jnp.where(strict, lb, 0.0)
            suffix, total = _cum_cols(lb, gt, suffix=True)
            w = jnp.exp(la + suffix + c_sc[hh])
            if diag:
                w = jnp.where(strict, w, 0.0)
            acc_sc[hh] += _dot(w.astype(BF16), v_ref[rows, cs])
            c_sc[hh] += total

        for hh in range(hp):
            tile(hh, qi, True)

        def step(it, carry):
            for hh in range(hp):
                tile(hh, qi - 1 - it, False)
            return carry

        lax.fori_loop(0, qi, step, 0)
        for hh in range(hp):
            o_ref[:, hh * HEAD:(hh + 1) * HEAD] = acc_sc[hh]
            t_ref[hh] = jnp.broadcast_to(c_sc[hh], (tq, LANES))

    w = hp * HEAD
    head_blk = lambda off: pl.BlockSpec((S, w), lambda h, i: (0, h + off // hp))
    outs, comm_outs = _hosted_call(
        body,
        out_shape=(jax.ShapeDtypeStruct((S, n_heads * HEAD), F32),
                   jax.ShapeDtypeStruct((n_heads, S, LANES), F32)),
        grid=(n_heads // hp, nq),
        in_specs=[pl.BlockSpec((tq, w), lambda h, i: (i, h + q_off // hp)), head_blk(k_off), head_blk(v_off)],
        out_specs=(pl.BlockSpec((tq, w), lambda h, i: (i, h)),
                   pl.BlockSpec((hp, tq, LANES), lambda h, i: (h, i, 0))),
        scratch_shapes=[pltpu.VMEM((hp, tq, 1), F32), pltpu.VMEM((hp, tq, HEAD), F32)], name=name,
        sem=("parallel", "arbitrary"), operands=(qkv, qkv, qkv), comm=comm)
    return outs if comm is None else (outs, comm_outs)


def _sb_bwd(qkv, do, tstat, *, name, n_heads=N_SB, q_off=0, k_off=N_SB, v_off=2 * N_SB, do_off=0, tq=512,
            hp=HEADS_PER_STEP, comm=None):
    S = qkv.shape[0]
    tq = _pick(S, tq)
    tk = tq
    scale = HEAD ** -0.5
    nq = S // tq
    assert n_heads % hp == 0 and q_off % hp == 0 and k_off % hp == 0 and v_off % hp == 0 and do_off % hp == 0

    def body(q_ref, k_ref, v_ref, do_ref, t_ref, dq_ref, dk_ref, dv_ref, p_sc, r_sc, dq_sc):
        qi = pl.program_id(1)

        @pl.when(qi == 0)
        def _():
            dk_ref[...] = jnp.zeros_like(dk_ref)
            dv_ref[...] = jnp.zeros_like(dv_ref)

        p_sc[...] = jnp.zeros_like(p_sc)
        r_sc[...] = jnp.zeros_like(r_sc)
        dq_sc[...] = jnp.zeros_like(dq_sc)
        le = _tri(min(CUM_CHUNK, tk), "le")
        lt = _tri(min(CUM_CHUNK, tk), "lt")

        def tile(hh, j, diag):
            cs = slice(hh * HEAD, (hh + 1) * HEAD)
            rows = pl.ds(pl.multiple_of(j * tk, tk), tk)
            q = q_ref[:, cs]
            do_b = do_ref[:, cs].astype(BF16)
            kb = k_ref[rows, cs]
            z = _dot_nt(q, kb) * scale
            sp = jnp.log(1.0 + jnp.exp(-jnp.abs(z)))
            la = jnp.minimum(z, 0.0) - sp
            lb = -jnp.maximum(z, 0.0) - sp
            if diag:
                strict = _diag_mask(tq, True)
                lb = jnp.where(strict, lb, 0.0)
            prefix, total_b = _cum_cols(lb, le, suffix=False)
            w = jnp.exp(la + t_ref[hh, :, 0:1] - (prefix + p_sc[hh]))
            if diag:
                w = jnp.where(strict, w, 0.0)
            r = w * _dot_nt(do_b, v_ref[rows, cs])
            rex, total_r = _cum_cols(r, lt, suffix=False)
            rex = rex + r_sc[hh]
            beta = jnp.exp(la)
            dz = (r * (1.0 - beta) - rex * beta) * scale
            if diag:
                dz = jnp.where(strict, dz, 0.0)
            dzb = dz.astype(BF16)
            dq_sc[hh] += _dot(dzb, kb)
            dk_ref[rows, cs] += _dot_tn(dzb, q)
            dv_ref[rows, cs] += _dot_tn(w.astype(BF16), do_b)
            p_sc[hh] += total_b
            r_sc[hh] += total_r

        def step(j, carry):
            for hh in range(hp):
                tile(hh, j, False)
            return carry

        lax.fori_loop(0, qi, step, 0)
        for hh in range(hp):
            tile(hh, qi, True)
            dq_ref[:, hh * HEAD:(hh + 1) * HEAD] = dq_sc[hh]

    w = hp * HEAD
    head_blk = lambda off: pl.BlockSpec((S, w), lambda h, i: (0, h + off // hp))
    out_head = pl.BlockSpec((S, w), lambda h, i: (0, h))
    out_sd = jax.ShapeDtypeStruct((S, n_heads * HEAD), F32)
    outs, comm_outs = _hosted_call(
        body, out_shape=(out_sd, out_sd, out_sd), grid=(n_heads // hp, nq),
        in_specs=[pl.BlockSpec((tq, w), lambda h, i: (i, h + q_off // hp)), head_blk(k_off), head_blk(v_off),
                  pl.BlockSpec((tq, w), lambda h, i: (i, h + do_off // hp)),
                  pl.BlockSpec((hp, tq, LANES), lambda h, i: (h, i, 0))],
        out_specs=(pl.BlockSpec((tq, w), lambda h, i: (i, h)), out_head, out_head),
        scratch_shapes=[pltpu.VMEM((hp, tq, 1), F32), pltpu.VMEM((hp, tq, 1), F32), pltpu.VMEM((hp, tq, HEAD), F32)],
        name=name, sem=("parallel", "arbitrary"), operands=(qkv, qkv, qkv, do, tstat), comm=comm)
    return outs if comm is None else (outs, comm_outs)


def _attn_fwd(q, k, v, bias, *, name, n_heads, dqk, scale, v_off=0, tq=1024, exact_p=False, hp=HEADS_PER_STEP,
              comm=None):
    S = q.shape[0]
    tq = _pick(S, tq)
    tk = tq
    nq = S // tq
    has_bias = bias is not None

    assert n_heads % hp == 0 and v_off % hp == 0

    def body(*refs):
        q_ref, k_ref, v_ref = refs[:3]
        b_ref = refs[3] if has_bias else None
        o_ref, lse_ref, m_sc, l_sc, acc_sc = refs[-5:]
        qi = pl.program_id(1)
        m_sc[...] = jnp.full_like(m_sc, NEG)
        l_sc[...] = jnp.zeros_like(l_sc)
        acc_sc[...] = jnp.zeros_like(acc_sc)

        def tile(hh, j, diag):
            rows = pl.ds(pl.multiple_of(j * tk, tk), tk)
            s = _dot_nt(q_ref[:, hh * dqk:(hh + 1) * dqk], k_ref[rows, hh * dqk:(hh + 1) * dqk]) * scale
            if has_bias:
                s = s + b_ref[hh, :, rows]
            if diag:
                s = jnp.where(_diag_mask(tq, False), s, NEG)
            m_old = m_sc[hh]
            m_new = jnp.maximum(m_old, jnp.max(s, axis=1, keepdims=True))
            alpha = jnp.exp(m_old - m_new)
            p = jnp.exp(s - m_new)
            l_sc[hh] = alpha * l_sc[hh] + jnp.sum(p, axis=1, keepdims=True)
            vb = v_ref[rows, hh * HEAD:(hh + 1) * HEAD]
            if exact_p:
                hi, lo = _split2(p)
                pv = _dot(hi, vb) + _dot(lo, vb)
            else:
                pv = _dot(p.astype(BF16), vb)
            acc_sc[hh] = alpha * acc_sc[hh] + pv
            m_sc[hh] = m_new

        def step(j, carry):
            for hh in range(hp):
                tile(hh, j, False)
            return carry

        lax.fori_loop(0, qi, step, 0)
        for hh in range(hp):
            tile(hh, qi, True)
            l = l_sc[hh]
            o_ref[:, hh * HEAD:(hh + 1) * HEAD] = acc_sc[hh] / l
            lse_ref[hh] = jnp.broadcast_to(m_sc[hh] + jnp.log(l), (tq, LANES))

    in_specs = [pl.BlockSpec((tq, hp * dqk), lambda h, i: (i, h)),
                pl.BlockSpec((S, hp * dqk), lambda h, i: (0, h)),
                pl.BlockSpec((S, hp * HEAD), lambda h, i: (0, h + v_off // hp))]
    operands = [q, k, v]
    if has_bias:
        in_specs.append(pl.BlockSpec((hp, 1, S), lambda h, i: (h, 0, 0)))
        operands.append(bias)
    outs, comm_outs = _hosted_call(
        body,
        out_shape=(jax.ShapeDtypeStruct((S, n_heads * HEAD), F32),
                   jax.ShapeDtypeStruct((n_heads, S, LANES), F32)),
        grid=(n_heads // hp, nq), in_specs=in_specs,
        out_specs=(pl.BlockSpec((tq, hp * HEAD), lambda h, i: (i, h)),
                   pl.BlockSpec((hp, tq, LANES), lambda h, i: (h, i, 0))),
        scratch_shapes=[pltpu.VMEM((hp, tq, 1), F32), pltpu.VMEM((hp, tq, 1), F32), pltpu.VMEM((hp, tq, HEAD), F32)],
        name=name, sem=("parallel", "arbitrary"), operands=operands, comm=comm)
    return outs if comm is None else (outs, comm_outs)


def _attn_bwd(q, k, v, bias, o, lse, do, *, name, n_heads, dqk, scale, v_off=0, do_off=0, tq=512,
              hp=HEADS_PER_STEP, comm=None):
    S = q.shape[0]
    tq = _pick(S, tq)
    tk = tq
    nq = S // tq
    has_bias = bias is not None
    assert n_heads % hp == 0 and v_off % hp == 0 and do_off % hp == 0

    def body(*refs):
        q_ref, k_ref, v_ref, o_ref, lse_ref, do_ref = refs[:6]
        b_ref = refs[6] if has_bias else None
        n_out = 5 if has_bias else 3
        outs = refs[-(n_out + 3):-3]
        dq_ref, dk_ref, dv_ref = outs[:3]
        db_ref, dr_ref = (outs[3], outs[4]) if has_bias else (None, None)
        dq_sc, rs_sc, delta_sc = refs[-3:]
        qi = pl.program_id(1)

        @pl.when(qi == 0)
        def _():
            dk_ref[...] = jnp.zeros_like(dk_ref)
            dv_ref[...] = jnp.zeros_like(dv_ref)
            if has_bias:
                db_ref[...] = jnp.zeros_like(db_ref)

        dq_sc[...] = jnp.zeros_like(dq_sc)
        rs_sc[...] = jnp.zeros_like(rs_sc)
        for hh in range(hp):
            vs = slice(hh * HEAD, (hh + 1) * HEAD)
            do_r = do_ref[:, vs].astype(BF16).astype(F32)
            delta_sc[hh] = jnp.sum(do_r * o_ref[:, vs], axis=1, keepdims=True)

        def tile(hh, j, diag):
            qs = slice(hh * dqk, (hh + 1) * dqk)
            vs = slice(hh * HEAD, (hh + 1) * HEAD)
            rows = pl.ds(pl.multiple_of(j * tk, tk), tk)
            qb = q_ref[:, qs]
            do_b = do_ref[:, vs].astype(BF16)
            delta = delta_sc[hh]
            kb = k_ref[rows, qs]
            s = _dot_nt(qb, kb) * scale
            if has_bias:
                s = s + b_ref[hh, :, rows]
            p = jnp.exp(s - lse_ref[hh, :, 0:1])
            if diag:
                p = jnp.where(_diag_mask(tq, False), p, 0.0)
            ds = p * (_dot_nt(do_b, v_ref[rows, vs]) - delta)
            dsb = (ds * scale).astype(BF16)
            dq_sc[hh] += _dot(dsb, kb)
            dk_ref[rows, qs] += _dot_tn(dsb, qb)
            dv_ref[rows, vs] += _dot_tn(p.astype(BF16), do_b)
            if has_bias:
                db_ref[hh, :, rows] += jnp.sum(ds, axis=0, keepdims=True)
                rs_sc[hh] += jnp.sum(ds, axis=1, keepdims=True)

        def step(j, carry):
            for hh in range(hp):
                tile(hh, j, False)
            return carry

        lax.fori_loop(0, qi, step, 0)
        for hh in range(hp):
            tile(hh, qi, True)
            dq_ref[:, hh * dqk:(hh + 1) * dqk] = dq_sc[hh]
            if has_bias:
                dr_ref[hh] = jnp.broadcast_to(rs_sc[hh], (tq, LANES))

    stat = pl.BlockSpec((hp, tq, LANES), lambda h, i: (h, i, 0))
    in_specs = [pl.BlockSpec((tq, hp * dqk), lambda h, i: (i, h)),
                pl.BlockSpec((S, hp * dqk), lambda h, i: (0, h)),
                pl.BlockSpec((S, hp * HEAD), lambda h, i: (0, h + v_off // hp)),
                pl.BlockSpec((tq, hp * HEAD), lambda h, i: (i, h)),
                stat,
                pl.BlockSpec((tq, hp * HEAD), lambda h, i: (i, h + do_off // hp))]
    operands = [q, k, v, o, lse, do]
    out_shape = [jax.ShapeDtypeStruct((S, n_heads * dqk), F32), jax.ShapeDtypeStruct((S, n_heads * dqk), F32),
                 jax.ShapeDtypeStruct((S, n_heads * HEAD), F32)]
    out_specs = [pl.BlockSpec((tq, hp * dqk), lambda h, i: (i, h)), pl.BlockSpec((S, hp * dqk), lambda h, i: (0, h)),
                 pl.BlockSpec((S, hp * HEAD), lambda h, i: (0, h))]
    if has_bias:
        in_specs.append(pl.BlockSpec((hp, 1, S), lambda h, i: (h, 0, 0)))
        operands.append(bias)
        out_shape.append(jax.ShapeDtypeStruct((n_heads, 1, S), F32))
        out_specs.append(pl.BlockSpec((hp, 1, S), lambda h, i: (h, 0, 0)))
        out_shape.append(jax.ShapeDtypeStruct((n_heads, S, LANES), F32))
        out_specs.append(stat)
    outs, comm_outs = _hosted_call(
        body, out_shape=tuple(out_shape), grid=(n_heads // hp, nq), in_specs=in_specs, out_specs=tuple(out_specs),
        scratch_shapes=[pltpu.VMEM((hp, tq, dqk), F32), pltpu.VMEM((hp, tq, 1), F32),
                        pltpu.VMEM((hp, tq, 1), F32)], name=name,
        sem=("parallel", "arbitrary"), operands=operands, comm=comm)
    return outs if comm is None else (outs, comm_outs)


def _rot_half(y):
    lane = lax.broadcasted_iota(jnp.int32, y.shape, 1)
    up = pltpu.roll(y, 96, 1)
    down = pltpu.roll(y, 32, 1)
    return jnp.where(lane < 32, -up, jnp.where(lane < 64, down, 0.0))


def _mla_prep_fwd(q_raw, kv_raw, down, cos, sin, q_g, k_g, *, name, ts=128):
    S = q_raw.shape[0]
    ts = _pick(S, ts)
    pe_blk = Q_RANK // LANES + KV_RANK // LANES

    def norm_rope(x0, x1, g0, g1, c, s):
        ms = (jnp.sum(x0 * x0, axis=-1, keepdims=True) + jnp.sum(x1 * x1, axis=-1, keepdims=True)) * (1.0 / QK_DIM)
        rstd = lax.rsqrt(ms + EPS)
        y0 = (x0 * rstd) * g0
        y1 = (x1 * rstd) * g1
        return y0, y1 * c + _rot_half(y1) * s

    def body(q_ref, kv_ref, pe_ref, cos_ref, sin_ref, qg_ref, kg_ref, qo_ref, ko_ref, vo_ref):
        c, s = cos_ref[...], sin_ref[...]
        pe = pe_ref[...]
        qg0, qg1 = qg_ref[:, :NOPE], qg_ref[:, NOPE:]
        kg0, kg1 = kg_ref[:, :NOPE], kg_ref[:, NOPE:]
        for h in range(N_MLA):
            b = h * QK_PAD
            y0, y1 = norm_rope(q_ref[:, b:b + NOPE], q_ref[:, b + NOPE:b + QK_PAD], qg0, qg1, c, s)
            qo_ref[:, b:b + NOPE] = y0.astype(BF16)
            qo_ref[:, b + NOPE:b + QK_PAD] = y1.astype(BF16)
            y0, y1 = norm_rope(kv_ref[:, b:b + NOPE], pe, kg0, kg1, c, s)
            ko_ref[:, b:b + NOPE] = y0.astype(BF16)
            ko_ref[:, b + NOPE:b + QK_PAD] = y1.astype(BF16)
            vo_ref[:, h * HEAD:(h + 1) * HEAD] = kv_ref[:, b + NOPE:b + QK_PAD].astype(BF16)

    wide = pl.BlockSpec((ts, N_MLA * QK_PAD), lambda i: (i, 0))
    lane_blk = pl.BlockSpec((ts, LANES), lambda i: (i, 0))
    gain = pl.BlockSpec((1, QK_PAD), lambda i: (0, 0))
    return pl.pallas_call(
        body,
        out_shape=(jax.ShapeDtypeStruct((S, N_MLA * QK_PAD), BF16), jax.ShapeDtypeStruct((S, N_MLA * QK_PAD), BF16),
                   jax.ShapeDtypeStruct((S, N_MLA * HEAD), BF16)),
        grid=(S // ts,),
        in_specs=[wide, wide, pl.BlockSpec((ts, LANES), lambda i: (i, pe_blk)), lane_blk, lane_blk, gain, gain],
        out_specs=(wide, wide, pl.BlockSpec((ts, N_MLA * HEAD), lambda i: (i, 0))), name=name,
        compiler_params=_cp("parallel"))(q_raw, kv_raw, down, cos, sin, q_g, k_g)


def _mla_prep_bwd(dq, dk, dv, q_raw, kv_raw, down, cos, sin, q_g, k_g, *, name, ts=128):
    S = q_raw.shape[0]
    ts = _pick(S, ts)
    nsteps = S // ts
    pe_blk = Q_RANK // LANES + KV_RANK // LANES

    def back(x0, x1, g0, g1, c, s, d0, d1r):
        d1 = d1r * c - _rot_half(d1r * s)
        ms = (jnp.sum(x0 * x0, axis=-1, keepdims=True) + jnp.sum(x1 * x1, axis=-1, keepdims=True)) * (1.0 / QK_DIM)
        rstd = lax.rsqrt(ms + EPS)
        h0, h1 = x0 * rstd, x1 * rstd
        e0, e1 = d0 * g0, d1 * g1
        m = (jnp.sum(e0 * h0, axis=-1, keepdims=True) + jnp.sum(e1 * h1, axis=-1, keepdims=True)) * (1.0 / QK_DIM)
        return rstd * (e0 - h0 * m), rstd * (e1 - h1 * m), d0 * h0, d1 * h1

    def fold(a):
        return jnp.sum(a.reshape(ts // 8, 8, a.shape[-1]), axis=0)

    def body(dq_ref, dk_ref, dv_ref, q_ref, kv_ref, pe_ref, cos_ref, sin_ref, qg_ref, kg_ref,
             dqr_ref, dkv_ref, dpe_ref, dqg_ref, dkg_ref, gq_sc, gk_sc):
        i = pl.program_id(0)

        @pl.when(i == 0)
        def _():
            gq_sc[...] = jnp.zeros_like(gq_sc)
            gk_sc[...] = jnp.zeros_like(gk_sc)

        c, s = cos_ref[...], sin_ref[...]
        pe = pe_ref[...]
        qg0, qg1 = qg_ref[:, :NOPE], qg_ref[:, NOPE:]
        kg0, kg1 = kg_ref[:, :NOPE], kg_ref[:, NOPE:]
        dpe = jnp.zeros((ts, LANES), F32)
        for h in range(N_MLA):
            b = h * QK_PAD
            dx0, dx1, a0, a1 = back(q_ref[:, b:b + NOPE], q_ref[:, b + NOPE:b + QK_PAD], qg0, qg1, c, s,
                                    dq_ref[:, b:b + NOPE], dq_ref[:, b + NOPE:b + QK_PAD])
            dqr_ref[:, b:b + NOPE] = dx0.astype(BF16)
            dqr_ref[:, b + NOPE:b + QK_PAD] = dx1.astype(BF16)
            gq_sc[:, :NOPE] += fold(a0)
            gq_sc[:, NOPE:] += fold(a1)
            dx0, dx1, a0, a1 = back(kv_ref[:, b:b + NOPE], pe, kg0, kg1, c, s,
                                    dk_ref[:, b:b + NOPE], dk_ref[:, b + NOPE:b + QK_PAD])
            dkv_ref[:, b:b + NOPE] = dx0.astype(BF16)
            dkv_ref[:, b + NOPE:b + QK_PAD] = dv_ref[:, h * HEAD:(h + 1) * HEAD].astype(BF16)
            dpe = dpe + dx1
            gk_sc[:, :NOPE] += fold(a0)
            gk_sc[:, NOPE:] += fold(a1)
        dpe_ref[...] = dpe

        @pl.when(i == nsteps - 1)
        def _():
            dqg_ref[...] = jnp.sum(gq_sc[...], axis=0, keepdims=True)
            dkg_ref[...] = jnp.sum(gk_sc[...], axis=0, keepdims=True)

    wide = pl.BlockSpec((ts, N_MLA * QK_PAD), lambda i: (i, 0))
    lane_blk = pl.BlockSpec((ts, LANES), lambda i: (i, 0))
    gain = pl.BlockSpec((1, QK_PAD), lambda i: (0, 0))
    outs = pl.pallas_call(
        body,
        out_shape=(jax.ShapeDtypeStruct((S, N_MLA * QK_PAD), BF16), jax.ShapeDtypeStruct((S, N_MLA * QK_PAD), BF16),
                   jax.ShapeDtypeStruct((S, LANES), F32), jax.ShapeDtypeStruct((1, QK_PAD), F32),
                   jax.ShapeDtypeStruct((1, QK_PAD), F32)),
        grid=(nsteps,),
        in_specs=[wide, wide, pl.BlockSpec((ts, N_MLA * HEAD), lambda i: (i, 0)), wide, wide,
                  pl.BlockSpec((ts, LANES), lambda i: (i, pe_blk)), lane_blk, lane_blk, gain, gain],
        out_specs=(wide, wide, lane_blk, gain, gain),
        scratch_shapes=[pltpu.VMEM((8, QK_PAD), F32), pltpu.VMEM((8, QK_PAD), F32)], name=name,
        compiler_params=_cp("arbitrary"))(dq, dk, dv, q_raw, kv_raw, down, cos, sin, q_g, k_g)
    return outs[0], outs[1], outs[2], outs[3][0], outs[4][0]


def _loss_head(y, target, *, name, tr=256):
    R, C = y.shape
    tr = _pick(R, tr)
    nsteps = R // tr

    def body(y_ref, t_ref, dy_ref, dyb_ref, loss_ref, acc):
        i = pl.program_id(0)

        @pl.when(i == 0)
        def _():
            acc[...] = jnp.zeros_like(acc)

        err = y_ref[...] - t_ref[...]
        dy = err * (1.0 / C)
        dy_ref[...] = dy
        dyb_ref[...] = dy.astype(BF16)
        acc[...] += jnp.sum((err * err).reshape(tr // 8, 8, C), axis=0)

        @pl.when(i == nsteps - 1)
        def _():
            tot = jnp.sum(jnp.sum(acc[...], axis=0, keepdims=True), axis=1, keepdims=True)
            loss_ref[...] = jnp.broadcast_to(tot * (0.5 / C), (8, LANES))

    blk = pl.BlockSpec((tr, C), lambda i: (i, 0))
    dy, dy_b, loss = pl.pallas_call(
        body, out_shape=(jax.ShapeDtypeStruct((R, C), F32), jax.ShapeDtypeStruct((R, C), BF16),
                         jax.ShapeDtypeStruct((8, LANES), F32)),
        grid=(nsteps,), in_specs=[blk, blk], out_specs=(blk, blk, pl.BlockSpec((8, LANES), lambda i: (0, 0))),
        scratch_shapes=[pltpu.VMEM((8, C), F32)], name=name, compiler_params=_cp("arbitrary"))(y, target)
    return dy, dy_b, loss[0, 0]


def _adamw(w, g, m, v, *, name, block_bytes=1 << 20, comm=None):
    L, R, C = w.shape
    tr = max(8, min(R, (block_bytes // (4 * C)) // 8 * 8))
    while R % tr:
        tr -= 8
    if tr <= 0:
        tr = R
    c1 = 1.0 / (1.0 - ADAM_B1 ** ADAM_STEP)
    c2 = 1.0 / (1.0 - ADAM_B2 ** ADAM_STEP)

    def body(w_ref, g_ref, m_ref, v_ref, d_ref, mo_ref, vo_ref):
        gv = g_ref[...]
        mn = ADAM_B1 * m_ref[...] + (1.0 - ADAM_B1) * gv
        vn = ADAM_B2 * v_ref[...] + (1.0 - ADAM_B2) * (gv * gv)
        d_ref[...] = -ADAM_LR * ((mn * c1) / (jnp.sqrt(vn * c2) + ADAM_EPS) + ADAM_WD * w_ref[...])
        mo_ref[...] = mn
        vo_ref[...] = vn

    blk = pl.BlockSpec((None, tr, C), lambda l, i: (l, i, 0))
    sd = jax.ShapeDtypeStruct((L, R, C), F32)
    outs, comm_outs = _hosted_call(
        body, out_shape=(sd, sd, sd), grid=(L, R // tr), in_specs=[blk] * 4, out_specs=(blk,) * 3, scratch_shapes=[],
        name=name, sem=("parallel", "parallel"), operands=(w, g, m, v), comm=comm)
    return outs if comm is None else (outs, comm_outs)


def _row_tile(r, c, itemsize=4, block_bytes=1 << 20):
    tr = max(16, min(r, (block_bytes // (itemsize * c)) // 16 * 16))
    while r % tr:
        tr -= 16
    return tr if tr > 0 else r


def _add_sibling(g, recv, core, *, name):
    nch, _, r, c = g.shape
    tr = _row_tile(r, c)

    def body(core_ref, g_ref, r_ref, o_ref):
        o_ref[...] = (g_ref[...] + r_ref[...]).astype(BF16)

    grid_spec = pltpu.PrefetchScalarGridSpec(
        num_scalar_prefetch=1, grid=(nch, r // tr),
        in_specs=[pl.BlockSpec((None, None, tr, c), lambda j, i, cr: (j, cr[0], i, 0)),
                  pl.BlockSpec((None, tr, c), lambda j, i, cr: (j, i, 0))],
        out_specs=pl.BlockSpec((None, tr, c), lambda j, i, cr: (j, i, 0)))
    return pl.pallas_call(
        body, out_shape=jax.ShapeDtypeStruct((nch, r, c), BF16), grid_spec=grid_spec, name=name,
        compiler_params=_cp("parallel", "parallel"))(core, g, recv)


def _add_chips(slots, *, name):
    nch, r, c = slots.shape
    tr = _row_tile(r, c)

    def body(s_ref, o_ref):
        acc = s_ref[0].astype(F32)
        for j in range(1, nch):
            acc = acc + s_ref[j].astype(F32)
        o_ref[...] = acc

    return pl.pallas_call(
        body, out_shape=jax.ShapeDtypeStruct((r, c), F32), grid=(r // tr,),
        in_specs=[pl.BlockSpec((nch, tr, c), lambda i: (0, i, 0))],
        out_specs=pl.BlockSpec((tr, c), lambda i: (i, 0)), name=name, compiler_params=_cp("parallel"))(slots)


def _place():
    x, y, c = lax.axis_index("x"), lax.axis_index("y"), lax.axis_index("c")
    others = [(1 - x, y), (x, 1 - y), (1 - x, 1 - y)]
    return x, y, c, 2 * x + y, others


ANY = pl.BlockSpec(memory_space=pl.ANY)


class _Exchange:
    def __init__(self, kind, arrays):
        self.kind, self.ins = kind, list(arrays)
        self.n_peers = 3 if kind in ("gather", "scatter") else 1
        n = len(self.ins) * self.n_peers
        shp = {"gather": lambda a: (N_CHIPS,) + a.shape, "scatter": lambda a: a.shape,
               "swap": lambda a: (a.shape[0],) + a.shape[2:], "join": lambda a: (2,) + a.shape}[kind]
        self.out_shapes = [jax.ShapeDtypeStruct(shp(a), a.dtype) for a in self.ins]
        self.sems = [pltpu.SemaphoreType.DMA((n,)), pltpu.SemaphoreType.DMA((n,))]

    def _copies(self, ins, outs, sems):
        send, recv = sems
        x, y, c, me, others = _place()
        peers = [(ox, oy, c) for ox, oy in others] if self.n_peers == 3 else [(x, y, 1 - c)]
        for a in range(len(self.ins)):
            for k, to in enumerate(peers):
                peer = 2 * to[0] + to[1]
                if self.kind == "gather":
                    hr = self.ins[a].shape[0] // 2
                    rows = pl.ds(c * hr, hr)
                    src, dst, land = ins[a].at[rows, :], outs[a].at[me, rows, :], outs[a].at[peer, rows, :]
                elif self.kind == "scatter":
                    src, dst, land = ins[a].at[peer], outs[a].at[me], outs[a].at[peer]
                elif self.kind == "swap":
                    src, dst, land = ins[a].at[:, 1 - c], outs[a], outs[a]
                else:
                    src, dst, land = ins[a], outs[a].at[c], outs[a].at[1 - c]
                i = self.n_peers * a + k
                mk = lambda s, d: pltpu.make_async_remote_copy(
                    src_ref=s, dst_ref=d, send_sem=send.at[i], recv_sem=recv.at[i], device_id=to,
                    device_id_type=MESH)
                yield mk(src, dst), mk(land, land)

    def start(self, ins, outs, sems):
        for cp, _ in self._copies(ins, outs, sems):
            cp.start()

    def finish(self, ins, outs, sems):
        pairs = list(self._copies(ins, outs, sems))
        for _, landing in pairs:
            landing.wait_recv()
        for cp, _ in pairs:
            cp.wait_send()


class _Several:
    def __init__(self, parts):
        self.parts = list(parts)
        self.ins = [a for p in self.parts for a in p.ins]
        self.out_shapes = [s for p in self.parts for s in p.out_shapes]
        self.sems = [s for p in self.parts for s in p.sems]

    def split(self, ins, outs, sems=None):
        i = 0
        for k, p in enumerate(self.parts):
            n = len(p.ins)
            yield p, ins[i:i + n], outs[i:i + n], None if sems is None else sems[2 * k:2 * k + 2]
            i += n

    def start(self, ins, outs, sems):
        for p, a, b, s in self.split(ins, outs, sems):
            p.start(a, b, s)

    def finish(self, ins, outs, sems):
        for p, a, b, s in self.split(ins, outs, sems):
            p.finish(a, b, s)


def _hosted_call(body, *, grid, in_specs, out_specs, out_shape, scratch_shapes, operands, name, sem, comm=None):
    out_specs, out_shape = tuple(out_specs), tuple(out_shape)
    if isinstance(comm, (list, tuple)):
        several = _Several(comm)
        outs, comm_outs = _hosted_call(body, grid=grid, in_specs=in_specs, out_specs=out_specs, out_shape=out_shape,
                                       scratch_shapes=scratch_shapes, operands=operands, name=name, sem=sem,
                                       comm=several)
        return outs, [tuple(o) for _, _, o, _ in several.split(several.ins, comm_outs)]
    if comm is None:
        res = pl.pallas_call(body, out_shape=out_shape, grid=grid, in_specs=list(in_specs), out_specs=out_specs,
                             scratch_shapes=list(scratch_shapes), name=name, compiler_params=_cp(*sem))(*operands)
        return tuple(res), ()
    n_in, n_out, n_sc = len(in_specs), len(out_specs), len(scratch_shapes)
    ci, co = len(comm.ins), len(comm.out_shapes)

    def wrapped(*refs):
        ins, c_ins = refs[:n_in], refs[n_in:n_in + ci]
        outs = refs[n_in + ci:n_in + ci + n_out]
        c_outs = refs[n_in + ci + n_out:n_in + ci + n_out + co]
        scratch = refs[n_in + ci + n_out + co:n_in + ci + n_out + co + n_sc]
        sems = refs[n_in + ci + n_out + co + n_sc:]
        ids = [pl.program_id(d) for d in range(len(grid))]
        first = functools.reduce(jnp.logical_and, [i == 0 for i in ids])
        last = functools.reduce(jnp.logical_and, [i == g - 1 for i, g in zip(ids, grid)])

        @pl.when(first)
        def _():
            comm.start(c_ins, c_outs, sems)

        body(*ins, *outs, *scratch)

        @pl.when(last)
        def _():
            comm.finish(c_ins, c_outs, sems)

    res = pl.pallas_call(
        wrapped, out_shape=out_shape + tuple(comm.out_shapes), grid=grid, in_specs=list(in_specs) + [ANY] * ci,
        out_specs=out_specs + tuple([ANY] * co), scratch_shapes=list(scratch_shapes) + comm.sems, name=name,
        compiler_params=pltpu.CompilerParams(dimension_semantics=("arbitrary",) * len(grid),
                                             vmem_limit_bytes=VMEM_LIMIT, has_side_effects=True),
    )(*operands, *comm.ins)
    return tuple(res[:n_out]), tuple(res[n_out:])


def _run_exchange(comm, *, name):
    ci = len(comm.ins)

    def body(*refs):
        ins, outs, sems = refs[:ci], refs[ci:2 * ci], refs[2 * ci:]
        comm.start(ins, outs, sems)
        comm.finish(ins, outs, sems)

    return pl.pallas_call(
        body, out_shape=tuple(comm.out_shapes), in_specs=[ANY] * ci, out_specs=tuple([ANY] * ci),
        scratch_shapes=comm.sems, name=name, compiler_params=pltpu.CompilerParams(has_side_effects=True))(*comm.ins)


def _forward_sibling(gathered, *, name):
    n = len(gathered)

    def body(*refs):
        ins, outs = refs[:n], refs[n:2 * n]
        send, recv = refs[2 * n:]
        x, y, c, me, others = _place()
        sends = []
        for a in range(n):
            hr = gathered[a].shape[1] // 2
            for k, (ox, oy) in enumerate(others):
                landed = ins[a].at[2 * ox + oy, pl.ds(c * hr, hr), :]
                cp = pltpu.make_async_remote_copy(
                    src_ref=landed, dst_ref=outs[a].at[2 * ox + oy, pl.ds(c * hr, hr), :], send_sem=send.at[3 * a + k],
                    recv_sem=recv.at[3 * a + k], device_id=(x, y, 1 - c), device_id_type=MESH)
                cp.start()
                sends.append(cp)
        for a in range(n):
            hr = gathered[a].shape[1] // 2
            for k, (ox, oy) in enumerate(others):
                got = outs[a].at[2 * ox + oy, pl.ds((1 - c) * hr, hr), :]
                pltpu.make_async_remote_copy(src_ref=got, dst_ref=got, send_sem=send.at[3 * a + k],
                                             recv_sem=recv.at[3 * a + k], device_id=(x, y, 1 - c),
                                             device_id_type=MESH).wait_recv()
        for cp in sends:
            cp.wait_send()

    return pl.pallas_call(
        body, out_shape=tuple(jax.ShapeDtypeStruct(g.shape, g.dtype) for g in gathered),
        in_specs=[ANY] * n, out_specs=tuple([ANY] * n), input_output_aliases={a: a for a in range(n)},
        scratch_shapes=[pltpu.SemaphoreType.DMA((3 * n,)), pltpu.SemaphoreType.DMA((3 * n,))],
        name=name, compiler_params=pltpu.CompilerParams(has_side_effects=True))(*gathered)


def _own_slot(buf, piece, idx):
    return lax.dynamic_update_slice(buf, piece[None], (idx,) + (0,) * piece.ndim)


def _all_reduce_small(v, *, name):
    R = v.shape[0]

    flips = [(dx, dy, dc) for dx in range(2) for dy in range(2) for dc in range(2) if dx or dy or dc]

    def body(v_ref, o_ref, slots, send, recv):
        x, y, c, me, others = _place()
        mine = 2 * me + c
        slots[mine] = v_ref[...]

        def copy(k, slot):
            dx, dy, dc = flips[k]
            peer = (x + dx - 2 * x * dx, y + dy - 2 * y * dy, c + dc - 2 * c * dc)
            peer_slot = 4 * peer[0] + 2 * peer[1] + peer[2]
            return pltpu.make_async_remote_copy(
                src_ref=v_ref, dst_ref=slots.at[mine if slot == "mine" else peer_slot], send_sem=send.at[k],
                recv_sem=recv.at[k], device_id=peer, device_id_type=MESH)

        for k in range(7):
            copy(k, "mine").start()
        for k in range(7):
            copy(k, "peer").wait_recv()
        for k in range(7):
            copy(k, "mine").wait_send()
        acc = slots[0]
        for j in range(1, 8):
            acc = acc + slots[j]
        o_ref[...] = acc

    vm = pl.BlockSpec(memory_space=pltpu.VMEM)
    return pl.pallas_call(
        body, out_shape=jax.ShapeDtypeStruct(v.shape, F32), in_specs=[vm], out_specs=vm,
        scratch_shapes=[pltpu.VMEM((8, R, LANES), F32), pltpu.SemaphoreType.DMA((7,)),
                        pltpu.SemaphoreType.DMA((7,))],
        name=name, compiler_params=pltpu.CompilerParams(has_side_effects=True))(v)


def _rows(v, n_rows):
    v = v.reshape(-1).astype(F32)
    return jnp.pad(v, (0, n_rows * LANES - v.shape[0])).reshape(n_rows, LANES)


def _mlp_fwd(x_in, g, w_up, w_down, tag):
    h = _rms_fwd(x_in, g, name=f"{tag}_norm")
    u, a = _matmul(h, w_up, b_split=True, epilogue="sqrelu", name=f"{tag}_up")
    x_out = _matmul(a, w_down, epilogue="res", res=x_in, name=f"{tag}_down")
    return x_out, (h, u, a)


def _mlp_bwd(dy, dy_b, x_in, g, w_up, w_down, saved, tag, comms=None):
    h, u, a = saved
    comms = comms or {}
    landed = {}

    def mm(key, *args, **kw):
        comm = comms.get(key)
        if callable(comm):
            comm = comm(landed)
        out = _matmul(*args, name=f"{tag}_{key}", comm=comm, **kw)
        if comm is not None:
            out, landed[key] = out
        return out

    dw_down = mm("dwdown", a, dy_b, form="tn")
    du = mm("du", dy_b, w_down, form="nt", epilogue="sqrelu_bwd", u=u, out_dtype=BF16)
    dw_up = mm("dwup", h, du, form="tn", out_split=True)
    dh = mm("dh", du, w_up, form="nt", b_split=True)
    dx, dx_b, dg = _rms_bwd(x_in, g, dh, res=dy, bf16_copy=True, name=f"{tag}_dnorm")
    return dx, dx_b, dg, dw_up, dw_down, landed


def kernel(x, positions, ln_mix_g, ln_mlp_g, sf_w_in, sf_b_f, fox_q_g, fox_k_g, sf_w_o, mla_w_down, mla_q_a_g, mla_kv_a_g, mla_w_uq, mla_w_ukv, mla_q_g, mla_k_g, mla_w_o, mlp_w_up, mlp_w_down, loss_target, m_ln_mix_g, m_ln_mlp_g, m_sf_w_in, m_sf_b_f, m_fox_q_g, m_fox_k_g, m_sf_w_o, m_mla_w_down, m_mla_q_a_g, m_mla_kv_a_g, m_mla_w_uq, m_mla_w_ukv, m_mla_q_g, m_mla_k_g, m_mla_w_o, m_mlp_w_up, m_mlp_w_down, v_ln_mix_g, v_ln_mlp_g, v_sf_w_in, v_sf_b_f, v_fox_q_g, v_fox_k_g, v_sf_w_o, v_mla_w_down, v_mla_q_a_g, v_mla_kv_a_g, v_mla_w_uq, v_mla_w_ukv, v_mla_q_g, v_mla_k_g, v_mla_w_o, v_mlp_w_up, v_mlp_w_down):
    S, D = x.shape[1], x.shape[2]
    xs, tgt, pos = x[0], loss_target[0], positions[0]
    xi, yi, ci = lax.axis_index("x"), lax.axis_index("y"), lax.axis_index("c")
    chip = 2 * xi + yi
    core = ci.astype(jnp.int32).reshape(1)
    d_ff = mlp_w_up.shape[2] * N_CHIPS
    in_w = sf_w_in.shape[2] * N_CHIPS
    qkv_w = 3 * N_SB * HEAD + 3 * N_FOX * HEAD
    dn_w = mla_w_down.shape[2]
    dn_pad = Q_RANK + KV_RANK + LANES

    def gather_begin(ws):
        shards = [w.astype(BF16) for w in ws]
        return shards, _Exchange("gather", shards)

    def gather_end(gathered, shards, name):
        both = _forward_sibling(list(gathered), name=name)
        return [_own_slot(ag, s, chip) for ag, s in zip(both, shards)]

    cols = lambda ag: ag.transpose(1, 0, 2).reshape(ag.shape[1], -1)
    rows = lambda ag: ag.reshape(-1, ag.shape[2])
    s_mix0, ex_mix0 = gather_begin([sf_w_in[0], sf_w_o[0]])
    ag_in, ag_o0 = gather_end(_run_exchange(ex_mix0, name="gather_mix0"), s_mix0, "gather_mix0_sibling")
    w_in_full = cols(ag_in)
    w_qkv = w_in_full[:, :qkv_w]
    w_f = jnp.pad(w_in_full[:, qkv_w:], ((0, 0), (0, LANES - (in_w - qkv_w))))
    w_o0 = rows(ag_o0)
    s_mlp0, ex_mlp0 = gather_begin([mlp_w_up[0], mlp_w_down[0]])
    s_mix1, ex_mix1 = gather_begin([mla_w_down[0], mla_w_uq[0], mla_w_ukv[0], mla_w_o[0]])
    s_mlp1, ex_mlp1 = gather_begin([mlp_w_up[1], mlp_w_down[1]])

    gain_blk = jnp.concatenate([mla_q_a_g, mla_kv_a_g], axis=0) * (ci == 0).astype(F32)
    placed = jnp.zeros((2, N_CHIPS, LANES), F32)
    placed = lax.dynamic_update_slice(placed, gain_blk[:, None, :], (0, chip, 0))
    gains = _all_reduce_small(placed.reshape(2 * N_CHIPS, LANES), name="gather_gains")
    q_a_full = gains[:N_CHIPS].reshape(Q_RANK)
    kv_a_full = gains[N_CHIPS:].reshape(KV_RANK)

    pad_gain = lambda g: jnp.pad(g.reshape(1, QK_DIM), ((0, 0), (0, QK_PAD - QK_DIM)))
    q_g_pad, k_g_pad = pad_gain(mla_q_g), pad_gain(mla_k_g)
    b_pad = _rows(sf_b_f, 1)

    h0 = _rms_fwd(xs, ln_mix_g[0], name="mix0_norm")
    qkv_sb = _matmul(h0, w_qkv, n=3 * N_SB * HEAD, b_n0=0, out_dtype=BF16, name="mix0_qkv_sb")
    qk_fx = _matmul(h0, w_qkv, n=2 * N_FOX * HEAD, b_n0=3 * N_SB * HEAD, name="mix0_qk_fox")
    v_fx = _matmul(h0, w_qkv, n=N_FOX * HEAD, b_n0=(3 * N_SB + 2 * N_FOX) * HEAD, out_dtype=BF16,
                   name="mix0_v_fox")
    fl = _matmul(h0, w_f, name="mix0_forget_logit")
    f_cum = _forget_fwd(fl, b_pad, name="forget_fwd")
    neg_f = (-f_cum[:, :N_FOX]).T.reshape(N_FOX, 1, S)
    q_f = _rms_fwd(qk_fx, fox_q_g[0], c0=0, width=N_FOX * HEAD, gw=HEAD, name="fox_q_norm")
    k_f = _rms_fwd(qk_fx, fox_k_g[0], c0=N_FOX * HEAD, width=N_FOX * HEAD, gw=HEAD, name="fox_k_norm")
    (o_sb, t_sb), landed = _sb_fwd(qkv_sb, name="sb_fwd", comm=ex_mlp0)
    ag_up0, ag_dw0 = gather_end(landed, s_mlp0, "gather_mlp0_sibling")
    (o_fx, lse0), landed = _attn_fwd(q_f, k_f, v_fx, neg_f, n_heads=N_FOX, dqk=HEAD, scale=HEAD ** -0.5,
                                     exact_p=True, name="fox_fwd", comm=ex_mix1)
    ag_dn, ag_uq, ag_ukv, ag_o1 = gather_end(landed, s_mix1, "gather_mix1_sibling")
    w_dn = jnp.pad(rows(ag_dn), ((0, 0), (0, dn_pad - dn_w)))
    w_uq = jnp.pad(cols(ag_uq).reshape(Q_RANK, N_MLA, QK_DIM), ((0, 0), (0, 0), (0, QK_PAD - QK_DIM)))
    w_uq = w_uq.reshape(Q_RANK, N_MLA * QK_PAD)
    w_ukv = cols(ag_ukv)
    w_o1 = rows(ag_o1)
    o0 = jnp.concatenate([o_sb, o_fx], axis=1)
    x1 = _matmul(o0, w_o0, epilogue="res", res=xs, name="mix0_out")
    x2, mlp0 = _mlp_fwd(x1, ln_mlp_g[0], ag_up0, rows(ag_dw0), "mlp0")

    h2 = _rms_fwd(x2, ln_mix_g[1], name="mix1_norm")
    down = _matmul(h2, w_dn, name="mix1_down")
    c_q = _rms_fwd(down, q_a_full, c0=0, width=Q_RANK, name="mix1_q_a_norm")
    c_kv = _rms_fwd(down, kv_a_full, c0=Q_RANK, width=KV_RANK, name="mix1_kv_a_norm")
    q_raw = _matmul(c_q, w_uq, name="mix1_uq")
    kv_raw = _matmul(c_kv, w_ukv, name="mix1_ukv")
    half = ROPE // 2
    inv_freq = ROPE_THETA ** (-jnp.arange(half, dtype=F32) / half)
    ang = pos.astype(F32)[:, None] * inv_freq
    table = lambda t: jnp.pad(jnp.concatenate([t, t], axis=1), ((0, 0), (0, LANES - ROPE)))
    cos_t, sin_t = table(jnp.cos(ang)), table(jnp.sin(ang))
    q_pad, k_pad, v1 = _mla_prep_fwd(q_raw, kv_raw, down, cos_t, sin_t, q_g_pad, k_g_pad, name="mla_prep_fwd")
    (o1, lse1), landed = _attn_fwd(q_pad, k_pad, v1, None, n_heads=N_MLA, dqk=QK_PAD, scale=QK_DIM ** -0.5,
                                   name="mla_fwd", comm=ex_mlp1)
    ag_up1, ag_dw1 = gather_end(landed, s_mlp1, "gather_mlp1_sibling")
    w_up = [ag_up0, ag_up1]
    w_dw = [rows(ag_dw0), rows(ag_dw1)]
    x3 = _matmul(o1, w_o1, epilogue="res", res=x2, name="mix1_out")
    x4, mlp1 = _mlp_fwd(x3, ln_mlp_g[1], w_up[1], w_dw[1], "mlp1")

    dx4, dx4_b, loss_local = _loss_head(x4, tgt, name="loss_head")
    loss = lax.psum(loss_local, ("x", "y", "c"))

    by_cols = lambda g: g.reshape(g.shape[0], N_CHIPS, -1).transpose(1, 0, 2)
    by_rows = lambda g: g.reshape(N_CHIPS, g.shape[0] // N_CHIPS, g.shape[1])
    halves = lambda g: g.reshape(N_CHIPS, 2, g.shape[1] // 2, g.shape[2])

    def scatter_of(grads, from_sibling, tags):
        parts = [_add_sibling(g, r, core, name=f"add_sibling_{t}") for g, r, t in zip(grads, from_sibling, tags)]
        return parts, _Exchange("scatter", parts)

    def reduce_end(slots, parts, tags, name):
        slots = [_own_slot(s, lax.dynamic_index_in_dim(p, chip, 0, keepdims=False), chip)
                 for s, p in zip(slots, parts)]
        mine = [_add_chips(s, name=f"add_chips_{t}") for s, t in zip(slots, tags)]
        joined = _run_exchange(_Exchange("join", mine), name=f"{name}_join")
        return [_own_slot(j, m, ci).reshape(2 * m.shape[0], m.shape[1]) for j, m in zip(joined, mine)]

    dx3, dx3_b, dg_mlp1, dw_up1, dw_dw1, _ = _mlp_bwd(dx4, dx4_b, x3, ln_mlp_g[1], w_up[1], w_dw[1], mlp1, "mlp1")
    tags_mlp1 = ["w_up1", "w_dw1"]
    g_mlp1 = [halves(dw_up1), halves(by_rows(dw_dw1))]

    dw_o1 = _matmul(o1, dx3_b, form="tn", name="mix1_dwo")
    do1 = _matmul(dx3_b, w_o1, form="nt", name="mix1_do")
    (dq_pad, dk_pad, dv1), from_sibling = _attn_bwd(
        q_pad, k_pad, v1, None, o1, lse1, do1, n_heads=N_MLA, dqk=QK_PAD, scale=QK_DIM ** -0.5, name="mla_bwd",
        comm=_Exchange("swap", g_mlp1))
    p_mlp1, _ = scatter_of(g_mlp1, from_sibling, tags_mlp1)
    dq_raw, dkv_raw, dpe, dg_q, dg_k = _mla_prep_bwd(dq_pad, dk_pad, dv1, q_raw, kv_raw, down, cos_t, sin_t,
                                                     q_g_pad, k_g_pad, name="mla_prep_bwd")
    dw_uq = _matmul(c_q, dq_raw, form="tn", name="mix1_dwuq")
    dc_q = _matmul(dq_raw, w_uq, form="nt", name="mix1_dcq")
    dw_ukv = _matmul(c_kv, dkv_raw, form="tn", name="mix1_dwukv")
    dc_kv = _matmul(dkv_raw, w_ukv, form="nt", name="mix1_dckv")
    d_cq, dg_qa = _rms_bwd(down, q_a_full, dc_q, c0=0, width=Q_RANK, name="mix1_q_a_dnorm")
    d_ckv, dg_kva = _rms_bwd(down, kv_a_full, dc_kv, c0=Q_RANK, width=KV_RANK, name="mix1_kv_a_dnorm")
    d_down = jnp.concatenate([d_cq, d_ckv, dpe], axis=1)
    dw_dn = _matmul(h2, d_down, form="tn", name="mix1_dwdown")
    dh2 = _matmul(d_down, w_dn, form="nt", name="mix1_dh")
    dx2, dx2_b, dg_mix1 = _rms_bwd(x2, ln_mix_g[1], dh2, res=dx3, bf16_copy=True, name="mix1_dnorm")
    g_uq = dw_uq.reshape(Q_RANK, N_MLA, QK_PAD)[:, :, :QK_DIM].reshape(Q_RANK, N_MLA * QK_DIM)
    tags_mix1 = ["w_dn", "w_uq", "w_ukv", "w_o1"]
    g_mix1 = [halves(by_rows(dw_dn[:, :dn_w])), halves(by_cols(g_uq)), halves(by_cols(dw_ukv)),
              halves(by_rows(dw_o1))]

    p_mix1 = []

    def scatter_mix1(landed):
        parts, ex = scatter_of(g_mix1, landed["dwdown"][1], tags_mix1)
        p_mix1.extend(parts)
        return ex

    dx1, dx1_b, dg_mlp0, dw_up0, dw_dw0, landed = _mlp_bwd(
        dx2, dx2_b, x1, ln_mlp_g[0], w_up[0], w_dw[0], mlp0, "mlp0",
        comms={"dwdown": [_Exchange("scatter", p_mlp1[:1]), _Exchange("swap", g_mix1)],
               "du": _Exchange("scatter", p_mlp1[1:]), "dwup": scatter_mix1})
    gs_up1, gs_dw1 = reduce_end(landed["dwdown"][0] + landed["du"], p_mlp1, tags_mlp1, "reduce_mlp1")
    gs_dn, gs_uq, gs_ukv, gs_o1 = reduce_end(landed["dwup"], p_mix1, tags_mix1, "reduce_mix1")
    tags_mlp0 = ["w_up0", "w_dw0"]
    g_mlp0 = [halves(dw_up0), halves(by_rows(dw_dw0))]

    dw_o0 = _matmul(o0, dx1_b, form="tn", name="mix0_dwo")
    do0 = _matmul(dx1_b, w_o0, form="nt", name="mix0_do")
    (dq_f, dk_f, dv_fx, dbias, drow), from_sibling = _attn_bwd(
        q_f, k_f, v_fx, neg_f, o_fx, lse0, do0, n_heads=N_FOX, dqk=HEAD, scale=HEAD ** -0.5, do_off=N_SB,
        name="fox_bwd", comm=_Exchange("swap", g_mlp0))
    p_mlp0, ex = scatter_of(g_mlp0, from_sibling, tags_mlp0)
    (dq_sb, dk_sb, dv_sb), slots = _sb_bwd(qkv_sb, do0, t_sb, do_off=0, name="sb_bwd", comm=ex)
    gs_up0, gs_dw0 = reduce_end(slots, p_mlp0, tags_mlp0, "reduce_mlp0")
    dq_fx, dg_fq = _rms_bwd(qk_fx, fox_q_g[0], dq_f, c0=0, width=N_FOX * HEAD, gw=HEAD, name="fox_q_dnorm")
    dk_fx, dg_fk = _rms_bwd(qk_fx, fox_k_g[0], dk_f, c0=N_FOX * HEAD, width=N_FOX * HEAD, gw=HEAD,
                            name="fox_k_dnorm")
    d_fcum = jnp.pad((jnp.max(drow, axis=-1) - dbias.reshape(N_FOX, S)).T, ((0, 0), (0, LANES - N_FOX)))
    dfl, db_f = _forget_bwd(fl, b_pad, d_fcum, name="forget_bwd")
    dproj = jnp.concatenate([dq_sb, dk_sb, dv_sb, dq_fx, dk_fx, dv_fx], axis=1).astype(BF16)
    dw_qkv = _matmul(h0, dproj, form="tn", name="mix0_dwqkv")
    dw_f = _matmul(h0, dfl, form="tn", name="mix0_dwf")
    g_in = jnp.concatenate([dw_qkv, dw_f[:, :in_w - qkv_w]], axis=1)
    tags_mix0 = ["w_in", "w_o0"]
    g_mix0 = [halves(by_cols(g_in)), halves(by_rows(dw_o0))]
    dh0 = _matmul(dfl, w_f, form="nt", name="mix0_dh_f")
    dh0, from_sibling = _matmul(dproj, w_qkv, form="nt", epilogue="res", res=dh0, name="mix0_dh",
                                comm=_Exchange("swap", g_mix0))
    p_mix0, ex_mix0_grads = scatter_of(g_mix0, from_sibling, tags_mix0)
    grad_x, dg_mix0 = _rms_bwd(xs, ln_mix_g[0], dh0, res=dx1, name="mix0_dnorm")
    gs_up = jnp.concatenate([gs_up0, gs_up1], axis=0)
    gs_dw = jnp.concatenate([gs_dw0, gs_dw1], axis=0)

    ln_rows = D // LANES
    small = jnp.concatenate([
        _rows(dg_mix0, ln_rows), _rows(dg_mix1, ln_rows), _rows(dg_mlp0, ln_rows), _rows(dg_mlp1, ln_rows),
        _rows(db_f, 8), _rows(dg_fq, 8), _rows(dg_fk, 8), _rows(dg_qa, 8), _rows(dg_kva, 8), _rows(dg_q, 8),
        _rows(dg_k, 8)], axis=0)
    small = _all_reduce_small(small, name="reduce_small")
    flat = lambda r0, nr, n: small[r0:r0 + nr].reshape(-1)[:n]
    r0 = 4 * ln_rows
    g_ln_mix = jnp.stack([flat(0, ln_rows, D), flat(ln_rows, ln_rows, D)])
    g_ln_mlp = jnp.stack([flat(2 * ln_rows, ln_rows, D), flat(3 * ln_rows, ln_rows, D)])
    g_b_f = flat(r0, 8, N_FOX)[None]
    g_fq, g_fk = flat(r0 + 8, 8, HEAD)[None], flat(r0 + 16, 8, HEAD)[None]
    g_qa = lax.dynamic_slice(flat(r0 + 24, 8, Q_RANK), (chip * LANES,), (LANES,))[None]
    g_kva = lax.dynamic_slice(flat(r0 + 32, 8, KV_RANK), (chip * LANES,), (LANES,))[None]
    g_q, g_k = flat(r0 + 40, 8, QK_DIM)[None], flat(r0 + 48, 8, QK_DIM)[None]

    def pack_small(ln_mix, ln_mlp, *rest):
        return jnp.concatenate([_rows(ln_mix, 2 * ln_rows), _rows(ln_mlp, 2 * ln_rows)] + [_rows(t, 8) for t in rest],
                               axis=0)

    def unpack_small(p):
        f = lambda r, nr, shape: p[r:r + nr].reshape(-1)[:int(np.prod(shape))].reshape(shape)
        shapes = [(1, N_FOX), (1, HEAD), (1, HEAD), (1, LANES), (1, LANES), (1, QK_DIM), (1, QK_DIM)]
        return (f(0, 2 * ln_rows, (2, D)), f(2 * ln_rows, 2 * ln_rows, (2, D)),
                *[f(r0 + 8 * i, 8, shp) for i, shp in enumerate(shapes)])

    small_out = _adamw(
        pack_small(ln_mix_g, ln_mlp_g, sf_b_f, fox_q_g, fox_k_g, mla_q_a_g, mla_kv_a_g, mla_q_g, mla_k_g)[None],
        pack_small(g_ln_mix, g_ln_mlp, g_b_f, g_fq, g_fk, g_qa, g_kva, g_q, g_k)[None],
        pack_small(m_ln_mix_g, m_ln_mlp_g, m_sf_b_f, m_fox_q_g, m_fox_k_g, m_mla_q_a_g, m_mla_kv_a_g, m_mla_q_g,
                   m_mla_k_g)[None],
        pack_small(v_ln_mix_g, v_ln_mlp_g, v_sf_b_f, v_fox_q_g, v_fox_k_g, v_mla_q_a_g, v_mla_kv_a_g, v_mla_q_g,
                   v_mla_k_g)[None], name="adamw_small")
    d_small, m_small, v_small = [unpack_small(p[0]) for p in small_out]

    def big(w, g, m, v, tag, comm=None):
        g = g.reshape(w.shape)
        out = _adamw(w, g, m, v, name=f"adamw_{tag}", comm=comm)
        (d, mn, vn), landed = out if comm is not None else (out, None)
        return (g, d, mn, vn) if comm is None else ((g, d, mn, vn), landed)

    r_up, slots = big(mlp_w_up, gs_up, m_mlp_w_up, v_mlp_w_up, "w_up", comm=ex_mix0_grads)
    gs_in, gs_o0 = reduce_end(slots, p_mix0, tags_mix0, "reduce_mix0")
    r_dw = big(mlp_w_down, gs_dw, m_mlp_w_down, v_mlp_w_down, "w_dw")
    r_in = big(sf_w_in, gs_in, m_sf_w_in, v_sf_w_in, "w_in")
    r_o0 = big(sf_w_o, gs_o0, m_sf_w_o, v_sf_w_o, "w_o0")
    r_dn = big(mla_w_down, gs_dn, m_mla_w_down, v_mla_w_down, "w_dn")
    r_uq = big(mla_w_uq, gs_uq, m_mla_w_uq, v_mla_w_uq, "w_uq")
    r_ukv = big(mla_w_ukv, gs_ukv, m_mla_w_ukv, v_mla_w_ukv, "w_ukv")
    r_o1 = big(mla_w_o, gs_o1, m_mla_w_o, v_mla_w_o, "w_o1")

    g_small = (g_ln_mix, g_ln_mlp, g_b_f, g_fq, g_fk, g_qa, g_kva, g_q, g_k)

    def ordered(k, sm):
        return (sm[0], sm[1], r_in[k], sm[2], sm[3], sm[4], r_o0[k], r_dn[k], sm[5], sm[6], r_uq[k], r_ukv[k],
                sm[7], sm[8], r_o1[k], r_up[k], r_dw[k])

    return (loss, grad_x[None], *ordered(0, g_small), *ordered(1, d_small), *ordered(2, m_small),
            *ordered(3, v_small))
```

```python
import functools

import numpy as np
import jax
import jax.numpy as jnp
from jax import lax
from jax.experimental import pallas as pl
from jax.experimental.pallas import tpu as pltpu

F32 = jnp.float32
BF16 = jnp.bfloat16
MESH = pl.DeviceIdType.MESH

EPS = 1e-6
HEAD = 128
N_SB = 8
N_FOX = 8
N_MLA = 16
Q_RANK = 512
KV_RANK = 512
NOPE = 128
ROPE = 64
QK_DIM = NOPE + ROPE
QK_PAD = 256
ROPE_THETA = 10000.0
N_CHIPS = 4

ADAM_LR = 0.001
ADAM_B1 = 0.9
ADAM_B2 = 0.999
ADAM_EPS = 1e-08
ADAM_WD = 0.01
ADAM_STEP = 10

VMEM_LIMIT = 56 * 1024 * 1024
LANES = 128
NEG = -1e30


def _cp(*sem):
    return pltpu.CompilerParams(dimension_semantics=sem, vmem_limit_bytes=VMEM_LIMIT)


def _pick(dim, target):
    if dim <= target:
        return dim
    t = (target // LANES) * LANES
    while t >= LANES:
        if dim % t == 0:
            return t
        t -= LANES
    raise ValueError(f"no tile for {dim}")


NT_DIMS = (((1,), (1,)), ((), ()))
TN_DIMS = (((0,), (0,)), ((), ()))


def _dot(a, b):
    return jnp.dot(a, b, preferred_element_type=F32)


def _dot_nt(a, b):
    return lax.dot_general(a, b, NT_DIMS, preferred_element_type=F32)


def _dot_tn(a, b):
    return lax.dot_general(a, b, TN_DIMS, preferred_element_type=F32)


def _matmul(a, b, *, name, form="nn", out_dtype=F32, n=None, b_n0=0, b_split=False,
            out_split=False, epilogue="plain", res=None, u=None, tm=1024, tn=1024, tk=2048, comm=None):
    if form == "tn":
        K, M = a.shape
    else:
        M, K = a.shape
    if b_split:
        if form == "nt":
            nb_full, kb_full = b.shape[1], b.shape[2] * N_CHIPS
        else:
            kb_full, nb_full = b.shape[1], b.shape[2] * N_CHIPS
    elif form == "nt":
        nb_full, kb_full = b.shape
    else:
        kb_full, nb_full = b.shape
    assert kb_full == K, (name, a.shape, b.shape)
    N = nb_full if n is None else n
    if a.dtype != BF16 or b.dtype != BF16:
        tk = max(tk // 2, LANES)
    tm, tn, tk = _pick(M, tm), _pick(N, tn), _pick(K, tk)
    if b_split:
        per_chip = (b.shape[2])
        if form == "nt":
            tk = _pick(per_chip, tk)
        else:
            tn = _pick(per_chip, tn)
    if out_split:
        tn = _pick(N // N_CHIPS, tn)
    assert b_n0 % tn == 0
    nb0 = b_n0 // tn
    nk = K // tk
    grid = (M // tm, N // tn, nk)

    if form == "tn":
        a_spec = pl.BlockSpec((tk, tm), lambda i, j, k: (k, i))
    else:
        a_spec = pl.BlockSpec((tm, tk), lambda i, j, k: (i, k))
    if b_split:
        if form == "nt":
            kc = b.shape[2] // tk
            b_spec = pl.BlockSpec((None, tn, tk), lambda i, j, k: (k // kc, j, k % kc))
        else:
            nc = b.shape[2] // tn
            b_spec = pl.BlockSpec((None, tk, tn), lambda i, j, k: (j // nc, k, j % nc))
    elif form == "nt":
        b_spec = pl.BlockSpec((tn, tk), lambda i, j, k: (j + nb0, k))
    else:
        b_spec = pl.BlockSpec((tk, tn), lambda i, j, k: (k, j + nb0))
    mn_spec = pl.BlockSpec((tm, tn), lambda i, j, k: (i, j))
    if out_split:
        oc = (N // N_CHIPS) // tn
        out_spec = pl.BlockSpec((None, tm, tn), lambda i, j, k: (j // oc, i, j % oc))
        out_shape = jax.ShapeDtypeStruct((N_CHIPS, M, N // N_CHIPS), out_dtype)
    else:
        out_spec = mn_spec
        out_shape = jax.ShapeDtypeStruct((M, N), out_dtype)

    in_specs = [a_spec, b_spec]
    operands = [a, b]
    out_specs = (out_spec,)
    out_shape = (out_shape,)
    if epilogue == "res":
        in_specs.append(mn_spec)
        operands.append(res)
    elif epilogue == "sqrelu_bwd":
        in_specs.append(mn_spec)
        operands.append(u)
    elif epilogue == "sqrelu":
        out_specs = (mn_spec, mn_spec)
        out_shape = (jax.ShapeDtypeStruct((M, N), F32), jax.ShapeDtypeStruct((M, N), BF16))

    def finish(refs, r):
        if epilogue == "plain":
            refs[2][...] = r.astype(out_dtype)
        elif epilogue == "res":
            refs[3][...] = (refs[2][...] + r).astype(out_dtype)
        elif epilogue == "sqrelu":
            refs[2][...] = r
            p = jnp.maximum(r, 0.0)
            refs[3][...] = (p * p).astype(BF16)
        else:
            refs[3][...] = (r * (2.0 * jnp.maximum(refs[2][...], 0.0))).astype(out_dtype)

    def body(*refs):
        at = refs[0][...].astype(BF16)
        bt = refs[1][...].astype(BF16)
        if form == "nn":
            part = _dot(at, bt)
        elif form == "nt":
            part = _dot_nt(at, bt)
        else:
            part = _dot_tn(at, bt)
        if nk == 1:
            finish(refs, part)
            return
        acc = refs[-1]
        k = pl.program_id(2)

        @pl.when(k == 0)
        def _():
            acc[...] = part

        @pl.when(jnp.logical_and(k > 0, k < nk - 1))
        def _():
            acc[...] += part

        @pl.when(k == nk - 1)
        def _():
            finish(refs, acc[...] + part)

    outs, comm_outs = _hosted_call(
        body, grid=grid, in_specs=in_specs, out_specs=out_specs, out_shape=out_shape,
        scratch_shapes=[] if nk == 1 else [pltpu.VMEM((tm, tn), F32)], operands=operands, name=name,
        sem=("parallel", "parallel", "arbitrary"), comm=comm)
    result = outs if epilogue == "sqrelu" else outs[0]
    return result if comm is None else (result, comm_outs)


def _rms_fwd(x, g, *, name, c0=0, width=None, gw=None, tr=256):
    R, ctot = x.shape
    C = ctot if width is None else width
    gw = C if gw is None else gw
    assert c0 % C == 0 and C % gw == 0
    tr = _pick(R, tr)
    cb = c0 // C
    ng = C // gw

    def body(x_ref, g_ref, o_ref):
        gv = g_ref[...]
        for gi in range(ng):
            cols = slice(gi * gw, (gi + 1) * gw)
            xs = x_ref[:, cols]
            ms = jnp.sum(xs * xs, axis=-1, keepdims=True) * (1.0 / gw)
            o_ref[:, cols] = ((xs * lax.rsqrt(ms + EPS)) * gv).astype(o_ref.dtype)

    return pl.pallas_call(
        body, out_shape=jax.ShapeDtypeStruct((R, C), BF16), grid=(R // tr,),
        in_specs=[pl.BlockSpec((tr, C), lambda i: (i, cb)), pl.BlockSpec((1, gw), lambda i: (0, 0))],
        out_specs=pl.BlockSpec((tr, C), lambda i: (i, 0)), name=name,
        compiler_params=_cp("parallel"))(x, g.reshape(1, gw).astype(F32))


def _rms_bwd(x, g, dy, *, name, res=None, c0=0, width=None, gw=None, tr=256, bf16_copy=False):
    bf16_copy = int(bf16_copy)
    R, ctot = x.shape
    C = ctot if width is None else width
    gw = C if gw is None else gw
    tr = _pick(R, tr)
    cb = c0 // C
    ng = C // gw
    nsteps = R // tr
    row_spec = pl.BlockSpec((tr, C), lambda i: (i, 0))
    in_specs = [pl.BlockSpec((tr, C), lambda i: (i, cb)), pl.BlockSpec((1, gw), lambda i: (0, 0)), row_spec]
    operands = [x, g.reshape(1, gw).astype(F32), dy]
    if res is not None:
        in_specs.append(row_spec)
        operands.append(res)

    def body(*refs):
        x_ref, g_ref, dy_ref = refs[:3]
        res_ref = refs[3] if res is not None else None
        dx_ref, dg_ref = refs[-3 - bf16_copy], refs[-2]
        acc = refs[-1]
        i = pl.program_id(0)

        @pl.when(i == 0)
        def _():
            acc[...] = jnp.zeros_like(acc)

        gv = g_ref[...]
        for gi in range(ng):
            cols = slice(gi * gw, (gi + 1) * gw)
            xs = x_ref[:, cols]
            dys = dy_ref[:, cols].astype(F32)
            rstd = lax.rsqrt(jnp.sum(xs * xs, axis=-1, keepdims=True) * (1.0 / gw) + EPS)
            xh = xs * rstd
            gdy = dys * gv
            m = jnp.sum(gdy * xh, axis=-1, keepdims=True) * (1.0 / gw)
            dx = rstd * (gdy - xh * m)
            if res_ref is not None:
                dx = dx + res_ref[:, cols]
            dx_ref[:, cols] = dx
            if bf16_copy:
                refs[-3][:, cols] = dx.astype(BF16)
            acc[...] += jnp.sum((dys * xh).reshape(tr // 8, 8, gw), axis=0)

        @pl.when(i == nsteps - 1)
        def _():
            dg_ref[...] = jnp.sum(acc[...], axis=0, keepdims=True)

    out_shape = [jax.ShapeDtypeStruct((R, C), F32)] + [jax.ShapeDtypeStruct((R, C), BF16)] * bf16_copy
    outs = pl.pallas_call(
        body, out_shape=tuple(out_shape + [jax.ShapeDtypeStruct((1, gw), F32)]),
        grid=(nsteps,), in_specs=in_specs,
        out_specs=tuple([row_spec] * len(out_shape) + [pl.BlockSpec((1, gw), lambda i: (0, 0))]),
        scratch_shapes=[pltpu.VMEM((8, gw), F32)], name=name,
        compiler_params=_cp("arbitrary"))(*operands)
    return (*outs[:-1], outs[-1][0])


def _split3(x):
    hi = x.astype(BF16)
    r1 = x - hi.astype(F32)
    mid = r1.astype(BF16)
    lo = (r1 - mid.astype(F32)).astype(BF16)
    return hi, mid, lo


def _log_sigmoid(z):
    return jnp.minimum(z, 0.0) - jnp.log(1.0 + jnp.exp(-jnp.abs(z)))


def _forget_fwd(fl, b, *, name, tb=512):
    S = fl.shape[0]
    tb = _pick(S, tb)

    def body(fl_ref, b_ref, f_ref, carry):
        i = pl.program_id(0)

        @pl.when(i == 0)
        def _():
            carry[...] = jnp.zeros_like(carry)

        lf = _log_sigmoid(fl_ref[...] + b_ref[...])
        r = lax.broadcasted_iota(jnp.int32, (tb, tb), 0)
        c = lax.broadcasted_iota(jnp.int32, (tb, tb), 1)
        tri = (c <= r).astype(BF16)
        hi, mid, lo = _split3(lf)
        cs = _dot(tri, hi) + _dot(tri, mid) + _dot(tri, lo)
        f_ref[...] = cs + carry[...]
        carry[...] += jnp.sum(lf, axis=0, keepdims=True)

    return pl.pallas_call(
        body, out_shape=jax.ShapeDtypeStruct((S, LANES), F32), grid=(S // tb,),
        in_specs=[pl.BlockSpec((tb, LANES), lambda i: (i, 0)), pl.BlockSpec((1, LANES), lambda i: (0, 0))],
        out_specs=pl.BlockSpec((tb, LANES), lambda i: (i, 0)),
        scratch_shapes=[pltpu.VMEM((1, LANES), F32)], name=name,
        compiler_params=_cp("arbitrary"))(fl, b)


def _forget_bwd(fl, b, dF, *, name, tb=512):
    S = fl.shape[0]
    tb = _pick(S, tb)
    nb = S // tb

    def body(fl_ref, b_ref, df_ref, dfl_ref, db_ref, carry, acc):
        i = pl.program_id(0)

        @pl.when(i == 0)
        def _():
            carry[...] = jnp.zeros_like(carry)
            acc[...] = jnp.zeros_like(acc)

        d = df_ref[...]
        r = lax.broadcasted_iota(jnp.int32, (tb, tb), 0)
        c = lax.broadcasted_iota(jnp.int32, (tb, tb), 1)
        tri = (c >= r).astype(BF16)
        hi, mid, lo = _split3(d)
        rc = _dot(tri, hi) + _dot(tri, mid) + _dot(tri, lo) + carry[...]
        z = fl_ref[...] + b_ref[...]
        dfl = rc * jnp.exp(_log_sigmoid(-z))
        dfl_ref[...] = dfl
        carry[...] += jnp.sum(d, axis=0, keepdims=True)
        acc[...] += jnp.sum(dfl, axis=0, keepdims=True)

        @pl.when(i == nb - 1)
        def _():
            db_ref[...] = acc[...]

    rev = lambda i: (nb - 1 - i, 0)
    dfl, db = pl.pallas_call(
        body, out_shape=(jax.ShapeDtypeStruct((S, LANES), F32), jax.ShapeDtypeStruct((1, LANES), F32)),
        grid=(nb,),
        in_specs=[pl.BlockSpec((tb, LANES), rev), pl.BlockSpec((1, LANES), lambda i: (0, 0)),
                  pl.BlockSpec((tb, LANES), rev)],
        out_specs=(pl.BlockSpec((tb, LANES), rev), pl.BlockSpec((1, LANES), lambda i: (0, 0))),
        scratch_shapes=[pltpu.VMEM((1, LANES), F32), pltpu.VMEM((1, LANES), F32)], name=name,
        compiler_params=_cp("arbitrary"))(fl, b, dF)
    return dfl, db[0]


def _tri(tk, rel):
    r = lax.broadcasted_iota(jnp.int32, (tk, tk), 0)
    c = lax.broadcasted_iota(jnp.int32, (tk, tk), 1)
    m = {"gt": r > c, "le": r <= c, "lt": r < c}[rel]
    return m.astype(BF16)


def _split2(x):
    hi = x.astype(BF16)
    return hi, (x - hi.astype(F32)).astype(BF16)


HEADS_PER_STEP = 2
CUM_CHUNK = 256


def _cum_cols(x, tri, suffix):
    ck = tri.shape[0]
    n = x.shape[1] // ck
    hi, lo = _split2(x)
    parts, sums = [], []
    for c in range(n):
        cs = slice(c * ck, (c + 1) * ck)
        parts.append(_dot(hi[:, cs], tri) + _dot(lo[:, cs], tri))
        sums.append(jnp.sum(x[:, cs], axis=1, keepdims=True))
    carry = None
    for c in (reversed(range(n)) if suffix else range(n)):
        if carry is not None:
            parts[c] = parts[c] + carry
        carry = sums[c] if carry is None else carry + sums[c]
    return (parts[0] if n == 1 else jnp.concatenate(parts, axis=1)), carry


def _diag_mask(tq, strict):
    r = lax.broadcasted_iota(jnp.int32, (tq, tq), 0)
    c = lax.broadcasted_iota(jnp.int32, (tq, tq), 1)
    return c < r if strict else c <= r


def _sb_fwd(qkv, *, name, n_heads=N_SB, q_off=0, k_off=N_SB, v_off=2 * N_SB, tq=512, hp=HEADS_PER_STEP,
            comm=None):
    S = qkv.shape[0]
    tq = _pick(S, tq)
    tk = tq
    scale = HEAD ** -0.5
    nq = S // tq
    assert n_heads % hp == 0 and q_off % hp == 0 and k_off % hp == 0 and v_off % hp == 0

    def body(q_ref, k_ref, v_ref, o_ref, t_ref, c_sc, acc_sc):
        qi = pl.program_id(1)
        c_sc[...] = jnp.zeros_like(c_sc)
        acc_sc[...] = jnp.zeros_like(acc_sc)
        gt = _tri(min(CUM_CHUNK, tk), "gt")

        def tile(hh, j, diag):
            cs = slice(hh * HEAD, (hh + 1) * HEAD)
            rows = pl.ds(pl.multiple_of(j * tk, tk), tk)
            z = _dot_nt(q_ref[:, cs], k_ref[rows, cs]) * scale
            sp = jnp.log(1.0 + jnp.exp(-jnp.abs(z)))
            la = jnp.minimum(z, 0.0) - sp
            lb = -jnp.maximum(z, 0.0) - sp
            if diag:
                strict = _diag_mask(tq, True)
                lb = jnp.where(strict, lb, 0.0)
            suffix, total = _cum_cols(lb, gt, suffix=True)
            w = jnp.exp(la + suffix + c_sc[hh])
            if diag:
                w = jnp.where(strict, w, 0.0)
            acc_sc[hh] += _dot(w.astype(BF16), v_ref[rows, cs])
            c_sc[hh] += total

        for hh in range(hp):
            tile(hh, qi, True)

        def step(it, carry):
            for hh in range(hp):
                tile(hh, qi - 1 - it, False)
            return carry

        lax.fori_loop(0, qi, step, 0)
        for hh in range(hp):
            o_ref[:, hh * HEAD:(hh + 1) * HEAD] = acc_sc[hh]
            t_ref[hh] = jnp.broadcast_to(c_sc[hh], (tq, LANES))

    w = hp * HEAD
    head_blk = lambda off: pl.BlockSpec((S, w), lambda h, i: (0, h + off // hp))
    outs, comm_outs = _hosted_call(
        body,
        out_shape=(jax.ShapeDtypeStruct((S, n_heads * HEAD), F32),
                   jax.ShapeDtypeStruct((n_heads, S, LANES), F32)),
        grid=(n_heads // hp, nq),
        in_specs=[pl.BlockSpec((tq, w), lambda h, i: (i, h + q_off // hp)), head_blk(k_off), head_blk(v_off)],
        out_specs=(pl.BlockSpec((tq, w), lambda h, i: (i, h)),
                   pl.BlockSpec((hp, tq, LANES), lambda h, i: (h, i, 0))),
        scratch_shapes=[pltpu.VMEM((hp, tq, 1), F32), pltpu.VMEM((hp, tq, HEAD), F32)], name=name,
        sem=("parallel", "arbitrary"), operands=(qkv, qkv, qkv), comm=comm)
    return outs if comm is None else (outs, comm_outs)


def _sb_bwd(qkv, do, tstat, *, name, n_heads=N_SB, q_off=0, k_off=N_SB, v_off=2 * N_SB, do_off=0, tq=512,
            hp=HEADS_PER_STEP, comm=None):
    S = qkv.shape[0]
    tq = _pick(S, tq)
    tk = tq
    scale = HEAD ** -0.5
    nq = S // tq
    assert n_heads % hp == 0 and q_off % hp == 0 and k_off % hp == 0 and v_off % hp == 0 and do_off % hp == 0

    def body(q_ref, k_ref, v_ref, do_ref, t_ref, dq_ref, dk_ref, dv_ref, p_sc, r_sc, dq_sc):
        qi = pl.program_id(1)

        @pl.when(qi == 0)
        def _():
            dk_ref[...] = jnp.zeros_like(dk_ref)
            dv_ref[...] = jnp.zeros_like(dv_ref)

        p_sc[...] = jnp.zeros_like(p_sc)
        r_sc[...] = jnp.zeros_like(r_sc)
        dq_sc[...] = jnp.zeros_like(dq_sc)
        le = _tri(min(CUM_CHUNK, tk), "le")
        lt = _tri(min(CUM_CHUNK, tk), "lt")

        def tile(hh, j, diag):
            cs = slice(hh * HEAD, (hh + 1) * HEAD)
            rows = pl.ds(pl.multiple_of(j * tk, tk), tk)
            q = q_ref[:, cs]
            do_b = do_ref[:, cs].astype(BF16)
            kb = k_ref[rows, cs]
            z = _dot_nt(q, kb) * scale
            sp = jnp.log(1.0 + jnp.exp(-jnp.abs(z)))
            la = jnp.minimum(z, 0.0) - sp
            lb = -jnp.maximum(z, 0.0) - sp
            if diag:
                strict = _diag_mask(tq, True)
                lb = jnp.where(strict, lb, 0.0)
            prefix, total_b = _cum_cols(lb, le, suffix=False)
            w = jnp.exp(la + t_ref[hh, :, 0:1] - (prefix + p_sc[hh]))
            if diag:
                w = jnp.where(strict, w, 0.0)
            r = w * _dot_nt(do_b, v_ref[rows, cs])
            rex, total_r = _cum_cols(r, lt, suffix=False)
            rex = rex + r_sc[hh]
            beta = jnp.exp(la)
            dz = (r * (1.0 - beta) - rex * beta) * scale
            if diag:
                dz = jnp.where(strict, dz, 0.0)
            dzb = dz.astype(BF16)
            dq_sc[hh] += _dot(dzb, kb)
            dk_ref[rows, cs] += _dot_tn(dzb, q)
            dv_ref[rows, cs] += _dot_tn(w.astype(BF16), do_b)
            p_sc[hh] += total_b
            r_sc[hh] += total_r

        def step(j, carry):
            for hh in range(hp):
                tile(hh, j, False)
            return carry

        lax.fori_loop(0, qi, step, 0)
        for hh in range(hp):
            tile(hh, qi, True)
            dq_ref[:, hh * HEAD:(hh + 1) * HEAD] = dq_sc[hh]

    w = hp * HEAD
    head_blk = lambda off: pl.BlockSpec((S, w), lambda h, i: (0, h + off // hp))
    out_head = pl.BlockSpec((S, w), lambda h, i: (0, h))
    out_sd = jax.ShapeDtypeStruct((S, n_heads * HEAD), F32)
    outs, comm_outs = _hosted_call(
        body, out_shape=(out_sd, out_sd, out_sd), grid=(n_heads // hp, nq),
        in_specs=[pl.BlockSpec((tq, w), lambda h, i: (i, h + q_off // hp)), head_blk(k_off), head_blk(v_off),
                  pl.BlockSpec((tq, w), lambda h, i: (i, h + do_off // hp)),
                  pl.BlockSpec((hp, tq, LANES), lambda h, i: (h, i, 0))],
        out_specs=(pl.BlockSpec((tq, w), lambda h, i: (i, h)), out_head, out_head),
        scratch_shapes=[pltpu.VMEM((hp, tq, 1), F32), pltpu.VMEM((hp, tq, 1), F32), pltpu.VMEM((hp, tq, HEAD), F32)],
        name=name, sem=("parallel", "arbitrary"), operands=(qkv, qkv, qkv, do, tstat), comm=comm)
    return outs if comm is None else (outs, comm_outs)


def _attn_fwd(q, k, v, bias, *, name, n_heads, dqk, scale, v_off=0, tq=1024, exact_p=False, hp=HEADS_PER_STEP,
              comm=None):
    S = q.shape[0]
    tq = _pick(S, tq)
    tk = tq
    nq = S // tq
    has_bias = bias is not None

    assert n_heads % hp == 0 and v_off % hp == 0

    def body(*refs):
        q_ref, k_ref, v_ref = refs[:3]
        b_ref = refs[3] if has_bias else None
        o_ref, lse_ref, m_sc, l_sc, acc_sc = refs[-5:]
        qi = pl.program_id(1)
        m_sc[...] = jnp.full_like(m_sc, NEG)
        l_sc[...] = jnp.zeros_like(l_sc)
        acc_sc[...] = jnp.zeros_like(acc_sc)

        def tile(hh, j, diag):
            rows = pl.ds(pl.multiple_of(j * tk, tk), tk)
            s = _dot_nt(q_ref[:, hh * dqk:(hh + 1) * dqk], k_ref[rows, hh * dqk:(hh + 1) * dqk]) * scale
            if has_bias:
                s = s + b_ref[hh, :, rows]
            if diag:
                s = jnp.where(_diag_mask(tq, False), s, NEG)
            m_old = m_sc[hh]
            m_new = jnp.maximum(m_old, jnp.max(s, axis=1, keepdims=True))
            alpha = jnp.exp(m_old - m_new)
            p = jnp.exp(s - m_new)
            l_sc[hh] = alpha * l_sc[hh] + jnp.sum(p, axis=1, keepdims=True)
            vb = v_ref[rows, hh * HEAD:(hh + 1) * HEAD]
            if exact_p:
                hi, lo = _split2(p)
                pv = _dot(hi, vb) + _dot(lo, vb)
            else:
                pv = _dot(p.astype(BF16), vb)
            acc_sc[hh] = alpha * acc_sc[hh] + pv
            m_sc[hh] = m_new

        def step(j, carry):
            for hh in range(hp):
                tile(hh, j, False)
            return carry

        lax.fori_loop(0, qi, step, 0)
        for hh in range(hp):
            tile(hh, qi, True)
            l = l_sc[hh]
            o_ref[:, hh * HEAD:(hh + 1) * HEAD] = acc_sc[hh] / l
            lse_ref[hh] = jnp.broadcast_to(m_sc[hh] + jnp.log(l), (tq, LANES))

    in_specs = [pl.BlockSpec((tq, hp * dqk), lambda h, i: (i, h)),
                pl.BlockSpec((S, hp * dqk), lambda h, i: (0, h)),
                pl.BlockSpec((S, hp * HEAD), lambda h, i: (0, h + v_off // hp))]
    operands = [q, k, v]
    if has_bias:
        in_specs.append(pl.BlockSpec((hp, 1, S), lambda h, i: (h, 0, 0)))
        operands.append(bias)
    outs, comm_outs = _hosted_call(
        body,
        out_shape=(jax.ShapeDtypeStruct((S, n_heads * HEAD), F32),
                   jax.ShapeDtypeStruct((n_heads, S, LANES), F32)),
        grid=(n_heads // hp, nq), in_specs=in_specs,
        out_specs=(pl.BlockSpec((tq, hp * HEAD), lambda h, i: (i, h)),
                   pl.BlockSpec((hp, tq, LANES), lambda h, i: (h, i, 0))),
        scratch_shapes=[pltpu.VMEM((hp, tq, 1), F32), pltpu.VMEM((hp, tq, 1), F32), pltpu.VMEM((hp, tq, HEAD), F32)],
        name=name, sem=("parallel", "arbitrary"), operands=operands, comm=comm)
    return outs if comm is None else (outs, comm_outs)


def _attn_bwd(q, k, v, bias, o, lse, do, *, name, n_heads, dqk, scale, v_off=0, do_off=0, tq=512,
              hp=HEADS_PER_STEP, comm=None):
    S = q.shape[0]
    tq = _pick(S, tq)
    tk = tq
    nq = S // tq
    has_bias = bias is not None
    assert n_heads % hp == 0 and v_off % hp == 0 and do_off % hp == 0

    def body(*refs):
        q_ref, k_ref, v_ref, o_ref, lse_ref, do_ref = refs[:6]
        b_ref = refs[6] if has_bias else None
        n_out = 5 if has_bias else 3
        outs = refs[-(n_out + 3):-3]
        dq_ref, dk_ref, dv_ref = outs[:3]
        db_ref, dr_ref = (outs[3], outs[4]) if has_bias else (None, None)
        dq_sc, rs_sc, delta_sc = refs[-3:]
        qi = pl.program_id(1)

        @pl.when(qi == 0)
        def _():
            dk_ref[...] = jnp.zeros_like(dk_ref)
            dv_ref[...] = jnp.zeros_like(dv_ref)
            if has_bias:
                db_ref[...] = jnp.zeros_like(db_ref)

        dq_sc[...] = jnp.zeros_like(dq_sc)
        rs_sc[...] = jnp.zeros_like(rs_sc)
        for hh in range(hp):
            vs = slice(hh * HEAD, (hh + 1) * HEAD)
            do_r = do_ref[:, vs].astype(BF16).astype(F32)
            delta_sc[hh] = jnp.sum(do_r * o_ref[:, vs], axis=1, keepdims=True)

        def tile(hh, j, diag):
            qs = slice(hh * dqk, (hh + 1) * dqk)
            vs = slice(hh * HEAD, (hh + 1) * HEAD)
            rows = pl.ds(pl.multiple_of(j * tk, tk), tk)
            qb = q_ref[:, qs]
            do_b = do_ref[:, vs].astype(BF16)
            delta = delta_sc[hh]
            kb = k_ref[rows, qs]
            s = _dot_nt(qb, kb) * scale
            if has_bias:
                s = s + b_ref[hh, :, rows]
            p = jnp.exp(s - lse_ref[hh, :, 0:1])
            if diag:
                p = jnp.where(_diag_mask(tq, False), p, 0.0)
            ds = p * (_dot_nt(do_b, v_ref[rows, vs]) - delta)
            dsb = (ds * scale).astype(BF16)
            dq_sc[hh] += _dot(dsb, kb)
            dk_ref[rows, qs] += _dot_tn(dsb, qb)
            dv_ref[rows, vs] += _dot_tn(p.astype(BF16), do_b)
            if has_bias:
                db_ref[hh, :, rows] += jnp.sum(ds, axis=0, keepdims=True)
                rs_sc[hh] += jnp.sum(ds, axis=1, keepdims=True)

        def step(j, carry):
            for hh in range(hp):
                tile(hh, j, False)
            return carry

        lax.fori_loop(0, qi, step, 0)
        for hh in range(hp):
            tile(hh, qi, True)
            dq_ref[:, hh * dqk:(hh + 1) * dqk] = dq_sc[hh]
            if has_bias:
                dr_ref[hh] = jnp.broadcast_to(rs_sc[hh], (tq, LANES))

    stat = pl.BlockSpec((hp, tq, LANES), lambda h, i: (h, i, 0))
    in_specs = [pl.BlockSpec((tq, hp * dqk), lambda h, i: (i, h)),
                pl.BlockSpec((S, hp * dqk), lambda h, i: (0, h)),
                pl.BlockSpec((S, hp * HEAD), lambda h, i: (0, h + v_off // hp)),
                pl.BlockSpec((tq, hp * HEAD), lambda h, i: (i, h)),
                stat,
                pl.BlockSpec((tq, hp * HEAD), lambda h, i: (i, h + do_off // hp))]
    operands = [q, k, v, o, lse, do]
    out_shape = [jax.ShapeDtypeStruct((S, n_heads * dqk), F32), jax.ShapeDtypeStruct((S, n_heads * dqk), F32),
                 jax.ShapeDtypeStruct((S, n_heads * HEAD), F32)]
    out_specs = [pl.BlockSpec((tq, hp * dqk), lambda h, i: (i, h)), pl.BlockSpec((S, hp * dqk), lambda h, i: (0, h)),
                 pl.BlockSpec((S, hp * HEAD), lambda h, i: (0, h))]
    if has_bias:
        in_specs.append(pl.BlockSpec((hp, 1, S), lambda h, i: (h, 0, 0)))
        operands.append(bias)
        out_shape.append(jax.ShapeDtypeStruct((n_heads, 1, S), F32))
        out_specs.append(pl.BlockSpec((hp, 1, S), lambda h, i: (h, 0, 0)))
        out_shape.append(jax.ShapeDtypeStruct((n_heads, S, LANES), F32))
        out_specs.append(stat)
    outs, comm_outs = _hosted_call(
        body, out_shape=tuple(out_shape), grid=(n_heads // hp, nq), in_specs=in_specs, out_specs=tuple(out_specs),
        scratch_shapes=[pltpu.VMEM((hp, tq, dqk), F32), pltpu.VMEM((hp, tq, 1), F32),
                        pltpu.VMEM((hp, tq, 1), F32)], name=name,
        sem=("parallel", "arbitrary"), operands=operands, comm=comm)
    return outs if comm is None else (outs, comm_outs)


def _rot_half(y):
    lane = lax.broadcasted_iota(jnp.int32, y.shape, 1)
    up = pltpu.roll(y, 96, 1)
    down = pltpu.roll(y, 32, 1)
    return jnp.where(lane < 32, -up, jnp.where(lane < 64, down, 0.0))


def _mla_prep_fwd(q_raw, kv_raw, down, cos, sin, q_g, k_g, *, name, ts=128):
    S = q_raw.shape[0]
    ts = _pick(S, ts)
    pe_blk = Q_RANK // LANES + KV_RANK // LANES

    def norm_rope(x0, x1, g0, g1, c, s):
        ms = (jnp.sum(x0 * x0, axis=-1, keepdims=True) + jnp.sum(x1 * x1, axis=-1, keepdims=True)) * (1.0 / QK_DIM)
        rstd = lax.rsqrt(ms + EPS)
        y0 = (x0 * rstd) * g0
        y1 = (x1 * rstd) * g1
        return y0, y1 * c + _rot_half(y1) * s

    def body(q_ref, kv_ref, pe_ref, cos_ref, sin_ref, qg_ref, kg_ref, qo_ref, ko_ref, vo_ref):
        c, s = cos_ref[...], sin_ref[...]
        pe = pe_ref[...]
        qg0, qg1 = qg_ref[:, :NOPE], qg_ref[:, NOPE:]
        kg0, kg1 = kg_ref[:, :NOPE], kg_ref[:, NOPE:]
        for h in range(N_MLA):
            b = h * QK_PAD
            y0, y1 = norm_rope(q_ref[:, b:b + NOPE], q_ref[:, b + NOPE:b + QK_PAD], qg0, qg1, c, s)
            qo_ref[:, b:b + NOPE] = y0.astype(BF16)
            qo_ref[:, b + NOPE:b + QK_PAD] = y1.astype(BF16)
            y0, y1 = norm_rope(kv_ref[:, b:b + NOPE], pe, kg0, kg1, c, s)
            ko_ref[:, b:b + NOPE] = y0.astype(BF16)
            ko_ref[:, b + NOPE:b + QK_PAD] = y1.astype(BF16)
            vo_ref[:, h * HEAD:(h + 1) * HEAD] = kv_ref[:, b + NOPE:b + QK_PAD].astype(BF16)

    wide = pl.BlockSpec((ts, N_MLA * QK_PAD), lambda i: (i, 0))
    lane_blk = pl.BlockSpec((ts, LANES), lambda i: (i, 0))
    gain = pl.BlockSpec((1, QK_PAD), lambda i: (0, 0))
    return pl.pallas_call(
        body,
        out_shape=(jax.ShapeDtypeStruct((S, N_MLA * QK_PAD), BF16), jax.ShapeDtypeStruct((S, N_MLA * QK_PAD), BF16),
                   jax.ShapeDtypeStruct((S, N_MLA * HEAD), BF16)),
        grid=(S // ts,),
        in_specs=[wide, wide, pl.BlockSpec((ts, LANES), lambda i: (i, pe_blk)), lane_blk, lane_blk, gain, gain],
        out_specs=(wide, wide, pl.BlockSpec((ts, N_MLA * HEAD), lambda i: (i, 0))), name=name,
        compiler_params=_cp("parallel"))(q_raw, kv_raw, down, cos, sin, q_g, k_g)


def _mla_prep_bwd(dq, dk, dv, q_raw, kv_raw, down, cos, sin, q_g, k_g, *, name, ts=128):
    S = q_raw.shape[0]
    ts = _pick(S, ts)
    nsteps = S // ts
    pe_blk = Q_RANK // LANES + KV_RANK // LANES

    def back(x0, x1, g0, g1, c, s, d0, d1r):
        d1 = d1r * c - _rot_half(d1r * s)
        ms = (jnp.sum(x0 * x0, axis=-1, keepdims=True) + jnp.sum(x1 * x1, axis=-1, keepdims=True)) * (1.0 / QK_DIM)
        rstd = lax.rsqrt(ms + EPS)
        h0, h1 = x0 * rstd, x1 * rstd
        e0, e1 = d0 * g0, d1 * g1
        m = (jnp.sum(e0 * h0, axis=-1, keepdims=True) + jnp.sum(e1 * h1, axis=-1, keepdims=True)) * (1.0 / QK_DIM)
        return rstd * (e0 - h0 * m), rstd * (e1 - h1 * m), d0 * h0, d1 * h1

    def fold(a):
        return jnp.sum(a.reshape(ts // 8, 8, a.shape[-1]), axis=0)

    def body(dq_ref, dk_ref, dv_ref, q_ref, kv_ref, pe_ref, cos_ref, sin_ref, qg_ref, kg_ref,
             dqr_ref, dkv_ref, dpe_ref, dqg_ref, dkg_ref, gq_sc, gk_sc):
        i = pl.program_id(0)

        @pl.when(i == 0)
        def _():
            gq_sc[...] = jnp.zeros_like(gq_sc)
            gk_sc[...] = jnp.zeros_like(gk_sc)

        c, s = cos_ref[...], sin_ref[...]
        pe = pe_ref[...]
        qg0, qg1 = qg_ref[:, :NOPE], qg_ref[:, NOPE:]
        kg0, kg1 = kg_ref[:, :NOPE], kg_ref[:, NOPE:]
        dpe = jnp.zeros((ts, LANES), F32)
        for h in range(N_MLA):
            b = h * QK_PAD
            dx0, dx1, a0, a1 = back(q_ref[:, b:b + NOPE], q_ref[:, b + NOPE:b + QK_PAD], qg0, qg1, c, s,
                                    dq_ref[:, b:b + NOPE], dq_ref[:, b + NOPE:b + QK_PAD])
            dqr_ref[:, b:b + NOPE] = dx0.astype(BF16)
            dqr_ref[:, b + NOPE:b + QK_PAD] = dx1.astype(BF16)
            gq_sc[:, :NOPE] += fold(a0)
            gq_sc[:, NOPE:] += fold(a1)
            dx0, dx1, a0, a1 = back(kv_ref[:, b:b + NOPE], pe, kg0, kg1, c, s,
                                    dk_ref[:, b:b + NOPE], dk_ref[:, b + NOPE:b + QK_PAD])
            dkv_ref[:, b:b + NOPE] = dx0.astype(BF16)
            dkv_ref[:, b + NOPE:b + QK_PAD] = dv_ref[:, h * HEAD:(h + 1) * HEAD].astype(BF16)
            dpe = dpe + dx1
            gk_sc[:, :NOPE] += fold(a0)
            gk_sc[:, NOPE:] += fold(a1)
        dpe_ref[...] = dpe

        @pl.when(i == nsteps - 1)
        def _():
            dqg_ref[...] = jnp.sum(gq_sc[...], axis=0, keepdims=True)
            dkg_ref[...] = jnp.sum(gk_sc[...], axis=0, keepdims=True)

    wide = pl.BlockSpec((ts, N_MLA * QK_PAD), lambda i: (i, 0))
    lane_blk = pl.BlockSpec((ts, LANES), lambda i: (i, 0))
    gain = pl.BlockSpec((1, QK_PAD), lambda i: (0, 0))
    outs = pl.pallas_call(
        body,
        out_shape=(jax.ShapeDtypeStruct((S, N_MLA * QK_PAD), BF16), jax.ShapeDtypeStruct((S, N_MLA * QK_PAD), BF16),
                   jax.ShapeDtypeStruct((S, LANES), F32), jax.ShapeDtypeStruct((1, QK_PAD), F32),
                   jax.ShapeDtypeStruct((1, QK_PAD), F32)),
        grid=(nsteps,),
        in_specs=[wide, wide, pl.BlockSpec((ts, N_MLA * HEAD), lambda i: (i, 0)), wide, wide,
                  pl.BlockSpec((ts, LANES), lambda i: (i, pe_blk)), lane_blk, lane_blk, gain, gain],
        out_specs=(wide, wide, lane_blk, gain, gain),
        scratch_shapes=[pltpu.VMEM((8, QK_PAD), F32), pltpu.VMEM((8, QK_PAD), F32)], name=name,
        compiler_params=_cp("arbitrary"))(dq, dk, dv, q_raw, kv_raw, down, cos, sin, q_g, k_g)
    return outs[0], outs[1], outs[2], outs[3][0], outs[4][0]


def _loss_head(y, target, *, name, tr=256):
    R, C = y.shape
    tr = _pick(R, tr)
    nsteps = R // tr

    def body(y_ref, t_ref, dy_ref, dyb_ref, loss_ref, acc):
        i = pl.program_id(0)

        @pl.when(i == 0)
        def _():
            acc[...] = jnp.zeros_like(acc)

        err = y_ref[...] - t_ref[...]
        dy = err * (1.0 / C)
        dy_ref[...] = dy
        dyb_ref[...] = dy.astype(BF16)
        acc[...] += jnp.sum((err * err).reshape(tr // 8, 8, C), axis=0)

        @pl.when(i == nsteps - 1)
        def _():
            tot = jnp.sum(jnp.sum(acc[...], axis=0, keepdims=True), axis=1, keepdims=True)
            loss_ref[...] = jnp.broadcast_to(tot * (0.5 / C), (8, LANES))

    blk = pl.BlockSpec((tr, C), lambda i: (i, 0))
    dy, dy_b, loss = pl.pallas_call(
        body, out_shape=(jax.ShapeDtypeStruct((R, C), F32), jax.ShapeDtypeStruct((R, C), BF16),
                         jax.ShapeDtypeStruct((8, LANES), F32)),
        grid=(nsteps,), in_specs=[blk, blk], out_specs=(blk, blk, pl.BlockSpec((8, LANES), lambda i: (0, 0))),
        scratch_shapes=[pltpu.VMEM((8, C), F32)], name=name, compiler_params=_cp("arbitrary"))(y, target)
    return dy, dy_b, loss[0, 0]


def _adamw(w, g, m, v, *, name, block_bytes=1 << 20, comm=None):
    L, R, C = w.shape
    tr = max(8, min(R, (block_bytes // (4 * C)) // 8 * 8))
    while R % tr:
        tr -= 8
    if tr <= 0:
        tr = R
    c1 = 1.0 / (1.0 - ADAM_B1 ** ADAM_STEP)
    c2 = 1.0 / (1.0 - ADAM_B2 ** ADAM_STEP)

    def body(w_ref, g_ref, m_ref, v_ref, d_ref, mo_ref, vo_ref):
        gv = g_ref[...]
        mn = ADAM_B1 * m_ref[...] + (1.0 - ADAM_B1) * gv
        vn = ADAM_B2 * v_ref[...] + (1.0 - ADAM_B2) * (gv * gv)
        d_ref[...] = -ADAM_LR * ((mn * c1) / (jnp.sqrt(vn * c2) + ADAM_EPS) + ADAM_WD * w_ref[...])
        mo_ref[...] = mn
        vo_ref[...] = vn

    blk = pl.BlockSpec((None, tr, C), lambda l, i: (l, i, 0))
    sd = jax.ShapeDtypeStruct((L, R, C), F32)
    outs, comm_outs = _hosted_call(
        body, out_shape=(sd, sd, sd), grid=(L, R // tr), in_specs=[blk] * 4, out_specs=(blk,) * 3, scratch_shapes=[],
        name=name, sem=("parallel", "parallel"), operands=(w, g, m, v), comm=comm)
    return outs if comm is None else (outs, comm_outs)


def _row_tile(r, c, itemsize=4, block_bytes=1 << 20):
    tr = max(16, min(r, (block_bytes // (itemsize * c)) // 16 * 16))
    while r % tr:
        tr -= 16
    return tr if tr > 0 else r


def _add_sibling(g, recv, core, *, name):
    nch, _, r, c = g.shape
    tr = _row_tile(r, c)

    def body(core_ref, g_ref, r_ref, o_ref):
        o_ref[...] = (g_ref[...] + r_ref[...]).astype(BF16)

    grid_spec = pltpu.PrefetchScalarGridSpec(
        num_scalar_prefetch=1, grid=(nch, r // tr),
        in_specs=[pl.BlockSpec((None, None, tr, c), lambda j, i, cr: (j, cr[0], i, 0)),
                  pl.BlockSpec((None, tr, c), lambda j, i, cr: (j, i, 0))],
        out_specs=pl.BlockSpec((None, tr, c), lambda j, i, cr: (j, i, 0)))
    return pl.pallas_call(
        body, out_shape=jax.ShapeDtypeStruct((nch, r, c), BF16), grid_spec=grid_spec, name=name,
        compiler_params=_cp("parallel", "parallel"))(core, g, recv)


def _add_chips(slots, *, name):
    nch, r, c = slots.shape
    tr = _row_tile(r, c)

    def body(s_ref, o_ref):
        acc = s_ref[0].astype(F32)
        for j in range(1, nch):
            acc = acc + s_ref[j].astype(F32)
        o_ref[...] = acc

    return pl.pallas_call(
        body, out_shape=jax.ShapeDtypeStruct((r, c), F32), grid=(r // tr,),
        in_specs=[pl.BlockSpec((nch, tr, c), lambda i: (0, i, 0))],
        out_specs=pl.BlockSpec((tr, c), lambda i: (i, 0)), name=name, compiler_params=_cp("parallel"))(slots)


def _place():
    x, y, c = lax.axis_index("x"), lax.axis_index("y"), lax.axis_index("c")
    others = [(1 - x, y), (x, 1 - y), (1 - x, 1 - y)]
    return x, y, c, 2 * x + y, others


ANY = pl.BlockSpec(memory_space=pl.ANY)


class _Exchange:
    def __init__(self, kind, arrays):
        self.kind, self.ins = kind, list(arrays)
        self.n_peers = 1 if kind in ("swap", "join") else 3
        self.aliased = kind == "forward"
        n = len(self.ins) * self.n_peers
        shp = {"gather": lambda a: (N_CHIPS,) + a.shape, "scatter": lambda a: a.shape, "forward": lambda a: a.shape,
               "swap": lambda a: (a.shape[0],) + a.shape[2:], "join": lambda a: (2,) + a.shape}[kind]
        self.out_shapes = [jax.ShapeDtypeStruct(shp(a), a.dtype) for a in self.ins]
        self.sems = [pltpu.SemaphoreType.DMA((n,)), pltpu.SemaphoreType.DMA((n,))]

    def _copies(self, ins, outs, sems):
        send, recv = sems
        x, y, c, me, others = _place()
        over_ici = self.kind in ("gather", "scatter")
        peers = [(ox, oy, c) for ox, oy in others] if over_ici else [(x, y, 1 - c)] * self.n_peers
        for a in range(len(self.ins)):
            for k, to in enumerate(peers):
                peer = 2 * others[k][0] + others[k][1]
                if self.kind == "gather":
                    hr = self.ins[a].shape[0] // 2
                    rows = pl.ds(c * hr, hr)
                    src, dst, land = ins[a].at[rows, :], outs[a].at[me, rows, :], outs[a].at[peer, rows, :]
                elif self.kind == "forward":
                    hr = self.ins[a].shape[1] // 2
                    mine, theirs = pl.ds(c * hr, hr), pl.ds((1 - c) * hr, hr)
                    src, dst, land = ins[a].at[peer, mine, :], outs[a].at[peer, mine, :], outs[a].at[peer, theirs, :]
                elif self.kind == "scatter":
                    src, dst, land = ins[a].at[peer], outs[a].at[me], outs[a].at[peer]
                elif self.kind == "swap":
                    src, dst, land = ins[a].at[:, 1 - c], outs[a], outs[a]
                else:
                    src, dst, land = ins[a], outs[a].at[c], outs[a].at[1 - c]
                i = self.n_peers * a + k
                mk = lambda s, d: pltpu.make_async_remote_copy(
                    src_ref=s, dst_ref=d, send_sem=send.at[i], recv_sem=recv.at[i], device_id=to,
                    device_id_type=MESH)
                yield mk(src, dst), mk(land, land)

    def alias_pairs(self):
        return [(i, i) for i in range(len(self.ins))] if self.aliased else []

    def start(self, ins, outs, sems):
        for cp, _ in self._copies(ins, outs, sems):
            cp.start()

    def finish(self, ins, outs, sems):
        pairs = list(self._copies(ins, outs, sems))
        for _, landing in pairs:
            landing.wait_recv()
        for cp, _ in pairs:
            cp.wait_send()


class _Several:
    def __init__(self, parts):
        self.parts = list(parts)
        self.ins = [a for p in self.parts for a in p.ins]
        self.out_shapes = [s for p in self.parts for s in p.out_shapes]
        self.sems = [s for p in self.parts for s in p.sems]

    def split(self, ins, outs, sems=None):
        i = 0
        for k, p in enumerate(self.parts):
            n = len(p.ins)
            yield p, ins[i:i + n], outs[i:i + n], None if sems is None else sems[2 * k:2 * k + 2]
            i += n

    def alias_pairs(self):
        pairs, i = [], 0
        for p in self.parts:
            pairs += [(i + a, i + b) for a, b in p.alias_pairs()]
            i += len(p.ins)
        return pairs

    def start(self, ins, outs, sems):
        for p, a, b, s in self.split(ins, outs, sems):
            p.start(a, b, s)

    def finish(self, ins, outs, sems):
        for p, a, b, s in self.split(ins, outs, sems):
            p.finish(a, b, s)


def _hosted_call(body, *, grid, in_specs, out_specs, out_shape, scratch_shapes, operands, name, sem, comm=None):
    out_specs, out_shape = tuple(out_specs), tuple(out_shape)
    if isinstance(comm, (list, tuple)):
        several = _Several(comm)
        outs, comm_outs = _hosted_call(body, grid=grid, in_specs=in_specs, out_specs=out_specs, out_shape=out_shape,
                                       scratch_shapes=scratch_shapes, operands=operands, name=name, sem=sem,
                                       comm=several)
        return outs, [tuple(o) for _, _, o, _ in several.split(several.ins, comm_outs)]
    if comm is None:
        res = pl.pallas_call(body, out_shape=out_shape, grid=grid, in_specs=list(in_specs), out_specs=out_specs,
                             scratch_shapes=list(scratch_shapes), name=name, compiler_params=_cp(*sem))(*operands)
        return tuple(res), ()
    n_in, n_out, n_sc = len(in_specs), len(out_specs), len(scratch_shapes)
    ci, co = len(comm.ins), len(comm.out_shapes)

    def wrapped(*refs):
        ins, c_ins = refs[:n_in], refs[n_in:n_in + ci]
        outs = refs[n_in + ci:n_in + ci + n_out]
        c_outs = refs[n_in + ci + n_out:n_in + ci + n_out + co]
        scratch = refs[n_in + ci + n_out + co:n_in + ci + n_out + co + n_sc]
        sems = refs[n_in + ci + n_out + co + n_sc:]
        ids = [pl.program_id(d) for d in range(len(grid))]
        first = functools.reduce(jnp.logical_and, [i == 0 for i in ids])
        last = functools.reduce(jnp.logical_and, [i == g - 1 for i, g in zip(ids, grid)])

        @pl.when(first)
        def _():
            comm.start(c_ins, c_outs, sems)

        body(*ins, *outs, *scratch)

        @pl.when(last)
        def _():
            comm.finish(c_ins, c_outs, sems)

    res = pl.pallas_call(
        wrapped, out_shape=out_shape + tuple(comm.out_shapes), grid=grid, in_specs=list(in_specs) + [ANY] * ci,
        out_specs=out_specs + tuple([ANY] * co), scratch_shapes=list(scratch_shapes) + comm.sems, name=name,
        input_output_aliases={n_in + i: n_out + o for i, o in comm.alias_pairs()},
        compiler_params=pltpu.CompilerParams(dimension_semantics=("arbitrary",) * len(grid),
                                             vmem_limit_bytes=VMEM_LIMIT, has_side_effects=True),
    )(*operands, *comm.ins)
    return tuple(res[:n_out]), tuple(res[n_out:])


def _run_exchange(comm, *, name):
    ci = len(comm.ins)

    def body(*refs):
        ins, outs, sems = refs[:ci], refs[ci:2 * ci], refs[2 * ci:]
        comm.start(ins, outs, sems)
        comm.finish(ins, outs, sems)

    return pl.pallas_call(
        body, out_shape=tuple(comm.out_shapes), in_specs=[ANY] * ci, out_specs=tuple([ANY] * ci),
        scratch_shapes=comm.sems, name=name, input_output_aliases=dict(comm.alias_pairs()),
        compiler_params=pltpu.CompilerParams(has_side_effects=True))(*comm.ins)


def _own_slot(buf, piece, idx):
    return lax.dynamic_update_slice(buf, piece[None], (idx,) + (0,) * piece.ndim)


def _all_reduce_small(v, *, name):
    R = v.shape[0]

    flips = [(dx, dy, dc) for dx in range(2) for dy in range(2) for dc in range(2) if dx or dy or dc]

    def body(v_ref, o_ref, slots, send, recv):
        x, y, c, me, others = _place()
        mine = 2 * me + c
        slots[mine] = v_ref[...]

        def copy(k, slot):
            dx, dy, dc = flips[k]
            peer = (x + dx - 2 * x * dx, y + dy - 2 * y * dy, c + dc - 2 * c * dc)
            peer_slot = 4 * peer[0] + 2 * peer[1] + peer[2]
            return pltpu.make_async_remote_copy(
                src_ref=v_ref, dst_ref=slots.at[mine if slot == "mine" else peer_slot], send_sem=send.at[k],
                recv_sem=recv.at[k], device_id=peer, device_id_type=MESH)

        for k in range(7):
            copy(k, "mine").start()
        for k in range(7):
            copy(k, "peer").wait_recv()
        for k in range(7):
            copy(k, "mine").wait_send()
        acc = slots[0]
        for j in range(1, 8):
            acc = acc + slots[j]
        o_ref[...] = acc

    vm = pl.BlockSpec(memory_space=pltpu.VMEM)
    return pl.pallas_call(
        body, out_shape=jax.ShapeDtypeStruct(v.shape, F32), in_specs=[vm], out_specs=vm,
        scratch_shapes=[pltpu.VMEM((8, R, LANES), F32), pltpu.SemaphoreType.DMA((7,)),
                        pltpu.SemaphoreType.DMA((7,))],
        name=name, compiler_params=pltpu.CompilerParams(has_side_effects=True))(v)


def _rows(v, n_rows):
    v = v.reshape(-1).astype(F32)
    return jnp.pad(v, (0, n_rows * LANES - v.shape[0])).reshape(n_rows, LANES)


def _mlp_fwd(x_in, g, w_up, w_down, tag):
    h = _rms_fwd(x_in, g, name=f"{tag}_norm")
    u, a = _matmul(h, w_up, b_split=True, epilogue="sqrelu", name=f"{tag}_up")
    x_out = _matmul(a, w_down, epilogue="res", res=x_in, name=f"{tag}_down")
    return x_out, (h, u, a)


def _mlp_bwd(dy, dy_b, x_in, g, w_up, w_down, saved, tag, comms=None):
    h, u, a = saved
    comms = comms or {}
    landed = {}

    def mm(key, *args, **kw):
        comm = comms.get(key)
        if callable(comm):
            comm = comm(landed)
        out = _matmul(*args, name=f"{tag}_{key}", comm=comm, **kw)
        if comm is not None:
            out, landed[key] = out
        return out

    dw_down = mm("dwdown", a, dy_b, form="tn")
    du = mm("du", dy_b, w_down, form="nt", epilogue="sqrelu_bwd", u=u, out_dtype=BF16)
    dw_up = mm("dwup", h, du, form="tn", out_split=True)
    dh = mm("dh", du, w_up, form="nt", b_split=True)
    dx, dx_b, dg = _rms_bwd(x_in, g, dh, res=dy, bf16_copy=True, name=f"{tag}_dnorm")
    return dx, dx_b, dg, dw_up, dw_down, landed


def kernel(x, positions, ln_mix_g, ln_mlp_g, sf_w_in, sf_b_f, fox_q_g, fox_k_g, sf_w_o, mla_w_down, mla_q_a_g, mla_kv_a_g, mla_w_uq, mla_w_ukv, mla_q_g, mla_k_g, mla_w_o, mlp_w_up, mlp_w_down, loss_target, m_ln_mix_g, m_ln_mlp_g, m_sf_w_in, m_sf_b_f, m_fox_q_g, m_fox_k_g, m_sf_w_o, m_mla_w_down, m_mla_q_a_g, m_mla_kv_a_g, m_mla_w_uq, m_mla_w_ukv, m_mla_q_g, m_mla_k_g, m_mla_w_o, m_mlp_w_up, m_mlp_w_down, v_ln_mix_g, v_ln_mlp_g, v_sf_w_in, v_sf_b_f, v_fox_q_g, v_fox_k_g, v_sf_w_o, v_mla_w_down, v_mla_q_a_g, v_mla_kv_a_g, v_mla_w_uq, v_mla_w_ukv, v_mla_q_g, v_mla_k_g, v_mla_w_o, v_mlp_w_up, v_mlp_w_down):
    S, D = x.shape[1], x.shape[2]
    xs, tgt, pos = x[0], loss_target[0], positions[0]
    xi, yi, ci = lax.axis_index("x"), lax.axis_index("y"), lax.axis_index("c")
    chip = 2 * xi + yi
    core = ci.astype(jnp.int32).reshape(1)
    d_ff = mlp_w_up.shape[2] * N_CHIPS
    in_w = sf_w_in.shape[2] * N_CHIPS
    qkv_w = 3 * N_SB * HEAD + 3 * N_FOX * HEAD
    dn_w = mla_w_down.shape[2]
    dn_pad = Q_RANK + KV_RANK + LANES

    def gather_begin(ws):
        shards = [w.astype(BF16) for w in ws]
        return shards, _Exchange("gather", shards)

    def hand_over(landed):
        return _Exchange("forward", list(landed))

    def gather_end(both, shards):
        return [_own_slot(ag, s, chip) for ag, s in zip(both, shards)]

    cols = lambda ag: ag.transpose(1, 0, 2).reshape(ag.shape[1], -1)
    rows = lambda ag: ag.reshape(-1, ag.shape[2])
    s_mix0, ex_mix0 = gather_begin([sf_w_in[0], sf_w_o[0]])
    landed = _run_exchange(ex_mix0, name="gather_mix0")
    ag_in, ag_o0 = gather_end(_run_exchange(hand_over(landed), name="gather_mix0_sibling"), s_mix0)
    w_in_full = cols(ag_in)
    w_qkv = w_in_full[:, :qkv_w]
    w_f = jnp.pad(w_in_full[:, qkv_w:], ((0, 0), (0, LANES - (in_w - qkv_w))))
    w_o0 = rows(ag_o0)
    s_mlp0, ex_mlp0 = gather_begin([mlp_w_up[0], mlp_w_down[0]])
    s_mix1, ex_mix1 = gather_begin([mla_w_down[0], mla_w_uq[0], mla_w_ukv[0], mla_w_o[0]])
    s_mlp1, ex_mlp1 = gather_begin([mlp_w_up[1], mlp_w_down[1]])

    gain_blk = jnp.concatenate([mla_q_a_g, mla_kv_a_g], axis=0) * (ci == 0).astype(F32)
    placed = jnp.zeros((2, N_CHIPS, LANES), F32)
    placed = lax.dynamic_update_slice(placed, gain_blk[:, None, :], (0, chip, 0))
    gains = _all_reduce_small(placed.reshape(2 * N_CHIPS, LANES), name="gather_gains")
    q_a_full = gains[:N_CHIPS].reshape(Q_RANK)
    kv_a_full = gains[N_CHIPS:].reshape(KV_RANK)

    pad_gain = lambda g: jnp.pad(g.reshape(1, QK_DIM), ((0, 0), (0, QK_PAD - QK_DIM)))
    q_g_pad, k_g_pad = pad_gain(mla_q_g), pad_gain(mla_k_g)
    b_pad = _rows(sf_b_f, 1)

    h0 = _rms_fwd(xs, ln_mix_g[0], name="mix0_norm")
    qkv_sb = _matmul(h0, w_qkv, n=3 * N_SB * HEAD, b_n0=0, out_dtype=BF16, name="mix0_qkv_sb")
    qk_fx = _matmul(h0, w_qkv, n=2 * N_FOX * HEAD, b_n0=3 * N_SB * HEAD, name="mix0_qk_fox")
    v_fx = _matmul(h0, w_qkv, n=N_FOX * HEAD, b_n0=(3 * N_SB + 2 * N_FOX) * HEAD, out_dtype=BF16,
                   name="mix0_v_fox")
    fl = _matmul(h0, w_f, name="mix0_forget_logit")
    f_cum = _forget_fwd(fl, b_pad, name="forget_fwd")
    neg_f = (-f_cum[:, :N_FOX]).T.reshape(N_FOX, 1, S)
    q_f = _rms_fwd(qk_fx, fox_q_g[0], c0=0, width=N_FOX * HEAD, gw=HEAD, name="fox_q_norm")
    k_f = _rms_fwd(qk_fx, fox_k_g[0], c0=N_FOX * HEAD, width=N_FOX * HEAD, gw=HEAD, name="fox_k_norm")
    (o_sb, t_sb), landed = _sb_fwd(qkv_sb, name="sb_fwd", comm=ex_mlp0)
    (o_fx, lse0), (landed, both) = _attn_fwd(q_f, k_f, v_fx, neg_f, n_heads=N_FOX, dqk=HEAD, scale=HEAD ** -0.5,
                                             exact_p=True, name="fox_fwd", comm=[ex_mix1, hand_over(landed)])
    ag_up0, ag_dw0 = gather_end(both, s_mlp0)
    o0 = jnp.concatenate([o_sb, o_fx], axis=1)
    x1, both = _matmul(o0, w_o0, epilogue="res", res=xs, name="mix0_out", comm=hand_over(landed))
    ag_dn, ag_uq, ag_ukv, ag_o1 = gather_end(both, s_mix1)
    w_dn = jnp.pad(rows(ag_dn), ((0, 0), (0, dn_pad - dn_w)))
    w_uq = jnp.pad(cols(ag_uq).reshape(Q_RANK, N_MLA, QK_DIM), ((0, 0), (0, 0), (0, QK_PAD - QK_DIM)))
    w_uq = w_uq.reshape(Q_RANK, N_MLA * QK_PAD)
    w_ukv = cols(ag_ukv)
    w_o1 = rows(ag_o1)
    x2, mlp0 = _mlp_fwd(x1, ln_mlp_g[0], ag_up0, rows(ag_dw0), "mlp0")

    h2 = _rms_fwd(x2, ln_mix_g[1], name="mix1_norm")
    down = _matmul(h2, w_dn, name="mix1_down")
    c_q = _rms_fwd(down, q_a_full, c0=0, width=Q_RANK, name="mix1_q_a_norm")
    c_kv = _rms_fwd(down, kv_a_full, c0=Q_RANK, width=KV_RANK, name="mix1_kv_a_norm")
    q_raw = _matmul(c_q, w_uq, name="mix1_uq")
    kv_raw = _matmul(c_kv, w_ukv, name="mix1_ukv")
    half = ROPE // 2
    inv_freq = ROPE_THETA ** (-jnp.arange(half, dtype=F32) / half)
    ang = pos.astype(F32)[:, None] * inv_freq
    table = lambda t: jnp.pad(jnp.concatenate([t, t], axis=1), ((0, 0), (0, LANES - ROPE)))
    cos_t, sin_t = table(jnp.cos(ang)), table(jnp.sin(ang))
    q_pad, k_pad, v1 = _mla_prep_fwd(q_raw, kv_raw, down, cos_t, sin_t, q_g_pad, k_g_pad, name="mla_prep_fwd")
    (o1, lse1), landed = _attn_fwd(q_pad, k_pad, v1, None, n_heads=N_MLA, dqk=QK_PAD, scale=QK_DIM ** -0.5,
                                   name="mla_fwd", comm=ex_mlp1)
    x3, both = _matmul(o1, w_o1, epilogue="res", res=x2, name="mix1_out", comm=hand_over(landed))
    ag_up1, ag_dw1 = gather_end(both, s_mlp1)
    w_up = [ag_up0, ag_up1]
    w_dw = [rows(ag_dw0), rows(ag_dw1)]
    x4, mlp1 = _mlp_fwd(x3, ln_mlp_g[1], w_up[1], w_dw[1], "mlp1")

    dx4, dx4_b, loss_local = _loss_head(x4, tgt, name="loss_head")
    loss = lax.psum(loss_local, ("x", "y", "c"))

    by_cols = lambda g: g.reshape(g.shape[0], N_CHIPS, -1).transpose(1, 0, 2)
    by_rows = lambda g: g.reshape(N_CHIPS, g.shape[0] // N_CHIPS, g.shape[1])
    halves = lambda g: g.reshape(N_CHIPS, 2, g.shape[1] // 2, g.shape[2])

    def scatter_of(grads, from_sibling, tags):
        parts = [_add_sibling(g, r, core, name=f"add_sibling_{t}") for g, r, t in zip(grads, from_sibling, tags)]
        return parts, _Exchange("scatter", parts)

    def sums_of(slots, parts, tags):
        slots = [_own_slot(s, lax.dynamic_index_in_dim(p, chip, 0, keepdims=False), chip)
                 for s, p in zip(slots, parts)]
        return [_add_chips(s, name=f"add_chips_{t}") for s, t in zip(slots, tags)]

    def shards_of(joined, mine):
        return [_own_slot(j, m, ci).reshape(2 * m.shape[0], m.shape[1]) for j, m in zip(joined, mine)]

    dx3, dx3_b, dg_mlp1, dw_up1, dw_dw1, _ = _mlp_bwd(dx4, dx4_b, x3, ln_mlp_g[1], w_up[1], w_dw[1], mlp1, "mlp1")
    tags_mlp1 = ["w_up1", "w_dw1"]
    g_mlp1 = [halves(dw_up1), halves(by_rows(dw_dw1))]

    dw_o1 = _matmul(o1, dx3_b, form="tn", name="mix1_dwo")
    do1 = _matmul(dx3_b, w_o1, form="nt", name="mix1_do")
    (dq_pad, dk_pad, dv1), from_sibling = _attn_bwd(
        q_pad, k_pad, v1, None, o1, lse1, do1, n_heads=N_MLA, dqk=QK_PAD, scale=QK_DIM ** -0.5, name="mla_bwd",
        comm=_Exchange("swap", g_mlp1))
    p_mlp1, _ = scatter_of(g_mlp1, from_sibling, tags_mlp1)
    dq_raw, dkv_raw, dpe, dg_q, dg_k = _mla_prep_bwd(dq_pad, dk_pad, dv1, q_raw, kv_raw, down, cos_t, sin_t,
                                                     q_g_pad, k_g_pad, name="mla_prep_bwd")
    dw_uq = _matmul(c_q, dq_raw, form="tn", name="mix1_dwuq")
    dc_q = _matmul(dq_raw, w_uq, form="nt", name="mix1_dcq")
    dw_ukv = _matmul(c_kv, dkv_raw, form="tn", name="mix1_dwukv")
    dc_kv = _matmul(dkv_raw, w_ukv, form="nt", name="mix1_dckv")
    d_cq, dg_qa = _rms_bwd(down, q_a_full, dc_q, c0=0, width=Q_RANK, name="mix1_q_a_dnorm")
    d_ckv, dg_kva = _rms_bwd(down, kv_a_full, dc_kv, c0=Q_RANK, width=KV_RANK, name="mix1_kv_a_dnorm")
    d_down = jnp.concatenate([d_cq, d_ckv, dpe], axis=1)
    dw_dn = _matmul(h2, d_down, form="tn", name="mix1_dwdown")
    dh2 = _matmul(d_down, w_dn, form="nt", name="mix1_dh")
    dx2, dx2_b, dg_mix1 = _rms_bwd(x2, ln_mix_g[1], dh2, res=dx3, bf16_copy=True, name="mix1_dnorm")
    g_uq = dw_uq.reshape(Q_RANK, N_MLA, QK_PAD)[:, :, :QK_DIM].reshape(Q_RANK, N_MLA * QK_DIM)
    tags_mix1 = ["w_dn", "w_uq", "w_ukv", "w_o1"]
    g_mix1 = [halves(by_rows(dw_dn[:, :dn_w])), halves(by_cols(g_uq)), halves(by_cols(dw_ukv)),
              halves(by_rows(dw_o1))]

    p_mix1, mine_mlp1 = [], []

    def scatter_mix1(landed):
        parts, ex = scatter_of(g_mix1, landed["dwdown"][1], tags_mix1)
        p_mix1.extend(parts)
        return ex

    def join_mlp1(landed):
        mine_mlp1.extend(sums_of(landed["dwdown"][0] + landed["du"], p_mlp1, tags_mlp1))
        return _Exchange("join", mine_mlp1)

    dx1, dx1_b, dg_mlp0, dw_up0, dw_dw0, landed = _mlp_bwd(
        dx2, dx2_b, x1, ln_mlp_g[0], w_up[0], w_dw[0], mlp0, "mlp0",
        comms={"dwdown": [_Exchange("scatter", p_mlp1[:1]), _Exchange("swap", g_mix1)],
               "du": _Exchange("scatter", p_mlp1[1:]), "dwup": scatter_mix1, "dh": join_mlp1})
    gs_up1, gs_dw1 = shards_of(landed["dh"], mine_mlp1)
    mine_mix1 = sums_of(landed["dwup"], p_mix1, tags_mix1)
    tags_mlp0 = ["w_up0", "w_dw0"]
    g_mlp0 = [halves(dw_up0), halves(by_rows(dw_dw0))]

    dw_o0, joined = _matmul(o0, dx1_b, form="tn", name="mix0_dwo", comm=_Exchange("join", mine_mix1))
    gs_dn, gs_uq, gs_ukv, gs_o1 = shards_of(joined, mine_mix1)
    g_o0 = [halves(by_rows(dw_o0))]
    do0 = _matmul(dx1_b, w_o0, form="nt", name="mix0_do")
    (dq_f, dk_f, dv_fx, dbias, drow), (fs_mlp0, fs_o0) = _attn_bwd(
        q_f, k_f, v_fx, neg_f, o_fx, lse0, do0, n_heads=N_FOX, dqk=HEAD, scale=HEAD ** -0.5, do_off=N_SB,
        name="fox_bwd", comm=[_Exchange("swap", g_mlp0), _Exchange("swap", g_o0)])
    p_mlp0, ex_a = scatter_of(g_mlp0, fs_mlp0, tags_mlp0)
    p_o0, ex_b = scatter_of(g_o0, fs_o0, ["w_o0"])
    (dq_sb, dk_sb, dv_sb), (sl_mlp0, sl_o0) = _sb_bwd(qkv_sb, do0, t_sb, do_off=0, name="sb_bwd", comm=[ex_a, ex_b])
    mine_mlp0 = sums_of(sl_mlp0, p_mlp0, tags_mlp0)
    mine_o0 = sums_of(sl_o0, p_o0, ["w_o0"])
    dq_fx, dg_fq = _rms_bwd(qk_fx, fox_q_g[0], dq_f, c0=0, width=N_FOX * HEAD, gw=HEAD, name="fox_q_dnorm")
    dk_fx, dg_fk = _rms_bwd(qk_fx, fox_k_g[0], dk_f, c0=N_FOX * HEAD, width=N_FOX * HEAD, gw=HEAD,
                            name="fox_k_dnorm")
    d_fcum = jnp.pad((jnp.max(drow, axis=-1) - dbias.reshape(N_FOX, S)).T, ((0, 0), (0, LANES - N_FOX)))
    dfl, db_f = _forget_bwd(fl, b_pad, d_fcum, name="forget_bwd")
    dproj = jnp.concatenate([dq_sb, dk_sb, dv_sb, dq_fx, dk_fx, dv_fx], axis=1).astype(BF16)
    dw_qkv, (j_mlp0, j_o0) = _matmul(h0, dproj, form="tn", name="mix0_dwqkv",
                                     comm=[_Exchange("join", mine_mlp0), _Exchange("join", mine_o0)])
    gs_up0, gs_dw0 = shards_of(j_mlp0, mine_mlp0)
    gs_o0, = shards_of(j_o0, mine_o0)
    dw_f = _matmul(h0, dfl, form="tn", name="mix0_dwf")
    g_in = [halves(by_cols(jnp.concatenate([dw_qkv, dw_f[:, :in_w - qkv_w]], axis=1)))]
    dh0 = _matmul(dfl, w_f, form="nt", name="mix0_dh_f")
    dh0, from_sibling = _matmul(dproj, w_qkv, form="nt", epilogue="res", res=dh0, name="mix0_dh",
                                comm=_Exchange("swap", g_in))
    p_in, ex_in_grads = scatter_of(g_in, from_sibling, ["w_in"])
    grad_x, dg_mix0 = _rms_bwd(xs, ln_mix_g[0], dh0, res=dx1, name="mix0_dnorm")
    gs_up = jnp.concatenate([gs_up0, gs_up1], axis=0)
    gs_dw = jnp.concatenate([gs_dw0, gs_dw1], axis=0)

    ln_rows = D // LANES
    small = jnp.concatenate([
        _rows(dg_mix0, ln_rows), _rows(dg_mix1, ln_rows), _rows(dg_mlp0, ln_rows), _rows(dg_mlp1, ln_rows),
        _rows(db_f, 8), _rows(dg_fq, 8), _rows(dg_fk, 8), _rows(dg_qa, 8), _rows(dg_kva, 8), _rows(dg_q, 8),
        _rows(dg_k, 8)], axis=0)
    small = _all_reduce_small(small, name="reduce_small")
    flat = lambda r0, nr, n: small[r0:r0 + nr].reshape(-1)[:n]
    r0 = 4 * ln_rows
    g_ln_mix = jnp.stack([flat(0, ln_rows, D), flat(ln_rows, ln_rows, D)])
    g_ln_mlp = jnp.stack([flat(2 * ln_rows, ln_rows, D), flat(3 * ln_rows, ln_rows, D)])
    g_b_f = flat(r0, 8, N_FOX)[None]
    g_fq, g_fk = flat(r0 + 8, 8, HEAD)[None], flat(r0 + 16, 8, HEAD)[None]
    g_qa = lax.dynamic_slice(flat(r0 + 24, 8, Q_RANK), (chip * LANES,), (LANES,))[None]
    g_kva = lax.dynamic_slice(flat(r0 + 32, 8, KV_RANK), (chip * LANES,), (LANES,))[None]
    g_q, g_k = flat(r0 + 40, 8, QK_DIM)[None], flat(r0 + 48, 8, QK_DIM)[None]

    def pack_small(ln_mix, ln_mlp, *rest):
        return jnp.concatenate([_rows(ln_mix, 2 * ln_rows), _rows(ln_mlp, 2 * ln_rows)] + [_rows(t, 8) for t in rest],
                               axis=0)

    def unpack_small(p):
        f = lambda r, nr, shape: p[r:r + nr].reshape(-1)[:int(np.prod(shape))].reshape(shape)
        shapes = [(1, N_FOX), (1, HEAD), (1, HEAD), (1, LANES), (1, LANES), (1, QK_DIM), (1, QK_DIM)]
        return (f(0, 2 * ln_rows, (2, D)), f(2 * ln_rows, 2 * ln_rows, (2, D)),
                *[f(r0 + 8 * i, 8, shp) for i, shp in enumerate(shapes)])

    small_out = _adamw(
        pack_small(ln_mix_g, ln_mlp_g, sf_b_f, fox_q_g, fox_k_g, mla_q_a_g, mla_kv_a_g, mla_q_g, mla_k_g)[None],
        pack_small(g_ln_mix, g_ln_mlp, g_b_f, g_fq, g_fk, g_qa, g_kva, g_q, g_k)[None],
        pack_small(m_ln_mix_g, m_ln_mlp_g, m_sf_b_f, m_fox_q_g, m_fox_k_g, m_mla_q_a_g, m_mla_kv_a_g, m_mla_q_g,
                   m_mla_k_g)[None],
        pack_small(v_ln_mix_g, v_ln_mlp_g, v_sf_b_f, v_fox_q_g, v_fox_k_g, v_mla_q_a_g, v_mla_kv_a_g, v_mla_q_g,
                   v_mla_k_g)[None], name="adamw_small")
    d_small, m_small, v_small = [unpack_small(p[0]) for p in small_out]

    def big(w, g, m, v, tag, comm=None):
        g = g.reshape(w.shape)
        out = _adamw(w, g, m, v, name=f"adamw_{tag}", comm=comm)
        (d, mn, vn), landed = out if comm is not None else (out, None)
        return (g, d, mn, vn) if comm is None else ((g, d, mn, vn), landed)

    r_up, slots = big(mlp_w_up, gs_up, m_mlp_w_up, v_mlp_w_up, "w_up", comm=ex_in_grads)
    mine_in = sums_of(slots, p_in, ["w_in"])
    gs_in, = shards_of(_run_exchange(_Exchange("join", mine_in), name="reduce_w_in_join"), mine_in)
    r_dw = big(mlp_w_down, gs_dw, m_mlp_w_down, v_mlp_w_down, "w_dw")
    r_in = big(sf_w_in, gs_in, m_sf_w_in, v_sf_w_in, "w_in")
    r_o0 = big(sf_w_o, gs_o0, m_sf_w_o, v_sf_w_o, "w_o0")
    r_dn = big(mla_w_down, gs_dn, m_mla_w_down, v_mla_w_down, "w_dn")
    r_uq = big(mla_w_uq, gs_uq, m_mla_w_uq, v_mla_w_uq, "w_uq")
    r_ukv = big(mla_w_ukv, gs_ukv, m_mla_w_ukv, v_mla_w_ukv, "w_ukv")
    r_o1 = big(mla_w_o, gs_o1, m_mla_w_o, v_mla_w_o, "w_o1")

    g_small = (g_ln_mix, g_ln_mlp, g_b_f, g_fq, g_fk, g_qa, g_kva, g_q, g_k)

    def ordered(k, sm):
        return (sm[0], sm[1], r_in[k], sm[2], sm[3], sm[4], r_o0[k], r_dn[k], sm[5], sm[6], r_uq[k], r_ukv[k],
                sm[7], sm[8], r_o1[k], r_up[k], r_dw[k])

    return (loss, grad_x[None], *ordered(0, g_small), *ordered(1, d_small), *ordered(2, m_small),
            *ordered(3, v_small))
```

```python
import functools

import numpy as np
import jax
import jax.numpy as jnp
from jax import lax
from jax.experimental import pallas as pl
from jax.experimental.pallas import tpu as pltpu

F32 = jnp.float32
BF16 = jnp.bfloat16
MESH = pl.DeviceIdType.MESH

EPS = 1e-6
HEAD = 128
N_SB = 8
N_FOX = 8
N_MLA = 16
Q_RANK = 512
KV_RANK = 512
NOPE = 128
ROPE = 64
QK_DIM = NOPE + ROPE
QK_PAD = 256
ROPE_THETA = 10000.0
N_CHIPS = 4

ADAM_LR = 0.001
ADAM_B1 = 0.9
ADAM_B2 = 0.999
ADAM_EPS = 1e-08
ADAM_WD = 0.01
ADAM_STEP = 10

VMEM_LIMIT = 56 * 1024 * 1024
LANES = 128
NEG = -1e30


def _cp(*sem):
    return pltpu.CompilerParams(dimension_semantics=sem, vmem_limit_bytes=VMEM_LIMIT)


def _pick(dim, target):
    if dim <= target:
        return dim
    t = (target // LANES) * LANES
    while t >= LANES:
        if dim % t == 0:
            return t
        t -= LANES
    raise ValueError(f"no tile for {dim}")


NT_DIMS = (((1,), (1,)), ((), ()))
TN_DIMS = (((0,), (0,)), ((), ()))


def _dot(a, b):
    return jnp.dot(a, b, preferred_element_type=F32)


def _dot_nt(a, b):
    return lax.dot_general(a, b, NT_DIMS, preferred_element_type=F32)


def _dot_tn(a, b):
    return lax.dot_general(a, b, TN_DIMS, preferred_element_type=F32)


def _matmul(a, b, *, name, form="nn", out_dtype=F32, n=None, b_n0=0, b_split=False,
            out_split=False, epilogue="plain", res=None, u=None, tm=1024, tn=1024, tk=2048, comm=None):
    if form == "tn":
        K, M = a.shape
    else:
        M, K = a.shape
    if b_split:
        if form == "nt":
            nb_full, kb_full = b.shape[1], b.shape[2] * N_CHIPS
        else:
            kb_full, nb_full = b.shape[1], b.shape[2] * N_CHIPS
    elif form == "nt":
        nb_full, kb_full = b.shape
    else:
        kb_full, nb_full = b.shape
    assert kb_full == K, (name, a.shape, b.shape)
    N = nb_full if n is None else n
    if a.dtype != BF16 or b.dtype != BF16:
        tk = max(tk // 2, LANES)
    tm, tn, tk = _pick(M, tm), _pick(N, tn), _pick(K, tk)
    if b_split:
        per_chip = (b.shape[2])
        if form == "nt":
            tk = _pick(per_chip, tk)
        else:
            tn = _pick(per_chip, tn)
    if out_split:
        tn = _pick(N // N_CHIPS, tn)
    assert b_n0 % tn == 0
    nb0 = b_n0 // tn
    nk = K // tk
    grid = (M // tm, N // tn, nk)

    if form == "tn":
        a_spec = pl.BlockSpec((tk, tm), lambda i, j, k: (k, i))
    else:
        a_spec = pl.BlockSpec((tm, tk), lambda i, j, k: (i, k))
    if b_split:
        if form == "nt":
            kc = b.shape[2] // tk
            b_spec = pl.BlockSpec((None, tn, tk), lambda i, j, k: (k // kc, j, k % kc))
        else:
            nc = b.shape[2] // tn
            b_spec = pl.BlockSpec((None, tk, tn), lambda i, j, k: (j // nc, k, j % nc))
    elif form == "nt":
        b_spec = pl.BlockSpec((tn, tk), lambda i, j, k: (j + nb0, k))
    else:
        b_spec = pl.BlockSpec((tk, tn), lambda i, j, k: (k, j + nb0))
    mn_spec = pl.BlockSpec((tm, tn), lambda i, j, k: (i, j))
    if out_split:
        oc = (N // N_CHIPS) // tn
        out_spec = pl.BlockSpec((None, tm, tn), lambda i, j, k: (j // oc, i, j % oc))
        out_shape = jax.ShapeDtypeStruct((N_CHIPS, M, N // N_CHIPS), out_dtype)
    else:
        out_spec = mn_spec
        out_shape = jax.ShapeDtypeStruct((M, N), out_dtype)

    in_specs = [a_spec, b_spec]
    operands = [a, b]
    out_specs = (out_spec,)
    out_shape = (out_shape,)
    if epilogue == "res":
        in_specs.append(mn_spec)
        operands.append(res)
    elif epilogue == "sqrelu_bwd":
        in_specs.append(mn_spec)
        operands.append(u)
    elif epilogue == "sqrelu":
        out_specs = (mn_spec, mn_spec)
        out_shape = (jax.ShapeDtypeStruct((M, N), F32), jax.ShapeDtypeStruct((M, N), BF16))

    def finish(refs, r):
        if epilogue == "plain":
            refs[2][...] = r.astype(out_dtype)
        elif epilogue == "res":
            refs[3][...] = (refs[2][...] + r).astype(out_dtype)
        elif epilogue == "sqrelu":
            refs[2][...] = r
            p = jnp.maximum(r, 0.0)
            refs[3][...] = (p * p).astype(BF16)
        else:
            refs[3][...] = (r * (2.0 * jnp.maximum(refs[2][...], 0.0))).astype(out_dtype)

    def body(*refs):
        at = refs[0][...].astype(BF16)
        bt = refs[1][...].astype(BF16)
        if form == "nn":
            part = _dot(at, bt)
        elif form == "nt":
            part = _dot_nt(at, bt)
        else:
            part = _dot_tn(at, bt)
        if nk == 1:
            finish(refs, part)
            return
        acc = refs[-1]
        k = pl.program_id(2)

        @pl.when(k == 0)
        def _():
            acc[...] = part

        @pl.when(jnp.logical_and(k > 0, k < nk - 1))
        def _():
            acc[...] += part

        @pl.when(k == nk - 1)
        def _():
            finish(refs, acc[...] + part)

    outs, comm_outs = _hosted_call(
        body, grid=grid, in_specs=in_specs, out_specs=out_specs, out_shape=out_shape,
        scratch_shapes=[] if nk == 1 else [pltpu.VMEM((tm, tn), F32)], operands=operands, name=name,
        sem=("parallel", "parallel", "arbitrary"), comm=comm)
    result = outs if epilogue == "sqrelu" else outs[0]
    return result if comm is None else (result, comm_outs)


def _rms_fwd(x, g, *, name, c0=0, width=None, gw=None, tr=256):
    R, ctot = x.shape
    C = ctot if width is None else width
    gw = C if gw is None else gw
    assert c0 % C == 0 and C % gw == 0
    tr = _pick(R, tr)
    cb = c0 // C
    ng = C // gw

    def body(x_ref, g_ref, o_ref):
        gv = g_ref[...]
        for gi in range(ng):
            cols = slice(gi * gw, (gi + 1) * gw)
            xs = x_ref[:, cols]
            ms = jnp.sum(xs * xs, axis=-1, keepdims=True) * (1.0 / gw)
            o_ref[:, cols] = ((xs * lax.rsqrt(ms + EPS)) * gv).astype(o_ref.dtype)

    return pl.pallas_call(
        body, out_shape=jax.ShapeDtypeStruct((R, C), BF16), grid=(R // tr,),
        in_specs=[pl.BlockSpec((tr, C), lambda i: (i, cb)), pl.BlockSpec((1, gw), lambda i: (0, 0))],
        out_specs=pl.BlockSpec((tr, C), lambda i: (i, 0)), name=name,
        compiler_params=_cp("parallel"))(x, g.reshape(1, gw).astype(F32))


def _rms_bwd(x, g, dy, *, name, res=None, c0=0, width=None, gw=None, tr=256, bf16_copy=False):
    bf16_copy = int(bf16_copy)
    R, ctot = x.shape
    C = ctot if width is None else width
    gw = C if gw is None else gw
    tr = _pick(R, tr)
    cb = c0 // C
    ng = C // gw
    nsteps = R // tr
    row_spec = pl.BlockSpec((tr, C), lambda i: (i, 0))
    in_specs = [pl.BlockSpec((tr, C), lambda i: (i, cb)), pl.BlockSpec((1, gw), lambda i: (0, 0)), row_spec]
    operands = [x, g.reshape(1, gw).astype(F32), dy]
    if res is not None:
        in_specs.append(row_spec)
        operands.append(res)

    def body(*refs):
        x_ref, g_ref, dy_ref = refs[:3]
        res_ref = refs[3] if res is not None else None
        dx_ref, dg_ref = refs[-3 - bf16_copy], refs[-2]
        acc = refs[-1]
        i = pl.program_id(0)

        @pl.when(i == 0)
        def _():
            acc[...] = jnp.zeros_like(acc)

        gv = g_ref[...]
        for gi in range(ng):
            cols = slice(gi * gw, (gi + 1) * gw)
            xs = x_ref[:, cols]
            dys = dy_ref[:, cols].astype(F32)
            rstd = lax.rsqrt(jnp.sum(xs * xs, axis=-1, keepdims=True) * (1.0 / gw) + EPS)
            xh = xs * rstd
            gdy = dys * gv
            m = jnp.sum(gdy * xh, axis=-1, keepdims=True) * (1.0 / gw)
            dx = rstd * (gdy - xh * m)
            if res_ref is not None:
                dx = dx + res_ref[:, cols]
            dx_ref[:, cols] = dx
            if bf16_copy:
                refs[-3][:, cols] = dx.astype(BF16)
            acc[...] += jnp.sum((dys * xh).reshape(tr // 8, 8, gw), axis=0)

        @pl.when(i == nsteps - 1)
        def _():
            dg_ref[...] = jnp.sum(acc[...], axis=0, keepdims=True)

    out_shape = [jax.ShapeDtypeStruct((R, C), F32)] + [jax.ShapeDtypeStruct((R, C), BF16)] * bf16_copy
    outs = pl.pallas_call(
        body, out_shape=tuple(out_shape + [jax.ShapeDtypeStruct((1, gw), F32)]),
        grid=(nsteps,), in_specs=in_specs,
        out_specs=tuple([row_spec] * len(out_shape) + [pl.BlockSpec((1, gw), lambda i: (0, 0))]),
        scratch_shapes=[pltpu.VMEM((8, gw), F32)], name=name,
        compiler_params=_cp("arbitrary"))(*operands)
    return (*outs[:-1], outs[-1][0])


def _split3(x):
    hi = x.astype(BF16)
    r1 = x - hi.astype(F32)
    mid = r1.astype(BF16)
    lo = (r1 - mid.astype(F32)).astype(BF16)
    return hi, mid, lo


def _log_sigmoid(z):
    return jnp.minimum(z, 0.0) - jnp.log(1.0 + jnp.exp(-jnp.abs(z)))


def _forget_fwd(fl, b, *, name, tb=512):
    S = fl.shape[0]
    tb = _pick(S, tb)

    def body(fl_ref, b_ref, f_ref, carry):
        i = pl.program_id(0)

        @pl.when(i == 0)
        def _():
            carry[...] = jnp.zeros_like(carry)

        lf = _log_sigmoid(fl_ref[...] + b_ref[...])
        r = lax.broadcasted_iota(jnp.int32, (tb, tb), 0)
        c = lax.broadcasted_iota(jnp.int32, (tb, tb), 1)
        tri = (c <= r).astype(BF16)
        hi, mid, lo = _split3(lf)
        cs = _dot(tri, hi) + _dot(tri, mid) + _dot(tri, lo)
        f_ref[...] = cs + carry[...]
        carry[...] += jnp.sum(lf, axis=0, keepdims=True)

    return pl.pallas_call(
        body, out_shape=jax.ShapeDtypeStruct((S, LANES), F32), grid=(S // tb,),
        in_specs=[pl.BlockSpec((tb, LANES), lambda i: (i, 0)), pl.BlockSpec((1, LANES), lambda i: (0, 0))],
        out_specs=pl.BlockSpec((tb, LANES), lambda i: (i, 0)),
        scratch_shapes=[pltpu.VMEM((1, LANES), F32)], name=name,
        compiler_params=_cp("arbitrary"))(fl, b)


def _forget_bwd(fl, b, dF, *, name, tb=512):
    S = fl.shape[0]
    tb = _pick(S, tb)
    nb = S // tb

    def body(fl_ref, b_ref, df_ref, dfl_ref, db_ref, carry, acc):
        i = pl.program_id(0)

        @pl.when(i == 0)
        def _():
            carry[...] = jnp.zeros_like(carry)
            acc[...] = jnp.zeros_like(acc)

        d = df_ref[...]
        r = lax.broadcasted_iota(jnp.int32, (tb, tb), 0)
        c = lax.broadcasted_iota(jnp.int32, (tb, tb), 1)
        tri = (c >= r).astype(BF16)
        hi, mid, lo = _split3(d)
        rc = _dot(tri, hi) + _dot(tri, mid) + _dot(tri, lo) + carry[...]
        z = fl_ref[...] + b_ref[...]
        dfl = rc * jnp.exp(_log_sigmoid(-z))
        dfl_ref[...] = dfl
        carry[...] += jnp.sum(d, axis=0, keepdims=True)
        acc[...] += jnp.sum(dfl, axis=0, keepdims=True)

        @pl.when(i == nb - 1)
        def _():
            db_ref[...] = acc[...]

    rev = lambda i: (nb - 1 - i, 0)
    dfl, db = pl.pallas_call(
        body, out_shape=(jax.ShapeDtypeStruct((S, LANES), F32), jax.ShapeDtypeStruct((1, LANES), F32)),
        grid=(nb,),
        in_specs=[pl.BlockSpec((tb, LANES), rev), pl.BlockSpec((1, LANES), lambda i: (0, 0)),
                  pl.BlockSpec((tb, LANES), rev)],
        out_specs=(pl.BlockSpec((tb, LANES), rev), pl.BlockSpec((1, LANES), lambda i: (0, 0))),
        scratch_shapes=[pltpu.VMEM((1, LANES), F32), pltpu.VMEM((1, LANES), F32)], name=name,
        compiler_params=_cp("arbitrary"))(fl, b, dF)
    return dfl, db[0]


def _tri(tk, rel):
    r = lax.broadcasted_iota(jnp.int32, (tk, tk), 0)
    c = lax.broadcasted_iota(jnp.int32, (tk, tk), 1)
    m = {"gt": r > c, "le": r <= c, "lt": r < c}[rel]
    return m.astype(BF16)


def _split2(x):
    hi = x.astype(BF16)
    return hi, (x - hi.astype(F32)).astype(BF16)


HEADS_PER_STEP = 2
CUM_CHUNK = 256


def _cum_cols(x, tri, suffix):
    ck = tri.shape[0]
    n = x.shape[1] // ck
    hi, lo = _split2(x)
    parts, sums = [], []
    for c in range(n):
        cs = slice(c * ck, (c + 1) * ck)
        parts.append(_dot(hi[:, cs], tri) + _dot(lo[:, cs], tri))
        sums.append(jnp.sum(x[:, cs], axis=1, keepdims=True))
    carry = None
    for c in (reversed(range(n)) if suffix else range(n)):
        if carry is not None:
            parts[c] = parts[c] + carry
        carry = sums[c] if carry is None else carry + sums[c]
    return (parts[0] if n == 1 else jnp.concatenate(parts, axis=1)), carry


def _diag_mask(tq, strict):
    r = lax.broadcasted_iota(jnp.int32, (tq, tq), 0)
    c = lax.broadcasted_iota(jnp.int32, (tq, tq), 1)
    return c < r if strict else c <= r


def _sb_fwd(qkv, *, name, n_heads=N_SB, q_off=0, k_off=N_SB, v_off=2 * N_SB, tq=512, hp=HEADS_PER_STEP,
            comm=None):
    S = qkv.shape[0]
    tq = _pick(S, tq)
    tk = tq
    scale = HEAD ** -0.5
    nq = S // tq
    assert n_heads % hp == 0 and q_off % hp == 0 and k_off % hp == 0 and v_off % hp == 0

    def body(q_ref, k_ref, v_ref, o_ref, t_ref, c_sc, acc_sc):
        qi = pl.program_id(1)
        c_sc[...] = jnp.zeros_like(c_sc)
        acc_sc[...] = jnp.zeros_like(acc_sc)
        gt = _tri(min(CUM_CHUNK, tk), "gt")

        def tile(hh, j, diag):
            cs = slice(hh * HEAD, (hh + 1) * HEAD)
            rows = pl.ds(pl.multiple_of(j * tk, tk), tk)
            z = _dot_nt(q_ref[:, cs], k_ref[rows, cs]) * scale
            sp = jnp.log(1.0 + jnp.exp(-jnp.abs(z)))
            la = jnp.minimum(z, 0.0) - sp
            lb = -jnp.maximum(z, 0.0) - sp
            if diag:
                strict = _diag_mask(tq, True)
                lb = jnp.where(strict, lb, 0.0)
            suffix, total = _cum_cols(lb, gt, suffix=True)
            w = jnp.exp(la + suffix + c_sc[hh])
            if diag:
                w = jnp.where(strict, w, 0.0)
            acc_sc[hh] += _dot(w.astype(BF16), v_ref[rows, cs])
            c_sc[hh] += total

        for hh in range(hp):
            tile(hh, qi, True)

        def step(it, carry):
            for hh in range(hp):
                tile(hh, qi - 1 - it, False)
            return carry

        lax.fori_loop(0, qi, step, 0)
        for hh in range(hp):
            o_ref[:, hh * HEAD:(hh + 1) * HEAD] = acc_sc[hh]
            t_ref[hh] = jnp.broadcast_to(c_sc[hh], (tq, LANES))

    w = hp * HEAD
    head_blk = lambda off: pl.BlockSpec((S, w), lambda h, i: (0, h + off // hp))
    outs, comm_outs = _hosted_call(
        body,
        out_shape=(jax.ShapeDtypeStruct((S, n_heads * HEAD), F32),
                   jax.ShapeDtypeStruct((n_heads, S, LANES), F32)),
        grid=(n_heads // hp, nq),
        in_specs=[pl.BlockSpec((tq, w), lambda h, i: (i, h + q_off // hp)), head_blk(k_off), head_blk(v_off)],
        out_specs=(pl.BlockSpec((tq, w), lambda h, i: (i, h)),
                   pl.BlockSpec((hp, tq, LANES), lambda h, i: (h, i, 0))),
        scratch_shapes=[pltpu.VMEM((hp, tq, 1), F32), pltpu.VMEM((hp, tq, HEAD), F32)], name=name,
        sem=("parallel", "arbitrary"), operands=(qkv, qkv, qkv), comm=comm)
    return outs if comm is None else (outs, comm_outs)


def _sb_bwd(qkv, do, tstat, *, name, n_heads=N_SB, q_off=0, k_off=N_SB, v_off=2 * N_SB, do_off=0, tq=512,
            hp=HEADS_PER_STEP, comm=None):
    S = qkv.shape[0]
    tq = _pick(S, tq)
    tk = tq
    scale = HEAD ** -0.5
    nq = S // tq
    assert n_heads % hp == 0 and q_off % hp == 0 and k_off % hp == 0 and v_off % hp == 0 and do_off % hp == 0

    def body(q_ref, k_ref, v_ref, do_ref, t_ref, dq_ref, dk_ref, dv_ref, p_sc, r_sc, dq_sc):
        qi = pl.program_id(1)

        @pl.when(qi == 0)
        def _():
            dk_ref[...] = jnp.zeros_like(dk_ref)
            dv_ref[...] = jnp.zeros_like(dv_ref)

        p_sc[...] = jnp.zeros_like(p_sc)
        r_sc[...] = jnp.zeros_like(r_sc)
        dq_sc[...] = jnp.zeros_like(dq_sc)
        le = _tri(min(CUM_CHUNK, tk), "le")
        lt = _tri(min(CUM_CHUNK, tk), "lt")

        def tile(hh, j, diag):
            cs = slice(hh * HEAD, (hh + 1) * HEAD)
            rows = pl.ds(pl.multiple_of(j * tk, tk), tk)
            q = q_ref[:, cs]
            do_b = do_ref[:, cs].astype(BF16)
            kb = k_ref[rows, cs]
            z = _dot_nt(q, kb) * scale
            sp = jnp.log(1.0 + jnp.exp(-jnp.abs(z)))
            la = jnp.minimum(z, 0.0) - sp
            lb = -jnp.maximum(z, 0.0) - sp
            if diag:
                strict = _diag_mask(tq, True)
                lb = jnp.where(strict, lb, 0.0)
            prefix, total_b = _cum_cols(lb, le, suffix=False)
            w = jnp.exp(la + t_ref[hh, :, 0:1] - (prefix + p_sc[hh]))
            if diag:
                w = jnp.where(strict, w, 0.0)
            r = w * _dot_nt(do_b, v_ref[rows, cs])
            rex, total_r = _cum_cols(r, lt, suffix=False)
            rex = rex + r_sc[hh]
            beta = jnp.exp(la)
            dz = r - beta * (r + rex)
            if diag:
                dz = jnp.where(strict, dz, 0.0)
            dzb = dz.astype(BF16)
            dq_sc[hh] += _dot(dzb, kb)
            dk_ref[rows, cs] += _dot_tn(dzb, q)
            dv_ref[rows, cs] += _dot_tn(w.astype(BF16), do_b)
            p_sc[hh] += total_b
            r_sc[hh] += total_r

        def step(j, carry):
            for hh in range(hp):
                tile(hh, j, False)
            return carry

        lax.fori_loop(0, qi, step, 0)
        for hh in range(hp):
            tile(hh, qi, True)
            dq_ref[:, hh * HEAD:(hh + 1) * HEAD] = dq_sc[hh] * scale

        @pl.when(qi == nq - 1)
        def _():
            dk_ref[...] = dk_ref[...] * scale

    w = hp * HEAD
    head_blk = lambda off: pl.BlockSpec((S, w), lambda h, i: (0, h + off // hp))
    out_head = pl.BlockSpec((S, w), lambda h, i: (0, h))
    out_sd = jax.ShapeDtypeStruct((S, n_heads * HEAD), F32)
    outs, comm_outs = _hosted_call(
        body, out_shape=(out_sd, out_sd, out_sd), grid=(n_heads // hp, nq),
        in_specs=[pl.BlockSpec((tq, w), lambda h, i: (i, h + q_off // hp)), head_blk(k_off), head_blk(v_off),
                  pl.BlockSpec((tq, w), lambda h, i: (i, h + do_off // hp)),
                  pl.BlockSpec((hp, tq, LANES), lambda h, i: (h, i, 0))],
        out_specs=(pl.BlockSpec((tq, w), lambda h, i: (i, h)), out_head, out_head),
        scratch_shapes=[pltpu.VMEM((hp, tq, 1), F32), pltpu.VMEM((hp, tq, 1), F32), pltpu.VMEM((hp, tq, HEAD), F32)],
        name=name, sem=("parallel", "arbitrary"), operands=(qkv, qkv, qkv, do, tstat), comm=comm)
    return outs if comm is None else (outs, comm_outs)


def _attn_fwd(q, k, v, bias, *, name, n_heads, dqk, scale, v_off=0, tq=1024, exact_p=False, hp=HEADS_PER_STEP,
              comm=None):
    S = q.shape[0]
    tq = _pick(S, tq)
    tk = tq
    nq = S // tq
    has_bias = bias is not None

    assert n_heads % hp == 0 and v_off % hp == 0

    def body(*refs):
        q_ref, k_ref, v_ref = refs[:3]
        b_ref = refs[3] if has_bias else None
        o_ref, lse_ref, m_sc, l_sc, acc_sc = refs[-5:]
        qi = pl.program_id(1)
        m_sc[...] = jnp.full_like(m_sc, NEG)
        l_sc[...] = jnp.zeros_like(l_sc)
        acc_sc[...] = jnp.zeros_like(acc_sc)

        def tile(hh, j, diag):
            rows = pl.ds(pl.multiple_of(j * tk, tk), tk)
            s = _dot_nt(q_ref[:, hh * dqk:(hh + 1) * dqk], k_ref[rows, hh * dqk:(hh + 1) * dqk]) * scale
            if has_bias:
                s = s + b_ref[hh, :, rows]
            if diag:
                s = jnp.where(_diag_mask(tq, False), s, NEG)
            m_old = m_sc[hh]
            m_new = jnp.maximum(m_old, jnp.max(s, axis=1, keepdims=True))
            alpha = jnp.exp(m_old - m_new)
            p = jnp.exp(s - m_new)
            l_sc[hh] = alpha * l_sc[hh] + jnp.sum(p, axis=1, keepdims=True)
            vb = v_ref[rows, hh * HEAD:(hh + 1) * HEAD]
            if exact_p:
                hi, lo = _split2(p)
                pv = _dot(hi, vb) + _dot(lo, vb)
            else:
                pv = _dot(p.astype(BF16), vb)
            acc_sc[hh] = alpha * acc_sc[hh] + pv
            m_sc[hh] = m_new

        def step(j, carry):
            for hh in range(hp):
                tile(hh, j, False)
            return carry

        lax.fori_loop(0, qi, step, 0)
        for hh in range(hp):
            tile(hh, qi, True)
            l = l_sc[hh]
            o_ref[:, hh * HEAD:(hh + 1) * HEAD] = acc_sc[hh] / l
            lse_ref[hh] = jnp.broadcast_to(m_sc[hh] + jnp.log(l), (tq, LANES))

    in_specs = [pl.BlockSpec((tq, hp * dqk), lambda h, i: (i, h)),
                pl.BlockSpec((S, hp * dqk), lambda h, i: (0, h)),
                pl.BlockSpec((S, hp * HEAD), lambda h, i: (0, h + v_off // hp))]
    operands = [q, k, v]
    if has_bias:
        in_specs.append(pl.BlockSpec((hp, 1, S), lambda h, i: (h, 0, 0)))
        operands.append(bias)
    outs, comm_outs = _hosted_call(
        body,
        out_shape=(jax.ShapeDtypeStruct((S, n_heads * HEAD), F32),
                   jax.ShapeDtypeStruct((n_heads, S, LANES), F32)),
        grid=(n_heads // hp, nq), in_specs=in_specs,
        out_specs=(pl.BlockSpec((tq, hp * HEAD), lambda h, i: (i, h)),
                   pl.BlockSpec((hp, tq, LANES), lambda h, i: (h, i, 0))),
        scratch_shapes=[pltpu.VMEM((hp, tq, 1), F32), pltpu.VMEM((hp, tq, 1), F32), pltpu.VMEM((hp, tq, HEAD), F32)],
        name=name, sem=("parallel", "arbitrary"), operands=operands, comm=comm)
    return outs if comm is None else (outs, comm_outs)


def _attn_bwd(q, k, v, bias, o, lse, do, *, name, n_heads, dqk, scale, v_off=0, do_off=0, tq=512,
              hp=HEADS_PER_STEP, comm=None):
    S = q.shape[0]
    tq = _pick(S, tq)
    tk = tq
    nq = S // tq
    has_bias = bias is not None
    assert n_heads % hp == 0 and v_off % hp == 0 and do_off % hp == 0

    def body(*refs):
        q_ref, k_ref, v_ref, o_ref, lse_ref, do_ref = refs[:6]
        b_ref = refs[6] if has_bias else None
        n_out = 5 if has_bias else 3
        outs = refs[-(n_out + 3):-3]
        dq_ref, dk_ref, dv_ref = outs[:3]
        db_ref, dr_ref = (outs[3], outs[4]) if has_bias else (None, None)
        dq_sc, rs_sc, delta_sc = refs[-3:]
        qi = pl.program_id(1)

        @pl.when(qi == 0)
        def _():
            dk_ref[...] = jnp.zeros_like(dk_ref)
            dv_ref[...] = jnp.zeros_like(dv_ref)
            if has_bias:
                db_ref[...] = jnp.zeros_like(db_ref)

        dq_sc[...] = jnp.zeros_like(dq_sc)
        rs_sc[...] = jnp.zeros_like(rs_sc)
        for hh in range(hp):
            vs = slice(hh * HEAD, (hh + 1) * HEAD)
            do_r = do_ref[:, vs].astype(BF16).astype(F32)
            delta_sc[hh] = jnp.sum(do_r * o_ref[:, vs], axis=1, keepdims=True)

        def tile(hh, j, diag):
            qs = slice(hh * dqk, (hh + 1) * dqk)
            vs = slice(hh * HEAD, (hh + 1) * HEAD)
            rows = pl.ds(pl.multiple_of(j * tk, tk), tk)
            qb = q_ref[:, qs]
            do_b = do_ref[:, vs].astype(BF16)
            delta = delta_sc[hh]
            kb = k_ref[rows, qs]
            s = _dot_nt(qb, kb) * scale
            if has_bias:
                s = s + b_ref[hh, :, rows]
            p = jnp.exp(s - lse_ref[hh, :, 0:1])
            if diag:
                p = jnp.where(_diag_mask(tq, False), p, 0.0)
            ds = p * (_dot_nt(do_b, v_ref[rows, vs]) - delta)
            dsb = (ds * scale).astype(BF16)
            dq_sc[hh] += _dot(dsb, kb)
            dk_ref[rows, qs] += _dot_tn(dsb, qb)
            dv_ref[rows, vs] += _dot_tn(p.astype(BF16), do_b)
            if has_bias:
                db_ref[hh, :, rows] += jnp.sum(ds, axis=0, keepdims=True)
                rs_sc[hh] += jnp.sum(ds, axis=1, keepdims=True)

        def step(j, carry):
            for hh in range(hp):
                tile(hh, j, False)
            return carry

        lax.fori_loop(0, qi, step, 0)
        for hh in range(hp):
            tile(hh, qi, True)
            dq_ref[:, hh * dqk:(hh + 1) * dqk] = dq_sc[hh]
            if has_bias:
                dr_ref[hh] = jnp.broadcast_to(rs_sc[hh], (tq, LANES))

    stat = pl.BlockSpec((hp, tq, LANES), lambda h, i: (h, i, 0))
    in_specs = [pl.BlockSpec((tq, hp * dqk), lambda h, i: (i, h)),
                pl.BlockSpec((S, hp * dqk), lambda h, i: (0, h)),
                pl.BlockSpec((S, hp * HEAD), lambda h, i: (0, h + v_off // hp)),
                pl.BlockSpec((tq, hp * HEAD), lambda h, i: (i, h)),
                stat,
                pl.BlockSpec((tq, hp * HEAD), lambda h, i: (i, h + do_off // hp))]
    operands = [q, k, v, o, lse, do]
    out_shape = [jax.ShapeDtypeStruct((S, n_heads * dqk), F32), jax.ShapeDtypeStruct((S, n_heads * dqk), F32),
                 jax.ShapeDtypeStruct((S, n_heads * HEAD), F32)]
    out_specs = [pl.BlockSpec((tq, hp * dqk), lambda h, i: (i, h)), pl.BlockSpec((S, hp * dqk), lambda h, i: (0, h)),
                 pl.BlockSpec((S, hp * HEAD), lambda h, i: (0, h))]
    if has_bias:
        in_specs.append(pl.BlockSpec((hp, 1, S), lambda h, i: (h, 0, 0)))
        operands.append(bias)
        out_shape.append(jax.ShapeDtypeStruct((n_heads, 1, S), F32))
        out_specs.append(pl.BlockSpec((hp, 1, S), lambda h, i: (h, 0, 0)))
        out_shape.append(jax.ShapeDtypeStruct((n_heads, S, LANES), F32))
        out_specs.append(stat)
    outs, comm_outs = _hosted_call(
        body, out_shape=tuple(out_shape), grid=(n_heads // hp, nq), in_specs=in_specs, out_specs=tuple(out_specs),
        scratch_shapes=[pltpu.VMEM((hp, tq, dqk), F32), pltpu.VMEM((hp, tq, 1), F32),
                        pltpu.VMEM((hp, tq, 1), F32)], name=name,
        sem=("parallel", "arbitrary"), operands=operands, comm=comm)
    return outs if comm is None else (outs, comm_outs)


def _rot_half(y):
    lane = lax.broadcasted_iota(jnp.int32, y.shape, 1)
    up = pltpu.roll(y, 96, 1)
    down = pltpu.roll(y, 32, 1)
    return jnp.where(lane < 32, -up, jnp.where(lane < 64, down, 0.0))


def _mla_prep_fwd(q_raw, kv_raw, down, cos, sin, q_g, k_g, *, name, ts=128):
    S = q_raw.shape[0]
    ts = _pick(S, ts)
    pe_blk = Q_RANK // LANES + KV_RANK // LANES

    def norm_rope(x0, x1, g0, g1, c, s):
        ms = (jnp.sum(x0 * x0, axis=-1, keepdims=True) + jnp.sum(x1 * x1, axis=-1, keepdims=True)) * (1.0 / QK_DIM)
        rstd = lax.rsqrt(ms + EPS)
        y0 = (x0 * rstd) * g0
        y1 = (x1 * rstd) * g1
        return y0, y1 * c + _rot_half(y1) * s

    def body(q_ref, kv_ref, pe_ref, cos_ref, sin_ref, qg_ref, kg_ref, qo_ref, ko_ref, vo_ref):
        c, s = cos_ref[...], sin_ref[...]
        pe = pe_ref[...]
        qg0, qg1 = qg_ref[:, :NOPE], qg_ref[:, NOPE:]
        kg0, kg1 = kg_ref[:, :NOPE], kg_ref[:, NOPE:]
        for h in range(N_MLA):
            b = h * QK_PAD
            y0, y1 = norm_rope(q_ref[:, b:b + NOPE], q_ref[:, b + NOPE:b + QK_PAD], qg0, qg1, c, s)
            qo_ref[:, b:b + NOPE] = y0.astype(BF16)
            qo_ref[:, b + NOPE:b + QK_PAD] = y1.astype(BF16)
            y0, y1 = norm_rope(kv_ref[:, b:b + NOPE], pe, kg0, kg1, c, s)
            ko_ref[:, b:b + NOPE] = y0.astype(BF16)
            ko_ref[:, b + NOPE:b + QK_PAD] = y1.astype(BF16)
            vo_ref[:, h * HEAD:(h + 1) * HEAD] = kv_ref[:, b + NOPE:b + QK_PAD].astype(BF16)

    wide = pl.BlockSpec((ts, N_MLA * QK_PAD), lambda i: (i, 0))
    lane_blk = pl.BlockSpec((ts, LANES), lambda i: (i, 0))
    gain = pl.BlockSpec((1, QK_PAD), lambda i: (0, 0))
    return pl.pallas_call(
        body,
        out_shape=(jax.ShapeDtypeStruct((S, N_MLA * QK_PAD), BF16), jax.ShapeDtypeStruct((S, N_MLA * QK_PAD), BF16),
                   jax.ShapeDtypeStruct((S, N_MLA * HEAD), BF16)),
        grid=(S // ts,),
        in_specs=[wide, wide, pl.BlockSpec((ts, LANES), lambda i: (i, pe_blk)), lane_blk, lane_blk, gain, gain],
        out_specs=(wide, wide, pl.BlockSpec((ts, N_MLA * HEAD), lambda i: (i, 0))), name=name,
        compiler_params=_cp("parallel"))(q_raw, kv_raw, down, cos, sin, q_g, k_g)


def _mla_prep_bwd(dq, dk, dv, q_raw, kv_raw, down, cos, sin, q_g, k_g, *, name, ts=128, comm=None):
    S = q_raw.shape[0]
    ts = _pick(S, ts)
    nsteps = S // ts
    pe_blk = Q_RANK // LANES + KV_RANK // LANES

    def back(x0, x1, g0, g1, c, s, d0, d1r):
        d1 = d1r * c - _rot_half(d1r * s)
        ms = (jnp.sum(x0 * x0, axis=-1, keepdims=True) + jnp.sum(x1 * x1, axis=-1, keepdims=True)) * (1.0 / QK_DIM)
        rstd = lax.rsqrt(ms + EPS)
        h0, h1 = x0 * rstd, x1 * rstd
        e0, e1 = d0 * g0, d1 * g1
        m = (jnp.sum(e0 * h0, axis=-1, keepdims=True) + jnp.sum(e1 * h1, axis=-1, keepdims=True)) * (1.0 / QK_DIM)
        return rstd * (e0 - h0 * m), rstd * (e1 - h1 * m), d0 * h0, d1 * h1

    def fold(a):
        return jnp.sum(a.reshape(ts // 8, 8, a.shape[-1]), axis=0)

    def body(dq_ref, dk_ref, dv_ref, q_ref, kv_ref, pe_ref, cos_ref, sin_ref, qg_ref, kg_ref,
             dqr_ref, dkv_ref, dpe_ref, dqg_ref, dkg_ref, gq_sc, gk_sc):
        i = pl.program_id(0)

        @pl.when(i == 0)
        def _():
            gq_sc[...] = jnp.zeros_like(gq_sc)
            gk_sc[...] = jnp.zeros_like(gk_sc)

        c, s = cos_ref[...], sin_ref[...]
        pe = pe_ref[...]
        qg0, qg1 = qg_ref[:, :NOPE], qg_ref[:, NOPE:]
        kg0, kg1 = kg_ref[:, :NOPE], kg_ref[:, NOPE:]
        dpe = jnp.zeros((ts, LANES), F32)
        for h in range(N_MLA):
            b = h * QK_PAD
            dx0, dx1, a0, a1 = back(q_ref[:, b:b + NOPE], q_ref[:, b + NOPE:b + QK_PAD], qg0, qg1, c, s,
                                    dq_ref[:, b:b + NOPE], dq_ref[:, b + NOPE:b + QK_PAD])
            dqr_ref[:, b:b + NOPE] = dx0.astype(BF16)
            dqr_ref[:, b + NOPE:b + QK_PAD] = dx1.astype(BF16)
            gq_sc[:, :NOPE] += fold(a0)
            gq_sc[:, NOPE:] += fold(a1)
            dx0, dx1, a0, a1 = back(kv_ref[:, b:b + NOPE], pe, kg0, kg1, c, s,
                                    dk_ref[:, b:b + NOPE], dk_ref[:, b + NOPE:b + QK_PAD])
            dkv_ref[:, b:b + NOPE] = dx0.astype(BF16)
            dkv_ref[:, b + NOPE:b + QK_PAD] = dv_ref[:, h * HEAD:(h + 1) * HEAD].astype(BF16)
            dpe = dpe + dx1
            gk_sc[:, :NOPE] += fold(a0)
            gk_sc[:, NOPE:] += fold(a1)
        dpe_ref[...] = dpe

        @pl.when(i == nsteps - 1)
        def _():
            dqg_ref[...] = jnp.sum(gq_sc[...], axis=0, keepdims=True)
            dkg_ref[...] = jnp.sum(gk_sc[...], axis=0, keepdims=True)

    wide = pl.BlockSpec((ts, N_MLA * QK_PAD), lambda i: (i, 0))
    lane_blk = pl.BlockSpec((ts, LANES), lambda i: (i, 0))
    gain = pl.BlockSpec((1, QK_PAD), lambda i: (0, 0))
    outs, comm_outs = _hosted_call(
        body,
        out_shape=(jax.ShapeDtypeStruct((S, N_MLA * QK_PAD), BF16), jax.ShapeDtypeStruct((S, N_MLA * QK_PAD), BF16),
                   jax.ShapeDtypeStruct((S, LANES), F32), jax.ShapeDtypeStruct((1, QK_PAD), F32),
                   jax.ShapeDtypeStruct((1, QK_PAD), F32)),
        grid=(nsteps,),
        in_specs=[wide, wide, pl.BlockSpec((ts, N_MLA * HEAD), lambda i: (i, 0)), wide, wide,
                  pl.BlockSpec((ts, LANES), lambda i: (i, pe_blk)), lane_blk, lane_blk, gain, gain],
        out_specs=(wide, wide, lane_blk, gain, gain),
        scratch_shapes=[pltpu.VMEM((8, QK_PAD), F32), pltpu.VMEM((8, QK_PAD), F32)], name=name,
        sem=("arbitrary",), operands=(dq, dk, dv, q_raw, kv_raw, down, cos, sin, q_g, k_g), comm=comm)
    res = (outs[0], outs[1], outs[2], outs[3][0], outs[4][0])
    return res if comm is None else (res, comm_outs)


def _loss_head(y, target, *, name, tr=256):
    R, C = y.shape
    tr = _pick(R, tr)
    nsteps = R // tr

    def body(y_ref, t_ref, dy_ref, dyb_ref, loss_ref, acc):
        i = pl.program_id(0)

        @pl.when(i == 0)
        def _():
            acc[...] = jnp.zeros_like(acc)

        err = y_ref[...] - t_ref[...]
        dy = err * (1.0 / C)
        dy_ref[...] = dy
        dyb_ref[...] = dy.astype(BF16)
        acc[...] += jnp.sum((err * err).reshape(tr // 8, 8, C), axis=0)

        @pl.when(i == nsteps - 1)
        def _():
            tot = jnp.sum(jnp.sum(acc[...], axis=0, keepdims=True), axis=1, keepdims=True)
            loss_ref[...] = jnp.broadcast_to(tot * (0.5 / C), (8, LANES))

    blk = pl.BlockSpec((tr, C), lambda i: (i, 0))
    dy, dy_b, loss = pl.pallas_call(
        body, out_shape=(jax.ShapeDtypeStruct((R, C), F32), jax.ShapeDtypeStruct((R, C), BF16),
                         jax.ShapeDtypeStruct((8, LANES), F32)),
        grid=(nsteps,), in_specs=[blk, blk], out_specs=(blk, blk, pl.BlockSpec((8, LANES), lambda i: (0, 0))),
        scratch_shapes=[pltpu.VMEM((8, C), F32)], name=name, compiler_params=_cp("arbitrary"))(y, target)
    return dy, dy_b, loss[0, 0]


def _adamw(w, g, m, v, *, name, block_bytes=1 << 20, comm=None):
    L, R, C = w.shape
    tr = max(8, min(R, (block_bytes // (4 * C)) // 8 * 8))
    while R % tr:
        tr -= 8
    if tr <= 0:
        tr = R
    c1 = 1.0 / (1.0 - ADAM_B1 ** ADAM_STEP)
    c2 = 1.0 / (1.0 - ADAM_B2 ** ADAM_STEP)

    def body(w_ref, g_ref, m_ref, v_ref, d_ref, mo_ref, vo_ref):
        gv = g_ref[...]
        mn = ADAM_B1 * m_ref[...] + (1.0 - ADAM_B1) * gv
        vn = ADAM_B2 * v_ref[...] + (1.0 - ADAM_B2) * (gv * gv)
        d_ref[...] = -ADAM_LR * ((mn * c1) / (jnp.sqrt(vn * c2) + ADAM_EPS) + ADAM_WD * w_ref[...])
        mo_ref[...] = mn
        vo_ref[...] = vn

    blk = pl.BlockSpec((None, tr, C), lambda l, i: (l, i, 0))
    sd = jax.ShapeDtypeStruct((L, R, C), F32)
    outs, comm_outs = _hosted_call(
        body, out_shape=(sd, sd, sd), grid=(L, R // tr), in_specs=[blk] * 4, out_specs=(blk,) * 3, scratch_shapes=[],
        name=name, sem=("parallel", "parallel"), operands=(w, g, m, v), comm=comm)
    return outs if comm is None else (outs, comm_outs)


def _row_tile(r, c, itemsize=4, block_bytes=1 << 20):
    tr = max(16, min(r, (block_bytes // (itemsize * c)) // 16 * 16))
    while r % tr:
        tr -= 16
    return tr if tr > 0 else r


def _add_sibling(g, recv, core, *, name):
    nch, _, r, c = g.shape
    tr = _row_tile(r, c)

    def body(core_ref, g_ref, r_ref, o_ref):
        o_ref[...] = (g_ref[...] + r_ref[...]).astype(BF16)

    grid_spec = pltpu.PrefetchScalarGridSpec(
        num_scalar_prefetch=1, grid=(nch, r // tr),
        in_specs=[pl.BlockSpec((None, None, tr, c), lambda j, i, cr: (j, cr[0], i, 0)),
                  pl.BlockSpec((None, tr, c), lambda j, i, cr: (j, i, 0))],
        out_specs=pl.BlockSpec((None, tr, c), lambda j, i, cr: (j, i, 0)))
    return pl.pallas_call(
        body, out_shape=jax.ShapeDtypeStruct((nch, r, c), BF16), grid_spec=grid_spec, name=name,
        compiler_params=_cp("parallel", "parallel"))(core, g, recv)


def _add_chips(slots, *, name):
    nch, r, c = slots.shape
    tr = _row_tile(r, c)

    def body(s_ref, o_ref):
        acc = s_ref[0].astype(F32)
        for j in range(1, nch):
            acc = acc + s_ref[j].astype(F32)
        o_ref[...] = acc

    return pl.pallas_call(
        body, out_shape=jax.ShapeDtypeStruct((r, c), F32), grid=(r // tr,),
        in_specs=[pl.BlockSpec((nch, tr, c), lambda i: (0, i, 0))],
        out_specs=pl.BlockSpec((tr, c), lambda i: (i, 0)), name=name, compiler_params=_cp("parallel"))(slots)


def _place():
    x, y, c = lax.axis_index("x"), lax.axis_index("y"), lax.axis_index("c")
    others = [(1 - x, y), (x, 1 - y), (1 - x, 1 - y)]
    return x, y, c, 2 * x + y, others


ANY = pl.BlockSpec(memory_space=pl.ANY)


class _Exchange:
    def __init__(self, kind, arrays):
        self.kind, self.ins = kind, list(arrays)
        self.n_peers = 1 if kind in ("swap", "join") else 3
        self.aliased = kind == "forward"
        n = len(self.ins) * self.n_peers
        shp = {"gather": lambda a: (N_CHIPS,) + a.shape, "scatter": lambda a: a.shape, "forward": lambda a: a.shape,
               "swap": lambda a: (a.shape[0],) + a.shape[2:], "join": lambda a: (2,) + a.shape}[kind]
        self.out_shapes = [jax.ShapeDtypeStruct(shp(a), a.dtype) for a in self.ins]
        self.sems = [pltpu.SemaphoreType.DMA((n,)), pltpu.SemaphoreType.DMA((n,))]

    def _copies(self, ins, outs, sems):
        send, recv = sems
        x, y, c, me, others = _place()
        over_ici = self.kind in ("gather", "scatter")
        peers = [(ox, oy, c) for ox, oy in others] if over_ici else [(x, y, 1 - c)] * self.n_peers
        for a in range(len(self.ins)):
            for k, to in enumerate(peers):
                peer = 2 * others[k][0] + others[k][1]
                if self.kind == "gather":
                    hr = self.ins[a].shape[0] // 2
                    rows = pl.ds(c * hr, hr)
                    src, dst, land = ins[a].at[rows, :], outs[a].at[me, rows, :], outs[a].at[peer, rows, :]
                elif self.kind == "forward":
                    hr = self.ins[a].shape[1] // 2
                    mine, theirs = pl.ds(c * hr, hr), pl.ds((1 - c) * hr, hr)
                    src, dst, land = ins[a].at[peer, mine, :], outs[a].at[peer, mine, :], outs[a].at[peer, theirs, :]
                elif self.kind == "scatter":
                    src, dst, land = ins[a].at[peer], outs[a].at[me], outs[a].at[peer]
                elif self.kind == "swap":
                    src, dst, land = ins[a].at[:, 1 - c], outs[a], outs[a]
                else:
                    src, dst, land = ins[a], outs[a].at[c], outs[a].at[1 - c]
                i = self.n_peers * a + k
                mk = lambda s, d: pltpu.make_async_remote_copy(
                    src_ref=s, dst_ref=d, send_sem=send.at[i], recv_sem=recv.at[i], device_id=to,
                    device_id_type=MESH)
                yield mk(src, dst), mk(land, land)

    def alias_pairs(self):
        return [(i, i) for i in range(len(self.ins))] if self.aliased else []

    def start(self, ins, outs, sems):
        for cp, _ in self._copies(ins, outs, sems):
            cp.start()

    def finish(self, ins, outs, sems):
        pairs = list(self._copies(ins, outs, sems))
        for _, landing in pairs:
            landing.wait_recv()
        for cp, _ in pairs:
            cp.wait_send()


class _Several:
    def __init__(self, parts):
        self.parts = list(parts)
        self.ins = [a for p in self.parts for a in p.ins]
        self.out_shapes = [s for p in self.parts for s in p.out_shapes]
        self.sems = [s for p in self.parts for s in p.sems]

    def split(self, ins, outs, sems=None):
        i = 0
        for k, p in enumerate(self.parts):
            n = len(p.ins)
            yield p, ins[i:i + n], outs[i:i + n], None if sems is None else sems[2 * k:2 * k + 2]
            i += n

    def alias_pairs(self):
        pairs, i = [], 0
        for p in self.parts:
            pairs += [(i + a, i + b) for a, b in p.alias_pairs()]
            i += len(p.ins)
        return pairs

    def start(self, ins, outs, sems):
        for p, a, b, s in self.split(ins, outs, sems):
            p.start(a, b, s)

    def finish(self, ins, outs, sems):
        for p, a, b, s in self.split(ins, outs, sems):
            p.finish(a, b, s)


def _hosted_call(body, *, grid, in_specs, out_specs, out_shape, scratch_shapes, operands, name, sem, comm=None):
    out_specs, out_shape = tuple(out_specs), tuple(out_shape)
    if isinstance(comm, (list, tuple)):
        several = _Several(comm)
        outs, comm_outs = _hosted_call(body, grid=grid, in_specs=in_specs, out_specs=out_specs, out_shape=out_shape,
                                       scratch_shapes=scratch_shapes, operands=operands, name=name, sem=sem,
                                       comm=several)
        return outs, [tuple(o) for _, _, o, _ in several.split(several.ins, comm_outs)]
    if comm is None:
        res = pl.pallas_call(body, out_shape=out_shape, grid=grid, in_specs=list(in_specs), out_specs=out_specs,
                             scratch_shapes=list(scratch_shapes), name=name, compiler_params=_cp(*sem))(*operands)
        return tuple(res), ()
    n_in, n_out, n_sc = len(in_specs), len(out_specs), len(scratch_shapes)
    ci, co = len(comm.ins), len(comm.out_shapes)

    def wrapped(*refs):
        ins, c_ins = refs[:n_in], refs[n_in:n_in + ci]
        outs = refs[n_in + ci:n_in + ci + n_out]
        c_outs = refs[n_in + ci + n_out:n_in + ci + n_out + co]
        scratch = refs[n_in + ci + n_out + co:n_in + ci + n_out + co + n_sc]
        sems = refs[n_in + ci + n_out + co + n_sc:]
        ids = [pl.program_id(d) for d in range(len(grid))]
        first = functools.reduce(jnp.logical_and, [i == 0 for i in ids])
        last = functools.reduce(jnp.logical_and, [i == g - 1 for i, g in zip(ids, grid)])

        @pl.when(first)
        def _():
            comm.start(c_ins, c_outs, sems)

        body(*ins, *outs, *scratch)

        @pl.when(last)
        def _():
            comm.finish(c_ins, c_outs, sems)

    res = pl.pallas_call(
        wrapped, out_shape=out_shape + tuple(comm.out_shapes), grid=grid, in_specs=list(in_specs) + [ANY] * ci,
        out_specs=out_specs + tuple([ANY] * co), scratch_shapes=list(scratch_shapes) + comm.sems, name=name,
        input_output_aliases={n_in + i: n_out + o for i, o in comm.alias_pairs()},
        compiler_params=pltpu.CompilerParams(dimension_semantics=("arbitrary",) * len(grid),
                                             vmem_limit_bytes=VMEM_LIMIT, has_side_effects=True),
    )(*operands, *comm.ins)
    return tuple(res[:n_out]), tuple(res[n_out:])


def _run_exchange(comm, *, name):
    ci = len(comm.ins)

    def body(*refs):
        ins, outs, sems = refs[:ci], refs[ci:2 * ci], refs[2 * ci:]
        comm.start(ins, outs, sems)
        comm.finish(ins, outs, sems)

    return pl.pallas_call(
        body, out_shape=tuple(comm.out_shapes), in_specs=[ANY] * ci, out_specs=tuple([ANY] * ci),
        scratch_shapes=comm.sems, name=name, input_output_aliases=dict(comm.alias_pairs()),
        compiler_params=pltpu.CompilerParams(has_side_effects=True))(*comm.ins)


def _own_slot(buf, piece, idx):
    return lax.dynamic_update_slice(buf, piece[None], (idx,) + (0,) * piece.ndim)


def _all_reduce_small(v, *, name):
    R = v.shape[0]

    flips = [(dx, dy, dc) for dx in range(2) for dy in range(2) for dc in range(2) if dx or dy or dc]

    def body(v_ref, o_ref, slots, send, recv):
        x, y, c, me, others = _place()
        mine = 2 * me + c
        slots[mine] = v_ref[...]

        def copy(k, slot):
            dx, dy, dc = flips[k]
            peer = (x + dx - 2 * x * dx, y + dy - 2 * y * dy, c + dc - 2 * c * dc)
            peer_slot = 4 * peer[0] + 2 * peer[1] + peer[2]
            return pltpu.make_async_remote_copy(
                src_ref=v_ref, dst_ref=slots.at[mine if slot == "mine" else peer_slot], send_sem=send.at[k],
                recv_sem=recv.at[k], device_id=peer, device_id_type=MESH)

        for k in range(7):
            copy(k, "mine").start()
        for k in range(7):
            copy(k, "peer").wait_recv()
        for k in range(7):
            copy(k, "mine").wait_send()
        acc = slots[0]
        for j in range(1, 8):
            acc = acc + slots[j]
        o_ref[...] = acc

    vm = pl.BlockSpec(memory_space=pltpu.VMEM)
    return pl.pallas_call(
        body, out_shape=jax.ShapeDtypeStruct(v.shape, F32), in_specs=[vm], out_specs=vm,
        scratch_shapes=[pltpu.VMEM((8, R, LANES), F32), pltpu.SemaphoreType.DMA((7,)),
                        pltpu.SemaphoreType.DMA((7,))],
        name=name, compiler_params=pltpu.CompilerParams(has_side_effects=True))(v)


def _rows(v, n_rows):
    v = v.reshape(-1).astype(F32)
    return jnp.pad(v, (0, n_rows * LANES - v.shape[0])).reshape(n_rows, LANES)


def _mlp_fwd(x_in, g, w_up, w_down, tag):
    h = _rms_fwd(x_in, g, name=f"{tag}_norm")
    u, a = _matmul(h, w_up, b_split=True, epilogue="sqrelu", name=f"{tag}_up")
    x_out = _matmul(a, w_down, epilogue="res", res=x_in, name=f"{tag}_down")
    return x_out, (h, u, a)


def _mlp_bwd(dy, dy_b, x_in, g, w_up, w_down, saved, tag, comms=None):
    h, u, a = saved
    comms = comms or {}
    landed = {}

    def mm(key, *args, **kw):
        comm = comms.get(key)
        if callable(comm):
            comm = comm(landed)
        out = _matmul(*args, name=f"{tag}_{key}", comm=comm, **kw)
        if comm is not None:
            out, landed[key] = out
        return out

    dw_down = mm("dwdown", a, dy_b, form="tn")
    du = mm("du", dy_b, w_down, form="nt", epilogue="sqrelu_bwd", u=u, out_dtype=BF16)
    dw_up = mm("dwup", h, du, form="tn", out_split=True)
    dh = mm("dh", du, w_up, form="nt", b_split=True)
    dx, dx_b, dg = _rms_bwd(x_in, g, dh, res=dy, bf16_copy=True, name=f"{tag}_dnorm")
    return dx, dx_b, dg, dw_up, dw_down, landed


def kernel(x, positions, ln_mix_g, ln_mlp_g, sf_w_in, sf_b_f, fox_q_g, fox_k_g, sf_w_o, mla_w_down, mla_q_a_g, mla_kv_a_g, mla_w_uq, mla_w_ukv, mla_q_g, mla_k_g, mla_w_o, mlp_w_up, mlp_w_down, loss_target, m_ln_mix_g, m_ln_mlp_g, m_sf_w_in, m_sf_b_f, m_fox_q_g, m_fox_k_g, m_sf_w_o, m_mla_w_down, m_mla_q_a_g, m_mla_kv_a_g, m_mla_w_uq, m_mla_w_ukv, m_mla_q_g, m_mla_k_g, m_mla_w_o, m_mlp_w_up, m_mlp_w_down, v_ln_mix_g, v_ln_mlp_g, v_sf_w_in, v_sf_b_f, v_fox_q_g, v_fox_k_g, v_sf_w_o, v_mla_w_down, v_mla_q_a_g, v_mla_kv_a_g, v_mla_w_uq, v_mla_w_ukv, v_mla_q_g, v_mla_k_g, v_mla_w_o, v_mlp_w_up, v_mlp_w_down):
    S, D = x.shape[1], x.shape[2]
    xs, tgt, pos = x[0], loss_target[0], positions[0]
    xi, yi, ci = lax.axis_index("x"), lax.axis_index("y"), lax.axis_index("c")
    chip = 2 * xi + yi
    core = ci.astype(jnp.int32).reshape(1)
    d_ff = mlp_w_up.shape[2] * N_CHIPS
    in_w = sf_w_in.shape[2] * N_CHIPS
    qkv_w = 3 * N_SB * HEAD + 3 * N_FOX * HEAD
    dn_w = mla_w_down.shape[2]
    dn_pad = Q_RANK + KV_RANK + LANES

    def gather_begin(ws):
        shards = [w.astype(BF16) for w in ws]
        return shards, _Exchange("gather", shards)

    def hand_over(landed):
        return _Exchange("forward", list(landed))

    def gather_end(both, shards):
        return [_own_slot(ag, s, chip) for ag, s in zip(both, shards)]

    cols = lambda ag: ag.transpose(1, 0, 2).reshape(ag.shape[1], -1)
    rows = lambda ag: ag.reshape(-1, ag.shape[2])
    s_mix0, ex_mix0 = gather_begin([sf_w_in[0], sf_w_o[0]])
    landed = _run_exchange(ex_mix0, name="gather_mix0")
    ag_in, ag_o0 = gather_end(_run_exchange(hand_over(landed), name="gather_mix0_sibling"), s_mix0)
    w_in_full = cols(ag_in)
    w_qkv = w_in_full[:, :qkv_w]
    w_f = jnp.pad(w_in_full[:, qkv_w:], ((0, 0), (0, LANES - (in_w - qkv_w))))
    w_o0 = rows(ag_o0)
    s_mlp0, ex_mlp0 = gather_begin([mlp_w_up[0], mlp_w_down[0]])
    s_mix1, ex_mix1 = gather_begin([mla_w_down[0], mla_w_uq[0], mla_w_ukv[0], mla_w_o[0]])
    s_mlp1, ex_mlp1 = gather_begin([mlp_w_up[1], mlp_w_down[1]])

    gain_blk = jnp.concatenate([mla_q_a_g, mla_kv_a_g], axis=0) * (ci == 0).astype(F32)
    placed = jnp.zeros((2, N_CHIPS, LANES), F32)
    placed = lax.dynamic_update_slice(placed, gain_blk[:, None, :], (0, chip, 0))
    gains = _all_reduce_small(placed.reshape(2 * N_CHIPS, LANES), name="gather_gains")
    q_a_full = gains[:N_CHIPS].reshape(Q_RANK)
    kv_a_full = gains[N_CHIPS:].reshape(KV_RANK)

    pad_gain = lambda g: jnp.pad(g.reshape(1, QK_DIM), ((0, 0), (0, QK_PAD - QK_DIM)))
    q_g_pad, k_g_pad = pad_gain(mla_q_g), pad_gain(mla_k_g)
    b_pad = _rows(sf_b_f, 1)

    h0 = _rms_fwd(xs, ln_mix_g[0], name="mix0_norm")
    qkv_sb = _matmul(h0, w_qkv, n=3 * N_SB * HEAD, b_n0=0, out_dtype=BF16, name="mix0_qkv_sb")
    qk_fx = _matmul(h0, w_qkv, n=2 * N_FOX * HEAD, b_n0=3 * N_SB * HEAD, name="mix0_qk_fox")
    v_fx = _matmul(h0, w_qkv, n=N_FOX * HEAD, b_n0=(3 * N_SB + 2 * N_FOX) * HEAD, out_dtype=BF16,
                   name="mix0_v_fox")
    fl = _matmul(h0, w_f, name="mix0_forget_logit")
    f_cum = _forget_fwd(fl, b_pad, name="forget_fwd")
    neg_f = (-f_cum[:, :N_FOX]).T.reshape(N_FOX, 1, S)
    q_f = _rms_fwd(qk_fx, fox_q_g[0], c0=0, width=N_FOX * HEAD, gw=HEAD, name="fox_q_norm")
    k_f = _rms_fwd(qk_fx, fox_k_g[0], c0=N_FOX * HEAD, width=N_FOX * HEAD, gw=HEAD, name="fox_k_norm")
    (o_sb, t_sb), landed = _sb_fwd(qkv_sb, name="sb_fwd", comm=ex_mlp0)
    (o_fx, lse0), (landed, both) = _attn_fwd(q_f, k_f, v_fx, neg_f, n_heads=N_FOX, dqk=HEAD, scale=HEAD ** -0.5,
                                             exact_p=True, name="fox_fwd", comm=[ex_mix1, hand_over(landed)])
    ag_up0, ag_dw0 = gather_end(both, s_mlp0)
    o0 = jnp.concatenate([o_sb, o_fx], axis=1)
    x1, both = _matmul(o0, w_o0, epilogue="res", res=xs, name="mix0_out", comm=hand_over(landed))
    ag_dn, ag_uq, ag_ukv, ag_o1 = gather_end(both, s_mix1)
    w_dn = jnp.pad(rows(ag_dn), ((0, 0), (0, dn_pad - dn_w)))
    w_uq = jnp.pad(cols(ag_uq).reshape(Q_RANK, N_MLA, QK_DIM), ((0, 0), (0, 0), (0, QK_PAD - QK_DIM)))
    w_uq = w_uq.reshape(Q_RANK, N_MLA * QK_PAD)
    w_ukv = cols(ag_ukv)
    w_o1 = rows(ag_o1)
    x2, mlp0 = _mlp_fwd(x1, ln_mlp_g[0], ag_up0, rows(ag_dw0), "mlp0")

    h2 = _rms_fwd(x2, ln_mix_g[1], name="mix1_norm")
    down = _matmul(h2, w_dn, name="mix1_down")
    c_q = _rms_fwd(down, q_a_full, c0=0, width=Q_RANK, name="mix1_q_a_norm")
    c_kv = _rms_fwd(down, kv_a_full, c0=Q_RANK, width=KV_RANK, name="mix1_kv_a_norm")
    q_raw = _matmul(c_q, w_uq, name="mix1_uq")
    kv_raw = _matmul(c_kv, w_ukv, name="mix1_ukv")
    half = ROPE // 2
    inv_freq = ROPE_THETA ** (-jnp.arange(half, dtype=F32) / half)
    ang = pos.astype(F32)[:, None] * inv_freq
    table = lambda t: jnp.pad(jnp.concatenate([t, t], axis=1), ((0, 0), (0, LANES - ROPE)))
    cos_t, sin_t = table(jnp.cos(ang)), table(jnp.sin(ang))
    q_pad, k_pad, v1 = _mla_prep_fwd(q_raw, kv_raw, down, cos_t, sin_t, q_g_pad, k_g_pad, name="mla_prep_fwd")
    (o1, lse1), landed = _attn_fwd(q_pad, k_pad, v1, None, n_heads=N_MLA, dqk=QK_PAD, scale=QK_DIM ** -0.5,
                                   name="mla_fwd", comm=ex_mlp1)
    x3, both = _matmul(o1, w_o1, epilogue="res", res=x2, name="mix1_out", comm=hand_over(landed))
    ag_up1, ag_dw1 = gather_end(both, s_mlp1)
    w_up = [ag_up0, ag_up1]
    w_dw = [rows(ag_dw0), rows(ag_dw1)]
    x4, mlp1 = _mlp_fwd(x3, ln_mlp_g[1], w_up[1], w_dw[1], "mlp1")

    dx4, dx4_b, loss_local = _loss_head(x4, tgt, name="loss_head")
    loss = lax.psum(loss_local, ("x", "y", "c"))

    by_cols = lambda g: g.reshape(g.shape[0], N_CHIPS, -1).transpose(1, 0, 2)
    by_rows = lambda g: g.reshape(N_CHIPS, g.shape[0] // N_CHIPS, g.shape[1])
    halves = lambda g: g.reshape(N_CHIPS, 2, g.shape[1] // 2, g.shape[2])

    def scatter_of(grads, from_sibling, tags):
        parts = [_add_sibling(g, r, core, name=f"add_sibling_{t}") for g, r, t in zip(grads, from_sibling, tags)]
        return parts, _Exchange("scatter", parts)

    def sums_of(slots, parts, tags):
        slots = [_own_slot(s, lax.dynamic_index_in_dim(p, chip, 0, keepdims=False), chip)
                 for s, p in zip(slots, parts)]
        return [_add_chips(s, name=f"add_chips_{t}") for s, t in zip(slots, tags)]

    def shards_of(joined, mine):
        return [_own_slot(j, m, ci).reshape(2 * m.shape[0], m.shape[1]) for j, m in zip(joined, mine)]

    dx3, dx3_b, dg_mlp1, dw_up1, dw_dw1, _ = _mlp_bwd(dx4, dx4_b, x3, ln_mlp_g[1], w_up[1], w_dw[1], mlp1, "mlp1")
    tags_mlp1 = ["w_up1", "w_dw1"]
    g_mlp1 = [halves(dw_up1), halves(by_rows(dw_dw1))]

    dw_o1 = _matmul(o1, dx3_b, form="tn", name="mix1_dwo")
    do1 = _matmul(dx3_b, w_o1, form="nt", name="mix1_do")
    (dq_pad, dk_pad, dv1), from_sibling = _attn_bwd(
        q_pad, k_pad, v1, None, o1, lse1, do1, n_heads=N_MLA, dqk=QK_PAD, scale=QK_DIM ** -0.5, name="mla_bwd",
        comm=_Exchange("swap", g_mlp1))
    p_mlp1, _ = scatter_of(g_mlp1, from_sibling, tags_mlp1)
    (dq_raw, dkv_raw, dpe, dg_q, dg_k), sl_up1 = _mla_prep_bwd(
        dq_pad, dk_pad, dv1, q_raw, kv_raw, down, cos_t, sin_t, q_g_pad, k_g_pad, name="mla_prep_bwd",
        comm=_Exchange("scatter", p_mlp1[:1]))
    dw_uq = _matmul(c_q, dq_raw, form="tn", name="mix1_dwuq")
    dc_q = _matmul(dq_raw, w_uq, form="nt", name="mix1_dcq")
    dw_ukv = _matmul(c_kv, dkv_raw, form="tn", name="mix1_dwukv")
    dc_kv = _matmul(dkv_raw, w_ukv, form="nt", name="mix1_dckv")
    d_cq, dg_qa = _rms_bwd(down, q_a_full, dc_q, c0=0, width=Q_RANK, name="mix1_q_a_dnorm")
    d_ckv, dg_kva = _rms_bwd(down, kv_a_full, dc_kv, c0=Q_RANK, width=KV_RANK, name="mix1_kv_a_dnorm")
    d_down = jnp.concatenate([d_cq, d_ckv, dpe], axis=1)
    dw_dn = _matmul(h2, d_down, form="tn", name="mix1_dwdown")
    dh2 = _matmul(d_down, w_dn, form="nt", name="mix1_dh")
    dx2, dx2_b, dg_mix1 = _rms_bwd(x2, ln_mix_g[1], dh2, res=dx3, bf16_copy=True, name="mix1_dnorm")
    g_uq = dw_uq.reshape(Q_RANK, N_MLA, QK_PAD)[:, :, :QK_DIM].reshape(Q_RANK, N_MLA * QK_DIM)
    tags_mix1 = ["w_dn", "w_uq", "w_ukv", "w_o1"]
    g_mix1 = [halves(by_rows(dw_dn[:, :dn_w])), halves(by_cols(g_uq)), halves(by_cols(dw_ukv)),
              halves(by_rows(dw_o1))]

    p_mix1, mine_mlp1 = [], []

    def scatter_mix1(landed):
        parts, ex = scatter_of(g_mix1, landed["dwdown"][1], tags_mix1)
        p_mix1.extend(parts)
        return ex

    def join_mlp1(landed):
        mine_mlp1.extend(sums_of(sl_up1 + landed["dwdown"][0], p_mlp1, tags_mlp1))
        return _Exchange("join", mine_mlp1)

    dx1, dx1_b, dg_mlp0, dw_up0, dw_dw0, landed = _mlp_bwd(
        dx2, dx2_b, x1, ln_mlp_g[0], w_up[0], w_dw[0], mlp0, "mlp0",
        comms={"dwdown": [_Exchange("scatter", p_mlp1[1:]), _Exchange("swap", g_mix1)],
               "du": scatter_mix1, "dwup": join_mlp1})
    gs_up1, gs_dw1 = shards_of(landed["dwup"], mine_mlp1)
    mine_mix1 = sums_of(landed["du"], p_mix1, tags_mix1)
    tags_mlp0 = ["w_up0", "w_dw0"]
    g_mlp0 = [halves(dw_up0), halves(by_rows(dw_dw0))]

    dw_o0, joined = _matmul(o0, dx1_b, form="tn", name="mix0_dwo", comm=_Exchange("join", mine_mix1))
    gs_dn, gs_uq, gs_ukv, gs_o1 = shards_of(joined, mine_mix1)
    g_o0 = [halves(by_rows(dw_o0))]
    do0 = _matmul(dx1_b, w_o0, form="nt", name="mix0_do")
    (dq_f, dk_f, dv_fx, dbias, drow), (fs_mlp0, fs_o0) = _attn_bwd(
        q_f, k_f, v_fx, neg_f, o_fx, lse0, do0, n_heads=N_FOX, dqk=HEAD, scale=HEAD ** -0.5, do_off=N_SB,
        name="fox_bwd", comm=[_Exchange("swap", g_mlp0), _Exchange("swap", g_o0)])
    p_mlp0, ex_a = scatter_of(g_mlp0, fs_mlp0, tags_mlp0)
    p_o0, ex_b = scatter_of(g_o0, fs_o0, ["w_o0"])
    (dq_sb, dk_sb, dv_sb), (sl_mlp0, sl_o0) = _sb_bwd(qkv_sb, do0, t_sb, do_off=0, name="sb_bwd", comm=[ex_a, ex_b])
    mine_mlp0 = sums_of(sl_mlp0, p_mlp0, tags_mlp0)
    mine_o0 = sums_of(sl_o0, p_o0, ["w_o0"])
    dq_fx, dg_fq = _rms_bwd(qk_fx, fox_q_g[0], dq_f, c0=0, width=N_FOX * HEAD, gw=HEAD, name="fox_q_dnorm")
    dk_fx, dg_fk = _rms_bwd(qk_fx, fox_k_g[0], dk_f, c0=N_FOX * HEAD, width=N_FOX * HEAD, gw=HEAD,
                            name="fox_k_dnorm")
    d_fcum = jnp.pad((jnp.max(drow, axis=-1) - dbias.reshape(N_FOX, S)).T, ((0, 0), (0, LANES - N_FOX)))
    dfl, db_f = _forget_bwd(fl, b_pad, d_fcum, name="forget_bwd")
    dproj = jnp.concatenate([dq_sb, dk_sb, dv_sb, dq_fx, dk_fx, dv_fx], axis=1).astype(BF16)
    dw_qkv, (j_mlp0, j_o0) = _matmul(h0, dproj, form="tn", name="mix0_dwqkv",
                                     comm=[_Exchange("join", mine_mlp0), _Exchange("join", mine_o0)])
    gs_up0, gs_dw0 = shards_of(j_mlp0, mine_mlp0)
    gs_o0, = shards_of(j_o0, mine_o0)
    dw_f = _matmul(h0, dfl, form="tn", name="mix0_dwf")
    g_in = [halves(by_cols(jnp.concatenate([dw_qkv, dw_f[:, :in_w - qkv_w]], axis=1)))]
    dh0 = _matmul(dfl, w_f, form="nt", name="mix0_dh_f")
    dh0, from_sibling = _matmul(dproj, w_qkv, form="nt", epilogue="res", res=dh0, name="mix0_dh",
                                comm=_Exchange("swap", g_in))
    p_in, ex_in_grads = scatter_of(g_in, from_sibling, ["w_in"])
    grad_x, dg_mix0 = _rms_bwd(xs, ln_mix_g[0], dh0, res=dx1, name="mix0_dnorm")
    gs_up = jnp.concatenate([gs_up0, gs_up1], axis=0)
    gs_dw = jnp.concatenate([gs_dw0, gs_dw1], axis=0)

    ln_rows = D // LANES
    small = jnp.concatenate([
        _rows(dg_mix0, ln_rows), _rows(dg_mix1, ln_rows), _rows(dg_mlp0, ln_rows), _rows(dg_mlp1, ln_rows),
        _rows(db_f, 8), _rows(dg_fq, 8), _rows(dg_fk, 8), _rows(dg_qa, 8), _rows(dg_kva, 8), _rows(dg_q, 8),
        _rows(dg_k, 8)], axis=0)
    small = _all_reduce_small(small, name="reduce_small")
    flat = lambda r0, nr, n: small[r0:r0 + nr].reshape(-1)[:n]
    r0 = 4 * ln_rows
    g_ln_mix = jnp.stack([flat(0, ln_rows, D), flat(ln_rows, ln_rows, D)])
    g_ln_mlp = jnp.stack([flat(2 * ln_rows, ln_rows, D), flat(3 * ln_rows, ln_rows, D)])
    g_b_f = flat(r0, 8, N_FOX)[None]
    g_fq, g_fk = flat(r0 + 8, 8, HEAD)[None], flat(r0 + 16, 8, HEAD)[None]
    g_qa = lax.dynamic_slice(flat(r0 + 24, 8, Q_RANK), (chip * LANES,), (LANES,))[None]
    g_kva = lax.dynamic_slice(flat(r0 + 32, 8, KV_RANK), (chip * LANES,), (LANES,))[None]
    g_q, g_k = flat(r0 + 40, 8, QK_DIM)[None], flat(r0 + 48, 8, QK_DIM)[None]

    def pack_small(ln_mix, ln_mlp, *rest):
        return jnp.concatenate([_rows(ln_mix, 2 * ln_rows), _rows(ln_mlp, 2 * ln_rows)] + [_rows(t, 8) for t in rest],
                               axis=0)

    def unpack_small(p):
        f = lambda r, nr, shape: p[r:r + nr].reshape(-1)[:int(np.prod(shape))].reshape(shape)
        shapes = [(1, N_FOX), (1, HEAD), (1, HEAD), (1, LANES), (1, LANES), (1, QK_DIM), (1, QK_DIM)]
        return (f(0, 2 * ln_rows, (2, D)), f(2 * ln_rows, 2 * ln_rows, (2, D)),
                *[f(r0 + 8 * i, 8, shp) for i, shp in enumerate(shapes)])

    small_out = _adamw(
        pack_small(ln_mix_g, ln_mlp_g, sf_b_f, fox_q_g, fox_k_g, mla_q_a_g, mla_kv_a_g, mla_q_g, mla_k_g)[None],
        pack_small(g_ln_mix, g_ln_mlp, g_b_f, g_fq, g_fk, g_qa, g_kva, g_q, g_k)[None],
        pack_small(m_ln_mix_g, m_ln_mlp_g, m_sf_b_f, m_fox_q_g, m_fox_k_g, m_mla_q_a_g, m_mla_kv_a_g, m_mla_q_g,
                   m_mla_k_g)[None],
        pack_small(v_ln_mix_g, v_ln_mlp_g, v_sf_b_f, v_fox_q_g, v_fox_k_g, v_mla_q_a_g, v_mla_kv_a_g, v_mla_q_g,
                   v_mla_k_g)[None], name="adamw_small")
    d_small, m_small, v_small = [unpack_small(p[0]) for p in small_out]

    def big(w, g, m, v, tag, comm=None):
        g = g.reshape(w.shape)
        out = _adamw(w, g, m, v, name=f"adamw_{tag}", comm=comm)
        (d, mn, vn), landed = out if comm is not None else (out, None)
        return (g, d, mn, vn) if comm is None else ((g, d, mn, vn), landed)

    r_up, slots = big(mlp_w_up, gs_up, m_mlp_w_up, v_mlp_w_up, "w_up", comm=ex_in_grads)
    mine_in = sums_of(slots, p_in, ["w_in"])
    gs_in, = shards_of(_run_exchange(_Exchange("join", mine_in), name="reduce_w_in_join"), mine_in)
    r_dw = big(mlp_w_down, gs_dw, m_mlp_w_down, v_mlp_w_down, "w_dw")
    r_in = big(sf_w_in, gs_in, m_sf_w_in, v_sf_w_in, "w_in")
    r_o0 = big(sf_w_o, gs_o0, m_sf_w_o, v_sf_w_o, "w_o0")
    r_dn = big(mla_w_down, gs_dn, m_mla_w_down, v_mla_w_down, "w_dn")
    r_uq = big(mla_w_uq, gs_uq, m_mla_w_uq, v_mla_w_uq, "w_uq")
    r_ukv = big(mla_w_ukv, gs_ukv, m_mla_w_ukv, v_mla_w_ukv, "w_ukv")
    r_o1 = big(mla_w_o, gs_o1, m_mla_w_o, v_mla_w_o, "w_o1")

    g_small = (g_ln_mix, g_ln_mlp, g_b_f, g_fq, g_fk, g_qa, g_kva, g_q, g_k)

    def ordered(k, sm):
        return (sm[0], sm[1], r_in[k], sm[2], sm[3], sm[4], r_o0[k], r_dn[k], sm[5], sm[6], r_uq[k], r_ukv[k],
                sm[7], sm[8], r_o1[k], r_up[k], r_dw[k])

    return (loss, grad_x[None], *ordered(0, g_small), *ordered(1, d_small), *ordered(2, m_small),
            *ordered(3, v_small))
```

```python
import functools

import numpy as np
import jax
import jax.numpy as jnp
from jax import lax
from jax.experimental import pallas as pl
from jax.experimental.pallas import tpu as pltpu

F32 = jnp.float32
BF16 = jnp.bfloat16
MESH = pl.DeviceIdType.MESH

EPS = 1e-6
HEAD = 128
N_SB = 8
N_FOX = 8
N_MLA = 16
Q_RANK = 512
KV_RANK = 512
NOPE = 128
ROPE = 64
QK_DIM = NOPE + ROPE
QK_PAD = 256
ROPE_THETA = 10000.0
N_CHIPS = 4

ADAM_LR = 0.001
ADAM_B1 = 0.9
ADAM_B2 = 0.999
ADAM_EPS = 1e-08
ADAM_WD = 0.01
ADAM_STEP = 10

VMEM_LIMIT = 56 * 1024 * 1024
LANES = 128
NEG = -1e30


def _cp(*sem):
    return pltpu.CompilerParams(dimension_semantics=sem, vmem_limit_bytes=VMEM_LIMIT)


def _pick(dim, target):
    if dim <= target:
        return dim
    t = (target // LANES) * LANES
    while t >= LANES:
        if dim % t == 0:
            return t
        t -= LANES
    raise ValueError(f"no tile for {dim}")


NT_DIMS = (((1,), (1,)), ((), ()))
TN_DIMS = (((0,), (0,)), ((), ()))


def _dot(a, b):
    return jnp.dot(a, b, preferred_element_type=F32)


def _dot_nt(a, b):
    return lax.dot_general(a, b, NT_DIMS, preferred_element_type=F32)


def _dot_tn(a, b):
    return lax.dot_general(a, b, TN_DIMS, preferred_element_type=F32)


def _matmul(a, b, *, name, form="nn", out_dtype=F32, n=None, b_n0=0, b_split=False,
            out_split=False, epilogue="plain", res=None, u=None, tm=1024, tn=1024, tk=2048, comm=None):
    if form == "tn":
        K, M = a.shape
    else:
        M, K = a.shape
    if b_split:
        if form == "nt":
            nb_full, kb_full = b.shape[1], b.shape[2] * N_CHIPS
        else:
            kb_full, nb_full = b.shape[1], b.shape[2] * N_CHIPS
    elif form == "nt":
        nb_full, kb_full = b.shape
    else:
        kb_full, nb_full = b.shape
    assert kb_full == K, (name, a.shape, b.shape)
    N = nb_full if n is None else n
    if a.dtype != BF16 or b.dtype != BF16:
        tk = max(tk // 2, LANES)
    tm, tn, tk = _pick(M, tm), _pick(N, tn), _pick(K, tk)
    if b_split:
        per_chip = (b.shape[2])
        if form == "nt":
            tk = _pick(per_chip, tk)
        else:
            tn = _pick(per_chip, tn)
    if out_split:
        tn = _pick(N // N_CHIPS, tn)
    assert b_n0 % tn == 0
    nb0 = b_n0 // tn
    nk = K // tk
    grid = (M // tm, N // tn, nk)

    if form == "tn":
        a_spec = pl.BlockSpec((tk, tm), lambda i, j, k: (k, i))
    else:
        a_spec = pl.BlockSpec((tm, tk), lambda i, j, k: (i, k))
    if b_split:
        if form == "nt":
            kc = b.shape[2] // tk
            b_spec = pl.BlockSpec((None, tn, tk), lambda i, j, k: (k // kc, j, k % kc))
        else:
            nc = b.shape[2] // tn
            b_spec = pl.BlockSpec((None, tk, tn), lambda i, j, k: (j // nc, k, j % nc))
    elif form == "nt":
        b_spec = pl.BlockSpec((tn, tk), lambda i, j, k: (j + nb0, k))
    else:
        b_spec = pl.BlockSpec((tk, tn), lambda i, j, k: (k, j + nb0))
    mn_spec = pl.BlockSpec((tm, tn), lambda i, j, k: (i, j))
    if out_split:
        oc = (N // N_CHIPS) // tn
        out_spec = pl.BlockSpec((None, tm, tn), lambda i, j, k: (j // oc, i, j % oc))
        out_shape = jax.ShapeDtypeStruct((N_CHIPS, M, N // N_CHIPS), out_dtype)
    else:
        out_spec = mn_spec
        out_shape = jax.ShapeDtypeStruct((M, N), out_dtype)

    in_specs = [a_spec, b_spec]
    operands = [a, b]
    out_specs = (out_spec,)
    out_shape = (out_shape,)
    if epilogue == "res":
        in_specs.append(mn_spec)
        operands.append(res)
    elif epilogue == "sqrelu_bwd":
        in_specs.append(mn_spec)
        operands.append(u)
    elif epilogue == "sqrelu":
        out_specs = (mn_spec, mn_spec)
        out_shape = (jax.ShapeDtypeStruct((M, N), F32), jax.ShapeDtypeStruct((M, N), BF16))

    def finish(refs, r):
        if epilogue == "plain":
            refs[2][...] = r.astype(out_dtype)
        elif epilogue == "res":
            refs[3][...] = (refs[2][...] + r).astype(out_dtype)
        elif epilogue == "sqrelu":
            refs[2][...] = r
            p = jnp.maximum(r, 0.0)
            refs[3][...] = (p * p).astype(BF16)
        else:
            refs[3][...] = (r * (2.0 * jnp.maximum(refs[2][...], 0.0))).astype(out_dtype)

    def body(*refs):
        at = refs[0][...].astype(BF16)
        bt = refs[1][...].astype(BF16)
        if form == "nn":
            part = _dot(at, bt)
        elif form == "nt":
            part = _dot_nt(at, bt)
        else:
            part = _dot_tn(at, bt)
        if nk == 1:
            finish(refs, part)
            return
        acc = refs[-1]
        k = pl.program_id(2)

        @pl.when(k == 0)
        def _():
            acc[...] = part

        @pl.when(jnp.logical_and(k > 0, k < nk - 1))
        def _():
            acc[...] += part

        @pl.when(k == nk - 1)
        def _():
            finish(refs, acc[...] + part)

    outs, comm_outs = _hosted_call(
        body, grid=grid, in_specs=in_specs, out_specs=out_specs, out_shape=out_shape,
        scratch_shapes=[] if nk == 1 else [pltpu.VMEM((tm, tn), F32)], operands=operands, name=name,
        sem=("parallel", "parallel", "arbitrary"), comm=comm)
    result = outs if epilogue == "sqrelu" else outs[0]
    return result if comm is None else (result, comm_outs)


def _rms_fwd(x, g, *, name, c0=0, width=None, gw=None, tr=256):
    R, ctot = x.shape
    C = ctot if width is None else width
    gw = C if gw is None else gw
    assert c0 % C == 0 and C % gw == 0
    tr = _pick(R, tr)
    cb = c0 // C
    ng = C // gw

    def body(x_ref, g_ref, o_ref):
        gv = g_ref[...]
        for gi in range(ng):
            cols = slice(gi * gw, (gi + 1) * gw)
            xs = x_ref[:, cols]
            ms = jnp.sum(xs * xs, axis=-1, keepdims=True) * (1.0 / gw)
            o_ref[:, cols] = ((xs * lax.rsqrt(ms + EPS)) * gv).astype(o_ref.dtype)

    return pl.pallas_call(
        body, out_shape=jax.ShapeDtypeStruct((R, C), BF16), grid=(R // tr,),
        in_specs=[pl.BlockSpec((tr, C), lambda i: (i, cb)), pl.BlockSpec((1, gw), lambda i: (0, 0))],
        out_specs=pl.BlockSpec((tr, C), lambda i: (i, 0)), name=name,
        compiler_params=_cp("parallel"))(x, g.reshape(1, gw).astype(F32))


def _rms_bwd(x, g, dy, *, name, res=None, c0=0, width=None, gw=None, tr=256, bf16_copy=False):
    bf16_copy = int(bf16_copy)
    R, ctot = x.shape
    C = ctot if width is None else width
    gw = C if gw is None else gw
    tr = _pick(R, tr)
    cb = c0 // C
    ng = C // gw
    nsteps = R // tr
    row_spec = pl.BlockSpec((tr, C), lambda i: (i, 0))
    in_specs = [pl.BlockSpec((tr, C), lambda i: (i, cb)), pl.BlockSpec((1, gw), lambda i: (0, 0)), row_spec]
    operands = [x, g.reshape(1, gw).astype(F32), dy]
    if res is not None:
        in_specs.append(row_spec)
        operands.append(res)

    def body(*refs):
        x_ref, g_ref, dy_ref = refs[:3]
        res_ref = refs[3] if res is not None else None
        dx_ref, dg_ref = refs[-3 - bf16_copy], refs[-2]
        acc = refs[-1]
        i = pl.program_id(0)

        @pl.when(i == 0)
        def _():
            acc[...] = jnp.zeros_like(acc)

        gv = g_ref[...]
        for gi in range(ng):
            cols = slice(gi * gw, (gi + 1) * gw)
            xs = x_ref[:, cols]
            dys = dy_ref[:, cols].astype(F32)
            rstd = lax.rsqrt(jnp.sum(xs * xs, axis=-1, keepdims=True) * (1.0 / gw) + EPS)
            xh = xs * rstd
            gdy = dys * gv
            m = jnp.sum(gdy * xh, axis=-1, keepdims=True) * (1.0 / gw)
            dx = rstd * (gdy - xh * m)
            if res_ref is not None:
                dx = dx + res_ref[:, cols]
            dx_ref[:, cols] = dx
            if bf16_copy:
                refs[-3][:, cols] = dx.astype(BF16)
            acc[...] += jnp.sum((dys * xh).reshape(tr // 8, 8, gw), axis=0)

        @pl.when(i == nsteps - 1)
        def _():
            dg_ref[...] = jnp.sum(acc[...], axis=0, keepdims=True)

    out_shape = [jax.ShapeDtypeStruct((R, C), F32)] + [jax.ShapeDtypeStruct((R, C), BF16)] * bf16_copy
    outs = pl.pallas_call(
        body, out_shape=tuple(out_shape + [jax.ShapeDtypeStruct((1, gw), F32)]),
        grid=(nsteps,), in_specs=in_specs,
        out_specs=tuple([row_spec] * len(out_shape) + [pl.BlockSpec((1, gw), lambda i: (0, 0))]),
        scratch_shapes=[pltpu.VMEM((8, gw), F32)], name=name,
        compiler_params=_cp("arbitrary"))(*operands)
    return (*outs[:-1], outs[-1][0])


def _split3(x):
    hi = x.astype(BF16)
    r1 = x - hi.astype(F32)
    mid = r1.astype(BF16)
    lo = (r1 - mid.astype(F32)).astype(BF16)
    return hi, mid, lo


def _log_sigmoid(z):
    return jnp.minimum(z, 0.0) - jnp.log(1.0 + jnp.exp(-jnp.abs(z)))


def _forget_fwd(fl, b, *, name, tb=512):
    S = fl.shape[0]
    tb = _pick(S, tb)

    def body(fl_ref, b_ref, f_ref, carry):
        i = pl.program_id(0)

        @pl.when(i == 0)
        def _():
            carry[...] = jnp.zeros_like(carry)

        lf = _log_sigmoid(fl_ref[...] + b_ref[...])
        r = lax.broadcasted_iota(jnp.int32, (tb, tb), 0)
        c = lax.broadcasted_iota(jnp.int32, (tb, tb), 1)
        tri = (c <= r).astype(BF16)
        hi, mid, lo = _split3(lf)
        cs = _dot(tri, hi) + _dot(tri, mid) + _dot(tri, lo)
        f_ref[...] = cs + carry[...]
        carry[...] += jnp.sum(lf, axis=0, keepdims=True)

    return pl.pallas_call(
        body, out_shape=jax.ShapeDtypeStruct((S, LANES), F32), grid=(S // tb,),
        in_specs=[pl.BlockSpec((tb, LANES), lambda i: (i, 0)), pl.BlockSpec((1, LANES), lambda i: (0, 0))],
        out_specs=pl.BlockSpec((tb, LANES), lambda i: (i, 0)),
        scratch_shapes=[pltpu.VMEM((1, LANES), F32)], name=name,
        compiler_params=_cp("arbitrary"))(fl, b)


def _forget_bwd(fl, b, dF, *, name, tb=512):
    S = fl.shape[0]
    tb = _pick(S, tb)
    nb = S // tb

    def body(fl_ref, b_ref, df_ref, dfl_ref, db_ref, carry, acc):
        i = pl.program_id(0)

        @pl.when(i == 0)
        def _():
            carry[...] = jnp.zeros_like(carry)
            acc[...] = jnp.zeros_like(acc)

        d = df_ref[...]
        r = lax.broadcasted_iota(jnp.int32, (tb, tb), 0)
        c = lax.broadcasted_iota(jnp.int32, (tb, tb), 1)
        tri = (c >= r).astype(BF16)
        hi, mid, lo = _split3(d)
        rc = _dot(tri, hi) + _dot(tri, mid) + _dot(tri, lo) + carry[...]
        z = fl_ref[...] + b_ref[...]
        dfl = rc * jnp.exp(_log_sigmoid(-z))
        dfl_ref[...] = dfl
        carry[...] += jnp.sum(d, axis=0, keepdims=True)
        acc[...] += jnp.sum(dfl, axis=0, keepdims=True)

        @pl.when(i == nb - 1)
        def _():
            db_ref[...] = acc[...]

    rev = lambda i: (nb - 1 - i, 0)
    dfl, db = pl.pallas_call(
        body, out_shape=(jax.ShapeDtypeStruct((S, LANES), F32), jax.ShapeDtypeStruct((1, LANES), F32)),
        grid=(nb,),
        in_specs=[pl.BlockSpec((tb, LANES), rev), pl.BlockSpec((1, LANES), lambda i: (0, 0)),
                  pl.BlockSpec((tb, LANES), rev)],
        out_specs=(pl.BlockSpec((tb, LANES), rev), pl.BlockSpec((1, LANES), lambda i: (0, 0))),
        scratch_shapes=[pltpu.VMEM((1, LANES), F32), pltpu.VMEM((1, LANES), F32)], name=name,
        compiler_params=_cp("arbitrary"))(fl, b, dF)
    return dfl, db[0]


def _tri(tk, rel):
    r = lax.broadcasted_iota(jnp.int32, (tk, tk), 0)
    c = lax.broadcasted_iota(jnp.int32, (tk, tk), 1)
    m = {"gt": r > c, "le": r <= c, "lt": r < c}[rel]
    return m.astype(BF16)


def _split2(x):
    hi = x.astype(BF16)
    return hi, (x - hi.astype(F32)).astype(BF16)


HEADS_PER_STEP = 2
CUM_CHUNK = 256


def _cum_cols(x, tri, suffix):
    ck = tri.shape[0]
    n = x.shape[1] // ck
    hi, lo = _split2(x)
    parts, sums = [], []
    for c in range(n):
        cs = slice(c * ck, (c + 1) * ck)
        parts.append(_dot(hi[:, cs], tri) + _dot(lo[:, cs], tri))
        sums.append(jnp.sum(x[:, cs], axis=1, keepdims=True))
    carry = None
    for c in (reversed(range(n)) if suffix else range(n)):
        if carry is not None:
            parts[c] = parts[c] + carry
        carry = sums[c] if carry is None else carry + sums[c]
    return (parts[0] if n == 1 else jnp.concatenate(parts, axis=1)), carry


def _diag_mask(tq, strict):
    r = lax.broadcasted_iota(jnp.int32, (tq, tq), 0)
    c = lax.broadcasted_iota(jnp.int32, (tq, tq), 1)
    return c < r if strict else c <= r


def _sb_fwd(qkv, *, name, n_heads=N_SB, q_off=0, k_off=N_SB, v_off=2 * N_SB, tq=512, hp=HEADS_PER_STEP,
            comm=None):
    S = qkv.shape[0]
    tq = _pick(S, tq)
    tk = tq
    scale = HEAD ** -0.5
    nq = S // tq
    assert n_heads % hp == 0 and q_off % hp == 0 and k_off % hp == 0 and v_off % hp == 0

    def body(q_ref, k_ref, v_ref, o_ref, t_ref, c_sc, acc_sc):
        qi = pl.program_id(1)
        c_sc[...] = jnp.zeros_like(c_sc)
        acc_sc[...] = jnp.zeros_like(acc_sc)
        gt = _tri(min(CUM_CHUNK, tk), "gt")

        def tile(hh, j, diag):
            cs = slice(hh * HEAD, (hh + 1) * HEAD)
            rows = pl.ds(pl.multiple_of(j * tk, tk), tk)
            z = _dot_nt(q_ref[:, cs], k_ref[rows, cs]) * scale
            sp = jnp.log(1.0 + jnp.exp(-jnp.abs(z)))
            la = jnp.minimum(z, 0.0) - sp
            lb = -jnp.maximum(z, 0.0) - sp
            if diag:
                strict = _diag_mask(tq, True)
                lb = jnp.where(strict, lb, 0.0)
            suffix, total = _cum_cols(lb, gt, suffix=True)
            w = jnp.exp(la + suffix + c_sc[hh])
            if diag:
                w = jnp.where(strict, w, 0.0)
            acc_sc[hh] += _dot(w.astype(BF16), v_ref[rows, cs])
            c_sc[hh] += total

        for hh in range(hp):
            tile(hh, qi, True)

        def step(it, carry):
            for hh in range(hp):
                tile(hh, qi - 1 - it, False)
            return carry

        lax.fori_loop(0, qi, step, 0)
        for hh in range(hp):
            o_ref[:, hh * HEAD:(hh + 1) * HEAD] = acc_sc[hh]
            t_ref[hh] = jnp.broadcast_to(c_sc[hh], (tq, LANES))

    w = hp * HEAD
    head_blk = lambda off: pl.BlockSpec((S, w), lambda h, i: (0, h + off // hp))
    outs, comm_outs = _hosted_call(
        body,
        out_shape=(jax.ShapeDtypeStruct((S, n_heads * HEAD), F32),
                   jax.ShapeDtypeStruct((n_heads, S, LANES), F32)),
        grid=(n_heads // hp, nq),
        in_specs=[pl.BlockSpec((tq, w), lambda h, i: (i, h + q_off // hp)), head_blk(k_off), head_blk(v_off)],
        out_specs=(pl.BlockSpec((tq, w), lambda h, i: (i, h)),
                   pl.BlockSpec((hp, tq, LANES), lambda h, i: (h, i, 0))),
        scratch_shapes=[pltpu.VMEM((hp, tq, 1), F32), pltpu.VMEM((hp, tq, HEAD), F32)], name=name,
        sem=("parallel", "arbitrary"), operands=(qkv, qkv, qkv), comm=comm)
    return outs if comm is None else (outs, comm_outs)


def _sb_bwd(qkv, do, tstat, *, name, n_heads=N_SB, q_off=0, k_off=N_SB, v_off=2 * N_SB, do_off=0, tq=512,
            hp=HEADS_PER_STEP, comm=None):
    S = qkv.shape[0]
    tq = _pick(S, tq)
    tk = tq
    scale = HEAD ** -0.5
    nq = S // tq
    assert n_heads % hp == 0 and q_off % hp == 0 and k_off % hp == 0 and v_off % hp == 0 and do_off % hp == 0

    def body(q_ref, k_ref, v_ref, do_ref, t_ref, dq_ref, dk_ref, dv_ref, p_sc, r_sc, dq_sc):
        qi = pl.program_id(1)

        @pl.when(qi == 0)
        def _():
            dk_ref[...] = jnp.zeros_like(dk_ref)
            dv_ref[...] = jnp.zeros_like(dv_ref)

        p_sc[...] = jnp.zeros_like(p_sc)
        r_sc[...] = jnp.zeros_like(r_sc)
        dq_sc[...] = jnp.zeros_like(dq_sc)
        le = _tri(min(CUM_CHUNK, tk), "le")
        lt = _tri(min(CUM_CHUNK, tk), "lt")

        def tile(hh, j, diag):
            cs = slice(hh * HEAD, (hh + 1) * HEAD)
            rows = pl.ds(pl.multiple_of(j * tk, tk), tk)
            q = q_ref[:, cs]
            do_b = do_ref[:, cs].astype(BF16)
            kb = k_ref[rows, cs]
            z = _dot_nt(q, kb) * scale
            sp = jnp.log(1.0 + jnp.exp(-jnp.abs(z)))
            la = jnp.minimum(z, 0.0) - sp
            lb = -jnp.maximum(z, 0.0) - sp
            if diag:
                strict = _diag_mask(tq, True)
                lb = jnp.where(strict, lb, 0.0)
            prefix, total_b = _cum_cols(lb, le, suffix=False)
            w = jnp.exp(la + t_ref[hh, :, 0:1] - (prefix + p_sc[hh]))
            if diag:
                w = jnp.where(strict, w, 0.0)
            r = w * _dot_nt(do_b, v_ref[rows, cs])
            rex, total_r = _cum_cols(r, lt, suffix=False)
            rex = rex + r_sc[hh]
            beta = jnp.exp(la)
            dz = r - beta * (r + rex)
            if diag:
                dz = jnp.where(strict, dz, 0.0)
            dzb = dz.astype(BF16)
            dq_sc[hh] += _dot(dzb, kb)
            dk_ref[rows, cs] += _dot_tn(dzb, q)
            dv_ref[rows, cs] += _dot_tn(w.astype(BF16), do_b)
            p_sc[hh] += total_b
            r_sc[hh] += total_r

        def step(j, carry):
            for hh in range(hp):
                tile(hh, j, False)
            return carry

        lax.fori_loop(0, qi, step, 0)
        for hh in range(hp):
            tile(hh, qi, True)
            dq_ref[:, hh * HEAD:(hh + 1) * HEAD] = dq_sc[hh] * scale

        @pl.when(qi == nq - 1)
        def _():
            dk_ref[...] = dk_ref[...] * scale

    w = hp * HEAD
    head_blk = lambda off: pl.BlockSpec((S, w), lambda h, i: (0, h + off // hp))
    out_head = pl.BlockSpec((S, w), lambda h, i: (0, h))
    out_sd = jax.ShapeDtypeStruct((S, n_heads * HEAD), F32)
    outs, comm_outs = _hosted_call(
        body, out_shape=(out_sd, out_sd, out_sd), grid=(n_heads // hp, nq),
        in_specs=[pl.BlockSpec((tq, w), lambda h, i: (i, h + q_off // hp)), head_blk(k_off), head_blk(v_off),
                  pl.BlockSpec((tq, w), lambda h, i: (i, h + do_off // hp)),
                  pl.BlockSpec((hp, tq, LANES), lambda h, i: (h, i, 0))],
        out_specs=(pl.BlockSpec((tq, w), lambda h, i: (i, h)), out_head, out_head),
        scratch_shapes=[pltpu.VMEM((hp, tq, 1), F32), pltpu.VMEM((hp, tq, 1), F32), pltpu.VMEM((hp, tq, HEAD), F32)],
        name=name, sem=("parallel", "arbitrary"), operands=(qkv, qkv, qkv, do, tstat), comm=comm)
    return outs if comm is None else (outs, comm_outs)


def _attn_fwd(q, k, v, bias, *, name, n_heads, dqk, scale, v_off=0, tq=1024, exact_p=False, hp=HEADS_PER_STEP,
              comm=None):
    S = q.shape[0]
    tq = _pick(S, tq)
    tk = tq
    nq = S // tq
    has_bias = bias is not None

    assert n_heads % hp == 0 and v_off % hp == 0

    def body(*refs):
        q_ref, k_ref, v_ref = refs[:3]
        b_ref = refs[3] if has_bias else None
        o_ref, lse_ref, m_sc, l_sc, acc_sc = refs[-5:]
        qi = pl.program_id(1)
        m_sc[...] = jnp.full_like(m_sc, NEG)
        l_sc[...] = jnp.zeros_like(l_sc)
        acc_sc[...] = jnp.zeros_like(acc_sc)

        def tile(hh, j, diag):
            rows = pl.ds(pl.multiple_of(j * tk, tk), tk)
            s = _dot_nt(q_ref[:, hh * dqk:(hh + 1) * dqk], k_ref[rows, hh * dqk:(hh + 1) * dqk]) * scale
            if has_bias:
                s = s + b_ref[hh, :, rows]
            if diag:
                s = jnp.where(_diag_mask(tq, False), s, NEG)
            m_old = m_sc[hh]
            m_new = jnp.maximum(m_old, jnp.max(s, axis=1, keepdims=True))
            alpha = jnp.exp(m_old - m_new)
            p = jnp.exp(s - m_new)
            l_sc[hh] = alpha * l_sc[hh] + jnp.sum(p, axis=1, keepdims=True)
            vb = v_ref[rows, hh * HEAD:(hh + 1) * HEAD]
            if exact_p:
                hi, lo = _split2(p)
                pv = _dot(hi, vb) + _dot(lo, vb)
            else:
                pv = _dot(p.astype(BF16), vb)
            acc_sc[hh] = alpha * acc_sc[hh] + pv
            m_sc[hh] = m_new

        def step(j, carry):
            for hh in range(hp):
                tile(hh, j, False)
            return carry

        lax.fori_loop(0, qi, step, 0)
        for hh in range(hp):
            tile(hh, qi, True)
            l = l_sc[hh]
            o_ref[:, hh * HEAD:(hh + 1) * HEAD] = acc_sc[hh] / l
            lse_ref[hh] = jnp.broadcast_to(m_sc[hh] + jnp.log(l), (tq, LANES))

    in_specs = [pl.BlockSpec((tq, hp * dqk), lambda h, i: (i, h)),
                pl.BlockSpec((S, hp * dqk), lambda h, i: (0, h)),
                pl.BlockSpec((S, hp * HEAD), lambda h, i: (0, h + v_off // hp))]
    operands = [q, k, v]
    if has_bias:
        in_specs.append(pl.BlockSpec((hp, 1, S), lambda h, i: (h, 0, 0)))
        operands.append(bias)
    outs, comm_outs = _hosted_call(
        body,
        out_shape=(jax.ShapeDtypeStruct((S, n_heads * HEAD), F32),
                   jax.ShapeDtypeStruct((n_heads, S, LANES), F32)),
        grid=(n_heads // hp, nq), in_specs=in_specs,
        out_specs=(pl.BlockSpec((tq, hp * HEAD), lambda h, i: (i, h)),
                   pl.BlockSpec((hp, tq, LANES), lambda h, i: (h, i, 0))),
        scratch_shapes=[pltpu.VMEM((hp, tq, 1), F32), pltpu.VMEM((hp, tq, 1), F32), pltpu.VMEM((hp, tq, HEAD), F32)],
        name=name, sem=("parallel", "arbitrary"), operands=operands, comm=comm)
    return outs if comm is None else (outs, comm_outs)


def _attn_bwd(q, k, v, bias, o, lse, do, *, name, n_heads, dqk, scale, v_off=0, do_off=0, tq=512,
              hp=HEADS_PER_STEP, comm=None):
    S = q.shape[0]
    tq = _pick(S, tq)
    tk = tq
    nq = S // tq
    has_bias = bias is not None
    assert n_heads % hp == 0 and v_off % hp == 0 and do_off % hp == 0

    def body(*refs):
        q_ref, k_ref, v_ref, o_ref, lse_ref, do_ref = refs[:6]
        b_ref = refs[6] if has_bias else None
        n_out = 5 if has_bias else 3
        outs = refs[-(n_out + 3):-3]
        dq_ref, dk_ref, dv_ref = outs[:3]
        db_ref, dr_ref = (outs[3], outs[4]) if has_bias else (None, None)
        dq_sc, rs_sc, delta_sc = refs[-3:]
        qi = pl.program_id(1)

        @pl.when(qi == 0)
        def _():
            dk_ref[...] = jnp.zeros_like(dk_ref)
            dv_ref[...] = jnp.zeros_like(dv_ref)
            if has_bias:
                db_ref[...] = jnp.zeros_like(db_ref)

        dq_sc[...] = jnp.zeros_like(dq_sc)
        rs_sc[...] = jnp.zeros_like(rs_sc)
        for hh in range(hp):
            vs = slice(hh * HEAD, (hh + 1) * HEAD)
            do_r = do_ref[:, vs].astype(BF16).astype(F32)
            delta_sc[hh] = jnp.sum(do_r * o_ref[:, vs], axis=1, keepdims=True)

        def tile(hh, j, diag):
            qs = slice(hh * dqk, (hh + 1) * dqk)
            vs = slice(hh * HEAD, (hh + 1) * HEAD)
            rows = pl.ds(pl.multiple_of(j * tk, tk), tk)
            qb = q_ref[:, qs]
            do_b = do_ref[:, vs].astype(BF16)
            delta = delta_sc[hh]
            kb = k_ref[rows, qs]
            s = _dot_nt(qb, kb) * scale
            if has_bias:
                s = s + b_ref[hh, :, rows]
            p = jnp.exp(s - lse_ref[hh, :, 0:1])
            if diag:
                p = jnp.where(_diag_mask(tq, False), p, 0.0)
            ds = p * (_dot_nt(do_b, v_ref[rows, vs]) - delta)
            dsb = (ds * scale).astype(BF16)
            dq_sc[hh] += _dot(dsb, kb)
            dk_ref[rows, qs] += _dot_tn(dsb, qb)
            dv_ref[rows, vs] += _dot_tn(p.astype(BF16), do_b)
            if has_bias:
                db_ref[hh, :, rows] += jnp.sum(ds, axis=0, keepdims=True)
                rs_sc[hh] += jnp.sum(ds, axis=1, keepdims=True)

        def step(j, carry):
            for hh in range(hp):
                tile(hh, j, False)
            return carry

        lax.fori_loop(0, qi, step, 0)
        for hh in range(hp):
            tile(hh, qi, True)
            dq_ref[:, hh * dqk:(hh + 1) * dqk] = dq_sc[hh]
            if has_bias:
                dr_ref[hh] = jnp.broadcast_to(rs_sc[hh], (tq, LANES))

    stat = pl.BlockSpec((hp, tq, LANES), lambda h, i: (h, i, 0))
    in_specs = [pl.BlockSpec((tq, hp * dqk), lambda h, i: (i, h)),
                pl.BlockSpec((S, hp * dqk), lambda h, i: (0, h)),
                pl.BlockSpec((S, hp * HEAD), lambda h, i: (0, h + v_off // hp)),
                pl.BlockSpec((tq, hp * HEAD), lambda h, i: (i, h)),
                stat,
                pl.BlockSpec((tq, hp * HEAD), lambda h, i: (i, h + do_off // hp))]
    operands = [q, k, v, o, lse, do]
    out_shape = [jax.ShapeDtypeStruct((S, n_heads * dqk), F32), jax.ShapeDtypeStruct((S, n_heads * dqk), F32),
                 jax.ShapeDtypeStruct((S, n_heads * HEAD), F32)]
    out_specs = [pl.BlockSpec((tq, hp * dqk), lambda h, i: (i, h)), pl.BlockSpec((S, hp * dqk), lambda h, i: (0, h)),
                 pl.BlockSpec((S, hp * HEAD), lambda h, i: (0, h))]
    if has_bias:
        in_specs.append(pl.BlockSpec((hp, 1, S), lambda h, i: (h, 0, 0)))
        operands.append(bias)
        out_shape.append(jax.ShapeDtypeStruct((n_heads, 1, S), F32))
        out_specs.append(pl.BlockSpec((hp, 1, S), lambda h, i: (h, 0, 0)))
        out_shape.append(jax.ShapeDtypeStruct((n_heads, S, LANES), F32))
        out_specs.append(stat)
    outs, comm_outs = _hosted_call(
        body, out_shape=tuple(out_shape), grid=(n_heads // hp, nq), in_specs=in_specs, out_specs=tuple(out_specs),
        scratch_shapes=[pltpu.VMEM((hp, tq, dqk), F32), pltpu.VMEM((hp, tq, 1), F32),
                        pltpu.VMEM((hp, tq, 1), F32)], name=name,
        sem=("parallel", "arbitrary"), operands=operands, comm=comm)
    return outs if comm is None else (outs, comm_outs)


def _rot_half(y):
    lane = lax.broadcasted_iota(jnp.int32, y.shape, 1)
    up = pltpu.roll(y, 96, 1)
    down = pltpu.roll(y, 32, 1)
    return jnp.where(lane < 32, -up, jnp.where(lane < 64, down, 0.0))


def _mla_prep_fwd(q_raw, kv_raw, down, cos, sin, q_g, k_g, *, name, ts=128):
    S = q_raw.shape[0]
    ts = _pick(S, ts)
    pe_blk = Q_RANK // LANES + KV_RANK // LANES

    def norm_rope(x0, x1, g0, g1, c, s):
        ms = (jnp.sum(x0 * x0, axis=-1, keepdims=True) + jnp.sum(x1 * x1, axis=-1, keepdims=True)) * (1.0 / QK_DIM)
        rstd = lax.rsqrt(ms + EPS)
        y0 = (x0 * rstd) * g0
        y1 = (x1 * rstd) * g1
        return y0, y1 * c + _rot_half(y1) * s

    def body(q_ref, kv_ref, pe_ref, cos_ref, sin_ref, qg_ref, kg_ref, qo_ref, ko_ref, vo_ref):
        c, s = cos_ref[...], sin_ref[...]
        pe = pe_ref[...]
        qg0, qg1 = qg_ref[:, :NOPE], qg_ref[:, NOPE:]
        kg0, kg1 = kg_ref[:, :NOPE], kg_ref[:, NOPE:]
        for h in range(N_MLA):
            b = h * QK_PAD
            y0, y1 = norm_rope(q_ref[:, b:b + NOPE], q_ref[:, b + NOPE:b + QK_PAD], qg0, qg1, c, s)
            qo_ref[:, b:b + NOPE] = y0.astype(BF16)
            qo_ref[:, b + NOPE:b + QK_PAD] = y1.astype(BF16)
            y0, y1 = norm_rope(kv_ref[:, b:b + NOPE], pe, kg0, kg1, c, s)
            ko_ref[:, b:b + NOPE] = y0.astype(BF16)
            ko_ref[:, b + NOPE:b + QK_PAD] = y1.astype(BF16)
            vo_ref[:, h * HEAD:(h + 1) * HEAD] = kv_ref[:, b + NOPE:b + QK_PAD].astype(BF16)

    wide = pl.BlockSpec((ts, N_MLA * QK_PAD), lambda i: (i, 0))
    lane_blk = pl.BlockSpec((ts, LANES), lambda i: (i, 0))
    gain = pl.BlockSpec((1, QK_PAD), lambda i: (0, 0))
    return pl.pallas_call(
        body,
        out_shape=(jax.ShapeDtypeStruct((S, N_MLA * QK_PAD), BF16), jax.ShapeDtypeStruct((S, N_MLA * QK_PAD), BF16),
                   jax.ShapeDtypeStruct((S, N_MLA * HEAD), BF16)),
        grid=(S // ts,),
        in_specs=[wide, wide, pl.BlockSpec((ts, LANES), lambda i: (i, pe_blk)), lane_blk, lane_blk, gain, gain],
        out_specs=(wide, wide, pl.BlockSpec((ts, N_MLA * HEAD), lambda i: (i, 0))), name=name,
        compiler_params=_cp("parallel"))(q_raw, kv_raw, down, cos, sin, q_g, k_g)


def _mla_prep_bwd(dq, dk, dv, q_raw, kv_raw, down, cos, sin, q_g, k_g, *, name, ts=128, comm=None):
    S = q_raw.shape[0]
    ts = _pick(S, ts)
    nsteps = S // ts
    pe_blk = Q_RANK // LANES + KV_RANK // LANES

    def back(x0, x1, g0, g1, c, s, d0, d1r):
        d1 = d1r * c - _rot_half(d1r * s)
        ms = (jnp.sum(x0 * x0, axis=-1, keepdims=True) + jnp.sum(x1 * x1, axis=-1, keepdims=True)) * (1.0 / QK_DIM)
        rstd = lax.rsqrt(ms + EPS)
        h0, h1 = x0 * rstd, x1 * rstd
        e0, e1 = d0 * g0, d1 * g1
        m = (jnp.sum(e0 * h0, axis=-1, keepdims=True) + jnp.sum(e1 * h1, axis=-1, keepdims=True)) * (1.0 / QK_DIM)
        return rstd * (e0 - h0 * m), rstd * (e1 - h1 * m), d0 * h0, d1 * h1

    def fold(a):
        return jnp.sum(a.reshape(ts // 8, 8, a.shape[-1]), axis=0)

    def body(dq_ref, dk_ref, dv_ref, q_ref, kv_ref, pe_ref, cos_ref, sin_ref, qg_ref, kg_ref,
             dqr_ref, dkv_ref, dpe_ref, dqg_ref, dkg_ref, gq_sc, gk_sc):
        i = pl.program_id(0)

        @pl.when(i == 0)
        def _():
            gq_sc[...] = jnp.zeros_like(gq_sc)
            gk_sc[...] = jnp.zeros_like(gk_sc)

        c, s = cos_ref[...], sin_ref[...]
        pe = pe_ref[...]
        qg0, qg1 = qg_ref[:, :NOPE], qg_ref[:, NOPE:]
        kg0, kg1 = kg_ref[:, :NOPE], kg_ref[:, NOPE:]
        dpe = jnp.zeros((ts, LANES), F32)
        for h in range(N_MLA):
            b = h * QK_PAD
            dx0, dx1, a0, a1 = back(q_ref[:, b:b + NOPE], q_ref[:, b + NOPE:b + QK_PAD], qg0, qg1, c, s,
                                    dq_ref[:, b:b + NOPE], dq_ref[:, b + NOPE:b + QK_PAD])
            dqr_ref[:, b:b + NOPE] = dx0.astype(BF16)
            dqr_ref[:, b + NOPE:b + QK_PAD] = dx1.astype(BF16)
            gq_sc[:, :NOPE] += fold(a0)
            gq_sc[:, NOPE:] += fold(a1)
            dx0, dx1, a0, a1 = back(kv_ref[:, b:b + NOPE], pe, kg0, kg1, c, s,
                                    dk_ref[:, b:b + NOPE], dk_ref[:, b + NOPE:b + QK_PAD])
            dkv_ref[:, b:b + NOPE] = dx0.astype(BF16)
            dkv_ref[:, b + NOPE:b + QK_PAD] = dv_ref[:, h * HEAD:(h + 1) * HEAD].astype(BF16)
            dpe = dpe + dx1
            gk_sc[:, :NOPE] += fold(a0)
            gk_sc[:, NOPE:] += fold(a1)
        dpe_ref[...] = dpe

        @pl.when(i == nsteps - 1)
        def _():
            dqg_ref[...] = jnp.sum(gq_sc[...], axis=0, keepdims=True)
            dkg_ref[...] = jnp.sum(gk_sc[...], axis=0, keepdims=True)

    wide = pl.BlockSpec((ts, N_MLA * QK_PAD), lambda i: (i, 0))
    lane_blk = pl.BlockSpec((ts, LANES), lambda i: (i, 0))
    gain = pl.BlockSpec((1, QK_PAD), lambda i: (0, 0))
    outs, comm_outs = _hosted_call(
        body,
        out_shape=(jax.ShapeDtypeStruct((S, N_MLA * QK_PAD), BF16), jax.ShapeDtypeStruct((S, N_MLA * QK_PAD), BF16),
                   jax.ShapeDtypeStruct((S, LANES), F32), jax.ShapeDtypeStruct((1, QK_PAD), F32),
                   jax.ShapeDtypeStruct((1, QK_PAD), F32)),
        grid=(nsteps,),
        in_specs=[wide, wide, pl.BlockSpec((ts, N_MLA * HEAD), lambda i: (i, 0)), wide, wide,
                  pl.BlockSpec((ts, LANES), lambda i: (i, pe_blk)), lane_blk, lane_blk, gain, gain],
        out_specs=(wide, wide, lane_blk, gain, gain),
        scratch_shapes=[pltpu.VMEM((8, QK_PAD), F32), pltpu.VMEM((8, QK_PAD), F32)], name=name,
        sem=("arbitrary",), operands=(dq, dk, dv, q_raw, kv_raw, down, cos, sin, q_g, k_g), comm=comm)
    res = (outs[0], outs[1], outs[2], outs[3][0], outs[4][0])
    return res if comm is None else (res, comm_outs)


def _loss_head(y, target, *, name, tr=256):
    R, C = y.shape
    tr = _pick(R, tr)
    nsteps = R // tr

    def body(y_ref, t_ref, dy_ref, dyb_ref, loss_ref, acc):
        i = pl.program_id(0)

        @pl.when(i == 0)
        def _():
            acc[...] = jnp.zeros_like(acc)

        err = y_ref[...] - t_ref[...]
        dy = err * (1.0 / C)
        dy_ref[...] = dy
        dyb_ref[...] = dy.astype(BF16)
        acc[...] += jnp.sum((err * err).reshape(tr // 8, 8, C), axis=0)

        @pl.when(i == nsteps - 1)
        def _():
            tot = jnp.sum(jnp.sum(acc[...], axis=0, keepdims=True), axis=1, keepdims=True)
            loss_ref[...] = jnp.broadcast_to(tot * (0.5 / C), (8, LANES))

    blk = pl.BlockSpec((tr, C), lambda i: (i, 0))
    dy, dy_b, loss = pl.pallas_call(
        body, out_shape=(jax.ShapeDtypeStruct((R, C), F32), jax.ShapeDtypeStruct((R, C), BF16),
                         jax.ShapeDtypeStruct((8, LANES), F32)),
        grid=(nsteps,), in_specs=[blk, blk], out_specs=(blk, blk, pl.BlockSpec((8, LANES), lambda i: (0, 0))),
        scratch_shapes=[pltpu.VMEM((8, C), F32)], name=name, compiler_params=_cp("arbitrary"))(y, target)
    return dy, dy_b, loss[0, 0]


def _adamw(w, g, m, v, *, name, block_bytes=1 << 20, comm=None):
    L, R, C = w.shape
    tr = max(8, min(R, (block_bytes // (4 * C)) // 8 * 8))
    while R % tr:
        tr -= 8
    if tr <= 0:
        tr = R
    c1 = 1.0 / (1.0 - ADAM_B1 ** ADAM_STEP)
    c2 = 1.0 / (1.0 - ADAM_B2 ** ADAM_STEP)

    def body(w_ref, g_ref, m_ref, v_ref, d_ref, mo_ref, vo_ref):
        gv = g_ref[...]
        mn = ADAM_B1 * m_ref[...] + (1.0 - ADAM_B1) * gv
        vn = ADAM_B2 * v_ref[...] + (1.0 - ADAM_B2) * (gv * gv)
        d_ref[...] = -ADAM_LR * ((mn * c1) / (jnp.sqrt(vn * c2) + ADAM_EPS) + ADAM_WD * w_ref[...])
        mo_ref[...] = mn
        vo_ref[...] = vn

    blk = pl.BlockSpec((None, tr, C), lambda l, i: (l, i, 0))
    sd = jax.ShapeDtypeStruct((L, R, C), F32)
    outs, comm_outs = _hosted_call(
        body, out_shape=(sd, sd, sd), grid=(L, R // tr), in_specs=[blk] * 4, out_specs=(blk,) * 3, scratch_shapes=[],
        name=name, sem=("parallel", "parallel"), operands=(w, g, m, v), comm=comm)
    return outs if comm is None else (outs, comm_outs)


def _row_tile(r, c, itemsize=4, block_bytes=1 << 20):
    tr = max(16, min(r, (block_bytes // (itemsize * c)) // 16 * 16))
    while r % tr:
        tr -= 16
    return tr if tr > 0 else r


def _add_sibling(g, recv, core, *, name):
    nch, _, r, c = g.shape
    tr = _row_tile(r, c)

    def body(core_ref, g_ref, r_ref, o_ref):
        o_ref[...] = (g_ref[...] + r_ref[...]).astype(BF16)

    grid_spec = pltpu.PrefetchScalarGridSpec(
        num_scalar_prefetch=1, grid=(nch, r // tr),
        in_specs=[pl.BlockSpec((None, None, tr, c), lambda j, i, cr: (j, cr[0], i, 0)),
                  pl.BlockSpec((None, tr, c), lambda j, i, cr: (j, i, 0))],
        out_specs=pl.BlockSpec((None, tr, c), lambda j, i, cr: (j, i, 0)))
    return pl.pallas_call(
        body, out_shape=jax.ShapeDtypeStruct((nch, r, c), BF16), grid_spec=grid_spec, name=name,
        compiler_params=_cp("parallel", "parallel"))(core, g, recv)


def _add_chips(slots, *, name):
    nch, r, c = slots.shape
    tr = _row_tile(r, c)

    def body(s_ref, o_ref):
        acc = s_ref[0].astype(F32)
        for j in range(1, nch):
            acc = acc + s_ref[j].astype(F32)
        o_ref[...] = acc

    return pl.pallas_call(
        body, out_shape=jax.ShapeDtypeStruct((r, c), F32), grid=(r // tr,),
        in_specs=[pl.BlockSpec((nch, tr, c), lambda i: (0, i, 0))],
        out_specs=pl.BlockSpec((tr, c), lambda i: (i, 0)), name=name, compiler_params=_cp("parallel"))(slots)


def _place():
    x, y, c = lax.axis_index("x"), lax.axis_index("y"), lax.axis_index("c")
    others = [(1 - x, y), (x, 1 - y), (1 - x, 1 - y)]
    return x, y, c, 2 * x + y, others


ANY = pl.BlockSpec(memory_space=pl.ANY)


class _Exchange:
    def __init__(self, kind, arrays):
        self.kind, self.ins = kind, list(arrays)
        self.n_peers = 1 if kind in ("swap", "join") else 3
        self.aliased = kind == "forward"
        n = len(self.ins) * self.n_peers
        shp = {"gather": lambda a: (N_CHIPS,) + a.shape, "scatter": lambda a: a.shape, "forward": lambda a: a.shape,
               "swap": lambda a: (a.shape[0],) + a.shape[2:], "join": lambda a: (2,) + a.shape}[kind]
        self.out_shapes = [jax.ShapeDtypeStruct(shp(a), a.dtype) for a in self.ins]
        self.sems = [pltpu.SemaphoreType.DMA((n,)), pltpu.SemaphoreType.DMA((n,))]

    def _copies(self, ins, outs, sems):
        send, recv = sems
        x, y, c, me, others = _place()
        over_ici = self.kind in ("gather", "scatter")
        peers = [(ox, oy, c) for ox, oy in others] if over_ici else [(x, y, 1 - c)] * self.n_peers
        for a in range(len(self.ins)):
            for k, to in enumerate(peers):
                peer = 2 * others[k][0] + others[k][1]
                if self.kind == "gather":
                    hr = self.ins[a].shape[0] // 2
                    rows = pl.ds(c * hr, hr)
                    src, dst, land = ins[a].at[rows, :], outs[a].at[me, rows, :], outs[a].at[peer, rows, :]
                elif self.kind == "forward":
                    hr = self.ins[a].shape[1] // 2
                    mine, theirs = pl.ds(c * hr, hr), pl.ds((1 - c) * hr, hr)
                    src, dst, land = ins[a].at[peer, mine, :], outs[a].at[peer, mine, :], outs[a].at[peer, theirs, :]
                elif self.kind == "scatter":
                    src, dst, land = ins[a].at[peer], outs[a].at[me], outs[a].at[peer]
                elif self.kind == "swap":
                    src, dst, land = ins[a].at[:, 1 - c], outs[a], outs[a]
                else:
                    src, dst, land = ins[a], outs[a].at[c], outs[a].at[1 - c]
                i = self.n_peers * a + k
                mk = lambda s, d: pltpu.make_async_remote_copy(
                    src_ref=s, dst_ref=d, send_sem=send.at[i], recv_sem=recv.at[i], device_id=to,
                    device_id_type=MESH)
                yield mk(src, dst), mk(land, land)

    def alias_pairs(self):
        return [(i, i) for i in range(len(self.ins))] if self.aliased else []

    def start(self, ins, outs, sems):
        for cp, _ in self._copies(ins, outs, sems):
            cp.start()

    def finish(self, ins, outs, sems):
        pairs = list(self._copies(ins, outs, sems))
        for _, landing in pairs:
            landing.wait_recv()
        for cp, _ in pairs:
            cp.wait_send()


class _Several:
    def __init__(self, parts):
        self.parts = list(parts)
        self.ins = [a for p in self.parts for a in p.ins]
        self.out_shapes = [s for p in self.parts for s in p.out_shapes]
        self.sems = [s for p in self.parts for s in p.sems]

    def split(self, ins, outs, sems=None):
        i = 0
        for k, p in enumerate(self.parts):
            n = len(p.ins)
            yield p, ins[i:i + n], outs[i:i + n], None if sems is None else sems[2 * k:2 * k + 2]
            i += n

    def alias_pairs(self):
        pairs, i = [], 0
        for p in self.parts:
            pairs += [(i + a, i + b) for a, b in p.alias_pairs()]
            i += len(p.ins)
        return pairs

    def start(self, ins, outs, sems):
        for p, a, b, s in self.split(ins, outs, sems):
            p.start(a, b, s)

    def finish(self, ins, outs, sems):
        for p, a, b, s in self.split(ins, outs, sems):
            p.finish(a, b, s)


def _hosted_call(body, *, grid, in_specs, out_specs, out_shape, scratch_shapes, operands, name, sem, comm=None):
    out_specs, out_shape = tuple(out_specs), tuple(out_shape)
    if isinstance(comm, (list, tuple)):
        several = _Several(comm)
        outs, comm_outs = _hosted_call(body, grid=grid, in_specs=in_specs, out_specs=out_specs, out_shape=out_shape,
                                       scratch_shapes=scratch_shapes, operands=operands, name=name, sem=sem,
                                       comm=several)
        return outs, [tuple(o) for _, _, o, _ in several.split(several.ins, comm_outs)]
    if comm is None:
        res = pl.pallas_call(body, out_shape=out_shape, grid=grid, in_specs=list(in_specs), out_specs=out_specs,
                             scratch_shapes=list(scratch_shapes), name=name, compiler_params=_cp(*sem))(*operands)
        return tuple(res), ()
    n_in, n_out, n_sc = len(in_specs), len(out_specs), len(scratch_shapes)
    ci, co = len(comm.ins), len(comm.out_shapes)

    def wrapped(*refs):
        ins, c_ins = refs[:n_in], refs[n_in:n_in + ci]
        outs = refs[n_in + ci:n_in + ci + n_out]
        c_outs = refs[n_in + ci + n_out:n_in + ci + n_out + co]
        scratch = refs[n_in + ci + n_out + co:n_in + ci + n_out + co + n_sc]
        sems = refs[n_in + ci + n_out + co + n_sc:]
        ids = [pl.program_id(d) for d in range(len(grid))]
        first = functools.reduce(jnp.logical_and, [i == 0 for i in ids])
        last = functools.reduce(jnp.logical_and, [i == g - 1 for i, g in zip(ids, grid)])

        @pl.when(first)
        def _():
            comm.start(c_ins, c_outs, sems)

        body(*ins, *outs, *scratch)

        @pl.when(last)
        def _():
            comm.finish(c_ins, c_outs, sems)

    res = pl.pallas_call(
        wrapped, out_shape=out_shape + tuple(comm.out_shapes), grid=grid, in_specs=list(in_specs) + [ANY] * ci,
        out_specs=out_specs + tuple([ANY] * co), scratch_shapes=list(scratch_shapes) + comm.sems, name=name,
        input_output_aliases={n_in + i: n_out + o for i, o in comm.alias_pairs()},
        compiler_params=pltpu.CompilerParams(dimension_semantics=("arbitrary",) * len(grid),
                                             vmem_limit_bytes=VMEM_LIMIT, has_side_effects=True),
    )(*operands, *comm.ins)
    return tuple(res[:n_out]), tuple(res[n_out:])


def _run_exchange(comm, *, name):
    ci = len(comm.ins)

    def body(*refs):
        ins, outs, sems = refs[:ci], refs[ci:2 * ci], refs[2 * ci:]
        comm.start(ins, outs, sems)
        comm.finish(ins, outs, sems)

    return pl.pallas_call(
        body, out_shape=tuple(comm.out_shapes), in_specs=[ANY] * ci, out_specs=tuple([ANY] * ci),
        scratch_shapes=comm.sems, name=name, input_output_aliases=dict(comm.alias_pairs()),
        compiler_params=pltpu.CompilerParams(has_side_effects=True))(*comm.ins)


def _own_slot(buf, piece, idx):
    return lax.dynamic_update_slice(buf, piece[None], (idx,) + (0,) * piece.ndim)


def _all_reduce_small(v, *, name):
    R = v.shape[0]

    flips = [(dx, dy, dc) for dx in range(2) for dy in range(2) for dc in range(2) if dx or dy or dc]

    def body(v_ref, o_ref, slots, send, recv):
        x, y, c, me, others = _place()
        mine = 2 * me + c
        slots[mine] = v_ref[...]

        def copy(k, slot):
            dx, dy, dc = flips[k]
            peer = (x + dx - 2 * x * dx, y + dy - 2 * y * dy, c + dc - 2 * c * dc)
            peer_slot = 4 * peer[0] + 2 * peer[1] + peer[2]
            return pltpu.make_async_remote_copy(
                src_ref=v_ref, dst_ref=slots.at[mine if slot == "mine" else peer_slot], send_sem=send.at[k],
                recv_sem=recv.at[k], device_id=peer, device_id_type=MESH)

        for k in range(7):
            copy(k, "mine").start()
        for k in range(7):
            copy(k, "peer").wait_recv()
        for k in range(7):
            copy(k, "mine").wait_send()
        acc = slots[0]
        for j in range(1, 8):
            acc = acc + slots[j]
        o_ref[...] = acc

    vm = pl.BlockSpec(memory_space=pltpu.VMEM)
    return pl.pallas_call(
        body, out_shape=jax.ShapeDtypeStruct(v.shape, F32), in_specs=[vm], out_specs=vm,
        scratch_shapes=[pltpu.VMEM((8, R, LANES), F32), pltpu.SemaphoreType.DMA((7,)),
                        pltpu.SemaphoreType.DMA((7,))],
        name=name, compiler_params=pltpu.CompilerParams(has_side_effects=True))(v)


def _rows(v, n_rows):
    v = v.reshape(-1).astype(F32)
    return jnp.pad(v, (0, n_rows * LANES - v.shape[0])).reshape(n_rows, LANES)


def _mlp_fwd(x_in, g, w_up, w_down, tag):
    h = _rms_fwd(x_in, g, name=f"{tag}_norm")
    u, a = _matmul(h, w_up, b_split=True, epilogue="sqrelu", name=f"{tag}_up")
    x_out = _matmul(a, w_down, epilogue="res", res=x_in, name=f"{tag}_down")
    return x_out, (h, u, a)


def _mlp_bwd(dy, dy_b, x_in, g, w_up, w_down, saved, tag, comms=None):
    h, u, a = saved
    comms = comms or {}
    landed = {}

    def mm(key, *args, **kw):
        comm = comms.get(key)
        if callable(comm):
            comm = comm(landed)
        out = _matmul(*args, name=f"{tag}_{key}", comm=comm, **kw)
        if comm is not None:
            out, landed[key] = out
        return out

    dw_down = mm("dwdown", a, dy_b, form="tn")
    du = mm("du", dy_b, w_down, form="nt", epilogue="sqrelu_bwd", u=u, out_dtype=BF16)
    dw_up = mm("dwup", h, du, form="tn", out_split=True)
    dh = mm("dh", du, w_up, form="nt", b_split=True)
    dx, dx_b, dg = _rms_bwd(x_in, g, dh, res=dy, bf16_copy=True, name=f"{tag}_dnorm")
    return dx, dx_b, dg, dw_up, dw_down, landed


def kernel(x, positions, ln_mix_g, ln_mlp_g, sf_w_in, sf_b_f, fox_q_g, fox_k_g, sf_w_o, mla_w_down, mla_q_a_g, mla_kv_a_g, mla_w_uq, mla_w_ukv, mla_q_g, mla_k_g, mla_w_o, mlp_w_up, mlp_w_down, loss_target, m_ln_mix_g, m_ln_mlp_g, m_sf_w_in, m_sf_b_f, m_fox_q_g, m_fox_k_g, m_sf_w_o, m_mla_w_down, m_mla_q_a_g, m_mla_kv_a_g, m_mla_w_uq, m_mla_w_ukv, m_mla_q_g, m_mla_k_g, m_mla_w_o, m_mlp_w_up, m_mlp_w_down, v_ln_mix_g, v_ln_mlp_g, v_sf_w_in, v_sf_b_f, v_fox_q_g, v_fox_k_g, v_sf_w_o, v_mla_w_down, v_mla_q_a_g, v_mla_kv_a_g, v_mla_w_uq, v_mla_w_ukv, v_mla_q_g, v_mla_k_g, v_mla_w_o, v_mlp_w_up, v_mlp_w_down):
    S, D = x.shape[1], x.shape[2]
    xs, tgt, pos = x[0], loss_target[0], positions[0]
    xi, yi, ci = lax.axis_index("x"), lax.axis_index("y"), lax.axis_index("c")
    chip = 2 * xi + yi
    core = ci.astype(jnp.int32).reshape(1)
    d_ff = mlp_w_up.shape[2] * N_CHIPS
    in_w = sf_w_in.shape[2] * N_CHIPS
    qkv_w = 3 * N_SB * HEAD + 3 * N_FOX * HEAD
    dn_w = mla_w_down.shape[2]
    dn_pad = Q_RANK + KV_RANK + LANES

    def gather_begin(ws):
        shards = [w.astype(BF16) for w in ws]
        return shards, _Exchange("gather", shards)

    def hand_over(landed):
        return _Exchange("forward", list(landed))

    def gather_end(both, shards):
        return [_own_slot(ag, s, chip) for ag, s in zip(both, shards)]

    cols = lambda ag: ag.transpose(1, 0, 2).reshape(ag.shape[1], -1)
    rows = lambda ag: ag.reshape(-1, ag.shape[2])
    s_mix0, ex_mix0 = gather_begin([sf_w_in[0], sf_w_o[0]])
    landed = _run_exchange(ex_mix0, name="gather_mix0")
    ag_in, ag_o0 = gather_end(_run_exchange(hand_over(landed), name="gather_mix0_sibling"), s_mix0)
    w_in_full = cols(ag_in)
    w_qkv = w_in_full[:, :qkv_w]
    w_f = jnp.pad(w_in_full[:, qkv_w:], ((0, 0), (0, LANES - (in_w - qkv_w))))
    w_o0 = rows(ag_o0)
    s_mlp0, ex_mlp0 = gather_begin([mlp_w_up[0], mlp_w_down[0]])
    s_mix1, ex_mix1 = gather_begin([mla_w_down[0], mla_w_uq[0], mla_w_ukv[0], mla_w_o[0]])
    s_mlp1, ex_mlp1 = gather_begin([mlp_w_up[1], mlp_w_down[1]])

    gain_blk = jnp.concatenate([mla_q_a_g, mla_kv_a_g], axis=0) * (ci == 0).astype(F32)
    placed = jnp.zeros((2, N_CHIPS, LANES), F32)
    placed = lax.dynamic_update_slice(placed, gain_blk[:, None, :], (0, chip, 0))
    gains = _all_reduce_small(placed.reshape(2 * N_CHIPS, LANES), name="gather_gains")
    q_a_full = gains[:N_CHIPS].reshape(Q_RANK)
    kv_a_full = gains[N_CHIPS:].reshape(KV_RANK)

    pad_gain = lambda g: jnp.pad(g.reshape(1, QK_DIM), ((0, 0), (0, QK_PAD - QK_DIM)))
    q_g_pad, k_g_pad = pad_gain(mla_q_g), pad_gain(mla_k_g)
    b_pad = _rows(sf_b_f, 1)

    h0 = _rms_fwd(xs, ln_mix_g[0], name="mix0_norm")
    qkv_sb = _matmul(h0, w_qkv, n=3 * N_SB * HEAD, b_n0=0, out_dtype=BF16, name="mix0_qkv_sb")
    qk_fx = _matmul(h0, w_qkv, n=2 * N_FOX * HEAD, b_n0=3 * N_SB * HEAD, name="mix0_qk_fox")
    v_fx = _matmul(h0, w_qkv, n=N_FOX * HEAD, b_n0=(3 * N_SB + 2 * N_FOX) * HEAD, out_dtype=BF16,
                   name="mix0_v_fox")
    fl = _matmul(h0, w_f, name="mix0_forget_logit")
    f_cum = _forget_fwd(fl, b_pad, name="forget_fwd")
    neg_f = (-f_cum[:, :N_FOX]).T.reshape(N_FOX, 1, S)
    q_f = _rms_fwd(qk_fx, fox_q_g[0], c0=0, width=N_FOX * HEAD, gw=HEAD, name="fox_q_norm")
    k_f = _rms_fwd(qk_fx, fox_k_g[0], c0=N_FOX * HEAD, width=N_FOX * HEAD, gw=HEAD, name="fox_k_norm")
    (o_sb, t_sb), landed = _sb_fwd(qkv_sb, name="sb_fwd", comm=ex_mlp0, tq=1024)
    (o_fx, lse0), (landed, both) = _attn_fwd(q_f, k_f, v_fx, neg_f, n_heads=N_FOX, dqk=HEAD, scale=HEAD ** -0.5,
                                             exact_p=True, name="fox_fwd", comm=[ex_mix1, hand_over(landed)])
    ag_up0, ag_dw0 = gather_end(both, s_mlp0)
    o0 = jnp.concatenate([o_sb, o_fx], axis=1)
    x1, both = _matmul(o0, w_o0, epilogue="res", res=xs, name="mix0_out", comm=hand_over(landed))
    ag_dn, ag_uq, ag_ukv, ag_o1 = gather_end(both, s_mix1)
    w_dn = jnp.pad(rows(ag_dn), ((0, 0), (0, dn_pad - dn_w)))
    w_uq = jnp.pad(cols(ag_uq).reshape(Q_RANK, N_MLA, QK_DIM), ((0, 0), (0, 0), (0, QK_PAD - QK_DIM)))
    w_uq = w_uq.reshape(Q_RANK, N_MLA * QK_PAD)
    w_ukv = cols(ag_ukv)
    w_o1 = rows(ag_o1)
    x2, mlp0 = _mlp_fwd(x1, ln_mlp_g[0], ag_up0, rows(ag_dw0), "mlp0")

    h2 = _rms_fwd(x2, ln_mix_g[1], name="mix1_norm")
    down = _matmul(h2, w_dn, name="mix1_down")
    c_q = _rms_fwd(down, q_a_full, c0=0, width=Q_RANK, name="mix1_q_a_norm")
    c_kv = _rms_fwd(down, kv_a_full, c0=Q_RANK, width=KV_RANK, name="mix1_kv_a_norm")
    q_raw = _matmul(c_q, w_uq, name="mix1_uq")
    kv_raw = _matmul(c_kv, w_ukv, name="mix1_ukv")
    half = ROPE // 2
    inv_freq = ROPE_THETA ** (-jnp.arange(half, dtype=F32) / half)
    ang = pos.astype(F32)[:, None] * inv_freq
    table = lambda t: jnp.pad(jnp.concatenate([t, t], axis=1), ((0, 0), (0, LANES - ROPE)))
    cos_t, sin_t = table(jnp.cos(ang)), table(jnp.sin(ang))
    q_pad, k_pad, v1 = _mla_prep_fwd(q_raw, kv_raw, down, cos_t, sin_t, q_g_pad, k_g_pad, name="mla_prep_fwd")
    (o1, lse1), landed = _attn_fwd(q_pad, k_pad, v1, None, n_heads=N_MLA, dqk=QK_PAD, scale=QK_DIM ** -0.5,
                                   name="mla_fwd", comm=ex_mlp1)
    x3, both = _matmul(o1, w_o1, epilogue="res", res=x2, name="mix1_out", comm=hand_over(landed))
    ag_up1, ag_dw1 = gather_end(both, s_mlp1)
    w_up = [ag_up0, ag_up1]
    w_dw = [rows(ag_dw0), rows(ag_dw1)]
    x4, mlp1 = _mlp_fwd(x3, ln_mlp_g[1], w_up[1], w_dw[1], "mlp1")

    dx4, dx4_b, loss_local = _loss_head(x4, tgt, name="loss_head")
    loss = lax.psum(loss_local, ("x", "y", "c"))

    by_cols = lambda g: g.reshape(g.shape[0], N_CHIPS, -1).transpose(1, 0, 2)
    by_rows = lambda g: g.reshape(N_CHIPS, g.shape[0] // N_CHIPS, g.shape[1])
    halves = lambda g: g.reshape(N_CHIPS, 2, g.shape[1] // 2, g.shape[2])

    def scatter_of(grads, from_sibling, tags):
        parts = [_add_sibling(g, r, core, name=f"add_sibling_{t}") for g, r, t in zip(grads, from_sibling, tags)]
        return parts, _Exchange("scatter", parts)

    def sums_of(slots, parts, tags):
        slots = [_own_slot(s, lax.dynamic_index_in_dim(p, chip, 0, keepdims=False), chip)
                 for s, p in zip(slots, parts)]
        return [_add_chips(s, name=f"add_chips_{t}") for s, t in zip(slots, tags)]

    def shards_of(joined, mine):
        return [_own_slot(j, m, ci).reshape(2 * m.shape[0], m.shape[1]) for j, m in zip(joined, mine)]

    dx3, dx3_b, dg_mlp1, dw_up1, dw_dw1, _ = _mlp_bwd(dx4, dx4_b, x3, ln_mlp_g[1], w_up[1], w_dw[1], mlp1, "mlp1")
    tags_mlp1 = ["w_up1", "w_dw1"]
    g_mlp1 = [halves(dw_up1), halves(by_rows(dw_dw1))]

    dw_o1 = _matmul(o1, dx3_b, form="tn", name="mix1_dwo")
    do1 = _matmul(dx3_b, w_o1, form="nt", name="mix1_do")
    (dq_pad, dk_pad, dv1), from_sibling = _attn_bwd(
        q_pad, k_pad, v1, None, o1, lse1, do1, n_heads=N_MLA, dqk=QK_PAD, scale=QK_DIM ** -0.5, name="mla_bwd",
        comm=_Exchange("swap", g_mlp1))
    p_mlp1, _ = scatter_of(g_mlp1, from_sibling, tags_mlp1)
    (dq_raw, dkv_raw, dpe, dg_q, dg_k), sl_up1 = _mla_prep_bwd(
        dq_pad, dk_pad, dv1, q_raw, kv_raw, down, cos_t, sin_t, q_g_pad, k_g_pad, name="mla_prep_bwd",
        comm=_Exchange("scatter", p_mlp1[:1]))
    dw_uq = _matmul(c_q, dq_raw, form="tn", name="mix1_dwuq")
    dc_q = _matmul(dq_raw, w_uq, form="nt", name="mix1_dcq")
    dw_ukv = _matmul(c_kv, dkv_raw, form="tn", name="mix1_dwukv")
    dc_kv = _matmul(dkv_raw, w_ukv, form="nt", name="mix1_dckv")
    d_cq, dg_qa = _rms_bwd(down, q_a_full, dc_q, c0=0, width=Q_RANK, name="mix1_q_a_dnorm")
    d_ckv, dg_kva = _rms_bwd(down, kv_a_full, dc_kv, c0=Q_RANK, width=KV_RANK, name="mix1_kv_a_dnorm")
    d_down = jnp.concatenate([d_cq, d_ckv, dpe], axis=1)
    dw_dn = _matmul(h2, d_down, form="tn", name="mix1_dwdown")
    dh2 = _matmul(d_down, w_dn, form="nt", name="mix1_dh")
    dx2, dx2_b, dg_mix1 = _rms_bwd(x2, ln_mix_g[1], dh2, res=dx3, bf16_copy=True, name="mix1_dnorm")
    g_uq = dw_uq.reshape(Q_RANK, N_MLA, QK_PAD)[:, :, :QK_DIM].reshape(Q_RANK, N_MLA * QK_DIM)
    tags_mix1 = ["w_dn", "w_uq", "w_ukv", "w_o1"]
    g_mix1 = [halves(by_rows(dw_dn[:, :dn_w])), halves(by_cols(g_uq)), halves(by_cols(dw_ukv)),
              halves(by_rows(dw_o1))]

    p_mix1, mine_mlp1 = [], []

    def scatter_mix1(landed):
        parts, ex = scatter_of(g_mix1, landed["dwdown"][1], tags_mix1)
        p_mix1.extend(parts)
        return ex

    def join_mlp1(landed):
        mine_mlp1.extend(sums_of(sl_up1 + landed["dwdown"][0], p_mlp1, tags_mlp1))
        return _Exchange("join", mine_mlp1)

    dx1, dx1_b, dg_mlp0, dw_up0, dw_dw0, landed = _mlp_bwd(
        dx2, dx2_b, x1, ln_mlp_g[0], w_up[0], w_dw[0], mlp0, "mlp0",
        comms={"dwdown": [_Exchange("scatter", p_mlp1[1:]), _Exchange("swap", g_mix1)],
               "du": scatter_mix1, "dwup": join_mlp1})
    gs_up1, gs_dw1 = shards_of(landed["dwup"], mine_mlp1)
    mine_mix1 = sums_of(landed["du"], p_mix1, tags_mix1)
    tags_mlp0 = ["w_up0", "w_dw0"]
    g_mlp0 = [halves(dw_up0), halves(by_rows(dw_dw0))]

    dw_o0, joined = _matmul(o0, dx1_b, form="tn", name="mix0_dwo", comm=_Exchange("join", mine_mix1))
    gs_dn, gs_uq, gs_ukv, gs_o1 = shards_of(joined, mine_mix1)
    g_o0 = [halves(by_rows(dw_o0))]
    do0 = _matmul(dx1_b, w_o0, form="nt", name="mix0_do")
    (dq_f, dk_f, dv_fx, dbias, drow), (fs_mlp0, fs_o0) = _attn_bwd(
        q_f, k_f, v_fx, neg_f, o_fx, lse0, do0, n_heads=N_FOX, dqk=HEAD, scale=HEAD ** -0.5, do_off=N_SB,
        name="fox_bwd", comm=[_Exchange("swap", g_mlp0), _Exchange("swap", g_o0)])
    p_mlp0, ex_a = scatter_of(g_mlp0, fs_mlp0, tags_mlp0)
    p_o0, ex_b = scatter_of(g_o0, fs_o0, ["w_o0"])
    (dq_sb, dk_sb, dv_sb), (sl_mlp0, sl_o0) = _sb_bwd(qkv_sb, do0, t_sb, do_off=0, name="sb_bwd", comm=[ex_a, ex_b])
    mine_mlp0 = sums_of(sl_mlp0, p_mlp0, tags_mlp0)
    mine_o0 = sums_of(sl_o0, p_o0, ["w_o0"])
    dq_fx, dg_fq = _rms_bwd(qk_fx, fox_q_g[0], dq_f, c0=0, width=N_FOX * HEAD, gw=HEAD, name="fox_q_dnorm")
    dk_fx, dg_fk = _rms_bwd(qk_fx, fox_k_g[0], dk_f, c0=N_FOX * HEAD, width=N_FOX * HEAD, gw=HEAD,
                            name="fox_k_dnorm")
    d_fcum = jnp.pad((jnp.max(drow, axis=-1) - dbias.reshape(N_FOX, S)).T, ((0, 0), (0, LANES - N_FOX)))
    dfl, db_f = _forget_bwd(fl, b_pad, d_fcum, name="forget_bwd")
    dproj = jnp.concatenate([dq_sb, dk_sb, dv_sb, dq_fx, dk_fx, dv_fx], axis=1).astype(BF16)
    dw_qkv, (j_mlp0, j_o0) = _matmul(h0, dproj, form="tn", name="mix0_dwqkv",
                                     comm=[_Exchange("join", mine_mlp0), _Exchange("join", mine_o0)])
    gs_up0, gs_dw0 = shards_of(j_mlp0, mine_mlp0)
    gs_o0, = shards_of(j_o0, mine_o0)
    dw_f = _matmul(h0, dfl, form="tn", name="mix0_dwf")
    g_in = [halves(by_cols(jnp.concatenate([dw_qkv, dw_f[:, :in_w - qkv_w]], axis=1)))]
    dh0 = _matmul(dfl, w_f, form="nt", name="mix0_dh_f")
    dh0, from_sibling = _matmul(dproj, w_qkv, form="nt", epilogue="res", res=dh0, name="mix0_dh",
                                comm=_Exchange("swap", g_in))
    p_in, ex_in_grads = scatter_of(g_in, from_sibling, ["w_in"])
    grad_x, dg_mix0 = _rms_bwd(xs, ln_mix_g[0], dh0, res=dx1, name="mix0_dnorm")
    gs_up = jnp.concatenate([gs_up0, gs_up1], axis=0)
    gs_dw = jnp.concatenate([gs_dw0, gs_dw1], axis=0)

    ln_rows = D // LANES
    small = jnp.concatenate([
        _rows(dg_mix0, ln_rows), _rows(dg_mix1, ln_rows), _rows(dg_mlp0, ln_rows), _rows(dg_mlp1, ln_rows),
        _rows(db_f, 8), _rows(dg_fq, 8), _rows(dg_fk, 8), _rows(dg_qa, 8), _rows(dg_kva, 8), _rows(dg_q, 8),
        _rows(dg_k, 8)], axis=0)
    small = _all_reduce_small(small, name="reduce_small")
    flat = lambda r0, nr, n: small[r0:r0 + nr].reshape(-1)[:n]
    r0 = 4 * ln_rows
    g_ln_mix = jnp.stack([flat(0, ln_rows, D), flat(ln_rows, ln_rows, D)])
    g_ln_mlp = jnp.stack([flat(2 * ln_rows, ln_rows, D), flat(3 * ln_rows, ln_rows, D)])
    g_b_f = flat(r0, 8, N_FOX)[None]
    g_fq, g_fk = flat(r0 + 8, 8, HEAD)[None], flat(r0 + 16, 8, HEAD)[None]
    g_qa = lax.dynamic_slice(flat(r0 + 24, 8, Q_RANK), (chip * LANES,), (LANES,))[None]
    g_kva = lax.dynamic_slice(flat(r0 + 32, 8, KV_RANK), (chip * LANES,), (LANES,))[None]
    g_q, g_k = flat(r0 + 40, 8, QK_DIM)[None], flat(r0 + 48, 8, QK_DIM)[None]

    def pack_small(ln_mix, ln_mlp, *rest):
        return jnp.concatenate([_rows(ln_mix, 2 * ln_rows), _rows(ln_mlp, 2 * ln_rows)] + [_rows(t, 8) for t in rest],
                               axis=0)

    def unpack_small(p):
        f = lambda r, nr, shape: p[r:r + nr].reshape(-1)[:int(np.prod(shape))].reshape(shape)
        shapes = [(1, N_FOX), (1, HEAD), (1, HEAD), (1, LANES), (1, LANES), (1, QK_DIM), (1, QK_DIM)]
        return (f(0, 2 * ln_rows, (2, D)), f(2 * ln_rows, 2 * ln_rows, (2, D)),
                *[f(r0 + 8 * i, 8, shp) for i, shp in enumerate(shapes)])

    small_out = _adamw(
        pack_small(ln_mix_g, ln_mlp_g, sf_b_f, fox_q_g, fox_k_g, mla_q_a_g, mla_kv_a_g, mla_q_g, mla_k_g)[None],
        pack_small(g_ln_mix, g_ln_mlp, g_b_f, g_fq, g_fk, g_qa, g_kva, g_q, g_k)[None],
        pack_small(m_ln_mix_g, m_ln_mlp_g, m_sf_b_f, m_fox_q_g, m_fox_k_g, m_mla_q_a_g, m_mla_kv_a_g, m_mla_q_g,
                   m_mla_k_g)[None],
        pack_small(v_ln_mix_g, v_ln_mlp_g, v_sf_b_f, v_fox_q_g, v_fox_k_g, v_mla_q_a_g, v_mla_kv_a_g, v_mla_q_g,
                   v_mla_k_g)[None], name="adamw_small")
    d_small, m_small, v_small = [unpack_small(p[0]) for p in small_out]

    def big(w, g, m, v, tag, comm=None):
        g = g.reshape(w.shape)
        out = _adamw(w, g, m, v, name=f"adamw_{tag}", comm=comm)
        (d, mn, vn), landed = out if comm is not None else (out, None)
        return (g, d, mn, vn) if comm is None else ((g, d, mn, vn), landed)

    r_up, slots = big(mlp_w_up, gs_up, m_mlp_w_up, v_mlp_w_up, "w_up", comm=ex_in_grads)
    mine_in = sums_of(slots, p_in, ["w_in"])
    gs_in, = shards_of(_run_exchange(_Exchange("join", mine_in), name="reduce_w_in_join"), mine_in)
    r_dw = big(mlp_w_down, gs_dw, m_mlp_w_down, v_mlp_w_down, "w_dw")
    r_in = big(sf_w_in, gs_in, m_sf_w_in, v_sf_w_in, "w_in")
    r_o0 = big(sf_w_o, gs_o0, m_sf_w_o, v_sf_w_o, "w_o0")
    r_dn = big(mla_w_down, gs_dn, m_mla_w_down, v_mla_w_down, "w_dn")
    r_uq = big(mla_w_uq, gs_uq, m_mla_w_uq, v_mla_w_uq, "w_uq")
    r_ukv = big(mla_w_ukv, gs_ukv, m_mla_w_ukv, v_mla_w_ukv, "w_ukv")
    r_o1 = big(mla_w_o, gs_o1, m_mla_w_o, v_mla_w_o, "w_o1")

    g_small = (g_ln_mix, g_ln_mlp, g_b_f, g_fq, g_fk, g_qa, g_kva, g_q, g_k)

    def ordered(k, sm):
        return (sm[0], sm[1], r_in[k], sm[2], sm[3], sm[4], r_o0[k], r_dn[k], sm[5], sm[6], r_uq[k], r_ukv[k],
                sm[7], sm[8], r_o1[k], r_up[k], r_dw[k])

    return (loss, grad_x[None], *ordered(0, g_small), *ordered(1, d_small), *ordered(2, m_small),
            *ordered(3, v_small))
```

```python
import functools

import numpy as np
import jax
import jax.numpy as jnp
from jax import lax
from jax.experimental import pallas as pl
from jax.experimental.pallas import tpu as pltpu

F32 = jnp.float32
BF16 = jnp.bfloat16
MESH = pl.DeviceIdType.MESH

EPS = 1e-6
HEAD = 128
N_SB = 8
N_FOX = 8
N_MLA = 16
Q_RANK = 512
KV_RANK = 512
NOPE = 128
ROPE = 64
QK_DIM = NOPE + ROPE
QK_PAD = 256
ROPE_THETA = 10000.0
N_CHIPS = 4

ADAM_LR = 0.001
ADAM_B1 = 0.9
ADAM_B2 = 0.999
ADAM_EPS = 1e-08
ADAM_WD = 0.01
ADAM_STEP = 10

VMEM_LIMIT = 56 * 1024 * 1024
LANES = 128
NEG = -1e30


def _cp(*sem):
    return pltpu.CompilerParams(dimension_semantics=sem, vmem_limit_bytes=VMEM_LIMIT)


def _pick(dim, target):
    if dim <= target:
        return dim
    t = (target // LANES) * LANES
    while t >= LANES:
        if dim % t == 0:
            return t
        t -= LANES
    raise ValueError(f"no tile for {dim}")


NT_DIMS = (((1,), (1,)), ((), ()))
TN_DIMS = (((0,), (0,)), ((), ()))


def _dot(a, b):
    return jnp.dot(a, b, preferred_element_type=F32)


def _dot_nt(a, b):
    return lax.dot_general(a, b, NT_DIMS, preferred_element_type=F32)


def _dot_tn(a, b):
    return lax.dot_general(a, b, TN_DIMS, preferred_element_type=F32)


def _matmul(a, b, *, name, form="nn", out_dtype=F32, n=None, b_n0=0, b_split=False,
            out_split=False, epilogue="plain", res=None, u=None, tm=1024, tn=1024, tk=2048, comm=None):
    if form == "tn":
        K, M = a.shape
    else:
        M, K = a.shape
    if b_split:
        if form == "nt":
            nb_full, kb_full = b.shape[1], b.shape[2] * N_CHIPS
        else:
            kb_full, nb_full = b.shape[1], b.shape[2] * N_CHIPS
    elif form == "nt":
        nb_full, kb_full = b.shape
    else:
        kb_full, nb_full = b.shape
    assert kb_full == K, (name, a.shape, b.shape)
    N = nb_full if n is None else n
    if a.dtype != BF16 or b.dtype != BF16:
        tk = max(tk // 2, LANES)
    tm, tn, tk = _pick(M, tm), _pick(N, tn), _pick(K, tk)
    if b_split:
        per_chip = (b.shape[2])
        if form == "nt":
            tk = _pick(per_chip, tk)
        else:
            tn = _pick(per_chip, tn)
    if out_split:
        tn = _pick(N // N_CHIPS, tn)
    assert b_n0 % tn == 0
    nb0 = b_n0 // tn
    nk = K // tk
    grid = (M // tm, N // tn, nk)

    if form == "tn":
        a_spec = pl.BlockSpec((tk, tm), lambda i, j, k: (k, i))
    else:
        a_spec = pl.BlockSpec((tm, tk), lambda i, j, k: (i, k))
    if b_split:
        if form == "nt":
            kc = b.shape[2] // tk
            b_spec = pl.BlockSpec((None, tn, tk), lambda i, j, k: (k // kc, j, k % kc))
        else:
            nc = b.shape[2] // tn
            b_spec = pl.BlockSpec((None, tk, tn), lambda i, j, k: (j // nc, k, j % nc))
    elif form == "nt":
        b_spec = pl.BlockSpec((tn, tk), lambda i, j, k: (j + nb0, k))
    else:
        b_spec = pl.BlockSpec((tk, tn), lambda i, j, k: (k, j + nb0))
    mn_spec = pl.BlockSpec((tm, tn), lambda i, j, k: (i, j))
    if out_split:
        oc = (N // N_CHIPS) // tn
        out_spec = pl.BlockSpec((None, tm, tn), lambda i, j, k: (j // oc, i, j % oc))
        out_shape = jax.ShapeDtypeStruct((N_CHIPS, M, N // N_CHIPS), out_dtype)
    else:
        out_spec = mn_spec
        out_shape = jax.ShapeDtypeStruct((M, N), out_dtype)

    in_specs = [a_spec, b_spec]
    operands = [a, b]
    out_specs = (out_spec,)
    out_shape = (out_shape,)
    if epilogue == "res":
        in_specs.append(mn_spec)
        operands.append(res)
    elif epilogue == "sqrelu_bwd":
        in_specs.append(mn_spec)
        operands.append(u)
    elif epilogue == "sqrelu":
        out_specs = (mn_spec, mn_spec)
        out_shape = (jax.ShapeDtypeStruct((M, N), F32), jax.ShapeDtypeStruct((M, N), BF16))

    def finish(refs, r):
        if epilogue == "plain":
            refs[2][...] = r.astype(out_dtype)
        elif epilogue == "res":
            refs[3][...] = (refs[2][...] + r).astype(out_dtype)
        elif epilogue == "sqrelu":
            refs[2][...] = r
            p = jnp.maximum(r, 0.0)
            refs[3][...] = (p * p).astype(BF16)
        else:
            refs[3][...] = (r * (2.0 * jnp.maximum(refs[2][...], 0.0))).astype(out_dtype)

    def body(*refs):
        at = refs[0][...].astype(BF16)
        bt = refs[1][...].astype(BF16)
        if form == "nn":
            part = _dot(at, bt)
        elif form == "nt":
            part = _dot_nt(at, bt)
        else:
            part = _dot_tn(at, bt)
        if nk == 1:
            finish(refs, part)
            return
        acc = refs[-1]
        k = pl.program_id(2)

        @pl.when(k == 0)
        def _():
            acc[...] = part

        @pl.when(jnp.logical_and(k > 0, k < nk - 1))
        def _():
            acc[...] += part

        @pl.when(k == nk - 1)
        def _():
            finish(refs, acc[...] + part)

    outs, comm_outs = _hosted_call(
        body, grid=grid, in_specs=in_specs, out_specs=out_specs, out_shape=out_shape,
        scratch_shapes=[] if nk == 1 else [pltpu.VMEM((tm, tn), F32)], operands=operands, name=name,
        sem=("parallel", "parallel", "arbitrary"), comm=comm)
    result = outs if epilogue == "sqrelu" else outs[0]
    return result if comm is None else (result, comm_outs)


def _rms_fwd(x, g, *, name, c0=0, width=None, gw=None, tr=256):
    R, ctot = x.shape
    C = ctot if width is None else width
    gw = C if gw is None else gw
    assert c0 % C == 0 and C % gw == 0
    tr = _pick(R, tr)
    cb = c0 // C
    ng = C // gw

    def body(x_ref, g_ref, o_ref):
        gv = g_ref[...]
        for gi in range(ng):
            cols = slice(gi * gw, (gi + 1) * gw)
            xs = x_ref[:, cols]
            ms = jnp.sum(xs * xs, axis=-1, keepdims=True) * (1.0 / gw)
            o_ref[:, cols] = ((xs * lax.rsqrt(ms + EPS)) * gv).astype(o_ref.dtype)

    return pl.pallas_call(
        body, out_shape=jax.ShapeDtypeStruct((R, C), BF16), grid=(R // tr,),
        in_specs=[pl.BlockSpec((tr, C), lambda i: (i, cb)), pl.BlockSpec((1, gw), lambda i: (0, 0))],
        out_specs=pl.BlockSpec((tr, C), lambda i: (i, 0)), name=name,
        compiler_params=_cp("parallel"))(x, g.reshape(1, gw).astype(F32))


def _rms_bwd(x, g, dy, *, name, res=None, c0=0, width=None, gw=None, tr=256, bf16_copy=False):
    bf16_copy = int(bf16_copy)
    R, ctot = x.shape
    C = ctot if width is None else width
    gw = C if gw is None else gw
    tr = _pick(R, tr)
    cb = c0 // C
    ng = C // gw
    nsteps = R // tr
    row_spec = pl.BlockSpec((tr, C), lambda i: (i, 0))
    in_specs = [pl.BlockSpec((tr, C), lambda i: (i, cb)), pl.BlockSpec((1, gw), lambda i: (0, 0)), row_spec]
    operands = [x, g.reshape(1, gw).astype(F32), dy]
    if res is not None:
        in_specs.append(row_spec)
        operands.append(res)

    def body(*refs):
        x_ref, g_ref, dy_ref = refs[:3]
        res_ref = refs[3] if res is not None else None
        dx_ref, dg_ref = refs[-3 - bf16_copy], refs[-2]
        acc = refs[-1]
        i = pl.program_id(0)

        @pl.when(i == 0)
        def _():
            acc[...] = jnp.zeros_like(acc)

        gv = g_ref[...]
        for gi in range(ng):
            cols = slice(gi * gw, (gi + 1) * gw)
            xs = x_ref[:, cols]
            dys = dy_ref[:, cols].astype(F32)
            rstd = lax.rsqrt(jnp.sum(xs * xs, axis=-1, keepdims=True) * (1.0 / gw) + EPS)
            xh = xs * rstd
            gdy = dys * gv
            m = jnp.sum(gdy * xh, axis=-1, keepdims=True) * (1.0 / gw)
            dx = rstd * (gdy - xh * m)
            if res_ref is not None:
                dx = dx + res_ref[:, cols]
            dx_ref[:, cols] = dx
            if bf16_copy:
                refs[-3][:, cols] = dx.astype(BF16)
            acc[...] += jnp.sum((dys * xh).reshape(tr // 8, 8, gw), axis=0)

        @pl.when(i == nsteps - 1)
        def _():
            dg_ref[...] = jnp.sum(acc[...], axis=0, keepdims=True)

    out_shape = [jax.ShapeDtypeStruct((R, C), F32)] + [jax.ShapeDtypeStruct((R, C), BF16)] * bf16_copy
    outs = pl.pallas_call(
        body, out_shape=tuple(out_shape + [jax.ShapeDtypeStruct((1, gw), F32)]),
        grid=(nsteps,), in_specs=in_specs,
        out_specs=tuple([row_spec] * len(out_shape) + [pl.BlockSpec((1, gw), lambda i: (0, 0))]),
        scratch_shapes=[pltpu.VMEM((8, gw), F32)], name=name,
        compiler_params=_cp("arbitrary"))(*operands)
    return (*outs[:-1], outs[-1][0])


def _split3(x):
    hi = x.astype(BF16)
    r1 = x - hi.astype(F32)
    mid = r1.astype(BF16)
    lo = (r1 - mid.astype(F32)).astype(BF16)
    return hi, mid, lo


def _log_sigmoid(z):
    return jnp.minimum(z, 0.0) - jnp.log(1.0 + jnp.exp(-jnp.abs(z)))


def _forget_fwd(fl, b, *, name, tb=512):
    S = fl.shape[0]
    tb = _pick(S, tb)

    def body(fl_ref, b_ref, f_ref, carry):
        i = pl.program_id(0)

        @pl.when(i == 0)
        def _():
            carry[...] = jnp.zeros_like(carry)

        lf = _log_sigmoid(fl_ref[...] + b_ref[...])
        r = lax.broadcasted_iota(jnp.int32, (tb, tb), 0)
        c = lax.broadcasted_iota(jnp.int32, (tb, tb), 1)
        tri = (c <= r).astype(BF16)
        hi, mid, lo = _split3(lf)
        cs = _dot(tri, hi) + _dot(tri, mid) + _dot(tri, lo)
        f_ref[...] = cs + carry[...]
        carry[...] += jnp.sum(lf, axis=0, keepdims=True)

    return pl.pallas_call(
        body, out_shape=jax.ShapeDtypeStruct((S, LANES), F32), grid=(S // tb,),
        in_specs=[pl.BlockSpec((tb, LANES), lambda i: (i, 0)), pl.BlockSpec((1, LANES), lambda i: (0, 0))],
        out_specs=pl.BlockSpec((tb, LANES), lambda i: (i, 0)),
        scratch_shapes=[pltpu.VMEM((1, LANES), F32)], name=name,
        compiler_params=_cp("arbitrary"))(fl, b)


def _forget_bwd(fl, b, dF, *, name, tb=512):
    S = fl.shape[0]
    tb = _pick(S, tb)
    nb = S // tb

    def body(fl_ref, b_ref, df_ref, dfl_ref, db_ref, carry, acc):
        i = pl.program_id(0)

        @pl.when(i == 0)
        def _():
            carry[...] = jnp.zeros_like(carry)
            acc[...] = jnp.zeros_like(acc)

        d = df_ref[...]
        r = lax.broadcasted_iota(jnp.int32, (tb, tb), 0)
        c = lax.broadcasted_iota(jnp.int32, (tb, tb), 1)
        tri = (c >= r).astype(BF16)
        hi, mid, lo = _split3(d)
        rc = _dot(tri, hi) + _dot(tri, mid) + _dot(tri, lo) + carry[...]
        z = fl_ref[...] + b_ref[...]
        dfl = rc * jnp.exp(_log_sigmoid(-z))
        dfl_ref[...] = dfl
        carry[...] += jnp.sum(d, axis=0, keepdims=True)
        acc[...] += jnp.sum(dfl, axis=0, keepdims=True)

        @pl.when(i == nb - 1)
        def _():
            db_ref[...] = acc[...]

    rev = lambda i: (nb - 1 - i, 0)
    dfl, db = pl.pallas_call(
        body, out_shape=(jax.ShapeDtypeStruct((S, LANES), F32), jax.ShapeDtypeStruct((1, LANES), F32)),
        grid=(nb,),
        in_specs=[pl.BlockSpec((tb, LANES), rev), pl.BlockSpec((1, LANES), lambda i: (0, 0)),
                  pl.BlockSpec((tb, LANES), rev)],
        out_specs=(pl.BlockSpec((tb, LANES), rev), pl.BlockSpec((1, LANES), lambda i: (0, 0))),
        scratch_shapes=[pltpu.VMEM((1, LANES), F32), pltpu.VMEM((1, LANES), F32)], name=name,
        compiler_params=_cp("arbitrary"))(fl, b, dF)
    return dfl, db[0]


def _tri(tk, rel):
    r = lax.broadcasted_iota(jnp.int32, (tk, tk), 0)
    c = lax.broadcasted_iota(jnp.int32, (tk, tk), 1)
    m = {"gt": r > c, "le": r <= c, "lt": r < c}[rel]
    return m.astype(BF16)


def _split2(x):
    hi = x.astype(BF16)
    return hi, (x - hi.astype(F32)).astype(BF16)


HEADS_PER_STEP = 2
CUM_CHUNK = 256


def _cum_cols(x, tri, suffix):
    ck = tri.shape[0]
    n = x.shape[1] // ck
    hi, lo = _split2(x)
    parts, sums = [], []
    for c in range(n):
        cs = slice(c * ck, (c + 1) * ck)
        parts.append(_dot(hi[:, cs], tri) + _dot(lo[:, cs], tri))
        sums.append(jnp.sum(x[:, cs], axis=1, keepdims=True))
    carry = None
    for c in (reversed(range(n)) if suffix else range(n)):
        if carry is not None:
            parts[c] = parts[c] + carry
        carry = sums[c] if carry is None else carry + sums[c]
    return (parts[0] if n == 1 else jnp.concatenate(parts, axis=1)), carry


def _diag_mask(tq, strict):
    r = lax.broadcasted_iota(jnp.int32, (tq, tq), 0)
    c = lax.broadcasted_iota(jnp.int32, (tq, tq), 1)
    return c < r if strict else c <= r


def _sb_fwd(qkv, *, name, n_heads=N_SB, q_off=0, k_off=N_SB, v_off=2 * N_SB, tq=512, hp=HEADS_PER_STEP,
            comm=None):
    S = qkv.shape[0]
    tq = _pick(S, tq)
    tk = tq
    scale = HEAD ** -0.5
    nq = S // tq
    assert n_heads % hp == 0 and q_off % hp == 0 and k_off % hp == 0 and v_off % hp == 0

    def body(q_ref, k_ref, v_ref, o_ref, t_ref, c_sc, acc_sc):
        qi = pl.program_id(1)
        c_sc[...] = jnp.zeros_like(c_sc)
        acc_sc[...] = jnp.zeros_like(acc_sc)
        gt = _tri(min(CUM_CHUNK, tk), "gt")

        def tile(hh, j, diag):
            cs = slice(hh * HEAD, (hh + 1) * HEAD)
            rows = pl.ds(pl.multiple_of(j * tk, tk), tk)
            z = _dot_nt(q_ref[:, cs], k_ref[rows, cs]) * scale
            sp = jnp.log(1.0 + jnp.exp(-jnp.abs(z)))
            la = jnp.minimum(z, 0.0) - sp
            lb = -jnp.maximum(z, 0.0) - sp
            if diag:
                strict = _diag_mask(tq, True)
                lb = jnp.where(strict, lb, 0.0)
            suffix, total = _cum_cols(lb, gt, suffix=True)
            w = jnp.exp(la + suffix + c_sc[hh])
            if diag:
                w = jnp.where(strict, w, 0.0)
            acc_sc[hh] += _dot(w.astype(BF16), v_ref[rows, cs])
            c_sc[hh] += total

        for hh in range(hp):
            tile(hh, qi, True)

        def step(it, carry):
            for hh in range(hp):
                tile(hh, qi - 1 - it, False)
            return carry

        lax.fori_loop(0, qi, step, 0)
        for hh in range(hp):
            o_ref[:, hh * HEAD:(hh + 1) * HEAD] = acc_sc[hh]
            t_ref[hh] = jnp.broadcast_to(c_sc[hh], (tq, LANES))

    w = hp * HEAD
    head_blk = lambda off: pl.BlockSpec((S, w), lambda h, i: (0, h + off // hp))
    outs, comm_outs = _hosted_call(
        body,
        out_shape=(jax.ShapeDtypeStruct((S, n_heads * HEAD), F32),
                   jax.ShapeDtypeStruct((n_heads, S, LANES), F32)),
        grid=(n_heads // hp, nq),
        in_specs=[pl.BlockSpec((tq, w), lambda h, i: (i, h + q_off // hp)), head_blk(k_off), head_blk(v_off)],
        out_specs=(pl.BlockSpec((tq, w), lambda h, i: (i, h)),
                   pl.BlockSpec((hp, tq, LANES), lambda h, i: (h, i, 0))),
        scratch_shapes=[pltpu.VMEM((hp, tq, 1), F32), pltpu.VMEM((hp, tq, HEAD), F32)], name=name,
        sem=("parallel", "arbitrary"), operands=(qkv, qkv, qkv), comm=comm)
    return outs if comm is None else (outs, comm_outs)


def _sb_bwd(qkv, do, tstat, *, name, n_heads=N_SB, q_off=0, k_off=N_SB, v_off=2 * N_SB, do_off=0, tq=512,
            hp=HEADS_PER_STEP, comm=None):
    S = qkv.shape[0]
    tq = _pick(S, tq)
    tk = tq
    scale = HEAD ** -0.5
    nq = S // tq
    assert n_heads % hp == 0 and q_off % hp == 0 and k_off % hp == 0 and v_off % hp == 0 and do_off % hp == 0

    def body(q_ref, k_ref, v_ref, do_ref, t_ref, dq_ref, dk_ref, dv_ref, p_sc, r_sc, dq_sc):
        qi = pl.program_id(1)

        @pl.when(qi == 0)
        def _():
            dk_ref[...] = jnp.zeros_like(dk_ref)
            dv_ref[...] = jnp.zeros_like(dv_ref)

        p_sc[...] = jnp.zeros_like(p_sc)
        r_sc[...] = jnp.zeros_like(r_sc)
        dq_sc[...] = jnp.zeros_like(dq_sc)
        le = _tri(min(CUM_CHUNK, tk), "le")
        lt = _tri(min(CUM_CHUNK, tk), "lt")

        def tile(hh, j, diag):
            cs = slice(hh * HEAD, (hh + 1) * HEAD)
            rows = pl.ds(pl.multiple_of(j * tk, tk), tk)
            q = q_ref[:, cs]
            do_b = do_ref[:, cs].astype(BF16)
            kb = k_ref[rows, cs]
            z = _dot_nt(q, kb) * scale
            sp = jnp.log(1.0 + jnp.exp(-jnp.abs(z)))
            la = jnp.minimum(z, 0.0) - sp
            lb = -jnp.maximum(z, 0.0) - sp
            if diag:
                strict = _diag_mask(tq, True)
                lb = jnp.where(strict, lb, 0.0)
            prefix, total_b = _cum_cols(lb, le, suffix=False)
            w = jnp.exp(la + t_ref[hh, :, 0:1] - (prefix + p_sc[hh]))
            if diag:
                w = jnp.where(strict, w, 0.0)
            r = w * _dot_nt(do_b, v_ref[rows, cs])
            rex, total_r = _cum_cols(r, lt, suffix=False)
            rex = rex + r_sc[hh]
            beta = jnp.exp(la)
            dz = r - beta * (r + rex)
            if diag:
                dz = jnp.where(strict, dz, 0.0)
            dzb = dz.astype(BF16)
            dq_sc[hh] += _dot(dzb, kb)
            dk_ref[rows, cs] += _dot_tn(dzb, q)
            dv_ref[rows, cs] += _dot_tn(w.astype(BF16), do_b)
            p_sc[hh] += total_b
            r_sc[hh] += total_r

        def step(j, carry):
            for hh in range(hp):
                tile(hh, j, False)
            return carry

        lax.fori_loop(0, qi, step, 0)
        for hh in range(hp):
            tile(hh, qi, True)
            dq_ref[:, hh * HEAD:(hh + 1) * HEAD] = dq_sc[hh] * scale

        @pl.when(qi == nq - 1)
        def _():
            dk_ref[...] = dk_ref[...] * scale

    w = hp * HEAD
    head_blk = lambda off: pl.BlockSpec((S, w), lambda h, i: (0, h + off // hp))
    out_head = pl.BlockSpec((S, w), lambda h, i: (0, h))
    out_sd = jax.ShapeDtypeStruct((S, n_heads * HEAD), F32)
    outs, comm_outs = _hosted_call(
        body, out_shape=(out_sd, out_sd, out_sd), grid=(n_heads // hp, nq),
        in_specs=[pl.BlockSpec((tq, w), lambda h, i: (i, h + q_off // hp)), head_blk(k_off), head_blk(v_off),
                  pl.BlockSpec((tq, w), lambda h, i: (i, h + do_off // hp)),
                  pl.BlockSpec((hp, tq, LANES), lambda h, i: (h, i, 0))],
        out_specs=(pl.BlockSpec((tq, w), lambda h, i: (i, h)), out_head, out_head),
        scratch_shapes=[pltpu.VMEM((hp, tq, 1), F32), pltpu.VMEM((hp, tq, 1), F32), pltpu.VMEM((hp, tq, HEAD), F32)],
        name=name, sem=("parallel", "arbitrary"), operands=(qkv, qkv, qkv, do, tstat), comm=comm)
    return outs if comm is None else (outs, comm_outs)


def _attn_fwd(q, k, v, bias, *, name, n_heads, dqk, scale, v_off=0, tq=1024, exact_p=False, hp=HEADS_PER_STEP,
              comm=None):
    S = q.shape[0]
    tq = _pick(S, tq)
    tk = tq
    nq = S // tq
    has_bias = bias is not None

    assert n_heads % hp == 0 and v_off % hp == 0

    def body(*refs):
        q_ref, k_ref, v_ref = refs[:3]
        b_ref = refs[3] if has_bias else None
        o_ref, lse_ref, m_sc, l_sc, acc_sc = refs[-5:]
        qi = pl.program_id(1)
        m_sc[...] = jnp.full_like(m_sc, NEG)
        l_sc[...] = jnp.zeros_like(l_sc)
        acc_sc[...] = jnp.zeros_like(acc_sc)

        def tile(hh, j, diag):
            rows = pl.ds(pl.multiple_of(j * tk, tk), tk)
            s = _dot_nt(q_ref[:, hh * dqk:(hh + 1) * dqk], k_ref[rows, hh * dqk:(hh + 1) * dqk]) * scale
            if has_bias:
                s = s + b_ref[hh, :, rows]
            if diag:
                s = jnp.where(_diag_mask(tq, False), s, NEG)
            m_old = m_sc[hh]
            m_new = jnp.maximum(m_old, jnp.max(s, axis=1, keepdims=True))
            alpha = jnp.exp(m_old - m_new)
            p = jnp.exp(s - m_new)
            l_sc[hh] = alpha * l_sc[hh] + jnp.sum(p, axis=1, keepdims=True)
            vb = v_ref[rows, hh * HEAD:(hh + 1) * HEAD]
            if exact_p:
                hi, lo = _split2(p)
                pv = _dot(hi, vb) + _dot(lo, vb)
            else:
                pv = _dot(p.astype(BF16), vb)
            acc_sc[hh] = alpha * acc_sc[hh] + pv
            m_sc[hh] = m_new

        def step(j, carry):
            for hh in range(hp):
                tile(hh, j, False)
            return carry

        lax.fori_loop(0, qi, step, 0)
        for hh in range(hp):
            tile(hh, qi, True)
            l = l_sc[hh]
            o_ref[:, hh * HEAD:(hh + 1) * HEAD] = acc_sc[hh] / l
            lse_ref[hh] = jnp.broadcast_to(m_sc[hh] + jnp.log(l), (tq, LANES))

    in_specs = [pl.BlockSpec((tq, hp * dqk), lambda h, i: (i, h)),
                pl.BlockSpec((S, hp * dqk), lambda h, i: (0, h)),
                pl.BlockSpec((S, hp * HEAD), lambda h, i: (0, h + v_off // hp))]
    operands = [q, k, v]
    if has_bias:
        in_specs.append(pl.BlockSpec((hp, 1, S), lambda h, i: (h, 0, 0)))
        operands.append(bias)
    outs, comm_outs = _hosted_call(
        body,
        out_shape=(jax.ShapeDtypeStruct((S, n_heads * HEAD), F32),
                   jax.ShapeDtypeStruct((n_heads, S, LANES), F32)),
        grid=(n_heads // hp, nq), in_specs=in_specs,
        out_specs=(pl.BlockSpec((tq, hp * HEAD), lambda h, i: (i, h)),
                   pl.BlockSpec((hp, tq, LANES), lambda h, i: (h, i, 0))),
        scratch_shapes=[pltpu.VMEM((hp, tq, 1), F32), pltpu.VMEM((hp, tq, 1), F32), pltpu.VMEM((hp, tq, HEAD), F32)],
        name=name, sem=("parallel", "arbitrary"), operands=operands, comm=comm)
    return outs if comm is None else (outs, comm_outs)


def _attn_bwd(q, k, v, bias, o, lse, do, *, name, n_heads, dqk, scale, v_off=0, do_off=0, tq=512,
              hp=HEADS_PER_STEP, comm=None):
    S = q.shape[0]
    tq = _pick(S, tq)
    tk = tq
    nq = S // tq
    has_bias = bias is not None
    assert n_heads % hp == 0 and v_off % hp == 0 and do_off % hp == 0

    def body(*refs):
        q_ref, k_ref, v_ref, o_ref, lse_ref, do_ref = refs[:6]
        b_ref = refs[6] if has_bias else None
        n_out = 5 if has_bias else 3
        outs = refs[-(n_out + 3):-3]
        dq_ref, dk_ref, dv_ref = outs[:3]
        db_ref, dr_ref = (outs[3], outs[4]) if has_bias else (None, None)
        dq_sc, rs_sc, delta_sc = refs[-3:]
        qi = pl.program_id(1)

        @pl.when(qi == 0)
        def _():
            dk_ref[...] = jnp.zeros_like(dk_ref)
            dv_ref[...] = jnp.zeros_like(dv_ref)
            if has_bias:
                db_ref[...] = jnp.zeros_like(db_ref)

        dq_sc[...] = jnp.zeros_like(dq_sc)
        rs_sc[...] = jnp.zeros_like(rs_sc)
        for hh in range(hp):
            vs = slice(hh * HEAD, (hh + 1) * HEAD)
            do_r = do_ref[:, vs].astype(BF16).astype(F32)
            delta_sc[hh] = jnp.sum(do_r * o_ref[:, vs], axis=1, keepdims=True)

        def tile(hh, j, diag):
            qs = slice(hh * dqk, (hh + 1) * dqk)
            vs = slice(hh * HEAD, (hh + 1) * HEAD)
            rows = pl.ds(pl.multiple_of(j * tk, tk), tk)
            qb = q_ref[:, qs]
            do_b = do_ref[:, vs].astype(BF16)
            delta = delta_sc[hh]
            kb = k_ref[rows, qs]
            s = _dot_nt(qb, kb) * scale
            if has_bias:
                s = s + b_ref[hh, :, rows]
            p = jnp.exp(s - lse_ref[hh, :, 0:1])
            if diag:
                p = jnp.where(_diag_mask(tq, False), p, 0.0)
            ds = p * (_dot_nt(do_b, v_ref[rows, vs]) - delta)
            dsb = (ds * scale).astype(BF16)
            dq_sc[hh] += _dot(dsb, kb)
            dk_ref[rows, qs] += _dot_tn(dsb, qb)
            dv_ref[rows, vs] += _dot_tn(p.astype(BF16), do_b)
            if has_bias:
                db_ref[hh, :, rows] += jnp.sum(ds, axis=0, keepdims=True)
                rs_sc[hh] += jnp.sum(ds, axis=1, keepdims=True)

        def step(j, carry):
            for hh in range(hp):
                tile(hh, j, False)
            return carry

        lax.fori_loop(0, qi, step, 0)
        for hh in range(hp):
            tile(hh, qi, True)
            dq_ref[:, hh * dqk:(hh + 1) * dqk] = dq_sc[hh]
            if has_bias:
                dr_ref[hh] = jnp.broadcast_to(rs_sc[hh], (tq, LANES))

    stat = pl.BlockSpec((hp, tq, LANES), lambda h, i: (h, i, 0))
    in_specs = [pl.BlockSpec((tq, hp * dqk), lambda h, i: (i, h)),
                pl.BlockSpec((S, hp * dqk), lambda h, i: (0, h)),
                pl.BlockSpec((S, hp * HEAD), lambda h, i: (0, h + v_off // hp)),
                pl.BlockSpec((tq, hp * HEAD), lambda h, i: (i, h)),
                stat,
                pl.BlockSpec((tq, hp * HEAD), lambda h, i: (i, h + do_off // hp))]
    operands = [q, k, v, o, lse, do]
    out_shape = [jax.ShapeDtypeStruct((S, n_heads * dqk), F32), jax.ShapeDtypeStruct((S, n_heads * dqk), F32),
                 jax.ShapeDtypeStruct((S, n_heads * HEAD), F32)]
    out_specs = [pl.BlockSpec((tq, hp * dqk), lambda h, i: (i, h)), pl.BlockSpec((S, hp * dqk), lambda h, i: (0, h)),
                 pl.BlockSpec((S, hp * HEAD), lambda h, i: (0, h))]
    if has_bias:
        in_specs.append(pl.BlockSpec((hp, 1, S), lambda h, i: (h, 0, 0)))
        operands.append(bias)
        out_shape.append(jax.ShapeDtypeStruct((n_heads, 1, S), F32))
        out_specs.append(pl.BlockSpec((hp, 1, S), lambda h, i: (h, 0, 0)))
        out_shape.append(jax.ShapeDtypeStruct((n_heads, S, LANES), F32))
        out_specs.append(stat)
    outs, comm_outs = _hosted_call(
        body, out_shape=tuple(out_shape), grid=(n_heads // hp, nq), in_specs=in_specs, out_specs=tuple(out_specs),
        scratch_shapes=[pltpu.VMEM((hp, tq, dqk), F32), pltpu.VMEM((hp, tq, 1), F32),
                        pltpu.VMEM((hp, tq, 1), F32)], name=name,
        sem=("parallel", "arbitrary"), operands=operands, comm=comm)
    return outs if comm is None else (outs, comm_outs)


def _rot_half(y):
    lane = lax.broadcasted_iota(jnp.int32, y.shape, 1)
    up = pltpu.roll(y, 96, 1)
    down = pltpu.roll(y, 32, 1)
    return jnp.where(lane < 32, -up, jnp.where(lane < 64, down, 0.0))


def _mla_prep_fwd(q_raw, kv_raw, down, cos, sin, q_g, k_g, *, name, ts=128):
    S = q_raw.shape[0]
    ts = _pick(S, ts)
    pe_blk = Q_RANK // LANES + KV_RANK // LANES

    def norm_rope(x0, x1, g0, g1, c, s):
        ms = (jnp.sum(x0 * x0, axis=-1, keepdims=True) + jnp.sum(x1 * x1, axis=-1, keepdims=True)) * (1.0 / QK_DIM)
        rstd = lax.rsqrt(ms + EPS)
        y0 = (x0 * rstd) * g0
        y1 = (x1 * rstd) * g1
        return y0, y1 * c + _rot_half(y1) * s

    def body(q_ref, kv_ref, pe_ref, cos_ref, sin_ref, qg_ref, kg_ref, qo_ref, ko_ref, vo_ref):
        c, s = cos_ref[...], sin_ref[...]
        pe = pe_ref[...]
        qg0, qg1 = qg_ref[:, :NOPE], qg_ref[:, NOPE:]
        kg0, kg1 = kg_ref[:, :NOPE], kg_ref[:, NOPE:]
        for h in range(N_MLA):
            b = h * QK_PAD
            y0, y1 = norm_rope(q_ref[:, b:b + NOPE], q_ref[:, b + NOPE:b + QK_PAD], qg0, qg1, c, s)
            qo_ref[:, b:b + NOPE] = y0.astype(BF16)
            qo_ref[:, b + NOPE:b + QK_PAD] = y1.astype(BF16)
            y0, y1 = norm_rope(kv_ref[:, b:b + NOPE], pe, kg0, kg1, c, s)
            ko_ref[:, b:b + NOPE] = y0.astype(BF16)
            ko_ref[:, b + NOPE:b + QK_PAD] = y1.astype(BF16)
            vo_ref[:, h * HEAD:(h + 1) * HEAD] = kv_ref[:, b + NOPE:b + QK_PAD].astype(BF16)

    wide = pl.BlockSpec((ts, N_MLA * QK_PAD), lambda i: (i, 0))
    lane_blk = pl.BlockSpec((ts, LANES), lambda i: (i, 0))
    gain = pl.BlockSpec((1, QK_PAD), lambda i: (0, 0))
    return pl.pallas_call(
        body,
        out_shape=(jax.ShapeDtypeStruct((S, N_MLA * QK_PAD), BF16), jax.ShapeDtypeStruct((S, N_MLA * QK_PAD), BF16),
                   jax.ShapeDtypeStruct((S, N_MLA * HEAD), BF16)),
        grid=(S // ts,),
        in_specs=[wide, wide, pl.BlockSpec((ts, LANES), lambda i: (i, pe_blk)), lane_blk, lane_blk, gain, gain],
        out_specs=(wide, wide, pl.BlockSpec((ts, N_MLA * HEAD), lambda i: (i, 0))), name=name,
        compiler_params=_cp("parallel"))(q_raw, kv_raw, down, cos, sin, q_g, k_g)


def _mla_prep_bwd(dq, dk, dv, q_raw, kv_raw, down, cos, sin, q_g, k_g, *, name, ts=128, comm=None):
    S = q_raw.shape[0]
    ts = _pick(S, ts)
    nsteps = S // ts
    pe_blk = Q_RANK // LANES + KV_RANK // LANES

    def back(x0, x1, g0, g1, c, s, d0, d1r):
        d1 = d1r * c - _rot_half(d1r * s)
        ms = (jnp.sum(x0 * x0, axis=-1, keepdims=True) + jnp.sum(x1 * x1, axis=-1, keepdims=True)) * (1.0 / QK_DIM)
        rstd = lax.rsqrt(ms + EPS)
        h0, h1 = x0 * rstd, x1 * rstd
        e0, e1 = d0 * g0, d1 * g1
        m = (jnp.sum(e0 * h0, axis=-1, keepdims=True) + jnp.sum(e1 * h1, axis=-1, keepdims=True)) * (1.0 / QK_DIM)
        return rstd * (e0 - h0 * m), rstd * (e1 - h1 * m), d0 * h0, d1 * h1

    def fold(a):
        return jnp.sum(a.reshape(ts // 8, 8, a.shape[-1]), axis=0)

    def body(dq_ref, dk_ref, dv_ref, q_ref, kv_ref, pe_ref, cos_ref, sin_ref, qg_ref, kg_ref,
             dqr_ref, dkv_ref, dpe_ref, dqg_ref, dkg_ref, gq_sc, gk_sc):
        i = pl.program_id(0)

        @pl.when(i == 0)
        def _():
            gq_sc[...] = jnp.zeros_like(gq_sc)
            gk_sc[...] = jnp.zeros_like(gk_sc)

        c, s = cos_ref[...], sin_ref[...]
        pe = pe_ref[...]
        qg0, qg1 = qg_ref[:, :NOPE], qg_ref[:, NOPE:]
        kg0, kg1 = kg_ref[:, :NOPE], kg_ref[:, NOPE:]
        dpe = jnp.zeros((ts, LANES), F32)
        for h in range(N_MLA):
            b = h * QK_PAD
            dx0, dx1, a0, a1 = back(q_ref[:, b:b + NOPE], q_ref[:, b + NOPE:b + QK_PAD], qg0, qg1, c, s,
                                    dq_ref[:, b:b + NOPE], dq_ref[:, b + NOPE:b + QK_PAD])
            dqr_ref[:, b:b + NOPE] = dx0.astype(BF16)
            dqr_ref[:, b + NOPE:b + QK_PAD] = dx1.astype(BF16)
            gq_sc[:, :NOPE] += fold(a0)
            gq_sc[:, NOPE:] += fold(a1)
            dx0, dx1, a0, a1 = back(kv_ref[:, b:b + NOPE], pe, kg0, kg1, c, s,
                                    dk_ref[:, b:b + NOPE], dk_ref[:, b + NOPE:b + QK_PAD])
            dkv_ref[:, b:b + NOPE] = dx0.astype(BF16)
            dkv_ref[:, b + NOPE:b + QK_PAD] = dv_ref[:, h * HEAD:(h + 1) * HEAD].astype(BF16)
            dpe = dpe + dx1
            gk_sc[:, :NOPE] += fold(a0)
            gk_sc[:, NOPE:] += fold(a1)
        dpe_ref[...] = dpe

        @pl.when(i == nsteps - 1)
        def _():
            dqg_ref[...] = jnp.sum(gq_sc[...], axis=0, keepdims=True)
            dkg_ref[...] = jnp.sum(gk_sc[...], axis=0, keepdims=True)

    wide = pl.BlockSpec((ts, N_MLA * QK_PAD), lambda i: (i, 0))
    lane_blk = pl.BlockSpec((ts, LANES), lambda i: (i, 0))
    gain = pl.BlockSpec((1, QK_PAD), lambda i: (0, 0))
    outs, comm_outs = _hosted_call(
        body,
        out_shape=(jax.ShapeDtypeStruct((S, N_MLA * QK_PAD), BF16), jax.ShapeDtypeStruct((S, N_MLA * QK_PAD), BF16),
                   jax.ShapeDtypeStruct((S, LANES), F32), jax.ShapeDtypeStruct((1, QK_PAD), F32),
                   jax.ShapeDtypeStruct((1, QK_PAD), F32)),
        grid=(nsteps,),
        in_specs=[wide, wide, pl.BlockSpec((ts, N_MLA * HEAD), lambda i: (i, 0)), wide, wide,
                  pl.BlockSpec((ts, LANES), lambda i: (i, pe_blk)), lane_blk, lane_blk, gain, gain],
        out_specs=(wide, wide, lane_blk, gain, gain),
        scratch_shapes=[pltpu.VMEM((8, QK_PAD), F32), pltpu.VMEM((8, QK_PAD), F32)], name=name,
        sem=("arbitrary",), operands=(dq, dk, dv, q_raw, kv_raw, down, cos, sin, q_g, k_g), comm=comm)
    res = (outs[0], outs[1], outs[2], outs[3][0], outs[4][0])
    return res if comm is None else (res, comm_outs)


def _loss_head(y, target, *, name, tr=256):
    R, C = y.shape
    tr = _pick(R, tr)
    nsteps = R // tr

    def body(y_ref, t_ref, dy_ref, dyb_ref, loss_ref, acc):
        i = pl.program_id(0)

        @pl.when(i == 0)
        def _():
            acc[...] = jnp.zeros_like(acc)

        err = y_ref[...] - t_ref[...]
        dy = err * (1.0 / C)
        dy_ref[...] = dy
        dyb_ref[...] = dy.astype(BF16)
        acc[...] += jnp.sum((err * err).reshape(tr // 8, 8, C), axis=0)

        @pl.when(i == nsteps - 1)
        def _():
            tot = jnp.sum(jnp.sum(acc[...], axis=0, keepdims=True), axis=1, keepdims=True)
            loss_ref[...] = jnp.broadcast_to(tot * (0.5 / C), (8, LANES))

    blk = pl.BlockSpec((tr, C), lambda i: (i, 0))
    dy, dy_b, loss = pl.pallas_call(
        body, out_shape=(jax.ShapeDtypeStruct((R, C), F32), jax.ShapeDtypeStruct((R, C), BF16),
                         jax.ShapeDtypeStruct((8, LANES), F32)),
        grid=(nsteps,), in_specs=[blk, blk], out_specs=(blk, blk, pl.BlockSpec((8, LANES), lambda i: (0, 0))),
        scratch_shapes=[pltpu.VMEM((8, C), F32)], name=name, compiler_params=_cp("arbitrary"))(y, target)
    return dy, dy_b, loss[0, 0]


def _adamw(w, g, m, v, *, name, block_bytes=1 << 20, comm=None):
    L, R, C = w.shape
    tr = max(8, min(R, (block_bytes // (4 * C)) // 8 * 8))
    while R % tr:
        tr -= 8
    if tr <= 0:
        tr = R
    c1 = 1.0 / (1.0 - ADAM_B1 ** ADAM_STEP)
    c2 = 1.0 / (1.0 - ADAM_B2 ** ADAM_STEP)

    def body(w_ref, g_ref, m_ref, v_ref, d_ref, mo_ref, vo_ref):
        gv = g_ref[...]
        mn = ADAM_B1 * m_ref[...] + (1.0 - ADAM_B1) * gv
        vn = ADAM_B2 * v_ref[...] + (1.0 - ADAM_B2) * (gv * gv)
        d_ref[...] = -ADAM_LR * ((mn * c1) / (jnp.sqrt(vn * c2) + ADAM_EPS) + ADAM_WD * w_ref[...])
        mo_ref[...] = mn
        vo_ref[...] = vn

    blk = pl.BlockSpec((None, tr, C), lambda l, i: (l, i, 0))
    sd = jax.ShapeDtypeStruct((L, R, C), F32)
    outs, comm_outs = _hosted_call(
        body, out_shape=(sd, sd, sd), grid=(L, R // tr), in_specs=[blk] * 4, out_specs=(blk,) * 3, scratch_shapes=[],
        name=name, sem=("parallel", "parallel"), operands=(w, g, m, v), comm=comm)
    return outs if comm is None else (outs, comm_outs)


def _row_tile(r, c, itemsize=4, block_bytes=1 << 20):
    tr = max(16, min(r, (block_bytes // (itemsize * c)) // 16 * 16))
    while r % tr:
        tr -= 16
    return tr if tr > 0 else r


def _add_sibling(g, recv, core, *, name):
    nch, _, r, c = g.shape
    tr = _row_tile(r, c)

    def body(core_ref, g_ref, r_ref, o_ref):
        o_ref[...] = (g_ref[...] + r_ref[...]).astype(BF16)

    grid_spec = pltpu.PrefetchScalarGridSpec(
        num_scalar_prefetch=1, grid=(nch, r // tr),
        in_specs=[pl.BlockSpec((None, None, tr, c), lambda j, i, cr: (j, cr[0], i, 0)),
                  pl.BlockSpec((None, tr, c), lambda j, i, cr: (j, i, 0))],
        out_specs=pl.BlockSpec((None, tr, c), lambda j, i, cr: (j, i, 0)))
    return pl.pallas_call(
        body, out_shape=jax.ShapeDtypeStruct((nch, r, c), BF16), grid_spec=grid_spec, name=name,
        compiler_params=_cp("parallel", "parallel"))(core, g, recv)


def _add_chips(slots, *, name):
    nch, r, c = slots.shape
    tr = _row_tile(r, c)

    def body(s_ref, o_ref):
        acc = s_ref[0].astype(F32)
        for j in range(1, nch):
            acc = acc + s_ref[j].astype(F32)
        o_ref[...] = acc

    return pl.pallas_call(
        body, out_shape=jax.ShapeDtypeStruct((r, c), F32), grid=(r // tr,),
        in_specs=[pl.BlockSpec((nch, tr, c), lambda i: (0, i, 0))],
        out_specs=pl.BlockSpec((tr, c), lambda i: (i, 0)), name=name, compiler_params=_cp("parallel"))(slots)


def _place():
    x, y, c = lax.axis_index("x"), lax.axis_index("y"), lax.axis_index("c")
    others = [(1 - x, y), (x, 1 - y), (1 - x, 1 - y)]
    return x, y, c, 2 * x + y, others


ANY = pl.BlockSpec(memory_space=pl.ANY)


class _Exchange:
    def __init__(self, kind, arrays):
        self.kind, self.ins = kind, list(arrays)
        self.n_peers = 1 if kind in ("swap", "join") else 3
        self.aliased = kind == "forward"
        n = len(self.ins) * self.n_peers
        shp = {"gather": lambda a: (N_CHIPS,) + a.shape, "scatter": lambda a: a.shape, "forward": lambda a: a.shape,
               "swap": lambda a: (a.shape[0],) + a.shape[2:], "join": lambda a: (2,) + a.shape}[kind]
        self.out_shapes = [jax.ShapeDtypeStruct(shp(a), a.dtype) for a in self.ins]
        self.sems = [pltpu.SemaphoreType.DMA((n,)), pltpu.SemaphoreType.DMA((n,))]

    def _copies(self, ins, outs, sems):
        send, recv = sems
        x, y, c, me, others = _place()
        over_ici = self.kind in ("gather", "scatter")
        peers = [(ox, oy, c) for ox, oy in others] if over_ici else [(x, y, 1 - c)] * self.n_peers
        for a in range(len(self.ins)):
            for k, to in enumerate(peers):
                peer = 2 * others[k][0] + others[k][1]
                if self.kind == "gather":
                    hr = self.ins[a].shape[0] // 2
                    rows = pl.ds(c * hr, hr)
                    src, dst, land = ins[a].at[rows, :], outs[a].at[me, rows, :], outs[a].at[peer, rows, :]
                elif self.kind == "forward":
                    hr = self.ins[a].shape[1] // 2
                    mine, theirs = pl.ds(c * hr, hr), pl.ds((1 - c) * hr, hr)
                    src, dst, land = ins[a].at[peer, mine, :], outs[a].at[peer, mine, :], outs[a].at[peer, theirs, :]
                elif self.kind == "scatter":
                    src, dst, land = ins[a].at[peer], outs[a].at[me], outs[a].at[peer]
                elif self.kind == "swap":
                    src, dst, land = ins[a].at[:, 1 - c], outs[a], outs[a]
                else:
                    src, dst, land = ins[a], outs[a].at[c], outs[a].at[1 - c]
                i = self.n_peers * a + k
                mk = lambda s, d: pltpu.make_async_remote_copy(
                    src_ref=s, dst_ref=d, send_sem=send.at[i], recv_sem=recv.at[i], device_id=to,
                    device_id_type=MESH)
                yield mk(src, dst), mk(land, land)

    def alias_pairs(self):
        return [(i, i) for i in range(len(self.ins))] if self.aliased else []

    def start(self, ins, outs, sems):
        for cp, _ in self._copies(ins, outs, sems):
            cp.start()

    def finish(self, ins, outs, sems):
        pairs = list(self._copies(ins, outs, sems))
        for _, landing in pairs:
            landing.wait_recv()
        for cp, _ in pairs:
            cp.wait_send()


class _Several:
    def __init__(self, parts):
        self.parts = list(parts)
        self.ins = [a for p in self.parts for a in p.ins]
        self.out_shapes = [s for p in self.parts for s in p.out_shapes]
        self.sems = [s for p in self.parts for s in p.sems]

    def split(self, ins, outs, sems=None):
        i = 0
        for k, p in enumerate(self.parts):
            n = len(p.ins)
            yield p, ins[i:i + n], outs[i:i + n], None if sems is None else sems[2 * k:2 * k + 2]
            i += n

    def alias_pairs(self):
        pairs, i = [], 0
        for p in self.parts:
            pairs += [(i + a, i + b) for a, b in p.alias_pairs()]
            i += len(p.ins)
        return pairs

    def start(self, ins, outs, sems):
        for p, a, b, s in self.split(ins, outs, sems):
            p.start(a, b, s)

    def finish(self, ins, outs, sems):
        for p, a, b, s in self.split(ins, outs, sems):
            p.finish(a, b, s)


def _hosted_call(body, *, grid, in_specs, out_specs, out_shape, scratch_shapes, operands, name, sem, comm=None):
    out_specs, out_shape = tuple(out_specs), tuple(out_shape)
    if isinstance(comm, (list, tuple)):
        several = _Several(comm)
        outs, comm_outs = _hosted_call(body, grid=grid, in_specs=in_specs, out_specs=out_specs, out_shape=out_shape,
                                       scratch_shapes=scratch_shapes, operands=operands, name=name, sem=sem,
                                       comm=several)
        return outs, [tuple(o) for _, _, o, _ in several.split(several.ins, comm_outs)]
    if comm is None:
        res = pl.pallas_call(body, out_shape=out_shape, grid=grid, in_specs=list(in_specs), out_specs=out_specs,
                             scratch_shapes=list(scratch_shapes), name=name, compiler_params=_cp(*sem))(*operands)
        return tuple(res), ()
    n_in, n_out, n_sc = len(in_specs), len(out_specs), len(scratch_shapes)
    ci, co = len(comm.ins), len(comm.out_shapes)

    def wrapped(*refs):
        ins, c_ins = refs[:n_in], refs[n_in:n_in + ci]
        outs = refs[n_in + ci:n_in + ci + n_out]
        c_outs = refs[n_in + ci + n_out:n_in + ci + n_out + co]
        scratch = refs[n_in + ci + n_out + co:n_in + ci + n_out + co + n_sc]
        sems = refs[n_in + ci + n_out + co + n_sc:]
        ids = [pl.program_id(d) for d in range(len(grid))]
        first = functools.reduce(jnp.logical_and, [i == 0 for i in ids])
        last = functools.reduce(jnp.logical_and, [i == g - 1 for i, g in zip(ids, grid)])

        @pl.when(first)
        def _():
            comm.start(c_ins, c_outs, sems)

        body(*ins, *outs, *scratch)

        @pl.when(last)
        def _():
            comm.finish(c_ins, c_outs, sems)

    res = pl.pallas_call(
        wrapped, out_shape=out_shape + tuple(comm.out_shapes), grid=grid, in_specs=list(in_specs) + [ANY] * ci,
        out_specs=out_specs + tuple([ANY] * co), scratch_shapes=list(scratch_shapes) + comm.sems, name=name,
        input_output_aliases={n_in + i: n_out + o for i, o in comm.alias_pairs()},
        compiler_params=pltpu.CompilerParams(dimension_semantics=("arbitrary",) * len(grid),
                                             vmem_limit_bytes=VMEM_LIMIT, has_side_effects=True),
    )(*operands, *comm.ins)
    return tuple(res[:n_out]), tuple(res[n_out:])


def _run_exchange(comm, *, name):
    ci = len(comm.ins)

    def body(*refs):
        ins, outs, sems = refs[:ci], refs[ci:2 * ci], refs[2 * ci:]
        comm.start(ins, outs, sems)
        comm.finish(ins, outs, sems)

    return pl.pallas_call(
        body, out_shape=tuple(comm.out_shapes), in_specs=[ANY] * ci, out_specs=tuple([ANY] * ci),
        scratch_shapes=comm.sems, name=name, input_output_aliases=dict(comm.alias_pairs()),
        compiler_params=pltpu.CompilerParams(has_side_effects=True))(*comm.ins)


def _own_slot(buf, piece, idx):
    return lax.dynamic_update_slice(buf, piece[None], (idx,) + (0,) * piece.ndim)


def _all_reduce_small(v, *, name):
    R = v.shape[0]

    flips = [(dx, dy, dc) for dx in range(2) for dy in range(2) for dc in range(2) if dx or dy or dc]

    def body(v_ref, o_ref, slots, send, recv):
        x, y, c, me, others = _place()
        mine = 2 * me + c
        slots[mine] = v_ref[...]

        def copy(k, slot):
            dx, dy, dc = flips[k]
            peer = (x + dx - 2 * x * dx, y + dy - 2 * y * dy, c + dc - 2 * c * dc)
            peer_slot = 4 * peer[0] + 2 * peer[1] + peer[2]
            return pltpu.make_async_remote_copy(
                src_ref=v_ref, dst_ref=slots.at[mine if slot == "mine" else peer_slot], send_sem=send.at[k],
                recv_sem=recv.at[k], device_id=peer, device_id_type=MESH)

        for k in range(7):
            copy(k, "mine").start()
        for k in range(7):
            copy(k, "peer").wait_recv()
        for k in range(7):
            copy(k, "mine").wait_send()
        acc = slots[0]
        for j in range(1, 8):
            acc = acc + slots[j]
        o_ref[...] = acc

    vm = pl.BlockSpec(memory_space=pltpu.VMEM)
    return pl.pallas_call(
        body, out_shape=jax.ShapeDtypeStruct(v.shape, F32), in_specs=[vm], out_specs=vm,
        scratch_shapes=[pltpu.VMEM((8, R, LANES), F32), pltpu.SemaphoreType.DMA((7,)),
                        pltpu.SemaphoreType.DMA((7,))],
        name=name, compiler_params=pltpu.CompilerParams(has_side_effects=True))(v)


def _rows(v, n_rows):
    v = v.reshape(-1).astype(F32)
    return jnp.pad(v, (0, n_rows * LANES - v.shape[0])).reshape(n_rows, LANES)


def _mlp_fwd(x_in, g, w_up, w_down, tag):
    h = _rms_fwd(x_in, g, name=f"{tag}_norm")
    u, a = _matmul(h, w_up, b_split=True, epilogue="sqrelu", name=f"{tag}_up")
    x_out = _matmul(a, w_down, epilogue="res", res=x_in, name=f"{tag}_down")
    return x_out, (h, u, a)


def _mlp_bwd(dy, dy_b, x_in, g, w_up, w_down, saved, tag, comms=None):
    h, u, a = saved
    comms = comms or {}
    landed = {}

    def mm(key, *args, **kw):
        comm = comms.get(key)
        if callable(comm):
            comm = comm(landed)
        out = _matmul(*args, name=f"{tag}_{key}", comm=comm, **kw)
        if comm is not None:
            out, landed[key] = out
        return out

    dw_down = mm("dwdown", a, dy_b, form="tn")
    du = mm("du", dy_b, w_down, form="nt", epilogue="sqrelu_bwd", u=u, out_dtype=BF16)
    dw_up = mm("dwup", h, du, form="tn", out_split=True)
    dh = mm("dh", du, w_up, form="nt", b_split=True)
    dx, dx_b, dg = _rms_bwd(x_in, g, dh, res=dy, bf16_copy=True, name=f"{tag}_dnorm")
    return dx, dx_b, dg, dw_up, dw_down, landed


def kernel(x, positions, ln_mix_g, ln_mlp_g, sf_w_in, sf_b_f, fox_q_g, fox_k_g, sf_w_o, mla_w_down, mla_q_a_g, mla_kv_a_g, mla_w_uq, mla_w_ukv, mla_q_g, mla_k_g, mla_w_o, mlp_w_up, mlp_w_down, loss_target, m_ln_mix_g, m_ln_mlp_g, m_sf_w_in, m_sf_b_f, m_fox_q_g, m_fox_k_g, m_sf_w_o, m_mla_w_down, m_mla_q_a_g, m_mla_kv_a_g, m_mla_w_uq, m_mla_w_ukv, m_mla_q_g, m_mla_k_g, m_mla_w_o, m_mlp_w_up, m_mlp_w_down, v_ln_mix_g, v_ln_mlp_g, v_sf_w_in, v_sf_b_f, v_fox_q_g, v_fox_k_g, v_sf_w_o, v_mla_w_down, v_mla_q_a_g, v_mla_kv_a_g, v_mla_w_uq, v_mla_w_ukv, v_mla_q_g, v_mla_k_g, v_mla_w_o, v_mlp_w_up, v_mlp_w_down):
    S, D = x.shape[1], x.shape[2]
    xs, tgt, pos = x[0], loss_target[0], positions[0]
    xi, yi, ci = lax.axis_index("x"), lax.axis_index("y"), lax.axis_index("c")
    chip = 2 * xi + yi
    core = ci.astype(jnp.int32).reshape(1)
    d_ff = mlp_w_up.shape[2] * N_CHIPS
    in_w = sf_w_in.shape[2] * N_CHIPS
    qkv_w = 3 * N_SB * HEAD + 3 * N_FOX * HEAD
    dn_w = mla_w_down.shape[2]
    dn_pad = Q_RANK + KV_RANK + LANES

    def gather_begin(ws):
        shards = [w.astype(BF16) for w in ws]
        return shards, _Exchange("gather", shards)

    def hand_over(landed):
        return _Exchange("forward", list(landed))

    def gather_end(both, shards):
        return [_own_slot(ag, s, chip) for ag, s in zip(both, shards)]

    cols = lambda ag: ag.transpose(1, 0, 2).reshape(ag.shape[1], -1)
    rows = lambda ag: ag.reshape(-1, ag.shape[2])
    s_mix0, ex_mix0 = gather_begin([sf_w_in[0], sf_w_o[0]])
    landed = _run_exchange(ex_mix0, name="gather_mix0")
    ag_in, ag_o0 = gather_end(_run_exchange(hand_over(landed), name="gather_mix0_sibling"), s_mix0)
    w_in_full = cols(ag_in)
    w_qkv = w_in_full[:, :qkv_w]
    w_f = jnp.pad(w_in_full[:, qkv_w:], ((0, 0), (0, LANES - (in_w - qkv_w))))
    w_o0 = rows(ag_o0)
    s_up0, ex_up0 = gather_begin([mlp_w_up[0]])
    s_dw0, ex_dw0 = gather_begin([mlp_w_down[0]])
    s_mix1, ex_mix1 = gather_begin([mla_w_down[0], mla_w_uq[0], mla_w_ukv[0], mla_w_o[0]])
    s_mlp1, ex_mlp1 = gather_begin([mlp_w_up[1], mlp_w_down[1]])

    gain_blk = jnp.concatenate([mla_q_a_g, mla_kv_a_g], axis=0) * (ci == 0).astype(F32)
    placed = jnp.zeros((2, N_CHIPS, LANES), F32)
    placed = lax.dynamic_update_slice(placed, gain_blk[:, None, :], (0, chip, 0))
    gains = _all_reduce_small(placed.reshape(2 * N_CHIPS, LANES), name="gather_gains")
    q_a_full = gains[:N_CHIPS].reshape(Q_RANK)
    kv_a_full = gains[N_CHIPS:].reshape(KV_RANK)

    pad_gain = lambda g: jnp.pad(g.reshape(1, QK_DIM), ((0, 0), (0, QK_PAD - QK_DIM)))
    q_g_pad, k_g_pad = pad_gain(mla_q_g), pad_gain(mla_k_g)
    b_pad = _rows(sf_b_f, 1)

    h0 = _rms_fwd(xs, ln_mix_g[0], name="mix0_norm")
    qkv_sb = _matmul(h0, w_qkv, n=3 * N_SB * HEAD, b_n0=0, out_dtype=BF16, name="mix0_qkv_sb")
    qk_fx = _matmul(h0, w_qkv, n=2 * N_FOX * HEAD, b_n0=3 * N_SB * HEAD, name="mix0_qk_fox")
    v_fx = _matmul(h0, w_qkv, n=N_FOX * HEAD, b_n0=(3 * N_SB + 2 * N_FOX) * HEAD, out_dtype=BF16,
                   name="mix0_v_fox")
    fl = _matmul(h0, w_f, name="mix0_forget_logit")
    f_cum = _forget_fwd(fl, b_pad, name="forget_fwd")
    neg_f = (-f_cum[:, :N_FOX]).T.reshape(N_FOX, 1, S)
    q_f = _rms_fwd(qk_fx, fox_q_g[0], c0=0, width=N_FOX * HEAD, gw=HEAD, name="fox_q_norm")
    k_f = _rms_fwd(qk_fx, fox_k_g[0], c0=N_FOX * HEAD, width=N_FOX * HEAD, gw=HEAD, name="fox_k_norm")
    (o_sb, t_sb), l_up0 = _sb_fwd(qkv_sb, name="sb_fwd", comm=ex_up0, tq=1024)
    (o_fx, lse0), (l_mix1, l_dw0, both) = _attn_fwd(
        q_f, k_f, v_fx, neg_f, n_heads=N_FOX, dqk=HEAD, scale=HEAD ** -0.5, exact_p=True, name="fox_fwd",
        comm=[ex_mix1, ex_dw0, hand_over(l_up0)])
    ag_up0, = gather_end(both, s_up0)
    o0 = jnp.concatenate([o_sb, o_fx], axis=1)
    x1, (b_mix1, b_dw0) = _matmul(o0, w_o0, epilogue="res", res=xs, name="mix0_out",
                                  comm=[hand_over(l_mix1), hand_over(l_dw0)])
    ag_dn, ag_uq, ag_ukv, ag_o1 = gather_end(b_mix1, s_mix1)
    ag_dw0, = gather_end(b_dw0, s_dw0)
    w_dn = jnp.pad(rows(ag_dn), ((0, 0), (0, dn_pad - dn_w)))
    w_uq = jnp.pad(cols(ag_uq).reshape(Q_RANK, N_MLA, QK_DIM), ((0, 0), (0, 0), (0, QK_PAD - QK_DIM)))
    w_uq = w_uq.reshape(Q_RANK, N_MLA * QK_PAD)
    w_ukv = cols(ag_ukv)
    w_o1 = rows(ag_o1)
    x2, mlp0 = _mlp_fwd(x1, ln_mlp_g[0], ag_up0, rows(ag_dw0), "mlp0")

    h2 = _rms_fwd(x2, ln_mix_g[1], name="mix1_norm")
    down = _matmul(h2, w_dn, name="mix1_down")
    c_q = _rms_fwd(down, q_a_full, c0=0, width=Q_RANK, name="mix1_q_a_norm")
    c_kv = _rms_fwd(down, kv_a_full, c0=Q_RANK, width=KV_RANK, name="mix1_kv_a_norm")
    q_raw = _matmul(c_q, w_uq, name="mix1_uq")
    kv_raw = _matmul(c_kv, w_ukv, name="mix1_ukv")
    half = ROPE // 2
    inv_freq = ROPE_THETA ** (-jnp.arange(half, dtype=F32) / half)
    ang = pos.astype(F32)[:, None] * inv_freq
    table = lambda t: jnp.pad(jnp.concatenate([t, t], axis=1), ((0, 0), (0, LANES - ROPE)))
    cos_t, sin_t = table(jnp.cos(ang)), table(jnp.sin(ang))
    q_pad, k_pad, v1 = _mla_prep_fwd(q_raw, kv_raw, down, cos_t, sin_t, q_g_pad, k_g_pad, name="mla_prep_fwd")
    (o1, lse1), landed = _attn_fwd(q_pad, k_pad, v1, None, n_heads=N_MLA, dqk=QK_PAD, scale=QK_DIM ** -0.5,
                                   name="mla_fwd", comm=ex_mlp1)
    x3, both = _matmul(o1, w_o1, epilogue="res", res=x2, name="mix1_out", comm=hand_over(landed))
    ag_up1, ag_dw1 = gather_end(both, s_mlp1)
    w_up = [ag_up0, ag_up1]
    w_dw = [rows(ag_dw0), rows(ag_dw1)]
    x4, mlp1 = _mlp_fwd(x3, ln_mlp_g[1], w_up[1], w_dw[1], "mlp1")

    dx4, dx4_b, loss_local = _loss_head(x4, tgt, name="loss_head")
    loss = lax.psum(loss_local, ("x", "y", "c"))

    by_cols = lambda g: g.reshape(g.shape[0], N_CHIPS, -1).transpose(1, 0, 2)
    by_rows = lambda g: g.reshape(N_CHIPS, g.shape[0] // N_CHIPS, g.shape[1])
    halves = lambda g: g.reshape(N_CHIPS, 2, g.shape[1] // 2, g.shape[2])

    def scatter_of(grads, from_sibling, tags):
        parts = [_add_sibling(g, r, core, name=f"add_sibling_{t}") for g, r, t in zip(grads, from_sibling, tags)]
        return parts, _Exchange("scatter", parts)

    def sums_of(slots, parts, tags):
        slots = [_own_slot(s, lax.dynamic_index_in_dim(p, chip, 0, keepdims=False), chip)
                 for s, p in zip(slots, parts)]
        return [_add_chips(s, name=f"add_chips_{t}") for s, t in zip(slots, tags)]

    def shards_of(joined, mine):
        return [_own_slot(j, m, ci).reshape(2 * m.shape[0], m.shape[1]) for j, m in zip(joined, mine)]

    dx3, dx3_b, dg_mlp1, dw_up1, dw_dw1, _ = _mlp_bwd(dx4, dx4_b, x3, ln_mlp_g[1], w_up[1], w_dw[1], mlp1, "mlp1")
    tags_mlp1 = ["w_up1", "w_dw1"]
    g_mlp1 = [halves(dw_up1), halves(by_rows(dw_dw1))]

    dw_o1 = _matmul(o1, dx3_b, form="tn", name="mix1_dwo")
    do1 = _matmul(dx3_b, w_o1, form="nt", name="mix1_do")
    (dq_pad, dk_pad, dv1), from_sibling = _attn_bwd(
        q_pad, k_pad, v1, None, o1, lse1, do1, n_heads=N_MLA, dqk=QK_PAD, scale=QK_DIM ** -0.5, name="mla_bwd",
        comm=_Exchange("swap", g_mlp1))
    p_mlp1, _ = scatter_of(g_mlp1, from_sibling, tags_mlp1)
    (dq_raw, dkv_raw, dpe, dg_q, dg_k), sl_up1 = _mla_prep_bwd(
        dq_pad, dk_pad, dv1, q_raw, kv_raw, down, cos_t, sin_t, q_g_pad, k_g_pad, name="mla_prep_bwd",
        comm=_Exchange("scatter", p_mlp1[:1]))
    dw_uq = _matmul(c_q, dq_raw, form="tn", name="mix1_dwuq")
    dc_q = _matmul(dq_raw, w_uq, form="nt", name="mix1_dcq")
    dw_ukv = _matmul(c_kv, dkv_raw, form="tn", name="mix1_dwukv")
    dc_kv = _matmul(dkv_raw, w_ukv, form="nt", name="mix1_dckv")
    d_cq, dg_qa = _rms_bwd(down, q_a_full, dc_q, c0=0, width=Q_RANK, name="mix1_q_a_dnorm")
    d_ckv, dg_kva = _rms_bwd(down, kv_a_full, dc_kv, c0=Q_RANK, width=KV_RANK, name="mix1_kv_a_dnorm")
    d_down = jnp.concatenate([d_cq, d_ckv, dpe], axis=1)
    dw_dn = _matmul(h2, d_down, form="tn", name="mix1_dwdown")
    dh2 = _matmul(d_down, w_dn, form="nt", name="mix1_dh")
    dx2, dx2_b, dg_mix1 = _rms_bwd(x2, ln_mix_g[1], dh2, res=dx3, bf16_copy=True, name="mix1_dnorm")
    g_uq = dw_uq.reshape(Q_RANK, N_MLA, QK_PAD)[:, :, :QK_DIM].reshape(Q_RANK, N_MLA * QK_DIM)
    tags_mix1 = ["w_dn", "w_uq", "w_ukv", "w_o1"]
    g_mix1 = [halves(by_rows(dw_dn[:, :dn_w])), halves(by_cols(g_uq)), halves(by_cols(dw_ukv)),
              halves(by_rows(dw_o1))]

    p_mix1, mine_mlp1 = [], []

    def scatter_mix1(landed):
        parts, ex = scatter_of(g_mix1, landed["dwdown"][1], tags_mix1)
        p_mix1.extend(parts)
        return ex

    def join_mlp1(landed):
        mine_mlp1.extend(sums_of(sl_up1 + landed["dwdown"][0], p_mlp1, tags_mlp1))
        return _Exchange("join", mine_mlp1)

    dx1, dx1_b, dg_mlp0, dw_up0, dw_dw0, landed = _mlp_bwd(
        dx2, dx2_b, x1, ln_mlp_g[0], w_up[0], w_dw[0], mlp0, "mlp0",
        comms={"dwdown": [_Exchange("scatter", p_mlp1[1:]), _Exchange("swap", g_mix1)],
               "du": scatter_mix1, "dwup": join_mlp1})
    gs_up1, gs_dw1 = shards_of(landed["dwup"], mine_mlp1)
    mine_mix1 = sums_of(landed["du"], p_mix1, tags_mix1)
    tags_mlp0 = ["w_up0", "w_dw0"]
    g_mlp0 = [halves(dw_up0), halves(by_rows(dw_dw0))]

    dw_o0, joined = _matmul(o0, dx1_b, form="tn", name="mix0_dwo", comm=_Exchange("join", mine_mix1))
    gs_dn, gs_uq, gs_ukv, gs_o1 = shards_of(joined, mine_mix1)
    g_o0 = [halves(by_rows(dw_o0))]
    do0 = _matmul(dx1_b, w_o0, form="nt", name="mix0_do")
    (dq_f, dk_f, dv_fx, dbias, drow), (fs_mlp0, fs_o0) = _attn_bwd(
        q_f, k_f, v_fx, neg_f, o_fx, lse0, do0, n_heads=N_FOX, dqk=HEAD, scale=HEAD ** -0.5, do_off=N_SB,
        name="fox_bwd", comm=[_Exchange("swap", g_mlp0), _Exchange("swap", g_o0)])
    p_mlp0, ex_a = scatter_of(g_mlp0, fs_mlp0, tags_mlp0)
    p_o0, ex_b = scatter_of(g_o0, fs_o0, ["w_o0"])
    (dq_sb, dk_sb, dv_sb), (sl_mlp0, sl_o0) = _sb_bwd(qkv_sb, do0, t_sb, do_off=0, name="sb_bwd", comm=[ex_a, ex_b])
    mine_mlp0 = sums_of(sl_mlp0, p_mlp0, tags_mlp0)
    mine_o0 = sums_of(sl_o0, p_o0, ["w_o0"])
    dq_fx, dg_fq = _rms_bwd(qk_fx, fox_q_g[0], dq_f, c0=0, width=N_FOX * HEAD, gw=HEAD, name="fox_q_dnorm")
    dk_fx, dg_fk = _rms_bwd(qk_fx, fox_k_g[0], dk_f, c0=N_FOX * HEAD, width=N_FOX * HEAD, gw=HEAD,
                            name="fox_k_dnorm")
    d_fcum = jnp.pad((jnp.max(drow, axis=-1) - dbias.reshape(N_FOX, S)).T, ((0, 0), (0, LANES - N_FOX)))
    dfl, db_f = _forget_bwd(fl, b_pad, d_fcum, name="forget_bwd")
    dproj = jnp.concatenate([dq_sb, dk_sb, dv_sb, dq_fx, dk_fx, dv_fx], axis=1).astype(BF16)
    dw_qkv, (j_mlp0, j_o0) = _matmul(h0, dproj, form="tn", name="mix0_dwqkv",
                                     comm=[_Exchange("join", mine_mlp0), _Exchange("join", mine_o0)])
    gs_up0, gs_dw0 = shards_of(j_mlp0, mine_mlp0)
    gs_o0, = shards_of(j_o0, mine_o0)
    dw_f = _matmul(h0, dfl, form="tn", name="mix0_dwf")
    g_in = [halves(by_cols(jnp.concatenate([dw_qkv, dw_f[:, :in_w - qkv_w]], axis=1)))]
    dh0 = _matmul(dfl, w_f, form="nt", name="mix0_dh_f")
    dh0, from_sibling = _matmul(dproj, w_qkv, form="nt", epilogue="res", res=dh0, name="mix0_dh",
                                comm=_Exchange("swap", g_in))
    p_in, ex_in_grads = scatter_of(g_in, from_sibling, ["w_in"])
    grad_x, dg_mix0 = _rms_bwd(xs, ln_mix_g[0], dh0, res=dx1, name="mix0_dnorm")
    gs_up = jnp.concatenate([gs_up0, gs_up1], axis=0)
    gs_dw = jnp.concatenate([gs_dw0, gs_dw1], axis=0)

    ln_rows = D // LANES
    small = jnp.concatenate([
        _rows(dg_mix0, ln_rows), _rows(dg_mix1, ln_rows), _rows(dg_mlp0, ln_rows), _rows(dg_mlp1, ln_rows),
        _rows(db_f, 8), _rows(dg_fq, 8), _rows(dg_fk, 8), _rows(dg_qa, 8), _rows(dg_kva, 8), _rows(dg_q, 8),
        _rows(dg_k, 8)], axis=0)
    small = _all_reduce_small(small, name="reduce_small")
    flat = lambda r0, nr, n: small[r0:r0 + nr].reshape(-1)[:n]
    r0 = 4 * ln_rows
    g_ln_mix = jnp.stack([flat(0, ln_rows, D), flat(ln_rows, ln_rows, D)])
    g_ln_mlp = jnp.stack([flat(2 * ln_rows, ln_rows, D), flat(3 * ln_rows, ln_rows, D)])
    g_b_f = flat(r0, 8, N_FOX)[None]
    g_fq, g_fk = flat(r0 + 8, 8, HEAD)[None], flat(r0 + 16, 8, HEAD)[None]
    g_qa = lax.dynamic_slice(flat(r0 + 24, 8, Q_RANK), (chip * LANES,), (LANES,))[None]
    g_kva = lax.dynamic_slice(flat(r0 + 32, 8, KV_RANK), (chip * LANES,), (LANES,))[None]
    g_q, g_k = flat(r0 + 40, 8, QK_DIM)[None], flat(r0 + 48, 8, QK_DIM)[None]

    def pack_small(ln_mix, ln_mlp, *rest):
        return jnp.concatenate([_rows(ln_mix, 2 * ln_rows), _rows(ln_mlp, 2 * ln_rows)] + [_rows(t, 8) for t in rest],
                               axis=0)

    def unpack_small(p):
        f = lambda r, nr, shape: p[r:r + nr].reshape(-1)[:int(np.prod(shape))].reshape(shape)
        shapes = [(1, N_FOX), (1, HEAD), (1, HEAD), (1, LANES), (1, LANES), (1, QK_DIM), (1, QK_DIM)]
        return (f(0, 2 * ln_rows, (2, D)), f(2 * ln_rows, 2 * ln_rows, (2, D)),
                *[f(r0 + 8 * i, 8, shp) for i, shp in enumerate(shapes)])

    small_out = _adamw(
        pack_small(ln_mix_g, ln_mlp_g, sf_b_f, fox_q_g, fox_k_g, mla_q_a_g, mla_kv_a_g, mla_q_g, mla_k_g)[None],
        pack_small(g_ln_mix, g_ln_mlp, g_b_f, g_fq, g_fk, g_qa, g_kva, g_q, g_k)[None],
        pack_small(m_ln_mix_g, m_ln_mlp_g, m_sf_b_f, m_fox_q_g, m_fox_k_g, m_mla_q_a_g, m_mla_kv_a_g, m_mla_q_g,
                   m_mla_k_g)[None],
        pack_small(v_ln_mix_g, v_ln_mlp_g, v_sf_b_f, v_fox_q_g, v_fox_k_g, v_mla_q_a_g, v_mla_kv_a_g, v_mla_q_g,
                   v_mla_k_g)[None], name="adamw_small")
    d_small, m_small, v_small = [unpack_small(p[0]) for p in small_out]

    def big(w, g, m, v, tag, comm=None):
        g = g.reshape(w.shape)
        out = _adamw(w, g, m, v, name=f"adamw_{tag}", comm=comm)
        (d, mn, vn), landed = out if comm is not None else (out, None)
        return (g, d, mn, vn) if comm is None else ((g, d, mn, vn), landed)

    r_up, slots = big(mlp_w_up, gs_up, m_mlp_w_up, v_mlp_w_up, "w_up", comm=ex_in_grads)
    mine_in = sums_of(slots, p_in, ["w_in"])
    gs_in, = shards_of(_run_exchange(_Exchange("join", mine_in), name="reduce_w_in_join"), mine_in)
    r_dw = big(mlp_w_down, gs_dw, m_mlp_w_down, v_mlp_w_down, "w_dw")
    r_in = big(sf_w_in, gs_in, m_sf_w_in, v_sf_w_in, "w_in")
    r_o0 = big(sf_w_o, gs_o0, m_sf_w_o, v_sf_w_o, "w_o0")
    r_dn = big(mla_w_down, gs_dn, m_mla_w_down, v_mla_w_down, "w_dn")
    r_uq = big(mla_w_uq, gs_uq, m_mla_w_uq, v_mla_w_uq, "w_uq")
    r_ukv = big(mla_w_ukv, gs_ukv, m_mla_w_ukv, v_mla_w_ukv, "w_ukv")
    r_o1 = big(mla_w_o, gs_o1, m_mla_w_o, v_mla_w_o, "w_o1")

    g_small = (g_ln_mix, g_ln_mlp, g_b_f, g_fq, g_fk, g_qa, g_kva, g_q, g_k)

    def ordered(k, sm):
        return (sm[0], sm[1], r_in[k], sm[2], sm[3], sm[4], r_o0[k], r_dn[k], sm[5], sm[6], r_uq[k], r_ukv[k],
                sm[7], sm[8], r_o1[k], r_up[k], r_dw[k])

    return (loss, grad_x[None], *ordered(0, g_small), *ordered(1, d_small), *ordered(2, m_small),
            *ordered(3, v_small))
```

```python
import functools

import numpy as np
import jax
import jax.numpy as jnp
from jax import lax
from jax.experimental import pallas as pl
from jax.experimental.pallas import tpu as pltpu

F32 = jnp.float32
BF16 = jnp.bfloat16
MESH = pl.DeviceIdType.MESH

EPS = 1e-6
HEAD = 128
N_SB = 8
N_FOX = 8
N_MLA = 16
Q_RANK = 512
KV_RANK = 512
NOPE = 128
ROPE = 64
QK_DIM = NOPE + ROPE
QK_PAD = 256
ROPE_THETA = 10000.0
N_CHIPS = 4

ADAM_LR = 0.001
ADAM_B1 = 0.9
ADAM_B2 = 0.999
ADAM_EPS = 1e-08
ADAM_WD = 0.01
ADAM_STEP = 10

VMEM_LIMIT = 56 * 1024 * 1024
LANES = 128
NEG = -1e30


def _cp(*sem):
    return pltpu.CompilerParams(dimension_semantics=sem, vmem_limit_bytes=VMEM_LIMIT)


def _pick(dim, target):
    if dim <= target:
        return dim
    t = (target // LANES) * LANES
    while t >= LANES:
        if dim % t == 0:
            return t
        t -= LANES
    raise ValueError(f"no tile for {dim}")


NT_DIMS = (((1,), (1,)), ((), ()))
TN_DIMS = (((0,), (0,)), ((), ()))


def _dot(a, b):
    return jnp.dot(a, b, preferred_element_type=F32)


def _dot_nt(a, b):
    return lax.dot_general(a, b, NT_DIMS, preferred_element_type=F32)


def _dot_tn(a, b):
    return lax.dot_general(a, b, TN_DIMS, preferred_element_type=F32)


def _matmul(a, b, *, name, form="nn", out_dtype=F32, n=None, b_n0=0, b_split=False,
            out_split=False, epilogue="plain", res=None, u=None, tm=1024, tn=1024, tk=2048, comm=None):
    if form == "tn":
        K, M = a.shape
    else:
        M, K = a.shape
    if b_split:
        if form == "nt":
            nb_full, kb_full = b.shape[1], b.shape[2] * N_CHIPS
        else:
            kb_full, nb_full = b.shape[1], b.shape[2] * N_CHIPS
    elif form == "nt":
        nb_full, kb_full = b.shape
    else:
        kb_full, nb_full = b.shape
    assert kb_full == K, (name, a.shape, b.shape)
    N = nb_full if n is None else n
    if a.dtype != BF16 or b.dtype != BF16:
        tk = max(tk // 2, LANES)
    tm, tn, tk = _pick(M, tm), _pick(N, tn), _pick(K, tk)
    if b_split:
        per_chip = (b.shape[2])
        if form == "nt":
            tk = _pick(per_chip, tk)
        else:
            tn = _pick(per_chip, tn)
    if out_split:
        tn = _pick(N // N_CHIPS, tn)
    assert b_n0 % tn == 0
    nb0 = b_n0 // tn
    nk = K // tk
    grid = (M // tm, N // tn, nk)

    if form == "tn":
        a_spec = pl.BlockSpec((tk, tm), lambda i, j, k: (k, i))
    else:
        a_spec = pl.BlockSpec((tm, tk), lambda i, j, k: (i, k))
    if b_split:
        if form == "nt":
            kc = b.shape[2] // tk
            b_spec = pl.BlockSpec((None, tn, tk), lambda i, j, k: (k // kc, j, k % kc))
        else:
            nc = b.shape[2] // tn
            b_spec = pl.BlockSpec((None, tk, tn), lambda i, j, k: (j // nc, k, j % nc))
    elif form == "nt":
        b_spec = pl.BlockSpec((tn, tk), lambda i, j, k: (j + nb0, k))
    else:
        b_spec = pl.BlockSpec((tk, tn), lambda i, j, k: (k, j + nb0))
    mn_spec = pl.BlockSpec((tm, tn), lambda i, j, k: (i, j))
    if out_split:
        oc = (N // N_CHIPS) // tn
        out_spec = pl.BlockSpec((None, tm, tn), lambda i, j, k: (j // oc, i, j % oc))
        out_shape = jax.ShapeDtypeStruct((N_CHIPS, M, N // N_CHIPS), out_dtype)
    else:
        out_spec = mn_spec
        out_shape = jax.ShapeDtypeStruct((M, N), out_dtype)

    in_specs = [a_spec, b_spec]
    operands = [a, b]
    out_specs = (out_spec,)
    out_shape = (out_shape,)
    if epilogue == "res":
        in_specs.append(mn_spec)
        operands.append(res)
    elif epilogue == "sqrelu_bwd":
        in_specs.append(mn_spec)
        operands.append(u)
    elif epilogue == "sqrelu":
        out_specs = (mn_spec, mn_spec)
        out_shape = (jax.ShapeDtypeStruct((M, N), F32), jax.ShapeDtypeStruct((M, N), BF16))

    def finish(refs, r):
        if epilogue == "plain":
            refs[2][...] = r.astype(out_dtype)
        elif epilogue == "res":
            refs[3][...] = (refs[2][...] + r).astype(out_dtype)
        elif epilogue == "sqrelu":
            refs[2][...] = r
            p = jnp.maximum(r, 0.0)
            refs[3][...] = (p * p).astype(BF16)
        else:
            refs[3][...] = (r * (2.0 * jnp.maximum(refs[2][...], 0.0))).astype(out_dtype)

    def body(*refs):
        at = refs[0][...].astype(BF16)
        bt = refs[1][...].astype(BF16)
        if form == "nn":
            part = _dot(at, bt)
        elif form == "nt":
            part = _dot_nt(at, bt)
        else:
            part = _dot_tn(at, bt)
        if nk == 1:
            finish(refs, part)
            return
        acc = refs[-1]
        k = pl.program_id(2)

        @pl.when(k == 0)
        def _():
            acc[...] = part

        @pl.when(jnp.logical_and(k > 0, k < nk - 1))
        def _():
            acc[...] += part

        @pl.when(k == nk - 1)
        def _():
            finish(refs, acc[...] + part)

    outs, comm_outs = _hosted_call(
        body, grid=grid, in_specs=in_specs, out_specs=out_specs, out_shape=out_shape,
        scratch_shapes=[] if nk == 1 else [pltpu.VMEM((tm, tn), F32)], operands=operands, name=name,
        sem=("parallel", "parallel", "arbitrary"), comm=comm)
    result = outs if epilogue == "sqrelu" else outs[0]
    return result if comm is None else (result, comm_outs)


def _rms_fwd(x, g, *, name, c0=0, width=None, gw=None, tr=256):
    R, ctot = x.shape
    C = ctot if width is None else width
    gw = C if gw is None else gw
    assert c0 % C == 0 and C % gw == 0
    tr = _pick(R, tr)
    cb = c0 // C
    ng = C // gw

    def body(x_ref, g_ref, o_ref):
        gv = g_ref[...]
        for gi in range(ng):
            cols = slice(gi * gw, (gi + 1) * gw)
            xs = x_ref[:, cols]
            ms = jnp.sum(xs * xs, axis=-1, keepdims=True) * (1.0 / gw)
            o_ref[:, cols] = ((xs * lax.rsqrt(ms + EPS)) * gv).astype(o_ref.dtype)

    return pl.pallas_call(
        body, out_shape=jax.ShapeDtypeStruct((R, C), BF16), grid=(R // tr,),
        in_specs=[pl.BlockSpec((tr, C), lambda i: (i, cb)), pl.BlockSpec((1, gw), lambda i: (0, 0))],
        out_specs=pl.BlockSpec((tr, C), lambda i: (i, 0)), name=name,
        compiler_params=_cp("parallel"))(x, g.reshape(1, gw).astype(F32))


def _rms_bwd(x, g, dy, *, name, res=None, c0=0, width=None, gw=None, tr=256, bf16_copy=False):
    bf16_copy = int(bf16_copy)
    R, ctot = x.shape
    C = ctot if width is None else width
    gw = C if gw is None else gw
    tr = _pick(R, tr)
    cb = c0 // C
    ng = C // gw
    nsteps = R // tr
    row_spec = pl.BlockSpec((tr, C), lambda i: (i, 0))
    in_specs = [pl.BlockSpec((tr, C), lambda i: (i, cb)), pl.BlockSpec((1, gw), lambda i: (0, 0)), row_spec]
    operands = [x, g.reshape(1, gw).astype(F32), dy]
    if res is not None:
        in_specs.append(row_spec)
        operands.append(res)

    def body(*refs):
        x_ref, g_ref, dy_ref = refs[:3]
        res_ref = refs[3] if res is not None else None
        dx_ref, dg_ref = refs[-3 - bf16_copy], refs[-2]
        acc = refs[-1]
        i = pl.program_id(0)

        @pl.when(i == 0)
        def _():
            acc[...] = jnp.zeros_like(acc)

        gv = g_ref[...]
        for gi in range(ng):
            cols = slice(gi * gw, (gi + 1) * gw)
            xs = x_ref[:, cols]
            dys = dy_ref[:, cols].astype(F32)
            rstd = lax.rsqrt(jnp.sum(xs * xs, axis=-1, keepdims=True) * (1.0 / gw) + EPS)
            xh = xs * rstd
            gdy = dys * gv
            m = jnp.sum(gdy * xh, axis=-1, keepdims=True) * (1.0 / gw)
            dx = rstd * (gdy - xh * m)
            if res_ref is not None:
                dx = dx + res_ref[:, cols]
            dx_ref[:, cols] = dx
            if bf16_copy:
                refs[-3][:, cols] = dx.astype(BF16)
            acc[...] += jnp.sum((dys * xh).reshape(tr // 8, 8, gw), axis=0)

        @pl.when(i == nsteps - 1)
        def _():
            dg_ref[...] = jnp.sum(acc[...], axis=0, keepdims=True)

    out_shape = [jax.ShapeDtypeStruct((R, C), F32)] + [jax.ShapeDtypeStruct((R, C), BF16)] * bf16_copy
    outs = pl.pallas_call(
        body, out_shape=tuple(out_shape + [jax.ShapeDtypeStruct((1, gw), F32)]),
        grid=(nsteps,), in_specs=in_specs,
        out_specs=tuple([row_spec] * len(out_shape) + [pl.BlockSpec((1, gw), lambda i: (0, 0))]),
        scratch_shapes=[pltpu.VMEM((8, gw), F32)], name=name,
        compiler_params=_cp("arbitrary"))(*operands)
    return (*outs[:-1], outs[-1][0])


def _split3(x):
    hi = x.astype(BF16)
    r1 = x - hi.astype(F32)
    mid = r1.astype(BF16)
    lo = (r1 - mid.astype(F32)).astype(BF16)
    return hi, mid, lo


def _log_sigmoid(z):
    return jnp.minimum(z, 0.0) - jnp.log(1.0 + jnp.exp(-jnp.abs(z)))


def _forget_fwd(fl, b, *, name, tb=512):
    S = fl.shape[0]
    tb = _pick(S, tb)

    def body(fl_ref, b_ref, f_ref, carry):
        i = pl.program_id(0)

        @pl.when(i == 0)
        def _():
            carry[...] = jnp.zeros_like(carry)

        lf = _log_sigmoid(fl_ref[...] + b_ref[...])
        r = lax.broadcasted_iota(jnp.int32, (tb, tb), 0)
        c = lax.broadcasted_iota(jnp.int32, (tb, tb), 1)
        tri = (c <= r).astype(BF16)
        hi, mid, lo = _split3(lf)
        cs = _dot(tri, hi) + _dot(tri, mid) + _dot(tri, lo)
        f_ref[...] = cs + carry[...]
        carry[...] += jnp.sum(lf, axis=0, keepdims=True)

    return pl.pallas_call(
        body, out_shape=jax.ShapeDtypeStruct((S, LANES), F32), grid=(S // tb,),
        in_specs=[pl.BlockSpec((tb, LANES), lambda i: (i, 0)), pl.BlockSpec((1, LANES), lambda i: (0, 0))],
        out_specs=pl.BlockSpec((tb, LANES), lambda i: (i, 0)),
        scratch_shapes=[pltpu.VMEM((1, LANES), F32)], name=name,
        compiler_params=_cp("arbitrary"))(fl, b)


def _forget_bwd(fl, b, dF, *, name, tb=512):
    S = fl.shape[0]
    tb = _pick(S, tb)
    nb = S // tb

    def body(fl_ref, b_ref, df_ref, dfl_ref, db_ref, carry, acc):
        i = pl.program_id(0)

        @pl.when(i == 0)
        def _():
            carry[...] = jnp.zeros_like(carry)
            acc[...] = jnp.zeros_like(acc)

        d = df_ref[...]
        r = lax.broadcasted_iota(jnp.int32, (tb, tb), 0)
        c = lax.broadcasted_iota(jnp.int32, (tb, tb), 1)
        tri = (c >= r).astype(BF16)
        hi, mid, lo = _split3(d)
        rc = _dot(tri, hi) + _dot(tri, mid) + _dot(tri, lo) + carry[...]
        z = fl_ref[...] + b_ref[...]
        dfl = rc * jnp.exp(_log_sigmoid(-z))
        dfl_ref[...] = dfl
        carry[...] += jnp.sum(d, axis=0, keepdims=True)
        acc[...] += jnp.sum(dfl, axis=0, keepdims=True)

        @pl.when(i == nb - 1)
        def _():
            db_ref[...] = acc[...]

    rev = lambda i: (nb - 1 - i, 0)
    dfl, db = pl.pallas_call(
        body, out_shape=(jax.ShapeDtypeStruct((S, LANES), F32), jax.ShapeDtypeStruct((1, LANES), F32)),
        grid=(nb,),
        in_specs=[pl.BlockSpec((tb, LANES), rev), pl.BlockSpec((1, LANES), lambda i: (0, 0)),
                  pl.BlockSpec((tb, LANES), rev)],
        out_specs=(pl.BlockSpec((tb, LANES), rev), pl.BlockSpec((1, LANES), lambda i: (0, 0))),
        scratch_shapes=[pltpu.VMEM((1, LANES), F32), pltpu.VMEM((1, LANES), F32)], name=name,
        compiler_params=_cp("arbitrary"))(fl, b, dF)
    return dfl, db[0]


def _tri(tk, rel):
    r = lax.broadcasted_iota(jnp.int32, (tk, tk), 0)
    c = lax.broadcasted_iota(jnp.int32, (tk, tk), 1)
    m = {"gt": r > c, "le": r <= c, "lt": r < c}[rel]
    return m.astype(BF16)


def _split2(x):
    hi = x.astype(BF16)
    return hi, (x - hi.astype(F32)).astype(BF16)


HEADS_PER_STEP = 2
CUM_CHUNK = 256


def _cum_cols(x, tri, suffix):
    ck = tri.shape[0]
    n = x.shape[1] // ck
    hi, lo = _split2(x)
    parts, sums = [], []
    for c in range(n):
        cs = slice(c * ck, (c + 1) * ck)
        parts.append(_dot(hi[:, cs], tri) + _dot(lo[:, cs], tri))
        sums.append(jnp.sum(x[:, cs], axis=1, keepdims=True))
    carry = None
    for c in (reversed(range(n)) if suffix else range(n)):
        if carry is not None:
            parts[c] = parts[c] + carry
        carry = sums[c] if carry is None else carry + sums[c]
    return (parts[0] if n == 1 else jnp.concatenate(parts, axis=1)), carry


def _diag_mask(tq, strict):
    r = lax.broadcasted_iota(jnp.int32, (tq, tq), 0)
    c = lax.broadcasted_iota(jnp.int32, (tq, tq), 1)
    return c < r if strict else c <= r


def _sb_fwd(qkv, *, name, n_heads=N_SB, q_off=0, k_off=N_SB, v_off=2 * N_SB, tq=512, hp=HEADS_PER_STEP,
            comm=None):
    S = qkv.shape[0]
    tq = _pick(S, tq)
    tk = tq
    scale = HEAD ** -0.5
    nq = S // tq
    assert n_heads % hp == 0 and q_off % hp == 0 and k_off % hp == 0 and v_off % hp == 0

    def body(q_ref, k_ref, v_ref, o_ref, t_ref, c_sc, acc_sc):
        qi = pl.program_id(1)
        c_sc[...] = jnp.zeros_like(c_sc)
        acc_sc[...] = jnp.zeros_like(acc_sc)
        gt = _tri(min(CUM_CHUNK, tk), "gt")

        def tile(hh, j, diag):
            cs = slice(hh * HEAD, (hh + 1) * HEAD)
            rows = pl.ds(pl.multiple_of(j * tk, tk), tk)
            z = _dot_nt(q_ref[:, cs], k_ref[rows, cs]) * scale
            sp = jnp.log(1.0 + jnp.exp(-jnp.abs(z)))
            la = jnp.minimum(z, 0.0) - sp
            lb = -jnp.maximum(z, 0.0) - sp
            if diag:
                strict = _diag_mask(tq, True)
                lb = jnp.where(strict, lb, 0.0)
            suffix, total = _cum_cols(lb, gt, suffix=True)
            w = jnp.exp(la + suffix + c_sc[hh])
            if diag:
                w = jnp.where(strict, w, 0.0)
            acc_sc[hh] += _dot(w.astype(BF16), v_ref[rows, cs])
            c_sc[hh] += total

        for hh in range(hp):
            tile(hh, qi, True)

        def step(it, carry):
            for hh in range(hp):
                tile(hh, qi - 1 - it, False)
            return carry

        lax.fori_loop(0, qi, step, 0)
        for hh in range(hp):
            o_ref[:, hh * HEAD:(hh + 1) * HEAD] = acc_sc[hh]
            t_ref[hh] = jnp.broadcast_to(c_sc[hh], (tq, LANES))

    w = hp * HEAD
    head_blk = lambda off: pl.BlockSpec((S, w), lambda h, i: (0, h + off // hp))
    outs, comm_outs = _hosted_call(
        body,
        out_shape=(jax.ShapeDtypeStruct((S, n_heads * HEAD), F32),
                   jax.ShapeDtypeStruct((n_heads, S, LANES), F32)),
        grid=(n_heads // hp, nq),
        in_specs=[pl.BlockSpec((tq, w), lambda h, i: (i, h + q_off // hp)), head_blk(k_off), head_blk(v_off)],
        out_specs=(pl.BlockSpec((tq, w), lambda h, i: (i, h)),
                   pl.BlockSpec((hp, tq, LANES), lambda h, i: (h, i, 0))),
        scratch_shapes=[pltpu.VMEM((hp, tq, 1), F32), pltpu.VMEM((hp, tq, HEAD), F32)], name=name,
        sem=("parallel", "arbitrary"), operands=(qkv, qkv, qkv), comm=comm)
    return outs if comm is None else (outs, comm_outs)


def _sb_bwd(qkv, do, tstat, *, name, n_heads=N_SB, q_off=0, k_off=N_SB, v_off=2 * N_SB, do_off=0, tq=512,
            hp=HEADS_PER_STEP, comm=None):
    S = qkv.shape[0]
    tq = _pick(S, tq)
    tk = tq
    scale = HEAD ** -0.5
    nq = S // tq
    assert n_heads % hp == 0 and q_off % hp == 0 and k_off % hp == 0 and v_off % hp == 0 and do_off % hp == 0

    def body(q_ref, k_ref, v_ref, do_ref, t_ref, dq_ref, dk_ref, dv_ref, p_sc, r_sc, dq_sc):
        qi = pl.program_id(1)

        @pl.when(qi == 0)
        def _():
            dk_ref[...] = jnp.zeros_like(dk_ref)
            dv_ref[...] = jnp.zeros_like(dv_ref)

        p_sc[...] = jnp.zeros_like(p_sc)
        r_sc[...] = jnp.zeros_like(r_sc)
        dq_sc[...] = jnp.zeros_like(dq_sc)
        le = _tri(min(CUM_CHUNK, tk), "le")
        lt = _tri(min(CUM_CHUNK, tk), "lt")

        def tile(hh, j, diag):
            cs = slice(hh * HEAD, (hh + 1) * HEAD)
            rows = pl.ds(pl.multiple_of(j * tk, tk), tk)
            q = q_ref[:, cs]
            do_b = do_ref[:, cs].astype(BF16)
            kb = k_ref[rows, cs]
            z = _dot_nt(q, kb) * scale
            sp = jnp.log(1.0 + jnp.exp(-jnp.abs(z)))
            la = jnp.minimum(z, 0.0) - sp
            lb = -jnp.maximum(z, 0.0) - sp
            if diag:
                strict = _diag_mask(tq, True)
                lb = jnp.where(strict, lb, 0.0)
            prefix, total_b = _cum_cols(lb, le, suffix=False)
            w = jnp.exp(la + t_ref[hh, :, 0:1] - (prefix + p_sc[hh]))
            if diag:
                w = jnp.where(strict, w, 0.0)
            r = w * _dot_nt(do_b, v_ref[rows, cs])
            rex, total_r = _cum_cols(r, lt, suffix=False)
            rex = rex + r_sc[hh]
            beta = jnp.exp(la)
            dz = r - beta * (r + rex)
            if diag:
                dz = jnp.where(strict, dz, 0.0)
            dzb = dz.astype(BF16)
            dq_sc[hh] += _dot(dzb, kb)
            dk_ref[rows, cs] += _dot_tn(dzb, q)
            dv_ref[rows, cs] += _dot_tn(w.astype(BF16), do_b)
            p_sc[hh] += total_b
            r_sc[hh] += total_r

        def step(j, carry):
            for hh in range(hp):
                tile(hh, j, False)
            return carry

        lax.fori_loop(0, qi, step, 0)
        for hh in range(hp):
            tile(hh, qi, True)
            dq_ref[:, hh * HEAD:(hh + 1) * HEAD] = dq_sc[hh] * scale

        @pl.when(qi == nq - 1)
        def _():
            dk_ref[...] = dk_ref[...] * scale

    w = hp * HEAD
    head_blk = lambda off: pl.BlockSpec((S, w), lambda h, i: (0, h + off // hp))
    out_head = pl.BlockSpec((S, w), lambda h, i: (0, h))
    out_sd = jax.ShapeDtypeStruct((S, n_heads * HEAD), F32)
    outs, comm_outs = _hosted_call(
        body, out_shape=(out_sd, out_sd, out_sd), grid=(n_heads // hp, nq),
        in_specs=[pl.BlockSpec((tq, w), lambda h, i: (i, h + q_off // hp)), head_blk(k_off), head_blk(v_off),
                  pl.BlockSpec((tq, w), lambda h, i: (i, h + do_off // hp)),
                  pl.BlockSpec((hp, tq, LANES), lambda h, i: (h, i, 0))],
        out_specs=(pl.BlockSpec((tq, w), lambda h, i: (i, h)), out_head, out_head),
        scratch_shapes=[pltpu.VMEM((hp, tq, 1), F32), pltpu.VMEM((hp, tq, 1), F32), pltpu.VMEM((hp, tq, HEAD), F32)],
        name=name, sem=("parallel", "arbitrary"), operands=(qkv, qkv, qkv, do, tstat), comm=comm)
    return outs if comm is None else (outs, comm_outs)


def _attn_fwd(q, k, v, bias, *, name, n_heads, dqk, scale, v_off=0, tq=1024, exact_p=False, hp=HEADS_PER_STEP,
              comm=None):
    S = q.shape[0]
    tq = _pick(S, tq)
    tk = tq
    nq = S // tq
    has_bias = bias is not None

    assert n_heads % hp == 0 and v_off % hp == 0

    def body(*refs):
        q_ref, k_ref, v_ref = refs[:3]
        b_ref = refs[3] if has_bias else None
        o_ref, lse_ref, m_sc, l_sc, acc_sc = refs[-5:]
        qi = pl.program_id(1)
        m_sc[...] = jnp.full_like(m_sc, NEG)
        l_sc[...] = jnp.zeros_like(l_sc)
        acc_sc[...] = jnp.zeros_like(acc_sc)

        def tile(hh, j, diag):
            rows = pl.ds(pl.multiple_of(j * tk, tk), tk)
            s = _dot_nt(q_ref[:, hh * dqk:(hh + 1) * dqk], k_ref[rows, hh * dqk:(hh + 1) * dqk]) * scale
            if has_bias:
                s = s + b_ref[hh, :, rows]
            if diag:
                s = jnp.where(_diag_mask(tq, False), s, NEG)
            m_old = m_sc[hh]
            m_new = jnp.maximum(m_old, jnp.max(s, axis=1, keepdims=True))
            alpha = jnp.exp(m_old - m_new)
            p = jnp.exp(s - m_new)
            l_sc[hh] = alpha * l_sc[hh] + jnp.sum(p, axis=1, keepdims=True)
            vb = v_ref[rows, hh * HEAD:(hh + 1) * HEAD]
            if exact_p:
                hi, lo = _split2(p)
                pv = _dot(hi, vb) + _dot(lo, vb)
            else:
                pv = _dot(p.astype(BF16), vb)
            acc_sc[hh] = alpha * acc_sc[hh] + pv
            m_sc[hh] = m_new

        def step(j, carry):
            for hh in range(hp):
                tile(hh, j, False)
            return carry

        lax.fori_loop(0, qi, step, 0)
        for hh in range(hp):
            tile(hh, qi, True)
            l = l_sc[hh]
            o_ref[:, hh * HEAD:(hh + 1) * HEAD] = acc_sc[hh] / l
            lse_ref[hh] = jnp.broadcast_to(m_sc[hh] + jnp.log(l), (tq, LANES))

    in_specs = [pl.BlockSpec((tq, hp * dqk), lambda h, i: (i, h)),
                pl.BlockSpec((S, hp * dqk), lambda h, i: (0, h)),
                pl.BlockSpec((S, hp * HEAD), lambda h, i: (0, h + v_off // hp))]
    operands = [q, k, v]
    if has_bias:
        in_specs.append(pl.BlockSpec((hp, 1, S), lambda h, i: (h, 0, 0)))
        operands.append(bias)
    outs, comm_outs = _hosted_call(
        body,
        out_shape=(jax.ShapeDtypeStruct((S, n_heads * HEAD), F32),
                   jax.ShapeDtypeStruct((n_heads, S, LANES), F32)),
        grid=(n_heads // hp, nq), in_specs=in_specs,
        out_specs=(pl.BlockSpec((tq, hp * HEAD), lambda h, i: (i, h)),
                   pl.BlockSpec((hp, tq, LANES), lambda h, i: (h, i, 0))),
        scratch_shapes=[pltpu.VMEM((hp, tq, 1), F32), pltpu.VMEM((hp, tq, 1), F32), pltpu.VMEM((hp, tq, HEAD), F32)],
        name=name, sem=("parallel", "arbitrary"), operands=operands, comm=comm)
    return outs if comm is None else (outs, comm_outs)


def _attn_bwd(q, k, v, bias, o, lse, do, *, name, n_heads, dqk, scale, v_off=0, do_off=0, tq=512,
              hp=HEADS_PER_STEP, comm=None):
    S = q.shape[0]
    tq = _pick(S, tq)
    tk = tq
    nq = S // tq
    has_bias = bias is not None
    assert n_heads % hp == 0 and v_off % hp == 0 and do_off % hp == 0

    def body(*refs):
        q_ref, k_ref, v_ref, o_ref, lse_ref, do_ref = refs[:6]
        b_ref = refs[6] if has_bias else None
        n_out = 5 if has_bias else 3
        outs = refs[-(n_out + 3):-3]
        dq_ref, dk_ref, dv_ref = outs[:3]
        db_ref, dr_ref = (outs[3], outs[4]) if has_bias else (None, None)
        dq_sc, rs_sc, delta_sc = refs[-3:]
        qi = pl.program_id(1)

        @pl.when(qi == 0)
        def _():
            dk_ref[...] = jnp.zeros_like(dk_ref)
            dv_ref[...] = jnp.zeros_like(dv_ref)
            if has_bias:
                db_ref[...] = jnp.zeros_like(db_ref)

        dq_sc[...] = jnp.zeros_like(dq_sc)
        rs_sc[...] = jnp.zeros_like(rs_sc)
        for hh in range(hp):
            vs = slice(hh * HEAD, (hh + 1) * HEAD)
            do_r = do_ref[:, vs].astype(BF16).astype(F32)
            delta_sc[hh] = jnp.sum(do_r * o_ref[:, vs], axis=1, keepdims=True)

        def tile(hh, j, diag):
            qs = slice(hh * dqk, (hh + 1) * dqk)
            vs = slice(hh * HEAD, (hh + 1) * HEAD)
            rows = pl.ds(pl.multiple_of(j * tk, tk), tk)
            qb = q_ref[:, qs]
            do_b = do_ref[:, vs].astype(BF16)
            delta = delta_sc[hh]
            kb = k_ref[rows, qs]
            s = _dot_nt(qb, kb) * scale
            if has_bias:
                s = s + b_ref[hh, :, rows]
            p = jnp.exp(s - lse_ref[hh, :, 0:1])
            if diag:
                p = jnp.where(_diag_mask(tq, False), p, 0.0)
            ds = p * (_dot_nt(do_b, v_ref[rows, vs]) - delta)
            dsb = (ds * scale).astype(BF16)
            dq_sc[hh] += _dot(dsb, kb)
            dk_ref[rows, qs] += _dot_tn(dsb, qb)
            dv_ref[rows, vs] += _dot_tn(p.astype(BF16), do_b)
            if has_bias:
                db_ref[hh, :, rows] += jnp.sum(ds, axis=0, keepdims=True)
                rs_sc[hh] += jnp.sum(ds, axis=1, keepdims=True)

        def step(j, carry):
            for hh in range(hp):
                tile(hh, j, False)
            return carry

        lax.fori_loop(0, qi, step, 0)
        for hh in range(hp):
            tile(hh, qi, True)
            dq_ref[:, hh * dqk:(hh + 1) * dqk] = dq_sc[hh]
            if has_bias:
                dr_ref[hh] = jnp.broadcast_to(rs_sc[hh], (tq, LANES))

    stat = pl.BlockSpec((hp, tq, LANES), lambda h, i: (h, i, 0))
    in_specs = [pl.BlockSpec((tq, hp * dqk), lambda h, i: (i, h)),
                pl.BlockSpec((S, hp * dqk), lambda h, i: (0, h)),
                pl.BlockSpec((S, hp * HEAD), lambda h, i: (0, h + v_off // hp)),
                pl.BlockSpec((tq, hp * HEAD), lambda h, i: (i, h)),
                stat,
                pl.BlockSpec((tq, hp * HEAD), lambda h, i: (i, h + do_off // hp))]
    operands = [q, k, v, o, lse, do]
    out_shape = [jax.ShapeDtypeStruct((S, n_heads * dqk), F32), jax.ShapeDtypeStruct((S, n_heads * dqk), F32),
                 jax.ShapeDtypeStruct((S, n_heads * HEAD), F32)]
    out_specs = [pl.BlockSpec((tq, hp * dqk), lambda h, i: (i, h)), pl.BlockSpec((S, hp * dqk), lambda h, i: (0, h)),
                 pl.BlockSpec((S, hp * HEAD), lambda h, i: (0, h))]
    if has_bias:
        in_specs.append(pl.BlockSpec((hp, 1, S), lambda h, i: (h, 0, 0)))
        operands.append(bias)
        out_shape.append(jax.ShapeDtypeStruct((n_heads, 1, S), F32))
        out_specs.append(pl.BlockSpec((hp, 1, S), lambda h, i: (h, 0, 0)))
        out_shape.append(jax.ShapeDtypeStruct((n_heads, S, LANES), F32))
        out_specs.append(stat)
    outs, comm_outs = _hosted_call(
        body, out_shape=tuple(out_shape), grid=(n_heads // hp, nq), in_specs=in_specs, out_specs=tuple(out_specs),
        scratch_shapes=[pltpu.VMEM((hp, tq, dqk), F32), pltpu.VMEM((hp, tq, 1), F32),
                        pltpu.VMEM((hp, tq, 1), F32)], name=name,
        sem=("parallel", "arbitrary"), operands=operands, comm=comm)
    return outs if comm is None else (outs, comm_outs)


def _rot_half(y):
    lane = lax.broadcasted_iota(jnp.int32, y.shape, 1)
    up = pltpu.roll(y, 96, 1)
    down = pltpu.roll(y, 32, 1)
    return jnp.where(lane < 32, -up, jnp.where(lane < 64, down, 0.0))


def _mla_prep_fwd(q_raw, kv_raw, down, cos, sin, q_g, k_g, *, name, ts=128):
    S = q_raw.shape[0]
    ts = _pick(S, ts)
    pe_blk = Q_RANK // LANES + KV_RANK // LANES

    def norm_rope(x0, x1, g0, g1, c, s):
        ms = (jnp.sum(x0 * x0, axis=-1, keepdims=True) + jnp.sum(x1 * x1, axis=-1, keepdims=True)) * (1.0 / QK_DIM)
        rstd = lax.rsqrt(ms + EPS)
        y0 = (x0 * rstd) * g0
        y1 = (x1 * rstd) * g1
        return y0, y1 * c + _rot_half(y1) * s

    def body(q_ref, kv_ref, pe_ref, cos_ref, sin_ref, qg_ref, kg_ref, qo_ref, ko_ref, vo_ref):
        c, s = cos_ref[...], sin_ref[...]
        pe = pe_ref[...]
        qg0, qg1 = qg_ref[:, :NOPE], qg_ref[:, NOPE:]
        kg0, kg1 = kg_ref[:, :NOPE], kg_ref[:, NOPE:]
        for h in range(N_MLA):
            b = h * QK_PAD
            y0, y1 = norm_rope(q_ref[:, b:b + NOPE], q_ref[:, b + NOPE:b + QK_PAD], qg0, qg1, c, s)
            qo_ref[:, b:b + NOPE] = y0.astype(BF16)
            qo_ref[:, b + NOPE:b + QK_PAD] = y1.astype(BF16)
            y0, y1 = norm_rope(kv_ref[:, b:b + NOPE], pe, kg0, kg1, c, s)
            ko_ref[:, b:b + NOPE] = y0.astype(BF16)
            ko_ref[:, b + NOPE:b + QK_PAD] = y1.astype(BF16)
            vo_ref[:, h * HEAD:(h + 1) * HEAD] = kv_ref[:, b + NOPE:b + QK_PAD].astype(BF16)

    wide = pl.BlockSpec((ts, N_MLA * QK_PAD), lambda i: (i, 0))
    lane_blk = pl.BlockSpec((ts, LANES), lambda i: (i, 0))
    gain = pl.BlockSpec((1, QK_PAD), lambda i: (0, 0))
    return pl.pallas_call(
        body,
        out_shape=(jax.ShapeDtypeStruct((S, N_MLA * QK_PAD), BF16), jax.ShapeDtypeStruct((S, N_MLA * QK_PAD), BF16),
                   jax.ShapeDtypeStruct((S, N_MLA * HEAD), BF16)),
        grid=(S // ts,),
        in_specs=[wide, wide, pl.BlockSpec((ts, LANES), lambda i: (i, pe_blk)), lane_blk, lane_blk, gain, gain],
        out_specs=(wide, wide, pl.BlockSpec((ts, N_MLA * HEAD), lambda i: (i, 0))), name=name,
        compiler_params=_cp("parallel"))(q_raw, kv_raw, down, cos, sin, q_g, k_g)


def _mla_prep_bwd(dq, dk, dv, q_raw, kv_raw, down, cos, sin, q_g, k_g, *, name, ts=128, comm=None):
    S = q_raw.shape[0]
    ts = _pick(S, ts)
    nsteps = S // ts
    pe_blk = Q_RANK // LANES + KV_RANK // LANES

    def back(x0, x1, g0, g1, c, s, d0, d1r):
        d1 = d1r * c - _rot_half(d1r * s)
        ms = (jnp.sum(x0 * x0, axis=-1, keepdims=True) + jnp.sum(x1 * x1, axis=-1, keepdims=True)) * (1.0 / QK_DIM)
        rstd = lax.rsqrt(ms + EPS)
        h0, h1 = x0 * rstd, x1 * rstd
        e0, e1 = d0 * g0, d1 * g1
        m = (jnp.sum(e0 * h0, axis=-1, keepdims=True) + jnp.sum(e1 * h1, axis=-1, keepdims=True)) * (1.0 / QK_DIM)
        return rstd * (e0 - h0 * m), rstd * (e1 - h1 * m), d0 * h0, d1 * h1

    def fold(a):
        return jnp.sum(a.reshape(ts // 8, 8, a.shape[-1]), axis=0)

    def body(dq_ref, dk_ref, dv_ref, q_ref, kv_ref, pe_ref, cos_ref, sin_ref, qg_ref, kg_ref,
             dqr_ref, dkv_ref, dpe_ref, dqg_ref, dkg_ref, gq_sc, gk_sc):
        i = pl.program_id(0)

        @pl.when(i == 0)
        def _():
            gq_sc[...] = jnp.zeros_like(gq_sc)
            gk_sc[...] = jnp.zeros_like(gk_sc)

        c, s = cos_ref[...], sin_ref[...]
        pe = pe_ref[...]
        qg0, qg1 = qg_ref[:, :NOPE], qg_ref[:, NOPE:]
        kg0, kg1 = kg_ref[:, :NOPE], kg_ref[:, NOPE:]
        dpe = jnp.zeros((ts, LANES), F32)
        for h in range(N_MLA):
            b = h * QK_PAD
            dx0, dx1, a0, a1 = back(q_ref[:, b:b + NOPE], q_ref[:, b + NOPE:b + QK_PAD], qg0, qg1, c, s,
                                    dq_ref[:, b:b + NOPE], dq_ref[:, b + NOPE:b + QK_PAD])
            dqr_ref[:, b:b + NOPE] = dx0.astype(BF16)
            dqr_ref[:, b + NOPE:b + QK_PAD] = dx1.astype(BF16)
            gq_sc[:, :NOPE] += fold(a0)
            gq_sc[:, NOPE:] += fold(a1)
            dx0, dx1, a0, a1 = back(kv_ref[:, b:b + NOPE], pe, kg0, kg1, c, s,
                                    dk_ref[:, b:b + NOPE], dk_ref[:, b + NOPE:b + QK_PAD])
            dkv_ref[:, b:b + NOPE] = dx0.astype(BF16)
            dkv_ref[:, b + NOPE:b + QK_PAD] = dv_ref[:, h * HEAD:(h + 1) * HEAD].astype(BF16)
            dpe = dpe + dx1
            gk_sc[:, :NOPE] += fold(a0)
            gk_sc[:, NOPE:] += fold(a1)
        dpe_ref[...] = dpe

        @pl.when(i == nsteps - 1)
        def _():
            dqg_ref[...] = jnp.sum(gq_sc[...], axis=0, keepdims=True)
            dkg_ref[...] = jnp.sum(gk_sc[...], axis=0, keepdims=True)

    wide = pl.BlockSpec((ts, N_MLA * QK_PAD), lambda i: (i, 0))
    lane_blk = pl.BlockSpec((ts, LANES), lambda i: (i, 0))
    gain = pl.BlockSpec((1, QK_PAD), lambda i: (0, 0))
    outs, comm_outs = _hosted_call(
        body,
        out_shape=(jax.ShapeDtypeStruct((S, N_MLA * QK_PAD), BF16), jax.ShapeDtypeStruct((S, N_MLA * QK_PAD), BF16),
                   jax.ShapeDtypeStruct((S, LANES), F32), jax.ShapeDtypeStruct((1, QK_PAD), F32),
                   jax.ShapeDtypeStruct((1, QK_PAD), F32)),
        grid=(nsteps,),
        in_specs=[wide, wide, pl.BlockSpec((ts, N_MLA * HEAD), lambda i: (i, 0)), wide, wide,
                  pl.BlockSpec((ts, LANES), lambda i: (i, pe_blk)), lane_blk, lane_blk, gain, gain],
        out_specs=(wide, wide, lane_blk, gain, gain),
        scratch_shapes=[pltpu.VMEM((8, QK_PAD), F32), pltpu.VMEM((8, QK_PAD), F32)], name=name,
        sem=("arbitrary",), operands=(dq, dk, dv, q_raw, kv_raw, down, cos, sin, q_g, k_g), comm=comm)
    res = (outs[0], outs[1], outs[2], outs[3][0], outs[4][0])
    return res if comm is None else (res, comm_outs)


def _loss_head(y, target, *, name, tr=256):
    R, C = y.shape
    tr = _pick(R, tr)
    nsteps = R // tr

    def body(y_ref, t_ref, dy_ref, dyb_ref, loss_ref, acc):
        i = pl.program_id(0)

        @pl.when(i == 0)
        def _():
            acc[...] = jnp.zeros_like(acc)

        err = y_ref[...] - t_ref[...]
        dy = err * (1.0 / C)
        dy_ref[...] = dy
        dyb_ref[...] = dy.astype(BF16)
        acc[...] += jnp.sum((err * err).reshape(tr // 8, 8, C), axis=0)

        @pl.when(i == nsteps - 1)
        def _():
            tot = jnp.sum(jnp.sum(acc[...], axis=0, keepdims=True), axis=1, keepdims=True)
            loss_ref[...] = jnp.broadcast_to(tot * (0.5 / C), (8, LANES))

    blk = pl.BlockSpec((tr, C), lambda i: (i, 0))
    dy, dy_b, loss = pl.pallas_call(
        body, out_shape=(jax.ShapeDtypeStruct((R, C), F32), jax.ShapeDtypeStruct((R, C), BF16),
                         jax.ShapeDtypeStruct((8, LANES), F32)),
        grid=(nsteps,), in_specs=[blk, blk], out_specs=(blk, blk, pl.BlockSpec((8, LANES), lambda i: (0, 0))),
        scratch_shapes=[pltpu.VMEM((8, C), F32)], name=name, compiler_params=_cp("arbitrary"))(y, target)
    return dy, dy_b, loss[0, 0]


def _adamw(w, g, m, v, *, name, block_bytes=1 << 20, comm=None):
    L, R, C = w.shape
    tr = max(8, min(R, (block_bytes // (4 * C)) // 8 * 8))
    while R % tr:
        tr -= 8
    if tr <= 0:
        tr = R
    c1 = 1.0 / (1.0 - ADAM_B1 ** ADAM_STEP)
    c2 = 1.0 / (1.0 - ADAM_B2 ** ADAM_STEP)

    def body(w_ref, g_ref, m_ref, v_ref, d_ref, mo_ref, vo_ref):
        gv = g_ref[...]
        mn = ADAM_B1 * m_ref[...] + (1.0 - ADAM_B1) * gv
        vn = ADAM_B2 * v_ref[...] + (1.0 - ADAM_B2) * (gv * gv)
        d_ref[...] = -ADAM_LR * ((mn * c1) / (jnp.sqrt(vn * c2) + ADAM_EPS) + ADAM_WD * w_ref[...])
        mo_ref[...] = mn
        vo_ref[...] = vn

    blk = pl.BlockSpec((None, tr, C), lambda l, i: (l, i, 0))
    sd = jax.ShapeDtypeStruct((L, R, C), F32)
    outs, comm_outs = _hosted_call(
        body, out_shape=(sd, sd, sd), grid=(L, R // tr), in_specs=[blk] * 4, out_specs=(blk,) * 3, scratch_shapes=[],
        name=name, sem=("parallel", "parallel"), operands=(w, g, m, v), comm=comm)
    return outs if comm is None else (outs, comm_outs)


def _row_tile(r, c, itemsize=4, block_bytes=1 << 20):
    tr = max(16, min(r, (block_bytes // (itemsize * c)) // 16 * 16))
    while r % tr:
        tr -= 16
    return tr if tr > 0 else r


def _add_sibling(g, recv, core, *, name):
    nch, _, r, c = g.shape
    tr = _row_tile(r, c)

    def body(core_ref, g_ref, r_ref, o_ref):
        o_ref[...] = (g_ref[...] + r_ref[...]).astype(BF16)

    grid_spec = pltpu.PrefetchScalarGridSpec(
        num_scalar_prefetch=1, grid=(nch, r // tr),
        in_specs=[pl.BlockSpec((None, None, tr, c), lambda j, i, cr: (j, cr[0], i, 0)),
                  pl.BlockSpec((None, tr, c), lambda j, i, cr: (j, i, 0))],
        out_specs=pl.BlockSpec((None, tr, c), lambda j, i, cr: (j, i, 0)))
    return pl.pallas_call(
        body, out_shape=jax.ShapeDtypeStruct((nch, r, c), BF16), grid_spec=grid_spec, name=name,
        compiler_params=_cp("parallel", "parallel"))(core, g, recv)


def _add_chips(slots, *, name):
    nch, r, c = slots.shape
    tr = _row_tile(r, c)

    def body(s_ref, o_ref):
        acc = s_ref[0].astype(F32)
        for j in range(1, nch):
            acc = acc + s_ref[j].astype(F32)
        o_ref[...] = acc

    return pl.pallas_call(
        body, out_shape=jax.ShapeDtypeStruct((r, c), F32), grid=(r // tr,),
        in_specs=[pl.BlockSpec((nch, tr, c), lambda i: (0, i, 0))],
        out_specs=pl.BlockSpec((tr, c), lambda i: (i, 0)), name=name, compiler_params=_cp("parallel"))(slots)


def _place():
    x, y, c = lax.axis_index("x"), lax.axis_index("y"), lax.axis_index("c")
    others = [(1 - x, y), (x, 1 - y), (1 - x, 1 - y)]
    return x, y, c, 2 * x + y, others


ANY = pl.BlockSpec(memory_space=pl.ANY)


class _Exchange:
    def __init__(self, kind, arrays):
        self.kind, self.ins = kind, list(arrays)
        self.n_peers = 1 if kind in ("swap", "join") else 3
        self.aliased = kind == "forward"
        n = len(self.ins) * self.n_peers
        shp = {"gather": lambda a: (N_CHIPS,) + a.shape, "scatter": lambda a: a.shape, "forward": lambda a: a.shape,
               "swap": lambda a: (a.shape[0],) + a.shape[2:], "join": lambda a: (2,) + a.shape}[kind]
        self.out_shapes = [jax.ShapeDtypeStruct(shp(a), a.dtype) for a in self.ins]
        self.sems = [pltpu.SemaphoreType.DMA((n,)), pltpu.SemaphoreType.DMA((n,))]

    def _copies(self, ins, outs, sems):
        send, recv = sems
        x, y, c, me, others = _place()
        over_ici = self.kind in ("gather", "scatter")
        peers = [(ox, oy, c) for ox, oy in others] if over_ici else [(x, y, 1 - c)] * self.n_peers
        for a in range(len(self.ins)):
            for k, to in enumerate(peers):
                peer = 2 * others[k][0] + others[k][1]
                if self.kind == "gather":
                    hr = self.ins[a].shape[0] // 2
                    rows = pl.ds(c * hr, hr)
                    src, dst, land = ins[a].at[rows, :], outs[a].at[me, rows, :], outs[a].at[peer, rows, :]
                elif self.kind == "forward":
                    hr = self.ins[a].shape[1] // 2
                    mine, theirs = pl.ds(c * hr, hr), pl.ds((1 - c) * hr, hr)
                    src, dst, land = ins[a].at[peer, mine, :], outs[a].at[peer, mine, :], outs[a].at[peer, theirs, :]
                elif self.kind == "scatter":
                    src, dst, land = ins[a].at[peer], outs[a].at[me], outs[a].at[peer]
                elif self.kind == "swap":
                    src, dst, land = ins[a].at[:, 1 - c], outs[a], outs[a]
                else:
                    src, dst, land = ins[a], outs[a].at[c], outs[a].at[1 - c]
                i = self.n_peers * a + k
                mk = lambda s, d: pltpu.make_async_remote_copy(
                    src_ref=s, dst_ref=d, send_sem=send.at[i], recv_sem=recv.at[i], device_id=to,
                    device_id_type=MESH)
                yield mk(src, dst), mk(land, land)

    def alias_pairs(self):
        return [(i, i) for i in range(len(self.ins))] if self.aliased else []

    def start(self, ins, outs, sems):
        for cp, _ in self._copies(ins, outs, sems):
            cp.start()

    def finish(self, ins, outs, sems):
        pairs = list(self._copies(ins, outs, sems))
        for _, landing in pairs:
            landing.wait_recv()
        for cp, _ in pairs:
            cp.wait_send()


class _Several:
    def __init__(self, parts):
        self.parts = list(parts)
        self.ins = [a for p in self.parts for a in p.ins]
        self.out_shapes = [s for p in self.parts for s in p.out_shapes]
        self.sems = [s for p in self.parts for s in p.sems]

    def split(self, ins, outs, sems=None):
        i = 0
        for k, p in enumerate(self.parts):
            n = len(p.ins)
            yield p, ins[i:i + n], outs[i:i + n], None if sems is None else sems[2 * k:2 * k + 2]
            i += n

    def alias_pairs(self):
        pairs, i = [], 0
        for p in self.parts:
            pairs += [(i + a, i + b) for a, b in p.alias_pairs()]
            i += len(p.ins)
        return pairs

    def start(self, ins, outs, sems):
        for p, a, b, s in self.split(ins, outs, sems):
            p.start(a, b, s)

    def finish(self, ins, outs, sems):
        for p, a, b, s in self.split(ins, outs, sems):
            p.finish(a, b, s)


def _hosted_call(body, *, grid, in_specs, out_specs, out_shape, scratch_shapes, operands, name, sem, comm=None):
    out_specs, out_shape = tuple(out_specs), tuple(out_shape)
    if isinstance(comm, (list, tuple)):
        several = _Several(comm)
        outs, comm_outs = _hosted_call(body, grid=grid, in_specs=in_specs, out_specs=out_specs, out_shape=out_shape,
                                       scratch_shapes=scratch_shapes, operands=operands, name=name, sem=sem,
                                       comm=several)
        return outs, [tuple(o) for _, _, o, _ in several.split(several.ins, comm_outs)]
    if comm is None:
        res = pl.pallas_call(body, out_shape=out_shape, grid=grid, in_specs=list(in_specs), out_specs=out_specs,
                             scratch_shapes=list(scratch_shapes), name=name, compiler_params=_cp(*sem))(*operands)
        return tuple(res), ()
    n_in, n_out, n_sc = len(in_specs), len(out_specs), len(scratch_shapes)
    ci, co = len(comm.ins), len(comm.out_shapes)

    def wrapped(*refs):
        ins, c_ins = refs[:n_in], refs[n_in:n_in + ci]
        outs = refs[n_in + ci:n_in + ci + n_out]
        c_outs = refs[n_in + ci + n_out:n_in + ci + n_out + co]
        scratch = refs[n_in + ci + n_out + co:n_in + ci + n_out + co + n_sc]
        sems = refs[n_in + ci + n_out + co + n_sc:]
        ids = [pl.program_id(d) for d in range(len(grid))]
        first = functools.reduce(jnp.logical_and, [i == 0 for i in ids])
        last = functools.reduce(jnp.logical_and, [i == g - 1 for i, g in zip(ids, grid)])

        @pl.when(first)
        def _():
            comm.start(c_ins, c_outs, sems)

        body(*ins, *outs, *scratch)

        @pl.when(last)
        def _():
            comm.finish(c_ins, c_outs, sems)

    res = pl.pallas_call(
        wrapped, out_shape=out_shape + tuple(comm.out_shapes), grid=grid, in_specs=list(in_specs) + [ANY] * ci,
        out_specs=out_specs + tuple([ANY] * co), scratch_shapes=list(scratch_shapes) + comm.sems, name=name,
        input_output_aliases={n_in + i: n_out + o for i, o in comm.alias_pairs()},
        compiler_params=pltpu.CompilerParams(dimension_semantics=("arbitrary",) * len(grid),
                                             vmem_limit_bytes=VMEM_LIMIT, has_side_effects=True),
    )(*operands, *comm.ins)
    return tuple(res[:n_out]), tuple(res[n_out:])


def _run_exchange(comm, *, name):
    ci = len(comm.ins)

    def body(*refs):
        ins, outs, sems = refs[:ci], refs[ci:2 * ci], refs[2 * ci:]
        comm.start(ins, outs, sems)
        comm.finish(ins, outs, sems)

    return pl.pallas_call(
        body, out_shape=tuple(comm.out_shapes), in_specs=[ANY] * ci, out_specs=tuple([ANY] * ci),
        scratch_shapes=comm.sems, name=name, input_output_aliases=dict(comm.alias_pairs()),
        compiler_params=pltpu.CompilerParams(has_side_effects=True))(*comm.ins)


def _own_slot(buf, piece, idx):
    return lax.dynamic_update_slice(buf, piece[None], (idx,) + (0,) * piece.ndim)


def _all_reduce_small(v, *, name):
    R = v.shape[0]

    flips = [(dx, dy, dc) for dx in range(2) for dy in range(2) for dc in range(2) if dx or dy or dc]

    def body(v_ref, o_ref, slots, send, recv):
        x, y, c, me, others = _place()
        mine = 2 * me + c
        slots[mine] = v_ref[...]

        def copy(k, slot):
            dx, dy, dc = flips[k]
            peer = (x + dx - 2 * x * dx, y + dy - 2 * y * dy, c + dc - 2 * c * dc)
            peer_slot = 4 * peer[0] + 2 * peer[1] + peer[2]
            return pltpu.make_async_remote_copy(
                src_ref=v_ref, dst_ref=slots.at[mine if slot == "mine" else peer_slot], send_sem=send.at[k],
                recv_sem=recv.at[k], device_id=peer, device_id_type=MESH)

        for k in range(7):
            copy(k, "mine").start()
        for k in range(7):
            copy(k, "peer").wait_recv()
        for k in range(7):
            copy(k, "mine").wait_send()
        acc = slots[0]
        for j in range(1, 8):
            acc = acc + slots[j]
        o_ref[...] = acc

    vm = pl.BlockSpec(memory_space=pltpu.VMEM)
    return pl.pallas_call(
        body, out_shape=jax.ShapeDtypeStruct(v.shape, F32), in_specs=[vm], out_specs=vm,
        scratch_shapes=[pltpu.VMEM((8, R, LANES), F32), pltpu.SemaphoreType.DMA((7,)),
                        pltpu.SemaphoreType.DMA((7,))],
        name=name, compiler_params=pltpu.CompilerParams(has_side_effects=True))(v)


def _rows(v, n_rows):
    v = v.reshape(-1).astype(F32)
    return jnp.pad(v, (0, n_rows * LANES - v.shape[0])).reshape(n_rows, LANES)


def _mlp_fwd(x_in, g, w_up, w_down, tag):
    h = _rms_fwd(x_in, g, name=f"{tag}_norm")
    u, a = _matmul(h, w_up, b_split=True, epilogue="sqrelu", name=f"{tag}_up")
    x_out = _matmul(a, w_down, epilogue="res", res=x_in, name=f"{tag}_down")
    return x_out, (h, u, a)


def _mlp_bwd(dy, dy_b, x_in, g, w_up, w_down, saved, tag, comms=None):
    h, u, a = saved
    comms = comms or {}
    landed = {}

    def mm(key, *args, **kw):
        comm = comms.get(key)
        if callable(comm):
            comm = comm(landed)
        out = _matmul(*args, name=f"{tag}_{key}", comm=comm, **kw)
        if comm is not None:
            out, landed[key] = out
        return out

    dw_down = mm("dwdown", a, dy_b, form="tn")
    du = mm("du", dy_b, w_down, form="nt", epilogue="sqrelu_bwd", u=u, out_dtype=BF16)
    dw_up = mm("dwup", h, du, form="tn", out_split=True)
    dh = mm("dh", du, w_up, form="nt", b_split=True)
    dx, dx_b, dg = _rms_bwd(x_in, g, dh, res=dy, bf16_copy=True, name=f"{tag}_dnorm")
    return dx, dx_b, dg, dw_up, dw_down, landed


def kernel(x, positions, ln_mix_g, ln_mlp_g, sf_w_in, sf_b_f, fox_q_g, fox_k_g, sf_w_o, mla_w_down, mla_q_a_g, mla_kv_a_g, mla_w_uq, mla_w_ukv, mla_q_g, mla_k_g, mla_w_o, mlp_w_up, mlp_w_down, loss_target, m_ln_mix_g, m_ln_mlp_g, m_sf_w_in, m_sf_b_f, m_fox_q_g, m_fox_k_g, m_sf_w_o, m_mla_w_down, m_mla_q_a_g, m_mla_kv_a_g, m_mla_w_uq, m_mla_w_ukv, m_mla_q_g, m_mla_k_g, m_mla_w_o, m_mlp_w_up, m_mlp_w_down, v_ln_mix_g, v_ln_mlp_g, v_sf_w_in, v_sf_b_f, v_fox_q_g, v_fox_k_g, v_sf_w_o, v_mla_w_down, v_mla_q_a_g, v_mla_kv_a_g, v_mla_w_uq, v_mla_w_ukv, v_mla_q_g, v_mla_k_g, v_mla_w_o, v_mlp_w_up, v_mlp_w_down):
    S, D = x.shape[1], x.shape[2]
    xs, tgt, pos = x[0], loss_target[0], positions[0]
    xi, yi, ci = lax.axis_index("x"), lax.axis_index("y"), lax.axis_index("c")
    chip = 2 * xi + yi
    core = ci.astype(jnp.int32).reshape(1)
    d_ff = mlp_w_up.shape[2] * N_CHIPS
    in_w = sf_w_in.shape[2] * N_CHIPS
    qkv_w = 3 * N_SB * HEAD + 3 * N_FOX * HEAD
    dn_w = mla_w_down.shape[2]
    dn_pad = Q_RANK + KV_RANK + LANES

    def gather_begin(ws):
        shards = [w.astype(BF16) for w in ws]
        return shards, _Exchange("gather", shards)

    def hand_over(landed):
        return _Exchange("forward", list(landed))

    def gather_end(both, shards):
        return [_own_slot(ag, s, chip) for ag, s in zip(both, shards)]

    cols = lambda ag: ag.transpose(1, 0, 2).reshape(ag.shape[1], -1)
    rows = lambda ag: ag.reshape(-1, ag.shape[2])
    s_mix0, ex_mix0 = gather_begin([sf_w_in[0], sf_w_o[0]])
    landed = _run_exchange(ex_mix0, name="gather_mix0")
    ag_in, ag_o0 = gather_end(_run_exchange(hand_over(landed), name="gather_mix0_sibling"), s_mix0)
    w_in_full = cols(ag_in)
    w_qkv = w_in_full[:, :qkv_w]
    w_f = jnp.pad(w_in_full[:, qkv_w:], ((0, 0), (0, LANES - (in_w - qkv_w))))
    w_o0 = rows(ag_o0)
    s_mlp0, ex_mlp0 = gather_begin([mlp_w_up[0], mlp_w_down[0]])
    s_mix1, ex_mix1 = gather_begin([mla_w_down[0], mla_w_uq[0], mla_w_ukv[0], mla_w_o[0]])
    s_mlp1, ex_mlp1 = gather_begin([mlp_w_up[1], mlp_w_down[1]])

    gain_blk = jnp.concatenate([mla_q_a_g, mla_kv_a_g], axis=0) * (ci == 0).astype(F32)
    placed = jnp.zeros((2, N_CHIPS, LANES), F32)
    placed = lax.dynamic_update_slice(placed, gain_blk[:, None, :], (0, chip, 0))
    gains = _all_reduce_small(placed.reshape(2 * N_CHIPS, LANES), name="gather_gains")
    q_a_full = gains[:N_CHIPS].reshape(Q_RANK)
    kv_a_full = gains[N_CHIPS:].reshape(KV_RANK)

    pad_gain = lambda g: jnp.pad(g.reshape(1, QK_DIM), ((0, 0), (0, QK_PAD - QK_DIM)))
    q_g_pad, k_g_pad = pad_gain(mla_q_g), pad_gain(mla_k_g)
    b_pad = _rows(sf_b_f, 1)

    h0 = _rms_fwd(xs, ln_mix_g[0], name="mix0_norm")
    qkv_sb = _matmul(h0, w_qkv, n=3 * N_SB * HEAD, b_n0=0, out_dtype=BF16, name="mix0_qkv_sb")
    qk_fx = _matmul(h0, w_qkv, n=2 * N_FOX * HEAD, b_n0=3 * N_SB * HEAD, name="mix0_qk_fox")
    v_fx = _matmul(h0, w_qkv, n=N_FOX * HEAD, b_n0=(3 * N_SB + 2 * N_FOX) * HEAD, out_dtype=BF16,
                   name="mix0_v_fox")
    fl = _matmul(h0, w_f, name="mix0_forget_logit")
    f_cum = _forget_fwd(fl, b_pad, name="forget_fwd")
    neg_f = (-f_cum[:, :N_FOX]).T.reshape(N_FOX, 1, S)
    q_f = _rms_fwd(qk_fx, fox_q_g[0], c0=0, width=N_FOX * HEAD, gw=HEAD, name="fox_q_norm")
    k_f = _rms_fwd(qk_fx, fox_k_g[0], c0=N_FOX * HEAD, width=N_FOX * HEAD, gw=HEAD, name="fox_k_norm")
    (o_sb, t_sb), landed = _sb_fwd(qkv_sb, name="sb_fwd", comm=ex_mlp0, tq=1024)
    (o_fx, lse0), (landed, both) = _attn_fwd(q_f, k_f, v_fx, neg_f, n_heads=N_FOX, dqk=HEAD, scale=HEAD ** -0.5,
                                             exact_p=True, name="fox_fwd", comm=[ex_mix1, hand_over(landed)])
    ag_up0, ag_dw0 = gather_end(both, s_mlp0)
    o0 = jnp.concatenate([o_sb, o_fx], axis=1).astype(BF16)
    x1, both = _matmul(o0, w_o0, epilogue="res", res=xs, name="mix0_out", comm=hand_over(landed))
    ag_dn, ag_uq, ag_ukv, ag_o1 = gather_end(both, s_mix1)
    w_dn = jnp.pad(rows(ag_dn), ((0, 0), (0, dn_pad - dn_w)))
    w_uq = jnp.pad(cols(ag_uq).reshape(Q_RANK, N_MLA, QK_DIM), ((0, 0), (0, 0), (0, QK_PAD - QK_DIM)))
    w_uq = w_uq.reshape(Q_RANK, N_MLA * QK_PAD)
    w_ukv = cols(ag_ukv)
    w_o1 = rows(ag_o1)
    x2, mlp0 = _mlp_fwd(x1, ln_mlp_g[0], ag_up0, rows(ag_dw0), "mlp0")

    h2 = _rms_fwd(x2, ln_mix_g[1], name="mix1_norm")
    down = _matmul(h2, w_dn, name="mix1_down")
    c_q = _rms_fwd(down, q_a_full, c0=0, width=Q_RANK, name="mix1_q_a_norm")
    c_kv = _rms_fwd(down, kv_a_full, c0=Q_RANK, width=KV_RANK, name="mix1_kv_a_norm")
    q_raw = _matmul(c_q, w_uq, name="mix1_uq")
    kv_raw = _matmul(c_kv, w_ukv, name="mix1_ukv")
    half = ROPE // 2
    inv_freq = ROPE_THETA ** (-jnp.arange(half, dtype=F32) / half)
    ang = pos.astype(F32)[:, None] * inv_freq
    table = lambda t: jnp.pad(jnp.concatenate([t, t], axis=1), ((0, 0), (0, LANES - ROPE)))
    cos_t, sin_t = table(jnp.cos(ang)), table(jnp.sin(ang))
    q_pad, k_pad, v1 = _mla_prep_fwd(q_raw, kv_raw, down, cos_t, sin_t, q_g_pad, k_g_pad, name="mla_prep_fwd")
    (o1, lse1), landed = _attn_fwd(q_pad, k_pad, v1, None, n_heads=N_MLA, dqk=QK_PAD, scale=QK_DIM ** -0.5,
                                   name="mla_fwd", comm=ex_mlp1)
    o1_b = o1.astype(BF16)
    x3, both = _matmul(o1_b, w_o1, epilogue="res", res=x2, name="mix1_out", comm=hand_over(landed))
    ag_up1, ag_dw1 = gather_end(both, s_mlp1)
    w_up = [ag_up0, ag_up1]
    w_dw = [rows(ag_dw0), rows(ag_dw1)]
    x4, mlp1 = _mlp_fwd(x3, ln_mlp_g[1], w_up[1], w_dw[1], "mlp1")

    dx4, dx4_b, loss_local = _loss_head(x4, tgt, name="loss_head")
    loss = lax.psum(loss_local, ("x", "y", "c"))

    by_cols = lambda g: g.reshape(g.shape[0], N_CHIPS, -1).transpose(1, 0, 2)
    by_rows = lambda g: g.reshape(N_CHIPS, g.shape[0] // N_CHIPS, g.shape[1])
    halves = lambda g: g.reshape(N_CHIPS, 2, g.shape[1] // 2, g.shape[2])

    def scatter_of(grads, from_sibling, tags):
        parts = [_add_sibling(g, r, core, name=f"add_sibling_{t}") for g, r, t in zip(grads, from_sibling, tags)]
        return parts, _Exchange("scatter", parts)

    def sums_of(slots, parts, tags):
        slots = [_own_slot(s, lax.dynamic_index_in_dim(p, chip, 0, keepdims=False), chip)
                 for s, p in zip(slots, parts)]
        return [_add_chips(s, name=f"add_chips_{t}") for s, t in zip(slots, tags)]

    def shards_of(joined, mine):
        return [_own_slot(j, m, ci).reshape(2 * m.shape[0], m.shape[1]) for j, m in zip(joined, mine)]

    dx3, dx3_b, dg_mlp1, dw_up1, dw_dw1, _ = _mlp_bwd(dx4, dx4_b, x3, ln_mlp_g[1], w_up[1], w_dw[1], mlp1, "mlp1")
    tags_mlp1 = ["w_up1", "w_dw1"]
    g_mlp1 = [halves(dw_up1), halves(by_rows(dw_dw1))]

    dw_o1 = _matmul(o1_b, dx3_b, form="tn", name="mix1_dwo")
    do1 = _matmul(dx3_b, w_o1, form="nt", name="mix1_do")
    (dq_pad, dk_pad, dv1), from_sibling = _attn_bwd(
        q_pad, k_pad, v1, None, o1, lse1, do1, n_heads=N_MLA, dqk=QK_PAD, scale=QK_DIM ** -0.5, name="mla_bwd",
        comm=_Exchange("swap", g_mlp1))
    p_mlp1, _ = scatter_of(g_mlp1, from_sibling, tags_mlp1)
    (dq_raw, dkv_raw, dpe, dg_q, dg_k), sl_up1 = _mla_prep_bwd(
        dq_pad, dk_pad, dv1, q_raw, kv_raw, down, cos_t, sin_t, q_g_pad, k_g_pad, name="mla_prep_bwd",
        comm=_Exchange("scatter", p_mlp1[:1]))
    dw_uq = _matmul(c_q, dq_raw, form="tn", name="mix1_dwuq")
    dc_q = _matmul(dq_raw, w_uq, form="nt", name="mix1_dcq")
    dw_ukv = _matmul(c_kv, dkv_raw, form="tn", name="mix1_dwukv")
    dc_kv = _matmul(dkv_raw, w_ukv, form="nt", name="mix1_dckv")
    d_cq, dg_qa = _rms_bwd(down, q_a_full, dc_q, c0=0, width=Q_RANK, name="mix1_q_a_dnorm")
    d_ckv, dg_kva = _rms_bwd(down, kv_a_full, dc_kv, c0=Q_RANK, width=KV_RANK, name="mix1_kv_a_dnorm")
    d_down = jnp.concatenate([d_cq, d_ckv, dpe], axis=1)
    dw_dn = _matmul(h2, d_down, form="tn", name="mix1_dwdown")
    dh2 = _matmul(d_down, w_dn, form="nt", name="mix1_dh")
    dx2, dx2_b, dg_mix1 = _rms_bwd(x2, ln_mix_g[1], dh2, res=dx3, bf16_copy=True, name="mix1_dnorm")
    g_uq = dw_uq.reshape(Q_RANK, N_MLA, QK_PAD)[:, :, :QK_DIM].reshape(Q_RANK, N_MLA * QK_DIM)
    tags_mix1 = ["w_dn", "w_uq", "w_ukv", "w_o1"]
    g_mix1 = [halves(by_rows(dw_dn[:, :dn_w])), halves(by_cols(g_uq)), halves(by_cols(dw_ukv)),
              halves(by_rows(dw_o1))]

    p_mix1, mine_mlp1 = [], []

    def scatter_mix1(landed):
        parts, ex = scatter_of(g_mix1, landed["dwdown"][1], tags_mix1)
        p_mix1.extend(parts)
        return ex

    def join_mlp1(landed):
        mine_mlp1.extend(sums_of(sl_up1 + landed["dwdown"][0], p_mlp1, tags_mlp1))
        return _Exchange("join", mine_mlp1)

    dx1, dx1_b, dg_mlp0, dw_up0, dw_dw0, landed = _mlp_bwd(
        dx2, dx2_b, x1, ln_mlp_g[0], w_up[0], w_dw[0], mlp0, "mlp0",
        comms={"dwdown": [_Exchange("scatter", p_mlp1[1:]), _Exchange("swap", g_mix1)],
               "du": scatter_mix1, "dwup": join_mlp1})
    gs_up1, gs_dw1 = shards_of(landed["dwup"], mine_mlp1)
    mine_mix1 = sums_of(landed["du"], p_mix1, tags_mix1)
    tags_mlp0 = ["w_up0", "w_dw0"]
    g_mlp0 = [halves(dw_up0), halves(by_rows(dw_dw0))]

    dw_o0, joined = _matmul(o0, dx1_b, form="tn", name="mix0_dwo", comm=_Exchange("join", mine_mix1))
    gs_dn, gs_uq, gs_ukv, gs_o1 = shards_of(joined, mine_mix1)
    g_o0 = [halves(by_rows(dw_o0))]
    do0 = _matmul(dx1_b, w_o0, form="nt", name="mix0_do")
    (dq_f, dk_f, dv_fx, dbias, drow), (fs_mlp0, fs_o0) = _attn_bwd(
        q_f, k_f, v_fx, neg_f, o_fx, lse0, do0, n_heads=N_FOX, dqk=HEAD, scale=HEAD ** -0.5, do_off=N_SB,
        name="fox_bwd", comm=[_Exchange("swap", g_mlp0), _Exchange("swap", g_o0)])
    p_mlp0, ex_a = scatter_of(g_mlp0, fs_mlp0, tags_mlp0)
    p_o0, ex_b = scatter_of(g_o0, fs_o0, ["w_o0"])
    (dq_sb, dk_sb, dv_sb), (sl_mlp0, sl_o0) = _sb_bwd(qkv_sb, do0, t_sb, do_off=0, name="sb_bwd", comm=[ex_a, ex_b])
    mine_mlp0 = sums_of(sl_mlp0, p_mlp0, tags_mlp0)
    mine_o0 = sums_of(sl_o0, p_o0, ["w_o0"])
    dq_fx, dg_fq = _rms_bwd(qk_fx, fox_q_g[0], dq_f, c0=0, width=N_FOX * HEAD, gw=HEAD, name="fox_q_dnorm")
    dk_fx, dg_fk = _rms_bwd(qk_fx, fox_k_g[0], dk_f, c0=N_FOX * HEAD, width=N_FOX * HEAD, gw=HEAD,
                            name="fox_k_dnorm")
    d_fcum = jnp.pad((jnp.max(drow, axis=-1) - dbias.reshape(N_FOX, S)).T, ((0, 0), (0, LANES - N_FOX)))
    dfl, db_f = _forget_bwd(fl, b_pad, d_fcum, name="forget_bwd")
    dproj = jnp.concatenate([dq_sb, dk_sb, dv_sb, dq_fx, dk_fx, dv_fx], axis=1).astype(BF16)
    dw_qkv, (j_mlp0, j_o0) = _matmul(h0, dproj, form="tn", name="mix0_dwqkv",
                                     comm=[_Exchange("join", mine_mlp0), _Exchange("join", mine_o0)])
    gs_up0, gs_dw0 = shards_of(j_mlp0, mine_mlp0)
    gs_o0, = shards_of(j_o0, mine_o0)
    dw_f = _matmul(h0, dfl, form="tn", name="mix0_dwf")
    g_in = [halves(by_cols(jnp.concatenate([dw_qkv, dw_f[:, :in_w - qkv_w]], axis=1)))]
    p_in, ex = scatter_of(g_in, _run_exchange(_Exchange("swap", g_in), name="reduce_w_in_swap"), ["w_in"])
    dh0 = _matmul(dfl, w_f, form="nt", name="mix0_dh_f")
    dh0, slots = _matmul(dproj, w_qkv, form="nt", epilogue="res", res=dh0, name="mix0_dh", comm=ex)
    mine_in = sums_of(slots, p_in, ["w_in"])
    gs_in, = shards_of(_run_exchange(_Exchange("join", mine_in), name="reduce_w_in_join"), mine_in)
    grad_x, dg_mix0 = _rms_bwd(xs, ln_mix_g[0], dh0, res=dx1, name="mix0_dnorm")
    gs_up = jnp.concatenate([gs_up0, gs_up1], axis=0)
    gs_dw = jnp.concatenate([gs_dw0, gs_dw1], axis=0)

    ln_rows = D // LANES
    small = jnp.concatenate([
        _rows(dg_mix0, ln_rows), _rows(dg_mix1, ln_rows), _rows(dg_mlp0, ln_rows), _rows(dg_mlp1, ln_rows),
        _rows(db_f, 8), _rows(dg_fq, 8), _rows(dg_fk, 8), _rows(dg_qa, 8), _rows(dg_kva, 8), _rows(dg_q, 8),
        _rows(dg_k, 8)], axis=0)
    small = _all_reduce_small(small, name="reduce_small")
    flat = lambda r0, nr, n: small[r0:r0 + nr].reshape(-1)[:n]
    r0 = 4 * ln_rows
    g_ln_mix = jnp.stack([flat(0, ln_rows, D), flat(ln_rows, ln_rows, D)])
    g_ln_mlp = jnp.stack([flat(2 * ln_rows, ln_rows, D), flat(3 * ln_rows, ln_rows, D)])
    g_b_f = flat(r0, 8, N_FOX)[None]
    g_fq, g_fk = flat(r0 + 8, 8, HEAD)[None], flat(r0 + 16, 8, HEAD)[None]
    g_qa = lax.dynamic_slice(flat(r0 + 24, 8, Q_RANK), (chip * LANES,), (LANES,))[None]
    g_kva = lax.dynamic_slice(flat(r0 + 32, 8, KV_RANK), (chip * LANES,), (LANES,))[None]
    g_q, g_k = flat(r0 + 40, 8, QK_DIM)[None], flat(r0 + 48, 8, QK_DIM)[None]

    def pack_small(ln_mix, ln_mlp, *rest):
        return jnp.concatenate([_rows(ln_mix, 2 * ln_rows), _rows(ln_mlp, 2 * ln_rows)] + [_rows(t, 8) for t in rest],
                               axis=0)

    def unpack_small(p):
        f = lambda r, nr, shape: p[r:r + nr].reshape(-1)[:int(np.prod(shape))].reshape(shape)
        shapes = [(1, N_FOX), (1, HEAD), (1, HEAD), (1, LANES), (1, LANES), (1, QK_DIM), (1, QK_DIM)]
        return (f(0, 2 * ln_rows, (2, D)), f(2 * ln_rows, 2 * ln_rows, (2, D)),
                *[f(r0 + 8 * i, 8, shp) for i, shp in enumerate(shapes)])

    small_out = _adamw(
        pack_small(ln_mix_g, ln_mlp_g, sf_b_f, fox_q_g, fox_k_g, mla_q_a_g, mla_kv_a_g, mla_q_g, mla_k_g)[None],
        pack_small(g_ln_mix, g_ln_mlp, g_b_f, g_fq, g_fk, g_qa, g_kva, g_q, g_k)[None],
        pack_small(m_ln_mix_g, m_ln_mlp_g, m_sf_b_f, m_fox_q_g, m_fox_k_g, m_mla_q_a_g, m_mla_kv_a_g, m_mla_q_g,
                   m_mla_k_g)[None],
        pack_small(v_ln_mix_g, v_ln_mlp_g, v_sf_b_f, v_fox_q_g, v_fox_k_g, v_mla_q_a_g, v_mla_kv_a_g, v_mla_q_g,
                   v_mla_k_g)[None], name="adamw_small")
    d_small, m_small, v_small = [unpack_small(p[0]) for p in small_out]

    def big(w, g, m, v, tag, comm=None):
        g = g.reshape(w.shape)
        out = _adamw(w, g, m, v, name=f"adamw_{tag}", comm=comm)
        (d, mn, vn), landed = out if comm is not None else (out, None)
        return (g, d, mn, vn) if comm is None else ((g, d, mn, vn), landed)

    r_up = big(mlp_w_up, gs_up, m_mlp_w_up, v_mlp_w_up, "w_up")
    r_dw = big(mlp_w_down, gs_dw, m_mlp_w_down, v_mlp_w_down, "w_dw")
    r_in = big(sf_w_in, gs_in, m_sf_w_in, v_sf_w_in, "w_in")
    r_o0 = big(sf_w_o, gs_o0, m_sf_w_o, v_sf_w_o, "w_o0")
    r_dn = big(mla_w_down, gs_dn, m_mla_w_down, v_mla_w_down, "w_dn")
    r_uq = big(mla_w_uq, gs_uq, m_mla_w_uq, v_mla_w_uq, "w_uq")
    r_ukv = big(mla_w_ukv, gs_ukv, m_mla_w_ukv, v_mla_w_ukv, "w_ukv")
    r_o1 = big(mla_w_o, gs_o1, m_mla_w_o, v_mla_w_o, "w_o1")

    g_small = (g_ln_mix, g_ln_mlp, g_b_f, g_fq, g_fk, g_qa, g_kva, g_q, g_k)

    def ordered(k, sm):
        return (sm[0], sm[1], r_in[k], sm[2], sm[3], sm[4], r_o0[k], r_dn[k], sm[5], sm[6], r_uq[k], r_ukv[k],
                sm[7], sm[8], r_o1[k], r_up[k], r_dw[k])

    return (loss, grad_x[None], *ordered(0, g_small), *ordered(1, d_small), *ordered(2, m_small),
            *ordered(3, v_small))
```

```python
import functools

import numpy as np
import jax
import jax.numpy as jnp
from jax import lax
from jax.experimental import pallas as pl
from jax.experimental.pallas import tpu as pltpu

F32 = jnp.float32
BF16 = jnp.bfloat16
MESH = pl.DeviceIdType.MESH

EPS = 1e-6
HEAD = 128
N_SB = 8
N_FOX = 8
N_MLA = 16
Q_RANK = 512
KV_RANK = 512
NOPE = 128
ROPE = 64
QK_DIM = NOPE + ROPE
QK_PAD = 256
ROPE_THETA = 10000.0
N_CHIPS = 4

ADAM_LR = 0.001
ADAM_B1 = 0.9
ADAM_B2 = 0.999
ADAM_EPS = 1e-08
ADAM_WD = 0.01
ADAM_STEP = 10

VMEM_LIMIT = 56 * 1024 * 1024
LANES = 128
NEG = -1e30


def _cp(*sem):
    return pltpu.CompilerParams(dimension_semantics=sem, vmem_limit_bytes=VMEM_LIMIT)


def _pick(dim, target):
    if dim <= target:
        return dim
    t = (target // LANES) * LANES
    while t >= LANES:
        if dim % t == 0:
            return t
        t -= LANES
    raise ValueError(f"no tile for {dim}")


NT_DIMS = (((1,), (1,)), ((), ()))
TN_DIMS = (((0,), (0,)), ((), ()))


def _dot(a, b):
    return jnp.dot(a, b, preferred_element_type=F32)


def _dot_nt(a, b):
    return lax.dot_general(a, b, NT_DIMS, preferred_element_type=F32)


def _dot_tn(a, b):
    return lax.dot_general(a, b, TN_DIMS, preferred_element_type=F32)


def _matmul(a, b, *, name, form="nn", out_dtype=F32, n=None, b_n0=0, b_split=False,
            out_split=False, epilogue="plain", res=None, u=None, tm=1024, tn=1024, tk=2048, comm=None):
    if form == "tn":
        K, M = a.shape
    else:
        M, K = a.shape
    if b_split:
        if form == "nt":
            nb_full, kb_full = b.shape[1], b.shape[2] * N_CHIPS
        else:
            kb_full, nb_full = b.shape[1], b.shape[2] * N_CHIPS
    elif form == "nt":
        nb_full, kb_full = b.shape
    else:
        kb_full, nb_full = b.shape
    assert kb_full == K, (name, a.shape, b.shape)
    N = nb_full if n is None else n
    if a.dtype != BF16 or b.dtype != BF16:
        tk = max(tk // 2, LANES)
    tm, tn, tk = _pick(M, tm), _pick(N, tn), _pick(K, tk)
    if b_split:
        per_chip = (b.shape[2])
        if form == "nt":
            tk = _pick(per_chip, tk)
        else:
            tn = _pick(per_chip, tn)
    if out_split:
        tn = _pick(N // N_CHIPS, tn)
    assert b_n0 % tn == 0
    nb0 = b_n0 // tn
    nk = K // tk
    grid = (M // tm, N // tn, nk)

    if form == "tn":
        a_spec = pl.BlockSpec((tk, tm), lambda i, j, k: (k, i))
    else:
        a_spec = pl.BlockSpec((tm, tk), lambda i, j, k: (i, k))
    if b_split:
        if form == "nt":
            kc = b.shape[2] // tk
            b_spec = pl.BlockSpec((None, tn, tk), lambda i, j, k: (k // kc, j, k % kc))
        else:
            nc = b.shape[2] // tn
            b_spec = pl.BlockSpec((None, tk, tn), lambda i, j, k: (j // nc, k, j % nc))
    elif form == "nt":
        b_spec = pl.BlockSpec((tn, tk), lambda i, j, k: (j + nb0, k))
    else:
        b_spec = pl.BlockSpec((tk, tn), lambda i, j, k: (k, j + nb0))
    mn_spec = pl.BlockSpec((tm, tn), lambda i, j, k: (i, j))
    if out_split:
        oc = (N // N_CHIPS) // tn
        out_spec = pl.BlockSpec((None, tm, tn), lambda i, j, k: (j // oc, i, j % oc))
        out_shape = jax.ShapeDtypeStruct((N_CHIPS, M, N // N_CHIPS), out_dtype)
    else:
        out_spec = mn_spec
        out_shape = jax.ShapeDtypeStruct((M, N), out_dtype)

    in_specs = [a_spec, b_spec]
    operands = [a, b]
    out_specs = (out_spec,)
    out_shape = (out_shape,)
    if epilogue == "res":
        in_specs.append(mn_spec)
        operands.append(res)
    elif epilogue == "sqrelu_bwd":
        in_specs.append(mn_spec)
        operands.append(u)
    elif epilogue == "sqrelu":
        out_specs = (mn_spec, mn_spec)
        out_shape = (jax.ShapeDtypeStruct((M, N), F32), jax.ShapeDtypeStruct((M, N), BF16))

    def finish(refs, r):
        if epilogue == "plain":
            refs[2][...] = r.astype(out_dtype)
        elif epilogue == "res":
            refs[3][...] = (refs[2][...] + r).astype(out_dtype)
        elif epilogue == "sqrelu":
            refs[2][...] = r
            p = jnp.maximum(r, 0.0)
            refs[3][...] = (p * p).astype(BF16)
        else:
            refs[3][...] = (r * (2.0 * jnp.maximum(refs[2][...], 0.0))).astype(out_dtype)

    def body(*refs):
        at = refs[0][...].astype(BF16)
        bt = refs[1][...].astype(BF16)
        if form == "nn":
            part = _dot(at, bt)
        elif form == "nt":
            part = _dot_nt(at, bt)
        else:
            part = _dot_tn(at, bt)
        if nk == 1:
            finish(refs, part)
            return
        acc = refs[-1]
        k = pl.program_id(2)

        @pl.when(k == 0)
        def _():
            acc[...] = part

        @pl.when(jnp.logical_and(k > 0, k < nk - 1))
        def _():
            acc[...] += part

        @pl.when(k == nk - 1)
        def _():
            finish(refs, acc[...] + part)

    outs, comm_outs = _hosted_call(
        body, grid=grid, in_specs=in_specs, out_specs=out_specs, out_shape=out_shape,
        scratch_shapes=[] if nk == 1 else [pltpu.VMEM((tm, tn), F32)], operands=operands, name=name,
        sem=("parallel", "parallel", "arbitrary"), comm=comm)
    result = outs if epilogue == "sqrelu" else outs[0]
    return result if comm is None else (result, comm_outs)


def _rms_fwd(x, g, *, name, c0=0, width=None, gw=None, tr=256):
    R, ctot = x.shape
    C = ctot if width is None else width
    gw = C if gw is None else gw
    assert c0 % C == 0 and C % gw == 0
    tr = _pick(R, tr)
    cb = c0 // C
    ng = C // gw

    def body(x_ref, g_ref, o_ref):
        gv = g_ref[...]
        for gi in range(ng):
            cols = slice(gi * gw, (gi + 1) * gw)
            xs = x_ref[:, cols]
            ms = jnp.sum(xs * xs, axis=-1, keepdims=True) * (1.0 / gw)
            o_ref[:, cols] = ((xs * lax.rsqrt(ms + EPS)) * gv).astype(o_ref.dtype)

    return pl.pallas_call(
        body, out_shape=jax.ShapeDtypeStruct((R, C), BF16), grid=(R // tr,),
        in_specs=[pl.BlockSpec((tr, C), lambda i: (i, cb)), pl.BlockSpec((1, gw), lambda i: (0, 0))],
        out_specs=pl.BlockSpec((tr, C), lambda i: (i, 0)), name=name,
        compiler_params=_cp("parallel"))(x, g.reshape(1, gw).astype(F32))


def _rms_bwd(x, g, dy, *, name, res=None, c0=0, width=None, gw=None, tr=256, bf16_copy=False):
    bf16_copy = int(bf16_copy)
    R, ctot = x.shape
    C = ctot if width is None else width
    gw = C if gw is None else gw
    tr = _pick(R, tr)
    cb = c0 // C
    ng = C // gw
    nsteps = R // tr
    row_spec = pl.BlockSpec((tr, C), lambda i: (i, 0))
    in_specs = [pl.BlockSpec((tr, C), lambda i: (i, cb)), pl.BlockSpec((1, gw), lambda i: (0, 0)), row_spec]
    operands = [x, g.reshape(1, gw).astype(F32), dy]
    if res is not None:
        in_specs.append(row_spec)
        operands.append(res)

    def body(*refs):
        x_ref, g_ref, dy_ref = refs[:3]
        res_ref = refs[3] if res is not None else None
        dx_ref, dg_ref = refs[-3 - bf16_copy], refs[-2]
        acc = refs[-1]
        i = pl.program_id(0)

        @pl.when(i == 0)
        def _():
            acc[...] = jnp.zeros_like(acc)

        gv = g_ref[...]
        for gi in range(ng):
            cols = slice(gi * gw, (gi + 1) * gw)
            xs = x_ref[:, cols]
            dys = dy_ref[:, cols].astype(F32)
            rstd = lax.rsqrt(jnp.sum(xs * xs, axis=-1, keepdims=True) * (1.0 / gw) + EPS)
            xh = xs * rstd
            gdy = dys * gv
            m = jnp.sum(gdy * xh, axis=-1, keepdims=True) * (1.0 / gw)
            dx = rstd * (gdy - xh * m)
            if res_ref is not None:
                dx = dx + res_ref[:, cols]
            dx_ref[:, cols] = dx
            if bf16_copy:
                refs[-3][:, cols] = dx.astype(BF16)
            acc[...] += jnp.sum((dys * xh).reshape(tr // 8, 8, gw), axis=0)

        @pl.when(i == nsteps - 1)
        def _():
            dg_ref[...] = jnp.sum(acc[...], axis=0, keepdims=True)

    out_shape = [jax.ShapeDtypeStruct((R, C), F32)] + [jax.ShapeDtypeStruct((R, C), BF16)] * bf16_copy
    outs = pl.pallas_call(
        body, out_shape=tuple(out_shape + [jax.ShapeDtypeStruct((1, gw), F32)]),
        grid=(nsteps,), in_specs=in_specs,
        out_specs=tuple([row_spec] * len(out_shape) + [pl.BlockSpec((1, gw), lambda i: (0, 0))]),
        scratch_shapes=[pltpu.VMEM((8, gw), F32)], name=name,
        compiler_params=_cp("arbitrary"))(*operands)
    return (*outs[:-1], outs[-1][0])


def _split3(x):
    hi = x.astype(BF16)
    r1 = x - hi.astype(F32)
    mid = r1.astype(BF16)
    lo = (r1 - mid.astype(F32)).astype(BF16)
    return hi, mid, lo


def _log_sigmoid(z):
    return jnp.minimum(z, 0.0) - jnp.log(1.0 + jnp.exp(-jnp.abs(z)))


def _forget_fwd(fl, b, *, name, tb=512):
    S = fl.shape[0]
    tb = _pick(S, tb)

    def body(fl_ref, b_ref, f_ref, carry):
        i = pl.program_id(0)

        @pl.when(i == 0)
        def _():
            carry[...] = jnp.zeros_like(carry)

        lf = _log_sigmoid(fl_ref[...] + b_ref[...])
        r = lax.broadcasted_iota(jnp.int32, (tb, tb), 0)
        c = lax.broadcasted_iota(jnp.int32, (tb, tb), 1)
        tri = (c <= r).astype(BF16)
        hi, mid, lo = _split3(lf)
        cs = _dot(tri, hi) + _dot(tri, mid) + _dot(tri, lo)
        f_ref[...] = cs + carry[...]
        carry[...] += jnp.sum(lf, axis=0, keepdims=True)

    return pl.pallas_call(
        body, out_shape=jax.ShapeDtypeStruct((S, LANES), F32), grid=(S // tb,),
        in_specs=[pl.BlockSpec((tb, LANES), lambda i: (i, 0)), pl.BlockSpec((1, LANES), lambda i: (0, 0))],
        out_specs=pl.BlockSpec((tb, LANES), lambda i: (i, 0)),
        scratch_shapes=[pltpu.VMEM((1, LANES), F32)], name=name,
        compiler_params=_cp("arbitrary"))(fl, b)


def _forget_bwd(fl, b, dF, *, name, tb=512):
    S = fl.shape[0]
    tb = _pick(S, tb)
    nb = S // tb

    def body(fl_ref, b_ref, df_ref, dfl_ref, db_ref, carry, acc):
        i = pl.program_id(0)

        @pl.when(i == 0)
        def _():
            carry[...] = jnp.zeros_like(carry)
            acc[...] = jnp.zeros_like(acc)

        d = df_ref[...]
        r = lax.broadcasted_iota(jnp.int32, (tb, tb), 0)
        c = lax.broadcasted_iota(jnp.int32, (tb, tb), 1)
        tri = (c >= r).astype(BF16)
        hi, mid, lo = _split3(d)
        rc = _dot(tri, hi) + _dot(tri, mid) + _dot(tri, lo) + carry[...]
        z = fl_ref[...] + b_ref[...]
        dfl = rc * jnp.exp(_log_sigmoid(-z))
        dfl_ref[...] = dfl
        carry[...] += jnp.sum(d, axis=0, keepdims=True)
        acc[...] += jnp.sum(dfl, axis=0, keepdims=True)

        @pl.when(i == nb - 1)
        def _():
            db_ref[...] = acc[...]

    rev = lambda i: (nb - 1 - i, 0)
    dfl, db = pl.pallas_call(
        body, out_shape=(jax.ShapeDtypeStruct((S, LANES), F32), jax.ShapeDtypeStruct((1, LANES), F32)),
        grid=(nb,),
        in_specs=[pl.BlockSpec((tb, LANES), rev), pl.BlockSpec((1, LANES), lambda i: (0, 0)),
                  pl.BlockSpec((tb, LANES), rev)],
        out_specs=(pl.BlockSpec((tb, LANES), rev), pl.BlockSpec((1, LANES), lambda i: (0, 0))),
        scratch_shapes=[pltpu.VMEM((1, LANES), F32), pltpu.VMEM((1, LANES), F32)], name=name,
        compiler_params=_cp("arbitrary"))(fl, b, dF)
    return dfl, db[0]


def _tri(tk, rel):
    r = lax.broadcasted_iota(jnp.int32, (tk, tk), 0)
    c = lax.broadcasted_iota(jnp.int32, (tk, tk), 1)
    m = {"gt": r > c, "le": r <= c, "lt": r < c}[rel]
    return m.astype(BF16)


def _split2(x):
    hi = x.astype(BF16)
    return hi, (x - hi.astype(F32)).astype(BF16)


HEADS_PER_STEP = 2
CUM_CHUNK = 256


def _cum_cols(x, tri, suffix):
    ck = tri.shape[0]
    n = x.shape[1] // ck
    hi, lo = _split2(x)
    parts, sums = [], []
    for c in range(n):
        cs = slice(c * ck, (c + 1) * ck)
        parts.append(_dot(hi[:, cs], tri) + _dot(lo[:, cs], tri))
        sums.append(jnp.sum(x[:, cs], axis=1, keepdims=True))
    carry = None
    for c in (reversed(range(n)) if suffix else range(n)):
        if carry is not None:
            parts[c] = parts[c] + carry
        carry = sums[c] if carry is None else carry + sums[c]
    return (parts[0] if n == 1 else jnp.concatenate(parts, axis=1)), carry


def _diag_mask(tq, strict):
    r = lax.broadcasted_iota(jnp.int32, (tq, tq), 0)
    c = lax.broadcasted_iota(jnp.int32, (tq, tq), 1)
    return c < r if strict else c <= r


def _sb_fwd(qkv, *, name, n_heads=N_SB, q_off=0, k_off=N_SB, v_off=2 * N_SB, tq=512, hp=HEADS_PER_STEP,
            comm=None):
    S = qkv.shape[0]
    tq = _pick(S, tq)
    tk = tq
    scale = HEAD ** -0.5
    nq = S // tq
    assert n_heads % hp == 0 and q_off % hp == 0 and k_off % hp == 0 and v_off % hp == 0

    def body(q_ref, k_ref, v_ref, o_ref, t_ref, c_sc, acc_sc):
        qi = pl.program_id(1)
        c_sc[...] = jnp.zeros_like(c_sc)
        acc_sc[...] = jnp.zeros_like(acc_sc)
        gt = _tri(min(CUM_CHUNK, tk), "gt")

        def tile(hh, j, diag):
            cs = slice(hh * HEAD, (hh + 1) * HEAD)
            rows = pl.ds(pl.multiple_of(j * tk, tk), tk)
            z = _dot_nt(q_ref[:, cs], k_ref[rows, cs]) * scale
            sp = jnp.log(1.0 + jnp.exp(-jnp.abs(z)))
            la = jnp.minimum(z, 0.0) - sp
            lb = -jnp.maximum(z, 0.0) - sp
            if diag:
                strict = _diag_mask(tq, True)
                lb = jnp.where(strict, lb, 0.0)
            suffix, total = _cum_cols(lb, gt, suffix=True)
            w = jnp.exp(la + suffix + c_sc[hh])
            if diag:
                w = jnp.where(strict, w, 0.0)
            acc_sc[hh] += _dot(w.astype(BF16), v_ref[rows, cs])
            c_sc[hh] += total

        for hh in range(hp):
            tile(hh, qi, True)

        def step(it, carry):
            for hh in range(hp):
                tile(hh, qi - 1 - it, False)
            return carry

        lax.fori_loop(0, qi, step, 0)
        for hh in range(hp):
            o_ref[:, hh * HEAD:(hh + 1) * HEAD] = acc_sc[hh]
            t_ref[hh] = jnp.broadcast_to(c_sc[hh], (tq, LANES))

    w = hp * HEAD
    head_blk = lambda off: pl.BlockSpec((S, w), lambda h, i: (0, h + off // hp))
    outs, comm_outs = _hosted_call(
        body,
        out_shape=(jax.ShapeDtypeStruct((S, n_heads * HEAD), F32),
                   jax.ShapeDtypeStruct((n_heads, S, LANES), F32)),
        grid=(n_heads // hp, nq),
        in_specs=[pl.BlockSpec((tq, w), lambda h, i: (i, h + q_off // hp)), head_blk(k_off), head_blk(v_off)],
        out_specs=(pl.BlockSpec((tq, w), lambda h, i: (i, h)),
                   pl.BlockSpec((hp, tq, LANES), lambda h, i: (h, i, 0))),
        scratch_shapes=[pltpu.VMEM((hp, tq, 1), F32), pltpu.VMEM((hp, tq, HEAD), F32)], name=name,
        sem=("parallel", "arbitrary"), operands=(qkv, qkv, qkv), comm=comm)
    return outs if comm is None else (outs, comm_outs)


def _sb_bwd(qkv, do, tstat, *, name, n_heads=N_SB, q_off=0, k_off=N_SB, v_off=2 * N_SB, do_off=0, tq=512,
            hp=HEADS_PER_STEP, comm=None):
    S = qkv.shape[0]
    tq = _pick(S, tq)
    tk = tq
    scale = HEAD ** -0.5
    nq = S // tq
    assert n_heads % hp == 0 and q_off % hp == 0 and k_off % hp == 0 and v_off % hp == 0 and do_off % hp == 0

    def body(q_ref, k_ref, v_ref, do_ref, t_ref, dq_ref, dk_ref, dv_ref, p_sc, r_sc, dq_sc):
        qi = pl.program_id(1)

        @pl.when(qi == 0)
        def _():
            dk_ref[...] = jnp.zeros_like(dk_ref)
            dv_ref[...] = jnp.zeros_like(dv_ref)

        p_sc[...] = jnp.zeros_like(p_sc)
        r_sc[...] = jnp.zeros_like(r_sc)
        dq_sc[...] = jnp.zeros_like(dq_sc)
        le = _tri(min(CUM_CHUNK, tk), "le")
        lt = _tri(min(CUM_CHUNK, tk), "lt")

        def tile(hh, j, diag):
            cs = slice(hh * HEAD, (hh + 1) * HEAD)
            rows = pl.ds(pl.multiple_of(j * tk, tk), tk)
            q = q_ref[:, cs]
            do_b = do_ref[:, cs].astype(BF16)
            kb = k_ref[rows, cs]
            z = _dot_nt(q, kb) * scale
            sp = jnp.log(1.0 + jnp.exp(-jnp.abs(z)))
            la = jnp.minimum(z, 0.0) - sp
            lb = -jnp.maximum(z, 0.0) - sp
            if diag:
                strict = _diag_mask(tq, True)
                lb = jnp.where(strict, lb, 0.0)
            prefix, total_b = _cum_cols(lb, le, suffix=False)
            w = jnp.exp(la + t_ref[hh, :, 0:1] - (prefix + p_sc[hh]))
            if diag:
                w = jnp.where(strict, w, 0.0)
            r = w * _dot_nt(do_b, v_ref[rows, cs])
            rex, total_r = _cum_cols(r, lt, suffix=False)
            rex = rex + r_sc[hh]
            beta = jnp.exp(la)
            dz = r - beta * (r + rex)
            if diag:
                dz = jnp.where(strict, dz, 0.0)
            dzb = dz.astype(BF16)
            dq_sc[hh] += _dot(dzb, kb)
            dk_ref[rows, cs] += _dot_tn(dzb, q)
            dv_ref[rows, cs] += _dot_tn(w.astype(BF16), do_b)
            p_sc[hh] += total_b
            r_sc[hh] += total_r

        def step(j, carry):
            for hh in range(hp):
                tile(hh, j, False)
            return carry

        lax.fori_loop(0, qi, step, 0)
        for hh in range(hp):
            tile(hh, qi, True)
            dq_ref[:, hh * HEAD:(hh + 1) * HEAD] = dq_sc[hh] * scale

        @pl.when(qi == nq - 1)
        def _():
            dk_ref[...] = dk_ref[...] * scale

    w = hp * HEAD
    head_blk = lambda off: pl.BlockSpec((S, w), lambda h, i: (0, h + off // hp))
    out_head = pl.BlockSpec((S, w), lambda h, i: (0, h))
    out_sd = jax.ShapeDtypeStruct((S, n_heads * HEAD), F32)
    outs, comm_outs = _hosted_call(
        body, out_shape=(out_sd, out_sd, out_sd), grid=(n_heads // hp, nq),
        in_specs=[pl.BlockSpec((tq, w), lambda h, i: (i, h + q_off // hp)), head_blk(k_off), head_blk(v_off),
                  pl.BlockSpec((tq, w), lambda h, i: (i, h + do_off // hp)),
                  pl.BlockSpec((hp, tq, LANES), lambda h, i: (h, i, 0))],
        out_specs=(pl.BlockSpec((tq, w), lambda h, i: (i, h)), out_head, out_head),
        scratch_shapes=[pltpu.VMEM((hp, tq, 1), F32), pltpu.VMEM((hp, tq, 1), F32), pltpu.VMEM((hp, tq, HEAD), F32)],
        name=name, sem=("parallel", "arbitrary"), operands=(qkv, qkv, qkv, do, tstat), comm=comm)
    return outs if comm is None else (outs, comm_outs)


def _attn_fwd(q, k, v, bias, *, name, n_heads, dqk, scale, v_off=0, tq=1024, exact_p=False, hp=HEADS_PER_STEP,
              comm=None):
    S = q.shape[0]
    tq = _pick(S, tq)
    tk = tq
    nq = S // tq
    has_bias = bias is not None

    assert n_heads % hp == 0 and v_off % hp == 0

    def body(*refs):
        q_ref, k_ref, v_ref = refs[:3]
        b_ref = refs[3] if has_bias else None
        o_ref, lse_ref, m_sc, l_sc, acc_sc = refs[-5:]
        qi = pl.program_id(1)
        m_sc[...] = jnp.full_like(m_sc, NEG)
        l_sc[...] = jnp.zeros_like(l_sc)
        acc_sc[...] = jnp.zeros_like(acc_sc)

        def tile(hh, j, diag):
            rows = pl.ds(pl.multiple_of(j * tk, tk), tk)
            s = _dot_nt(q_ref[:, hh * dqk:(hh + 1) * dqk], k_ref[rows, hh * dqk:(hh + 1) * dqk]) * scale
            if has_bias:
                s = s + b_ref[hh, :, rows]
            if diag:
                s = jnp.where(_diag_mask(tq, False), s, NEG)
            m_old = m_sc[hh]
            m_new = jnp.maximum(m_old, jnp.max(s, axis=1, keepdims=True))
            alpha = jnp.exp(m_old - m_new)
            p = jnp.exp(s - m_new)
            l_sc[hh] = alpha * l_sc[hh] + jnp.sum(p, axis=1, keepdims=True)
            vb = v_ref[rows, hh * HEAD:(hh + 1) * HEAD]
            if exact_p:
                hi, lo = _split2(p)
                pv = _dot(hi, vb) + _dot(lo, vb)
            else:
                pv = _dot(p.astype(BF16), vb)
            acc_sc[hh] = alpha * acc_sc[hh] + pv
            m_sc[hh] = m_new

        def step(j, carry):
            for hh in range(hp):
                tile(hh, j, False)
            return carry

        lax.fori_loop(0, qi, step, 0)
        for hh in range(hp):
            tile(hh, qi, True)
            l = l_sc[hh]
            o_ref[:, hh * HEAD:(hh + 1) * HEAD] = acc_sc[hh] / l
            lse_ref[hh] = jnp.broadcast_to(m_sc[hh] + jnp.log(l), (tq, LANES))

    in_specs = [pl.BlockSpec((tq, hp * dqk), lambda h, i: (i, h)),
                pl.BlockSpec((S, hp * dqk), lambda h, i: (0, h)),
                pl.BlockSpec((S, hp * HEAD), lambda h, i: (0, h + v_off // hp))]
    operands = [q, k, v]
    if has_bias:
        in_specs.append(pl.BlockSpec((hp, 1, S), lambda h, i: (h, 0, 0)))
        operands.append(bias)
    outs, comm_outs = _hosted_call(
        body,
        out_shape=(jax.ShapeDtypeStruct((S, n_heads * HEAD), F32),
                   jax.ShapeDtypeStruct((n_heads, S, LANES), F32)),
        grid=(n_heads // hp, nq), in_specs=in_specs,
        out_specs=(pl.BlockSpec((tq, hp * HEAD), lambda h, i: (i, h)),
                   pl.BlockSpec((hp, tq, LANES), lambda h, i: (h, i, 0))),
        scratch_shapes=[pltpu.VMEM((hp, tq, 1), F32), pltpu.VMEM((hp, tq, 1), F32), pltpu.VMEM((hp, tq, HEAD), F32)],
        name=name, sem=("parallel", "arbitrary"), operands=operands, comm=comm)
    return outs if comm is None else (outs, comm_outs)


def _attn_bwd(q, k, v, bias, o, lse, do, *, name, n_heads, dqk, scale, v_off=0, do_off=0, tq=512,
              hp=HEADS_PER_STEP, comm=None):
    S = q.shape[0]
    tq = _pick(S, tq)
    tk = tq
    nq = S // tq
    has_bias = bias is not None
    assert n_heads % hp == 0 and v_off % hp == 0 and do_off % hp == 0

    def body(*refs):
        q_ref, k_ref, v_ref, o_ref, lse_ref, do_ref = refs[:6]
        b_ref = refs[6] if has_bias else None
        n_out = 5 if has_bias else 3
        outs = refs[-(n_out + 3):-3]
        dq_ref, dk_ref, dv_ref = outs[:3]
        db_ref, dr_ref = (outs[3], outs[4]) if has_bias else (None, None)
        dq_sc, rs_sc, delta_sc = refs[-3:]
        qi = pl.program_id(1)

        @pl.when(qi == 0)
        def _():
            dk_ref[...] = jnp.zeros_like(dk_ref)
            dv_ref[...] = jnp.zeros_like(dv_ref)
            if has_bias:
                db_ref[...] = jnp.zeros_like(db_ref)

        dq_sc[...] = jnp.zeros_like(dq_sc)
        rs_sc[...] = jnp.zeros_like(rs_sc)
        for hh in range(hp):
            vs = slice(hh * HEAD, (hh + 1) * HEAD)
            do_r = do_ref[:, vs].astype(BF16).astype(F32)
            delta_sc[hh] = jnp.sum(do_r * o_ref[:, vs], axis=1, keepdims=True)

        def tile(hh, j, diag):
            qs = slice(hh * dqk, (hh + 1) * dqk)
            vs = slice(hh * HEAD, (hh + 1) * HEAD)
            rows = pl.ds(pl.multiple_of(j * tk, tk), tk)
            qb = q_ref[:, qs]
            do_b = do_ref[:, vs].astype(BF16)
            delta = delta_sc[hh]
            kb = k_ref[rows, qs]
            s = _dot_nt(qb, kb) * scale
            if has_bias:
                s = s + b_ref[hh, :, rows]
            p = jnp.exp(s - lse_ref[hh, :, 0:1])
            if diag:
                p = jnp.where(_diag_mask(tq, False), p, 0.0)
            ds = p * (_dot_nt(do_b, v_ref[rows, vs]) - delta)
            dsb = (ds * scale).astype(BF16)
            dq_sc[hh] += _dot(dsb, kb)
            dk_ref[rows, qs] += _dot_tn(dsb, qb)
            dv_ref[rows, vs] += _dot_tn(p.astype(BF16), do_b)
            if has_bias:
                db_ref[hh, :, rows] += jnp.sum(ds, axis=0, keepdims=True)
                rs_sc[hh] += jnp.sum(ds, axis=1, keepdims=True)

        def step(j, carry):
            for hh in range(hp):
                tile(hh, j, False)
            return carry

        lax.fori_loop(0, qi, step, 0)
        for hh in range(hp):
            tile(hh, qi, True)
            dq_ref[:, hh * dqk:(hh + 1) * dqk] = dq_sc[hh]
            if has_bias:
                dr_ref[hh] = jnp.broadcast_to(rs_sc[hh], (tq, LANES))

    stat = pl.BlockSpec((hp, tq, LANES), lambda h, i: (h, i, 0))
    in_specs = [pl.BlockSpec((tq, hp * dqk), lambda h, i: (i, h)),
                pl.BlockSpec((S, hp * dqk), lambda h, i: (0, h)),
                pl.BlockSpec((S, hp * HEAD), lambda h, i: (0, h + v_off // hp)),
                pl.BlockSpec((tq, hp * HEAD), lambda h, i: (i, h)),
                stat,
                pl.BlockSpec((tq, hp * HEAD), lambda h, i: (i, h + do_off // hp))]
    operands = [q, k, v, o, lse, do]
    out_shape = [jax.ShapeDtypeStruct((S, n_heads * dqk), F32), jax.ShapeDtypeStruct((S, n_heads * dqk), F32),
                 jax.ShapeDtypeStruct((S, n_heads * HEAD), F32)]
    out_specs = [pl.BlockSpec((tq, hp * dqk), lambda h, i: (i, h)), pl.BlockSpec((S, hp * dqk), lambda h, i: (0, h)),
                 pl.BlockSpec((S, hp * HEAD), lambda h, i: (0, h))]
    if has_bias:
        in_specs.append(pl.BlockSpec((hp, 1, S), lambda h, i: (h, 0, 0)))
        operands.append(bias)
        out_shape.append(jax.ShapeDtypeStruct((n_heads, 1, S), F32))
        out_specs.append(pl.BlockSpec((hp, 1, S), lambda h, i: (h, 0, 0)))
        out_shape.append(jax.ShapeDtypeStruct((n_heads, S, LANES), F32))
        out_specs.append(stat)
    outs, comm_outs = _hosted_call(
        body, out_shape=tuple(out_shape), grid=(n_heads // hp, nq), in_specs=in_specs, out_specs=tuple(out_specs),
        scratch_shapes=[pltpu.VMEM((hp, tq, dqk), F32), pltpu.VMEM((hp, tq, 1), F32),
                        pltpu.VMEM((hp, tq, 1), F32)], name=name,
        sem=("parallel", "arbitrary"), operands=operands, comm=comm)
    return outs if comm is None else (outs, comm_outs)


def _rot_half(y):
    lane = lax.broadcasted_iota(jnp.int32, y.shape, 1)
    up = pltpu.roll(y, 96, 1)
    down = pltpu.roll(y, 32, 1)
    return jnp.where(lane < 32, -up, jnp.where(lane < 64, down, 0.0))


def _mla_prep_fwd(q_raw, kv_raw, down, cos, sin, q_g, k_g, *, name, ts=128):
    S = q_raw.shape[0]
    ts = _pick(S, ts)
    pe_blk = Q_RANK // LANES + KV_RANK // LANES

    def norm_rope(x0, x1, g0, g1, c, s):
        ms = (jnp.sum(x0 * x0, axis=-1, keepdims=True) + jnp.sum(x1 * x1, axis=-1, keepdims=True)) * (1.0 / QK_DIM)
        rstd = lax.rsqrt(ms + EPS)
        y0 = (x0 * rstd) * g0
        y1 = (x1 * rstd) * g1
        return y0, y1 * c + _rot_half(y1) * s

    def body(q_ref, kv_ref, pe_ref, cos_ref, sin_ref, qg_ref, kg_ref, qo_ref, ko_ref, vo_ref):
        c, s = cos_ref[...], sin_ref[...]
        pe = pe_ref[...]
        qg0, qg1 = qg_ref[:, :NOPE], qg_ref[:, NOPE:]
        kg0, kg1 = kg_ref[:, :NOPE], kg_ref[:, NOPE:]
        for h in range(N_MLA):
            b = h * QK_PAD
            y0, y1 = norm_rope(q_ref[:, b:b + NOPE], q_ref[:, b + NOPE:b + QK_PAD], qg0, qg1, c, s)
            qo_ref[:, b:b + NOPE] = y0.astype(BF16)
            qo_ref[:, b + NOPE:b + QK_PAD] = y1.astype(BF16)
            y0, y1 = norm_rope(kv_ref[:, b:b + NOPE], pe, kg0, kg1, c, s)
            ko_ref[:, b:b + NOPE] = y0.astype(BF16)
            ko_ref[:, b + NOPE:b + QK_PAD] = y1.astype(BF16)
            vo_ref[:, h * HEAD:(h + 1) * HEAD] = kv_ref[:, b + NOPE:b + QK_PAD].astype(BF16)

    wide = pl.BlockSpec((ts, N_MLA * QK_PAD), lambda i: (i, 0))
    lane_blk = pl.BlockSpec((ts, LANES), lambda i: (i, 0))
    gain = pl.BlockSpec((1, QK_PAD), lambda i: (0, 0))
    return pl.pallas_call(
        body,
        out_shape=(jax.ShapeDtypeStruct((S, N_MLA * QK_PAD), BF16), jax.ShapeDtypeStruct((S, N_MLA * QK_PAD), BF16),
                   jax.ShapeDtypeStruct((S, N_MLA * HEAD), BF16)),
        grid=(S // ts,),
        in_specs=[wide, wide, pl.BlockSpec((ts, LANES), lambda i: (i, pe_blk)), lane_blk, lane_blk, gain, gain],
        out_specs=(wide, wide, pl.BlockSpec((ts, N_MLA * HEAD), lambda i: (i, 0))), name=name,
        compiler_params=_cp("parallel"))(q_raw, kv_raw, down, cos, sin, q_g, k_g)


def _mla_prep_bwd(dq, dk, dv, q_raw, kv_raw, down, cos, sin, q_g, k_g, *, name, ts=128, comm=None):
    S = q_raw.shape[0]
    ts = _pick(S, ts)
    nsteps = S // ts
    pe_blk = Q_RANK // LANES + KV_RANK // LANES

    def back(x0, x1, g0, g1, c, s, d0, d1r):
        d1 = d1r * c - _rot_half(d1r * s)
        ms = (jnp.sum(x0 * x0, axis=-1, keepdims=True) + jnp.sum(x1 * x1, axis=-1, keepdims=True)) * (1.0 / QK_DIM)
        rstd = lax.rsqrt(ms + EPS)
        h0, h1 = x0 * rstd, x1 * rstd
        e0, e1 = d0 * g0, d1 * g1
        m = (jnp.sum(e0 * h0, axis=-1, keepdims=True) + jnp.sum(e1 * h1, axis=-1, keepdims=True)) * (1.0 / QK_DIM)
        return rstd * (e0 - h0 * m), rstd * (e1 - h1 * m), d0 * h0, d1 * h1

    def fold(a):
        return jnp.sum(a.reshape(ts // 8, 8, a.shape[-1]), axis=0)

    def body(dq_ref, dk_ref, dv_ref, q_ref, kv_ref, pe_ref, cos_ref, sin_ref, qg_ref, kg_ref,
             dqr_ref, dkv_ref, dpe_ref, dqg_ref, dkg_ref, gq_sc, gk_sc):
        i = pl.program_id(0)

        @pl.when(i == 0)
        def _():
            gq_sc[...] = jnp.zeros_like(gq_sc)
            gk_sc[...] = jnp.zeros_like(gk_sc)

        c, s = cos_ref[...], sin_ref[...]
        pe = pe_ref[...]
        qg0, qg1 = qg_ref[:, :NOPE], qg_ref[:, NOPE:]
        kg0, kg1 = kg_ref[:, :NOPE], kg_ref[:, NOPE:]
        dpe = jnp.zeros((ts, LANES), F32)
        for h in range(N_MLA):
            b = h * QK_PAD
            dx0, dx1, a0, a1 = back(q_ref[:, b:b + NOPE], q_ref[:, b + NOPE:b + QK_PAD], qg0, qg1, c, s,
                                    dq_ref[:, b:b + NOPE], dq_ref[:, b + NOPE:b + QK_PAD])
            dqr_ref[:, b:b + NOPE] = dx0.astype(BF16)
            dqr_ref[:, b + NOPE:b + QK_PAD] = dx1.astype(BF16)
            gq_sc[:, :NOPE] += fold(a0)
            gq_sc[:, NOPE:] += fold(a1)
            dx0, dx1, a0, a1 = back(kv_ref[:, b:b + NOPE], pe, kg0, kg1, c, s,
                                    dk_ref[:, b:b + NOPE], dk_ref[:, b + NOPE:b + QK_PAD])
            dkv_ref[:, b:b + NOPE] = dx0.astype(BF16)
            dkv_ref[:, b + NOPE:b + QK_PAD] = dv_ref[:, h * HEAD:(h + 1) * HEAD].astype(BF16)
            dpe = dpe + dx1
            gk_sc[:, :NOPE] += fold(a0)
            gk_sc[:, NOPE:] += fold(a1)
        dpe_ref[...] = dpe

        @pl.when(i == nsteps - 1)
        def _():
            dqg_ref[...] = jnp.sum(gq_sc[...], axis=0, keepdims=True)
            dkg_ref[...] = jnp.sum(gk_sc[...], axis=0, keepdims=True)

    wide = pl.BlockSpec((ts, N_MLA * QK_PAD), lambda i: (i, 0))
    lane_blk = pl.BlockSpec((ts, LANES), lambda i: (i, 0))
    gain = pl.BlockSpec((1, QK_PAD), lambda i: (0, 0))
    outs, comm_outs = _hosted_call(
        body,
        out_shape=(jax.ShapeDtypeStruct((S, N_MLA * QK_PAD), BF16), jax.ShapeDtypeStruct((S, N_MLA * QK_PAD), BF16),
                   jax.ShapeDtypeStruct((S, LANES), F32), jax.ShapeDtypeStruct((1, QK_PAD), F32),
                   jax.ShapeDtypeStruct((1, QK_PAD), F32)),
        grid=(nsteps,),
        in_specs=[wide, wide, pl.BlockSpec((ts, N_MLA * HEAD), lambda i: (i, 0)), wide, wide,
                  pl.BlockSpec((ts, LANES), lambda i: (i, pe_blk)), lane_blk, lane_blk, gain, gain],
        out_specs=(wide, wide, lane_blk, gain, gain),
        scratch_shapes=[pltpu.VMEM((8, QK_PAD), F32), pltpu.VMEM((8, QK_PAD), F32)], name=name,
        sem=("arbitrary",), operands=(dq, dk, dv, q_raw, kv_raw, down, cos, sin, q_g, k_g), comm=comm)
    res = (outs[0], outs[1], outs[2], outs[3][0], outs[4][0])
    return res if comm is None else (res, comm_outs)


def _loss_head(y, target, *, name, tr=256):
    R, C = y.shape
    tr = _pick(R, tr)
    nsteps = R // tr

    def body(y_ref, t_ref, dy_ref, dyb_ref, loss_ref, acc):
        i = pl.program_id(0)

        @pl.when(i == 0)
        def _():
            acc[...] = jnp.zeros_like(acc)

        err = y_ref[...] - t_ref[...]
        dy = err * (1.0 / C)
        dy_ref[...] = dy
        dyb_ref[...] = dy.astype(BF16)
        acc[...] += jnp.sum((err * err).reshape(tr // 8, 8, C), axis=0)

        @pl.when(i == nsteps - 1)
        def _():
            tot = jnp.sum(jnp.sum(acc[...], axis=0, keepdims=True), axis=1, keepdims=True)
            loss_ref[...] = jnp.broadcast_to(tot * (0.5 / C), (8, LANES))

    blk = pl.BlockSpec((tr, C), lambda i: (i, 0))
    dy, dy_b, loss = pl.pallas_call(
        body, out_shape=(jax.ShapeDtypeStruct((R, C), F32), jax.ShapeDtypeStruct((R, C), BF16),
                         jax.ShapeDtypeStruct((8, LANES), F32)),
        grid=(nsteps,), in_specs=[blk, blk], out_specs=(blk, blk, pl.BlockSpec((8, LANES), lambda i: (0, 0))),
        scratch_shapes=[pltpu.VMEM((8, C), F32)], name=name, compiler_params=_cp("arbitrary"))(y, target)
    return dy, dy_b, loss[0, 0]


def _adamw(w, g, m, v, *, name, block_bytes=1 << 20, comm=None):
    L, R, C = w.shape
    tr = max(8, min(R, (block_bytes // (4 * C)) // 8 * 8))
    while R % tr:
        tr -= 8
    if tr <= 0:
        tr = R
    c1 = 1.0 / (1.0 - ADAM_B1 ** ADAM_STEP)
    c2 = 1.0 / (1.0 - ADAM_B2 ** ADAM_STEP)

    def body(w_ref, g_ref, m_ref, v_ref, d_ref, mo_ref, vo_ref):
        gv = g_ref[...]
        mn = ADAM_B1 * m_ref[...] + (1.0 - ADAM_B1) * gv
        vn = ADAM_B2 * v_ref[...] + (1.0 - ADAM_B2) * (gv * gv)
        d_ref[...] = -ADAM_LR * ((mn * c1) / (jnp.sqrt(vn * c2) + ADAM_EPS) + ADAM_WD * w_ref[...])
        mo_ref[...] = mn
        vo_ref[...] = vn

    blk = pl.BlockSpec((None, tr, C), lambda l, i: (l, i, 0))
    sd = jax.ShapeDtypeStruct((L, R, C), F32)
    outs, comm_outs = _hosted_call(
        body, out_shape=(sd, sd, sd), grid=(L, R // tr), in_specs=[blk] * 4, out_specs=(blk,) * 3, scratch_shapes=[],
        name=name, sem=("parallel", "parallel"), operands=(w, g, m, v), comm=comm)
    return outs if comm is None else (outs, comm_outs)


def _row_tile(r, c, itemsize=4, block_bytes=1 << 20):
    tr = max(16, min(r, (block_bytes // (itemsize * c)) // 16 * 16))
    while r % tr:
        tr -= 16
    return tr if tr > 0 else r


def _add_sibling(g, recv, core, *, name):
    nch, _, r, c = g.shape
    tr = _row_tile(r, c)

    def body(core_ref, g_ref, r_ref, o_ref):
        o_ref[...] = (g_ref[...] + r_ref[...]).astype(BF16)

    grid_spec = pltpu.PrefetchScalarGridSpec(
        num_scalar_prefetch=1, grid=(nch, r // tr),
        in_specs=[pl.BlockSpec((None, None, tr, c), lambda j, i, cr: (j, cr[0], i, 0)),
                  pl.BlockSpec((None, tr, c), lambda j, i, cr: (j, i, 0))],
        out_specs=pl.BlockSpec((None, tr, c), lambda j, i, cr: (j, i, 0)))
    return pl.pallas_call(
        body, out_shape=jax.ShapeDtypeStruct((nch, r, c), BF16), grid_spec=grid_spec, name=name,
        compiler_params=_cp("parallel", "parallel"))(core, g, recv)


def _add_chips(slots, *, name):
    nch, r, c = slots.shape
    tr = _row_tile(r, c)

    def body(s_ref, o_ref):
        acc = s_ref[0].astype(F32)
        for j in range(1, nch):
            acc = acc + s_ref[j].astype(F32)
        o_ref[...] = acc

    return pl.pallas_call(
        body, out_shape=jax.ShapeDtypeStruct((r, c), F32), grid=(r // tr,),
        in_specs=[pl.BlockSpec((nch, tr, c), lambda i: (0, i, 0))],
        out_specs=pl.BlockSpec((tr, c), lambda i: (i, 0)), name=name, compiler_params=_cp("parallel"))(slots)


def _place():
    x, y, c = lax.axis_index("x"), lax.axis_index("y"), lax.axis_index("c")
    others = [(1 - x, y), (x, 1 - y), (1 - x, 1 - y)]
    return x, y, c, 2 * x + y, others


ANY = pl.BlockSpec(memory_space=pl.ANY)


class _Exchange:
    def __init__(self, kind, arrays):
        self.kind, self.ins = kind, list(arrays)
        self.n_peers = 1 if kind in ("swap", "join") else 3
        self.aliased = kind == "forward"
        n = len(self.ins) * self.n_peers
        shp = {"gather": lambda a: (N_CHIPS,) + a.shape, "scatter": lambda a: a.shape, "forward": lambda a: a.shape,
               "swap": lambda a: (a.shape[0],) + a.shape[2:], "join": lambda a: (2,) + a.shape}[kind]
        self.out_shapes = [jax.ShapeDtypeStruct(shp(a), a.dtype) for a in self.ins]
        self.sems = [pltpu.SemaphoreType.DMA((n,)), pltpu.SemaphoreType.DMA((n,))]

    def _copies(self, ins, outs, sems):
        send, recv = sems
        x, y, c, me, others = _place()
        over_ici = self.kind in ("gather", "scatter")
        peers = [(ox, oy, c) for ox, oy in others] if over_ici else [(x, y, 1 - c)] * self.n_peers
        for a in range(len(self.ins)):
            for k, to in enumerate(peers):
                peer = 2 * others[k][0] + others[k][1]
                if self.kind == "gather":
                    hr = self.ins[a].shape[0] // 2
                    rows = pl.ds(c * hr, hr)
                    src, dst, land = ins[a].at[rows, :], outs[a].at[me, rows, :], outs[a].at[peer, rows, :]
                elif self.kind == "forward":
                    hr = self.ins[a].shape[1] // 2
                    mine, theirs = pl.ds(c * hr, hr), pl.ds((1 - c) * hr, hr)
                    src, dst, land = ins[a].at[peer, mine, :], outs[a].at[peer, mine, :], outs[a].at[peer, theirs, :]
                elif self.kind == "scatter":
                    src, dst, land = ins[a].at[peer], outs[a].at[me], outs[a].at[peer]
                elif self.kind == "swap":
                    src, dst, land = ins[a].at[:, 1 - c], outs[a], outs[a]
                else:
                    src, dst, land = ins[a], outs[a].at[c], outs[a].at[1 - c]
                i = self.n_peers * a + k
                mk = lambda s, d: pltpu.make_async_remote_copy(
                    src_ref=s, dst_ref=d, send_sem=send.at[i], recv_sem=recv.at[i], device_id=to,
                    device_id_type=MESH)
                yield mk(src, dst), mk(land, land)

    def alias_pairs(self):
        return [(i, i) for i in range(len(self.ins))] if self.aliased else []

    def start(self, ins, outs, sems):
        for cp, _ in self._copies(ins, outs, sems):
            cp.start()

    def finish(self, ins, outs, sems):
        pairs = list(self._copies(ins, outs, sems))
        for _, landing in pairs:
            landing.wait_recv()
        for cp, _ in pairs:
            cp.wait_send()


class _Several:
    def __init__(self, parts):
        self.parts = list(parts)
        self.ins = [a for p in self.parts for a in p.ins]
        self.out_shapes = [s for p in self.parts for s in p.out_shapes]
        self.sems = [s for p in self.parts for s in p.sems]

    def split(self, ins, outs, sems=None):
        i = 0
        for k, p in enumerate(self.parts):
            n = len(p.ins)
            yield p, ins[i:i + n], outs[i:i + n], None if sems is None else sems[2 * k:2 * k + 2]
            i += n

    def alias_pairs(self):
        pairs, i = [], 0
        for p in self.parts:
            pairs += [(i + a, i + b) for a, b in p.alias_pairs()]
            i += len(p.ins)
        return pairs

    def start(self, ins, outs, sems):
        for p, a, b, s in self.split(ins, outs, sems):
            p.start(a, b, s)

    def finish(self, ins, outs, sems):
        for p, a, b, s in self.split(ins, outs, sems):
            p.finish(a, b, s)


def _hosted_call(body, *, grid, in_specs, out_specs, out_shape, scratch_shapes, operands, name, sem, comm=None):
    out_specs, out_shape = tuple(out_specs), tuple(out_shape)
    if isinstance(comm, (list, tuple)):
        several = _Several(comm)
        outs, comm_outs = _hosted_call(body, grid=grid, in_specs=in_specs, out_specs=out_specs, out_shape=out_shape,
                                       scratch_shapes=scratch_shapes, operands=operands, name=name, sem=sem,
                                       comm=several)
        return outs, [tuple(o) for _, _, o, _ in several.split(several.ins, comm_outs)]
    if comm is None:
        res = pl.pallas_call(body, out_shape=out_shape, grid=grid, in_specs=list(in_specs), out_specs=out_specs,
                             scratch_shapes=list(scratch_shapes), name=name, compiler_params=_cp(*sem))(*operands)
        return tuple(res), ()
    n_in, n_out, n_sc = len(in_specs), len(out_specs), len(scratch_shapes)
    ci, co = len(comm.ins), len(comm.out_shapes)

    def wrapped(*refs):
        ins, c_ins = refs[:n_in], refs[n_in:n_in + ci]
        outs = refs[n_in + ci:n_in + ci + n_out]
        c_outs = refs[n_in + ci + n_out:n_in + ci + n_out + co]
        scratch = refs[n_in + ci + n_out + co:n_in + ci + n_out + co + n_sc]
        sems = refs[n_in + ci + n_out + co + n_sc:]
        ids = [pl.program_id(d) for d in range(len(grid))]
        first = functools.reduce(jnp.logical_and, [i == 0 for i in ids])
        last = functools.reduce(jnp.logical_and, [i == g - 1 for i, g in zip(ids, grid)])

        @pl.when(first)
        def _():
            comm.start(c_ins, c_outs, sems)

        body(*ins, *outs, *scratch)

        @pl.when(last)
        def _():
            comm.finish(c_ins, c_outs, sems)

    res = pl.pallas_call(
        wrapped, out_shape=out_shape + tuple(comm.out_shapes), grid=grid, in_specs=list(in_specs) + [ANY] * ci,
        out_specs=out_specs + tuple([ANY] * co), scratch_shapes=list(scratch_shapes) + comm.sems, name=name,
        input_output_aliases={n_in + i: n_out + o for i, o in comm.alias_pairs()},
        compiler_params=pltpu.CompilerParams(dimension_semantics=("arbitrary",) * len(grid),
                                             vmem_limit_bytes=VMEM_LIMIT, has_side_effects=True),
    )(*operands, *comm.ins)
    return tuple(res[:n_out]), tuple(res[n_out:])


def _run_exchange(comm, *, name):
    ci = len(comm.ins)

    def body(*refs):
        ins, outs, sems = refs[:ci], refs[ci:2 * ci], refs[2 * ci:]
        comm.start(ins, outs, sems)
        comm.finish(ins, outs, sems)

    return pl.pallas_call(
        body, out_shape=tuple(comm.out_shapes), in_specs=[ANY] * ci, out_specs=tuple([ANY] * ci),
        scratch_shapes=comm.sems, name=name, input_output_aliases=dict(comm.alias_pairs()),
        compiler_params=pltpu.CompilerParams(has_side_effects=True))(*comm.ins)


def _own_slot(buf, piece, idx):
    return lax.dynamic_update_slice(buf, piece[None], (idx,) + (0,) * piece.ndim)


def _all_reduce_small(v, *, name):
    R = v.shape[0]

    flips = [(dx, dy, dc) for dx in range(2) for dy in range(2) for dc in range(2) if dx or dy or dc]

    def body(v_ref, o_ref, slots, send, recv):
        x, y, c, me, others = _place()
        mine = 2 * me + c
        slots[mine] = v_ref[...]

        def copy(k, slot):
            dx, dy, dc = flips[k]
            peer = (x + dx - 2 * x * dx, y + dy - 2 * y * dy, c + dc - 2 * c * dc)
            peer_slot = 4 * peer[0] + 2 * peer[1] + peer[2]
            return pltpu.make_async_remote_copy(
                src_ref=v_ref, dst_ref=slots.at[mine if slot == "mine" else peer_slot], send_sem=send.at[k],
                recv_sem=recv.at[k], device_id=peer, device_id_type=MESH)

        for k in range(7):
            copy(k, "mine").start()
        for k in range(7):
            copy(k, "peer").wait_recv()
        for k in range(7):
            copy(k, "mine").wait_send()
        acc = slots[0]
        for j in range(1, 8):
            acc = acc + slots[j]
        o_ref[...] = acc

    vm = pl.BlockSpec(memory_space=pltpu.VMEM)
    return pl.pallas_call(
        body, out_shape=jax.ShapeDtypeStruct(v.shape, F32), in_specs=[vm], out_specs=vm,
        scratch_shapes=[pltpu.VMEM((8, R, LANES), F32), pltpu.SemaphoreType.DMA((7,)),
                        pltpu.SemaphoreType.DMA((7,))],
        name=name, compiler_params=pltpu.CompilerParams(has_side_effects=True))(v)


def _rows(v, n_rows):
    v = v.reshape(-1).astype(F32)
    return jnp.pad(v, (0, n_rows * LANES - v.shape[0])).reshape(n_rows, LANES)


def _mlp_fwd(x_in, g, w_up, w_down, tag):
    h = _rms_fwd(x_in, g, name=f"{tag}_norm")
    u, a = _matmul(h, w_up, b_split=True, epilogue="sqrelu", name=f"{tag}_up")
    x_out = _matmul(a, w_down, epilogue="res", res=x_in, name=f"{tag}_down")
    return x_out, (h, u, a)


def _mlp_bwd(dy, dy_b, x_in, g, w_up, w_down, saved, tag, comms=None):
    h, u, a = saved
    comms = comms or {}
    landed = {}

    def mm(key, *args, **kw):
        comm = comms.get(key)
        if callable(comm):
            comm = comm(landed)
        out = _matmul(*args, name=f"{tag}_{key}", comm=comm, **kw)
        if comm is not None:
            out, landed[key] = out
        return out

    dw_down = mm("dwdown", a, dy_b, form="tn")
    du = mm("du", dy_b, w_down, form="nt", epilogue="sqrelu_bwd", u=u, out_dtype=BF16)
    dw_up = mm("dwup", h, du, form="tn", out_split=True)
    dh = mm("dh", du, w_up, form="nt", b_split=True)
    dx, dx_b, dg = _rms_bwd(x_in, g, dh, res=dy, bf16_copy=True, name=f"{tag}_dnorm")
    return dx, dx_b, dg, dw_up, dw_down, landed


def kernel(x, positions, ln_mix_g, ln_mlp_g, sf_w_in, sf_b_f, fox_q_g, fox_k_g, sf_w_o, mla_w_down, mla_q_a_g, mla_kv_a_g, mla_w_uq, mla_w_ukv, mla_q_g, mla_k_g, mla_w_o, mlp_w_up, mlp_w_down, loss_target, m_ln_mix_g, m_ln_mlp_g, m_sf_w_in, m_sf_b_f, m_fox_q_g, m_fox_k_g, m_sf_w_o, m_mla_w_down, m_mla_q_a_g, m_mla_kv_a_g, m_mla_w_uq, m_mla_w_ukv, m_mla_q_g, m_mla_k_g, m_mla_w_o, m_mlp_w_up, m_mlp_w_down, v_ln_mix_g, v_ln_mlp_g, v_sf_w_in, v_sf_b_f, v_fox_q_g, v_fox_k_g, v_sf_w_o, v_mla_w_down, v_mla_q_a_g, v_mla_kv_a_g, v_mla_w_uq, v_mla_w_ukv, v_mla_q_g, v_mla_k_g, v_mla_w_o, v_mlp_w_up, v_mlp_w_down):
    S, D = x.shape[1], x.shape[2]
    xs, tgt, pos = x[0], loss_target[0], positions[0]
    xi, yi, ci = lax.axis_index("x"), lax.axis_index("y"), lax.axis_index("c")
    chip = 2 * xi + yi
    core = ci.astype(jnp.int32).reshape(1)
    d_ff = mlp_w_up.shape[2] * N_CHIPS
    in_w = sf_w_in.shape[2] * N_CHIPS
    qkv_w = 3 * N_SB * HEAD + 3 * N_FOX * HEAD
    dn_w = mla_w_down.shape[2]
    dn_pad = Q_RANK + KV_RANK + LANES

    def gather_begin(ws):
        shards = [w.astype(BF16) for w in ws]
        return shards, _Exchange("gather", shards)

    def hand_over(landed):
        return _Exchange("forward", list(landed))

    def gather_end(both, shards):
        return [_own_slot(ag, s, chip) for ag, s in zip(both, shards)]

    cols = lambda ag: ag.transpose(1, 0, 2).reshape(ag.shape[1], -1)
    rows = lambda ag: ag.reshape(-1, ag.shape[2])
    s_in, ex_in = gather_begin([sf_w_in[0]])
    landed = _run_exchange(ex_in, name="gather_mix0")
    ag_in, = gather_end(_run_exchange(hand_over(landed), name="gather_mix0_sibling"), s_in)
    w_in_full = cols(ag_in)
    w_qkv = w_in_full[:, :qkv_w]
    w_f = jnp.pad(w_in_full[:, qkv_w:], ((0, 0), (0, LANES - (in_w - qkv_w))))
    s_o0, ex_o0 = gather_begin([sf_w_o[0]])
    s_mlp0, ex_mlp0 = gather_begin([mlp_w_up[0], mlp_w_down[0]])
    s_mix1, ex_mix1 = gather_begin([mla_w_down[0], mla_w_uq[0], mla_w_ukv[0], mla_w_o[0]])
    s_mlp1, ex_mlp1 = gather_begin([mlp_w_up[1], mlp_w_down[1]])

    gain_blk = jnp.concatenate([mla_q_a_g, mla_kv_a_g], axis=0) * (ci == 0).astype(F32)
    placed = jnp.zeros((2, N_CHIPS, LANES), F32)
    placed = lax.dynamic_update_slice(placed, gain_blk[:, None, :], (0, chip, 0))
    gains = _all_reduce_small(placed.reshape(2 * N_CHIPS, LANES), name="gather_gains")
    q_a_full = gains[:N_CHIPS].reshape(Q_RANK)
    kv_a_full = gains[N_CHIPS:].reshape(KV_RANK)

    pad_gain = lambda g: jnp.pad(g.reshape(1, QK_DIM), ((0, 0), (0, QK_PAD - QK_DIM)))
    q_g_pad, k_g_pad = pad_gain(mla_q_g), pad_gain(mla_k_g)
    b_pad = _rows(sf_b_f, 1)

    h0 = _rms_fwd(xs, ln_mix_g[0], name="mix0_norm")
    qkv_sb, landed = _matmul(h0, w_qkv, n=3 * N_SB * HEAD, b_n0=0, out_dtype=BF16, name="mix0_qkv_sb", comm=ex_o0)
    qk_fx, both = _matmul(h0, w_qkv, n=2 * N_FOX * HEAD, b_n0=3 * N_SB * HEAD, name="mix0_qk_fox",
                          comm=hand_over(landed))
    ag_o0, = gather_end(both, s_o0)
    w_o0 = rows(ag_o0)
    v_fx = _matmul(h0, w_qkv, n=N_FOX * HEAD, b_n0=(3 * N_SB + 2 * N_FOX) * HEAD, out_dtype=BF16,
                   name="mix0_v_fox")
    fl = _matmul(h0, w_f, name="mix0_forget_logit")
    f_cum = _forget_fwd(fl, b_pad, name="forget_fwd")
    neg_f = (-f_cum[:, :N_FOX]).T.reshape(N_FOX, 1, S)
    q_f = _rms_fwd(qk_fx, fox_q_g[0], c0=0, width=N_FOX * HEAD, gw=HEAD, name="fox_q_norm")
    k_f = _rms_fwd(qk_fx, fox_k_g[0], c0=N_FOX * HEAD, width=N_FOX * HEAD, gw=HEAD, name="fox_k_norm")
    (o_sb, t_sb), landed = _sb_fwd(qkv_sb, name="sb_fwd", comm=ex_mlp0, tq=1024)
    (o_fx, lse0), (landed, both) = _attn_fwd(q_f, k_f, v_fx, neg_f, n_heads=N_FOX, dqk=HEAD, scale=HEAD ** -0.5,
                                             exact_p=True, name="fox_fwd", comm=[ex_mix1, hand_over(landed)])
    ag_up0, ag_dw0 = gather_end(both, s_mlp0)
    o0 = jnp.concatenate([o_sb, o_fx], axis=1).astype(BF16)
    x1, both = _matmul(o0, w_o0, epilogue="res", res=xs, name="mix0_out", comm=hand_over(landed))
    ag_dn, ag_uq, ag_ukv, ag_o1 = gather_end(both, s_mix1)
    w_dn = jnp.pad(rows(ag_dn), ((0, 0), (0, dn_pad - dn_w)))
    w_uq = jnp.pad(cols(ag_uq).reshape(Q_RANK, N_MLA, QK_DIM), ((0, 0), (0, 0), (0, QK_PAD - QK_DIM)))
    w_uq = w_uq.reshape(Q_RANK, N_MLA * QK_PAD)
    w_ukv = cols(ag_ukv)
    w_o1 = rows(ag_o1)
    x2, mlp0 = _mlp_fwd(x1, ln_mlp_g[0], ag_up0, rows(ag_dw0), "mlp0")

    h2 = _rms_fwd(x2, ln_mix_g[1], name="mix1_norm")
    down = _matmul(h2, w_dn, name="mix1_down")
    c_q = _rms_fwd(down, q_a_full, c0=0, width=Q_RANK, name="mix1_q_a_norm")
    c_kv = _rms_fwd(down, kv_a_full, c0=Q_RANK, width=KV_RANK, name="mix1_kv_a_norm")
    q_raw = _matmul(c_q, w_uq, name="mix1_uq")
    kv_raw = _matmul(c_kv, w_ukv, name="mix1_ukv")
    half = ROPE // 2
    inv_freq = ROPE_THETA ** (-jnp.arange(half, dtype=F32) / half)
    ang = pos.astype(F32)[:, None] * inv_freq
    table = lambda t: jnp.pad(jnp.concatenate([t, t], axis=1), ((0, 0), (0, LANES - ROPE)))
    cos_t, sin_t = table(jnp.cos(ang)), table(jnp.sin(ang))
    q_pad, k_pad, v1 = _mla_prep_fwd(q_raw, kv_raw, down, cos_t, sin_t, q_g_pad, k_g_pad, name="mla_prep_fwd")
    (o1, lse1), landed = _attn_fwd(q_pad, k_pad, v1, None, n_heads=N_MLA, dqk=QK_PAD, scale=QK_DIM ** -0.5,
                                   name="mla_fwd", comm=ex_mlp1)
    o1_b = o1.astype(BF16)
    x3, both = _matmul(o1_b, w_o1, epilogue="res", res=x2, name="mix1_out", comm=hand_over(landed))
    ag_up1, ag_dw1 = gather_end(both, s_mlp1)
    w_up = [ag_up0, ag_up1]
    w_dw = [rows(ag_dw0), rows(ag_dw1)]
    x4, mlp1 = _mlp_fwd(x3, ln_mlp_g[1], w_up[1], w_dw[1], "mlp1")

    dx4, dx4_b, loss_local = _loss_head(x4, tgt, name="loss_head")
    loss = lax.psum(loss_local, ("x", "y", "c"))

    by_cols = lambda g: g.reshape(g.shape[0], N_CHIPS, -1).transpose(1, 0, 2)
    by_rows = lambda g: g.reshape(N_CHIPS, g.shape[0] // N_CHIPS, g.shape[1])
    halves = lambda g: g.reshape(N_CHIPS, 2, g.shape[1] // 2, g.shape[2])

    def scatter_of(grads, from_sibling, tags):
        parts = [_add_sibling(g, r, core, name=f"add_sibling_{t}") for g, r, t in zip(grads, from_sibling, tags)]
        return parts, _Exchange("scatter", parts)

    def sums_of(slots, parts, tags):
        slots = [_own_slot(s, lax.dynamic_index_in_dim(p, chip, 0, keepdims=False), chip)
                 for s, p in zip(slots, parts)]
        return [_add_chips(s, name=f"add_chips_{t}") for s, t in zip(slots, tags)]

    def shards_of(joined, mine):
        return [_own_slot(j, m, ci).reshape(2 * m.shape[0], m.shape[1]) for j, m in zip(joined, mine)]

    dx3, dx3_b, dg_mlp1, dw_up1, dw_dw1, _ = _mlp_bwd(dx4, dx4_b, x3, ln_mlp_g[1], w_up[1], w_dw[1], mlp1, "mlp1")
    tags_mlp1 = ["w_up1", "w_dw1"]
    g_mlp1 = [halves(dw_up1), halves(by_rows(dw_dw1))]

    dw_o1 = _matmul(o1_b, dx3_b, form="tn", name="mix1_dwo")
    do1 = _matmul(dx3_b, w_o1, form="nt", name="mix1_do")
    (dq_pad, dk_pad, dv1), from_sibling = _attn_bwd(
        q_pad, k_pad, v1, None, o1, lse1, do1, n_heads=N_MLA, dqk=QK_PAD, scale=QK_DIM ** -0.5, name="mla_bwd",
        comm=_Exchange("swap", g_mlp1))
    p_mlp1, _ = scatter_of(g_mlp1, from_sibling, tags_mlp1)
    (dq_raw, dkv_raw, dpe, dg_q, dg_k), sl_up1 = _mla_prep_bwd(
        dq_pad, dk_pad, dv1, q_raw, kv_raw, down, cos_t, sin_t, q_g_pad, k_g_pad, name="mla_prep_bwd",
        comm=_Exchange("scatter", p_mlp1[:1]))
    dw_uq = _matmul(c_q, dq_raw, form="tn", name="mix1_dwuq")
    dc_q = _matmul(dq_raw, w_uq, form="nt", name="mix1_dcq")
    dw_ukv = _matmul(c_kv, dkv_raw, form="tn", name="mix1_dwukv")
    dc_kv = _matmul(dkv_raw, w_ukv, form="nt", name="mix1_dckv")
    d_cq, dg_qa = _rms_bwd(down, q_a_full, dc_q, c0=0, width=Q_RANK, name="mix1_q_a_dnorm")
    d_ckv, dg_kva = _rms_bwd(down, kv_a_full, dc_kv, c0=Q_RANK, width=KV_RANK, name="mix1_kv_a_dnorm")
    d_down = jnp.concatenate([d_cq, d_ckv, dpe], axis=1)
    dw_dn = _matmul(h2, d_down, form="tn", name="mix1_dwdown")
    dh2 = _matmul(d_down, w_dn, form="nt", name="mix1_dh")
    dx2, dx2_b, dg_mix1 = _rms_bwd(x2, ln_mix_g[1], dh2, res=dx3, bf16_copy=True, name="mix1_dnorm")
    g_uq = dw_uq.reshape(Q_RANK, N_MLA, QK_PAD)[:, :, :QK_DIM].reshape(Q_RANK, N_MLA * QK_DIM)
    tags_mix1 = ["w_dn", "w_uq", "w_ukv", "w_o1"]
    g_mix1 = [halves(by_rows(dw_dn[:, :dn_w])), halves(by_cols(g_uq)), halves(by_cols(dw_ukv)),
              halves(by_rows(dw_o1))]

    p_mix1, mine_mlp1 = [], []

    def scatter_mix1(landed):
        parts, ex = scatter_of(g_mix1, landed["dwdown"][1], tags_mix1)
        p_mix1.extend(parts)
        return ex

    def join_mlp1(landed):
        mine_mlp1.extend(sums_of(sl_up1 + landed["dwdown"][0], p_mlp1, tags_mlp1))
        return _Exchange("join", mine_mlp1)

    dx1, dx1_b, dg_mlp0, dw_up0, dw_dw0, landed = _mlp_bwd(
        dx2, dx2_b, x1, ln_mlp_g[0], w_up[0], w_dw[0], mlp0, "mlp0",
        comms={"dwdown": [_Exchange("scatter", p_mlp1[1:]), _Exchange("swap", g_mix1)],
               "du": scatter_mix1, "dwup": join_mlp1})
    gs_up1, gs_dw1 = shards_of(landed["dwup"], mine_mlp1)
    mine_mix1 = sums_of(landed["du"], p_mix1, tags_mix1)
    tags_mlp0 = ["w_up0", "w_dw0"]
    g_mlp0 = [halves(dw_up0), halves(by_rows(dw_dw0))]

    dw_o0, joined = _matmul(o0, dx1_b, form="tn", name="mix0_dwo", comm=_Exchange("join", mine_mix1))
    gs_dn, gs_uq, gs_ukv, gs_o1 = shards_of(joined, mine_mix1)
    g_o0 = [halves(by_rows(dw_o0))]
    do0 = _matmul(dx1_b, w_o0, form="nt", name="mix0_do")
    (dq_f, dk_f, dv_fx, dbias, drow), (fs_mlp0, fs_o0) = _attn_bwd(
        q_f, k_f, v_fx, neg_f, o_fx, lse0, do0, n_heads=N_FOX, dqk=HEAD, scale=HEAD ** -0.5, do_off=N_SB,
        name="fox_bwd", comm=[_Exchange("swap", g_mlp0), _Exchange("swap", g_o0)])
    p_mlp0, ex_a = scatter_of(g_mlp0, fs_mlp0, tags_mlp0)
    p_o0, ex_b = scatter_of(g_o0, fs_o0, ["w_o0"])
    (dq_sb, dk_sb, dv_sb), (sl_mlp0, sl_o0) = _sb_bwd(qkv_sb, do0, t_sb, do_off=0, name="sb_bwd", comm=[ex_a, ex_b])
    mine_mlp0 = sums_of(sl_mlp0, p_mlp0, tags_mlp0)
    mine_o0 = sums_of(sl_o0, p_o0, ["w_o0"])
    dq_fx, dg_fq = _rms_bwd(qk_fx, fox_q_g[0], dq_f, c0=0, width=N_FOX * HEAD, gw=HEAD, name="fox_q_dnorm")
    dk_fx, dg_fk = _rms_bwd(qk_fx, fox_k_g[0], dk_f, c0=N_FOX * HEAD, width=N_FOX * HEAD, gw=HEAD,
                            name="fox_k_dnorm")
    d_fcum = jnp.pad((jnp.max(drow, axis=-1) - dbias.reshape(N_FOX, S)).T, ((0, 0), (0, LANES - N_FOX)))
    dfl, db_f = _forget_bwd(fl, b_pad, d_fcum, name="forget_bwd")
    dproj = jnp.concatenate([dq_sb, dk_sb, dv_sb, dq_fx, dk_fx, dv_fx], axis=1).astype(BF16)
    dw_qkv, (j_mlp0, j_o0) = _matmul(h0, dproj, form="tn", name="mix0_dwqkv",
                                     comm=[_Exchange("join", mine_mlp0), _Exchange("join", mine_o0)])
    gs_up0, gs_dw0 = shards_of(j_mlp0, mine_mlp0)
    gs_o0, = shards_of(j_o0, mine_o0)
    dw_f = _matmul(h0, dfl, form="tn", name="mix0_dwf")
    g_in = [halves(by_cols(jnp.concatenate([dw_qkv, dw_f[:, :in_w - qkv_w]], axis=1)))]
    p_in, ex = scatter_of(g_in, _run_exchange(_Exchange("swap", g_in), name="reduce_w_in_swap"), ["w_in"])
    dh0 = _matmul(dfl, w_f, form="nt", name="mix0_dh_f")
    dh0, slots = _matmul(dproj, w_qkv, form="nt", epilogue="res", res=dh0, name="mix0_dh", comm=ex)
    mine_in = sums_of(slots, p_in, ["w_in"])
    gs_in, = shards_of(_run_exchange(_Exchange("join", mine_in), name="reduce_w_in_join"), mine_in)
    grad_x, dg_mix0 = _rms_bwd(xs, ln_mix_g[0], dh0, res=dx1, name="mix0_dnorm")
    gs_up = jnp.concatenate([gs_up0, gs_up1], axis=0)
    gs_dw = jnp.concatenate([gs_dw0, gs_dw1], axis=0)

    ln_rows = D // LANES
    small = jnp.concatenate([
        _rows(dg_mix0, ln_rows), _rows(dg_mix1, ln_rows), _rows(dg_mlp0, ln_rows), _rows(dg_mlp1, ln_rows),
        _rows(db_f, 8), _rows(dg_fq, 8), _rows(dg_fk, 8), _rows(dg_qa, 8), _rows(dg_kva, 8), _rows(dg_q, 8),
        _rows(dg_k, 8)], axis=0)
    small = _all_reduce_small(small, name="reduce_small")
    flat = lambda r0, nr, n: small[r0:r0 + nr].reshape(-1)[:n]
    r0 = 4 * ln_rows
    g_ln_mix = jnp.stack([flat(0, ln_rows, D), flat(ln_rows, ln_rows, D)])
    g_ln_mlp = jnp.stack([flat(2 * ln_rows, ln_rows, D), flat(3 * ln_rows, ln_rows, D)])
    g_b_f = flat(r0, 8, N_FOX)[None]
    g_fq, g_fk = flat(r0 + 8, 8, HEAD)[None], flat(r0 + 16, 8, HEAD)[None]
    g_qa = lax.dynamic_slice(flat(r0 + 24, 8, Q_RANK), (chip * LANES,), (LANES,))[None]
    g_kva = lax.dynamic_slice(flat(r0 + 32, 8, KV_RANK), (chip * LANES,), (LANES,))[None]
    g_q, g_k = flat(r0 + 40, 8, QK_DIM)[None], flat(r0 + 48, 8, QK_DIM)[None]

    def pack_small(ln_mix, ln_mlp, *rest):
        return jnp.concatenate([_rows(ln_mix, 2 * ln_rows), _rows(ln_mlp, 2 * ln_rows)] + [_rows(t, 8) for t in rest],
                               axis=0)

    def unpack_small(p):
        f = lambda r, nr, shape: p[r:r + nr].reshape(-1)[:int(np.prod(shape))].reshape(shape)
        shapes = [(1, N_FOX), (1, HEAD), (1, HEAD), (1, LANES), (1, LANES), (1, QK_DIM), (1, QK_DIM)]
        return (f(0, 2 * ln_rows, (2, D)), f(2 * ln_rows, 2 * ln_rows, (2, D)),
                *[f(r0 + 8 * i, 8, shp) for i, shp in enumerate(shapes)])

    small_out = _adamw(
        pack_small(ln_mix_g, ln_mlp_g, sf_b_f, fox_q_g, fox_k_g, mla_q_a_g, mla_kv_a_g, mla_q_g, mla_k_g)[None],
        pack_small(g_ln_mix, g_ln_mlp, g_b_f, g_fq, g_fk, g_qa, g_kva, g_q, g_k)[None],
        pack_small(m_ln_mix_g, m_ln_mlp_g, m_sf_b_f, m_fox_q_g, m_fox_k_g, m_mla_q_a_g, m_mla_kv_a_g, m_mla_q_g,
                   m_mla_k_g)[None],
        pack_small(v_ln_mix_g, v_ln_mlp_g, v_sf_b_f, v_fox_q_g, v_fox_k_g, v_mla_q_a_g, v_mla_kv_a_g, v_mla_q_g,
                   v_mla_k_g)[None], name="adamw_small")
    d_small, m_small, v_small = [unpack_small(p[0]) for p in small_out]

    def big(w, g, m, v, tag, comm=None):
        g = g.reshape(w.shape)
        out = _adamw(w, g, m, v, name=f"adamw_{tag}", comm=comm)
        (d, mn, vn), landed = out if comm is not None else (out, None)
        return (g, d, mn, vn) if comm is None else ((g, d, mn, vn), landed)

    r_up = big(mlp_w_up, gs_up, m_mlp_w_up, v_mlp_w_up, "w_up")
    r_dw = big(mlp_w_down, gs_dw, m_mlp_w_down, v_mlp_w_down, "w_dw")
    r_in = big(sf_w_in, gs_in, m_sf_w_in, v_sf_w_in, "w_in")
    r_o0 = big(sf_w_o, gs_o0, m_sf_w_o, v_sf_w_o, "w_o0")
    r_dn = big(mla_w_down, gs_dn, m_mla_w_down, v_mla_w_down, "w_dn")
    r_uq = big(mla_w_uq, gs_uq, m_mla_w_uq, v_mla_w_uq, "w_uq")
    r_ukv = big(mla_w_ukv, gs_ukv, m_mla_w_ukv, v_mla_w_ukv, "w_ukv")
    r_o1 = big(mla_w_o, gs_o1, m_mla_w_o, v_mla_w_o, "w_o1")

    g_small = (g_ln_mix, g_ln_mlp, g_b_f, g_fq, g_fk, g_qa, g_kva, g_q, g_k)

    def ordered(k, sm):
        return (sm[0], sm[1], r_in[k], sm[2], sm[3], sm[4], r_o0[k], r_dn[k], sm[5], sm[6], r_uq[k], r_ukv[k],
                sm[7], sm[8], r_o1[k], r_up[k], r_dw[k])

    return (loss, grad_x[None], *ordered(0, g_small), *ordered(1, d_small), *ordered(2, m_small),
            *ordered(3, v_small))
```

```python
import functools

import numpy as np
import jax
import jax.numpy as jnp
from jax import lax
from jax.experimental import pallas as pl
from jax.experimental.pallas import tpu as pltpu

F32 = jnp.float32
BF16 = jnp.bfloat16
MESH = pl.DeviceIdType.MESH

EPS = 1e-6
HEAD = 128
N_SB = 8
N_FOX = 8
N_MLA = 16
Q_RANK = 512
KV_RANK = 512
NOPE = 128
ROPE = 64
QK_DIM = NOPE + ROPE
QK_PAD = 256
ROPE_THETA = 10000.0
N_CHIPS = 4

ADAM_LR = 0.001
ADAM_B1 = 0.9
ADAM_B2 = 0.999
ADAM_EPS = 1e-08
ADAM_WD = 0.01
ADAM_STEP = 10

VMEM_LIMIT = 56 * 1024 * 1024
LANES = 128
NEG = -1e30


def _cp(*sem):
    return pltpu.CompilerParams(dimension_semantics=sem, vmem_limit_bytes=VMEM_LIMIT)


def _pick(dim, target):
    if dim <= target:
        return dim
    t = (target // LANES) * LANES
    while t >= LANES:
        if dim % t == 0:
            return t
        t -= LANES
    raise ValueError(f"no tile for {dim}")


NT_DIMS = (((1,), (1,)), ((), ()))
TN_DIMS = (((0,), (0,)), ((), ()))


def _dot(a, b):
    return jnp.dot(a, b, preferred_element_type=F32)


def _dot_nt(a, b):
    return lax.dot_general(a, b, NT_DIMS, preferred_element_type=F32)


def _dot_tn(a, b):
    return lax.dot_general(a, b, TN_DIMS, preferred_element_type=F32)


def _matmul(a, b, *, name, form="nn", out_dtype=F32, n=None, b_n0=0, b_split=False,
            out_split=False, epilogue="plain", res=None, u=None, tm=1024, tn=1024, tk=2048, comm=None):
    if form == "tn":
        K, M = a.shape
    else:
        M, K = a.shape
    if b_split:
        if form == "nt":
            nb_full, kb_full = b.shape[1], b.shape[2] * N_CHIPS
        else:
            kb_full, nb_full = b.shape[1], b.shape[2] * N_CHIPS
    elif form == "nt":
        nb_full, kb_full = b.shape
    else:
        kb_full, nb_full = b.shape
    assert kb_full == K, (name, a.shape, b.shape)
    N = nb_full if n is None else n
    if a.dtype != BF16 or b.dtype != BF16:
        tk = max(tk // 2, LANES)
    tm, tn, tk = _pick(M, tm), _pick(N, tn), _pick(K, tk)
    if b_split:
        per_chip = (b.shape[2])
        if form == "nt":
            tk = _pick(per_chip, tk)
        else:
            tn = _pick(per_chip, tn)
    if out_split:
        tn = _pick(N // N_CHIPS, tn)
    assert b_n0 % tn == 0
    nb0 = b_n0 // tn
    nk = K // tk
    grid = (M // tm, N // tn, nk)

    if form == "tn":
        a_spec = pl.BlockSpec((tk, tm), lambda i, j, k: (k, i))
    else:
        a_spec = pl.BlockSpec((tm, tk), lambda i, j, k: (i, k))
    if b_split:
        if form == "nt":
            kc = b.shape[2] // tk
            b_spec = pl.BlockSpec((None, tn, tk), lambda i, j, k: (k // kc, j, k % kc))
        else:
            nc = b.shape[2] // tn
            b_spec = pl.BlockSpec((None, tk, tn), lambda i, j, k: (j // nc, k, j % nc))
    elif form == "nt":
        b_spec = pl.BlockSpec((tn, tk), lambda i, j, k: (j + nb0, k))
    else:
        b_spec = pl.BlockSpec((tk, tn), lambda i, j, k: (k, j + nb0))
    mn_spec = pl.BlockSpec((tm, tn), lambda i, j, k: (i, j))
    if out_split:
        oc = (N // N_CHIPS) // tn
        out_spec = pl.BlockSpec((None, tm, tn), lambda i, j, k: (j // oc, i, j % oc))
        out_shape = jax.ShapeDtypeStruct((N_CHIPS, M, N // N_CHIPS), out_dtype)
    else:
        out_spec = mn_spec
        out_shape = jax.ShapeDtypeStruct((M, N), out_dtype)

    in_specs = [a_spec, b_spec]
    operands = [a, b]
    out_specs = (out_spec,)
    out_shape = (out_shape,)
    if epilogue == "res":
        in_specs.append(mn_spec)
        operands.append(res)
    elif epilogue == "sqrelu_bwd":
        in_specs.append(mn_spec)
        operands.append(u)
    elif epilogue == "sqrelu":
        out_specs = (mn_spec, mn_spec)
        out_shape = (jax.ShapeDtypeStruct((M, N), F32), jax.ShapeDtypeStruct((M, N), BF16))

    def finish(refs, r):
        if epilogue == "plain":
            refs[2][...] = r.astype(out_dtype)
        elif epilogue == "res":
            refs[3][...] = (refs[2][...] + r).astype(out_dtype)
        elif epilogue == "sqrelu":
            refs[2][...] = r
            p = jnp.maximum(r, 0.0)
            refs[3][...] = (p * p).astype(BF16)
        else:
            refs[3][...] = (r * (2.0 * jnp.maximum(refs[2][...], 0.0))).astype(out_dtype)

    def body(*refs):
        at = refs[0][...].astype(BF16)
        bt = refs[1][...].astype(BF16)
        if form == "nn":
            part = _dot(at, bt)
        elif form == "nt":
            part = _dot_nt(at, bt)
        else:
            part = _dot_tn(at, bt)
        if nk == 1:
            finish(refs, part)
            return
        acc = refs[-1]
        k = pl.program_id(2)

        @pl.when(k == 0)
        def _():
            acc[...] = part

        @pl.when(jnp.logical_and(k > 0, k < nk - 1))
        def _():
            acc[...] += part

        @pl.when(k == nk - 1)
        def _():
            finish(refs, acc[...] + part)

    outs, comm_outs = _hosted_call(
        body, grid=grid, in_specs=in_specs, out_specs=out_specs, out_shape=out_shape,
        scratch_shapes=[] if nk == 1 else [pltpu.VMEM((tm, tn), F32)], operands=operands, name=name,
        sem=("parallel", "parallel", "arbitrary"), comm=comm)
    result = outs if epilogue == "sqrelu" else outs[0]
    return result if comm is None else (result, comm_outs)


def _rms_fwd(x, g, *, name, c0=0, width=None, gw=None, tr=256):
    R, ctot = x.shape
    C = ctot if width is None else width
    gw = C if gw is None else gw
    assert c0 % C == 0 and C % gw == 0
    tr = _pick(R, tr)
    cb = c0 // C
    ng = C // gw

    def body(x_ref, g_ref, o_ref):
        gv = g_ref[...]
        for gi in range(ng):
            cols = slice(gi * gw, (gi + 1) * gw)
            xs = x_ref[:, cols]
            ms = jnp.sum(xs * xs, axis=-1, keepdims=True) * (1.0 / gw)
            o_ref[:, cols] = ((xs * lax.rsqrt(ms + EPS)) * gv).astype(o_ref.dtype)

    return pl.pallas_call(
        body, out_shape=jax.ShapeDtypeStruct((R, C), BF16), grid=(R // tr,),
        in_specs=[pl.BlockSpec((tr, C), lambda i: (i, cb)), pl.BlockSpec((1, gw), lambda i: (0, 0))],
        out_specs=pl.BlockSpec((tr, C), lambda i: (i, 0)), name=name,
        compiler_params=_cp("parallel"))(x, g.reshape(1, gw).astype(F32))


def _rms_bwd(x, g, dy, *, name, res=None, c0=0, width=None, gw=None, tr=256, bf16_copy=False):
    bf16_copy = int(bf16_copy)
    R, ctot = x.shape
    C = ctot if width is None else width
    gw = C if gw is None else gw
    tr = _pick(R, tr)
    cb = c0 // C
    ng = C // gw
    nsteps = R // tr
    row_spec = pl.BlockSpec((tr, C), lambda i: (i, 0))
    in_specs = [pl.BlockSpec((tr, C), lambda i: (i, cb)), pl.BlockSpec((1, gw), lambda i: (0, 0)), row_spec]
    operands = [x, g.reshape(1, gw).astype(F32), dy]
    if res is not None:
        in_specs.append(row_spec)
        operands.append(res)

    def body(*refs):
        x_ref, g_ref, dy_ref = refs[:3]
        res_ref = refs[3] if res is not None else None
        dx_ref, dg_ref = refs[-3 - bf16_copy], refs[-2]
        acc = refs[-1]
        i = pl.program_id(0)

        @pl.when(i == 0)
        def _():
            acc[...] = jnp.zeros_like(acc)

        gv = g_ref[...]
        for gi in range(ng):
            cols = slice(gi * gw, (gi + 1) * gw)
            xs = x_ref[:, cols]
            dys = dy_ref[:, cols].astype(F32)
            rstd = lax.rsqrt(jnp.sum(xs * xs, axis=-1, keepdims=True) * (1.0 / gw) + EPS)
            xh = xs * rstd
            gdy = dys * gv
            m = jnp.sum(gdy * xh, axis=-1, keepdims=True) * (1.0 / gw)
            dx = rstd * (gdy - xh * m)
            if res_ref is not None:
                dx = dx + res_ref[:, cols]
            dx_ref[:, cols] = dx
            if bf16_copy:
                refs[-3][:, cols] = dx.astype(BF16)
            acc[...] += jnp.sum((dys * xh).reshape(tr // 8, 8, gw), axis=0)

        @pl.when(i == nsteps - 1)
        def _():
            dg_ref[...] = jnp.sum(acc[...], axis=0, keepdims=True)

    out_shape = [jax.ShapeDtypeStruct((R, C), F32)] + [jax.ShapeDtypeStruct((R, C), BF16)] * bf16_copy
    outs = pl.pallas_call(
        body, out_shape=tuple(out_shape + [jax.ShapeDtypeStruct((1, gw), F32)]),
        grid=(nsteps,), in_specs=in_specs,
        out_specs=tuple([row_spec] * len(out_shape) + [pl.BlockSpec((1, gw), lambda i: (0, 0))]),
        scratch_shapes=[pltpu.VMEM((8, gw), F32)], name=name,
        compiler_params=_cp("arbitrary"))(*operands)
    return (*outs[:-1], outs[-1][0])


def _split3(x):
    hi = x.astype(BF16)
    r1 = x - hi.astype(F32)
    mid = r1.astype(BF16)
    lo = (r1 - mid.astype(F32)).astype(BF16)
    return hi, mid, lo


def _log_sigmoid(z):
    return jnp.minimum(z, 0.0) - jnp.log(1.0 + jnp.exp(-jnp.abs(z)))


def _forget_fwd(fl, b, *, name, tb=512):
    S = fl.shape[0]
    tb = _pick(S, tb)

    def body(fl_ref, b_ref, f_ref, carry):
        i = pl.program_id(0)

        @pl.when(i == 0)
        def _():
            carry[...] = jnp.zeros_like(carry)

        lf = _log_sigmoid(fl_ref[...] + b_ref[...])
        r = lax.broadcasted_iota(jnp.int32, (tb, tb), 0)
        c = lax.broadcasted_iota(jnp.int32, (tb, tb), 1)
        tri = (c <= r).astype(BF16)
        hi, mid, lo = _split3(lf)
        cs = _dot(tri, hi) + _dot(tri, mid) + _dot(tri, lo)
        f_ref[...] = cs + carry[...]
        carry[...] += jnp.sum(lf, axis=0, keepdims=True)

    return pl.pallas_call(
        body, out_shape=jax.ShapeDtypeStruct((S, LANES), F32), grid=(S // tb,),
        in_specs=[pl.BlockSpec((tb, LANES), lambda i: (i, 0)), pl.BlockSpec((1, LANES), lambda i: (0, 0))],
        out_specs=pl.BlockSpec((tb, LANES), lambda i: (i, 0)),
        scratch_shapes=[pltpu.VMEM((1, LANES), F32)], name=name,
        compiler_params=_cp("arbitrary"))(fl, b)


def _forget_bwd(fl, b, dF, *, name, tb=512):
    S = fl.shape[0]
    tb = _pick(S, tb)
    nb = S // tb

    def body(fl_ref, b_ref, df_ref, dfl_ref, db_ref, carry, acc):
        i = pl.program_id(0)

        @pl.when(i == 0)
        def _():
            carry[...] = jnp.zeros_like(carry)
            acc[...] = jnp.zeros_like(acc)

        d = df_ref[...]
        r = lax.broadcasted_iota(jnp.int32, (tb, tb), 0)
        c = lax.broadcasted_iota(jnp.int32, (tb, tb), 1)
        tri = (c >= r).astype(BF16)
        hi, mid, lo = _split3(d)
        rc = _dot(tri, hi) + _dot(tri, mid) + _dot(tri, lo) + carry[...]
        z = fl_ref[...] + b_ref[...]
        dfl = rc * jnp.exp(_log_sigmoid(-z))
        dfl_ref[...] = dfl
        carry[...] += jnp.sum(d, axis=0, keepdims=True)
        acc[...] += jnp.sum(dfl, axis=0, keepdims=True)

        @pl.when(i == nb - 1)
        def _():
            db_ref[...] = acc[...]

    rev = lambda i: (nb - 1 - i, 0)
    dfl, db = pl.pallas_call(
        body, out_shape=(jax.ShapeDtypeStruct((S, LANES), F32), jax.ShapeDtypeStruct((1, LANES), F32)),
        grid=(nb,),
        in_specs=[pl.BlockSpec((tb, LANES), rev), pl.BlockSpec((1, LANES), lambda i: (0, 0)),
                  pl.BlockSpec((tb, LANES), rev)],
        out_specs=(pl.BlockSpec((tb, LANES), rev), pl.BlockSpec((1, LANES), lambda i: (0, 0))),
        scratch_shapes=[pltpu.VMEM((1, LANES), F32), pltpu.VMEM((1, LANES), F32)], name=name,
        compiler_params=_cp("arbitrary"))(fl, b, dF)
    return dfl, db[0]


def _tri(tk, rel):
    r = lax.broadcasted_iota(jnp.int32, (tk, tk), 0)
    c = lax.broadcasted_iota(jnp.int32, (tk, tk), 1)
    m = {"gt": r > c, "le": r <= c, "lt": r < c}[rel]
    return m.astype(BF16)


def _split2(x):
    hi = x.astype(BF16)
    return hi, (x - hi.astype(F32)).astype(BF16)


HEADS_PER_STEP = 2
CUM_CHUNK = 256


def _cum_cols(x, tri, suffix):
    ck = tri.shape[0]
    n = x.shape[1] // ck
    hi, lo = _split2(x)
    parts, sums = [], []
    for c in range(n):
        cs = slice(c * ck, (c + 1) * ck)
        parts.append(_dot(hi[:, cs], tri) + _dot(lo[:, cs], tri))
        sums.append(jnp.sum(x[:, cs], axis=1, keepdims=True))
    carry = None
    for c in (reversed(range(n)) if suffix else range(n)):
        if carry is not None:
            parts[c] = parts[c] + carry
        carry = sums[c] if carry is None else carry + sums[c]
    return (parts[0] if n == 1 else jnp.concatenate(parts, axis=1)), carry


def _diag_mask(tq, strict):
    r = lax.broadcasted_iota(jnp.int32, (tq, tq), 0)
    c = lax.broadcasted_iota(jnp.int32, (tq, tq), 1)
    return c < r if strict else c <= r


def _sb_fwd(qkv, *, name, n_heads=N_SB, q_off=0, k_off=N_SB, v_off=2 * N_SB, tq=512, hp=HEADS_PER_STEP,
            comm=None):
    S = qkv.shape[0]
    tq = _pick(S, tq)
    tk = tq
    scale = HEAD ** -0.5
    nq = S // tq
    assert n_heads % hp == 0 and q_off % hp == 0 and k_off % hp == 0 and v_off % hp == 0

    def body(q_ref, k_ref, v_ref, o_ref, t_ref, c_sc, acc_sc):
        qi = pl.program_id(1)
        c_sc[...] = jnp.zeros_like(c_sc)
        acc_sc[...] = jnp.zeros_like(acc_sc)
        gt = _tri(min(CUM_CHUNK, tk), "gt")

        def tile(hh, j, diag):
            cs = slice(hh * HEAD, (hh + 1) * HEAD)
            rows = pl.ds(pl.multiple_of(j * tk, tk), tk)
            z = _dot_nt(q_ref[:, cs], k_ref[rows, cs]) * scale
            sp = jnp.log(1.0 + jnp.exp(-jnp.abs(z)))
            la = jnp.minimum(z, 0.0) - sp
            lb = -jnp.maximum(z, 0.0) - sp
            if diag:
                strict = _diag_mask(tq, True)
                lb = jnp.where(strict, lb, 0.0)
            suffix, total = _cum_cols(lb, gt, suffix=True)
            w = jnp.exp(la + suffix + c_sc[hh])
            if diag:
                w = jnp.where(strict, w, 0.0)
            acc_sc[hh] += _dot(w.astype(BF16), v_ref[rows, cs])
            c_sc[hh] += total

        for hh in range(hp):
            tile(hh, qi, True)

        def step(it, carry):
            for hh in range(hp):
                tile(hh, qi - 1 - it, False)
            return carry

        lax.fori_loop(0, qi, step, 0)
        for hh in range(hp):
            o_ref[:, hh * HEAD:(hh + 1) * HEAD] = acc_sc[hh]
            t_ref[hh] = jnp.broadcast_to(c_sc[hh], (tq, LANES))

    w = hp * HEAD
    head_blk = lambda off: pl.BlockSpec((S, w), lambda h, i: (0, h + off // hp))
    outs, comm_outs = _hosted_call(
        body,
        out_shape=(jax.ShapeDtypeStruct((S, n_heads * HEAD), F32),
                   jax.ShapeDtypeStruct((n_heads, S, LANES), F32)),
        grid=(n_heads // hp, nq),
        in_specs=[pl.BlockSpec((tq, w), lambda h, i: (i, h + q_off // hp)), head_blk(k_off), head_blk(v_off)],
        out_specs=(pl.BlockSpec((tq, w), lambda h, i: (i, h)),
                   pl.BlockSpec((hp, tq, LANES), lambda h, i: (h, i, 0))),
        scratch_shapes=[pltpu.VMEM((hp, tq, 1), F32), pltpu.VMEM((hp, tq, HEAD), F32)], name=name,
        sem=("parallel", "arbitrary"), operands=(qkv, qkv, qkv), comm=comm)
    return outs if comm is None else (outs, comm_outs)


def _sb_bwd(qkv, do, tstat, *, name, n_heads=N_SB, q_off=0, k_off=N_SB, v_off=2 * N_SB, do_off=0, tq=512,
            hp=HEADS_PER_STEP, comm=None):
    S = qkv.shape[0]
    tq = _pick(S, tq)
    tk = tq
    scale = HEAD ** -0.5
    nq = S // tq
    assert n_heads % hp == 0 and q_off % hp == 0 and k_off % hp == 0 and v_off % hp == 0 and do_off % hp == 0

    def body(q_ref, k_ref, v_ref, do_ref, t_ref, dq_ref, dk_ref, dv_ref, p_sc, r_sc, dq_sc):
        qi = pl.program_id(1)

        @pl.when(qi == 0)
        def _():
            dk_ref[...] = jnp.zeros_like(dk_ref)
            dv_ref[...] = jnp.zeros_like(dv_ref)

        p_sc[...] = jnp.zeros_like(p_sc)
        r_sc[...] = jnp.zeros_like(r_sc)
        dq_sc[...] = jnp.zeros_like(dq_sc)
        le = _tri(min(CUM_CHUNK, tk), "le")
        lt = _tri(min(CUM_CHUNK, tk), "lt")

        def tile(hh, j, diag):
            cs = slice(hh * HEAD, (hh + 1) * HEAD)
            rows = pl.ds(pl.multiple_of(j * tk, tk), tk)
            q = q_ref[:, cs]
            do_b = do_ref[:, cs].astype(BF16)
            kb = k_ref[rows, cs]
            z = _dot_nt(q, kb) * scale
            sp = jnp.log(1.0 + jnp.exp(-jnp.abs(z)))
            la = jnp.minimum(z, 0.0) - sp
            lb = -jnp.maximum(z, 0.0) - sp
            if diag:
                strict = _diag_mask(tq, True)
                lb = jnp.where(strict, lb, 0.0)
            prefix, total_b = _cum_cols(lb, le, suffix=False)
            w = jnp.exp(la + t_ref[hh, :, 0:1] - (prefix + p_sc[hh]))
            if diag:
                w = jnp.where(strict, w, 0.0)
            r = w * _dot_nt(do_b, v_ref[rows, cs])
            rex, total_r = _cum_cols(r, lt, suffix=False)
            rex = rex + r_sc[hh]
            beta = jnp.exp(la)
            dz = r - beta * (r + rex)
            if diag:
                dz = jnp.where(strict, dz, 0.0)
            dzb = dz.astype(BF16)
            dq_sc[hh] += _dot(dzb, kb)
            dk_ref[rows, cs] += _dot_tn(dzb, q)
            dv_ref[rows, cs] += _dot_tn(w.astype(BF16), do_b)
            p_sc[hh] += total_b
            r_sc[hh] += total_r

        def step(j, carry):
            for hh in range(hp):
                tile(hh, j, False)
            return carry

        lax.fori_loop(0, qi, step, 0)
        for hh in range(hp):
            tile(hh, qi, True)
            dq_ref[:, hh * HEAD:(hh + 1) * HEAD] = dq_sc[hh] * scale

        @pl.when(qi == nq - 1)
        def _():
            dk_ref[...] = dk_ref[...] * scale

    w = hp * HEAD
    head_blk = lambda off: pl.BlockSpec((S, w), lambda h, i: (0, h + off // hp))
    out_head = pl.BlockSpec((S, w), lambda h, i: (0, h))
    out_sd = jax.ShapeDtypeStruct((S, n_heads * HEAD), F32)
    outs, comm_outs = _hosted_call(
        body, out_shape=(out_sd, out_sd, out_sd), grid=(n_heads // hp, nq),
        in_specs=[pl.BlockSpec((tq, w), lambda h, i: (i, h + q_off // hp)), head_blk(k_off), head_blk(v_off),
                  pl.BlockSpec((tq, w), lambda h, i: (i, h + do_off // hp)),
                  pl.BlockSpec((hp, tq, LANES), lambda h, i: (h, i, 0))],
        out_specs=(pl.BlockSpec((tq, w), lambda h, i: (i, h)), out_head, out_head),
        scratch_shapes=[pltpu.VMEM((hp, tq, 1), F32), pltpu.VMEM((hp, tq, 1), F32), pltpu.VMEM((hp, tq, HEAD), F32)],
        name=name, sem=("parallel", "arbitrary"), operands=(qkv, qkv, qkv, do, tstat), comm=comm)
    return outs if comm is None else (outs, comm_outs)


def _attn_fwd(q, k, v, bias, *, name, n_heads, dqk, scale, v_off=0, tq=1024, exact_p=False, hp=HEADS_PER_STEP,
              comm=None):
    S = q.shape[0]
    tq = _pick(S, tq)
    tk = tq
    nq = S // tq
    has_bias = bias is not None

    assert n_heads % hp == 0 and v_off % hp == 0

    def body(*refs):
        q_ref, k_ref, v_ref = refs[:3]
        b_ref = refs[3] if has_bias else None
        o_ref, lse_ref, m_sc, l_sc, acc_sc = refs[-5:]
        qi = pl.program_id(1)
        m_sc[...] = jnp.full_like(m_sc, NEG)
        l_sc[...] = jnp.zeros_like(l_sc)
        acc_sc[...] = jnp.zeros_like(acc_sc)

        def tile(hh, j, diag):
            rows = pl.ds(pl.multiple_of(j * tk, tk), tk)
            s = _dot_nt(q_ref[:, hh * dqk:(hh + 1) * dqk], k_ref[rows, hh * dqk:(hh + 1) * dqk]) * scale
            if has_bias:
                s = s + b_ref[hh, :, rows]
            if diag:
                s = jnp.where(_diag_mask(tq, False), s, NEG)
            m_old = m_sc[hh]
            m_new = jnp.maximum(m_old, jnp.max(s, axis=1, keepdims=True))
            alpha = jnp.exp(m_old - m_new)
            p = jnp.exp(s - m_new)
            l_sc[hh] = alpha * l_sc[hh] + jnp.sum(p, axis=1, keepdims=True)
            vb = v_ref[rows, hh * HEAD:(hh + 1) * HEAD]
            if exact_p:
                hi, lo = _split2(p)
                pv = _dot(hi, vb) + _dot(lo, vb)
            else:
                pv = _dot(p.astype(BF16), vb)
            acc_sc[hh] = alpha * acc_sc[hh] + pv
            m_sc[hh] = m_new

        def step(j, carry):
            for hh in range(hp):
                tile(hh, j, False)
            return carry

        lax.fori_loop(0, qi, step, 0)
        for hh in range(hp):
            tile(hh, qi, True)
            l = l_sc[hh]
            o_ref[:, hh * HEAD:(hh + 1) * HEAD] = acc_sc[hh] / l
            lse_ref[hh] = jnp.broadcast_to(m_sc[hh] + jnp.log(l), (tq, LANES))

    in_specs = [pl.BlockSpec((tq, hp * dqk), lambda h, i: (i, h)),
                pl.BlockSpec((S, hp * dqk), lambda h, i: (0, h)),
                pl.BlockSpec((S, hp * HEAD), lambda h, i: (0, h + v_off // hp))]
    operands = [q, k, v]
    if has_bias:
        in_specs.append(pl.BlockSpec((hp, 1, S), lambda h, i: (h, 0, 0)))
        operands.append(bias)
    outs, comm_outs = _hosted_call(
        body,
        out_shape=(jax.ShapeDtypeStruct((S, n_heads * HEAD), F32),
                   jax.ShapeDtypeStruct((n_heads, S, LANES), F32)),
        grid=(n_heads // hp, nq), in_specs=in_specs,
        out_specs=(pl.BlockSpec((tq, hp * HEAD), lambda h, i: (i, h)),
                   pl.BlockSpec((hp, tq, LANES), lambda h, i: (h, i, 0))),
        scratch_shapes=[pltpu.VMEM((hp, tq, 1), F32), pltpu.VMEM((hp, tq, 1), F32), pltpu.VMEM((hp, tq, HEAD), F32)],
        name=name, sem=("parallel", "arbitrary"), operands=operands, comm=comm)
    return outs if comm is None else (outs, comm_outs)


def _attn_bwd(q, k, v, bias, o, lse, do, *, name, n_heads, dqk, scale, v_off=0, do_off=0, tq=512,
              hp=HEADS_PER_STEP, comm=None):
    S = q.shape[0]
    tq = _pick(S, tq)
    tk = tq
    nq = S // tq
    has_bias = bias is not None
    assert n_heads % hp == 0 and v_off % hp == 0 and do_off % hp == 0

    def body(*refs):
        q_ref, k_ref, v_ref, o_ref, lse_ref, do_ref = refs[:6]
        b_ref = refs[6] if has_bias else None
        n_out = 5 if has_bias else 3
        outs = refs[-(n_out + 3):-3]
        dq_ref, dk_ref, dv_ref = outs[:3]
        db_ref, dr_ref = (outs[3], outs[4]) if has_bias else (None, None)
        dq_sc, rs_sc, delta_sc = refs[-3:]
        qi = pl.program_id(1)

        @pl.when(qi == 0)
        def _():
            dk_ref[...] = jnp.zeros_like(dk_ref)
            dv_ref[...] = jnp.zeros_like(dv_ref)
            if has_bias:
                db_ref[...] = jnp.zeros_like(db_ref)

        dq_sc[...] = jnp.zeros_like(dq_sc)
        rs_sc[...] = jnp.zeros_like(rs_sc)
        for hh in range(hp):
            vs = slice(hh * HEAD, (hh + 1) * HEAD)
            do_r = do_ref[:, vs].astype(BF16).astype(F32)
            delta_sc[hh] = jnp.sum(do_r * o_ref[:, vs], axis=1, keepdims=True)

        def tile(hh, j, diag):
            qs = slice(hh * dqk, (hh + 1) * dqk)
            vs = slice(hh * HEAD, (hh + 1) * HEAD)
            rows = pl.ds(pl.multiple_of(j * tk, tk), tk)
            qb = q_ref[:, qs]
            do_b = do_ref[:, vs].astype(BF16)
            delta = delta_sc[hh]
            kb = k_ref[rows, qs]
            s = _dot_nt(qb, kb) * scale
            if has_bias:
                s = s + b_ref[hh, :, rows]
            p = jnp.exp(s - lse_ref[hh, :, 0:1])
            if diag:
                p = jnp.where(_diag_mask(tq, False), p, 0.0)
            ds = p * (_dot_nt(do_b, v_ref[rows, vs]) - delta)
            dsb = (ds * scale).astype(BF16)
            dq_sc[hh] += _dot(dsb, kb)
            dk_ref[rows, qs] += _dot_tn(dsb, qb)
            dv_ref[rows, vs] += _dot_tn(p.astype(BF16), do_b)
            if has_bias:
                db_ref[hh, :, rows] += jnp.sum(ds, axis=0, keepdims=True)
                rs_sc[hh] += jnp.sum(ds, axis=1, keepdims=True)

        def step(j, carry):
            for hh in range(hp):
                tile(hh, j, False)
            return carry

        lax.fori_loop(0, qi, step, 0)
        for hh in range(hp):
            tile(hh, qi, True)
            dq_ref[:, hh * dqk:(hh + 1) * dqk] = dq_sc[hh]
            if has_bias:
                dr_ref[hh] = jnp.broadcast_to(rs_sc[hh], (tq, LANES))

    stat = pl.BlockSpec((hp, tq, LANES), lambda h, i: (h, i, 0))
    in_specs = [pl.BlockSpec((tq, hp * dqk), lambda h, i: (i, h)),
                pl.BlockSpec((S, hp * dqk), lambda h, i: (0, h)),
                pl.BlockSpec((S, hp * HEAD), lambda h, i: (0, h + v_off // hp)),
                pl.BlockSpec((tq, hp * HEAD), lambda h, i: (i, h)),
                stat,
                pl.BlockSpec((tq, hp * HEAD), lambda h, i: (i, h + do_off // hp))]
    operands = [q, k, v, o, lse, do]
    out_shape = [jax.ShapeDtypeStruct((S, n_heads * dqk), F32), jax.ShapeDtypeStruct((S, n_heads * dqk), F32),
                 jax.ShapeDtypeStruct((S, n_heads * HEAD), F32)]
    out_specs = [pl.BlockSpec((tq, hp * dqk), lambda h, i: (i, h)), pl.BlockSpec((S, hp * dqk), lambda h, i: (0, h)),
                 pl.BlockSpec((S, hp * HEAD), lambda h, i: (0, h))]
    if has_bias:
        in_specs.append(pl.BlockSpec((hp, 1, S), lambda h, i: (h, 0, 0)))
        operands.append(bias)
        out_shape.append(jax.ShapeDtypeStruct((n_heads, 1, S), F32))
        out_specs.append(pl.BlockSpec((hp, 1, S), lambda h, i: (h, 0, 0)))
        out_shape.append(jax.ShapeDtypeStruct((n_heads, S, LANES), F32))
        out_specs.append(stat)
    outs, comm_outs = _hosted_call(
        body, out_shape=tuple(out_shape), grid=(n_heads // hp, nq), in_specs=in_specs, out_specs=tuple(out_specs),
        scratch_shapes=[pltpu.VMEM((hp, tq, dqk), F32), pltpu.VMEM((hp, tq, 1), F32),
                        pltpu.VMEM((hp, tq, 1), F32)], name=name,
        sem=("parallel", "arbitrary"), operands=operands, comm=comm)
    return outs if comm is None else (outs, comm_outs)


def _rot_half(y):
    lane = lax.broadcasted_iota(jnp.int32, y.shape, 1)
    up = pltpu.roll(y, 96, 1)
    down = pltpu.roll(y, 32, 1)
    return jnp.where(lane < 32, -up, jnp.where(lane < 64, down, 0.0))


def _mla_prep_fwd(q_raw, kv_raw, down, cos, sin, q_g, k_g, *, name, ts=128):
    S = q_raw.shape[0]
    ts = _pick(S, ts)
    pe_blk = Q_RANK // LANES + KV_RANK // LANES

    def norm_rope(x0, x1, g0, g1, c, s):
        ms = (jnp.sum(x0 * x0, axis=-1, keepdims=True) + jnp.sum(x1 * x1, axis=-1, keepdims=True)) * (1.0 / QK_DIM)
        rstd = lax.rsqrt(ms + EPS)
        y0 = (x0 * rstd) * g0
        y1 = (x1 * rstd) * g1
        return y0, y1 * c + _rot_half(y1) * s

    def body(q_ref, kv_ref, pe_ref, cos_ref, sin_ref, qg_ref, kg_ref, qo_ref, ko_ref, vo_ref):
        c, s = cos_ref[...], sin_ref[...]
        pe = pe_ref[...]
        qg0, qg1 = qg_ref[:, :NOPE], qg_ref[:, NOPE:]
        kg0, kg1 = kg_ref[:, :NOPE], kg_ref[:, NOPE:]
        for h in range(N_MLA):
            b = h * QK_PAD
            y0, y1 = norm_rope(q_ref[:, b:b + NOPE], q_ref[:, b + NOPE:b + QK_PAD], qg0, qg1, c, s)
            qo_ref[:, b:b + NOPE] = y0.astype(BF16)
            qo_ref[:, b + NOPE:b + QK_PAD] = y1.astype(BF16)
            y0, y1 = norm_rope(kv_ref[:, b:b + NOPE], pe, kg0, kg1, c, s)
            ko_ref[:, b:b + NOPE] = y0.astype(BF16)
            ko_ref[:, b + NOPE:b + QK_PAD] = y1.astype(BF16)
            vo_ref[:, h * HEAD:(h + 1) * HEAD] = kv_ref[:, b + NOPE:b + QK_PAD].astype(BF16)

    wide = pl.BlockSpec((ts, N_MLA * QK_PAD), lambda i: (i, 0))
    lane_blk = pl.BlockSpec((ts, LANES), lambda i: (i, 0))
    gain = pl.BlockSpec((1, QK_PAD), lambda i: (0, 0))
    return pl.pallas_call(
        body,
        out_shape=(jax.ShapeDtypeStruct((S, N_MLA * QK_PAD), BF16), jax.ShapeDtypeStruct((S, N_MLA * QK_PAD), BF16),
                   jax.ShapeDtypeStruct((S, N_MLA * HEAD), BF16)),
        grid=(S // ts,),
        in_specs=[wide, wide, pl.BlockSpec((ts, LANES), lambda i: (i, pe_blk)), lane_blk, lane_blk, gain, gain],
        out_specs=(wide, wide, pl.BlockSpec((ts, N_MLA * HEAD), lambda i: (i, 0))), name=name,
        compiler_params=_cp("parallel"))(q_raw, kv_raw, down, cos, sin, q_g, k_g)


def _mla_prep_bwd(dq, dk, dv, q_raw, kv_raw, down, cos, sin, q_g, k_g, *, name, ts=128, comm=None):
    S = q_raw.shape[0]
    ts = _pick(S, ts)
    nsteps = S // ts
    pe_blk = Q_RANK // LANES + KV_RANK // LANES

    def back(x0, x1, g0, g1, c, s, d0, d1r):
        d1 = d1r * c - _rot_half(d1r * s)
        ms = (jnp.sum(x0 * x0, axis=-1, keepdims=True) + jnp.sum(x1 * x1, axis=-1, keepdims=True)) * (1.0 / QK_DIM)
        rstd = lax.rsqrt(ms + EPS)
        h0, h1 = x0 * rstd, x1 * rstd
        e0, e1 = d0 * g0, d1 * g1
        m = (jnp.sum(e0 * h0, axis=-1, keepdims=True) + jnp.sum(e1 * h1, axis=-1, keepdims=True)) * (1.0 / QK_DIM)
        return rstd * (e0 - h0 * m), rstd * (e1 - h1 * m), d0 * h0, d1 * h1

    def fold(a):
        return jnp.sum(a.reshape(ts // 8, 8, a.shape[-1]), axis=0)

    def body(dq_ref, dk_ref, dv_ref, q_ref, kv_ref, pe_ref, cos_ref, sin_ref, qg_ref, kg_ref,
             dqr_ref, dkv_ref, dpe_ref, dqg_ref, dkg_ref, gq_sc, gk_sc):
        i = pl.program_id(0)

        @pl.when(i == 0)
        def _():
            gq_sc[...] = jnp.zeros_like(gq_sc)
            gk_sc[...] = jnp.zeros_like(gk_sc)

        c, s = cos_ref[...], sin_ref[...]
        pe = pe_ref[...]
        qg0, qg1 = qg_ref[:, :NOPE], qg_ref[:, NOPE:]
        kg0, kg1 = kg_ref[:, :NOPE], kg_ref[:, NOPE:]
        dpe = jnp.zeros((ts, LANES), F32)
        for h in range(N_MLA):
            b = h * QK_PAD
            dx0, dx1, a0, a1 = back(q_ref[:, b:b + NOPE], q_ref[:, b + NOPE:b + QK_PAD], qg0, qg1, c, s,
                                    dq_ref[:, b:b + NOPE], dq_ref[:, b + NOPE:b + QK_PAD])
            dqr_ref[:, b:b + NOPE] = dx0.astype(BF16)
            dqr_ref[:, b + NOPE:b + QK_PAD] = dx1.astype(BF16)
            gq_sc[:, :NOPE] += fold(a0)
            gq_sc[:, NOPE:] += fold(a1)
            dx0, dx1, a0, a1 = back(kv_ref[:, b:b + NOPE], pe, kg0, kg1, c, s,
                                    dk_ref[:, b:b + NOPE], dk_ref[:, b + NOPE:b + QK_PAD])
            dkv_ref[:, b:b + NOPE] = dx0.astype(BF16)
            dkv_ref[:, b + NOPE:b + QK_PAD] = dv_ref[:, h * HEAD:(h + 1) * HEAD].astype(BF16)
            dpe = dpe + dx1
            gk_sc[:, :NOPE] += fold(a0)
            gk_sc[:, NOPE:] += fold(a1)
        dpe_ref[...] = dpe

        @pl.when(i == nsteps - 1)
        def _():
            dqg_ref[...] = jnp.sum(gq_sc[...], axis=0, keepdims=True)
            dkg_ref[...] = jnp.sum(gk_sc[...], axis=0, keepdims=True)

    wide = pl.BlockSpec((ts, N_MLA * QK_PAD), lambda i: (i, 0))
    lane_blk = pl.BlockSpec((ts, LANES), lambda i: (i, 0))
    gain = pl.BlockSpec((1, QK_PAD), lambda i: (0, 0))
    outs, comm_outs = _hosted_call(
        body,
        out_shape=(jax.ShapeDtypeStruct((S, N_MLA * QK_PAD), BF16), jax.ShapeDtypeStruct((S, N_MLA * QK_PAD), BF16),
                   jax.ShapeDtypeStruct((S, LANES), F32), jax.ShapeDtypeStruct((1, QK_PAD), F32),
                   jax.ShapeDtypeStruct((1, QK_PAD), F32)),
        grid=(nsteps,),
        in_specs=[wide, wide, pl.BlockSpec((ts, N_MLA * HEAD), lambda i: (i, 0)), wide, wide,
                  pl.BlockSpec((ts, LANES), lambda i: (i, pe_blk)), lane_blk, lane_blk, gain, gain],
        out_specs=(wide, wide, lane_blk, gain, gain),
        scratch_shapes=[pltpu.VMEM((8, QK_PAD), F32), pltpu.VMEM((8, QK_PAD), F32)], name=name,
        sem=("arbitrary",), operands=(dq, dk, dv, q_raw, kv_raw, down, cos, sin, q_g, k_g), comm=comm)
    res = (outs[0], outs[1], outs[2], outs[3][0], outs[4][0])
    return res if comm is None else (res, comm_outs)


def _loss_head(y, target, *, name, tr=256):
    R, C = y.shape
    tr = _pick(R, tr)
    nsteps = R // tr

    def body(y_ref, t_ref, dy_ref, dyb_ref, loss_ref, acc):
        i = pl.program_id(0)

        @pl.when(i == 0)
        def _():
            acc[...] = jnp.zeros_like(acc)

        err = y_ref[...] - t_ref[...]
        dy = err * (1.0 / C)
        dy_ref[...] = dy
        dyb_ref[...] = dy.astype(BF16)
        acc[...] += jnp.sum((err * err).reshape(tr // 8, 8, C), axis=0)

        @pl.when(i == nsteps - 1)
        def _():
            tot = jnp.sum(jnp.sum(acc[...], axis=0, keepdims=True), axis=1, keepdims=True)
            loss_ref[...] = jnp.broadcast_to(tot * (0.5 / C), (8, LANES))

    blk = pl.BlockSpec((tr, C), lambda i: (i, 0))
    dy, dy_b, loss = pl.pallas_call(
        body, out_shape=(jax.ShapeDtypeStruct((R, C), F32), jax.ShapeDtypeStruct((R, C), BF16),
                         jax.ShapeDtypeStruct((8, LANES), F32)),
        grid=(nsteps,), in_specs=[blk, blk], out_specs=(blk, blk, pl.BlockSpec((8, LANES), lambda i: (0, 0))),
        scratch_shapes=[pltpu.VMEM((8, C), F32)], name=name, compiler_params=_cp("arbitrary"))(y, target)
    return dy, dy_b, loss[0, 0]


def _adamw(w, g, m, v, *, name, block_bytes=1 << 20, comm=None):
    L, R, C = w.shape
    tr = max(8, min(R, (block_bytes // (4 * C)) // 8 * 8))
    while R % tr:
        tr -= 8
    if tr <= 0:
        tr = R
    c1 = 1.0 / (1.0 - ADAM_B1 ** ADAM_STEP)
    c2 = 1.0 / (1.0 - ADAM_B2 ** ADAM_STEP)

    def body(w_ref, g_ref, m_ref, v_ref, d_ref, mo_ref, vo_ref):
        gv = g_ref[...]
        mn = ADAM_B1 * m_ref[...] + (1.0 - ADAM_B1) * gv
        vn = ADAM_B2 * v_ref[...] + (1.0 - ADAM_B2) * (gv * gv)
        d_ref[...] = -ADAM_LR * ((mn * c1) / (jnp.sqrt(vn * c2) + ADAM_EPS) + ADAM_WD * w_ref[...])
        mo_ref[...] = mn
        vo_ref[...] = vn

    blk = pl.BlockSpec((None, tr, C), lambda l, i: (l, i, 0))
    sd = jax.ShapeDtypeStruct((L, R, C), F32)
    outs, comm_outs = _hosted_call(
        body, out_shape=(sd, sd, sd), grid=(L, R // tr), in_specs=[blk] * 4, out_specs=(blk,) * 3, scratch_shapes=[],
        name=name, sem=("parallel", "parallel"), operands=(w, g, m, v), comm=comm)
    return outs if comm is None else (outs, comm_outs)


def _row_tile(r, c, itemsize=4, block_bytes=1 << 20):
    tr = max(16, min(r, (block_bytes // (itemsize * c)) // 16 * 16))
    while r % tr:
        tr -= 16
    return tr if tr > 0 else r


def _add_sibling(g, recv, core, *, name):
    nch, _, r, c = g.shape
    tr = _row_tile(r, c)

    def body(core_ref, g_ref, r_ref, o_ref):
        o_ref[...] = (g_ref[...] + r_ref[...]).astype(BF16)

    grid_spec = pltpu.PrefetchScalarGridSpec(
        num_scalar_prefetch=1, grid=(nch, r // tr),
        in_specs=[pl.BlockSpec((None, None, tr, c), lambda j, i, cr: (j, cr[0], i, 0)),
                  pl.BlockSpec((None, tr, c), lambda j, i, cr: (j, i, 0))],
        out_specs=pl.BlockSpec((None, tr, c), lambda j, i, cr: (j, i, 0)))
    return pl.pallas_call(
        body, out_shape=jax.ShapeDtypeStruct((nch, r, c), BF16), grid_spec=grid_spec, name=name,
        compiler_params=_cp("parallel", "parallel"))(core, g, recv)


def _add_chips(slots, *, name):
    nch, r, c = slots.shape
    tr = _row_tile(r, c)

    def body(s_ref, o_ref):
        acc = s_ref[0].astype(F32)
        for j in range(1, nch):
            acc = acc + s_ref[j].astype(F32)
        o_ref[...] = acc

    return pl.pallas_call(
        body, out_shape=jax.ShapeDtypeStruct((r, c), F32), grid=(r // tr,),
        in_specs=[pl.BlockSpec((nch, tr, c), lambda i: (0, i, 0))],
        out_specs=pl.BlockSpec((tr, c), lambda i: (i, 0)), name=name, compiler_params=_cp("parallel"))(slots)


def _place():
    x, y, c = lax.axis_index("x"), lax.axis_index("y"), lax.axis_index("c")
    others = [(1 - x, y), (x, 1 - y), (1 - x, 1 - y)]
    return x, y, c, 2 * x + y, others


ANY = pl.BlockSpec(memory_space=pl.ANY)


class _Exchange:
    def __init__(self, kind, arrays):
        self.kind, self.ins = kind, list(arrays)
        self.n_peers = 1 if kind in ("swap", "join") else 3
        self.aliased = kind == "forward"
        n = len(self.ins) * self.n_peers
        shp = {"gather": lambda a: (N_CHIPS,) + a.shape, "scatter": lambda a: a.shape, "forward": lambda a: a.shape,
               "swap": lambda a: (a.shape[0],) + a.shape[2:], "join": lambda a: (2,) + a.shape}[kind]
        self.out_shapes = [jax.ShapeDtypeStruct(shp(a), a.dtype) for a in self.ins]
        self.sems = [pltpu.SemaphoreType.DMA((n,)), pltpu.SemaphoreType.DMA((n,))]

    def _copies(self, ins, outs, sems):
        send, recv = sems
        x, y, c, me, others = _place()
        over_ici = self.kind in ("gather", "scatter")
        peers = [(ox, oy, c) for ox, oy in others] if over_ici else [(x, y, 1 - c)] * self.n_peers
        for a in range(len(self.ins)):
            for k, to in enumerate(peers):
                peer = 2 * others[k][0] + others[k][1]
                if self.kind == "gather":
                    hr = self.ins[a].shape[0] // 2
                    rows = pl.ds(c * hr, hr)
                    src, dst, land = ins[a].at[rows, :], outs[a].at[me, rows, :], outs[a].at[peer, rows, :]
                elif self.kind == "forward":
                    hr = self.ins[a].shape[1] // 2
                    mine, theirs = pl.ds(c * hr, hr), pl.ds((1 - c) * hr, hr)
                    src, dst, land = ins[a].at[peer, mine, :], outs[a].at[peer, mine, :], outs[a].at[peer, theirs, :]
                elif self.kind == "scatter":
                    src, dst, land = ins[a].at[peer], outs[a].at[me], outs[a].at[peer]
                elif self.kind == "swap":
                    src, dst, land = ins[a].at[:, 1 - c], outs[a], outs[a]
                else:
                    src, dst, land = ins[a], outs[a].at[c], outs[a].at[1 - c]
                i = self.n_peers * a + k
                mk = lambda s, d: pltpu.make_async_remote_copy(
                    src_ref=s, dst_ref=d, send_sem=send.at[i], recv_sem=recv.at[i], device_id=to,
                    device_id_type=MESH)
                yield mk(src, dst), mk(land, land)

    def alias_pairs(self):
        return [(i, i) for i in range(len(self.ins))] if self.aliased else []

    def start(self, ins, outs, sems):
        for cp, _ in self._copies(ins, outs, sems):
            cp.start()

    def finish(self, ins, outs, sems):
        pairs = list(self._copies(ins, outs, sems))
        for _, landing in pairs:
            landing.wait_recv()
        for cp, _ in pairs:
            cp.wait_send()


class _Several:
    def __init__(self, parts):
        self.parts = list(parts)
        self.ins = [a for p in self.parts for a in p.ins]
        self.out_shapes = [s for p in self.parts for s in p.out_shapes]
        self.sems = [s for p in self.parts for s in p.sems]

    def split(self, ins, outs, sems=None):
        i = 0
        for k, p in enumerate(self.parts):
            n = len(p.ins)
            yield p, ins[i:i + n], outs[i:i + n], None if sems is None else sems[2 * k:2 * k + 2]
            i += n

    def alias_pairs(self):
        pairs, i = [], 0
        for p in self.parts:
            pairs += [(i + a, i + b) for a, b in p.alias_pairs()]
            i += len(p.ins)
        return pairs

    def start(self, ins, outs, sems):
        for p, a, b, s in self.split(ins, outs, sems):
            p.start(a, b, s)

    def finish(self, ins, outs, sems):
        for p, a, b, s in self.split(ins, outs, sems):
            p.finish(a, b, s)


def _hosted_call(body, *, grid, in_specs, out_specs, out_shape, scratch_shapes, operands, name, sem, comm=None):
    out_specs, out_shape = tuple(out_specs), tuple(out_shape)
    if isinstance(comm, (list, tuple)):
        several = _Several(comm)
        outs, comm_outs = _hosted_call(body, grid=grid, in_specs=in_specs, out_specs=out_specs, out_shape=out_shape,
                                       scratch_shapes=scratch_shapes, operands=operands, name=name, sem=sem,
                                       comm=several)
        return outs, [tuple(o) for _, _, o, _ in several.split(several.ins, comm_outs)]
    if comm is None:
        res = pl.pallas_call(body, out_shape=out_shape, grid=grid, in_specs=list(in_specs), out_specs=out_specs,
                             scratch_shapes=list(scratch_shapes), name=name, compiler_params=_cp(*sem))(*operands)
        return tuple(res), ()
    n_in, n_out, n_sc = len(in_specs), len(out_specs), len(scratch_shapes)
    ci, co = len(comm.ins), len(comm.out_shapes)

    def wrapped(*refs):
        ins, c_ins = refs[:n_in], refs[n_in:n_in + ci]
        outs = refs[n_in + ci:n_in + ci + n_out]
        c_outs = refs[n_in + ci + n_out:n_in + ci + n_out + co]
        scratch = refs[n_in + ci + n_out + co:n_in + ci + n_out + co + n_sc]
        sems = refs[n_in + ci + n_out + co + n_sc:]
        ids = [pl.program_id(d) for d in range(len(grid))]
        first = functools.reduce(jnp.logical_and, [i == 0 for i in ids])
        last = functools.reduce(jnp.logical_and, [i == g - 1 for i, g in zip(ids, grid)])

        @pl.when(first)
        def _():
            comm.start(c_ins, c_outs, sems)

        body(*ins, *outs, *scratch)

        @pl.when(last)
        def _():
            comm.finish(c_ins, c_outs, sems)

    res = pl.pallas_call(
        wrapped, out_shape=out_shape + tuple(comm.out_shapes), grid=grid, in_specs=list(in_specs) + [ANY] * ci,
        out_specs=out_specs + tuple([ANY] * co), scratch_shapes=list(scratch_shapes) + comm.sems, name=name,
        input_output_aliases={n_in + i: n_out + o for i, o in comm.alias_pairs()},
        compiler_params=pltpu.CompilerParams(dimension_semantics=("arbitrary",) * len(grid),
                                             vmem_limit_bytes=VMEM_LIMIT, has_side_effects=True),
    )(*operands, *comm.ins)
    return tuple(res[:n_out]), tuple(res[n_out:])


def _run_exchange(comm, *, name):
    ci = len(comm.ins)

    def body(*refs):
        ins, outs, sems = refs[:ci], refs[ci:2 * ci], refs[2 * ci:]
        comm.start(ins, outs, sems)
        comm.finish(ins, outs, sems)

    return pl.pallas_call(
        body, out_shape=tuple(comm.out_shapes), in_specs=[ANY] * ci, out_specs=tuple([ANY] * ci),
        scratch_shapes=comm.sems, name=name, input_output_aliases=dict(comm.alias_pairs()),
        compiler_params=pltpu.CompilerParams(has_side_effects=True))(*comm.ins)


def _own_slot(buf, piece, idx):
    return lax.dynamic_update_slice(buf, piece[None], (idx,) + (0,) * piece.ndim)


def _all_reduce_small(v, *, name):
    R = v.shape[0]

    flips = [(dx, dy, dc) for dx in range(2) for dy in range(2) for dc in range(2) if dx or dy or dc]

    def body(v_ref, o_ref, slots, send, recv):
        x, y, c, me, others = _place()
        mine = 2 * me + c
        slots[mine] = v_ref[...]

        def copy(k, slot):
            dx, dy, dc = flips[k]
            peer = (x + dx - 2 * x * dx, y + dy - 2 * y * dy, c + dc - 2 * c * dc)
            peer_slot = 4 * peer[0] + 2 * peer[1] + peer[2]
            return pltpu.make_async_remote_copy(
                src_ref=v_ref, dst_ref=slots.at[mine if slot == "mine" else peer_slot], send_sem=send.at[k],
                recv_sem=recv.at[k], device_id=peer, device_id_type=MESH)

        for k in range(7):
            copy(k, "mine").start()
        for k in range(7):
            copy(k, "peer").wait_recv()
        for k in range(7):
            copy(k, "mine").wait_send()
        acc = slots[0]
        for j in range(1, 8):
            acc = acc + slots[j]
        o_ref[...] = acc

    vm = pl.BlockSpec(memory_space=pltpu.VMEM)
    return pl.pallas_call(
        body, out_shape=jax.ShapeDtypeStruct(v.shape, F32), in_specs=[vm], out_specs=vm,
        scratch_shapes=[pltpu.VMEM((8, R, LANES), F32), pltpu.SemaphoreType.DMA((7,)),
                        pltpu.SemaphoreType.DMA((7,))],
        name=name, compiler_params=pltpu.CompilerParams(has_side_effects=True))(v)


def _rows(v, n_rows):
    v = v.reshape(-1).astype(F32)
    return jnp.pad(v, (0, n_rows * LANES - v.shape[0])).reshape(n_rows, LANES)


def _mlp_fwd(x_in, g, w_up, w_down, tag):
    h = _rms_fwd(x_in, g, name=f"{tag}_norm")
    u, a = _matmul(h, w_up, b_split=True, epilogue="sqrelu", name=f"{tag}_up")
    x_out = _matmul(a, w_down, epilogue="res", res=x_in, name=f"{tag}_down")
    return x_out, (h, u, a)


def _mlp_bwd(dy, dy_b, x_in, g, w_up, w_down, saved, tag, comms=None):
    h, u, a = saved
    comms = comms or {}
    landed = {}

    def mm(key, *args, **kw):
        comm = comms.get(key)
        if callable(comm):
            comm = comm(landed)
        out = _matmul(*args, name=f"{tag}_{key}", comm=comm, **kw)
        if comm is not None:
            out, landed[key] = out
        return out

    dw_down = mm("dwdown", a, dy_b, form="tn")
    du = mm("du", dy_b, w_down, form="nt", epilogue="sqrelu_bwd", u=u, out_dtype=BF16)
    dw_up = mm("dwup", h, du, form="tn", out_split=True)
    dh = mm("dh", du, w_up, form="nt", b_split=True)
    dx, dx_b, dg = _rms_bwd(x_in, g, dh, res=dy, bf16_copy=True, name=f"{tag}_dnorm")
    return dx, dx_b, dg, dw_up, dw_down, landed


def kernel(x, positions, ln_mix_g, ln_mlp_g, sf_w_in, sf_b_f, fox_q_g, fox_k_g, sf_w_o, mla_w_down, mla_q_a_g, mla_kv_a_g, mla_w_uq, mla_w_ukv, mla_q_g, mla_k_g, mla_w_o, mlp_w_up, mlp_w_down, loss_target, m_ln_mix_g, m_ln_mlp_g, m_sf_w_in, m_sf_b_f, m_fox_q_g, m_fox_k_g, m_sf_w_o, m_mla_w_down, m_mla_q_a_g, m_mla_kv_a_g, m_mla_w_uq, m_mla_w_ukv, m_mla_q_g, m_mla_k_g, m_mla_w_o, m_mlp_w_up, m_mlp_w_down, v_ln_mix_g, v_ln_mlp_g, v_sf_w_in, v_sf_b_f, v_fox_q_g, v_fox_k_g, v_sf_w_o, v_mla_w_down, v_mla_q_a_g, v_mla_kv_a_g, v_mla_w_uq, v_mla_w_ukv, v_mla_q_g, v_mla_k_g, v_mla_w_o, v_mlp_w_up, v_mlp_w_down):
    S, D = x.shape[1], x.shape[2]
    xs, tgt, pos = x[0], loss_target[0], positions[0]
    xi, yi, ci = lax.axis_index("x"), lax.axis_index("y"), lax.axis_index("c")
    chip = 2 * xi + yi
    core = ci.astype(jnp.int32).reshape(1)
    d_ff = mlp_w_up.shape[2] * N_CHIPS
    in_w = sf_w_in.shape[2] * N_CHIPS
    qkv_w = 3 * N_SB * HEAD + 3 * N_FOX * HEAD
    dn_w = mla_w_down.shape[2]
    dn_pad = Q_RANK + KV_RANK + LANES

    def gather_begin(ws):
        shards = [w.astype(BF16) for w in ws]
        return shards, _Exchange("gather", shards)

    def hand_over(landed):
        return _Exchange("forward", list(landed))

    def gather_end(both, shards):
        return [_own_slot(ag, s, chip) for ag, s in zip(both, shards)]

    cols = lambda ag: ag.transpose(1, 0, 2).reshape(ag.shape[1], -1)
    rows = lambda ag: ag.reshape(-1, ag.shape[2])
    s_in, ex_in = gather_begin([sf_w_in[0]])
    landed = _run_exchange(ex_in, name="gather_mix0")
    ag_in, = gather_end(_run_exchange(hand_over(landed), name="gather_mix0_sibling"), s_in)
    w_in_full = cols(ag_in)
    w_qkv = w_in_full[:, :qkv_w]
    w_f = jnp.pad(w_in_full[:, qkv_w:], ((0, 0), (0, LANES - (in_w - qkv_w))))
    s_o0, ex_o0 = gather_begin([sf_w_o[0]])
    s_up0, ex_up0 = gather_begin([mlp_w_up[0]])
    s_dw0, ex_dw0 = gather_begin([mlp_w_down[0]])
    s_mix1, ex_mix1 = gather_begin([mla_w_down[0], mla_w_uq[0], mla_w_ukv[0], mla_w_o[0]])
    s_mlp1, ex_mlp1 = gather_begin([mlp_w_up[1], mlp_w_down[1]])

    gain_blk = jnp.concatenate([mla_q_a_g, mla_kv_a_g], axis=0) * (ci == 0).astype(F32)
    placed = jnp.zeros((2, N_CHIPS, LANES), F32)
    placed = lax.dynamic_update_slice(placed, gain_blk[:, None, :], (0, chip, 0))
    gains = _all_reduce_small(placed.reshape(2 * N_CHIPS, LANES), name="gather_gains")
    q_a_full = gains[:N_CHIPS].reshape(Q_RANK)
    kv_a_full = gains[N_CHIPS:].reshape(KV_RANK)

    pad_gain = lambda g: jnp.pad(g.reshape(1, QK_DIM), ((0, 0), (0, QK_PAD - QK_DIM)))
    q_g_pad, k_g_pad = pad_gain(mla_q_g), pad_gain(mla_k_g)
    b_pad = _rows(sf_b_f, 1)

    h0 = _rms_fwd(xs, ln_mix_g[0], name="mix0_norm")
    qkv_sb, landed = _matmul(h0, w_qkv, n=3 * N_SB * HEAD, b_n0=0, out_dtype=BF16, name="mix0_qkv_sb", comm=ex_o0)
    qk_fx, both = _matmul(h0, w_qkv, n=2 * N_FOX * HEAD, b_n0=3 * N_SB * HEAD, name="mix0_qk_fox",
                          comm=hand_over(landed))
    ag_o0, = gather_end(both, s_o0)
    w_o0 = rows(ag_o0)
    v_fx = _matmul(h0, w_qkv, n=N_FOX * HEAD, b_n0=(3 * N_SB + 2 * N_FOX) * HEAD, out_dtype=BF16,
                   name="mix0_v_fox")
    fl = _matmul(h0, w_f, name="mix0_forget_logit")
    f_cum = _forget_fwd(fl, b_pad, name="forget_fwd")
    neg_f = (-f_cum[:, :N_FOX]).T.reshape(N_FOX, 1, S)
    q_f = _rms_fwd(qk_fx, fox_q_g[0], c0=0, width=N_FOX * HEAD, gw=HEAD, name="fox_q_norm")
    k_f = _rms_fwd(qk_fx, fox_k_g[0], c0=N_FOX * HEAD, width=N_FOX * HEAD, gw=HEAD, name="fox_k_norm")
    (o_sb, t_sb), (l_up0, l_mix1) = _sb_fwd(qkv_sb, name="sb_fwd", comm=[ex_up0, ex_mix1], tq=1024)
    (o_fx, lse0), (l_dw0, b_up0, b_mix1) = _attn_fwd(
        q_f, k_f, v_fx, neg_f, n_heads=N_FOX, dqk=HEAD, scale=HEAD ** -0.5, exact_p=True, name="fox_fwd",
        comm=[ex_dw0, hand_over(l_up0), hand_over(l_mix1)])
    ag_up0, = gather_end(b_up0, s_up0)
    ag_dn, ag_uq, ag_ukv, ag_o1 = gather_end(b_mix1, s_mix1)
    o0 = jnp.concatenate([o_sb, o_fx], axis=1).astype(BF16)
    x1, b_dw0 = _matmul(o0, w_o0, epilogue="res", res=xs, name="mix0_out", comm=hand_over(l_dw0))
    ag_dw0, = gather_end(b_dw0, s_dw0)
    w_dn = jnp.pad(rows(ag_dn), ((0, 0), (0, dn_pad - dn_w)))
    w_uq = jnp.pad(cols(ag_uq).reshape(Q_RANK, N_MLA, QK_DIM), ((0, 0), (0, 0), (0, QK_PAD - QK_DIM)))
    w_uq = w_uq.reshape(Q_RANK, N_MLA * QK_PAD)
    w_ukv = cols(ag_ukv)
    w_o1 = rows(ag_o1)
    x2, mlp0 = _mlp_fwd(x1, ln_mlp_g[0], ag_up0, rows(ag_dw0), "mlp0")

    h2 = _rms_fwd(x2, ln_mix_g[1], name="mix1_norm")
    down = _matmul(h2, w_dn, name="mix1_down")
    c_q = _rms_fwd(down, q_a_full, c0=0, width=Q_RANK, name="mix1_q_a_norm")
    c_kv = _rms_fwd(down, kv_a_full, c0=Q_RANK, width=KV_RANK, name="mix1_kv_a_norm")
    q_raw = _matmul(c_q, w_uq, name="mix1_uq")
    kv_raw = _matmul(c_kv, w_ukv, name="mix1_ukv")
    half = ROPE // 2
    inv_freq = ROPE_THETA ** (-jnp.arange(half, dtype=F32) / half)
    ang = pos.astype(F32)[:, None] * inv_freq
    table = lambda t: jnp.pad(jnp.concatenate([t, t], axis=1), ((0, 0), (0, LANES - ROPE)))
    cos_t, sin_t = table(jnp.cos(ang)), table(jnp.sin(ang))
    q_pad, k_pad, v1 = _mla_prep_fwd(q_raw, kv_raw, down, cos_t, sin_t, q_g_pad, k_g_pad, name="mla_prep_fwd")
    (o1, lse1), landed = _attn_fwd(q_pad, k_pad, v1, None, n_heads=N_MLA, dqk=QK_PAD, scale=QK_DIM ** -0.5,
                                   name="mla_fwd", comm=ex_mlp1)
    o1_b = o1.astype(BF16)
    x3, both = _matmul(o1_b, w_o1, epilogue="res", res=x2, name="mix1_out", comm=hand_over(landed))
    ag_up1, ag_dw1 = gather_end(both, s_mlp1)
    w_up = [ag_up0, ag_up1]
    w_dw = [rows(ag_dw0), rows(ag_dw1)]
    x4, mlp1 = _mlp_fwd(x3, ln_mlp_g[1], w_up[1], w_dw[1], "mlp1")

    dx4, dx4_b, loss_local = _loss_head(x4, tgt, name="loss_head")
    loss = lax.psum(loss_local, ("x", "y", "c"))

    by_cols = lambda g: g.reshape(g.shape[0], N_CHIPS, -1).transpose(1, 0, 2)
    by_rows = lambda g: g.reshape(N_CHIPS, g.shape[0] // N_CHIPS, g.shape[1])
    halves = lambda g: g.reshape(N_CHIPS, 2, g.shape[1] // 2, g.shape[2])

    def scatter_of(grads, from_sibling, tags):
        parts = [_add_sibling(g, r, core, name=f"add_sibling_{t}") for g, r, t in zip(grads, from_sibling, tags)]
        return parts, _Exchange("scatter", parts)

    def sums_of(slots, parts, tags):
        slots = [_own_slot(s, lax.dynamic_index_in_dim(p, chip, 0, keepdims=False), chip)
                 for s, p in zip(slots, parts)]
        return [_add_chips(s, name=f"add_chips_{t}") for s, t in zip(slots, tags)]

    def shards_of(joined, mine):
        return [_own_slot(j, m, ci).reshape(2 * m.shape[0], m.shape[1]) for j, m in zip(joined, mine)]

    dx3, dx3_b, dg_mlp1, dw_up1, dw_dw1, _ = _mlp_bwd(dx4, dx4_b, x3, ln_mlp_g[1], w_up[1], w_dw[1], mlp1, "mlp1")
    tags_mlp1 = ["w_up1", "w_dw1"]
    g_mlp1 = [halves(dw_up1), halves(by_rows(dw_dw1))]

    dw_o1 = _matmul(o1_b, dx3_b, form="tn", name="mix1_dwo")
    do1 = _matmul(dx3_b, w_o1, form="nt", name="mix1_do")
    (dq_pad, dk_pad, dv1), from_sibling = _attn_bwd(
        q_pad, k_pad, v1, None, o1, lse1, do1, n_heads=N_MLA, dqk=QK_PAD, scale=QK_DIM ** -0.5, name="mla_bwd",
        comm=_Exchange("swap", g_mlp1))
    p_mlp1, _ = scatter_of(g_mlp1, from_sibling, tags_mlp1)
    (dq_raw, dkv_raw, dpe, dg_q, dg_k), sl_up1 = _mla_prep_bwd(
        dq_pad, dk_pad, dv1, q_raw, kv_raw, down, cos_t, sin_t, q_g_pad, k_g_pad, name="mla_prep_bwd",
        comm=_Exchange("scatter", p_mlp1[:1]))
    dw_uq = _matmul(c_q, dq_raw, form="tn", name="mix1_dwuq")
    dc_q = _matmul(dq_raw, w_uq, form="nt", name="mix1_dcq")
    dw_ukv = _matmul(c_kv, dkv_raw, form="tn", name="mix1_dwukv")
    dc_kv = _matmul(dkv_raw, w_ukv, form="nt", name="mix1_dckv")
    d_cq, dg_qa = _rms_bwd(down, q_a_full, dc_q, c0=0, width=Q_RANK, name="mix1_q_a_dnorm")
    d_ckv, dg_kva = _rms_bwd(down, kv_a_full, dc_kv, c0=Q_RANK, width=KV_RANK, name="mix1_kv_a_dnorm")
    d_down = jnp.concatenate([d_cq, d_ckv, dpe], axis=1)
    dw_dn = _matmul(h2, d_down, form="tn", name="mix1_dwdown")
    dh2 = _matmul(d_down, w_dn, form="nt", name="mix1_dh")
    dx2, dx2_b, dg_mix1 = _rms_bwd(x2, ln_mix_g[1], dh2, res=dx3, bf16_copy=True, name="mix1_dnorm")
    g_uq = dw_uq.reshape(Q_RANK, N_MLA, QK_PAD)[:, :, :QK_DIM].reshape(Q_RANK, N_MLA * QK_DIM)
    tags_mix1 = ["w_dn", "w_uq", "w_ukv", "w_o1"]
    g_mix1 = [halves(by_rows(dw_dn[:, :dn_w])), halves(by_cols(g_uq)), halves(by_cols(dw_ukv)),
              halves(by_rows(dw_o1))]

    p_mix1, mine_mlp1 = [], []

    def scatter_mix1(landed):
        parts, ex = scatter_of(g_mix1, landed["dwdown"][1], tags_mix1)
        p_mix1.extend(parts)
        return ex

    def join_mlp1(landed):
        mine_mlp1.extend(sums_of(sl_up1 + landed["dwdown"][0], p_mlp1, tags_mlp1))
        return _Exchange("join", mine_mlp1)

    dx1, dx1_b, dg_mlp0, dw_up0, dw_dw0, landed = _mlp_bwd(
        dx2, dx2_b, x1, ln_mlp_g[0], w_up[0], w_dw[0], mlp0, "mlp0",
        comms={"dwdown": [_Exchange("scatter", p_mlp1[1:]), _Exchange("swap", g_mix1)],
               "du": scatter_mix1, "dwup": join_mlp1})
    gs_up1, gs_dw1 = shards_of(landed["dwup"], mine_mlp1)
    mine_mix1 = sums_of(landed["du"], p_mix1, tags_mix1)
    tags_mlp0 = ["w_up0", "w_dw0"]
    g_mlp0 = [halves(dw_up0), halves(by_rows(dw_dw0))]

    dw_o0, joined = _matmul(o0, dx1_b, form="tn", name="mix0_dwo", comm=_Exchange("join", mine_mix1))
    gs_dn, gs_uq, gs_ukv, gs_o1 = shards_of(joined, mine_mix1)
    g_o0 = [halves(by_rows(dw_o0))]
    do0 = _matmul(dx1_b, w_o0, form="nt", name="mix0_do")
    (dq_f, dk_f, dv_fx, dbias, drow), (fs_mlp0, fs_o0) = _attn_bwd(
        q_f, k_f, v_fx, neg_f, o_fx, lse0, do0, n_heads=N_FOX, dqk=HEAD, scale=HEAD ** -0.5, do_off=N_SB,
        name="fox_bwd", comm=[_Exchange("swap", g_mlp0), _Exchange("swap", g_o0)])
    p_mlp0, ex_a = scatter_of(g_mlp0, fs_mlp0, tags_mlp0)
    p_o0, ex_b = scatter_of(g_o0, fs_o0, ["w_o0"])
    (dq_sb, dk_sb, dv_sb), (sl_mlp0, sl_o0) = _sb_bwd(qkv_sb, do0, t_sb, do_off=0, name="sb_bwd", comm=[ex_a, ex_b])
    mine_mlp0 = sums_of(sl_mlp0, p_mlp0, tags_mlp0)
    mine_o0 = sums_of(sl_o0, p_o0, ["w_o0"])
    dq_fx, dg_fq = _rms_bwd(qk_fx, fox_q_g[0], dq_f, c0=0, width=N_FOX * HEAD, gw=HEAD, name="fox_q_dnorm")
    dk_fx, dg_fk = _rms_bwd(qk_fx, fox_k_g[0], dk_f, c0=N_FOX * HEAD, width=N_FOX * HEAD, gw=HEAD,
                            name="fox_k_dnorm")
    d_fcum = jnp.pad((jnp.max(drow, axis=-1) - dbias.reshape(N_FOX, S)).T, ((0, 0), (0, LANES - N_FOX)))
    dfl, db_f = _forget_bwd(fl, b_pad, d_fcum, name="forget_bwd")
    dproj = jnp.concatenate([dq_sb, dk_sb, dv_sb, dq_fx, dk_fx, dv_fx], axis=1).astype(BF16)
    dw_qkv, (j_mlp0, j_o0) = _matmul(h0, dproj, form="tn", name="mix0_dwqkv",
                                     comm=[_Exchange("join", mine_mlp0), _Exchange("join", mine_o0)])
    gs_up0, gs_dw0 = shards_of(j_mlp0, mine_mlp0)
    gs_o0, = shards_of(j_o0, mine_o0)
    dw_f = _matmul(h0, dfl, form="tn", name="mix0_dwf")
    g_in = [halves(by_cols(jnp.concatenate([dw_qkv, dw_f[:, :in_w - qkv_w]], axis=1)))]
    p_in, ex = scatter_of(g_in, _run_exchange(_Exchange("swap", g_in), name="reduce_w_in_swap"), ["w_in"])
    dh0 = _matmul(dfl, w_f, form="nt", name="mix0_dh_f")
    dh0, slots = _matmul(dproj, w_qkv, form="nt", epilogue="res", res=dh0, name="mix0_dh", comm=ex)
    mine_in = sums_of(slots, p_in, ["w_in"])
    gs_in, = shards_of(_run_exchange(_Exchange("join", mine_in), name="reduce_w_in_join"), mine_in)
    grad_x, dg_mix0 = _rms_bwd(xs, ln_mix_g[0], dh0, res=dx1, name="mix0_dnorm")
    gs_up = jnp.concatenate([gs_up0, gs_up1], axis=0)
    gs_dw = jnp.concatenate([gs_dw0, gs_dw1], axis=0)

    ln_rows = D // LANES
    small = jnp.concatenate([
        _rows(dg_mix0, ln_rows), _rows(dg_mix1, ln_rows), _rows(dg_mlp0, ln_rows), _rows(dg_mlp1, ln_rows),
        _rows(db_f, 8), _rows(dg_fq, 8), _rows(dg_fk, 8), _rows(dg_qa, 8), _rows(dg_kva, 8), _rows(dg_q, 8),
        _rows(dg_k, 8)], axis=0)
    small = _all_reduce_small(small, name="reduce_small")
    flat = lambda r0, nr, n: small[r0:r0 + nr].reshape(-1)[:n]
    r0 = 4 * ln_rows
    g_ln_mix = jnp.stack([flat(0, ln_rows, D), flat(ln_rows, ln_rows, D)])
    g_ln_mlp = jnp.stack([flat(2 * ln_rows, ln_rows, D), flat(3 * ln_rows, ln_rows, D)])
    g_b_f = flat(r0, 8, N_FOX)[None]
    g_fq, g_fk = flat(r0 + 8, 8, HEAD)[None], flat(r0 + 16, 8, HEAD)[None]
    g_qa = lax.dynamic_slice(flat(r0 + 24, 8, Q_RANK), (chip * LANES,), (LANES,))[None]
    g_kva = lax.dynamic_slice(flat(r0 + 32, 8, KV_RANK), (chip * LANES,), (LANES,))[None]
    g_q, g_k = flat(r0 + 40, 8, QK_DIM)[None], flat(r0 + 48, 8, QK_DIM)[None]

    def pack_small(ln_mix, ln_mlp, *rest):
        return jnp.concatenate([_rows(ln_mix, 2 * ln_rows), _rows(ln_mlp, 2 * ln_rows)] + [_rows(t, 8) for t in rest],
                               axis=0)

    def unpack_small(p):
        f = lambda r, nr, shape: p[r:r + nr].reshape(-1)[:int(np.prod(shape))].reshape(shape)
        shapes = [(1, N_FOX), (1, HEAD), (1, HEAD), (1, LANES), (1, LANES), (1, QK_DIM), (1, QK_DIM)]
        return (f(0, 2 * ln_rows, (2, D)), f(2 * ln_rows, 2 * ln_rows, (2, D)),
                *[f(r0 + 8 * i, 8, shp) for i, shp in enumerate(shapes)])

    small_out = _adamw(
        pack_small(ln_mix_g, ln_mlp_g, sf_b_f, fox_q_g, fox_k_g, mla_q_a_g, mla_kv_a_g, mla_q_g, mla_k_g)[None],
        pack_small(g_ln_mix, g_ln_mlp, g_b_f, g_fq, g_fk, g_qa, g_kva, g_q, g_k)[None],
        pack_small(m_ln_mix_g, m_ln_mlp_g, m_sf_b_f, m_fox_q_g, m_fox_k_g, m_mla_q_a_g, m_mla_kv_a_g, m_mla_q_g,
                   m_mla_k_g)[None],
        pack_small(v_ln_mix_g, v_ln_mlp_g, v_sf_b_f, v_fox_q_g, v_fox_k_g, v_mla_q_a_g, v_mla_kv_a_g, v_mla_q_g,
                   v_mla_k_g)[None], name="adamw_small")
    d_small, m_small, v_small = [unpack_small(p[0]) for p in small_out]

    def big(w, g, m, v, tag, comm=None):
        g = g.reshape(w.shape)
        out = _adamw(w, g, m, v, name=f"adamw_{tag}", comm=comm)
        (d, mn, vn), landed = out if comm is not None else (out, None)
        return (g, d, mn, vn) if comm is None else ((g, d, mn, vn), landed)

    r_up = big(mlp_w_up, gs_up, m_mlp_w_up, v_mlp_w_up, "w_up")
    r_dw = big(mlp_w_down, gs_dw, m_mlp_w_down, v_mlp_w_down, "w_dw")
    r_in = big(sf_w_in, gs_in, m_sf_w_in, v_sf_w_in, "w_in")
    r_o0 = big(sf_w_o, gs_o0, m_sf_w_o, v_sf_w_o, "w_o0")
    r_dn = big(mla_w_down, gs_dn, m_mla_w_down, v_mla_w_down, "w_dn")
    r_uq = big(mla_w_uq, gs_uq, m_mla_w_uq, v_mla_w_uq, "w_uq")
    r_ukv = big(mla_w_ukv, gs_ukv, m_mla_w_ukv, v_mla_w_ukv, "w_ukv")
    r_o1 = big(mla_w_o, gs_o1, m_mla_w_o, v_mla_w_o, "w_o1")

    g_small = (g_ln_mix, g_ln_mlp, g_b_f, g_fq, g_fk, g_qa, g_kva, g_q, g_k)

    def ordered(k, sm):
        return (sm[0], sm[1], r_in[k], sm[2], sm[3], sm[4], r_o0[k], r_dn[k], sm[5], sm[6], r_uq[k], r_ukv[k],
                sm[7], sm[8], r_o1[k], r_up[k], r_dw[k])

    return (loss, grad_x[None], *ordered(0, g_small), *ordered(1, d_small), *ordered(2, m_small),
            *ordered(3, v_small))
```
